```python
import jax
import jax.numpy as jnp
from jax import lax
import numpy as np

D_MODEL = 1024
BATCH = 4
SEQ = 4096
DEPTH = 1
DEC_BATCH = 128
DEC_SEQ = 8
PAST_LEN = 8192
PAGE_SIZE = 128

HEAD_DIM = 64
N_HEADS = D_MODEL // HEAD_DIM
SWA_HEADS = N_HEADS // 2
GDN_HEADS = N_HEADS - SWA_HEADS
SWA_KV_HEADS = 2
GQA_GROUP = SWA_HEADS // SWA_KV_HEADS
WINDOW = 128
ATTN_SCALE = HEAD_DIM ** -0.5
GDN_DK = HEAD_DIM
GDN_DV = HEAD_DIM
CONV_W = 4
GDN_CHUNK = 64
N_GROUPS = 8
EXPERTS_PER_GROUP = 8
N_EXPERTS = N_GROUPS * EXPERTS_PER_GROUP
TOP_K_IN_GROUP = 2
D_EXPERT = D_MODEL // 4
MOE_BLOCK = 128
EPS = 1e-6

SWA_Q = SWA_HEADS * HEAD_DIM
SWA_KV = SWA_KV_HEADS * HEAD_DIM
GDN_QK = GDN_HEADS * GDN_DK
GDN_V = GDN_HEADS * GDN_DV
GDN_CONV_CH = 2 * GDN_QK + GDN_V
D_MIX = SWA_Q + GDN_V
IN_SPLITS = (SWA_Q,
             SWA_Q + SWA_KV,
             SWA_Q + 2 * SWA_KV,
             SWA_Q + 2 * SWA_KV + GDN_CONV_CH,
             SWA_Q + 2 * SWA_KV + GDN_CONV_CH + GDN_V,
             SWA_Q + 2 * SWA_KV + GDN_CONV_CH + GDN_V + GDN_HEADS)
D_IN = SWA_Q + 2 * SWA_KV + GDN_CONV_CH + GDN_V + 2 * GDN_HEADS

kernel_name = 'hymba_swa_sink_gdn_hier_moe_step'


def rmsnorm(x, g):
    xf = x.astype(jnp.float32)
    y = xf * lax.rsqrt(jnp.mean(xf * xf, axis=-1, keepdims=True) + EPS)
    return (y * g.astype(jnp.float32)).astype(x.dtype)


def l2norm(x):
    return x * lax.rsqrt(jnp.sum(x * x, axis=-1, keepdims=True) + EPS)


def sink_softmax(s, sink):
    m = jnp.maximum(jnp.max(s, axis=-1, keepdims=True), sink)
    p = jnp.exp(s - m)
    return p / (jnp.sum(p, axis=-1, keepdims=True) + jnp.exp(sink - m))


def swa_prompt(q, k, v, sinks):
    B, S = q.shape[0], q.shape[1]
    nb = S // WINDOW
    qb = q.reshape(B, nb, WINDOW, SWA_KV_HEADS, GQA_GROUP, HEAD_DIM)
    kb = k.reshape(B, nb, WINDOW, SWA_KV_HEADS, HEAD_DIM)
    vb = v.reshape(B, nb, WINDOW, SWA_KV_HEADS, HEAD_DIM)
    pad = ((0, 0), (1, 0), (0, 0), (0, 0), (0, 0))
    kk = jnp.concatenate([jnp.pad(kb, pad)[:, :-1], kb], axis=2)
    vv = jnp.concatenate([jnp.pad(vb, pad)[:, :-1], vb], axis=2)
    s = jnp.einsum('bnqhgd,bnkhd->bnhgqk', qb, kk, preferred_element_type=jnp.float32) * ATTN_SCALE
    qi = WINDOW + jnp.arange(WINDOW)[:, None]
    kj = jnp.arange(2 * WINDOW)[None, :]
    band = (kj <= qi) & (qi - kj < WINDOW)
    has_prev = (jnp.arange(nb) > 0)[:, None, None] | (kj >= WINDOW)[None]
    mask = band[None] & has_prev
    s = jnp.where(mask[None, :, None, None], s, -jnp.inf)
    p = sink_softmax(s, sinks.astype(jnp.float32).reshape(SWA_KV_HEADS, GQA_GROUP, 1, 1))
    o = jnp.einsum('bnhgqk,bnkhd->bnqhgd', p.astype(vv.dtype), vv)
    return o.reshape(B, S, SWA_Q)


def swa_sample(q, k, v, k_buf, v_buf, sinks):
    Bd, T = q.shape[0], q.shape[1]
    Wb = k_buf.shape[1]
    kk = jnp.concatenate([k_buf.astype(k.dtype), k.reshape(Bd, T, SWA_KV_HEADS, HEAD_DIM)], axis=1)
    vv = jnp.concatenate([v_buf.astype(v.dtype), v.reshape(Bd, T, SWA_KV_HEADS, HEAD_DIM)], axis=1)
    qg = q.reshape(Bd, T, SWA_KV_HEADS, GQA_GROUP, HEAD_DIM)
    s = jnp.einsum('bqhgd,bkhd->bhgqk', qg, kk, preferred_element_type=jnp.float32) * ATTN_SCALE
    qpos = Wb + jnp.arange(T)[:, None]
    kpos = jnp.arange(Wb + T)[None, :]
    mask = (kpos <= qpos) & (qpos - kpos < WINDOW)
    s = jnp.where(mask, s, -jnp.inf)
    p = sink_softmax(s, sinks.astype(jnp.float32).reshape(SWA_KV_HEADS, GQA_GROUP, 1, 1))
    o = jnp.einsum('bhgqk,bkhd->bqhgd', p.astype(vv.dtype), vv)
    return o.reshape(Bd, T, SWA_Q), kk[:, -Wb:], vv[:, -Wb:]


def causal_short_conv(x_ext, w):
    T = x_ext.shape[1] - (CONV_W - 1)
    out = x_ext[:, 0:T] * w[0]
    for i in range(1, CONV_W):
        out = out + x_ext[:, i:i + T] * w[i]
    return jax.nn.silu(out)


def chunked_gated_delta(q, k, v, g, beta, S0):
    B, T, H, dk = q.shape
    dv = v.shape[-1]
    C = GDN_CHUNK
    n = -(-T // C)
    pad = n * C - T

    def blocks(x):
        x = jnp.pad(x, [(0, 0), (0, pad)] + [(0, 0)] * (x.ndim - 2))
        x = x.reshape((B, n, C) + x.shape[2:])
        return jnp.moveaxis(x, (1, 3), (0, 2))

    qc, kc, vc, bc = blocks(q), blocks(k), blocks(v), blocks(beta)
    gc = jnp.cumsum(blocks(g), axis=-1)
    causal = jnp.tril(jnp.ones((C, C), bool))
    strict = jnp.tril(jnp.ones((C, C), bool), -1)
    decay = jnp.exp(jnp.where(causal, gc[..., :, None] - gc[..., None, :], -jnp.inf))
    kb = kc * bc[..., None]
    L = jnp.where(strict, jnp.einsum('nbhcd,nbhjd->nbhcj', kb, kc) * decay, 0.0)
    A = L + jnp.eye(C, dtype=L.dtype)
    rhs = jnp.concatenate([vc * bc[..., None], kb * jnp.exp(gc)[..., None]], axis=-1)
    sol = lax.linalg.triangular_solve(A, rhs, left_side=True, lower=True, unit_diagonal=True)
    u, w = sol[..., :dv], sol[..., dv:]
    qk = jnp.einsum('nbhcd,nbhjd->nbhcj', qc, kc) * decay
    q_dec = qc * jnp.exp(gc)[..., None]
    k_dec = kc * jnp.exp(gc[..., -1:] - gc)[..., None]
    g_tot = jnp.exp(gc[..., -1])

    def step(S, xs):
        u_c, w_c, qk_c, qd_c, kd_c, gt_c = xs
        v_new = u_c - jnp.einsum('bhcd,bhdv->bhcv', w_c, S)
        o_c = jnp.einsum('bhcd,bhdv->bhcv', qd_c, S) + jnp.einsum('bhcj,bhjv->bhcv', qk_c, v_new)
        S = S * gt_c[..., None, None] + jnp.einsum('bhcd,bhcv->bhdv', kd_c, v_new)
        return S, o_c

    S_fin, o = lax.scan(step, S0, (u, w, qk, q_dec, k_dec, g_tot))
    o = jnp.moveaxis(o, (0, 2), (1, 3)).reshape(B, n * C, H, dv)[:, :T]
    return o, S_fin


def gated_deltanet(c, z, a, b, S0, A_log, dt_bias, norm_w):
    B, T = c.shape[0], c.shape[1]
    cf = c.astype(jnp.float32)
    q = l2norm(cf[..., :GDN_QK].reshape(B, T, GDN_HEADS, GDN_DK)) * (GDN_DK ** -0.5)
    k = l2norm(cf[..., GDN_QK:2 * GDN_QK].reshape(B, T, GDN_HEADS, GDN_DK))
    v = cf[..., 2 * GDN_QK:].reshape(B, T, GDN_HEADS, GDN_DV)
    beta = jax.nn.sigmoid(b.astype(jnp.float32))
    g = -jnp.exp(A_log.astype(jnp.float32)) * jax.nn.softplus(a.astype(jnp.float32) + dt_bias.astype(jnp.float32))
    o, S = chunked_gated_delta(q, k, v, g, beta, S0.astype(jnp.float32))
    o = o * lax.rsqrt(jnp.mean(o * o, axis=-1, keepdims=True) + EPS) * norm_w.astype(jnp.float32)
    o = o * jax.nn.silu(z.astype(jnp.float32).reshape(B, T, GDN_HEADS, GDN_DV))
    return o.reshape(B, T, GDN_V).astype(c.dtype), S


def routed_expert_mlps(x, expert_ids, gates, w_gate, w_up, w_down):
    N, D = x.shape
    A = expert_ids.shape[0] * expert_ids.shape[1]
    e_flat = expert_ids.reshape(A)
    tok_flat = jnp.arange(A, dtype=jnp.int32) // TOP_K_IN_GROUP
    g_flat = gates.reshape(A)
    order = jnp.argsort(e_flat)
    e_s, tok_s, g_s = e_flat[order], tok_flat[order], g_flat[order]
    counts = jnp.zeros((N_EXPERTS,), jnp.int32).at[e_flat].add(1)
    start = jnp.cumsum(counts) - counts
    padded = (counts + MOE_BLOCK - 1) // MOE_BLOCK * MOE_BLOCK
    p_end = jnp.cumsum(padded)
    dest = (p_end - padded)[e_s] + (jnp.arange(A, dtype=jnp.int32) - start[e_s])
    n_blocks = (A + N_EXPERTS * (MOE_BLOCK - 1) + MOE_BLOCK - 1) // MOE_BLOCK
    P = n_blocks * MOE_BLOCK
    tok_buf = jnp.zeros((P,), jnp.int32).at[dest].set(tok_s)
    gate_buf = jnp.zeros((P,), jnp.float32).at[dest].set(g_s)
    blk_expert = jnp.minimum(
        jnp.searchsorted(p_end, jnp.arange(n_blocks, dtype=jnp.int32) * MOE_BLOCK, side='right'),
        N_EXPERTS - 1)

    def expert_block(args):
        tb, eb = args
        xb = x[tb]
        h = jax.nn.silu(xb @ w_gate[eb]) * (xb @ w_up[eb])
        return h @ w_down[eb]

    y_buf = lax.map(expert_block, (tok_buf.reshape(n_blocks, MOE_BLOCK), blk_expert))
    y = jnp.zeros((N, D), jnp.float32).at[tok_buf].add(
        y_buf.reshape(P, D).astype(jnp.float32) * gate_buf[:, None])
    return y.astype(x.dtype)


def hierarchical_moe(x, w_rg, b_rg, w_re, b_re, w_gate, w_up, w_down):
    B, T, D = x.shape
    xt = x.reshape(B * T, D)
    xf = xt.astype(jnp.float32)
    group_p = jax.nn.softmax(xf @ w_rg.astype(jnp.float32), axis=-1)
    g_sel = jnp.argmax(group_p + b_rg.astype(jnp.float32), axis=-1)
    g_w = jnp.take_along_axis(group_p, g_sel[:, None], axis=-1)
    e_logits = (xf @ w_re.astype(jnp.float32)).reshape(B * T, N_GROUPS, EXPERTS_PER_GROUP)
    e_logits = jnp.take_along_axis(e_logits, g_sel[:, None, None], axis=1)[:, 0]
    e_p = jax.nn.softmax(e_logits, axis=-1)
    _, e_loc = lax.top_k(e_p + b_re.astype(jnp.float32)[g_sel], TOP_K_IN_GROUP)
    e_w = jnp.take_along_axis(e_p, e_loc, axis=-1)
    e_w = e_w / jnp.sum(e_w, axis=-1, keepdims=True)
    expert_ids = (g_sel[:, None] * EXPERTS_PER_GROUP + e_loc).astype(jnp.int32)
    y = routed_expert_mlps(xt, expert_ids, g_w * e_w, w_gate, w_up, w_down)
    return y.reshape(B, T, D)


def setup_inputs(seed: int = 0) -> dict:
    key = jax.random.key(seed)
    ks = jax.random.split(key, 24)
    f32 = jnp.float32
    w_buf = min(WINDOW, PAST_LEN)

    def nrm(k, shape, s):
        return jax.random.normal(k, shape, f32) * s

    dt = jnp.exp(jax.random.uniform(ks[10], (DEPTH, GDN_HEADS), f32, np.log(1e-3), np.log(1e-1)))
    return {
        'x_prompt': nrm(ks[0], (BATCH, SEQ, D_MODEL), 1.0),
        'x_sample': nrm(ks[1], (DEC_BATCH, DEC_SEQ, D_MODEL), 1.0),
        'cache_swa_k': nrm(ks[2], (DEPTH, DEC_BATCH, w_buf, SWA_KV_HEADS, HEAD_DIM), 1.0),
        'cache_swa_v': nrm(ks[3], (DEPTH, DEC_BATCH, w_buf, SWA_KV_HEADS, HEAD_DIM), 1.0),
        'state_gdn_conv': nrm(ks[4], (DEPTH, DEC_BATCH, CONV_W - 1, GDN_CONV_CH), 1.0),
        'state_gdn': nrm(ks[5], (DEPTH, DEC_BATCH, GDN_HEADS, GDN_DK, GDN_DV), 0.1),
        'norm_mix': 1.0 + nrm(ks[6], (DEPTH, D_MODEL), 0.02),
        'w_in': nrm(ks[7], (DEPTH, D_MODEL, D_IN), D_MODEL ** -0.5),
        'swa_sinks': nrm(ks[8], (DEPTH, SWA_HEADS), 0.5),
        'gdn_conv_w': nrm(ks[9], (DEPTH, CONV_W, GDN_CONV_CH), 0.5),
        'gdn_A_log': jnp.log(jax.random.uniform(ks[11], (DEPTH, GDN_HEADS), f32, 1.0, 16.0)),
        'gdn_dt_bias': dt + jnp.log(-jnp.expm1(-dt)),
        'gdn_norm_w': 1.0 + nrm(ks[12], (DEPTH, GDN_DV), 0.02),
        'w_out': nrm(ks[13], (DEPTH, D_MIX, D_MODEL), D_MIX ** -0.5),
        'norm_ffn': 1.0 + nrm(ks[14], (DEPTH, D_MODEL), 0.02),
        'w_router_group': nrm(ks[15], (DEPTH, D_MODEL, N_GROUPS), D_MODEL ** -0.5),
        'b_router_group': nrm(ks[16], (DEPTH, N_GROUPS), 0.01),
        'w_router_expert': nrm(ks[17], (DEPTH, D_MODEL, N_EXPERTS), D_MODEL ** -0.5),
        'b_router_expert': nrm(ks[18], (DEPTH, N_GROUPS, EXPERTS_PER_GROUP), 0.01),
        'w_exp_gate': nrm(ks[19], (DEPTH, N_EXPERTS, D_MODEL, D_EXPERT), D_MODEL ** -0.5),
        'w_exp_up': nrm(ks[20], (DEPTH, N_EXPERTS, D_MODEL, D_EXPERT), D_MODEL ** -0.5),
        'w_exp_down': nrm(ks[21], (DEPTH, N_EXPERTS, D_EXPERT, D_MODEL), D_EXPERT ** -0.5),
        'norm_final': 1.0 + nrm(ks[22], (D_MODEL,), 0.02),
    }


def reference(x_prompt, x_sample, cache_swa_k, cache_swa_v, state_gdn_conv, state_gdn,
              norm_mix, w_in, swa_sinks, gdn_conv_w, gdn_A_log, gdn_dt_bias, gdn_norm_w, w_out,
              norm_ffn, w_router_group, b_router_group, w_router_expert, b_router_expert,
              w_exp_gate, w_exp_up, w_exp_down, norm_final):
    hp, hs = x_prompt, x_sample
    Bp = x_prompt.shape[0]
    kp_l, vp_l, ks_l, vs_l, cp_l, cs_l, sp_l, ss_l = [], [], [], [], [], [], [], []
    for l in range(DEPTH):
        u_p = rmsnorm(hp, norm_mix[l]) @ w_in[l]
        u_s = rmsnorm(hs, norm_mix[l]) @ w_in[l]
        q_p, k_p, v_p, c_p, z_p, a_p, b_p = jnp.split(u_p, IN_SPLITS, axis=-1)
        q_s, k_s, v_s, c_s, z_s, a_s, b_s = jnp.split(u_s, IN_SPLITS, axis=-1)

        o_swa_p = swa_prompt(q_p, k_p, v_p, swa_sinks[l])
        kp_l.append(k_p.reshape(Bp, -1, SWA_KV_HEADS, HEAD_DIM)[:, -WINDOW:])
        vp_l.append(v_p.reshape(Bp, -1, SWA_KV_HEADS, HEAD_DIM)[:, -WINDOW:])
        o_swa_s, kbuf_s, vbuf_s = swa_sample(q_s, k_s, v_s, cache_swa_k[l], cache_swa_v[l], swa_sinks[l])
        ks_l.append(kbuf_s)
        vs_l.append(vbuf_s)

        ext_p = jnp.concatenate([jnp.zeros((Bp, CONV_W - 1, GDN_CONV_CH), c_p.dtype), c_p], axis=1)
        ext_s = jnp.concatenate([state_gdn_conv[l].astype(c_s.dtype), c_s], axis=1)
        cp_l.append(ext_p[:, -(CONV_W - 1):])
        cs_l.append(ext_s[:, -(CONV_W - 1):])
        S0_p = jnp.zeros((Bp, GDN_HEADS, GDN_DK, GDN_DV), jnp.float32)
        o_gdn_p, S_p = gated_deltanet(causal_short_conv(ext_p, gdn_conv_w[l]), z_p, a_p, b_p, S0_p,
                                      gdn_A_log[l], gdn_dt_bias[l], gdn_norm_w[l])
        o_gdn_s, S_s = gated_deltanet(causal_short_conv(ext_s, gdn_conv_w[l]), z_s, a_s, b_s, state_gdn[l],
                                      gdn_A_log[l], gdn_dt_bias[l], gdn_norm_w[l])
        sp_l.append(S_p.astype(state_gdn.dtype))
        ss_l.append(S_s.astype(state_gdn.dtype))

        hp = hp + jnp.concatenate([o_swa_p, o_gdn_p], axis=-1) @ w_out[l]
        hs = hs + jnp.concatenate([o_swa_s, o_gdn_s], axis=-1) @ w_out[l]

        hp = hp + hierarchical_moe(rmsnorm(hp, norm_ffn[l]), w_router_group[l], b_router_group[l],
                                   w_router_expert[l], b_router_expert[l],
                                   w_exp_gate[l], w_exp_up[l], w_exp_down[l])
        hs = hs + hierarchical_moe(rmsnorm(hs, norm_ffn[l]), w_router_group[l], b_router_group[l],
                                   w_router_expert[l], b_router_expert[l],
                                   w_exp_gate[l], w_exp_up[l], w_exp_down[l])

    y_prompt = rmsnorm(hp, norm_final)
    y_sample = rmsnorm(hs, norm_final)
    new_swa_k_prompt = jnp.stack(kp_l, axis=0)
    new_swa_v_prompt = jnp.stack(vp_l, axis=0)
    new_swa_k_sample = jnp.stack(ks_l, axis=0)
    new_swa_v_sample = jnp.stack(vs_l, axis=0)
    new_conv_prompt = jnp.stack(cp_l, axis=0)
    new_conv_sample = jnp.stack(cs_l, axis=0)
    new_gdn_prompt = jnp.stack(sp_l, axis=0)
    new_gdn_sample = jnp.stack(ss_l, axis=0)
    return (y_prompt, y_sample, new_swa_k_prompt, new_swa_v_prompt, new_swa_k_sample, new_swa_v_sample,
            new_conv_prompt, new_conv_sample, new_gdn_prompt, new_gdn_sample)
```

```python
import functools

import jax
import jax.numpy as jnp
from jax import lax
from jax.experimental import pallas as pl
from jax.experimental.pallas import tpu as pltpu

F32 = jnp.float32
BF16 = jnp.bfloat16
I32 = jnp.int32

D_MODEL = 1024
HEAD_DIM = 64
SWA_HEADS = 8
GDN_HEADS = 8
SWA_KV_HEADS = 2
GQA_GROUP = SWA_HEADS // SWA_KV_HEADS
WINDOW = 128
ATTN_SCALE = HEAD_DIM ** -0.5
CONV_W = 4
N_GROUPS = 8
EXPERTS_PER_GROUP = 8
N_EXPERTS = 64
D_EXPERT = 256
EPS = 1e-6

SWA_Q = SWA_HEADS * HEAD_DIM
SWA_KV = SWA_KV_HEADS * HEAD_DIM
GDN_QK = GDN_HEADS * HEAD_DIM
GDN_V = GDN_HEADS * HEAD_DIM
GDN_CONV_CH = 2 * GDN_QK + GDN_V
D_MIX = SWA_Q + GDN_V
D_IN = SWA_Q + 2 * SWA_KV + GDN_CONV_CH + GDN_V + 2 * GDN_HEADS
COL_K = SWA_Q
COL_V = COL_K + SWA_KV
COL_C = COL_V + SWA_KV
COL_Z = COL_C + GDN_CONV_CH
COL_AB = COL_Z + GDN_V

LANES = 128
NEG_BIG = -1e30
ROW_TILE = 512
MOE_BLOCK = 128
COMBINE_TILE = 256
GDN_CHUNK = 64


def _rms(x, g):
    return x * lax.rsqrt(jnp.mean(x * x, axis=-1, keepdims=True) + EPS) * g


def _dot(a, b):
    return jnp.dot(a, b, preferred_element_type=F32)


def _dot_nt(a, b):
    return lax.dot_general(a, b, (((1,), (1,)), ((), ())), preferred_element_type=F32)


def _dot_tn(a, b):
    return lax.dot_general(a, b, (((0,), (0,)), ((), ())), preferred_element_type=F32)


def _split3(x):
    p1 = x.astype(BF16).astype(F32)
    r = x - p1
    p2 = r.astype(BF16).astype(F32)
    p3 = (r - p2).astype(BF16).astype(F32)
    return p1, p2, p3


def _inproj_kernel(x_ref, g_ref, w_ref, q_ref, k_ref, v_ref, c_ref, z_ref, ab_ref):
    x = x_ref[...]
    xb = _rms(x, g_ref[...]).astype(BF16)
    q_ref[...] = _dot(xb, w_ref[:, 0:COL_K])
    k_ref[...] = _dot(xb, w_ref[:, COL_K:COL_V])
    v_ref[...] = _dot(xb, w_ref[:, COL_V:COL_C])
    c_ref[...] = _dot(xb, w_ref[:, COL_C:COL_Z])
    z_ref[...] = _dot(xb, w_ref[:, COL_Z:COL_AB])
    ab_ref[...] = _dot(xb, w_ref[:, COL_AB:D_IN])


def _inproj(x2d, gain, w_bf16):
    n = x2d.shape[0]
    tm = ROW_TILE
    row = lambda w: pl.BlockSpec((tm, w), lambda i: (i, 0))
    full = lambda a: pl.BlockSpec(a.shape, lambda i: (0,) * a.ndim)
    widths = (SWA_Q, SWA_KV, SWA_KV, GDN_CONV_CH, GDN_V, 2 * GDN_HEADS)
    return pl.pallas_call(
        _inproj_kernel,
        grid=(n // tm,),
        in_specs=[row(D_MODEL), full(gain), full(w_bf16)],
        out_specs=[row(w) for w in widths],
        out_shape=[jax.ShapeDtypeStruct((n, w), F32) for w in widths],
        compiler_params=pltpu.CompilerParams(dimension_semantics=("arbitrary",)),
        name="inproj",
    )(x2d, gain, w_bf16)


def _swa_kernel(sink_ref, q_ref, kc_ref, vc_ref, kp_ref, vp_ref, o_ref, *cache_refs,
                bb, t, blocks_per_seq, emit_cache):
    if blocks_per_seq is None:
        has_prev = None
    else:
        has_prev = lax.rem(pl.program_id(0), blocks_per_seq) != 0
    rows = GQA_GROUP * t
    ri = lax.broadcasted_iota(I32, (rows, 1), 0)
    qi = lax.rem(ri, t)
    gi = ri // t
    for b in range(bb):
        out_pieces = []
        for h in range(SWA_KV_HEADS):
            hs = slice(h * HEAD_DIM, (h + 1) * HEAD_DIM)
            kp, vp = kp_ref[b, :, hs], vp_ref[b, :, hs]
            kc, vc = kc_ref[b, :, hs], vc_ref[b, :, hs]
            if emit_cache:
                keys = jnp.concatenate([kp[t:], kc], axis=0)
                vals = jnp.concatenate([vp[t:], vc], axis=0)
                cache_refs[0][b, :, hs] = keys
                cache_refs[1][b, :, hs] = vals
                kj = lax.broadcasted_iota(I32, (rows, WINDOW), 1)
                mask = kj <= qi + (WINDOW - t)
            else:
                keys = jnp.concatenate([kp, kc], axis=0)
                vals = jnp.concatenate([vp, vc], axis=0)
                kj = lax.broadcasted_iota(I32, (rows, WINDOW + t), 1)
                mask = (kj <= qi + WINDOW) & (kj > qi)
                if has_prev is not None:
                    mask = mask & (has_prev | (kj >= WINDOW))
            q4 = jnp.concatenate(
                [q_ref[b, :, (GQA_GROUP * h + g) * HEAD_DIM:(GQA_GROUP * h + g + 1) * HEAD_DIM]
                 for g in range(GQA_GROUP)], axis=0)
            s = _dot_nt(q4.astype(BF16), keys.astype(BF16)) * ATTN_SCALE
            s = jnp.where(mask, s, NEG_BIG)
            sink = jnp.zeros((rows, 1), F32)
            for g in range(GQA_GROUP):
                sink = jnp.where(gi == g, sink_ref[GQA_GROUP * h + g], sink)
            m = jnp.maximum(jnp.max(s, axis=-1, keepdims=True), sink)
            p = jnp.exp(s - m)
            den = jnp.sum(p, axis=-1, keepdims=True) + jnp.exp(sink - m)
            o4 = _dot(p.astype(BF16), vals.astype(BF16)) / den
            out_pieces += [o4[g * t:(g + 1) * t] for g in range(GQA_GROUP)]
        o_ref[b] = jnp.concatenate(out_pieces, axis=-1)


def _swa(sinks, q3, k3, v3, kprev3, vprev3, *, bb, blocks_per_seq, emit_cache):
    nb, t, _ = q3.shape
    cur = lambda w: pl.BlockSpec((bb, t, w), lambda i: (i, 0, 0))
    if blocks_per_seq is None:
        prev = pl.BlockSpec((bb, WINDOW, SWA_KV), lambda i: (i, 0, 0))
    else:
        prev = pl.BlockSpec((bb, WINDOW, SWA_KV), lambda i: (jnp.maximum(i - 1, 0), 0, 0))
    out_specs = [cur(SWA_Q)]
    out_shape = [jax.ShapeDtypeStruct((nb, t, SWA_Q), F32)]
    if emit_cache:
        cache = pl.BlockSpec((bb, WINDOW, SWA_KV), lambda i: (i, 0, 0))
        out_specs += [cache, cache]
        out_shape += [jax.ShapeDtypeStruct((nb, WINDOW, SWA_KV), F32)] * 2
    return pl.pallas_call(
        functools.partial(_swa_kernel, bb=bb, t=t, blocks_per_seq=blocks_per_seq, emit_cache=emit_cache),
        grid=(nb // bb,),
        in_specs=[pl.BlockSpec(memory_space=pltpu.SMEM), cur(SWA_Q), cur(SWA_KV), cur(SWA_KV), prev, prev],
        out_specs=out_specs,
        out_shape=out_shape,
        compiler_params=pltpu.CompilerParams(dimension_semantics=("arbitrary",)),
        name="swa_cache" if emit_cache else "swa_band",
    )(sinks, q3, k3, v3, kprev3, vprev3)


def _gdn_kernel(c_ref, hist_ref, z_ref, ab_ref, s0_ref, cw_ref, alog_ref, dtb_ref, nw_ref,
                o_ref, sfin_ref, cbuf_ref, s_ref, *, chunk, n_chunks):
    ci = pl.program_id(1)
    cn = chunk
    lowp = (lambda x: x.astype(BF16)) if cn >= 16 else (lambda x: x)

    @pl.when(ci == 0)
    def _():
        cbuf_ref[8 - (CONV_W - 1):8, :] = hist_ref[0]
        s_ref[...] = s0_ref[0]

    cbuf_ref[8:8 + cn, :] = c_ref[0]
    conv = cbuf_ref[8 - (CONV_W - 1):8 - (CONV_W - 1) + cn, :] * cw_ref[0:1, :]
    for i in range(1, CONV_W):
        conv = conv + cbuf_ref[8 - (CONV_W - 1) + i:8 - (CONV_W - 1) + i + cn, :] * cw_ref[i:i + 1, :]
    tail = cbuf_ref[8 + cn - (CONV_W - 1):8 + cn, :]
    cbuf_ref[8 - (CONV_W - 1):8, :] = tail
    conv = conv * jax.nn.sigmoid(conv)

    ab = ab_ref[0]
    lane16 = lax.broadcasted_iota(I32, (1, 2 * GDN_HEADS), 1)
    is_g = lane16 < GDN_HEADS
    gcol = -jnp.exp(alog_ref[...]) * jax.nn.softplus(ab + dtb_ref[...])
    gcol = jnp.where(is_g, gcol, 0.0)
    beta = jax.nn.sigmoid(ab)
    r_i = lax.broadcasted_iota(I32, (cn, cn), 0)
    c_i = lax.broadcasted_iota(I32, (cn, cn), 1)
    tril = (r_i >= c_i).astype(F32)
    gc = sum(_dot(tril, p) for p in _split3(gcol))
    eye16 = (lax.broadcasted_iota(I32, (16, 16), 0) == lax.broadcasted_iota(I32, (16, 16), 1)).astype(F32)
    gc_row = sum(_dot_nt(eye16, p) for p in _split3(gc))
    causal = r_i >= c_i
    strict = r_i > c_i
    eye = (r_i == c_i).astype(F32)

    o_pieces = []
    for h in range(GDN_HEADS):
        hs = slice(h * HEAD_DIM, (h + 1) * HEAD_DIM)
        q = conv[:, hs]
        k = conv[:, GDN_QK + h * HEAD_DIM:GDN_QK + (h + 1) * HEAD_DIM]
        v = conv[:, 2 * GDN_QK + h * HEAD_DIM:2 * GDN_QK + (h + 1) * HEAD_DIM]
        q = q * lax.rsqrt(jnp.sum(q * q, axis=-1, keepdims=True) + EPS) * (HEAD_DIM ** -0.5)
        k = k * lax.rsqrt(jnp.sum(k * k, axis=-1, keepdims=True) + EPS)
        b_h = beta[:, GDN_HEADS + h:GDN_HEADS + h + 1]
        g_c = gc[:, h:h + 1]
        g_r = gc_row[h:h + 1, :]
        g_last = gc[cn - 1:cn, h:h + 1]
        decay = jnp.exp(jnp.where(causal, g_c - g_r, NEG_BIG))
        eg = jnp.exp(g_c)
        kb = k * b_h
        lmat = jnp.where(strict, _dot_nt(lowp(kb), lowp(k)) * decay, 0.0)
        inv = eye - lmat
        pw = lmat
        span = 2
        while span < cn:
            pw = _dot(lowp(pw), lowp(pw))
            inv = inv + _dot(lowp(inv), lowp(pw))
            span *= 2
        rhs = jnp.concatenate([v * b_h, kb * eg], axis=-1)
        sol = _dot(lowp(inv), lowp(rhs))
        u, w = sol[:, :HEAD_DIM], sol[:, HEAD_DIM:]
        qk = _dot_nt(lowp(q), lowp(k)) * decay
        s_h = s_ref[h]
        s_lp = lowp(s_h)
        v_new = u - _dot(lowp(w), s_lp)
        o_h = _dot(lowp(q * eg), s_lp) + _dot(lowp(qk), lowp(v_new))
        k_dec = k * jnp.exp(g_last - g_c)
        s_ref[h] = s_h * jnp.exp(g_last) + _dot_tn(lowp(k_dec), lowp(v_new))
        o_h = o_h * lax.rsqrt(jnp.mean(o_h * o_h, axis=-1, keepdims=True) + EPS) * nw_ref[...]
        zh = z_ref[0, :, hs]
        o_pieces.append(o_h * (zh * jax.nn.sigmoid(zh)))
    o_ref[0] = jnp.concatenate(o_pieces, axis=-1)

    @pl.when(ci == n_chunks - 1)
    def _():
        sfin_ref[0] = s_ref[...]


def _gdn(c3, hist, z3, ab3, s0, conv_w, alog16, dtb16, norm_w, *, chunk):
    nb, t, _ = c3.shape
    n_chunks = t // chunk
    tok = lambda w: pl.BlockSpec((1, chunk, w), lambda b, c: (b, c, 0))
    full = lambda a: pl.BlockSpec(a.shape, lambda b, c: (0,) * a.ndim)
    per_seq = lambda a: pl.BlockSpec((1,) + a.shape[1:], lambda b, c: (b,) + (0,) * (a.ndim - 1))
    return pl.pallas_call(
        functools.partial(_gdn_kernel, chunk=chunk, n_chunks=n_chunks),
        grid=(nb, n_chunks),
        in_specs=[tok(GDN_CONV_CH), per_seq(hist), tok(GDN_V), tok(2 * GDN_HEADS), per_seq(s0),
                  full(conv_w), full(alog16), full(dtb16), full(norm_w)],
        out_specs=[tok(GDN_V), per_seq(s0)],
        out_shape=[jax.ShapeDtypeStruct((nb, t, GDN_V), F32), jax.ShapeDtypeStruct(s0.shape, F32)],
        scratch_shapes=[pltpu.VMEM((8 + chunk, GDN_CONV_CH), F32),
                        pltpu.VMEM((GDN_HEADS, HEAD_DIM, HEAD_DIM), F32)],
        compiler_params=pltpu.CompilerParams(dimension_semantics=("arbitrary", "arbitrary")),
        name="gdn",
    )(c3, hist, z3, ab3, s0, conv_w, alog16, dtb16, norm_w)


def _outproj_kernel(x_ref, osw_ref, ogd_ref, wo_ref, gf_ref, wr_ref, br_ref,
                    h_ref, xn_ref, ids_ref, gates_ref, rank_ref, cnt_ref, run_ref):
    i = pl.program_id(0)
    tm = x_ref.shape[0]

    @pl.when(i == 0)
    def _():
        run_ref[...] = jnp.zeros_like(run_ref)

    h = (x_ref[...] + _dot(osw_ref[...].astype(BF16), wo_ref[0:SWA_Q, :])
         + _dot(ogd_ref[...].astype(BF16), wo_ref[SWA_Q:D_MIX, :]))
    h_ref[...] = h
    xn = _rms(h, gf_ref[...])
    xn_ref[...] = xn
    logits = jnp.dot(xn, wr_ref[...], preferred_element_type=F32, precision=lax.Precision.HIGHEST)

    lane = lax.broadcasted_iota(I32, (tm, LANES), 1)
    bias = br_ref[...]
    is_g = (lane >= N_EXPERTS) & (lane < N_EXPERTS + N_GROUPS)
    lg = jnp.where(is_g, logits, NEG_BIG)
    pg = jnp.where(is_g, jnp.exp(lg - jnp.max(lg, axis=-1, keepdims=True)), 0.0)
    group_p = pg / jnp.sum(pg, axis=-1, keepdims=True)
    score_g = jnp.where(is_g, group_p + bias, NEG_BIG)
    g_lane = jnp.min(jnp.where(score_g == jnp.max(score_g, axis=-1, keepdims=True), lane, 2 * LANES),
                     axis=-1, keepdims=True)
    g_w = jnp.sum(jnp.where(lane == g_lane, group_p, 0.0), axis=-1, keepdims=True)
    sel = (lane < N_EXPERTS) & ((lane // EXPERTS_PER_GROUP) == (g_lane - N_EXPERTS))
    le = jnp.where(sel, logits, NEG_BIG)
    pe = jnp.where(sel, jnp.exp(le - jnp.max(le, axis=-1, keepdims=True)), 0.0)
    e_p = pe / jnp.sum(pe, axis=-1, keepdims=True)
    score = jnp.where(sel, e_p + bias, NEG_BIG)
    i1 = jnp.min(jnp.where(score == jnp.max(score, axis=-1, keepdims=True), lane, 2 * LANES),
                 axis=-1, keepdims=True)
    score2 = jnp.where(lane == i1, NEG_BIG, score)
    i2 = jnp.min(jnp.where(score2 == jnp.max(score2, axis=-1, keepdims=True), lane, 2 * LANES),
                 axis=-1, keepdims=True)
    w1 = jnp.sum(jnp.where(lane == i1, e_p, 0.0), axis=-1, keepdims=True)
    w2 = jnp.sum(jnp.where(lane == i2, e_p, 0.0), axis=-1, keepdims=True)
    wsum = w1 + w2

    oh1 = lane == i1
    oh2 = lane == i2
    ohs = (oh1 | oh2).astype(BF16)
    tri = (lax.broadcasted_iota(I32, (tm, tm), 0) > lax.broadcasted_iota(I32, (tm, tm), 1)).astype(BF16)
    before = _dot(tri, ohs) + run_ref[...]
    r1 = jnp.sum(jnp.where(oh1, before, 0.0), axis=-1, keepdims=True)
    r2 = jnp.sum(jnp.where(oh2, before, 0.0), axis=-1, keepdims=True)
    run_ref[...] = run_ref[...] + jnp.sum(ohs.astype(F32), axis=0, keepdims=True)
    cnt_ref[...] = run_ref[...]

    two = lax.broadcasted_iota(I32, (tm, 2), 1)
    ids_ref[...] = jnp.where(two == 0, i1, i2)
    gates_ref[...] = jnp.where(two == 0, g_w * (w1 / wsum), g_w * (w2 / wsum))
    rank_ref[...] = jnp.where(two == 0, r1, r2).astype(I32)


def _outproj(x2d, o_swa, o_gdn, wo_bf16, gain, w_router, b_router):
    n = x2d.shape[0]
    tm = ROW_TILE
    row = lambda w: pl.BlockSpec((tm, w), lambda i: (i, 0))
    full = lambda a: pl.BlockSpec(a.shape, lambda i: (0,) * a.ndim)
    return pl.pallas_call(
        _outproj_kernel,
        grid=(n // tm,),
        in_specs=[row(D_MODEL), row(SWA_Q), row(GDN_V), full(wo_bf16), full(gain), full(w_router), full(b_router)],
        out_specs=[row(D_MODEL), row(D_MODEL), row(2), row(2), row(2), pl.BlockSpec((1, LANES), lambda i: (0, 0))],
        out_shape=[jax.ShapeDtypeStruct((n, D_MODEL), F32), jax.ShapeDtypeStruct((n, D_MODEL), F32),
                   jax.ShapeDtypeStruct((n, 2), I32), jax.ShapeDtypeStruct((n, 2), F32),
                   jax.ShapeDtypeStruct((n, 2), I32), jax.ShapeDtypeStruct((1, LANES), F32)],
        scratch_shapes=[pltpu.VMEM((1, LANES), F32)],
        compiler_params=pltpu.CompilerParams(dimension_semantics=("arbitrary",)),
        name="outproj_router",
    )(x2d, o_swa, o_gdn, wo_bf16, gain, w_router, b_router)


def _row_copy(src_ref, src_row, dst_ref, dst_row, sem):
    return pltpu.make_async_copy(src_ref.at[pl.ds(src_row, 1)], dst_ref.at[pl.ds(dst_row, 1)], sem)


def _scatter_kernel(dest_ref, xp_ref, xs_ref, out_ref, sem, *, tiles_p):
    i = pl.program_id(0)
    tm = xp_ref.shape[0]

    def run(src_ref):
        def issue(r, carry):
            _row_copy(src_ref, r, out_ref, dest_ref[0, 0, 2 * r], sem).start()
            _row_copy(src_ref, r, out_ref, dest_ref[0, 0, 2 * r + 1], sem).start()
            return carry

        lax.fori_loop(0, tm, issue, 0)

        def drain(r, carry):
            _row_copy(src_ref, 0, out_ref, 0, sem).wait()
            _row_copy(src_ref, 0, out_ref, 0, sem).wait()
            return carry

        lax.fori_loop(0, tm, drain, 0)

    @pl.when(i < tiles_p)
    def _():
        run(xp_ref)

    @pl.when(i >= tiles_p)
    def _():
        run(xs_ref)


def _scatter(dest3, xn_p, xn_s):
    tm = ROW_TILE
    tiles_p, tiles_s = xn_p.shape[0] // tm, xn_s.shape[0] // tm
    rows = 2 * (xn_p.shape[0] + xn_s.shape[0])
    return pl.pallas_call(
        functools.partial(_scatter_kernel, tiles_p=tiles_p),
        grid=(tiles_p + tiles_s,),
        in_specs=[pl.BlockSpec((1, 1, 2 * tm), lambda i: (i, 0, 0), memory_space=pltpu.SMEM),
                  pl.BlockSpec((tm, D_MODEL), lambda i: (jnp.minimum(i, tiles_p - 1), 0)),
                  pl.BlockSpec((tm, D_MODEL), lambda i: (jnp.maximum(i - tiles_p, 0), 0))],
        out_specs=pl.BlockSpec(memory_space=pl.ANY),
        out_shape=jax.ShapeDtypeStruct((rows, D_MODEL), F32),
        scratch_shapes=[pltpu.SemaphoreType.DMA(())],
        compiler_params=pltpu.CompilerParams(dimension_semantics=("arbitrary",), has_side_effects=True),
        name="scatter_rows",
    )(dest3, xn_p, xn_s)


def _experts_kernel(blk_ref, exp_ref, lo_ref, hi_ref, first_ref, x_ref, wg_ref, wu_ref, wd_ref, y_ref):
    j = pl.program_id(0)
    lo, hi = lo_ref[j], hi_ref[j]

    @pl.when(hi > lo)
    def _():
        x = x_ref[...].astype(BF16)
        gate = _dot(x, wg_ref[0].astype(BF16))
        up = _dot(x, wu_ref[0].astype(BF16))
        hid = (gate * jax.nn.sigmoid(gate)) * up
        y = _dot(hid.astype(BF16), wd_ref[0].astype(BF16))
        r = lax.broadcasted_iota(I32, (MOE_BLOCK, 1), 0)
        mine = (r >= lo) & (r < hi)

        @pl.when(first_ref[j] == 1)
        def _():
            y_ref[...] = jnp.where(mine, y, 0.0)

        @pl.when(first_ref[j] == 0)
        def _():
            y_ref[...] = jnp.where(mine, y, y_ref[...])


def _experts(items, xs, w_gate, w_up, w_down):
    n_items = items[0].shape[0]
    xblk = pl.BlockSpec((MOE_BLOCK, D_MODEL), lambda j, blk, ex, lo, hi, fi: (blk[j], 0))
    wspec = lambda a: pl.BlockSpec((1,) + a.shape[1:], lambda j, blk, ex, lo, hi, fi: (ex[j], 0, 0))
    return pl.pallas_call(
        _experts_kernel,
        grid_spec=pltpu.PrefetchScalarGridSpec(
            num_scalar_prefetch=5,
            grid=(n_items,),
            in_specs=[xblk, wspec(w_gate), wspec(w_up), wspec(w_down)],
            out_specs=xblk),
        out_shape=jax.ShapeDtypeStruct(xs.shape, F32),
        compiler_params=pltpu.CompilerParams(dimension_semantics=("arbitrary",)),
        name="experts",
    )(*items, xs, w_gate, w_up, w_down)


def _work_items(counts):
    total_rows = 2 * (4 * 4096 + 128 * 8)
    n_blocks = total_rows // MOE_BLOCK
    n_items = n_blocks + N_EXPERTS
    end = jnp.cumsum(counts)
    start = end - counts
    first_blk = start // MOE_BLOCK
    nb = jnp.where(counts > 0, (end - 1) // MOE_BLOCK - first_blk + 1, 0)
    item_end = jnp.cumsum(nb)
    used = item_end[-1]
    j = jnp.arange(n_items, dtype=I32)
    jj = jnp.minimum(j, used - 1)
    e = jnp.minimum(jnp.searchsorted(item_end, jj, side="right").astype(I32), N_EXPERTS - 1)
    blk = first_blk[e] + (jj - (item_end[e] - nb[e]))
    lo = jnp.maximum(start[e], blk * MOE_BLOCK) - blk * MOE_BLOCK
    hi = jnp.minimum(end[e], (blk + 1) * MOE_BLOCK) - blk * MOE_BLOCK
    live = j < used
    lo = jnp.where(live, lo, 0)
    hi = jnp.where(live, hi, 0)
    prev_blk = jnp.concatenate([jnp.full((1,), -1, I32), blk[:-1]])
    first = (live & (blk != prev_blk)).astype(I32)
    return (blk.astype(I32), e, lo.astype(I32), hi.astype(I32), first), start


def _combine_kernel(dest_ref, gates_ref, h_ref, gf_ref, ys_ref, out_ref, buf_ref, sem):
    tm = h_ref.shape[0]

    def issue(r, carry):
        _row_copy(ys_ref, dest_ref[0, 0, 2 * r], buf_ref.at[0], r, sem).start()
        _row_copy(ys_ref, dest_ref[0, 0, 2 * r + 1], buf_ref.at[1], r, sem).start()
        return carry

    lax.fori_loop(0, tm, issue, 0)

    def drain(r, carry):
        _row_copy(ys_ref, 0, buf_ref.at[0], 0, sem).wait()
        _row_copy(ys_ref, 0, buf_ref.at[1], 0, sem).wait()
        return carry

    lax.fori_loop(0, tm, drain, 0)
    gates = gates_ref[...]
    y = h_ref[...] + (buf_ref[0] * gates[:, 0:1] + buf_ref[1] * gates[:, 1:2])
    out_ref[...] = _rms(y, gf_ref[...])


def _combine(dest3, gates, h, gain, ys):
    n = h.shape[0]
    tm = COMBINE_TILE
    row = lambda w: pl.BlockSpec((tm, w), lambda i: (i, 0))
    return pl.pallas_call(
        _combine_kernel,
        grid=(n // tm,),
        in_specs=[pl.BlockSpec((1, 1, 2 * tm), lambda i: (i, 0, 0), memory_space=pltpu.SMEM),
                  row(2), row(D_MODEL), pl.BlockSpec(gain.shape, lambda i: (0, 0)),
                  pl.BlockSpec(memory_space=pl.ANY)],
        out_specs=row(D_MODEL),
        out_shape=jax.ShapeDtypeStruct((n, D_MODEL), F32),
        scratch_shapes=[pltpu.VMEM((2, tm, D_MODEL), F32), pltpu.SemaphoreType.DMA(())],
        compiler_params=pltpu.CompilerParams(dimension_semantics=("arbitrary",)),
        name="combine_norm",
    )(dest3, gates, h, gain, ys)


def kernel(x_prompt, x_sample, cache_swa_k, cache_swa_v, state_gdn_conv, state_gdn, norm_mix, w_in, swa_sinks,
           gdn_conv_w, gdn_A_log, gdn_dt_bias, gdn_norm_w, w_out, norm_ffn, w_router_group, b_router_group,
           w_router_expert, b_router_expert, w_exp_gate, w_exp_up, w_exp_down, norm_final):
    depth = w_in.shape[0]
    assert depth == 1, "single trunk layer"
    bp, sp, _ = x_prompt.shape
    bs, ts, _ = x_sample.shape
    np_, ns = bp * sp, bs * ts
    l = 0

    w_in_b = w_in[l].astype(BF16)
    w_out_b = w_out[l].astype(BF16)
    g_mix = norm_mix[l].reshape(1, D_MODEL)
    g_ffn = norm_ffn[l].reshape(1, D_MODEL)
    g_fin = norm_final.reshape(1, D_MODEL)
    pad = LANES - N_EXPERTS - N_GROUPS
    w_router = jnp.concatenate([w_router_expert[l], w_router_group[l], jnp.zeros((D_MODEL, pad), F32)], axis=1)
    b_router = jnp.concatenate([b_router_expert[l].reshape(-1), b_router_group[l], jnp.zeros((pad,), F32)])[None]
    zeros8 = jnp.zeros((GDN_HEADS,), F32)
    alog16 = jnp.concatenate([gdn_A_log[l], zeros8])[None]
    dtb16 = jnp.concatenate([gdn_dt_bias[l], zeros8])[None]
    norm_w = gdn_norm_w[l].reshape(1, HEAD_DIM)
    sinks = swa_sinks[l]

    q_p, k_p, v_p, c_p, z_p, ab_p = _inproj(x_prompt.reshape(np_, D_MODEL), g_mix, w_in_b)
    nblk = np_ // WINDOW
    k_p3 = k_p.reshape(nblk, WINDOW, SWA_KV)
    v_p3 = v_p.reshape(nblk, WINDOW, SWA_KV)
    (o_swa_p,) = _swa(sinks, q_p.reshape(nblk, WINDOW, SWA_Q), k_p3, v_p3, k_p3, v_p3,
                      bb=1, blocks_per_seq=sp // WINDOW, emit_cache=False)
    c_p3 = c_p.reshape(bp, sp, GDN_CONV_CH)
    o_gdn_p, s_fin_p = _gdn(c_p3, jnp.zeros((bp, CONV_W - 1, GDN_CONV_CH), F32), z_p.reshape(bp, sp, GDN_V),
                            ab_p.reshape(bp, sp, 2 * GDN_HEADS),
                            jnp.zeros((bp, GDN_HEADS, HEAD_DIM, HEAD_DIM), F32),
                            gdn_conv_w[l], alog16, dtb16, norm_w, chunk=GDN_CHUNK)
    h_p, xn_p, ids_p, gates_p, rank_p, cnt_p = _outproj(
        x_prompt.reshape(np_, D_MODEL), o_swa_p.reshape(np_, SWA_Q), o_gdn_p.reshape(np_, GDN_V),
        w_out_b, g_ffn, w_router, b_router)

    q_s, k_s, v_s, c_s, z_s, ab_s = _inproj(x_sample.reshape(ns, D_MODEL), g_mix, w_in_b)
    o_swa_s, kcache_s, vcache_s = _swa(
        sinks, q_s.reshape(bs, ts, SWA_Q), k_s.reshape(bs, ts, SWA_KV), v_s.reshape(bs, ts, SWA_KV),
        cache_swa_k[l].reshape(bs, WINDOW, SWA_KV), cache_swa_v[l].reshape(bs, WINDOW, SWA_KV),
        bb=8, blocks_per_seq=None, emit_cache=True)
    c_s3 = c_s.reshape(bs, ts, GDN_CONV_CH)
    o_gdn_s, s_fin_s = _gdn(c_s3, state_gdn_conv[l], z_s.reshape(bs, ts, GDN_V),
                            ab_s.reshape(bs, ts, 2 * GDN_HEADS), state_gdn[l],
                            gdn_conv_w[l], alog16, dtb16, norm_w, chunk=ts)
    h_s, xn_s, ids_s, gates_s, rank_s, cnt_s = _outproj(
        x_sample.reshape(ns, D_MODEL), o_swa_s.reshape(ns, SWA_Q), o_gdn_s.reshape(ns, GDN_V),
        w_out_b, g_ffn, w_router, b_router)

    cnt_p_i = cnt_p[0, :N_EXPERTS].astype(I32)
    cnt_s_i = cnt_s[0, :N_EXPERTS].astype(I32)
    items, start = _work_items(cnt_p_i + cnt_s_i)
    dest_p = start[ids_p] + rank_p
    dest_s = (start + cnt_p_i)[ids_s] + rank_s
    dest_scatter = jnp.concatenate([dest_p.reshape(-1), dest_s.reshape(-1)]).reshape(-1, 1, 2 * ROW_TILE)
    xs = _scatter(dest_scatter, xn_p, xn_s)
    ys = _experts(items, xs, w_exp_gate[l], w_exp_up[l], w_exp_down[l])
    y_p = _combine(dest_p.reshape(-1, 1, 2 * COMBINE_TILE), gates_p, h_p, g_fin, ys)
    y_s = _combine(dest_s.reshape(-1, 1, 2 * COMBINE_TILE), gates_s, h_s, g_fin, ys)

    kv5 = lambda a, b: a.reshape(b, -1, SWA_KV_HEADS, HEAD_DIM)[None]
    return (y_p.reshape(bp, sp, D_MODEL), y_s.reshape(bs, ts, D_MODEL),
            kv5(k_p.reshape(bp, sp, SWA_KV)[:, -WINDOW:], bp), kv5(v_p.reshape(bp, sp, SWA_KV)[:, -WINDOW:], bp),
            kv5(kcache_s, bs), kv5(vcache_s, bs),
            c_p3[:, -(CONV_W - 1):][None], c_s3[:, -(CONV_W - 1):][None],
            s_fin_p[None], s_fin_s[None])
```

```python
import functools

import jax
import jax.numpy as jnp
from jax import lax
from jax.experimental import pallas as pl
from jax.experimental.pallas import tpu as pltpu

F32 = jnp.float32
BF16 = jnp.bfloat16
I32 = jnp.int32

D_MODEL = 1024
HEAD_DIM = 64
SWA_HEADS = 8
GDN_HEADS = 8
SWA_KV_HEADS = 2
GQA_GROUP = SWA_HEADS // SWA_KV_HEADS
WINDOW = 128
ATTN_SCALE = HEAD_DIM ** -0.5
CONV_W = 4
N_GROUPS = 8
EXPERTS_PER_GROUP = 8
N_EXPERTS = 64
D_EXPERT = 256
EPS = 1e-6

SWA_Q = SWA_HEADS * HEAD_DIM
SWA_KV = SWA_KV_HEADS * HEAD_DIM
GDN_QK = GDN_HEADS * HEAD_DIM
GDN_V = GDN_HEADS * HEAD_DIM
GDN_CONV_CH = 2 * GDN_QK + GDN_V
D_MIX = SWA_Q + GDN_V
D_IN = SWA_Q + 2 * SWA_KV + GDN_CONV_CH + GDN_V + 2 * GDN_HEADS
COL_K = SWA_Q
COL_V = COL_K + SWA_KV
COL_C = COL_V + SWA_KV
COL_Z = COL_C + GDN_CONV_CH
COL_AB = COL_Z + GDN_V

LANES = 128
NEG_BIG = -1e30
ROW_TILE = 512
MOE_BLOCK = 128
COMBINE_TILE = 256
GDN_CHUNK = 64
GDN_TILE = 256
GDN_GROUP = 4
GDN_GROUP_W = GDN_GROUP * HEAD_DIM


def _rms(x, g):
    return x * lax.rsqrt(jnp.mean(x * x, axis=-1, keepdims=True) + EPS) * g


def _dot(a, b):
    return jnp.dot(a, b, preferred_element_type=F32)


def _dot_nt(a, b):
    return lax.dot_general(a, b, (((1,), (1,)), ((), ())), preferred_element_type=F32)


def _dot_tn(a, b):
    return lax.dot_general(a, b, (((0,), (0,)), ((), ())), preferred_element_type=F32)


def _split3(x):
    p1 = x.astype(BF16).astype(F32)
    r = x - p1
    p2 = r.astype(BF16).astype(F32)
    p3 = (r - p2).astype(BF16).astype(F32)
    return p1, p2, p3


def _inproj_kernel(x_ref, g_ref, w_ref, q_ref, k_ref, v_ref, c_ref, z_ref, ab_ref):
    x = x_ref[...]
    xb = _rms(x, g_ref[...]).astype(BF16)
    q_ref[...] = _dot(xb, w_ref[:, 0:COL_K])
    k_ref[...] = _dot(xb, w_ref[:, COL_K:COL_V])
    v_ref[...] = _dot(xb, w_ref[:, COL_V:COL_C])
    c_ref[...] = _dot(xb, w_ref[:, COL_C:COL_Z])
    z_ref[...] = _dot(xb, w_ref[:, COL_Z:COL_AB])
    ab_ref[...] = _dot(xb, w_ref[:, COL_AB:D_IN])


def _inproj(x2d, gain, w_bf16):
    n = x2d.shape[0]
    tm = ROW_TILE
    row = lambda w: pl.BlockSpec((tm, w), lambda i: (i, 0))
    full = lambda a: pl.BlockSpec(a.shape, lambda i: (0,) * a.ndim)
    widths = (SWA_Q, SWA_KV, SWA_KV, GDN_CONV_CH, GDN_V, 2 * GDN_HEADS)
    return pl.pallas_call(
        _inproj_kernel,
        grid=(n // tm,),
        in_specs=[row(D_MODEL), full(gain), full(w_bf16)],
        out_specs=[row(w) for w in widths],
        out_shape=[jax.ShapeDtypeStruct((n, w), F32) for w in widths],
        compiler_params=pltpu.CompilerParams(dimension_semantics=("arbitrary",)),
        name="inproj",
    )(x2d, gain, w_bf16)


def _swa_kernel(sink_ref, q_ref, kc_ref, vc_ref, kp_ref, vp_ref, o_ref, *cache_refs,
                bb, t, blocks_per_seq, emit_cache):
    if blocks_per_seq is None:
        has_prev = None
    else:
        has_prev = lax.rem(pl.program_id(0), blocks_per_seq) != 0
    rows = GQA_GROUP * t
    ri = lax.broadcasted_iota(I32, (rows, 1), 0)
    qi = lax.rem(ri, t)
    gi = ri // t
    for b in range(bb):
        out_pieces = []
        for h in range(SWA_KV_HEADS):
            hs = slice(h * HEAD_DIM, (h + 1) * HEAD_DIM)
            kp, vp = kp_ref[b, :, hs], vp_ref[b, :, hs]
            kc, vc = kc_ref[b, :, hs], vc_ref[b, :, hs]
            if emit_cache:
                keys = jnp.concatenate([kp[t:], kc], axis=0)
                vals = jnp.concatenate([vp[t:], vc], axis=0)
                cache_refs[0][b, :, hs] = keys
                cache_refs[1][b, :, hs] = vals
                kj = lax.broadcasted_iota(I32, (rows, WINDOW), 1)
                mask = kj <= qi + (WINDOW - t)
            else:
                keys = jnp.concatenate([kp, kc], axis=0)
                vals = jnp.concatenate([vp, vc], axis=0)
                kj = lax.broadcasted_iota(I32, (rows, WINDOW + t), 1)
                mask = (kj <= qi + WINDOW) & (kj > qi)
                if has_prev is not None:
                    mask = mask & (has_prev | (kj >= WINDOW))
            q4 = jnp.concatenate(
                [q_ref[b, :, (GQA_GROUP * h + g) * HEAD_DIM:(GQA_GROUP * h + g + 1) * HEAD_DIM]
                 for g in range(GQA_GROUP)], axis=0)
            s = _dot_nt(q4.astype(BF16), keys.astype(BF16)) * ATTN_SCALE
            s = jnp.where(mask, s, NEG_BIG)
            sink = jnp.zeros((rows, 1), F32)
            for g in range(GQA_GROUP):
                sink = jnp.where(gi == g, sink_ref[GQA_GROUP * h + g], sink)
            m = jnp.maximum(jnp.max(s, axis=-1, keepdims=True), sink)
            p = jnp.exp(s - m)
            den = jnp.sum(p, axis=-1, keepdims=True) + jnp.exp(sink - m)
            o4 = _dot(p.astype(BF16), vals.astype(BF16)) / den
            out_pieces += [o4[g * t:(g + 1) * t] for g in range(GQA_GROUP)]
        o_ref[b] = jnp.concatenate(out_pieces, axis=-1)


def _swa(sinks, q3, k3, v3, kprev3, vprev3, *, bb, blocks_per_seq, emit_cache):
    nb, t, _ = q3.shape
    cur = lambda w: pl.BlockSpec((bb, t, w), lambda i: (i, 0, 0))
    if blocks_per_seq is None:
        prev = pl.BlockSpec((bb, WINDOW, SWA_KV), lambda i: (i, 0, 0))
    else:
        prev = pl.BlockSpec((bb, WINDOW, SWA_KV), lambda i: (jnp.maximum(i - 1, 0), 0, 0))
    out_specs = [cur(SWA_Q)]
    out_shape = [jax.ShapeDtypeStruct((nb, t, SWA_Q), F32)]
    if emit_cache:
        cache = pl.BlockSpec((bb, WINDOW, SWA_KV), lambda i: (i, 0, 0))
        out_specs += [cache, cache]
        out_shape += [jax.ShapeDtypeStruct((nb, WINDOW, SWA_KV), F32)] * 2
    return pl.pallas_call(
        functools.partial(_swa_kernel, bb=bb, t=t, blocks_per_seq=blocks_per_seq, emit_cache=emit_cache),
        grid=(nb // bb,),
        in_specs=[pl.BlockSpec(memory_space=pltpu.SMEM), cur(SWA_Q), cur(SWA_KV), cur(SWA_KV), prev, prev],
        out_specs=out_specs,
        out_shape=out_shape,
        compiler_params=pltpu.CompilerParams(dimension_semantics=("arbitrary",)),
        name="swa_cache" if emit_cache else "swa_band",
    )(sinks, q3, k3, v3, kprev3, vprev3)


def _gdn_prep_kernel(c_ref, hist_ref, ab_ref, cw_ref, alog_ref, dtb_ref,
                     u_ref, w_ref, qd_ref, kd_ref, qk_ref, gt_ref, cbuf_ref, *, chunk):
    sb, r, _ = c_ref.shape
    tp = sb * r
    cn = chunk
    low = w_ref.dtype
    hist_rows = CONV_W - 1

    @pl.when(pl.program_id(1) == 0)
    def _():
        cbuf_ref[:, 8 - hist_rows:8, :] = hist_ref[...]

    cbuf_ref[:, 8:8 + r, :] = c_ref[...]
    conv = cbuf_ref[:, 8 - hist_rows:8 - hist_rows + r, :] * cw_ref[0:1, :]
    for i in range(1, CONV_W):
        conv = conv + cbuf_ref[:, 8 - hist_rows + i:8 - hist_rows + i + r, :] * cw_ref[i:i + 1, :]
    tail = cbuf_ref[:, 8 + r - hist_rows:8 + r, :]
    cbuf_ref[:, 8 - hist_rows:8, :] = tail
    conv = (conv * jax.nn.sigmoid(conv)).reshape(tp, GDN_CONV_CH)

    ab = ab_ref[...].reshape(tp, 2 * GDN_HEADS)
    is_g = lax.broadcasted_iota(I32, (1, 2 * GDN_HEADS), 1) < GDN_HEADS
    g = jnp.where(is_g, -jnp.exp(alog_ref[...]) * jax.nn.softplus(ab + dtb_ref[...]), 0.0)
    beta = jax.nn.sigmoid(ab)

    ri = lax.broadcasted_iota(I32, (tp, tp), 0)
    ci = lax.broadcasted_iota(I32, (tp, tp), 1)
    same = (ri // cn) == (ci // cn)
    causal = same & (ri >= ci)
    strict = same & (ri > ci)
    eye = (ri == ci).astype(F32)
    stack = jnp.concatenate([causal.astype(BF16), same.astype(BF16)], axis=0)
    both = sum(_dot(stack, p.astype(BF16)) for p in _split3(g))
    gc, gl = both[:tp], both[tp:]
    e16 = lax.broadcasted_iota(I32, (2 * GDN_HEADS, 2 * GDN_HEADS), 0)
    eye16 = (e16 == lax.broadcasted_iota(I32, (2 * GDN_HEADS, 2 * GDN_HEADS), 1)).astype(BF16)
    gc_row = sum(_dot_nt(eye16, p.astype(BF16)) for p in _split3(gc))
    gt_ref[...] = jnp.exp(gl)
    fold = (lax.broadcasted_iota(I32, (tp, cn), 0) % cn == lax.broadcasted_iota(I32, (tp, cn), 1)).astype(low)

    levels = cn.bit_length() - 2
    us, ws, qds, kds, qks = [], [], [], [], []
    for h in range(GDN_HEADS):
        q = conv[:, h * HEAD_DIM:(h + 1) * HEAD_DIM]
        k = conv[:, GDN_QK + h * HEAD_DIM:GDN_QK + (h + 1) * HEAD_DIM]
        v = conv[:, 2 * GDN_QK + h * HEAD_DIM:2 * GDN_QK + (h + 1) * HEAD_DIM]
        q = q * lax.rsqrt(jnp.sum(q * q, axis=-1, keepdims=True) + EPS) * (HEAD_DIM ** -0.5)
        k = k * lax.rsqrt(jnp.sum(k * k, axis=-1, keepdims=True) + EPS)
        b_h = beta[:, GDN_HEADS + h:GDN_HEADS + h + 1]
        g_c = gc[:, h:h + 1]
        decay = jnp.exp(jnp.where(causal, g_c - gc_row[h:h + 1, :], NEG_BIG))
        eg = jnp.exp(g_c)
        kb = k * b_h
        k_l = k.astype(low)
        lmat = jnp.where(strict, _dot_nt(kb.astype(low), k_l) * decay, 0.0)
        qk = _dot_nt(q.astype(low), k_l) * decay
        qks.append(_dot(qk.astype(low), fold))
        l_l = lmat.astype(low)
        x = eye - lmat
        p = _dot(l_l, l_l)
        for lev in range(1, levels + 1):
            p_l = p.astype(low)
            if lev < levels:
                xp = _dot(jnp.concatenate([x.astype(low), p_l], axis=0), p_l)
                x = x + xp[:tp]
                p = xp[tp:]
            else:
                x = x + _dot(x.astype(low), p_l)
        rhs = jnp.concatenate([v * b_h, kb * eg], axis=-1)
        sol = _dot(x.astype(low), rhs.astype(low))
        us.append(sol[:, :HEAD_DIM])
        ws.append(sol[:, HEAD_DIM:])
        qds.append(q * eg)
        kds.append(k * jnp.exp(gl[:, h:h + 1] - g_c))
    u_ref[...] = jnp.concatenate(us, axis=-1)
    w_ref[...] = jnp.concatenate(ws, axis=-1).astype(low)
    qd_ref[...] = jnp.concatenate(qds, axis=-1).astype(low)
    kd_ref[...] = jnp.concatenate(kds, axis=-1).astype(low)
    qk_ref[...] = jnp.concatenate(qks, axis=-1).astype(low)


def _gdn_scan_kernel(u_ref, w_ref, qd_ref, kd_ref, qk_ref, gt_ref, z_ref, s0_ref, nw_ref,
                     o_ref, sfin_ref, sbd_ref, *, chunk, n_chunks):
    bb = u_ref.shape[0]
    cn = chunk
    low = w_ref.dtype
    gw = GDN_GROUP_W
    ni = pl.program_id(1)

    @pl.when(ni == 0)
    def _():
        sbd_ref[...] = jnp.zeros_like(sbd_ref)
        for b in range(bb):
            for h in range(GDN_HEADS):
                gi, hh = divmod(h, GDN_GROUP)
                ds = slice(hh * HEAD_DIM, (hh + 1) * HEAD_DIM)
                sbd_ref[b, gi, ds, ds] = s0_ref[b, h]

    bdmask = ((lax.broadcasted_iota(I32, (gw, gw), 0) // HEAD_DIM)
              == (lax.broadcasted_iota(I32, (gw, gw), 1) // HEAD_DIM))
    ones_bd = bdmask.astype(BF16)
    vmask = ((lax.broadcasted_iota(I32, (GDN_GROUP * cn, gw), 0) // cn)
             == (lax.broadcasted_iota(I32, (GDN_GROUP * cn, gw), 1) // HEAD_DIM))
    e_row = lax.broadcasted_iota(I32, (2 * GDN_HEADS, gw), 0)
    e_col = lax.broadcasted_iota(I32, (2 * GDN_HEADS, gw), 1) // HEAD_DIM
    for b in range(bb):
        gt_parts = _split3(gt_ref[b, 0:8, :])
        for gi in range(GDN_HEADS // GDN_GROUP):
            ls = slice(gi * gw, (gi + 1) * gw)
            s = sbd_ref[b, gi]
            s_l = s.astype(low)
            v_new = u_ref[b, :, ls] - _dot(w_ref[b, :, ls], s_l)
            v_l = v_new.astype(low)
            vbd = jnp.where(vmask, jnp.concatenate([v_l] * GDN_GROUP, axis=0), jnp.zeros((), low))
            qk_g = qk_ref[b, :, gi * GDN_GROUP * cn:(gi + 1) * GDN_GROUP * cn]
            o = _dot(qd_ref[b, :, ls], s_l) + _dot(qk_g, vbd)
            upd = _dot_tn(kd_ref[b, :, ls], v_l)
            expand = (e_row == e_col + gi * GDN_GROUP).astype(BF16)
            gte = sum(_dot(p.astype(BF16), expand) for p in gt_parts)[0:1]
            sbd_ref[b, gi] = s * gte + jnp.where(bdmask, upd, 0.0)
            o2 = o * o
            hi = o2.astype(BF16)
            lo = (o2 - hi.astype(F32)).astype(BF16)
            ms = (_dot(hi, ones_bd) + _dot(lo, ones_bd)) * (1.0 / HEAD_DIM)
            zg = z_ref[b, :, ls]
            o_ref[b, :, ls] = o * lax.rsqrt(ms + EPS) * nw_ref[...] * (zg * jax.nn.sigmoid(zg))

    @pl.when(ni == n_chunks - 1)
    def _():
        for b in range(bb):
            for h in range(GDN_HEADS):
                gi, hh = divmod(h, GDN_GROUP)
                ds = slice(hh * HEAD_DIM, (hh + 1) * HEAD_DIM)
                sfin_ref[b, h] = sbd_ref[b, gi, ds, ds]


def _gdn(c3, hist, z3, ab3, s0, conv_w, alog16, dtb16, nw_group, *, chunk, seq_block):
    nseq, t, _ = c3.shape
    n = nseq * t
    sb, r = (1, GDN_TILE) if t >= GDN_TILE else (GDN_TILE // t, t)
    tiles = t // r
    low = BF16 if chunk >= 16 else F32
    blk = lambda w: pl.BlockSpec((sb, r, w), lambda s, i: (s, i, 0))
    full = lambda a: pl.BlockSpec(a.shape, lambda s, i: (0,) * a.ndim)
    flat = lambda w: pl.BlockSpec((GDN_TILE, w), lambda s, i: (s * tiles + i, 0))
    widths = (GDN_V, GDN_V, GDN_QK, GDN_QK, GDN_HEADS * chunk, 2 * GDN_HEADS)
    dtypes = (F32, low, low, low, low, F32)
    u, w, qd, kd, qk, gt = pl.pallas_call(
        functools.partial(_gdn_prep_kernel, chunk=chunk),
        grid=(nseq // sb, tiles),
        in_specs=[blk(GDN_CONV_CH), pl.BlockSpec((sb, CONV_W - 1, GDN_CONV_CH), lambda s, i: (s, 0, 0)),
                  blk(2 * GDN_HEADS), full(conv_w), full(alog16), full(dtb16)],
        out_specs=[flat(wd) for wd in widths],
        out_shape=[jax.ShapeDtypeStruct((n, wd), dt) for wd, dt in zip(widths, dtypes)],
        scratch_shapes=[pltpu.VMEM((sb, 8 + r, GDN_CONV_CH), F32)],
        compiler_params=pltpu.CompilerParams(dimension_semantics=("arbitrary", "arbitrary")),
        name="gdn_prep",
    )(c3, hist, ab3, conv_w, alog16, dtb16)

    n_chunks = t // chunk
    tok = lambda wd: pl.BlockSpec((seq_block, chunk, wd), lambda s, c: (s, c, 0))
    per_seq = pl.BlockSpec((seq_block,) + s0.shape[1:], lambda s, c: (s, 0, 0, 0))
    seq3 = lambda a: a.reshape(nseq, t, a.shape[-1])
    return pl.pallas_call(
        functools.partial(_gdn_scan_kernel, chunk=chunk, n_chunks=n_chunks),
        grid=(nseq // seq_block, n_chunks),
        in_specs=[tok(wd) for wd in widths] + [tok(GDN_V), per_seq,
                                               pl.BlockSpec(nw_group.shape, lambda s, c: (0, 0))],
        out_specs=[tok(GDN_V), per_seq],
        out_shape=[jax.ShapeDtypeStruct((nseq, t, GDN_V), F32), jax.ShapeDtypeStruct(s0.shape, F32)],
        scratch_shapes=[pltpu.VMEM((seq_block, GDN_HEADS // GDN_GROUP, GDN_GROUP_W, GDN_GROUP_W), F32)],
        compiler_params=pltpu.CompilerParams(dimension_semantics=("arbitrary", "arbitrary")),
        name="gdn_scan",
    )(seq3(u), seq3(w), seq3(qd), seq3(kd), seq3(qk), seq3(gt), z3, s0, nw_group)


def _outproj_kernel(x_ref, osw_ref, ogd_ref, wo_ref, gf_ref, wr_ref, br_ref,
                    h_ref, xn_ref, ids_ref, gates_ref, rank_ref, cnt_ref, run_ref):
    i = pl.program_id(0)
    tm = x_ref.shape[0]

    @pl.when(i == 0)
    def _():
        run_ref[...] = jnp.zeros_like(run_ref)

    h = (x_ref[...] + _dot(osw_ref[...].astype(BF16), wo_ref[0:SWA_Q, :])
         + _dot(ogd_ref[...].astype(BF16), wo_ref[SWA_Q:D_MIX, :]))
    h_ref[...] = h
    xn = _rms(h, gf_ref[...])
    xn_ref[...] = xn
    logits = _dot(xn.astype(BF16), wr_ref[...])

    lane = lax.broadcasted_iota(I32, (tm, LANES), 1)
    bias = br_ref[...]
    is_g = (lane >= N_EXPERTS) & (lane < N_EXPERTS + N_GROUPS)
    lg = jnp.where(is_g, logits, NEG_BIG)
    pg = jnp.where(is_g, jnp.exp(lg - jnp.max(lg, axis=-1, keepdims=True)), 0.0)
    group_p = pg / jnp.sum(pg, axis=-1, keepdims=True)
    score_g = jnp.where(is_g, group_p + bias, NEG_BIG)
    g_lane = jnp.min(jnp.where(score_g == jnp.max(score_g, axis=-1, keepdims=True), lane, 2 * LANES),
                     axis=-1, keepdims=True)
    g_w = jnp.sum(jnp.where(lane == g_lane, group_p, 0.0), axis=-1, keepdims=True)
    sel = (lane < N_EXPERTS) & ((lane // EXPERTS_PER_GROUP) == (g_lane - N_EXPERTS))
    le = jnp.where(sel, logits, NEG_BIG)
    pe = jnp.where(sel, jnp.exp(le - jnp.max(le, axis=-1, keepdims=True)), 0.0)
    e_p = pe / jnp.sum(pe, axis=-1, keepdims=True)
    score = jnp.where(sel, e_p + bias, NEG_BIG)
    i1 = jnp.min(jnp.where(score == jnp.max(score, axis=-1, keepdims=True), lane, 2 * LANES),
                 axis=-1, keepdims=True)
    score2 = jnp.where(lane == i1, NEG_BIG, score)
    i2 = jnp.min(jnp.where(score2 == jnp.max(score2, axis=-1, keepdims=True), lane, 2 * LANES),
                 axis=-1, keepdims=True)
    w1 = jnp.sum(jnp.where(lane == i1, e_p, 0.0), axis=-1, keepdims=True)
    w2 = jnp.sum(jnp.where(lane == i2, e_p, 0.0), axis=-1, keepdims=True)
    wsum = w1 + w2

    oh1 = lane == i1
    oh2 = lane == i2
    ohs = (oh1 | oh2).astype(BF16)
    tri = (lax.broadcasted_iota(I32, (tm, tm), 0) > lax.broadcasted_iota(I32, (tm, tm), 1)).astype(BF16)
    before = _dot(tri, ohs) + run_ref[...]
    r1 = jnp.sum(jnp.where(oh1, before, 0.0), axis=-1, keepdims=True)
    r2 = jnp.sum(jnp.where(oh2, before, 0.0), axis=-1, keepdims=True)
    run_ref[...] = run_ref[...] + jnp.sum(ohs.astype(F32), axis=0, keepdims=True)
    cnt_ref[...] = run_ref[...]

    two = lax.broadcasted_iota(I32, (tm, 2), 1)
    ids_ref[...] = jnp.where(two == 0, i1, i2)
    gates_ref[...] = jnp.where(two == 0, g_w * (w1 / wsum), g_w * (w2 / wsum))
    rank_ref[...] = jnp.where(two == 0, r1, r2).astype(I32)


def _outproj(x2d, o_swa, o_gdn, wo_bf16, gain, w_router, b_router):
    n = x2d.shape[0]
    tm = ROW_TILE
    row = lambda w: pl.BlockSpec((tm, w), lambda i: (i, 0))
    full = lambda a: pl.BlockSpec(a.shape, lambda i: (0,) * a.ndim)
    return pl.pallas_call(
        _outproj_kernel,
        grid=(n // tm,),
        in_specs=[row(D_MODEL), row(SWA_Q), row(GDN_V), full(wo_bf16), full(gain), full(w_router), full(b_router)],
        out_specs=[row(D_MODEL), row(D_MODEL), row(2), row(2), row(2), pl.BlockSpec((1, LANES), lambda i: (0, 0))],
        out_shape=[jax.ShapeDtypeStruct((n, D_MODEL), F32), jax.ShapeDtypeStruct((n, D_MODEL), F32),
                   jax.ShapeDtypeStruct((n, 2), I32), jax.ShapeDtypeStruct((n, 2), F32),
                   jax.ShapeDtypeStruct((n, 2), I32), jax.ShapeDtypeStruct((1, LANES), F32)],
        scratch_shapes=[pltpu.VMEM((1, LANES), F32)],
        compiler_params=pltpu.CompilerParams(dimension_semantics=("arbitrary",)),
        name="outproj_router",
    )(x2d, o_swa, o_gdn, wo_bf16, gain, w_router, b_router)


def _row_copy(src_ref, src_row, dst_ref, dst_row, sem):
    return pltpu.make_async_copy(src_ref.at[pl.ds(src_row, 1)], dst_ref.at[pl.ds(dst_row, 1)], sem)


def _scatter_kernel(dest_ref, xp_ref, xs_ref, out_ref, sem, *, tiles_p):
    i = pl.program_id(0)
    tm = xp_ref.shape[0]

    def run(src_ref):
        def issue(r, carry):
            _row_copy(src_ref, r, out_ref, dest_ref[0, 0, 2 * r], sem).start()
            _row_copy(src_ref, r, out_ref, dest_ref[0, 0, 2 * r + 1], sem).start()
            return carry

        lax.fori_loop(0, tm, issue, 0)

        def drain(r, carry):
            _row_copy(src_ref, 0, out_ref, 0, sem).wait()
            _row_copy(src_ref, 0, out_ref, 0, sem).wait()
            return carry

        lax.fori_loop(0, tm, drain, 0)

    @pl.when(i < tiles_p)
    def _():
        run(xp_ref)

    @pl.when(i >= tiles_p)
    def _():
        run(xs_ref)


def _scatter(dest3, xn_p, xn_s):
    tm = ROW_TILE
    tiles_p, tiles_s = xn_p.shape[0] // tm, xn_s.shape[0] // tm
    rows = 2 * (xn_p.shape[0] + xn_s.shape[0])
    return pl.pallas_call(
        functools.partial(_scatter_kernel, tiles_p=tiles_p),
        grid=(tiles_p + tiles_s,),
        in_specs=[pl.BlockSpec((1, 1, 2 * tm), lambda i: (i, 0, 0), memory_space=pltpu.SMEM),
                  pl.BlockSpec((tm, D_MODEL), lambda i: (jnp.minimum(i, tiles_p - 1), 0)),
                  pl.BlockSpec((tm, D_MODEL), lambda i: (jnp.maximum(i - tiles_p, 0), 0))],
        out_specs=pl.BlockSpec(memory_space=pl.ANY),
        out_shape=jax.ShapeDtypeStruct((rows, D_MODEL), F32),
        scratch_shapes=[pltpu.SemaphoreType.DMA(())],
        compiler_params=pltpu.CompilerParams(dimension_semantics=("arbitrary",)),
        name="scatter_rows",
    )(dest3, xn_p, xn_s)


def _experts_kernel(blk_ref, exp_ref, lo_ref, hi_ref, first_ref, x_ref, wg_ref, wu_ref, wd_ref, y_ref):
    j = pl.program_id(0)
    lo, hi = lo_ref[j], hi_ref[j]

    @pl.when(hi > lo)
    def _():
        x = x_ref[...].astype(BF16)
        gate = _dot(x, wg_ref[0].astype(BF16))
        up = _dot(x, wu_ref[0].astype(BF16))
        hid = (gate * jax.nn.sigmoid(gate)) * up
        y = _dot(hid.astype(BF16), wd_ref[0].astype(BF16))
        r = lax.broadcasted_iota(I32, (MOE_BLOCK, 1), 0)
        mine = (r >= lo) & (r < hi)

        @pl.when(first_ref[j] == 1)
        def _():
            y_ref[...] = jnp.where(mine, y, 0.0)

        @pl.when(first_ref[j] == 0)
        def _():
            y_ref[...] = jnp.where(mine, y, y_ref[...])


def _experts(items, xs, w_gate, w_up, w_down):
    n_items = items[0].shape[0]
    xblk = pl.BlockSpec((MOE_BLOCK, D_MODEL), lambda j, blk, ex, lo, hi, fi: (blk[j], 0))
    wspec = lambda a: pl.BlockSpec((1,) + a.shape[1:], lambda j, blk, ex, lo, hi, fi: (ex[j], 0, 0))
    return pl.pallas_call(
        _experts_kernel,
        grid_spec=pltpu.PrefetchScalarGridSpec(
            num_scalar_prefetch=5,
            grid=(n_items,),
            in_specs=[xblk, wspec(w_gate), wspec(w_up), wspec(w_down)],
            out_specs=xblk),
        out_shape=jax.ShapeDtypeStruct(xs.shape, F32),
        compiler_params=pltpu.CompilerParams(dimension_semantics=("arbitrary",)),
        name="experts",
    )(*items, xs, w_gate, w_up, w_down)


def _work_items(counts, total_rows):
    n_blocks = total_rows // MOE_BLOCK
    n_items = n_blocks + N_EXPERTS
    end = jnp.cumsum(counts)
    start = end - counts
    first_blk = start // MOE_BLOCK
    nb = jnp.where(counts > 0, (end - 1) // MOE_BLOCK - first_blk + 1, 0)
    item_end = jnp.cumsum(nb)
    used = item_end[-1]
    j = jnp.arange(n_items, dtype=I32)
    jj = jnp.minimum(j, used - 1)
    e = jnp.minimum(jnp.sum((item_end[None, :] <= jj[:, None]).astype(I32), axis=1), N_EXPERTS - 1)
    onehot = (e[:, None] == jnp.arange(N_EXPERTS, dtype=I32)[None, :]).astype(I32)
    pick = lambda a: jnp.sum(onehot * a[None, :], axis=1)
    blk = pick(first_blk) + (jj - (pick(item_end) - pick(nb)))
    lo = jnp.maximum(pick(start), blk * MOE_BLOCK) - blk * MOE_BLOCK
    hi = jnp.minimum(pick(end), (blk + 1) * MOE_BLOCK) - blk * MOE_BLOCK
    live = j < used
    lo = jnp.where(live, lo, 0)
    hi = jnp.where(live, hi, 0)
    prev_blk = jnp.concatenate([jnp.full((1,), -1, I32), blk[:-1]])
    first = (live & (blk != prev_blk)).astype(I32)
    return (blk.astype(I32), e.astype(I32), lo.astype(I32), hi.astype(I32), first), start


def _combine_kernel(dest_ref, gates_ref, h_ref, gf_ref, ys_ref, out_ref, buf_ref, sem):
    tm = h_ref.shape[0]

    def issue(r, carry):
        _row_copy(ys_ref, dest_ref[0, 0, 2 * r], buf_ref.at[0], r, sem).start()
        _row_copy(ys_ref, dest_ref[0, 0, 2 * r + 1], buf_ref.at[1], r, sem).start()
        return carry

    lax.fori_loop(0, tm, issue, 0)

    def drain(r, carry):
        _row_copy(ys_ref, 0, buf_ref.at[0], 0, sem).wait()
        _row_copy(ys_ref, 0, buf_ref.at[1], 0, sem).wait()
        return carry

    lax.fori_loop(0, tm, drain, 0)
    gates = gates_ref[...]
    y = h_ref[...] + (buf_ref[0] * gates[:, 0:1] + buf_ref[1] * gates[:, 1:2])
    out_ref[...] = _rms(y, gf_ref[...])


def _combine(dest3, gates, h, gain, ys):
    n = h.shape[0]
    tm = COMBINE_TILE
    row = lambda w: pl.BlockSpec((tm, w), lambda i: (i, 0))
    return pl.pallas_call(
        _combine_kernel,
        grid=(n // tm,),
        in_specs=[pl.BlockSpec((1, 1, 2 * tm), lambda i: (i, 0, 0), memory_space=pltpu.SMEM),
                  row(2), row(D_MODEL), pl.BlockSpec(gain.shape, lambda i: (0, 0)),
                  pl.BlockSpec(memory_space=pl.ANY)],
        out_specs=row(D_MODEL),
        out_shape=jax.ShapeDtypeStruct((n, D_MODEL), F32),
        scratch_shapes=[pltpu.VMEM((2, tm, D_MODEL), F32), pltpu.SemaphoreType.DMA(())],
        compiler_params=pltpu.CompilerParams(dimension_semantics=("arbitrary",)),
        name="combine_norm",
    )(dest3, gates, h, gain, ys)


def kernel(x_prompt, x_sample, cache_swa_k, cache_swa_v, state_gdn_conv, state_gdn, norm_mix, w_in, swa_sinks,
           gdn_conv_w, gdn_A_log, gdn_dt_bias, gdn_norm_w, w_out, norm_ffn, w_router_group, b_router_group,
           w_router_expert, b_router_expert, w_exp_gate, w_exp_up, w_exp_down, norm_final):
    depth = w_in.shape[0]
    assert depth == 1, "single trunk layer"
    bp, sp, _ = x_prompt.shape
    bs, ts, _ = x_sample.shape
    np_, ns = bp * sp, bs * ts
    l = 0

    w_in_b = w_in[l].astype(BF16)
    w_out_b = w_out[l].astype(BF16)
    g_mix = norm_mix[l].reshape(1, D_MODEL)
    g_ffn = norm_ffn[l].reshape(1, D_MODEL)
    g_fin = norm_final.reshape(1, D_MODEL)
    pad = LANES - N_EXPERTS - N_GROUPS
    w_router = jnp.concatenate([w_router_expert[l], w_router_group[l], jnp.zeros((D_MODEL, pad), F32)],
                               axis=1).astype(BF16)
    b_router = jnp.concatenate([b_router_expert[l].reshape(-1), b_router_group[l], jnp.zeros((pad,), F32)])[None]
    zeros8 = jnp.zeros((GDN_HEADS,), F32)
    alog16 = jnp.concatenate([gdn_A_log[l], zeros8])[None]
    dtb16 = jnp.concatenate([gdn_dt_bias[l], zeros8])[None]
    nw_group = jnp.tile(gdn_norm_w[l].reshape(1, HEAD_DIM), (1, GDN_GROUP))
    sinks = swa_sinks[l]

    q_p, k_p, v_p, c_p, z_p, ab_p = _inproj(x_prompt.reshape(np_, D_MODEL), g_mix, w_in_b)
    nblk = np_ // WINDOW
    k_p3 = k_p.reshape(nblk, WINDOW, SWA_KV)
    v_p3 = v_p.reshape(nblk, WINDOW, SWA_KV)
    (o_swa_p,) = _swa(sinks, q_p.reshape(nblk, WINDOW, SWA_Q), k_p3, v_p3, k_p3, v_p3,
                      bb=1, blocks_per_seq=sp // WINDOW, emit_cache=False)
    c_p3 = c_p.reshape(bp, sp, GDN_CONV_CH)
    o_gdn_p, s_fin_p = _gdn(c_p3, jnp.zeros((bp, CONV_W - 1, GDN_CONV_CH), F32), z_p.reshape(bp, sp, GDN_V),
                            ab_p.reshape(bp, sp, 2 * GDN_HEADS),
                            jnp.zeros((bp, GDN_HEADS, HEAD_DIM, HEAD_DIM), F32),
                            gdn_conv_w[l], alog16, dtb16, nw_group, chunk=GDN_CHUNK, seq_block=bp)
    h_p, xn_p, ids_p, gates_p, rank_p, cnt_p = _outproj(
        x_prompt.reshape(np_, D_MODEL), o_swa_p.reshape(np_, SWA_Q), o_gdn_p.reshape(np_, GDN_V),
        w_out_b, g_ffn, w_router, b_router)

    q_s, k_s, v_s, c_s, z_s, ab_s = _inproj(x_sample.reshape(ns, D_MODEL), g_mix, w_in_b)
    o_swa_s, kcache_s, vcache_s = _swa(
        sinks, q_s.reshape(bs, ts, SWA_Q), k_s.reshape(bs, ts, SWA_KV), v_s.reshape(bs, ts, SWA_KV),
        cache_swa_k[l].reshape(bs, WINDOW, SWA_KV), cache_swa_v[l].reshape(bs, WINDOW, SWA_KV),
        bb=8, blocks_per_seq=None, emit_cache=True)
    c_s3 = c_s.reshape(bs, ts, GDN_CONV_CH)
    o_gdn_s, s_fin_s = _gdn(c_s3, state_gdn_conv[l], z_s.reshape(bs, ts, GDN_V),
                            ab_s.reshape(bs, ts, 2 * GDN_HEADS), state_gdn[l],
                            gdn_conv_w[l], alog16, dtb16, nw_group, chunk=ts, seq_block=8)
    h_s, xn_s, ids_s, gates_s, rank_s, cnt_s = _outproj(
        x_sample.reshape(ns, D_MODEL), o_swa_s.reshape(ns, SWA_Q), o_gdn_s.reshape(ns, GDN_V),
        w_out_b, g_ffn, w_router, b_router)

    cnt_p_i = cnt_p[0, :N_EXPERTS].astype(I32)
    cnt_s_i = cnt_s[0, :N_EXPERTS].astype(I32)
    items, start = _work_items(cnt_p_i + cnt_s_i, 2 * (np_ + ns))
    expert_ids = jnp.arange(N_EXPERTS, dtype=I32)
    lookup = lambda table, ids: jnp.sum(jnp.where(ids[..., None] == expert_ids, table, 0), axis=-1)
    dest_p = lookup(start, ids_p) + rank_p
    dest_s = lookup(start + cnt_p_i, ids_s) + rank_s
    dest_scatter = jnp.concatenate([dest_p.reshape(-1), dest_s.reshape(-1)]).reshape(-1, 1, 2 * ROW_TILE)
    xs = _scatter(dest_scatter, xn_p, xn_s)
    ys = _experts(items, xs, w_exp_gate[l], w_exp_up[l], w_exp_down[l])
    y_p = _combine(dest_p.reshape(-1, 1, 2 * COMBINE_TILE), gates_p, h_p, g_fin, ys)
    y_s = _combine(dest_s.reshape(-1, 1, 2 * COMBINE_TILE), gates_s, h_s, g_fin, ys)

    kv5 = lambda a, b: a.reshape(b, -1, SWA_KV_HEADS, HEAD_DIM)[None]
    return (y_p.reshape(bp, sp, D_MODEL), y_s.reshape(bs, ts, D_MODEL),
            kv5(k_p.reshape(bp, sp, SWA_KV)[:, -WINDOW:], bp), kv5(v_p.reshape(bp, sp, SWA_KV)[:, -WINDOW:], bp),
            kv5(kcache_s, bs), kv5(vcache_s, bs),
            c_p3[:, -(CONV_W - 1):][None], c_s3[:, -(CONV_W - 1):][None],
            s_fin_p[None], s_fin_s[None])
```

```python
import functools

import jax
import jax.numpy as jnp
from jax import lax
from jax.experimental import pallas as pl
from jax.experimental.pallas import tpu as pltpu

F32 = jnp.float32
BF16 = jnp.bfloat16
I32 = jnp.int32

D_MODEL = 1024
HEAD_DIM = 64
SWA_HEADS = 8
GDN_HEADS = 8
SWA_KV_HEADS = 2
GQA_GROUP = SWA_HEADS // SWA_KV_HEADS
WINDOW = 128
ATTN_SCALE = HEAD_DIM ** -0.5
CONV_W = 4
N_GROUPS = 8
EXPERTS_PER_GROUP = 8
N_EXPERTS = 64
D_EXPERT = 256
EPS = 1e-6

SWA_Q = SWA_HEADS * HEAD_DIM
SWA_KV = SWA_KV_HEADS * HEAD_DIM
GDN_QK = GDN_HEADS * HEAD_DIM
GDN_V = GDN_HEADS * HEAD_DIM
GDN_CONV_CH = 2 * GDN_QK + GDN_V
D_MIX = SWA_Q + GDN_V
D_IN = SWA_Q + 2 * SWA_KV + GDN_CONV_CH + GDN_V + 2 * GDN_HEADS
COL_K = SWA_Q
COL_V = COL_K + SWA_KV
COL_C = COL_V + SWA_KV
COL_Z = COL_C + GDN_CONV_CH
COL_AB = COL_Z + GDN_V

LANES = 128
NEG_BIG = -1e30
ROW_TILE = 512
MOE_BLOCK = 128
COMBINE_TILE = 256
DMA_UNROLL = 8
GDN_CHUNK = 64
GDN_TILE = 256
GDN_GROUP = 4
GDN_GROUP_W = GDN_GROUP * HEAD_DIM


def _rms(x, g):
    return x * lax.rsqrt(jnp.mean(x * x, axis=-1, keepdims=True) + EPS) * g


def _dot(a, b):
    return jnp.dot(a, b, preferred_element_type=F32)


def _dot_nt(a, b):
    return lax.dot_general(a, b, (((1,), (1,)), ((), ())), preferred_element_type=F32)


def _dot_tn(a, b):
    return lax.dot_general(a, b, (((0,), (0,)), ((), ())), preferred_element_type=F32)


def _split3(x):
    p1 = x.astype(BF16).astype(F32)
    r = x - p1
    p2 = r.astype(BF16).astype(F32)
    p3 = (r - p2).astype(BF16).astype(F32)
    return p1, p2, p3


def _inproj_kernel(x_ref, g_ref, w_ref, q_ref, k_ref, v_ref, c_ref, z_ref, ab_ref):
    x = x_ref[...]
    xb = _rms(x, g_ref[...]).astype(BF16)
    q_ref[...] = _dot(xb, w_ref[:, 0:COL_K])
    k_ref[...] = _dot(xb, w_ref[:, COL_K:COL_V])
    v_ref[...] = _dot(xb, w_ref[:, COL_V:COL_C])
    c_ref[...] = _dot(xb, w_ref[:, COL_C:COL_Z])
    z_ref[...] = _dot(xb, w_ref[:, COL_Z:COL_AB])
    ab_ref[...] = _dot(xb, w_ref[:, COL_AB:D_IN])


def _inproj(x2d, gain, w_bf16):
    n = x2d.shape[0]
    tm = ROW_TILE
    row = lambda w: pl.BlockSpec((tm, w), lambda i: (i, 0))
    full = lambda a: pl.BlockSpec(a.shape, lambda i: (0,) * a.ndim)
    widths = (SWA_Q, SWA_KV, SWA_KV, GDN_CONV_CH, GDN_V, 2 * GDN_HEADS)
    return pl.pallas_call(
        _inproj_kernel,
        grid=(n // tm,),
        in_specs=[row(D_MODEL), full(gain), full(w_bf16)],
        out_specs=[row(w) for w in widths],
        out_shape=[jax.ShapeDtypeStruct((n, w), F32) for w in widths],
        compiler_params=pltpu.CompilerParams(dimension_semantics=("arbitrary",)),
        name="inproj",
    )(x2d, gain, w_bf16)


def _swa_kernel(sink_ref, q_ref, kc_ref, vc_ref, kp_ref, vp_ref, o_ref, *cache_refs,
                bb, t, blocks_per_seq, emit_cache):
    if blocks_per_seq is None:
        has_prev = None
    else:
        has_prev = lax.rem(pl.program_id(0), blocks_per_seq) != 0
    rows = GQA_GROUP * t
    ri = lax.broadcasted_iota(I32, (rows, 1), 0)
    qi = lax.rem(ri, t)
    gi = ri // t
    for b in range(bb):
        out_pieces = []
        for h in range(SWA_KV_HEADS):
            hs = slice(h * HEAD_DIM, (h + 1) * HEAD_DIM)
            kp, vp = kp_ref[b, :, hs], vp_ref[b, :, hs]
            kc, vc = kc_ref[b, :, hs], vc_ref[b, :, hs]
            if emit_cache:
                keys = jnp.concatenate([kp[t:], kc], axis=0)
                vals = jnp.concatenate([vp[t:], vc], axis=0)
                cache_refs[0][b, :, hs] = keys
                cache_refs[1][b, :, hs] = vals
                kj = lax.broadcasted_iota(I32, (rows, WINDOW), 1)
                mask = kj <= qi + (WINDOW - t)
            else:
                keys = jnp.concatenate([kp, kc], axis=0)
                vals = jnp.concatenate([vp, vc], axis=0)
                kj = lax.broadcasted_iota(I32, (rows, WINDOW + t), 1)
                mask = (kj <= qi + WINDOW) & (kj > qi)
                if has_prev is not None:
                    mask = mask & (has_prev | (kj >= WINDOW))
            q4 = jnp.concatenate(
                [q_ref[b, :, (GQA_GROUP * h + g) * HEAD_DIM:(GQA_GROUP * h + g + 1) * HEAD_DIM]
                 for g in range(GQA_GROUP)], axis=0)
            s = _dot_nt(q4.astype(BF16), keys.astype(BF16)) * ATTN_SCALE
            s = jnp.where(mask, s, NEG_BIG)
            sink = jnp.zeros((rows, 1), F32)
            for g in range(GQA_GROUP):
                sink = jnp.where(gi == g, sink_ref[GQA_GROUP * h + g], sink)
            m = jnp.maximum(jnp.max(s, axis=-1, keepdims=True), sink)
            p = jnp.exp(s - m)
            den = jnp.sum(p, axis=-1, keepdims=True) + jnp.exp(sink - m)
            o4 = _dot(p.astype(BF16), vals.astype(BF16)) / den
            out_pieces += [o4[g * t:(g + 1) * t] for g in range(GQA_GROUP)]
        o_ref[b] = jnp.concatenate(out_pieces, axis=-1)


def _swa(sinks, q3, k3, v3, kprev3, vprev3, *, bb, blocks_per_seq, emit_cache):
    nb, t, _ = q3.shape
    cur = lambda w: pl.BlockSpec((bb, t, w), lambda i: (i, 0, 0))
    if blocks_per_seq is None:
        prev = pl.BlockSpec((bb, WINDOW, SWA_KV), lambda i: (i, 0, 0))
    else:
        prev = pl.BlockSpec((bb, WINDOW, SWA_KV), lambda i: (jnp.maximum(i - 1, 0), 0, 0))
    out_specs = [cur(SWA_Q)]
    out_shape = [jax.ShapeDtypeStruct((nb, t, SWA_Q), F32)]
    if emit_cache:
        cache = pl.BlockSpec((bb, WINDOW, SWA_KV), lambda i: (i, 0, 0))
        out_specs += [cache, cache]
        out_shape += [jax.ShapeDtypeStruct((nb, WINDOW, SWA_KV), F32)] * 2
    return pl.pallas_call(
        functools.partial(_swa_kernel, bb=bb, t=t, blocks_per_seq=blocks_per_seq, emit_cache=emit_cache),
        grid=(nb // bb,),
        in_specs=[pl.BlockSpec(memory_space=pltpu.SMEM), cur(SWA_Q), cur(SWA_KV), cur(SWA_KV), prev, prev],
        out_specs=out_specs,
        out_shape=out_shape,
        compiler_params=pltpu.CompilerParams(dimension_semantics=("arbitrary",)),
        name="swa_cache" if emit_cache else "swa_band",
    )(sinks, q3, k3, v3, kprev3, vprev3)


def _gdn_prep_kernel(c_ref, hist_ref, ab_ref, cw_ref, alog_ref, dtb_ref,
                     u_ref, w_ref, qd_ref, kd_ref, qk_ref, gt_ref, cbuf_ref, *, chunk):
    sb, r, _ = c_ref.shape
    tp = sb * r
    cn = chunk
    low = w_ref.dtype
    hist_rows = CONV_W - 1

    @pl.when(pl.program_id(1) == 0)
    def _():
        cbuf_ref[:, 8 - hist_rows:8, :] = hist_ref[...]

    cbuf_ref[:, 8:8 + r, :] = c_ref[...]
    conv = cbuf_ref[:, 8 - hist_rows:8 - hist_rows + r, :] * cw_ref[0:1, :]
    for i in range(1, CONV_W):
        conv = conv + cbuf_ref[:, 8 - hist_rows + i:8 - hist_rows + i + r, :] * cw_ref[i:i + 1, :]
    tail = cbuf_ref[:, 8 + r - hist_rows:8 + r, :]
    cbuf_ref[:, 8 - hist_rows:8, :] = tail
    conv = (conv * jax.nn.sigmoid(conv)).reshape(tp, GDN_CONV_CH)

    ab = ab_ref[...].reshape(tp, 2 * GDN_HEADS)
    is_g = lax.broadcasted_iota(I32, (1, 2 * GDN_HEADS), 1) < GDN_HEADS
    g = jnp.where(is_g, -jnp.exp(alog_ref[...]) * jax.nn.softplus(ab + dtb_ref[...]), 0.0)
    beta = jax.nn.sigmoid(ab)

    ri = lax.broadcasted_iota(I32, (tp, tp), 0)
    ci = lax.broadcasted_iota(I32, (tp, tp), 1)
    same = (ri // cn) == (ci // cn)
    causal = same & (ri >= ci)
    strict = same & (ri > ci)
    eye = (ri == ci).astype(F32)
    stack = jnp.concatenate([causal.astype(BF16), same.astype(BF16)], axis=0)
    both = sum(_dot(stack, p.astype(BF16)) for p in _split3(g))
    gc, gl = both[:tp], both[tp:]
    e16 = lax.broadcasted_iota(I32, (2 * GDN_HEADS, 2 * GDN_HEADS), 0)
    eye16 = (e16 == lax.broadcasted_iota(I32, (2 * GDN_HEADS, 2 * GDN_HEADS), 1)).astype(BF16)
    gc_row = sum(_dot_nt(eye16, p.astype(BF16)) for p in _split3(gc))
    gt_ref[...] = jnp.exp(gl)
    fold = (lax.broadcasted_iota(I32, (tp, cn), 0) % cn == lax.broadcasted_iota(I32, (tp, cn), 1)).astype(low)

    levels = cn.bit_length() - 2
    us, ws, qds, kds, qks = [], [], [], [], []
    for h in range(GDN_HEADS):
        q = conv[:, h * HEAD_DIM:(h + 1) * HEAD_DIM]
        k = conv[:, GDN_QK + h * HEAD_DIM:GDN_QK + (h + 1) * HEAD_DIM]
        v = conv[:, 2 * GDN_QK + h * HEAD_DIM:2 * GDN_QK + (h + 1) * HEAD_DIM]
        q = q * lax.rsqrt(jnp.sum(q * q, axis=-1, keepdims=True) + EPS) * (HEAD_DIM ** -0.5)
        k = k * lax.rsqrt(jnp.sum(k * k, axis=-1, keepdims=True) + EPS)
        b_h = beta[:, GDN_HEADS + h:GDN_HEADS + h + 1]
        g_c = gc[:, h:h + 1]
        decay = jnp.exp(jnp.where(causal, g_c - gc_row[h:h + 1, :], NEG_BIG))
        eg = jnp.exp(g_c)
        kb = k * b_h
        k_l = k.astype(low)
        lmat = jnp.where(strict, _dot_nt(kb.astype(low), k_l) * decay, 0.0)
        qk = _dot_nt(q.astype(low), k_l) * decay
        qks.append(_dot(qk.astype(low), fold))
        l_l = lmat.astype(low)
        x = eye - lmat
        p = _dot(l_l, l_l)
        for lev in range(1, levels + 1):
            p_l = p.astype(low)
            if lev < levels:
                xp = _dot(jnp.concatenate([x.astype(low), p_l], axis=0), p_l)
                x = x + xp[:tp]
                p = xp[tp:]
            else:
                x = x + _dot(x.astype(low), p_l)
        rhs = jnp.concatenate([v * b_h, kb * eg], axis=-1)
        sol = _dot(x.astype(low), rhs.astype(low))
        us.append(sol[:, :HEAD_DIM])
        ws.append(sol[:, HEAD_DIM:])
        qds.append(q * eg)
        kds.append(k * jnp.exp(gl[:, h:h + 1] - g_c))
    u_ref[...] = jnp.concatenate(us, axis=-1)
    w_ref[...] = jnp.concatenate(ws, axis=-1).astype(low)
    qd_ref[...] = jnp.concatenate(qds, axis=-1).astype(low)
    kd_ref[...] = jnp.concatenate(kds, axis=-1).astype(low)
    qk_ref[...] = jnp.concatenate(qks, axis=-1).astype(low)


def _gdn_prep_pair_kernel(c_ref, hist_ref, ab_ref, cw_ref, alog_ref, dtb_ref,
                          u_ref, w_ref, qd_ref, kd_ref, qk_ref, gt_ref, cbuf_ref):
    _, r, _ = c_ref.shape
    tp = r
    cn = HEAD_DIM
    pair_w = 2 * HEAD_DIM
    hist_rows = CONV_W - 1

    @pl.when(pl.program_id(1) == 0)
    def _():
        cbuf_ref[:, 8 - hist_rows:8, :] = hist_ref[...]

    cbuf_ref[:, 8:8 + r, :] = c_ref[...]
    conv = cbuf_ref[:, 8 - hist_rows:8 - hist_rows + r, :] * cw_ref[0:1, :]
    for i in range(1, CONV_W):
        conv = conv + cbuf_ref[:, 8 - hist_rows + i:8 - hist_rows + i + r, :] * cw_ref[i:i + 1, :]
    tail = cbuf_ref[:, 8 + r - hist_rows:8 + r, :]
    cbuf_ref[:, 8 - hist_rows:8, :] = tail
    conv = (conv * jax.nn.sigmoid(conv)).reshape(tp, GDN_CONV_CH)

    ab = ab_ref[...].reshape(tp, 2 * GDN_HEADS)
    is_g = lax.broadcasted_iota(I32, (1, 2 * GDN_HEADS), 1) < GDN_HEADS
    g = jnp.where(is_g, -jnp.exp(alog_ref[...]) * jax.nn.softplus(ab + dtb_ref[...]), 0.0)
    beta = jax.nn.sigmoid(ab)

    ri = lax.broadcasted_iota(I32, (tp, tp), 0)
    ci = lax.broadcasted_iota(I32, (tp, tp), 1)
    same = (ri // cn) == (ci // cn)
    same_l = same.astype(BF16)
    stack = jnp.concatenate([(same & (ri >= ci)).astype(BF16), same_l], axis=0)
    both = sum(_dot(stack, p.astype(BF16)) for p in _split3(g))
    gc, gl = both[:tp], both[tp:]
    gt_ref[...] = jnp.exp(gl)

    lane = lax.broadcasted_iota(I32, (tp, pair_w), 1)
    c_in = lax.broadcasted_iota(I32, (tp, pair_w), 0) % cn
    j_in = lane % cn
    left = lane < cn
    causal = c_in >= j_in
    strict = c_in > j_in
    diag = c_in == j_in
    eye = diag.astype(F32)
    bdmask = ((lax.broadcasted_iota(I32, (pair_w, pair_w), 0) // cn)
              == (lax.broadcasted_iota(I32, (pair_w, pair_w), 1) // cn))
    ones_bd = bdmask.astype(BF16)

    def bd(m):
        return jnp.where(bdmask, jnp.concatenate([m, m], axis=0), 0.0).astype(BF16)

    def head_sum(x):
        hi = x.astype(BF16)
        lo = (x - hi.astype(F32)).astype(BF16)
        return _dot(hi, ones_bd) + _dot(lo, ones_bd)

    levels = cn.bit_length() - 2
    chunks = [slice(n * cn, (n + 1) * cn) for n in range(tp // cn)]
    for p in range(GDN_HEADS // 2):
        a, b = 2 * p, 2 * p + 1
        ls = slice(p * pair_w, (p + 1) * pair_w)
        q = conv[:, p * pair_w:(p + 1) * pair_w]
        k = conv[:, GDN_QK + p * pair_w:GDN_QK + (p + 1) * pair_w]
        v = conv[:, 2 * GDN_QK + p * pair_w:2 * GDN_QK + (p + 1) * pair_w]
        q = q * lax.rsqrt(head_sum(q * q) + EPS) * (HEAD_DIM ** -0.5)
        k = k * lax.rsqrt(head_sum(k * k) + EPS)
        pick = lambda m, off: jnp.where(left, m[:, off + a:off + a + 1], m[:, off + b:off + b + 1])
        gcp, glp, bp = pick(gc, 0), pick(gl, 0), pick(beta, GDN_HEADS)
        rowm = sum(_dot(same_l, part.astype(BF16)) for part in _split3(jnp.where(diag, gcp, 0.0)))
        decay = jnp.exp(jnp.where(causal, gcp - rowm, NEG_BIG))
        eg = jnp.exp(gcp)
        kb = k * bp
        vb = v * bp
        kbeg = kb * eg
        qd_ref[:, ls] = (q * eg).astype(BF16)
        kd_ref[:, ls] = (k * jnp.exp(glp - gcp)).astype(BF16)
        q_l, kb_l = q.astype(BF16), kb.astype(BF16)
        kbd = [bd(k[rs]) for rs in chunks]
        kk = jnp.concatenate([_dot_nt(kb_l[rs], kbd[n]) for n, rs in enumerate(chunks)], axis=0)
        qk = jnp.concatenate([_dot_nt(q_l[rs], kbd[n]) for n, rs in enumerate(chunks)], axis=0)
        qk_ref[:, ls] = (qk * decay).astype(BF16)
        lmat = jnp.where(strict, kk * decay, 0.0)
        xs = [eye[rs] - lmat[rs] for rs in chunks]
        ps = [_dot(lmat[rs].astype(BF16), bd(lmat[rs])) for rs in chunks]
        for lev in range(1, levels + 1):
            pbd = [bd(pm) for pm in ps]
            if lev < levels:
                xp = [_dot(jnp.concatenate([xm, pm], axis=0).astype(BF16), wm) for xm, pm, wm in zip(xs, ps, pbd)]
                xs = [xm + m[:cn] for xm, m in zip(xs, xp)]
                ps = [m[cn:] for m in xp]
            else:
                xs = [xm + _dot(xm.astype(BF16), wm) for xm, wm in zip(xs, pbd)]
        x_l = [xm.astype(BF16) for xm in xs]
        u_ref[:, ls] = jnp.concatenate([_dot(xm, bd(vb[rs])) for xm, rs in zip(x_l, chunks)], axis=0)
        w_ref[:, ls] = jnp.concatenate([_dot(xm, bd(kbeg[rs])) for xm, rs in zip(x_l, chunks)], axis=0).astype(BF16)


def _gdn_scan_kernel(u_ref, w_ref, qd_ref, kd_ref, qk_ref, gt_ref, z_ref, s0_ref, nw_ref,
                     o_ref, sfin_ref, sbd_ref, *, chunk, n_chunks):
    bb = u_ref.shape[0]
    cn = chunk
    low = w_ref.dtype
    gw = GDN_GROUP_W
    ni = pl.program_id(1)

    @pl.when(ni == 0)
    def _():
        sbd_ref[...] = jnp.zeros_like(sbd_ref)
        for b in range(bb):
            for h in range(GDN_HEADS):
                gi, hh = divmod(h, GDN_GROUP)
                ds = slice(hh * HEAD_DIM, (hh + 1) * HEAD_DIM)
                sbd_ref[b, gi, ds, ds] = s0_ref[b, h]

    bdmask = ((lax.broadcasted_iota(I32, (gw, gw), 0) // HEAD_DIM)
              == (lax.broadcasted_iota(I32, (gw, gw), 1) // HEAD_DIM))
    ones_bd = bdmask.astype(BF16)
    vmask = ((lax.broadcasted_iota(I32, (GDN_GROUP * cn, gw), 0) // cn)
             == (lax.broadcasted_iota(I32, (GDN_GROUP * cn, gw), 1) // HEAD_DIM))
    e_row = lax.broadcasted_iota(I32, (2 * GDN_HEADS, gw), 0)
    e_col = lax.broadcasted_iota(I32, (2 * GDN_HEADS, gw), 1) // HEAD_DIM
    for b in range(bb):
        gt_parts = _split3(gt_ref[b, 0:8, :])
        for gi in range(GDN_HEADS // GDN_GROUP):
            ls = slice(gi * gw, (gi + 1) * gw)
            s = sbd_ref[b, gi]
            s_l = s.astype(low)
            v_new = u_ref[b, :, ls] - _dot(w_ref[b, :, ls], s_l)
            v_l = v_new.astype(low)
            vbd = jnp.where(vmask, jnp.concatenate([v_l] * GDN_GROUP, axis=0), jnp.zeros((), low))
            qk_g = qk_ref[b, :, gi * GDN_GROUP * cn:(gi + 1) * GDN_GROUP * cn]
            o = _dot(qd_ref[b, :, ls], s_l) + _dot(qk_g, vbd)
            upd = _dot_tn(kd_ref[b, :, ls], v_l)
            expand = (e_row == e_col + gi * GDN_GROUP).astype(BF16)
            gte = sum(_dot(p.astype(BF16), expand) for p in gt_parts)[0:1]
            sbd_ref[b, gi] = s * gte + jnp.where(bdmask, upd, 0.0)
            o2 = o * o
            hi = o2.astype(BF16)
            lo = (o2 - hi.astype(F32)).astype(BF16)
            ms = (_dot(hi, ones_bd) + _dot(lo, ones_bd)) * (1.0 / HEAD_DIM)
            zg = z_ref[b, :, ls]
            o_ref[b, :, ls] = o * lax.rsqrt(ms + EPS) * nw_ref[...] * (zg * jax.nn.sigmoid(zg))

    @pl.when(ni == n_chunks - 1)
    def _():
        for b in range(bb):
            for h in range(GDN_HEADS):
                gi, hh = divmod(h, GDN_GROUP)
                ds = slice(hh * HEAD_DIM, (hh + 1) * HEAD_DIM)
                sfin_ref[b, h] = sbd_ref[b, gi, ds, ds]


def _gdn(c3, hist, z3, ab3, s0, conv_w, alog16, dtb16, nw_group, *, chunk, seq_block):
    nseq, t, _ = c3.shape
    n = nseq * t
    sb, r = (1, GDN_TILE) if t >= GDN_TILE else (GDN_TILE // t, t)
    tiles = t // r
    low = BF16 if chunk >= 16 else F32
    blk = lambda w: pl.BlockSpec((sb, r, w), lambda s, i: (s, i, 0))
    full = lambda a: pl.BlockSpec(a.shape, lambda s, i: (0,) * a.ndim)
    flat = lambda w: pl.BlockSpec((GDN_TILE, w), lambda s, i: (s * tiles + i, 0))
    widths = (GDN_V, GDN_V, GDN_QK, GDN_QK, GDN_HEADS * chunk, 2 * GDN_HEADS)
    dtypes = (F32, low, low, low, low, F32)
    lane_dense = chunk == HEAD_DIM and sb == 1
    u, w, qd, kd, qk, gt = pl.pallas_call(
        _gdn_prep_pair_kernel if lane_dense else functools.partial(_gdn_prep_kernel, chunk=chunk),
        grid=(nseq // sb, tiles),
        in_specs=[blk(GDN_CONV_CH), pl.BlockSpec((sb, CONV_W - 1, GDN_CONV_CH), lambda s, i: (s, 0, 0)),
                  blk(2 * GDN_HEADS), full(conv_w), full(alog16), full(dtb16)],
        out_specs=[flat(wd) for wd in widths],
        out_shape=[jax.ShapeDtypeStruct((n, wd), dt) for wd, dt in zip(widths, dtypes)],
        scratch_shapes=[pltpu.VMEM((sb, 8 + r, GDN_CONV_CH), F32)],
        compiler_params=pltpu.CompilerParams(dimension_semantics=("arbitrary", "arbitrary")),
        name="gdn_prep",
    )(c3, hist, ab3, conv_w, alog16, dtb16)

    n_chunks = t // chunk
    tok = lambda wd: pl.BlockSpec((seq_block, chunk, wd), lambda s, c: (s, c, 0))
    per_seq = pl.BlockSpec((seq_block,) + s0.shape[1:], lambda s, c: (s, 0, 0, 0))
    seq3 = lambda a: a.reshape(nseq, t, a.shape[-1])
    return pl.pallas_call(
        functools.partial(_gdn_scan_kernel, chunk=chunk, n_chunks=n_chunks),
        grid=(nseq // seq_block, n_chunks),
        in_specs=[tok(wd) for wd in widths] + [tok(GDN_V), per_seq,
                                               pl.BlockSpec(nw_group.shape, lambda s, c: (0, 0))],
        out_specs=[tok(GDN_V), per_seq],
        out_shape=[jax.ShapeDtypeStruct((nseq, t, GDN_V), F32), jax.ShapeDtypeStruct(s0.shape, F32)],
        scratch_shapes=[pltpu.VMEM((seq_block, GDN_HEADS // GDN_GROUP, GDN_GROUP_W, GDN_GROUP_W), F32)],
        compiler_params=pltpu.CompilerParams(dimension_semantics=("arbitrary", "arbitrary")),
        name="gdn_scan",
    )(seq3(u), seq3(w), seq3(qd), seq3(kd), seq3(qk), seq3(gt), z3, s0, nw_group)


def _outproj_kernel(x_ref, osw_ref, ogd_ref, wo_ref, gf_ref, wr_ref, br_ref,
                    h_ref, xn_ref, ids_ref, gates_ref, rank_ref, cnt_ref, run_ref):
    i = pl.program_id(0)
    tm = x_ref.shape[0]

    @pl.when(i == 0)
    def _():
        run_ref[...] = jnp.zeros_like(run_ref)

    h = (x_ref[...] + _dot(osw_ref[...].astype(BF16), wo_ref[0:SWA_Q, :])
         + _dot(ogd_ref[...].astype(BF16), wo_ref[SWA_Q:D_MIX, :]))
    h_ref[...] = h
    xn = _rms(h, gf_ref[...])
    xn_ref[...] = xn
    logits = _dot(xn.astype(BF16), wr_ref[...])

    lane = lax.broadcasted_iota(I32, (tm, LANES), 1)
    bias = br_ref[...]
    is_g = (lane >= N_EXPERTS) & (lane < N_EXPERTS + N_GROUPS)
    lg = jnp.where(is_g, logits, NEG_BIG)
    pg = jnp.where(is_g, jnp.exp(lg - jnp.max(lg, axis=-1, keepdims=True)), 0.0)
    group_p = pg / jnp.sum(pg, axis=-1, keepdims=True)
    score_g = jnp.where(is_g, group_p + bias, NEG_BIG)
    g_lane = jnp.min(jnp.where(score_g == jnp.max(score_g, axis=-1, keepdims=True), lane, 2 * LANES),
                     axis=-1, keepdims=True)
    g_w = jnp.sum(jnp.where(lane == g_lane, group_p, 0.0), axis=-1, keepdims=True)
    sel = (lane < N_EXPERTS) & ((lane // EXPERTS_PER_GROUP) == (g_lane - N_EXPERTS))
    le = jnp.where(sel, logits, NEG_BIG)
    pe = jnp.where(sel, jnp.exp(le - jnp.max(le, axis=-1, keepdims=True)), 0.0)
    e_p = pe / jnp.sum(pe, axis=-1, keepdims=True)
    score = jnp.where(sel, e_p + bias, NEG_BIG)
    i1 = jnp.min(jnp.where(score == jnp.max(score, axis=-1, keepdims=True), lane, 2 * LANES),
                 axis=-1, keepdims=True)
    score2 = jnp.where(lane == i1, NEG_BIG, score)
    i2 = jnp.min(jnp.where(score2 == jnp.max(score2, axis=-1, keepdims=True), lane, 2 * LANES),
                 axis=-1, keepdims=True)
    w1 = jnp.sum(jnp.where(lane == i1, e_p, 0.0), axis=-1, keepdims=True)
    w2 = jnp.sum(jnp.where(lane == i2, e_p, 0.0), axis=-1, keepdims=True)
    wsum = w1 + w2

    oh1 = lane == i1
    oh2 = lane == i2
    ohs = (oh1 | oh2).astype(BF16)
    tri = (lax.broadcasted_iota(I32, (tm, tm), 0) > lax.broadcasted_iota(I32, (tm, tm), 1)).astype(BF16)
    before = _dot(tri, ohs) + run_ref[...]
    r1 = jnp.sum(jnp.where(oh1, before, 0.0), axis=-1, keepdims=True)
    r2 = jnp.sum(jnp.where(oh2, before, 0.0), axis=-1, keepdims=True)
    run_ref[...] = run_ref[...] + jnp.sum(ohs.astype(F32), axis=0, keepdims=True)
    cnt_ref[...] = run_ref[...]

    two = lax.broadcasted_iota(I32, (tm, 2), 1)
    ids_ref[...] = jnp.where(two == 0, i1, i2)
    gates_ref[...] = jnp.where(two == 0, g_w * (w1 / wsum), g_w * (w2 / wsum))
    rank_ref[...] = jnp.where(two == 0, r1, r2).astype(I32)


def _outproj(x2d, o_swa, o_gdn, wo_bf16, gain, w_router, b_router):
    n = x2d.shape[0]
    tm = ROW_TILE
    row = lambda w: pl.BlockSpec((tm, w), lambda i: (i, 0))
    full = lambda a: pl.BlockSpec(a.shape, lambda i: (0,) * a.ndim)
    return pl.pallas_call(
        _outproj_kernel,
        grid=(n // tm,),
        in_specs=[row(D_MODEL), row(SWA_Q), row(GDN_V), full(wo_bf16), full(gain), full(w_router), full(b_router)],
        out_specs=[row(D_MODEL), row(D_MODEL), row(2), row(2), row(2), pl.BlockSpec((1, LANES), lambda i: (0, 0))],
        out_shape=[jax.ShapeDtypeStruct((n, D_MODEL), F32), jax.ShapeDtypeStruct((n, D_MODEL), F32),
                   jax.ShapeDtypeStruct((n, 2), I32), jax.ShapeDtypeStruct((n, 2), F32),
                   jax.ShapeDtypeStruct((n, 2), I32), jax.ShapeDtypeStruct((1, LANES), F32)],
        scratch_shapes=[pltpu.VMEM((1, LANES), F32)],
        compiler_params=pltpu.CompilerParams(dimension_semantics=("arbitrary",)),
        name="outproj_router",
    )(x2d, o_swa, o_gdn, wo_bf16, gain, w_router, b_router)


def _row_copy(src_ref, src_row, dst_ref, dst_row, sem):
    return pltpu.make_async_copy(src_ref.at[pl.ds(src_row, 1)], dst_ref.at[pl.ds(dst_row, 1)], sem)


def _scatter_kernel(dest_ref, xp_ref, xs_ref, out_ref, sem, *, tiles_p):
    i = pl.program_id(0)
    tm = xp_ref.shape[0]

    def run(src_ref):
        def issue(g, carry):
            for u in range(DMA_UNROLL):
                r = g * DMA_UNROLL + u
                _row_copy(src_ref, r, out_ref, dest_ref[0, 0, 2 * r], sem).start(priority=0)
                _row_copy(src_ref, r, out_ref, dest_ref[0, 0, 2 * r + 1], sem).start(priority=1)
            return carry

        lax.fori_loop(0, tm // DMA_UNROLL, issue, 0)
        for _ in range(2):
            pltpu.make_async_copy(src_ref, out_ref.at[pl.ds(0, tm)], sem).wait()

    @pl.when(i < tiles_p)
    def _():
        run(xp_ref)

    @pl.when(i >= tiles_p)
    def _():
        run(xs_ref)


def _scatter(dest3, xn_p, xn_s):
    tm = ROW_TILE
    tiles_p, tiles_s = xn_p.shape[0] // tm, xn_s.shape[0] // tm
    rows = 2 * (xn_p.shape[0] + xn_s.shape[0])
    return pl.pallas_call(
        functools.partial(_scatter_kernel, tiles_p=tiles_p),
        grid=(tiles_p + tiles_s,),
        in_specs=[pl.BlockSpec((1, 1, 2 * tm), lambda i: (i, 0, 0), memory_space=pltpu.SMEM),
                  pl.BlockSpec((tm, D_MODEL), lambda i: (jnp.minimum(i, tiles_p - 1), 0)),
                  pl.BlockSpec((tm, D_MODEL), lambda i: (jnp.maximum(i - tiles_p, 0), 0))],
        out_specs=pl.BlockSpec(memory_space=pl.ANY),
        out_shape=jax.ShapeDtypeStruct((rows, D_MODEL), F32),
        scratch_shapes=[pltpu.SemaphoreType.DMA(())],
        compiler_params=pltpu.CompilerParams(dimension_semantics=("arbitrary",)),
        name="scatter_rows",
    )(dest3, xn_p, xn_s)


def _experts_kernel(blk_ref, exp_ref, lo_ref, hi_ref, first_ref, x_ref, wg_ref, wu_ref, wd_ref, y_ref):
    j = pl.program_id(0)
    lo, hi = lo_ref[j], hi_ref[j]

    @pl.when(hi > lo)
    def _():
        x = x_ref[...].astype(BF16)
        gate = _dot(x, wg_ref[0].astype(BF16))
        up = _dot(x, wu_ref[0].astype(BF16))
        hid = (gate * jax.nn.sigmoid(gate)) * up
        y = _dot(hid.astype(BF16), wd_ref[0].astype(BF16))
        r = lax.broadcasted_iota(I32, (MOE_BLOCK, 1), 0)
        mine = (r >= lo) & (r < hi)

        @pl.when(first_ref[j] == 1)
        def _():
            y_ref[...] = jnp.where(mine, y, 0.0)

        @pl.when(first_ref[j] == 0)
        def _():
            y_ref[...] = jnp.where(mine, y, y_ref[...])


def _experts(items, xs, w_gate, w_up, w_down):
    n_items = items[0].shape[0]
    xblk = pl.BlockSpec((MOE_BLOCK, D_MODEL), lambda j, blk, ex, lo, hi, fi: (blk[j], 0))
    wspec = lambda a: pl.BlockSpec((1,) + a.shape[1:], lambda j, blk, ex, lo, hi, fi: (ex[j], 0, 0))
    return pl.pallas_call(
        _experts_kernel,
        grid_spec=pltpu.PrefetchScalarGridSpec(
            num_scalar_prefetch=5,
            grid=(n_items,),
            in_specs=[xblk, wspec(w_gate), wspec(w_up), wspec(w_down)],
            out_specs=xblk),
        out_shape=jax.ShapeDtypeStruct(xs.shape, F32),
        compiler_params=pltpu.CompilerParams(dimension_semantics=("arbitrary",)),
        name="experts",
    )(*items, xs, w_gate, w_up, w_down)


def _work_items(counts, total_rows):
    n_blocks = total_rows // MOE_BLOCK
    n_items = n_blocks + N_EXPERTS
    end = jnp.cumsum(counts)
    start = end - counts
    first_blk = start // MOE_BLOCK
    nb = jnp.where(counts > 0, (end - 1) // MOE_BLOCK - first_blk + 1, 0)
    item_end = jnp.cumsum(nb)
    used = item_end[-1]
    j = jnp.arange(n_items, dtype=I32)
    jj = jnp.minimum(j, used - 1)
    e = jnp.minimum(jnp.sum((item_end[None, :] <= jj[:, None]).astype(I32), axis=1), N_EXPERTS - 1)
    onehot = (e[:, None] == jnp.arange(N_EXPERTS, dtype=I32)[None, :]).astype(I32)
    pick = lambda a: jnp.sum(onehot * a[None, :], axis=1)
    blk = pick(first_blk) + (jj - (pick(item_end) - pick(nb)))
    lo = jnp.maximum(pick(start), blk * MOE_BLOCK) - blk * MOE_BLOCK
    hi = jnp.minimum(pick(end), (blk + 1) * MOE_BLOCK) - blk * MOE_BLOCK
    live = j < used
    lo = jnp.where(live, lo, 0)
    hi = jnp.where(live, hi, 0)
    prev_blk = jnp.concatenate([jnp.full((1,), -1, I32), blk[:-1]])
    first = (live & (blk != prev_blk)).astype(I32)
    return (blk.astype(I32), e.astype(I32), lo.astype(I32), hi.astype(I32), first), start


def _combine_kernel(dest_ref, dnext_ref, gates_ref, h_ref, gf_ref, ys_ref, out_ref, buf_ref, sem, *, n_steps):
    i = pl.program_id(0)
    tm = h_ref.shape[0]
    slot = lax.rem(i, 2)

    def issue(d_ref, s):
        def body(g, carry):
            for u in range(DMA_UNROLL):
                r = g * DMA_UNROLL + u
                _row_copy(ys_ref, d_ref[0, 0, 2 * r], buf_ref.at[s, 0], r, sem.at[s]).start(priority=0)
                _row_copy(ys_ref, d_ref[0, 0, 2 * r + 1], buf_ref.at[s, 1], r, sem.at[s]).start(priority=1)
            return carry

        lax.fori_loop(0, tm // DMA_UNROLL, body, 0)

    @pl.when(i == 0)
    def _():
        issue(dest_ref, 0)

    @pl.when(i + 1 < n_steps)
    def _():
        issue(dnext_ref, 1 - slot)

    for k in range(2):
        pltpu.make_async_copy(ys_ref.at[pl.ds(0, tm)], buf_ref.at[slot, k], sem.at[slot]).wait()
    gates = gates_ref[...]
    y = h_ref[...] + (buf_ref[slot, 0] * gates[:, 0:1] + buf_ref[slot, 1] * gates[:, 1:2])
    out_ref[...] = _rms(y, gf_ref[...])


def _combine(dest3, gates, h, gain, ys):
    n = h.shape[0]
    tm = COMBINE_TILE
    n_steps = n // tm
    row = lambda w: pl.BlockSpec((tm, w), lambda i: (i, 0))
    idx = lambda step: pl.BlockSpec((1, 1, 2 * tm), lambda i: (jnp.minimum(i + step, n_steps - 1), 0, 0),
                                    memory_space=pltpu.SMEM)
    return pl.pallas_call(
        functools.partial(_combine_kernel, n_steps=n_steps),
        grid=(n_steps,),
        in_specs=[idx(0), idx(1), row(2), row(D_MODEL), pl.BlockSpec(gain.shape, lambda i: (0, 0)),
                  pl.BlockSpec(memory_space=pl.ANY)],
        out_specs=row(D_MODEL),
        out_shape=jax.ShapeDtypeStruct((n, D_MODEL), F32),
        scratch_shapes=[pltpu.VMEM((2, 2, tm, D_MODEL), F32), pltpu.SemaphoreType.DMA((2,))],
        compiler_params=pltpu.CompilerParams(dimension_semantics=("arbitrary",)),
        name="combine_norm",
    )(dest3, dest3, gates, h, gain, ys)


def kernel(x_prompt, x_sample, cache_swa_k, cache_swa_v, state_gdn_conv, state_gdn, norm_mix, w_in, swa_sinks,
           gdn_conv_w, gdn_A_log, gdn_dt_bias, gdn_norm_w, w_out, norm_ffn, w_router_group, b_router_group,
           w_router_expert, b_router_expert, w_exp_gate, w_exp_up, w_exp_down, norm_final):
    depth = w_in.shape[0]
    assert depth == 1, "single trunk layer"
    bp, sp, _ = x_prompt.shape
    bs, ts, _ = x_sample.shape
    np_, ns = bp * sp, bs * ts
    l = 0

    w_in_b = w_in[l].astype(BF16)
    w_out_b = w_out[l].astype(BF16)
    g_mix = norm_mix[l].reshape(1, D_MODEL)
    g_ffn = norm_ffn[l].reshape(1, D_MODEL)
    g_fin = norm_final.reshape(1, D_MODEL)
    pad = LANES - N_EXPERTS - N_GROUPS
    w_router = jnp.concatenate([w_router_expert[l], w_router_group[l], jnp.zeros((D_MODEL, pad), F32)],
                               axis=1).astype(BF16)
    b_router = jnp.concatenate([b_router_expert[l].reshape(-1), b_router_group[l], jnp.zeros((pad,), F32)])[None]
    zeros8 = jnp.zeros((GDN_HEADS,), F32)
    alog16 = jnp.concatenate([gdn_A_log[l], zeros8])[None]
    dtb16 = jnp.concatenate([gdn_dt_bias[l], zeros8])[None]
    nw_group = jnp.tile(gdn_norm_w[l].reshape(1, HEAD_DIM), (1, GDN_GROUP))
    sinks = swa_sinks[l]

    q_p, k_p, v_p, c_p, z_p, ab_p = _inproj(x_prompt.reshape(np_, D_MODEL), g_mix, w_in_b)
    nblk = np_ // WINDOW
    k_p3 = k_p.reshape(nblk, WINDOW, SWA_KV)
    v_p3 = v_p.reshape(nblk, WINDOW, SWA_KV)
    (o_swa_p,) = _swa(sinks, q_p.reshape(nblk, WINDOW, SWA_Q), k_p3, v_p3, k_p3, v_p3,
                      bb=1, blocks_per_seq=sp // WINDOW, emit_cache=False)
    c_p3 = c_p.reshape(bp, sp, GDN_CONV_CH)
    o_gdn_p, s_fin_p = _gdn(c_p3, jnp.zeros((bp, CONV_W - 1, GDN_CONV_CH), F32), z_p.reshape(bp, sp, GDN_V),
                            ab_p.reshape(bp, sp, 2 * GDN_HEADS),
                            jnp.zeros((bp, GDN_HEADS, HEAD_DIM, HEAD_DIM), F32),
                            gdn_conv_w[l], alog16, dtb16, nw_group, chunk=GDN_CHUNK, seq_block=bp)
    h_p, xn_p, ids_p, gates_p, rank_p, cnt_p = _outproj(
        x_prompt.reshape(np_, D_MODEL), o_swa_p.reshape(np_, SWA_Q), o_gdn_p.reshape(np_, GDN_V),
        w_out_b, g_ffn, w_router, b_router)

    q_s, k_s, v_s, c_s, z_s, ab_s = _inproj(x_sample.reshape(ns, D_MODEL), g_mix, w_in_b)
    o_swa_s, kcache_s, vcache_s = _swa(
        sinks, q_s.reshape(bs, ts, SWA_Q), k_s.reshape(bs, ts, SWA_KV), v_s.reshape(bs, ts, SWA_KV),
        cache_swa_k[l].reshape(bs, WINDOW, SWA_KV), cache_swa_v[l].reshape(bs, WINDOW, SWA_KV),
        bb=8, blocks_per_seq=None, emit_cache=True)
    c_s3 = c_s.reshape(bs, ts, GDN_CONV_CH)
    o_gdn_s, s_fin_s = _gdn(c_s3, state_gdn_conv[l], z_s.reshape(bs, ts, GDN_V),
                            ab_s.reshape(bs, ts, 2 * GDN_HEADS), state_gdn[l],
                            gdn_conv_w[l], alog16, dtb16, nw_group, chunk=ts, seq_block=8)
    h_s, xn_s, ids_s, gates_s, rank_s, cnt_s = _outproj(
        x_sample.reshape(ns, D_MODEL), o_swa_s.reshape(ns, SWA_Q), o_gdn_s.reshape(ns, GDN_V),
        w_out_b, g_ffn, w_router, b_router)

    cnt_p_i = cnt_p[0, :N_EXPERTS].astype(I32)
    cnt_s_i = cnt_s[0, :N_EXPERTS].astype(I32)
    items, start = _work_items(cnt_p_i + cnt_s_i, 2 * (np_ + ns))
    expert_ids = jnp.arange(N_EXPERTS, dtype=I32)
    lookup = lambda table, ids: jnp.sum(jnp.where(ids[..., None] == expert_ids, table, 0), axis=-1)
    dest_p = lookup(start, ids_p) + rank_p
    dest_s = lookup(start + cnt_p_i, ids_s) + rank_s
    dest_scatter = jnp.concatenate([dest_p.reshape(-1), dest_s.reshape(-1)]).reshape(-1, 1, 2 * ROW_TILE)
    xs = _scatter(dest_scatter, xn_p, xn_s)
    ys = _experts(items, xs, w_exp_gate[l], w_exp_up[l], w_exp_down[l])
    y_p = _combine(dest_p.reshape(-1, 1, 2 * COMBINE_TILE), gates_p, h_p, g_fin, ys)
    y_s = _combine(dest_s.reshape(-1, 1, 2 * COMBINE_TILE), gates_s, h_s, g_fin, ys)

    kv5 = lambda a, b: a.reshape(b, -1, SWA_KV_HEADS, HEAD_DIM)[None]
    return (y_p.reshape(bp, sp, D_MODEL), y_s.reshape(bs, ts, D_MODEL),
            kv5(k_p.reshape(bp, sp, SWA_KV)[:, -WINDOW:], bp), kv5(v_p.reshape(bp, sp, SWA_KV)[:, -WINDOW:], bp),
            kv5(kcache_s, bs), kv5(vcache_s, bs),
            c_p3[:, -(CONV_W - 1):][None], c_s3[:, -(CONV_W - 1):][None],
            s_fin_p[None], s_fin_s[None])
```

```python
import functools

import jax
import jax.numpy as jnp
from jax import lax
from jax.experimental import pallas as pl
from jax.experimental.pallas import tpu as pltpu

F32 = jnp.float32
BF16 = jnp.bfloat16
I32 = jnp.int32

D_MODEL = 1024
HEAD_DIM = 64
SWA_HEADS = 8
GDN_HEADS = 8
SWA_KV_HEADS = 2
GQA_GROUP = SWA_HEADS // SWA_KV_HEADS
WINDOW = 128
ATTN_SCALE = HEAD_DIM ** -0.5
CONV_W = 4
N_GROUPS = 8
EXPERTS_PER_GROUP = 8
N_EXPERTS = 64
D_EXPERT = 256
EPS = 1e-6

SWA_Q = SWA_HEADS * HEAD_DIM
SWA_KV = SWA_KV_HEADS * HEAD_DIM
GDN_QK = GDN_HEADS * HEAD_DIM
GDN_V = GDN_HEADS * HEAD_DIM
GDN_CONV_CH = 2 * GDN_QK + GDN_V
D_MIX = SWA_Q + GDN_V
D_IN = SWA_Q + 2 * SWA_KV + GDN_CONV_CH + GDN_V + 2 * GDN_HEADS
COL_K = SWA_Q
COL_V = COL_K + SWA_KV
COL_C = COL_V + SWA_KV
COL_Z = COL_C + GDN_CONV_CH
COL_AB = COL_Z + GDN_V

LANES = 128
NEG_BIG = -1e30
ROW_TILE = 512
MOE_BLOCK = 256
COMBINE_TILE = 256
DMA_UNROLL = 8
GDN_CHUNK = 64
GDN_TILE = 256
GDN_GROUP = 4
GDN_GROUP_W = GDN_GROUP * HEAD_DIM


def _rms(x, g):
    return x * lax.rsqrt(jnp.mean(x * x, axis=-1, keepdims=True) + EPS) * g


def _dot(a, b):
    return jnp.dot(a, b, preferred_element_type=F32)


def _dot_nt(a, b):
    return lax.dot_general(a, b, (((1,), (1,)), ((), ())), preferred_element_type=F32)


def _dot_tn(a, b):
    return lax.dot_general(a, b, (((0,), (0,)), ((), ())), preferred_element_type=F32)


def _split3(x):
    p1 = x.astype(BF16).astype(F32)
    r = x - p1
    p2 = r.astype(BF16).astype(F32)
    p3 = (r - p2).astype(BF16).astype(F32)
    return p1, p2, p3


def _inproj_kernel(x_ref, g_ref, w_ref, q_ref, k_ref, v_ref, c_ref, z_ref, ab_ref):
    x = x_ref[...]
    xb = _rms(x, g_ref[...]).astype(BF16)
    q_ref[...] = _dot(xb, w_ref[:, 0:COL_K])
    k_ref[...] = _dot(xb, w_ref[:, COL_K:COL_V])
    v_ref[...] = _dot(xb, w_ref[:, COL_V:COL_C])
    c_ref[...] = _dot(xb, w_ref[:, COL_C:COL_Z])
    z_ref[...] = _dot(xb, w_ref[:, COL_Z:COL_AB])
    ab_ref[...] = _dot(xb, w_ref[:, COL_AB:D_IN])


def _inproj(x2d, gain, w_bf16):
    n = x2d.shape[0]
    tm = ROW_TILE
    row = lambda w: pl.BlockSpec((tm, w), lambda i: (i, 0))
    full = lambda a: pl.BlockSpec(a.shape, lambda i: (0,) * a.ndim)
    widths = (SWA_Q, SWA_KV, SWA_KV, GDN_CONV_CH, GDN_V, 2 * GDN_HEADS)
    return pl.pallas_call(
        _inproj_kernel,
        grid=(n // tm,),
        in_specs=[row(D_MODEL), full(gain), full(w_bf16)],
        out_specs=[row(w) for w in widths],
        out_shape=[jax.ShapeDtypeStruct((n, w), F32) for w in widths],
        compiler_params=pltpu.CompilerParams(dimension_semantics=("arbitrary",)),
        name="inproj",
    )(x2d, gain, w_bf16)


def _swa_kernel(sink_ref, q_ref, kc_ref, vc_ref, kp_ref, vp_ref, o_ref, *cache_refs,
                bb, t, blocks_per_seq, emit_cache):
    if blocks_per_seq is None:
        has_prev = None
    else:
        has_prev = lax.rem(pl.program_id(0), blocks_per_seq) != 0
    rows = GQA_GROUP * t
    ri = lax.broadcasted_iota(I32, (rows, 1), 0)
    qi = lax.rem(ri, t)
    gi = ri // t
    for b in range(bb):
        out_pieces = []
        for h in range(SWA_KV_HEADS):
            hs = slice(h * HEAD_DIM, (h + 1) * HEAD_DIM)
            kp, vp = kp_ref[b, :, hs], vp_ref[b, :, hs]
            kc, vc = kc_ref[b, :, hs], vc_ref[b, :, hs]
            if emit_cache:
                keys = jnp.concatenate([kp[t:], kc], axis=0)
                vals = jnp.concatenate([vp[t:], vc], axis=0)
                cache_refs[0][b, :, hs] = keys
                cache_refs[1][b, :, hs] = vals
                kj = lax.broadcasted_iota(I32, (rows, WINDOW), 1)
                mask = kj <= qi + (WINDOW - t)
            else:
                keys = jnp.concatenate([kp, kc], axis=0)
                vals = jnp.concatenate([vp, vc], axis=0)
                kj = lax.broadcasted_iota(I32, (rows, WINDOW + t), 1)
                mask = (kj <= qi + WINDOW) & (kj > qi)
                if has_prev is not None:
                    mask = mask & (has_prev | (kj >= WINDOW))
            q4 = jnp.concatenate(
                [q_ref[b, :, (GQA_GROUP * h + g) * HEAD_DIM:(GQA_GROUP * h + g + 1) * HEAD_DIM]
                 for g in range(GQA_GROUP)], axis=0)
            s = _dot_nt(q4.astype(BF16), keys.astype(BF16)) * ATTN_SCALE
            s = jnp.where(mask, s, NEG_BIG)
            sink = jnp.zeros((rows, 1), F32)
            for g in range(GQA_GROUP):
                sink = jnp.where(gi == g, sink_ref[GQA_GROUP * h + g], sink)
            m = jnp.maximum(jnp.max(s, axis=-1, keepdims=True), sink)
            p = jnp.exp(s - m)
            den = jnp.sum(p, axis=-1, keepdims=True) + jnp.exp(sink - m)
            o4 = _dot(p.astype(BF16), vals.astype(BF16)) / den
            out_pieces += [o4[g * t:(g + 1) * t] for g in range(GQA_GROUP)]
        o_ref[b] = jnp.concatenate(out_pieces, axis=-1)


def _swa(sinks, q3, k3, v3, kprev3, vprev3, *, bb, blocks_per_seq, emit_cache):
    nb, t, _ = q3.shape
    cur = lambda w: pl.BlockSpec((bb, t, w), lambda i: (i, 0, 0))
    if blocks_per_seq is None:
        prev = pl.BlockSpec((bb, WINDOW, SWA_KV), lambda i: (i, 0, 0))
    else:
        prev = pl.BlockSpec((bb, WINDOW, SWA_KV), lambda i: (jnp.maximum(i - 1, 0), 0, 0))
    out_specs = [cur(SWA_Q)]
    out_shape = [jax.ShapeDtypeStruct((nb, t, SWA_Q), F32)]
    if emit_cache:
        cache = pl.BlockSpec((bb, WINDOW, SWA_KV), lambda i: (i, 0, 0))
        out_specs += [cache, cache]
        out_shape += [jax.ShapeDtypeStruct((nb, WINDOW, SWA_KV), F32)] * 2
    return pl.pallas_call(
        functools.partial(_swa_kernel, bb=bb, t=t, blocks_per_seq=blocks_per_seq, emit_cache=emit_cache),
        grid=(nb // bb,),
        in_specs=[pl.BlockSpec(memory_space=pltpu.SMEM), cur(SWA_Q), cur(SWA_KV), cur(SWA_KV), prev, prev],
        out_specs=out_specs,
        out_shape=out_shape,
        compiler_params=pltpu.CompilerParams(dimension_semantics=("arbitrary",)),
        name="swa_cache" if emit_cache else "swa_band",
    )(sinks, q3, k3, v3, kprev3, vprev3)


def _gdn_prep_kernel(c_ref, hist_ref, ab_ref, cw_ref, alog_ref, dtb_ref,
                     u_ref, w_ref, qd_ref, kd_ref, qk_ref, gt_ref, cbuf_ref, *, chunk):
    sb, r, _ = c_ref.shape
    tp = sb * r
    cn = chunk
    low = w_ref.dtype
    hist_rows = CONV_W - 1

    @pl.when(pl.program_id(1) == 0)
    def _():
        cbuf_ref[:, 8 - hist_rows:8, :] = hist_ref[...]

    cbuf_ref[:, 8:8 + r, :] = c_ref[...]
    conv = cbuf_ref[:, 8 - hist_rows:8 - hist_rows + r, :] * cw_ref[0:1, :]
    for i in range(1, CONV_W):
        conv = conv + cbuf_ref[:, 8 - hist_rows + i:8 - hist_rows + i + r, :] * cw_ref[i:i + 1, :]
    tail = cbuf_ref[:, 8 + r - hist_rows:8 + r, :]
    cbuf_ref[:, 8 - hist_rows:8, :] = tail
    conv = (conv * jax.nn.sigmoid(conv)).reshape(tp, GDN_CONV_CH)

    ab = ab_ref[...].reshape(tp, 2 * GDN_HEADS)
    is_g = lax.broadcasted_iota(I32, (1, 2 * GDN_HEADS), 1) < GDN_HEADS
    g = jnp.where(is_g, -jnp.exp(alog_ref[...]) * jax.nn.softplus(ab + dtb_ref[...]), 0.0)
    beta = jax.nn.sigmoid(ab)

    ri = lax.broadcasted_iota(I32, (tp, tp), 0)
    ci = lax.broadcasted_iota(I32, (tp, tp), 1)
    same = (ri // cn) == (ci // cn)
    causal = same & (ri >= ci)
    strict = same & (ri > ci)
    eye = (ri == ci).astype(F32)
    stack = jnp.concatenate([causal.astype(BF16), same.astype(BF16)], axis=0)
    both = sum(_dot(stack, p.astype(BF16)) for p in _split3(g))
    gc, gl = both[:tp], both[tp:]
    e16 = lax.broadcasted_iota(I32, (2 * GDN_HEADS, 2 * GDN_HEADS), 0)
    eye16 = (e16 == lax.broadcasted_iota(I32, (2 * GDN_HEADS, 2 * GDN_HEADS), 1)).astype(BF16)
    gc_row = sum(_dot_nt(eye16, p.astype(BF16)) for p in _split3(gc))
    gt_ref[...] = jnp.exp(gl)
    fold = (lax.broadcasted_iota(I32, (tp, cn), 0) % cn == lax.broadcasted_iota(I32, (tp, cn), 1)).astype(low)

    levels = cn.bit_length() - 2
    us, ws, qds, kds, qks = [], [], [], [], []
    for h in range(GDN_HEADS):
        q = conv[:, h * HEAD_DIM:(h + 1) * HEAD_DIM]
        k = conv[:, GDN_QK + h * HEAD_DIM:GDN_QK + (h + 1) * HEAD_DIM]
        v = conv[:, 2 * GDN_QK + h * HEAD_DIM:2 * GDN_QK + (h + 1) * HEAD_DIM]
        q = q * lax.rsqrt(jnp.sum(q * q, axis=-1, keepdims=True) + EPS) * (HEAD_DIM ** -0.5)
        k = k * lax.rsqrt(jnp.sum(k * k, axis=-1, keepdims=True) + EPS)
        b_h = beta[:, GDN_HEADS + h:GDN_HEADS + h + 1]
        g_c = gc[:, h:h + 1]
        decay = jnp.exp(jnp.where(causal, g_c - gc_row[h:h + 1, :], NEG_BIG))
        eg = jnp.exp(g_c)
        kb = k * b_h
        k_l = k.astype(low)
        lmat = jnp.where(strict, _dot_nt(kb.astype(low), k_l) * decay, 0.0)
        qk = _dot_nt(q.astype(low), k_l) * decay
        qks.append(_dot(qk.astype(low), fold))
        l_l = lmat.astype(low)
        x = eye - lmat
        p = _dot(l_l, l_l)
        for lev in range(1, levels + 1):
            p_l = p.astype(low)
            if lev < levels:
                xp = _dot(jnp.concatenate([x.astype(low), p_l], axis=0), p_l)
                x = x + xp[:tp]
                p = xp[tp:]
            else:
                x = x + _dot(x.astype(low), p_l)
        rhs = jnp.concatenate([v * b_h, kb * eg], axis=-1)
        sol = _dot(x.astype(low), rhs.astype(low))
        us.append(sol[:, :HEAD_DIM])
        ws.append(sol[:, HEAD_DIM:])
        qds.append(q * eg)
        kds.append(k * jnp.exp(gl[:, h:h + 1] - g_c))
    u_ref[...] = jnp.concatenate(us, axis=-1)
    w_ref[...] = jnp.concatenate(ws, axis=-1).astype(low)
    qd_ref[...] = jnp.concatenate(qds, axis=-1).astype(low)
    kd_ref[...] = jnp.concatenate(kds, axis=-1).astype(low)
    qk_ref[...] = jnp.concatenate(qks, axis=-1).astype(low)


def _gdn_prep_pair_kernel(c_ref, hist_ref, ab_ref, cw_ref, alog_ref, dtb_ref,
                          u_ref, w_ref, qd_ref, kd_ref, qk_ref, gt_ref, cbuf_ref):
    _, r, _ = c_ref.shape
    tp = r
    cn = HEAD_DIM
    pair_w = 2 * HEAD_DIM
    hist_rows = CONV_W - 1

    @pl.when(pl.program_id(1) == 0)
    def _():
        cbuf_ref[:, 8 - hist_rows:8, :] = hist_ref[...]

    cbuf_ref[:, 8:8 + r, :] = c_ref[...]
    conv = cbuf_ref[:, 8 - hist_rows:8 - hist_rows + r, :] * cw_ref[0:1, :]
    for i in range(1, CONV_W):
        conv = conv + cbuf_ref[:, 8 - hist_rows + i:8 - hist_rows + i + r, :] * cw_ref[i:i + 1, :]
    tail = cbuf_ref[:, 8 + r - hist_rows:8 + r, :]
    cbuf_ref[:, 8 - hist_rows:8, :] = tail
    conv = (conv * jax.nn.sigmoid(conv)).reshape(tp, GDN_CONV_CH)

    ab = ab_ref[...].reshape(tp, 2 * GDN_HEADS)
    is_g = lax.broadcasted_iota(I32, (1, 2 * GDN_HEADS), 1) < GDN_HEADS
    g = jnp.where(is_g, -jnp.exp(alog_ref[...]) * jax.nn.softplus(ab + dtb_ref[...]), 0.0)
    beta = jax.nn.sigmoid(ab)

    ri = lax.broadcasted_iota(I32, (tp, tp), 0)
    ci = lax.broadcasted_iota(I32, (tp, tp), 1)
    same = (ri // cn) == (ci // cn)
    same_l = same.astype(BF16)
    stack = jnp.concatenate([(same & (ri >= ci)).astype(BF16), same_l], axis=0)
    both = sum(_dot(stack, p.astype(BF16)) for p in _split3(g))
    gc, gl = both[:tp], both[tp:]
    gt_ref[...] = jnp.exp(gl)

    lane = lax.broadcasted_iota(I32, (tp, pair_w), 1)
    c_in = lax.broadcasted_iota(I32, (tp, pair_w), 0) % cn
    j_in = lane % cn
    left = lane < cn
    causal = c_in >= j_in
    strict = c_in > j_in
    diag = c_in == j_in
    eye = diag.astype(F32)
    bdmask = ((lax.broadcasted_iota(I32, (pair_w, pair_w), 0) // cn)
              == (lax.broadcasted_iota(I32, (pair_w, pair_w), 1) // cn))
    ones_bd = bdmask.astype(BF16)

    def bd(m):
        return jnp.where(bdmask, jnp.concatenate([m, m], axis=0), 0.0).astype(BF16)

    def head_sum(x):
        hi = x.astype(BF16)
        lo = (x - hi.astype(F32)).astype(BF16)
        return _dot(hi, ones_bd) + _dot(lo, ones_bd)

    levels = cn.bit_length() - 2
    chunks = [slice(n * cn, (n + 1) * cn) for n in range(tp // cn)]
    for p in range(GDN_HEADS // 2):
        a, b = 2 * p, 2 * p + 1
        ls = slice(p * pair_w, (p + 1) * pair_w)
        q = conv[:, p * pair_w:(p + 1) * pair_w]
        k = conv[:, GDN_QK + p * pair_w:GDN_QK + (p + 1) * pair_w]
        v = conv[:, 2 * GDN_QK + p * pair_w:2 * GDN_QK + (p + 1) * pair_w]
        q = q * lax.rsqrt(head_sum(q * q) + EPS) * (HEAD_DIM ** -0.5)
        k = k * lax.rsqrt(head_sum(k * k) + EPS)
        pick = lambda m, off: jnp.where(left, m[:, off + a:off + a + 1], m[:, off + b:off + b + 1])
        gcp, glp, bp = pick(gc, 0), pick(gl, 0), pick(beta, GDN_HEADS)
        rowm = sum(_dot(same_l, part.astype(BF16)) for part in _split3(jnp.where(diag, gcp, 0.0)))
        decay = jnp.exp(jnp.where(causal, gcp - rowm, NEG_BIG))
        eg = jnp.exp(gcp)
        kb = k * bp
        vb = v * bp
        kbeg = kb * eg
        qd_ref[:, ls] = (q * eg).astype(BF16)
        kd_ref[:, ls] = (k * jnp.exp(glp - gcp)).astype(BF16)
        q_l, kb_l = q.astype(BF16), kb.astype(BF16)
        kbd = [bd(k[rs]) for rs in chunks]
        kk = jnp.concatenate([_dot_nt(kb_l[rs], kbd[n]) for n, rs in enumerate(chunks)], axis=0)
        qk = jnp.concatenate([_dot_nt(q_l[rs], kbd[n]) for n, rs in enumerate(chunks)], axis=0)
        qk_ref[:, ls] = (qk * decay).astype(BF16)
        lmat = jnp.where(strict, kk * decay, 0.0)
        xs = [eye[rs] - lmat[rs] for rs in chunks]
        ps = [_dot(lmat[rs].astype(BF16), bd(lmat[rs])) for rs in chunks]
        for lev in range(1, levels + 1):
            pbd = [bd(pm) for pm in ps]
            if lev < levels:
                xp = [_dot(jnp.concatenate([xm, pm], axis=0).astype(BF16), wm) for xm, pm, wm in zip(xs, ps, pbd)]
                xs = [xm + m[:cn] for xm, m in zip(xs, xp)]
                ps = [m[cn:] for m in xp]
            else:
                xs = [xm + _dot(xm.astype(BF16), wm) for xm, wm in zip(xs, pbd)]
        x_l = [xm.astype(BF16) for xm in xs]
        u_ref[:, ls] = jnp.concatenate([_dot(xm, bd(vb[rs])) for xm, rs in zip(x_l, chunks)], axis=0)
        w_ref[:, ls] = jnp.concatenate([_dot(xm, bd(kbeg[rs])) for xm, rs in zip(x_l, chunks)], axis=0).astype(BF16)


def _gdn_scan_kernel(u_ref, w_ref, qd_ref, kd_ref, qk_ref, gt_ref, z_ref, s0_ref, nw_ref,
                     o_ref, sfin_ref, sbd_ref, *, chunk, n_chunks):
    bb = u_ref.shape[0]
    cn = chunk
    low = w_ref.dtype
    gw = GDN_GROUP_W
    ni = pl.program_id(1)

    @pl.when(ni == 0)
    def _():
        sbd_ref[...] = jnp.zeros_like(sbd_ref)
        for b in range(bb):
            for h in range(GDN_HEADS):
                gi, hh = divmod(h, GDN_GROUP)
                ds = slice(hh * HEAD_DIM, (hh + 1) * HEAD_DIM)
                sbd_ref[b, gi, ds, ds] = s0_ref[b, h]

    bdmask = ((lax.broadcasted_iota(I32, (gw, gw), 0) // HEAD_DIM)
              == (lax.broadcasted_iota(I32, (gw, gw), 1) // HEAD_DIM))
    ones_bd = bdmask.astype(BF16)
    vmask = ((lax.broadcasted_iota(I32, (GDN_GROUP * cn, gw), 0) // cn)
             == (lax.broadcasted_iota(I32, (GDN_GROUP * cn, gw), 1) // HEAD_DIM))
    e_row = lax.broadcasted_iota(I32, (2 * GDN_HEADS, gw), 0)
    e_col = lax.broadcasted_iota(I32, (2 * GDN_HEADS, gw), 1) // HEAD_DIM
    for b in range(bb):
        gt_parts = _split3(gt_ref[b, 0:8, :])
        for gi in range(GDN_HEADS // GDN_GROUP):
            ls = slice(gi * gw, (gi + 1) * gw)
            s = sbd_ref[b, gi]
            s_l = s.astype(low)
            v_new = u_ref[b, :, ls] - _dot(w_ref[b, :, ls], s_l)
            v_l = v_new.astype(low)
            vbd = jnp.where(vmask, jnp.concatenate([v_l] * GDN_GROUP, axis=0), jnp.zeros((), low))
            qk_g = qk_ref[b, :, gi * GDN_GROUP * cn:(gi + 1) * GDN_GROUP * cn]
            o = _dot(qd_ref[b, :, ls], s_l) + _dot(qk_g, vbd)
            upd = _dot_tn(kd_ref[b, :, ls], v_l)
            expand = (e_row == e_col + gi * GDN_GROUP).astype(BF16)
            gte = sum(_dot(p.astype(BF16), expand) for p in gt_parts)[0:1]
            sbd_ref[b, gi] = s * gte + jnp.where(bdmask, upd, 0.0)
            o2 = o * o
            hi = o2.astype(BF16)
            lo = (o2 - hi.astype(F32)).astype(BF16)
            ms = (_dot(hi, ones_bd) + _dot(lo, ones_bd)) * (1.0 / HEAD_DIM)
            zg = z_ref[b, :, ls]
            o_ref[b, :, ls] = o * lax.rsqrt(ms + EPS) * nw_ref[...] * (zg * jax.nn.sigmoid(zg))

    @pl.when(ni == n_chunks - 1)
    def _():
        for b in range(bb):
            for h in range(GDN_HEADS):
                gi, hh = divmod(h, GDN_GROUP)
                ds = slice(hh * HEAD_DIM, (hh + 1) * HEAD_DIM)
                sfin_ref[b, h] = sbd_ref[b, gi, ds, ds]


def _gdn(c3, hist, z3, ab3, s0, conv_w, alog16, dtb16, nw_group, *, chunk, seq_block):
    nseq, t, _ = c3.shape
    n = nseq * t
    sb, r = (1, GDN_TILE) if t >= GDN_TILE else (GDN_TILE // t, t)
    tiles = t // r
    low = BF16 if chunk >= 16 else F32
    blk = lambda w: pl.BlockSpec((sb, r, w), lambda s, i: (s, i, 0))
    full = lambda a: pl.BlockSpec(a.shape, lambda s, i: (0,) * a.ndim)
    flat = lambda w: pl.BlockSpec((GDN_TILE, w), lambda s, i: (s * tiles + i, 0))
    widths = (GDN_V, GDN_V, GDN_QK, GDN_QK, GDN_HEADS * chunk, 2 * GDN_HEADS)
    dtypes = (F32, low, low, low, low, F32)
    lane_dense = chunk == HEAD_DIM and sb == 1
    u, w, qd, kd, qk, gt = pl.pallas_call(
        _gdn_prep_pair_kernel if lane_dense else functools.partial(_gdn_prep_kernel, chunk=chunk),
        grid=(nseq // sb, tiles),
        in_specs=[blk(GDN_CONV_CH), pl.BlockSpec((sb, CONV_W - 1, GDN_CONV_CH), lambda s, i: (s, 0, 0)),
                  blk(2 * GDN_HEADS), full(conv_w), full(alog16), full(dtb16)],
        out_specs=[flat(wd) for wd in widths],
        out_shape=[jax.ShapeDtypeStruct((n, wd), dt) for wd, dt in zip(widths, dtypes)],
        scratch_shapes=[pltpu.VMEM((sb, 8 + r, GDN_CONV_CH), F32)],
        compiler_params=pltpu.CompilerParams(dimension_semantics=("arbitrary", "arbitrary")),
        name="gdn_prep",
    )(c3, hist, ab3, conv_w, alog16, dtb16)

    n_chunks = t // chunk
    tok = lambda wd: pl.BlockSpec((seq_block, chunk, wd), lambda s, c: (s, c, 0))
    per_seq = pl.BlockSpec((seq_block,) + s0.shape[1:], lambda s, c: (s, 0, 0, 0))
    seq3 = lambda a: a.reshape(nseq, t, a.shape[-1])
    return pl.pallas_call(
        functools.partial(_gdn_scan_kernel, chunk=chunk, n_chunks=n_chunks),
        grid=(nseq // seq_block, n_chunks),
        in_specs=[tok(wd) for wd in widths] + [tok(GDN_V), per_seq,
                                               pl.BlockSpec(nw_group.shape, lambda s, c: (0, 0))],
        out_specs=[tok(GDN_V), per_seq],
        out_shape=[jax.ShapeDtypeStruct((nseq, t, GDN_V), F32), jax.ShapeDtypeStruct(s0.shape, F32)],
        scratch_shapes=[pltpu.VMEM((seq_block, GDN_HEADS // GDN_GROUP, GDN_GROUP_W, GDN_GROUP_W), F32)],
        compiler_params=pltpu.CompilerParams(dimension_semantics=("arbitrary", "arbitrary")),
        name="gdn_scan",
    )(seq3(u), seq3(w), seq3(qd), seq3(kd), seq3(qk), seq3(gt), z3, s0, nw_group)


def _outproj_kernel(x_ref, osw_ref, ogd_ref, wo_ref, gf_ref, wr_ref, br_ref,
                    h_ref, xn_ref, gates_ref, meta_ref, cnt_ref, run_ref):
    i = pl.program_id(0)
    tm = x_ref.shape[0]

    @pl.when(i == 0)
    def _():
        run_ref[...] = jnp.zeros_like(run_ref)

    h = (x_ref[...] + _dot(osw_ref[...].astype(BF16), wo_ref[0:SWA_Q, :])
         + _dot(ogd_ref[...].astype(BF16), wo_ref[SWA_Q:D_MIX, :]))
    h_ref[...] = h
    xn = _rms(h, gf_ref[...])
    xn_ref[...] = xn
    logits = _dot(xn.astype(BF16), wr_ref[...])

    lane = lax.broadcasted_iota(I32, (tm, LANES), 1)
    bias = br_ref[...]
    is_g = (lane >= N_EXPERTS) & (lane < N_EXPERTS + N_GROUPS)
    lg = jnp.where(is_g, logits, NEG_BIG)
    pg = jnp.where(is_g, jnp.exp(lg - jnp.max(lg, axis=-1, keepdims=True)), 0.0)
    group_p = pg / jnp.sum(pg, axis=-1, keepdims=True)
    score_g = jnp.where(is_g, group_p + bias, NEG_BIG)
    g_lane = jnp.min(jnp.where(score_g == jnp.max(score_g, axis=-1, keepdims=True), lane, 2 * LANES),
                     axis=-1, keepdims=True)
    g_w = jnp.sum(jnp.where(lane == g_lane, group_p, 0.0), axis=-1, keepdims=True)
    sel = (lane < N_EXPERTS) & ((lane // EXPERTS_PER_GROUP) == (g_lane - N_EXPERTS))
    le = jnp.where(sel, logits, NEG_BIG)
    pe = jnp.where(sel, jnp.exp(le - jnp.max(le, axis=-1, keepdims=True)), 0.0)
    e_p = pe / jnp.sum(pe, axis=-1, keepdims=True)
    score = jnp.where(sel, e_p + bias, NEG_BIG)
    i1 = jnp.min(jnp.where(score == jnp.max(score, axis=-1, keepdims=True), lane, 2 * LANES),
                 axis=-1, keepdims=True)
    score2 = jnp.where(lane == i1, NEG_BIG, score)
    i2 = jnp.min(jnp.where(score2 == jnp.max(score2, axis=-1, keepdims=True), lane, 2 * LANES),
                 axis=-1, keepdims=True)
    w1 = jnp.sum(jnp.where(lane == i1, e_p, 0.0), axis=-1, keepdims=True)
    w2 = jnp.sum(jnp.where(lane == i2, e_p, 0.0), axis=-1, keepdims=True)
    wsum = w1 + w2

    oh1 = lane == i1
    oh2 = lane == i2
    ohs = (oh1 | oh2).astype(BF16)
    tri = (lax.broadcasted_iota(I32, (tm, tm), 0) > lax.broadcasted_iota(I32, (tm, tm), 1)).astype(BF16)
    before = _dot(tri, ohs) + run_ref[...]
    r1 = jnp.sum(jnp.where(oh1, before, 0.0), axis=-1, keepdims=True)
    r2 = jnp.sum(jnp.where(oh2, before, 0.0), axis=-1, keepdims=True)
    run_ref[...] = run_ref[...] + jnp.sum(ohs.astype(F32), axis=0, keepdims=True)
    cnt_ref[...] = run_ref[...]

    two = lax.broadcasted_iota(I32, (tm, 2), 1)
    gates_ref[...] = jnp.where(two == 0, g_w * (w1 / wsum), g_w * (w2 / wsum))
    packed = jnp.where(lane == 0, i1.astype(F32), jnp.where(lane == 1, i2.astype(F32),
                       jnp.where(lane == 2, r1, jnp.where(lane == 3, r2, 0.0))))
    eye8 = (lax.broadcasted_iota(I32, (8, LANES), 0) == lax.broadcasted_iota(I32, (8, LANES), 1)).astype(BF16)
    meta_ref[...] = sum(_dot_nt(eye8, part.astype(BF16)) for part in _split3(packed))


def _outproj(x2d, o_swa, o_gdn, wo_bf16, gain, w_router, b_router):
    n = x2d.shape[0]
    tm = ROW_TILE
    row = lambda w: pl.BlockSpec((tm, w), lambda i: (i, 0))
    full = lambda a: pl.BlockSpec(a.shape, lambda i: (0,) * a.ndim)
    return pl.pallas_call(
        _outproj_kernel,
        grid=(n // tm,),
        in_specs=[row(D_MODEL), row(SWA_Q), row(GDN_V), full(wo_bf16), full(gain), full(w_router), full(b_router)],
        out_specs=[row(D_MODEL), row(D_MODEL), row(2), pl.BlockSpec((8, tm), lambda i: (0, i)),
                   pl.BlockSpec((1, LANES), lambda i: (0, 0))],
        out_shape=[jax.ShapeDtypeStruct((n, D_MODEL), F32), jax.ShapeDtypeStruct((n, D_MODEL), F32),
                   jax.ShapeDtypeStruct((n, 2), F32), jax.ShapeDtypeStruct((8, n), F32),
                   jax.ShapeDtypeStruct((1, LANES), F32)],
        scratch_shapes=[pltpu.VMEM((1, LANES), F32)],
        compiler_params=pltpu.CompilerParams(dimension_semantics=("arbitrary",)),
        name="outproj_router",
    )(x2d, o_swa, o_gdn, wo_bf16, gain, w_router, b_router)


def _row_copy(src_ref, src_row, dst_ref, dst_row, sem):
    return pltpu.make_async_copy(src_ref.at[pl.ds(src_row, 1)], dst_ref.at[pl.ds(dst_row, 1)], sem)


def _scatter_kernel(dest_ref, xp_ref, xs_ref, out_ref, sem, *, tiles_p):
    i = pl.program_id(0)
    tm = xp_ref.shape[0]

    def run(src_ref):
        def issue(g, carry):
            for u in range(DMA_UNROLL):
                r = g * DMA_UNROLL + u
                _row_copy(src_ref, r, out_ref, dest_ref[0, r], sem).start(priority=0)
                _row_copy(src_ref, r, out_ref, dest_ref[1, r], sem).start(priority=1)
            return carry

        lax.fori_loop(0, tm // DMA_UNROLL, issue, 0)
        for _ in range(2):
            pltpu.make_async_copy(src_ref, out_ref.at[pl.ds(0, tm)], sem).wait()

    @pl.when(i < tiles_p)
    def _():
        run(xp_ref)

    @pl.when(i >= tiles_p)
    def _():
        run(xs_ref)


def _scatter(dest3, xn_p, xn_s):
    tm = ROW_TILE
    tiles_p, tiles_s = xn_p.shape[0] // tm, xn_s.shape[0] // tm
    rows = 2 * (xn_p.shape[0] + xn_s.shape[0])
    return pl.pallas_call(
        functools.partial(_scatter_kernel, tiles_p=tiles_p),
        grid=(tiles_p + tiles_s,),
        in_specs=[pl.BlockSpec((2, tm), lambda i: (0, i), memory_space=pltpu.SMEM),
                  pl.BlockSpec((tm, D_MODEL), lambda i: (jnp.minimum(i, tiles_p - 1), 0)),
                  pl.BlockSpec((tm, D_MODEL), lambda i: (jnp.maximum(i - tiles_p, 0), 0))],
        out_specs=pl.BlockSpec(memory_space=pl.ANY),
        out_shape=jax.ShapeDtypeStruct((rows, D_MODEL), F32),
        scratch_shapes=[pltpu.SemaphoreType.DMA(())],
        compiler_params=pltpu.CompilerParams(dimension_semantics=("arbitrary",)),
        name="scatter_rows",
    )(dest3, xn_p, xn_s)


def _experts_kernel(blk_ref, exp_ref, lo_ref, hi_ref, first_ref, fresh_ref, x_ref, wg_ref, wu_ref, wd_ref, y_ref,
                    wg_l, wu_l, wd_l):
    j = pl.program_id(0)
    lo, hi = lo_ref[j], hi_ref[j]

    @pl.when(fresh_ref[j] == 1)
    def _():
        wg_l[...] = wg_ref[0].astype(BF16)
        wu_l[...] = wu_ref[0].astype(BF16)
        wd_l[...] = wd_ref[0].astype(BF16)

    @pl.when(hi > lo)
    def _():
        x = x_ref[...].astype(BF16)
        gate = _dot(x, wg_l[...])
        up = _dot(x, wu_l[...])
        hid = (gate * jax.nn.sigmoid(gate)) * up
        y = _dot(hid.astype(BF16), wd_l[...])
        r = lax.broadcasted_iota(I32, (MOE_BLOCK, 1), 0)
        mine = (r >= lo) & (r < hi)

        @pl.when(first_ref[j] == 1)
        def _():
            y_ref[...] = jnp.where(mine, y, 0.0)

        @pl.when(first_ref[j] == 0)
        def _():
            y_ref[...] = jnp.where(mine, y, y_ref[...])


def _experts(items, xs, w_gate, w_up, w_down):
    n_items = items[0].shape[0]
    xblk = pl.BlockSpec((MOE_BLOCK, D_MODEL), lambda j, blk, *_: (blk[j], 0))
    wspec = lambda a: pl.BlockSpec((1,) + a.shape[1:], lambda j, blk, ex, *_: (ex[j], 0, 0))
    return pl.pallas_call(
        _experts_kernel,
        grid_spec=pltpu.PrefetchScalarGridSpec(
            num_scalar_prefetch=len(items),
            grid=(n_items,),
            in_specs=[xblk, wspec(w_gate), wspec(w_up), wspec(w_down)],
            out_specs=xblk,
            scratch_shapes=[pltpu.VMEM(w_gate.shape[1:], BF16), pltpu.VMEM(w_up.shape[1:], BF16),
                            pltpu.VMEM(w_down.shape[1:], BF16)]),
        out_shape=jax.ShapeDtypeStruct(xs.shape, F32),
        compiler_params=pltpu.CompilerParams(dimension_semantics=("arbitrary",)),
        name="experts",
    )(*items, xs, w_gate, w_up, w_down)


def _work_items(counts, total_rows):
    n_blocks = total_rows // MOE_BLOCK
    n_items = n_blocks + N_EXPERTS
    end = jnp.cumsum(counts)
    start = end - counts
    first_blk = start // MOE_BLOCK
    nb = jnp.where(counts > 0, (end - 1) // MOE_BLOCK - first_blk + 1, 0)
    item_end = jnp.cumsum(nb)
    used = item_end[-1]
    j = jnp.arange(n_items, dtype=I32)
    jj = jnp.minimum(j, used - 1)
    e = jnp.minimum(jnp.sum((item_end[None, :] <= jj[:, None]).astype(I32), axis=1), N_EXPERTS - 1)
    onehot = (e[:, None] == jnp.arange(N_EXPERTS, dtype=I32)[None, :]).astype(I32)
    pick = lambda a: jnp.sum(onehot * a[None, :], axis=1)
    blk = pick(first_blk) + (jj - (pick(item_end) - pick(nb)))
    lo = jnp.maximum(pick(start), blk * MOE_BLOCK) - blk * MOE_BLOCK
    hi = jnp.minimum(pick(end), (blk + 1) * MOE_BLOCK) - blk * MOE_BLOCK
    live = j < used
    lo = jnp.where(live, lo, 0)
    hi = jnp.where(live, hi, 0)
    prev_blk = jnp.concatenate([jnp.full((1,), -1, I32), blk[:-1]])
    first = (live & (blk != prev_blk)).astype(I32)
    prev_e = jnp.concatenate([jnp.full((1,), -1, I32), e[:-1]])
    fresh = (live & (e != prev_e)).astype(I32)
    return (blk.astype(I32), e.astype(I32), lo.astype(I32), hi.astype(I32), first, fresh), start


def _combine_kernel(dest_ref, dnext_ref, gates_ref, h_ref, gf_ref, ys_ref, out_ref, buf_ref, sem, *, n_steps):
    i = pl.program_id(0)
    tm = h_ref.shape[0]
    slot = lax.rem(i, 2)

    def issue(d_ref, s):
        def body(g, carry):
            for u in range(DMA_UNROLL):
                r = g * DMA_UNROLL + u
                _row_copy(ys_ref, d_ref[0, r], buf_ref.at[s, 0], r, sem.at[s]).start(priority=0)
                _row_copy(ys_ref, d_ref[1, r], buf_ref.at[s, 1], r, sem.at[s]).start(priority=1)
            return carry

        lax.fori_loop(0, tm // DMA_UNROLL, body, 0)

    @pl.when(i == 0)
    def _():
        issue(dest_ref, 0)

    @pl.when(i + 1 < n_steps)
    def _():
        issue(dnext_ref, 1 - slot)

    for k in range(2):
        pltpu.make_async_copy(ys_ref.at[pl.ds(0, tm)], buf_ref.at[slot, k], sem.at[slot]).wait()
    gates = gates_ref[...]
    y = h_ref[...] + (buf_ref[slot, 0] * gates[:, 0:1] + buf_ref[slot, 1] * gates[:, 1:2])
    out_ref[...] = _rms(y, gf_ref[...])


def _combine(dest3, gates, h, gain, ys):
    n = h.shape[0]
    tm = COMBINE_TILE
    n_steps = n // tm
    row = lambda w: pl.BlockSpec((tm, w), lambda i: (i, 0))
    idx = lambda step: pl.BlockSpec((2, tm), lambda i: (0, jnp.minimum(i + step, n_steps - 1)),
                                    memory_space=pltpu.SMEM)
    return pl.pallas_call(
        functools.partial(_combine_kernel, n_steps=n_steps),
        grid=(n_steps,),
        in_specs=[idx(0), idx(1), row(2), row(D_MODEL), pl.BlockSpec(gain.shape, lambda i: (0, 0)),
                  pl.BlockSpec(memory_space=pl.ANY)],
        out_specs=row(D_MODEL),
        out_shape=jax.ShapeDtypeStruct((n, D_MODEL), F32),
        scratch_shapes=[pltpu.VMEM((2, 2, tm, D_MODEL), F32), pltpu.SemaphoreType.DMA((2,))],
        compiler_params=pltpu.CompilerParams(dimension_semantics=("arbitrary",)),
        name="combine_norm",
    )(dest3, dest3, gates, h, gain, ys)


def kernel(x_prompt, x_sample, cache_swa_k, cache_swa_v, state_gdn_conv, state_gdn, norm_mix, w_in, swa_sinks,
           gdn_conv_w, gdn_A_log, gdn_dt_bias, gdn_norm_w, w_out, norm_ffn, w_router_group, b_router_group,
           w_router_expert, b_router_expert, w_exp_gate, w_exp_up, w_exp_down, norm_final):
    depth = w_in.shape[0]
    assert depth == 1, "single trunk layer"
    bp, sp, _ = x_prompt.shape
    bs, ts, _ = x_sample.shape
    np_, ns = bp * sp, bs * ts
    l = 0

    w_in_b = w_in[l].astype(BF16)
    w_out_b = w_out[l].astype(BF16)
    g_mix = norm_mix[l].reshape(1, D_MODEL)
    g_ffn = norm_ffn[l].reshape(1, D_MODEL)
    g_fin = norm_final.reshape(1, D_MODEL)
    pad = LANES - N_EXPERTS - N_GROUPS
    w_router = jnp.concatenate([w_router_expert[l], w_router_group[l], jnp.zeros((D_MODEL, pad), F32)],
                               axis=1).astype(BF16)
    b_router = jnp.concatenate([b_router_expert[l].reshape(-1), b_router_group[l], jnp.zeros((pad,), F32)])[None]
    zeros8 = jnp.zeros((GDN_HEADS,), F32)
    alog16 = jnp.concatenate([gdn_A_log[l], zeros8])[None]
    dtb16 = jnp.concatenate([gdn_dt_bias[l], zeros8])[None]
    nw_group = jnp.tile(gdn_norm_w[l].reshape(1, HEAD_DIM), (1, GDN_GROUP))
    sinks = swa_sinks[l]

    q_p, k_p, v_p, c_p, z_p, ab_p = _inproj(x_prompt.reshape(np_, D_MODEL), g_mix, w_in_b)
    nblk = np_ // WINDOW
    k_p3 = k_p.reshape(nblk, WINDOW, SWA_KV)
    v_p3 = v_p.reshape(nblk, WINDOW, SWA_KV)
    (o_swa_p,) = _swa(sinks, q_p.reshape(nblk, WINDOW, SWA_Q), k_p3, v_p3, k_p3, v_p3,
                      bb=1, blocks_per_seq=sp // WINDOW, emit_cache=False)
    c_p3 = c_p.reshape(bp, sp, GDN_CONV_CH)
    o_gdn_p, s_fin_p = _gdn(c_p3, jnp.zeros((bp, CONV_W - 1, GDN_CONV_CH), F32), z_p.reshape(bp, sp, GDN_V),
                            ab_p.reshape(bp, sp, 2 * GDN_HEADS),
                            jnp.zeros((bp, GDN_HEADS, HEAD_DIM, HEAD_DIM), F32),
                            gdn_conv_w[l], alog16, dtb16, nw_group, chunk=GDN_CHUNK, seq_block=bp)
    h_p, xn_p, gates_p, meta_p, cnt_p = _outproj(
        x_prompt.reshape(np_, D_MODEL), o_swa_p.reshape(np_, SWA_Q), o_gdn_p.reshape(np_, GDN_V),
        w_out_b, g_ffn, w_router, b_router)

    q_s, k_s, v_s, c_s, z_s, ab_s = _inproj(x_sample.reshape(ns, D_MODEL), g_mix, w_in_b)
    o_swa_s, kcache_s, vcache_s = _swa(
        sinks, q_s.reshape(bs, ts, SWA_Q), k_s.reshape(bs, ts, SWA_KV), v_s.reshape(bs, ts, SWA_KV),
        cache_swa_k[l].reshape(bs, WINDOW, SWA_KV), cache_swa_v[l].reshape(bs, WINDOW, SWA_KV),
        bb=8, blocks_per_seq=None, emit_cache=True)
    c_s3 = c_s.reshape(bs, ts, GDN_CONV_CH)
    o_gdn_s, s_fin_s = _gdn(c_s3, state_gdn_conv[l], z_s.reshape(bs, ts, GDN_V),
                            ab_s.reshape(bs, ts, 2 * GDN_HEADS), state_gdn[l],
                            gdn_conv_w[l], alog16, dtb16, nw_group, chunk=ts, seq_block=8)
    h_s, xn_s, gates_s, meta_s, cnt_s = _outproj(
        x_sample.reshape(ns, D_MODEL), o_swa_s.reshape(ns, SWA_Q), o_gdn_s.reshape(ns, GDN_V),
        w_out_b, g_ffn, w_router, b_router)

    cnt_p_i = cnt_p[0, :N_EXPERTS].astype(I32)
    cnt_s_i = cnt_s[0, :N_EXPERTS].astype(I32)
    items, start = _work_items(cnt_p_i + cnt_s_i, 2 * (np_ + ns))
    expert_ids = jnp.arange(N_EXPERTS, dtype=I32)
    lookup = lambda table, ids: jnp.sum(jnp.where(ids[..., None] == expert_ids, table, 0), axis=-1)
    dest_p = lookup(start, meta_p[0:2].astype(I32)) + meta_p[2:4].astype(I32)
    dest_s = lookup(start + cnt_p_i, meta_s[0:2].astype(I32)) + meta_s[2:4].astype(I32)
    xs = _scatter(jnp.concatenate([dest_p, dest_s], axis=1), xn_p, xn_s)
    ys = _experts(items, xs, w_exp_gate[l], w_exp_up[l], w_exp_down[l])
    y_p = _combine(dest_p, gates_p, h_p, g_fin, ys)
    y_s = _combine(dest_s, gates_s, h_s, g_fin, ys)

    kv5 = lambda a, b: a.reshape(b, -1, SWA_KV_HEADS, HEAD_DIM)[None]
    return (y_p.reshape(bp, sp, D_MODEL), y_s.reshape(bs, ts, D_MODEL),
            kv5(k_p.reshape(bp, sp, SWA_KV)[:, -WINDOW:], bp), kv5(v_p.reshape(bp, sp, SWA_KV)[:, -WINDOW:], bp),
            kv5(kcache_s, bs), kv5(vcache_s, bs),
            c_p3[:, -(CONV_W - 1):][None], c_s3[:, -(CONV_W - 1):][None],
            s_fin_p[None], s_fin_s[None])
```

```python
import functools

import jax
import jax.numpy as jnp
from jax import lax
from jax.experimental import pallas as pl
from jax.experimental.pallas import tpu as pltpu

F32 = jnp.float32
BF16 = jnp.bfloat16
I32 = jnp.int32

D_MODEL = 1024
HEAD_DIM = 64
SWA_HEADS = 8
GDN_HEADS = 8
SWA_KV_HEADS = 2
GQA_GROUP = SWA_HEADS // SWA_KV_HEADS
WINDOW = 128
ATTN_SCALE = HEAD_DIM ** -0.5
CONV_W = 4
N_GROUPS = 8
EXPERTS_PER_GROUP = 8
N_EXPERTS = 64
D_EXPERT = 256
EPS = 1e-6

SWA_Q = SWA_HEADS * HEAD_DIM
SWA_KV = SWA_KV_HEADS * HEAD_DIM
GDN_QK = GDN_HEADS * HEAD_DIM
GDN_V = GDN_HEADS * HEAD_DIM
GDN_CONV_CH = 2 * GDN_QK + GDN_V
D_MIX = SWA_Q + GDN_V
D_IN = SWA_Q + 2 * SWA_KV + GDN_CONV_CH + GDN_V + 2 * GDN_HEADS
COL_K = SWA_Q
COL_V = COL_K + SWA_KV
COL_C = COL_V + SWA_KV
COL_Z = COL_C + GDN_CONV_CH
COL_AB = COL_Z + GDN_V

LANES = 128
NEG_BIG = -1e30
ROW_TILE = 512
MOE_BLOCK = 256
COMBINE_TILE = 256
DMA_UNROLL = 8
GDN_CHUNK = 64
GDN_TILE = 256
GDN_GROUP = 4
GDN_GROUP_W = GDN_GROUP * HEAD_DIM


def _rms(x, g):
    return x * lax.rsqrt(jnp.mean(x * x, axis=-1, keepdims=True) + EPS) * g


def _dot(a, b):
    return jnp.dot(a, b, preferred_element_type=F32)


def _dot_nt(a, b):
    return lax.dot_general(a, b, (((1,), (1,)), ((), ())), preferred_element_type=F32)


def _dot_tn(a, b):
    return lax.dot_general(a, b, (((0,), (0,)), ((), ())), preferred_element_type=F32)


def _split3(x):
    p1 = x.astype(BF16).astype(F32)
    r = x - p1
    p2 = r.astype(BF16).astype(F32)
    p3 = (r - p2).astype(BF16).astype(F32)
    return p1, p2, p3


def _inproj_kernel(x_ref, g_ref, w_ref, q_ref, k_ref, v_ref, c_ref, z_ref, ab_ref):
    x = x_ref[...]
    xb = _rms(x, g_ref[...]).astype(BF16)
    q_ref[...] = _dot(xb, w_ref[:, 0:COL_K])
    k_ref[...] = _dot(xb, w_ref[:, COL_K:COL_V])
    v_ref[...] = _dot(xb, w_ref[:, COL_V:COL_C])
    c_ref[...] = _dot(xb, w_ref[:, COL_C:COL_Z])
    z_ref[...] = _dot(xb, w_ref[:, COL_Z:COL_AB])
    ab_ref[...] = _dot(xb, w_ref[:, COL_AB:D_IN])


def _inproj(x2d, gain, w_bf16):
    n = x2d.shape[0]
    tm = ROW_TILE
    row = lambda w: pl.BlockSpec((tm, w), lambda i: (i, 0))
    full = lambda a: pl.BlockSpec(a.shape, lambda i: (0,) * a.ndim)
    widths = (SWA_Q, SWA_KV, SWA_KV, GDN_CONV_CH, GDN_V, 2 * GDN_HEADS)
    return pl.pallas_call(
        _inproj_kernel,
        grid=(n // tm,),
        in_specs=[row(D_MODEL), full(gain), full(w_bf16)],
        out_specs=[row(w) for w in widths],
        out_shape=[jax.ShapeDtypeStruct((n, w), F32) for w in widths],
        compiler_params=pltpu.CompilerParams(dimension_semantics=("arbitrary",)),
        name="inproj",
    )(x2d, gain, w_bf16)


def _swa_kernel(sink_ref, q_ref, kc_ref, vc_ref, kp_ref, vp_ref, o_ref, *cache_refs,
                bb, t, blocks_per_seq, emit_cache):
    if blocks_per_seq is None:
        has_prev = None
    else:
        has_prev = lax.rem(pl.program_id(0), blocks_per_seq) != 0
    rows = GQA_GROUP * t
    ri = lax.broadcasted_iota(I32, (rows, 1), 0)
    qi = lax.rem(ri, t)
    gi = ri // t
    if emit_cache:
        kj = lax.broadcasted_iota(I32, (rows, WINDOW), 1)
        mask = kj <= qi + (WINDOW - t)
    else:
        kj = lax.broadcasted_iota(I32, (rows, WINDOW + t), 1)
        mask = (kj <= qi + WINDOW) & (kj > qi)
        if has_prev is not None:
            mask = mask & (has_prev | (kj >= WINDOW))
    sinks = []
    for h in range(SWA_KV_HEADS):
        sink = jnp.zeros((rows, 1), F32)
        for g in range(GQA_GROUP):
            sink = jnp.where(gi == g, sink_ref[GQA_GROUP * h + g], sink)
        sinks.append(sink)
    chains = [(b, h) for b in range(bb) for h in range(SWA_KV_HEADS)]
    scores, values = [], []
    for b, h in chains:
        hs = slice(h * HEAD_DIM, (h + 1) * HEAD_DIM)
        kp, vp = kp_ref[b, :, hs], vp_ref[b, :, hs]
        kc, vc = kc_ref[b, :, hs], vc_ref[b, :, hs]
        if emit_cache:
            keys = jnp.concatenate([kp[t:], kc], axis=0)
            vals = jnp.concatenate([vp[t:], vc], axis=0)
            cache_refs[0][b, :, hs] = keys
            cache_refs[1][b, :, hs] = vals
        else:
            keys = jnp.concatenate([kp, kc], axis=0)
            vals = jnp.concatenate([vp, vc], axis=0)
        q4 = jnp.concatenate(
            [q_ref[b, :, (GQA_GROUP * h + g) * HEAD_DIM:(GQA_GROUP * h + g + 1) * HEAD_DIM]
             for g in range(GQA_GROUP)], axis=0)
        scores.append(_dot_nt(q4.astype(BF16), keys.astype(BF16)))
        values.append(vals.astype(BF16))
    probs, dens = [], []
    for (b, h), s in zip(chains, scores):
        s = jnp.where(mask, s * ATTN_SCALE, NEG_BIG)
        m = jnp.maximum(jnp.max(s, axis=-1, keepdims=True), sinks[h])
        p = jnp.exp(s - m)
        dens.append(jnp.sum(p, axis=-1, keepdims=True) + jnp.exp(sinks[h] - m))
        probs.append(p.astype(BF16))
    outs = [_dot(p, v) / den for p, v, den in zip(probs, values, dens)]
    for b in range(bb):
        o_ref[b] = jnp.concatenate([outs[b * SWA_KV_HEADS + h][g * t:(g + 1) * t]
                                    for h in range(SWA_KV_HEADS) for g in range(GQA_GROUP)], axis=-1)


def _swa(sinks, q3, k3, v3, kprev3, vprev3, *, bb, blocks_per_seq, emit_cache):
    nb, t, _ = q3.shape
    cur = lambda w: pl.BlockSpec((bb, t, w), lambda i: (i, 0, 0))
    if blocks_per_seq is None:
        prev = pl.BlockSpec((bb, WINDOW, SWA_KV), lambda i: (i, 0, 0))
    else:
        prev = pl.BlockSpec((bb, WINDOW, SWA_KV), lambda i: (jnp.maximum(i - 1, 0), 0, 0))
    out_specs = [cur(SWA_Q)]
    out_shape = [jax.ShapeDtypeStruct((nb, t, SWA_Q), F32)]
    if emit_cache:
        cache = pl.BlockSpec((bb, WINDOW, SWA_KV), lambda i: (i, 0, 0))
        out_specs += [cache, cache]
        out_shape += [jax.ShapeDtypeStruct((nb, WINDOW, SWA_KV), F32)] * 2
    return pl.pallas_call(
        functools.partial(_swa_kernel, bb=bb, t=t, blocks_per_seq=blocks_per_seq, emit_cache=emit_cache),
        grid=(nb // bb,),
        in_specs=[pl.BlockSpec(memory_space=pltpu.SMEM), cur(SWA_Q), cur(SWA_KV), cur(SWA_KV), prev, prev],
        out_specs=out_specs,
        out_shape=out_shape,
        compiler_params=pltpu.CompilerParams(dimension_semantics=("arbitrary",)),
        name="swa_cache" if emit_cache else "swa_band",
    )(sinks, q3, k3, v3, kprev3, vprev3)


def _gdn_prep_kernel(c_ref, hist_ref, ab_ref, cw_ref, alog_ref, dtb_ref,
                     u_ref, w_ref, qd_ref, kd_ref, qk_ref, gt_ref, cbuf_ref, *, chunk):
    sb, r, _ = c_ref.shape
    tp = sb * r
    cn = chunk
    low = w_ref.dtype
    hist_rows = CONV_W - 1

    @pl.when(pl.program_id(1) == 0)
    def _():
        cbuf_ref[:, 8 - hist_rows:8, :] = hist_ref[...]

    cbuf_ref[:, 8:8 + r, :] = c_ref[...]
    conv = cbuf_ref[:, 8 - hist_rows:8 - hist_rows + r, :] * cw_ref[0:1, :]
    for i in range(1, CONV_W):
        conv = conv + cbuf_ref[:, 8 - hist_rows + i:8 - hist_rows + i + r, :] * cw_ref[i:i + 1, :]
    tail = cbuf_ref[:, 8 + r - hist_rows:8 + r, :]
    cbuf_ref[:, 8 - hist_rows:8, :] = tail
    conv = (conv * jax.nn.sigmoid(conv)).reshape(tp, GDN_CONV_CH)

    ab = ab_ref[...].reshape(tp, 2 * GDN_HEADS)
    is_g = lax.broadcasted_iota(I32, (1, 2 * GDN_HEADS), 1) < GDN_HEADS
    g = jnp.where(is_g, -jnp.exp(alog_ref[...]) * jax.nn.softplus(ab + dtb_ref[...]), 0.0)
    beta = jax.nn.sigmoid(ab)

    ri = lax.broadcasted_iota(I32, (tp, tp), 0)
    ci = lax.broadcasted_iota(I32, (tp, tp), 1)
    same = (ri // cn) == (ci // cn)
    causal = same & (ri >= ci)
    strict = same & (ri > ci)
    eye = (ri == ci).astype(F32)
    stack = jnp.concatenate([causal.astype(BF16), same.astype(BF16)], axis=0)
    both = sum(_dot(stack, p.astype(BF16)) for p in _split3(g))
    gc, gl = both[:tp], both[tp:]
    e16 = lax.broadcasted_iota(I32, (2 * GDN_HEADS, 2 * GDN_HEADS), 0)
    eye16 = (e16 == lax.broadcasted_iota(I32, (2 * GDN_HEADS, 2 * GDN_HEADS), 1)).astype(BF16)
    gc_row = sum(_dot_nt(eye16, p.astype(BF16)) for p in _split3(gc))
    gt_ref[...] = jnp.exp(gl)
    fold = (lax.broadcasted_iota(I32, (tp, cn), 0) % cn == lax.broadcasted_iota(I32, (tp, cn), 1)).astype(low)

    levels = cn.bit_length() - 2
    us, ws, qds, kds, qks = [], [], [], [], []
    for h in range(GDN_HEADS):
        q = conv[:, h * HEAD_DIM:(h + 1) * HEAD_DIM]
        k = conv[:, GDN_QK + h * HEAD_DIM:GDN_QK + (h + 1) * HEAD_DIM]
        v = conv[:, 2 * GDN_QK + h * HEAD_DIM:2 * GDN_QK + (h + 1) * HEAD_DIM]
        q = q * lax.rsqrt(jnp.sum(q * q, axis=-1, keepdims=True) + EPS) * (HEAD_DIM ** -0.5)
        k = k * lax.rsqrt(jnp.sum(k * k, axis=-1, keepdims=True) + EPS)
        b_h = beta[:, GDN_HEADS + h:GDN_HEADS + h + 1]
        g_c = gc[:, h:h + 1]
        decay = jnp.exp(jnp.where(causal, g_c - gc_row[h:h + 1, :], NEG_BIG))
        eg = jnp.exp(g_c)
        kb = k * b_h
        k_l = k.astype(low)
        lmat = jnp.where(strict, _dot_nt(kb.astype(low), k_l) * decay, 0.0)
        qk = _dot_nt(q.astype(low), k_l) * decay
        qks.append(_dot(qk.astype(low), fold))
        l_l = lmat.astype(low)
        x = eye - lmat
        p = _dot(l_l, l_l)
        for lev in range(1, levels + 1):
            p_l = p.astype(low)
            if lev < levels:
                xp = _dot(jnp.concatenate([x.astype(low), p_l], axis=0), p_l)
                x = x + xp[:tp]
                p = xp[tp:]
            else:
                x = x + _dot(x.astype(low), p_l)
        rhs = jnp.concatenate([v * b_h, kb * eg], axis=-1)
        sol = _dot(x.astype(low), rhs.astype(low))
        us.append(sol[:, :HEAD_DIM])
        ws.append(sol[:, HEAD_DIM:])
        qds.append(q * eg)
        kds.append(k * jnp.exp(gl[:, h:h + 1] - g_c))
    u_ref[...] = jnp.concatenate(us, axis=-1)
    w_ref[...] = jnp.concatenate(ws, axis=-1).astype(low)
    qd_ref[...] = jnp.concatenate(qds, axis=-1).astype(low)
    kd_ref[...] = jnp.concatenate(kds, axis=-1).astype(low)
    qk_ref[...] = jnp.concatenate(qks, axis=-1).astype(low)


def _gdn_prep_pair_kernel(c_ref, hist_ref, ab_ref, cw_ref, alog_ref, dtb_ref,
                          u_ref, w_ref, qd_ref, kd_ref, qk_ref, gt_ref, cbuf_ref):
    _, r, _ = c_ref.shape
    tp = r
    cn = HEAD_DIM
    pair_w = 2 * HEAD_DIM
    hist_rows = CONV_W - 1

    @pl.when(pl.program_id(1) == 0)
    def _():
        cbuf_ref[:, 8 - hist_rows:8, :] = hist_ref[...]

    cbuf_ref[:, 8:8 + r, :] = c_ref[...]
    conv = cbuf_ref[:, 8 - hist_rows:8 - hist_rows + r, :] * cw_ref[0:1, :]
    for i in range(1, CONV_W):
        conv = conv + cbuf_ref[:, 8 - hist_rows + i:8 - hist_rows + i + r, :] * cw_ref[i:i + 1, :]
    tail = cbuf_ref[:, 8 + r - hist_rows:8 + r, :]
    cbuf_ref[:, 8 - hist_rows:8, :] = tail
    conv = (conv * jax.nn.sigmoid(conv)).reshape(tp, GDN_CONV_CH)

    ab = ab_ref[...].reshape(tp, 2 * GDN_HEADS)
    is_g = lax.broadcasted_iota(I32, (1, 2 * GDN_HEADS), 1) < GDN_HEADS
    g = jnp.where(is_g, -jnp.exp(alog_ref[...]) * jax.nn.softplus(ab + dtb_ref[...]), 0.0)
    beta = jax.nn.sigmoid(ab)

    ri = lax.broadcasted_iota(I32, (tp, tp), 0)
    ci = lax.broadcasted_iota(I32, (tp, tp), 1)
    same = (ri // cn) == (ci // cn)
    same_l = same.astype(BF16)
    stack = jnp.concatenate([(same & (ri >= ci)).astype(BF16), same_l], axis=0)
    both = sum(_dot(stack, p.astype(BF16)) for p in _split3(g))
    gc, gl = both[:tp], both[tp:]
    gt_ref[...] = jnp.exp(gl)

    lane = lax.broadcasted_iota(I32, (tp, pair_w), 1)
    c_in = lax.broadcasted_iota(I32, (tp, pair_w), 0) % cn
    j_in = lane % cn
    left = lane < cn
    causal = c_in >= j_in
    strict = c_in > j_in
    diag = c_in == j_in
    eye = diag.astype(F32)
    bdmask = ((lax.broadcasted_iota(I32, (pair_w, pair_w), 0) // cn)
              == (lax.broadcasted_iota(I32, (pair_w, pair_w), 1) // cn))
    ones_bd = bdmask.astype(BF16)

    def bd(m):
        return jnp.where(bdmask, jnp.concatenate([m, m], axis=0), 0.0).astype(BF16)

    def head_sum(x):
        hi = x.astype(BF16)
        lo = (x - hi.astype(F32)).astype(BF16)
        return _dot(hi, ones_bd) + _dot(lo, ones_bd)

    levels = cn.bit_length() - 2
    chunks = [slice(n * cn, (n + 1) * cn) for n in range(tp // cn)]
    n_pairs = GDN_HEADS // 2
    pairs = range(n_pairs)
    pick = lambda m, off, p: jnp.where(left, m[:, off + 2 * p:off + 2 * p + 1], m[:, off + 2 * p + 1:off + 2 * p + 2])
    third = lambda i, p: conv[:, i * GDN_QK + p * pair_w:i * GDN_QK + (p + 1) * pair_w]
    qs = [third(0, p) for p in pairs]
    ks = [third(1, p) for p in pairs]
    qs = [q * lax.rsqrt(head_sum(q * q) + EPS) * (HEAD_DIM ** -0.5) for q in qs]
    ks = [k * lax.rsqrt(head_sum(k * k) + EPS) for k in ks]
    gcps = [pick(gc, 0, p) for p in pairs]
    rowms = [sum(_dot(same_l, part.astype(BF16)) for part in _split3(jnp.where(diag, gcp, 0.0))) for gcp in gcps]
    decays = [jnp.exp(jnp.where(causal, gcp - rowm, NEG_BIG)) for gcp, rowm in zip(gcps, rowms)]
    lmats, vbs, kbegs = [], [], []
    for p in pairs:
        ls = slice(p * pair_w, (p + 1) * pair_w)
        q, k, gcp, decay = qs[p], ks[p], gcps[p], decays[p]
        bp = pick(beta, GDN_HEADS, p)
        eg = jnp.exp(gcp)
        kb = k * bp
        vbs.append(third(2, p) * bp)
        kbegs.append(kb * eg)
        qd_ref[:, ls] = (q * eg).astype(BF16)
        kd_ref[:, ls] = (k * jnp.exp(pick(gl, 0, p) - gcp)).astype(BF16)
        q_l, kb_l = q.astype(BF16), kb.astype(BF16)
        kbd = [bd(k[rs]) for rs in chunks]
        kk = jnp.concatenate([_dot_nt(kb_l[rs], kbd[n]) for n, rs in enumerate(chunks)], axis=0)
        qk = jnp.concatenate([_dot_nt(q_l[rs], kbd[n]) for n, rs in enumerate(chunks)], axis=0)
        qk_ref[:, ls] = (qk * decay).astype(BF16)
        lmats.append(jnp.where(strict, kk * decay, 0.0))
    bodies = [(p, rs) for p in range(n_pairs) for rs in chunks]
    xs = [eye[rs] - lmats[p][rs] for p, rs in bodies]
    ps = [_dot(lmats[p][rs].astype(BF16), bd(lmats[p][rs])) for p, rs in bodies]
    for lev in range(1, levels + 1):
        pbd = [bd(pm) for pm in ps]
        if lev < levels:
            xp = [_dot(jnp.concatenate([xm, pm], axis=0).astype(BF16), wm) for xm, pm, wm in zip(xs, ps, pbd)]
            xs = [xm + m[:cn] for xm, m in zip(xs, xp)]
            ps = [m[cn:] for m in xp]
        else:
            xs = [xm + _dot(xm.astype(BF16), wm) for xm, wm in zip(xs, pbd)]
    x_l = [xm.astype(BF16) for xm in xs]
    us = [_dot(xm, bd(vbs[p][rs])) for xm, (p, rs) in zip(x_l, bodies)]
    ws = [_dot(xm, bd(kbegs[p][rs])) for xm, (p, rs) in zip(x_l, bodies)]
    nc = len(chunks)
    for p in range(n_pairs):
        ls = slice(p * pair_w, (p + 1) * pair_w)
        u_ref[:, ls] = jnp.concatenate(us[p * nc:(p + 1) * nc], axis=0)
        w_ref[:, ls] = jnp.concatenate(ws[p * nc:(p + 1) * nc], axis=0).astype(BF16)


def _gdn_scan_kernel(u_ref, w_ref, qd_ref, kd_ref, qk_ref, gt_ref, z_ref, s0_ref, nw_ref,
                     o_ref, sfin_ref, sbd_ref, *, chunk, n_chunks):
    bb = u_ref.shape[0]
    cn = chunk
    low = w_ref.dtype
    gw = GDN_GROUP_W
    ni = pl.program_id(1)

    @pl.when(ni == 0)
    def _():
        sbd_ref[...] = jnp.zeros_like(sbd_ref)
        for b in range(bb):
            for h in range(GDN_HEADS):
                gi, hh = divmod(h, GDN_GROUP)
                ds = slice(hh * HEAD_DIM, (hh + 1) * HEAD_DIM)
                sbd_ref[b, gi, ds, ds] = s0_ref[b, h]

    bdmask = ((lax.broadcasted_iota(I32, (gw, gw), 0) // HEAD_DIM)
              == (lax.broadcasted_iota(I32, (gw, gw), 1) // HEAD_DIM))
    ones_bd = bdmask.astype(BF16)
    vmask = ((lax.broadcasted_iota(I32, (GDN_GROUP * cn, gw), 0) // cn)
             == (lax.broadcasted_iota(I32, (GDN_GROUP * cn, gw), 1) // HEAD_DIM))
    e_row = lax.broadcasted_iota(I32, (2 * GDN_HEADS, gw), 0)
    e_col = lax.broadcasted_iota(I32, (2 * GDN_HEADS, gw), 1) // HEAD_DIM
    chains = [(b, gi) for b in range(bb) for gi in range(GDN_HEADS // GDN_GROUP)]
    lanes = lambda gi: slice(gi * gw, (gi + 1) * gw)
    states = [sbd_ref[b, gi] for b, gi in chains]
    states_l = [s.astype(low) for s in states]
    v_new = [u_ref[b, :, lanes(gi)] - _dot(w_ref[b, :, lanes(gi)], s_l) for (b, gi), s_l in zip(chains, states_l)]
    q_s = [_dot(qd_ref[b, :, lanes(gi)], s_l) for (b, gi), s_l in zip(chains, states_l)]
    v_l = [v.astype(low) for v in v_new]
    outs = []
    for (b, gi), s, v, qs in zip(chains, states, v_l, q_s):
        vbd = jnp.where(vmask, jnp.concatenate([v] * GDN_GROUP, axis=0), jnp.zeros((), low))
        outs.append(qs + _dot(qk_ref[b, :, gi * GDN_GROUP * cn:(gi + 1) * GDN_GROUP * cn], vbd))
        upd = _dot_tn(kd_ref[b, :, lanes(gi)], v)
        expand = (e_row == e_col + gi * GDN_GROUP).astype(BF16)
        gte = sum(_dot(p.astype(BF16), expand) for p in _split3(gt_ref[b, 0:8, :]))[0:1]
        sbd_ref[b, gi] = s * gte + jnp.where(bdmask, upd, 0.0)
    for (b, gi), o in zip(chains, outs):
        o2 = o * o
        hi = o2.astype(BF16)
        lo = (o2 - hi.astype(F32)).astype(BF16)
        ms = (_dot(hi, ones_bd) + _dot(lo, ones_bd)) * (1.0 / HEAD_DIM)
        zg = z_ref[b, :, lanes(gi)]
        o_ref[b, :, lanes(gi)] = o * lax.rsqrt(ms + EPS) * nw_ref[...] * (zg * jax.nn.sigmoid(zg))

    @pl.when(ni == n_chunks - 1)
    def _():
        for b in range(bb):
            for h in range(GDN_HEADS):
                gi, hh = divmod(h, GDN_GROUP)
                ds = slice(hh * HEAD_DIM, (hh + 1) * HEAD_DIM)
                sfin_ref[b, h] = sbd_ref[b, gi, ds, ds]


def _gdn(c3, hist, z3, ab3, s0, conv_w, alog16, dtb16, nw_group, *, chunk, seq_block):
    nseq, t, _ = c3.shape
    n = nseq * t
    sb, r = (1, GDN_TILE) if t >= GDN_TILE else (GDN_TILE // t, t)
    tiles = t // r
    low = BF16 if chunk >= 16 else F32
    blk = lambda w: pl.BlockSpec((sb, r, w), lambda s, i: (s, i, 0))
    full = lambda a: pl.BlockSpec(a.shape, lambda s, i: (0,) * a.ndim)
    flat = lambda w: pl.BlockSpec((GDN_TILE, w), lambda s, i: (s * tiles + i, 0))
    widths = (GDN_V, GDN_V, GDN_QK, GDN_QK, GDN_HEADS * chunk, 2 * GDN_HEADS)
    dtypes = (F32, low, low, low, low, F32)
    prep_out = dict(out_specs=[flat(wd) for wd in widths],
                    out_shape=[jax.ShapeDtypeStruct((n, wd), dt) for wd, dt in zip(widths, dtypes)],
                    compiler_params=pltpu.CompilerParams(dimension_semantics=("arbitrary", "arbitrary")))
    lane_dense = chunk == HEAD_DIM and sb == 1
    u, w, qd, kd, qk, gt = pl.pallas_call(
        _gdn_prep_pair_kernel if lane_dense else functools.partial(_gdn_prep_kernel, chunk=chunk),
        grid=(nseq // sb, tiles),
        in_specs=[blk(GDN_CONV_CH), pl.BlockSpec((sb, CONV_W - 1, GDN_CONV_CH), lambda s, i: (s, 0, 0)),
                  blk(2 * GDN_HEADS), full(conv_w), full(alog16), full(dtb16)],
        scratch_shapes=[pltpu.VMEM((sb, 8 + r, GDN_CONV_CH), F32)],
        name="gdn_prep_pair" if lane_dense else "gdn_prep", **prep_out)(c3, hist, ab3, conv_w, alog16, dtb16)

    n_chunks = t // chunk
    tok = lambda wd: pl.BlockSpec((seq_block, chunk, wd), lambda s, c: (s, c, 0))
    per_seq = pl.BlockSpec((seq_block,) + s0.shape[1:], lambda s, c: (s, 0, 0, 0))
    seq3 = lambda a: a.reshape(nseq, t, a.shape[-1])
    return pl.pallas_call(
        functools.partial(_gdn_scan_kernel, chunk=chunk, n_chunks=n_chunks),
        grid=(nseq // seq_block, n_chunks),
        in_specs=[tok(wd) for wd in widths] + [tok(GDN_V), per_seq,
                                               pl.BlockSpec(nw_group.shape, lambda s, c: (0, 0))],
        out_specs=[tok(GDN_V), per_seq],
        out_shape=[jax.ShapeDtypeStruct((nseq, t, GDN_V), F32), jax.ShapeDtypeStruct(s0.shape, F32)],
        scratch_shapes=[pltpu.VMEM((seq_block, GDN_HEADS // GDN_GROUP, GDN_GROUP_W, GDN_GROUP_W), F32)],
        compiler_params=pltpu.CompilerParams(dimension_semantics=("arbitrary", "arbitrary")),
        name="gdn_scan",
    )(seq3(u), seq3(w), seq3(qd), seq3(kd), seq3(qk), seq3(gt), z3, s0, nw_group)


def _outproj_kernel(x_ref, osw_ref, ogd_ref, wo_ref, gf_ref, wr_ref, br_ref,
                    h_ref, xn_ref, gates_ref, meta_ref, cnt_ref, run_ref):
    i = pl.program_id(0)
    tm = x_ref.shape[0]

    @pl.when(i == 0)
    def _():
        run_ref[...] = jnp.zeros_like(run_ref)

    h = (x_ref[...] + _dot(osw_ref[...].astype(BF16), wo_ref[0:SWA_Q, :])
         + _dot(ogd_ref[...].astype(BF16), wo_ref[SWA_Q:D_MIX, :]))
    h_ref[...] = h
    xn = _rms(h, gf_ref[...])
    xn_ref[...] = xn
    logits = _dot(xn.astype(BF16), wr_ref[...])

    lane = lax.broadcasted_iota(I32, (tm, LANES), 1)
    bias = br_ref[...]
    is_g = (lane >= N_EXPERTS) & (lane < N_EXPERTS + N_GROUPS)
    lg = jnp.where(is_g, logits, NEG_BIG)
    pg = jnp.where(is_g, jnp.exp(lg - jnp.max(lg, axis=-1, keepdims=True)), 0.0)
    group_p = pg / jnp.sum(pg, axis=-1, keepdims=True)
    score_g = jnp.where(is_g, group_p + bias, NEG_BIG)
    g_lane = jnp.min(jnp.where(score_g == jnp.max(score_g, axis=-1, keepdims=True), lane, 2 * LANES),
                     axis=-1, keepdims=True)
    g_w = jnp.sum(jnp.where(lane == g_lane, group_p, 0.0), axis=-1, keepdims=True)
    sel = (lane < N_EXPERTS) & ((lane // EXPERTS_PER_GROUP) == (g_lane - N_EXPERTS))
    le = jnp.where(sel, logits, NEG_BIG)
    pe = jnp.where(sel, jnp.exp(le - jnp.max(le, axis=-1, keepdims=True)), 0.0)
    e_p = pe / jnp.sum(pe, axis=-1, keepdims=True)
    score = jnp.where(sel, e_p + bias, NEG_BIG)
    i1 = jnp.min(jnp.where(score == jnp.max(score, axis=-1, keepdims=True), lane, 2 * LANES),
                 axis=-1, keepdims=True)
    score2 = jnp.where(lane == i1, NEG_BIG, score)
    i2 = jnp.min(jnp.where(score2 == jnp.max(score2, axis=-1, keepdims=True), lane, 2 * LANES),
                 axis=-1, keepdims=True)
    w1 = jnp.sum(jnp.where(lane == i1, e_p, 0.0), axis=-1, keepdims=True)
    w2 = jnp.sum(jnp.where(lane == i2, e_p, 0.0), axis=-1, keepdims=True)
    wsum = w1 + w2

    oh1 = lane == i1
    oh2 = lane == i2
    ohs = (oh1 | oh2).astype(BF16)
    tri = (lax.broadcasted_iota(I32, (tm, tm), 0) > lax.broadcasted_iota(I32, (tm, tm), 1)).astype(BF16)
    before = _dot(tri, ohs) + run_ref[...]
    r1 = jnp.sum(jnp.where(oh1, before, 0.0), axis=-1, keepdims=True)
    r2 = jnp.sum(jnp.where(oh2, before, 0.0), axis=-1, keepdims=True)
    run_ref[...] = run_ref[...] + jnp.sum(ohs.astype(F32), axis=0, keepdims=True)
    cnt_ref[...] = run_ref[...]

    two = lax.broadcasted_iota(I32, (tm, 2), 1)
    gates_ref[...] = jnp.where(two == 0, g_w * (w1 / wsum), g_w * (w2 / wsum))
    packed = jnp.where(lane == 0, i1.astype(F32), jnp.where(lane == 1, i2.astype(F32),
                       jnp.where(lane == 2, r1, jnp.where(lane == 3, r2, 0.0))))
    eye8 = (lax.broadcasted_iota(I32, (8, LANES), 0) == lax.broadcasted_iota(I32, (8, LANES), 1)).astype(BF16)
    meta_ref[...] = sum(_dot_nt(eye8, part.astype(BF16)) for part in _split3(packed))


def _outproj(x2d, o_swa, o_gdn, wo_bf16, gain, w_router, b_router):
    n = x2d.shape[0]
    tm = ROW_TILE
    row = lambda w: pl.BlockSpec((tm, w), lambda i: (i, 0))
    full = lambda a: pl.BlockSpec(a.shape, lambda i: (0,) * a.ndim)
    return pl.pallas_call(
        _outproj_kernel,
        grid=(n // tm,),
        in_specs=[row(D_MODEL), row(SWA_Q), row(GDN_V), full(wo_bf16), full(gain), full(w_router), full(b_router)],
        out_specs=[row(D_MODEL), row(D_MODEL), row(2), pl.BlockSpec((8, tm), lambda i: (0, i)),
                   pl.BlockSpec((1, LANES), lambda i: (0, 0))],
        out_shape=[jax.ShapeDtypeStruct((n, D_MODEL), F32), jax.ShapeDtypeStruct((n, D_MODEL), F32),
                   jax.ShapeDtypeStruct((n, 2), F32), jax.ShapeDtypeStruct((8, n), F32),
                   jax.ShapeDtypeStruct((1, LANES), F32)],
        scratch_shapes=[pltpu.VMEM((1, LANES), F32)],
        compiler_params=pltpu.CompilerParams(dimension_semantics=("arbitrary",)),
        name="outproj_router",
    )(x2d, o_swa, o_gdn, wo_bf16, gain, w_router, b_router)


def _row_copy(src_ref, src_row, dst_ref, dst_row, sem):
    return pltpu.make_async_copy(src_ref.at[pl.ds(src_row, 1)], dst_ref.at[pl.ds(dst_row, 1)], sem)


def _scatter_kernel(dest_ref, xp_ref, xs_ref, out_ref, sem, *, tiles_p):
    i = pl.program_id(0)
    tm = xp_ref.shape[0]

    def run(src_ref):
        def issue(g, carry):
            for u in range(DMA_UNROLL):
                r = g * DMA_UNROLL + u
                _row_copy(src_ref, r, out_ref, dest_ref[0, r], sem).start(priority=0)
                _row_copy(src_ref, r, out_ref, dest_ref[1, r], sem).start(priority=1)
            return carry

        lax.fori_loop(0, tm // DMA_UNROLL, issue, 0)
        for _ in range(2):
            pltpu.make_async_copy(src_ref, out_ref.at[pl.ds(0, tm)], sem).wait()

    @pl.when(i < tiles_p)
    def _():
        run(xp_ref)

    @pl.when(i >= tiles_p)
    def _():
        run(xs_ref)


def _scatter(dest3, xn_p, xn_s):
    tm = ROW_TILE
    tiles_p, tiles_s = xn_p.shape[0] // tm, xn_s.shape[0] // tm
    rows = 2 * (xn_p.shape[0] + xn_s.shape[0])
    return pl.pallas_call(
        functools.partial(_scatter_kernel, tiles_p=tiles_p),
        grid=(tiles_p + tiles_s,),
        in_specs=[pl.BlockSpec((2, tm), lambda i: (0, i), memory_space=pltpu.SMEM),
                  pl.BlockSpec((tm, D_MODEL), lambda i: (jnp.minimum(i, tiles_p - 1), 0)),
                  pl.BlockSpec((tm, D_MODEL), lambda i: (jnp.maximum(i - tiles_p, 0), 0))],
        out_specs=pl.BlockSpec(memory_space=pl.ANY),
        out_shape=jax.ShapeDtypeStruct((rows, D_MODEL), F32),
        scratch_shapes=[pltpu.SemaphoreType.DMA(())],
        compiler_params=pltpu.CompilerParams(dimension_semantics=("arbitrary",)),
        name="scatter_rows",
    )(dest3, xn_p, xn_s)


def _experts_kernel(blk_ref, exp_ref, lo_ref, hi_ref, first_ref, fresh_ref, x_ref, wg_ref, wu_ref, wd_ref, y_ref,
                    wg_l, wu_l, wd_l):
    j = pl.program_id(0)
    lo, hi = lo_ref[j], hi_ref[j]

    @pl.when(fresh_ref[j] == 1)
    def _():
        wg_l[...] = wg_ref[0].astype(BF16)
        wu_l[...] = wu_ref[0].astype(BF16)
        wd_l[...] = wd_ref[0].astype(BF16)

    @pl.when(hi > lo)
    def _():
        x = x_ref[...].astype(BF16)
        gate = _dot(x, wg_l[...])
        up = _dot(x, wu_l[...])
        hid = (gate * jax.nn.sigmoid(gate)) * up
        y = _dot(hid.astype(BF16), wd_l[...])
        r = lax.broadcasted_iota(I32, (MOE_BLOCK, 1), 0)
        mine = (r >= lo) & (r < hi)

        @pl.when(first_ref[j] == 1)
        def _():
            y_ref[...] = jnp.where(mine, y, 0.0)

        @pl.when(first_ref[j] == 0)
        def _():
            y_ref[...] = jnp.where(mine, y, y_ref[...])


def _experts(items, xs, w_gate, w_up, w_down):
    n_items = items[0].shape[0]
    xblk = pl.BlockSpec((MOE_BLOCK, D_MODEL), lambda j, blk, *_: (blk[j], 0))
    wspec = lambda a: pl.BlockSpec((1,) + a.shape[1:], lambda j, blk, ex, *_: (ex[j], 0, 0))
    return pl.pallas_call(
        _experts_kernel,
        grid_spec=pltpu.PrefetchScalarGridSpec(
            num_scalar_prefetch=len(items),
            grid=(n_items,),
            in_specs=[xblk, wspec(w_gate), wspec(w_up), wspec(w_down)],
            out_specs=xblk,
            scratch_shapes=[pltpu.VMEM(w_gate.shape[1:], BF16), pltpu.VMEM(w_up.shape[1:], BF16),
                            pltpu.VMEM(w_down.shape[1:], BF16)]),
        out_shape=jax.ShapeDtypeStruct(xs.shape, F32),
        compiler_params=pltpu.CompilerParams(dimension_semantics=("arbitrary",)),
        name="experts",
    )(*items, xs, w_gate, w_up, w_down)


def _work_items(counts, total_rows):
    n_blocks = total_rows // MOE_BLOCK
    n_items = n_blocks + N_EXPERTS
    end = jnp.cumsum(counts)
    start = end - counts
    first_blk = start // MOE_BLOCK
    nb = jnp.where(counts > 0, (end - 1) // MOE_BLOCK - first_blk + 1, 0)
    item_end = jnp.cumsum(nb)
    used = item_end[-1]
    j = jnp.arange(n_items, dtype=I32)
    jj = jnp.minimum(j, used - 1)
    e = jnp.minimum(jnp.sum((item_end[None, :] <= jj[:, None]).astype(I32), axis=1), N_EXPERTS - 1)
    onehot = (e[:, None] == jnp.arange(N_EXPERTS, dtype=I32)[None, :]).astype(I32)
    pick = lambda a: jnp.sum(onehot * a[None, :], axis=1)
    blk = pick(first_blk) + (jj - (pick(item_end) - pick(nb)))
    lo = jnp.maximum(pick(start), blk * MOE_BLOCK) - blk * MOE_BLOCK
    hi = jnp.minimum(pick(end), (blk + 1) * MOE_BLOCK) - blk * MOE_BLOCK
    live = j < used
    lo = jnp.where(live, lo, 0)
    hi = jnp.where(live, hi, 0)
    prev_blk = jnp.concatenate([jnp.full((1,), -1, I32), blk[:-1]])
    first = (live & (blk != prev_blk)).astype(I32)
    prev_e = jnp.concatenate([jnp.full((1,), -1, I32), e[:-1]])
    fresh = (live & (e != prev_e)).astype(I32)
    return (blk.astype(I32), e.astype(I32), lo.astype(I32), hi.astype(I32), first, fresh), start


def _combine_kernel(dest_ref, dnext_ref, gates_ref, h_ref, gf_ref, ys_ref, out_ref, buf_ref, sem, *, n_steps):
    i = pl.program_id(0)
    tm = h_ref.shape[0]
    slot = lax.rem(i, 2)

    def issue(d_ref, s):
        def body(g, carry):
            for u in range(DMA_UNROLL):
                r = g * DMA_UNROLL + u
                _row_copy(ys_ref, d_ref[0, r], buf_ref.at[s, 0], r, sem.at[s]).start(priority=0)
                _row_copy(ys_ref, d_ref[1, r], buf_ref.at[s, 1], r, sem.at[s]).start(priority=1)
            return carry

        lax.fori_loop(0, tm // DMA_UNROLL, body, 0)

    @pl.when(i == 0)
    def _():
        issue(dest_ref, 0)

    @pl.when(i + 1 < n_steps)
    def _():
        issue(dnext_ref, 1 - slot)

    for k in range(2):
        pltpu.make_async_copy(ys_ref.at[pl.ds(0, tm)], buf_ref.at[slot, k], sem.at[slot]).wait()
    gates = gates_ref[...]
    y = h_ref[...] + (buf_ref[slot, 0] * gates[:, 0:1] + buf_ref[slot, 1] * gates[:, 1:2])
    out_ref[...] = _rms(y, gf_ref[...])


def _combine(dest3, gates, h, gain, ys):
    n = h.shape[0]
    tm = COMBINE_TILE
    n_steps = n // tm
    row = lambda w: pl.BlockSpec((tm, w), lambda i: (i, 0))
    idx = lambda step: pl.BlockSpec((2, tm), lambda i: (0, jnp.minimum(i + step, n_steps - 1)),
                                    memory_space=pltpu.SMEM)
    return pl.pallas_call(
        functools.partial(_combine_kernel, n_steps=n_steps),
        grid=(n_steps,),
        in_specs=[idx(0), idx(1), row(2), row(D_MODEL), pl.BlockSpec(gain.shape, lambda i: (0, 0)),
                  pl.BlockSpec(memory_space=pl.ANY)],
        out_specs=row(D_MODEL),
        out_shape=jax.ShapeDtypeStruct((n, D_MODEL), F32),
        scratch_shapes=[pltpu.VMEM((2, 2, tm, D_MODEL), F32), pltpu.SemaphoreType.DMA((2,))],
        compiler_params=pltpu.CompilerParams(dimension_semantics=("arbitrary",)),
        name="combine_norm",
    )(dest3, dest3, gates, h, gain, ys)


def kernel(x_prompt, x_sample, cache_swa_k, cache_swa_v, state_gdn_conv, state_gdn, norm_mix, w_in, swa_sinks,
           gdn_conv_w, gdn_A_log, gdn_dt_bias, gdn_norm_w, w_out, norm_ffn, w_router_group, b_router_group,
           w_router_expert, b_router_expert, w_exp_gate, w_exp_up, w_exp_down, norm_final):
    depth = w_in.shape[0]
    assert depth == 1, "single trunk layer"
    bp, sp, _ = x_prompt.shape
    bs, ts, _ = x_sample.shape
    np_, ns = bp * sp, bs * ts
    l = 0

    w_in_b = w_in[l].astype(BF16)
    w_out_b = w_out[l].astype(BF16)
    g_mix = norm_mix[l].reshape(1, D_MODEL)
    g_ffn = norm_ffn[l].reshape(1, D_MODEL)
    g_fin = norm_final.reshape(1, D_MODEL)
    pad = LANES - N_EXPERTS - N_GROUPS
    w_router = jnp.concatenate([w_router_expert[l], w_router_group[l], jnp.zeros((D_MODEL, pad), F32)],
                               axis=1).astype(BF16)
    b_router = jnp.concatenate([b_router_expert[l].reshape(-1), b_router_group[l], jnp.zeros((pad,), F32)])[None]
    zeros8 = jnp.zeros((GDN_HEADS,), F32)
    alog16 = jnp.concatenate([gdn_A_log[l], zeros8])[None]
    dtb16 = jnp.concatenate([gdn_dt_bias[l], zeros8])[None]
    nw_group = jnp.tile(gdn_norm_w[l].reshape(1, HEAD_DIM), (1, GDN_GROUP))
    sinks = swa_sinks[l]

    q_p, k_p, v_p, c_p, z_p, ab_p = _inproj(x_prompt.reshape(np_, D_MODEL), g_mix, w_in_b)
    nblk = np_ // WINDOW
    k_p3 = k_p.reshape(nblk, WINDOW, SWA_KV)
    v_p3 = v_p.reshape(nblk, WINDOW, SWA_KV)
    (o_swa_p,) = _swa(sinks, q_p.reshape(nblk, WINDOW, SWA_Q), k_p3, v_p3, k_p3, v_p3,
                      bb=1, blocks_per_seq=sp // WINDOW, emit_cache=False)
    c_p3 = c_p.reshape(bp, sp, GDN_CONV_CH)
    o_gdn_p, s_fin_p = _gdn(c_p3, jnp.zeros((bp, CONV_W - 1, GDN_CONV_CH), F32), z_p.reshape(bp, sp, GDN_V),
                            ab_p.reshape(bp, sp, 2 * GDN_HEADS),
                            jnp.zeros((bp, GDN_HEADS, HEAD_DIM, HEAD_DIM), F32),
                            gdn_conv_w[l], alog16, dtb16, nw_group, chunk=GDN_CHUNK, seq_block=bp)
    h_p, xn_p, gates_p, meta_p, cnt_p = _outproj(
        x_prompt.reshape(np_, D_MODEL), o_swa_p.reshape(np_, SWA_Q), o_gdn_p.reshape(np_, GDN_V),
        w_out_b, g_ffn, w_router, b_router)

    q_s, k_s, v_s, c_s, z_s, ab_s = _inproj(x_sample.reshape(ns, D_MODEL), g_mix, w_in_b)
    o_swa_s, kcache_s, vcache_s = _swa(
        sinks, q_s.reshape(bs, ts, SWA_Q), k_s.reshape(bs, ts, SWA_KV), v_s.reshape(bs, ts, SWA_KV),
        cache_swa_k[l].reshape(bs, WINDOW, SWA_KV), cache_swa_v[l].reshape(bs, WINDOW, SWA_KV),
        bb=8, blocks_per_seq=None, emit_cache=True)
    c_s3 = c_s.reshape(bs, ts, GDN_CONV_CH)
    o_gdn_s, s_fin_s = _gdn(c_s3, state_gdn_conv[l], z_s.reshape(bs, ts, GDN_V),
                            ab_s.reshape(bs, ts, 2 * GDN_HEADS), state_gdn[l],
                            gdn_conv_w[l], alog16, dtb16, nw_group, chunk=ts, seq_block=8)
    h_s, xn_s, gates_s, meta_s, cnt_s = _outproj(
        x_sample.reshape(ns, D_MODEL), o_swa_s.reshape(ns, SWA_Q), o_gdn_s.reshape(ns, GDN_V),
        w_out_b, g_ffn, w_router, b_router)

    cnt_p_i = cnt_p[0, :N_EXPERTS].astype(I32)
    cnt_s_i = cnt_s[0, :N_EXPERTS].astype(I32)
    items, start = _work_items(cnt_p_i + cnt_s_i, 2 * (np_ + ns))
    expert_ids = jnp.arange(N_EXPERTS, dtype=I32)
    lookup = lambda table, ids: jnp.sum(jnp.where(ids[..., None] == expert_ids, table, 0), axis=-1)
    dest_p = lookup(start, meta_p[0:2].astype(I32)) + meta_p[2:4].astype(I32)
    dest_s = lookup(start + cnt_p_i, meta_s[0:2].astype(I32)) + meta_s[2:4].astype(I32)
    xs = _scatter(jnp.concatenate([dest_p, dest_s], axis=1), xn_p, xn_s)
    ys = _experts(items, xs, w_exp_gate[l], w_exp_up[l], w_exp_down[l])
    y_p = _combine(dest_p, gates_p, h_p, g_fin, ys)
    y_s = _combine(dest_s, gates_s, h_s, g_fin, ys)

    kv5 = lambda a, b: a.reshape(b, -1, SWA_KV_HEADS, HEAD_DIM)[None]
    return (y_p.reshape(bp, sp, D_MODEL), y_s.reshape(bs, ts, D_MODEL),
            kv5(k_p.reshape(bp, sp, SWA_KV)[:, -WINDOW:], bp), kv5(v_p.reshape(bp, sp, SWA_KV)[:, -WINDOW:], bp),
            kv5(kcache_s, bs), kv5(vcache_s, bs),
            c_p3[:, -(CONV_W - 1):][None], c_s3[:, -(CONV_W - 1):][None],
            s_fin_p[None], s_fin_s[None])
```

```python
import functools

import jax
import jax.numpy as jnp
from jax import lax
from jax.experimental import pallas as pl
from jax.experimental.pallas import tpu as pltpu

F32 = jnp.float32
BF16 = jnp.bfloat16
I32 = jnp.int32

D_MODEL = 1024
HEAD_DIM = 64
SWA_HEADS = 8
GDN_HEADS = 8
SWA_KV_HEADS = 2
GQA_GROUP = SWA_HEADS // SWA_KV_HEADS
WINDOW = 128
ATTN_SCALE = HEAD_DIM ** -0.5
CONV_W = 4
N_GROUPS = 8
EXPERTS_PER_GROUP = 8
N_EXPERTS = 64
D_EXPERT = 256
EPS = 1e-6

SWA_Q = SWA_HEADS * HEAD_DIM
SWA_KV = SWA_KV_HEADS * HEAD_DIM
GDN_QK = GDN_HEADS * HEAD_DIM
GDN_V = GDN_HEADS * HEAD_DIM
GDN_CONV_CH = 2 * GDN_QK + GDN_V
D_MIX = SWA_Q + GDN_V
D_IN = SWA_Q + 2 * SWA_KV + GDN_CONV_CH + GDN_V + 2 * GDN_HEADS
COL_K = SWA_Q
COL_V = COL_K + SWA_KV
COL_C = COL_V + SWA_KV
COL_Z = COL_C + GDN_CONV_CH
COL_AB = COL_Z + GDN_V

LANES = 128
NEG_BIG = -1e30
ROW_TILE = 512
MOE_BLOCK = 256
COMBINE_TILE = 256
DMA_UNROLL = 8
GDN_CHUNK = 64
GDN_TILE = 256
GDN_GROUP = 4
GDN_GROUP_W = GDN_GROUP * HEAD_DIM


def _rms(x, g):
    return x * lax.rsqrt(jnp.mean(x * x, axis=-1, keepdims=True) + EPS) * g


def _dot(a, b):
    return jnp.dot(a, b, preferred_element_type=F32)


def _dot_nt(a, b):
    return lax.dot_general(a, b, (((1,), (1,)), ((), ())), preferred_element_type=F32)


def _dot_tn(a, b):
    return lax.dot_general(a, b, (((0,), (0,)), ((), ())), preferred_element_type=F32)


def _split3(x):
    p1 = x.astype(BF16).astype(F32)
    r = x - p1
    p2 = r.astype(BF16).astype(F32)
    p3 = (r - p2).astype(BF16).astype(F32)
    return p1, p2, p3


def _inproj_kernel(x_ref, g_ref, w_ref, q_ref, k_ref, v_ref, c_ref, z_ref, ab_ref):
    x = x_ref[...]
    xb = _rms(x, g_ref[...]).astype(BF16)
    q_ref[...] = _dot(xb, w_ref[:, 0:COL_K])
    k_ref[...] = _dot(xb, w_ref[:, COL_K:COL_V])
    v_ref[...] = _dot(xb, w_ref[:, COL_V:COL_C])
    c_ref[...] = _dot(xb, w_ref[:, COL_C:COL_Z])
    z_ref[...] = _dot(xb, w_ref[:, COL_Z:COL_AB])
    ab_ref[...] = _dot(xb, w_ref[:, COL_AB:D_IN])


def _inproj(x2d, gain, w_bf16):
    n = x2d.shape[0]
    tm = ROW_TILE
    row = lambda w: pl.BlockSpec((tm, w), lambda i: (i, 0))
    full = lambda a: pl.BlockSpec(a.shape, lambda i: (0,) * a.ndim)
    widths = (SWA_Q, SWA_KV, SWA_KV, GDN_CONV_CH, GDN_V, 2 * GDN_HEADS)
    return pl.pallas_call(
        _inproj_kernel,
        grid=(n // tm,),
        in_specs=[row(D_MODEL), full(gain), full(w_bf16)],
        out_specs=[row(w) for w in widths],
        out_shape=[jax.ShapeDtypeStruct((n, w), F32) for w in widths],
        compiler_params=pltpu.CompilerParams(dimension_semantics=("arbitrary",)),
        name="inproj",
    )(x2d, gain, w_bf16)


def _swa_kernel(sink_ref, q_ref, kc_ref, vc_ref, kp_ref, vp_ref, o_ref, *cache_refs,
                bb, t, blocks_per_seq, emit_cache):
    if blocks_per_seq is None:
        has_prev = None
    else:
        has_prev = lax.rem(pl.program_id(0), blocks_per_seq) != 0
    rows = GQA_GROUP * t
    ri = lax.broadcasted_iota(I32, (rows, 1), 0)
    qi = lax.rem(ri, t)
    gi = ri // t
    if emit_cache:
        kj = lax.broadcasted_iota(I32, (rows, WINDOW), 1)
        mask = kj <= qi + (WINDOW - t)
    else:
        kj = lax.broadcasted_iota(I32, (rows, WINDOW + t), 1)
        mask = (kj <= qi + WINDOW) & (kj > qi)
        if has_prev is not None:
            mask = mask & (has_prev | (kj >= WINDOW))
    sinks = []
    for h in range(SWA_KV_HEADS):
        sink = jnp.zeros((rows, 1), F32)
        for g in range(GQA_GROUP):
            sink = jnp.where(gi == g, sink_ref[GQA_GROUP * h + g], sink)
        sinks.append(sink)
    chains = [(b, h) for b in range(bb) for h in range(SWA_KV_HEADS)]
    scores, values = [], []
    for b, h in chains:
        hs = slice(h * HEAD_DIM, (h + 1) * HEAD_DIM)
        kp, vp = kp_ref[b, :, hs], vp_ref[b, :, hs]
        kc, vc = kc_ref[b, :, hs], vc_ref[b, :, hs]
        if emit_cache:
            keys = jnp.concatenate([kp[t:], kc], axis=0)
            vals = jnp.concatenate([vp[t:], vc], axis=0)
            cache_refs[0][b, :, hs] = keys
            cache_refs[1][b, :, hs] = vals
        else:
            keys = jnp.concatenate([kp, kc], axis=0)
            vals = jnp.concatenate([vp, vc], axis=0)
        q4 = jnp.concatenate(
            [q_ref[b, :, (GQA_GROUP * h + g) * HEAD_DIM:(GQA_GROUP * h + g + 1) * HEAD_DIM]
             for g in range(GQA_GROUP)], axis=0)
        scores.append(_dot_nt(q4.astype(BF16), keys.astype(BF16)))
        values.append(vals.astype(BF16))
    probs, dens = [], []
    for (b, h), s in zip(chains, scores):
        s = jnp.where(mask, s * ATTN_SCALE, NEG_BIG)
        m = jnp.maximum(jnp.max(s, axis=-1, keepdims=True), sinks[h])
        p = jnp.exp(s - m)
        dens.append(jnp.sum(p, axis=-1, keepdims=True) + jnp.exp(sinks[h] - m))
        probs.append(p.astype(BF16))
    outs = [_dot(p, v) / den for p, v, den in zip(probs, values, dens)]
    for b in range(bb):
        o_ref[b] = jnp.concatenate([outs[b * SWA_KV_HEADS + h][g * t:(g + 1) * t]
                                    for h in range(SWA_KV_HEADS) for g in range(GQA_GROUP)], axis=-1)


def _swa(sinks, q3, k3, v3, kprev3, vprev3, *, bb, blocks_per_seq, emit_cache):
    nb, t, _ = q3.shape
    cur = lambda w: pl.BlockSpec((bb, t, w), lambda i: (i, 0, 0))
    if blocks_per_seq is None:
        prev = pl.BlockSpec((bb, WINDOW, SWA_KV), lambda i: (i, 0, 0))
    else:
        prev = pl.BlockSpec((bb, WINDOW, SWA_KV), lambda i: (jnp.maximum(i - 1, 0), 0, 0))
    out_specs = [cur(SWA_Q)]
    out_shape = [jax.ShapeDtypeStruct((nb, t, SWA_Q), F32)]
    if emit_cache:
        cache = pl.BlockSpec((bb, WINDOW, SWA_KV), lambda i: (i, 0, 0))
        out_specs += [cache, cache]
        out_shape += [jax.ShapeDtypeStruct((nb, WINDOW, SWA_KV), F32)] * 2
    return pl.pallas_call(
        functools.partial(_swa_kernel, bb=bb, t=t, blocks_per_seq=blocks_per_seq, emit_cache=emit_cache),
        grid=(nb // bb,),
        in_specs=[pl.BlockSpec(memory_space=pltpu.SMEM), cur(SWA_Q), cur(SWA_KV), cur(SWA_KV), prev, prev],
        out_specs=out_specs,
        out_shape=out_shape,
        compiler_params=pltpu.CompilerParams(dimension_semantics=("arbitrary",)),
        name="swa_cache" if emit_cache else "swa_band",
    )(sinks, q3, k3, v3, kprev3, vprev3)


def _gdn_prep_kernel(c_ref, hist_ref, ab_ref, cw_ref, alog_ref, dtb_ref,
                     u_ref, w_ref, qd_ref, kd_ref, qk_ref, gt_ref, cbuf_ref, *, chunk):
    sb, r, _ = c_ref.shape
    tp = sb * r
    cn = chunk
    low = w_ref.dtype
    hist_rows = CONV_W - 1

    @pl.when(pl.program_id(1) == 0)
    def _():
        cbuf_ref[:, 8 - hist_rows:8, :] = hist_ref[...]

    cbuf_ref[:, 8:8 + r, :] = c_ref[...]
    conv = cbuf_ref[:, 8 - hist_rows:8 - hist_rows + r, :] * cw_ref[0:1, :]
    for i in range(1, CONV_W):
        conv = conv + cbuf_ref[:, 8 - hist_rows + i:8 - hist_rows + i + r, :] * cw_ref[i:i + 1, :]
    tail = cbuf_ref[:, 8 + r - hist_rows:8 + r, :]
    cbuf_ref[:, 8 - hist_rows:8, :] = tail
    conv = (conv * jax.nn.sigmoid(conv)).reshape(tp, GDN_CONV_CH)

    ab = ab_ref[...].reshape(tp, 2 * GDN_HEADS)
    is_g = lax.broadcasted_iota(I32, (1, 2 * GDN_HEADS), 1) < GDN_HEADS
    g = jnp.where(is_g, -jnp.exp(alog_ref[...]) * jax.nn.softplus(ab + dtb_ref[...]), 0.0)
    beta = jax.nn.sigmoid(ab)

    ri = lax.broadcasted_iota(I32, (tp, tp), 0)
    ci = lax.broadcasted_iota(I32, (tp, tp), 1)
    same = (ri // cn) == (ci // cn)
    causal = same & (ri >= ci)
    strict = same & (ri > ci)
    eye = (ri == ci).astype(F32)
    stack = jnp.concatenate([causal.astype(BF16), same.astype(BF16)], axis=0)
    both = sum(_dot(stack, p.astype(BF16)) for p in _split3(g))
    gc, gl = both[:tp], both[tp:]
    e16 = lax.broadcasted_iota(I32, (2 * GDN_HEADS, 2 * GDN_HEADS), 0)
    eye16 = (e16 == lax.broadcasted_iota(I32, (2 * GDN_HEADS, 2 * GDN_HEADS), 1)).astype(BF16)
    gc_row = sum(_dot_nt(eye16, p.astype(BF16)) for p in _split3(gc))
    gt_ref[...] = jnp.exp(gl)
    fold = (lax.broadcasted_iota(I32, (tp, cn), 0) % cn == lax.broadcasted_iota(I32, (tp, cn), 1)).astype(low)

    levels = cn.bit_length() - 2
    us, ws, qds, kds, qks = [], [], [], [], []
    for h in range(GDN_HEADS):
        q = conv[:, h * HEAD_DIM:(h + 1) * HEAD_DIM]
        k = conv[:, GDN_QK + h * HEAD_DIM:GDN_QK + (h + 1) * HEAD_DIM]
        v = conv[:, 2 * GDN_QK + h * HEAD_DIM:2 * GDN_QK + (h + 1) * HEAD_DIM]
        q = q * lax.rsqrt(jnp.sum(q * q, axis=-1, keepdims=True) + EPS) * (HEAD_DIM ** -0.5)
        k = k * lax.rsqrt(jnp.sum(k * k, axis=-1, keepdims=True) + EPS)
        b_h = beta[:, GDN_HEADS + h:GDN_HEADS + h + 1]
        g_c = gc[:, h:h + 1]
        decay = jnp.exp(jnp.where(causal, g_c - gc_row[h:h + 1, :], NEG_BIG))
        eg = jnp.exp(g_c)
        kb = k * b_h
        k_l = k.astype(low)
        lmat = jnp.where(strict, _dot_nt(kb.astype(low), k_l) * decay, 0.0)
        qk = _dot_nt(q.astype(low), k_l) * decay
        qks.append(_dot(qk.astype(low), fold))
        l_l = lmat.astype(low)
        x = eye - lmat
        p = _dot(l_l, l_l)
        for lev in range(1, levels + 1):
            p_l = p.astype(low)
            if lev < levels:
                xp = _dot(jnp.concatenate([x.astype(low), p_l], axis=0), p_l)
                x = x + xp[:tp]
                p = xp[tp:]
            else:
                x = x + _dot(x.astype(low), p_l)
        rhs = jnp.concatenate([v * b_h, kb * eg], axis=-1)
        sol = _dot(x.astype(low), rhs.astype(low))
        us.append(sol[:, :HEAD_DIM])
        ws.append(sol[:, HEAD_DIM:])
        qds.append(q * eg)
        kds.append(k * jnp.exp(gl[:, h:h + 1] - g_c))
    u_ref[...] = jnp.concatenate(us, axis=-1)
    w_ref[...] = jnp.concatenate(ws, axis=-1).astype(low)
    qd_ref[...] = jnp.concatenate(qds, axis=-1).astype(low)
    kd_ref[...] = jnp.concatenate(kds, axis=-1).astype(low)
    qk_ref[...] = jnp.concatenate(qks, axis=-1).astype(low)


def _gdn_prep_pair_kernel(c_ref, hist_ref, ab_ref, cw_ref, alog_ref, dtb_ref,
                          u_ref, w_ref, qd_ref, kd_ref, qk_ref, gt_ref, cbuf_ref):
    _, r, _ = c_ref.shape
    tp = r
    cn = HEAD_DIM
    pair_w = 2 * HEAD_DIM
    hist_rows = CONV_W - 1

    @pl.when(pl.program_id(1) == 0)
    def _():
        cbuf_ref[:, 8 - hist_rows:8, :] = hist_ref[...]

    cbuf_ref[:, 8:8 + r, :] = c_ref[...]
    conv = cbuf_ref[:, 8 - hist_rows:8 - hist_rows + r, :] * cw_ref[0:1, :]
    for i in range(1, CONV_W):
        conv = conv + cbuf_ref[:, 8 - hist_rows + i:8 - hist_rows + i + r, :] * cw_ref[i:i + 1, :]
    tail = cbuf_ref[:, 8 + r - hist_rows:8 + r, :]
    cbuf_ref[:, 8 - hist_rows:8, :] = tail
    conv = (conv * jax.nn.sigmoid(conv)).reshape(tp, GDN_CONV_CH)

    ab = ab_ref[...].reshape(tp, 2 * GDN_HEADS)
    is_g = lax.broadcasted_iota(I32, (1, 2 * GDN_HEADS), 1) < GDN_HEADS
    g = jnp.where(is_g, -jnp.exp(alog_ref[...]) * jax.nn.softplus(ab + dtb_ref[...]), 0.0)
    beta = jax.nn.sigmoid(ab)

    ri = lax.broadcasted_iota(I32, (tp, tp), 0)
    ci = lax.broadcasted_iota(I32, (tp, tp), 1)
    same = (ri // cn) == (ci // cn)
    same_l = same.astype(BF16)
    stack = jnp.concatenate([(same & (ri >= ci)).astype(BF16), same_l], axis=0)
    both = sum(_dot(stack, p.astype(BF16)) for p in _split3(g))
    gc, gl = both[:tp], both[tp:]
    gt_ref[...] = jnp.exp(gl)

    lane = lax.broadcasted_iota(I32, (tp, pair_w), 1)
    c_in = lax.broadcasted_iota(I32, (tp, pair_w), 0) % cn
    j_in = lane % cn
    left = lane < cn
    causal = c_in >= j_in
    strict = c_in > j_in
    diag = c_in == j_in
    eye = diag.astype(F32)
    bdmask = ((lax.broadcasted_iota(I32, (pair_w, pair_w), 0) // cn)
              == (lax.broadcasted_iota(I32, (pair_w, pair_w), 1) // cn))
    ones_bd = bdmask.astype(BF16)

    def bd(m):
        return jnp.where(bdmask, jnp.concatenate([m, m], axis=0), 0.0).astype(BF16)

    def head_sum(x):
        hi = x.astype(BF16)
        lo = (x - hi.astype(F32)).astype(BF16)
        return _dot(hi, ones_bd) + _dot(lo, ones_bd)

    levels = cn.bit_length() - 2
    chunks = [slice(n * cn, (n + 1) * cn) for n in range(tp // cn)]
    n_pairs = GDN_HEADS // 2
    pairs = range(n_pairs)
    pick = lambda m, off, p: jnp.where(left, m[:, off + 2 * p:off + 2 * p + 1], m[:, off + 2 * p + 1:off + 2 * p + 2])
    third = lambda i, p: conv[:, i * GDN_QK + p * pair_w:i * GDN_QK + (p + 1) * pair_w]
    qs = [third(0, p) for p in pairs]
    ks = [third(1, p) for p in pairs]
    qs = [q * lax.rsqrt(head_sum(q * q) + EPS) * (HEAD_DIM ** -0.5) for q in qs]
    ks = [k * lax.rsqrt(head_sum(k * k) + EPS) for k in ks]
    gcps = [pick(gc, 0, p) for p in pairs]
    rowms = [sum(_dot(same_l, part.astype(BF16)) for part in _split3(jnp.where(diag, gcp, 0.0))) for gcp in gcps]
    decays = [jnp.exp(jnp.where(causal, gcp - rowm, NEG_BIG)) for gcp, rowm in zip(gcps, rowms)]
    lmats, vbs, kbegs = [], [], []
    for p in pairs:
        ls = slice(p * pair_w, (p + 1) * pair_w)
        q, k, gcp, decay = qs[p], ks[p], gcps[p], decays[p]
        bp = pick(beta, GDN_HEADS, p)
        eg = jnp.exp(gcp)
        kb = k * bp
        vbs.append(third(2, p) * bp)
        kbegs.append(kb * eg)
        qd_ref[:, ls] = (q * eg).astype(BF16)
        kd_ref[:, ls] = (k * jnp.exp(pick(gl, 0, p) - gcp)).astype(BF16)
        q_l, kb_l = q.astype(BF16), kb.astype(BF16)
        kbd = [bd(k[rs]) for rs in chunks]
        kk = jnp.concatenate([_dot_nt(kb_l[rs], kbd[n]) for n, rs in enumerate(chunks)], axis=0)
        qk = jnp.concatenate([_dot_nt(q_l[rs], kbd[n]) for n, rs in enumerate(chunks)], axis=0)
        qk_ref[:, ls] = (qk * decay).astype(BF16)
        lmats.append(jnp.where(strict, kk * decay, 0.0))
    bodies = [(p, rs) for p in range(n_pairs) for rs in chunks]
    xs = [eye[rs] - lmats[p][rs] for p, rs in bodies]
    ps = [_dot(lmats[p][rs].astype(BF16), bd(lmats[p][rs])) for p, rs in bodies]
    for lev in range(1, levels + 1):
        pbd = [bd(pm) for pm in ps]
        if lev < levels:
            xp = [_dot(jnp.concatenate([xm, pm], axis=0).astype(BF16), wm) for xm, pm, wm in zip(xs, ps, pbd)]
            xs = [xm + m[:cn] for xm, m in zip(xs, xp)]
            ps = [m[cn:] for m in xp]
        else:
            xs = [xm + _dot(xm.astype(BF16), wm) for xm, wm in zip(xs, pbd)]
    x_l = [xm.astype(BF16) for xm in xs]
    us = [_dot(xm, bd(vbs[p][rs])) for xm, (p, rs) in zip(x_l, bodies)]
    ws = [_dot(xm, bd(kbegs[p][rs])) for xm, (p, rs) in zip(x_l, bodies)]
    nc = len(chunks)
    for p in range(n_pairs):
        ls = slice(p * pair_w, (p + 1) * pair_w)
        u_ref[:, ls] = jnp.concatenate(us[p * nc:(p + 1) * nc], axis=0)
        w_ref[:, ls] = jnp.concatenate(ws[p * nc:(p + 1) * nc], axis=0).astype(BF16)


def _gdn_scan_kernel(u_ref, w_ref, qd_ref, kd_ref, qk_ref, gt_ref, z_ref, s0_ref, nw_ref,
                     o_ref, sfin_ref, sbd_ref, *, chunk, n_chunks):
    bb = u_ref.shape[0]
    cn = chunk
    low = w_ref.dtype
    gw = GDN_GROUP_W
    ni = pl.program_id(1)

    @pl.when(ni == 0)
    def _():
        sbd_ref[...] = jnp.zeros_like(sbd_ref)
        for b in range(bb):
            for h in range(GDN_HEADS):
                gi, hh = divmod(h, GDN_GROUP)
                ds = slice(hh * HEAD_DIM, (hh + 1) * HEAD_DIM)
                sbd_ref[b, gi, ds, ds] = s0_ref[b, h]

    bdmask = ((lax.broadcasted_iota(I32, (gw, gw), 0) // HEAD_DIM)
              == (lax.broadcasted_iota(I32, (gw, gw), 1) // HEAD_DIM))
    ones_bd = bdmask.astype(BF16)
    vmask = ((lax.broadcasted_iota(I32, (GDN_GROUP * cn, gw), 0) // cn)
             == (lax.broadcasted_iota(I32, (GDN_GROUP * cn, gw), 1) // HEAD_DIM))
    e_row = lax.broadcasted_iota(I32, (2 * GDN_HEADS, gw), 0)
    e_col = lax.broadcasted_iota(I32, (2 * GDN_HEADS, gw), 1) // HEAD_DIM
    chains = [(b, gi) for b in range(bb) for gi in range(GDN_HEADS // GDN_GROUP)]
    lanes = lambda gi: slice(gi * gw, (gi + 1) * gw)
    states = [sbd_ref[b, gi] for b, gi in chains]
    states_l = [s.astype(low) for s in states]
    v_new = [u_ref[b, :, lanes(gi)] - _dot(w_ref[b, :, lanes(gi)], s_l) for (b, gi), s_l in zip(chains, states_l)]
    q_s = [_dot(qd_ref[b, :, lanes(gi)], s_l) for (b, gi), s_l in zip(chains, states_l)]
    v_l = [v.astype(low) for v in v_new]
    outs = []
    for (b, gi), s, v, qs in zip(chains, states, v_l, q_s):
        vbd = jnp.where(vmask, jnp.concatenate([v] * GDN_GROUP, axis=0), jnp.zeros((), low))
        outs.append(qs + _dot(qk_ref[b, :, gi * GDN_GROUP * cn:(gi + 1) * GDN_GROUP * cn], vbd))
        upd = _dot_tn(kd_ref[b, :, lanes(gi)], v)
        expand = (e_row == e_col + gi * GDN_GROUP).astype(BF16)
        gte = sum(_dot(p.astype(BF16), expand) for p in _split3(gt_ref[b, 0:8, :]))[0:1]
        sbd_ref[b, gi] = s * gte + jnp.where(bdmask, upd, 0.0)
    for (b, gi), o in zip(chains, outs):
        o2 = o * o
        hi = o2.astype(BF16)
        lo = (o2 - hi.astype(F32)).astype(BF16)
        ms = (_dot(hi, ones_bd) + _dot(lo, ones_bd)) * (1.0 / HEAD_DIM)
        zg = z_ref[b, :, lanes(gi)]
        o_ref[b, :, lanes(gi)] = o * lax.rsqrt(ms + EPS) * nw_ref[...] * (zg * jax.nn.sigmoid(zg))

    @pl.when(ni == n_chunks - 1)
    def _():
        for b in range(bb):
            for h in range(GDN_HEADS):
                gi, hh = divmod(h, GDN_GROUP)
                ds = slice(hh * HEAD_DIM, (hh + 1) * HEAD_DIM)
                sfin_ref[b, h] = sbd_ref[b, gi, ds, ds]


def _gdn(c3, hist, z3, ab3, s0, conv_w, alog16, dtb16, nw_group, *, chunk, seq_block):
    nseq, t, _ = c3.shape
    n = nseq * t
    sb, r = (1, GDN_TILE) if t >= GDN_TILE else (GDN_TILE // t, t)
    tiles = t // r
    low = BF16 if chunk >= 16 else F32
    blk = lambda w: pl.BlockSpec((sb, r, w), lambda s, i: (s, i, 0))
    full = lambda a: pl.BlockSpec(a.shape, lambda s, i: (0,) * a.ndim)
    flat = lambda w: pl.BlockSpec((GDN_TILE, w), lambda s, i: (s * tiles + i, 0))
    widths = (GDN_V, GDN_V, GDN_QK, GDN_QK, GDN_HEADS * chunk, 2 * GDN_HEADS)
    dtypes = (F32, low, low, low, low, F32)
    prep_out = dict(out_specs=[flat(wd) for wd in widths],
                    out_shape=[jax.ShapeDtypeStruct((n, wd), dt) for wd, dt in zip(widths, dtypes)],
                    compiler_params=pltpu.CompilerParams(dimension_semantics=("arbitrary", "arbitrary")))
    lane_dense = chunk == HEAD_DIM and sb == 1
    u, w, qd, kd, qk, gt = pl.pallas_call(
        _gdn_prep_pair_kernel if lane_dense else functools.partial(_gdn_prep_kernel, chunk=chunk),
        grid=(nseq // sb, tiles),
        in_specs=[blk(GDN_CONV_CH), pl.BlockSpec((sb, CONV_W - 1, GDN_CONV_CH), lambda s, i: (s, 0, 0)),
                  blk(2 * GDN_HEADS), full(conv_w), full(alog16), full(dtb16)],
        scratch_shapes=[pltpu.VMEM((sb, 8 + r, GDN_CONV_CH), F32)],
        name="gdn_prep_pair" if lane_dense else "gdn_prep", **prep_out)(c3, hist, ab3, conv_w, alog16, dtb16)

    n_chunks = t // chunk
    tok = lambda wd: pl.BlockSpec((seq_block, chunk, wd), lambda s, c: (s, c, 0))
    per_seq = pl.BlockSpec((seq_block,) + s0.shape[1:], lambda s, c: (s, 0, 0, 0))
    seq3 = lambda a: a.reshape(nseq, t, a.shape[-1])
    return pl.pallas_call(
        functools.partial(_gdn_scan_kernel, chunk=chunk, n_chunks=n_chunks),
        grid=(nseq // seq_block, n_chunks),
        in_specs=[tok(wd) for wd in widths] + [tok(GDN_V), per_seq,
                                               pl.BlockSpec(nw_group.shape, lambda s, c: (0, 0))],
        out_specs=[tok(GDN_V), per_seq],
        out_shape=[jax.ShapeDtypeStruct((nseq, t, GDN_V), F32), jax.ShapeDtypeStruct(s0.shape, F32)],
        scratch_shapes=[pltpu.VMEM((seq_block, GDN_HEADS // GDN_GROUP, GDN_GROUP_W, GDN_GROUP_W), F32)],
        compiler_params=pltpu.CompilerParams(dimension_semantics=("arbitrary", "arbitrary")),
        name="gdn_scan",
    )(seq3(u), seq3(w), seq3(qd), seq3(kd), seq3(qk), seq3(gt), z3, s0, nw_group)


def _outproj_kernel(x_ref, osw_ref, ogd_ref, wo_ref, gf_ref, wr_ref, br_ref,
                    h_ref, xn_ref, gates_ref, meta_ref, cnt_ref, run_ref):
    i = pl.program_id(0)
    tm = x_ref.shape[0]

    @pl.when(i == 0)
    def _():
        run_ref[...] = jnp.zeros_like(run_ref)

    h = (x_ref[...] + _dot(osw_ref[...].astype(BF16), wo_ref[0:SWA_Q, :])
         + _dot(ogd_ref[...].astype(BF16), wo_ref[SWA_Q:D_MIX, :]))
    h_ref[...] = h
    xn = _rms(h, gf_ref[...])
    xn_ref[...] = xn
    logits = _dot(xn.astype(BF16), wr_ref[...])

    lane = lax.broadcasted_iota(I32, (tm, LANES), 1)
    bias = br_ref[...]
    is_g = (lane >= N_EXPERTS) & (lane < N_EXPERTS + N_GROUPS)
    lg = jnp.where(is_g, logits, NEG_BIG)
    pg = jnp.where(is_g, jnp.exp(lg - jnp.max(lg, axis=-1, keepdims=True)), 0.0)
    group_p = pg / jnp.sum(pg, axis=-1, keepdims=True)
    score_g = jnp.where(is_g, group_p + bias, NEG_BIG)
    g_lane = jnp.min(jnp.where(score_g == jnp.max(score_g, axis=-1, keepdims=True), lane, 2 * LANES),
                     axis=-1, keepdims=True)
    g_w = jnp.sum(jnp.where(lane == g_lane, group_p, 0.0), axis=-1, keepdims=True)
    sel = (lane < N_EXPERTS) & ((lane // EXPERTS_PER_GROUP) == (g_lane - N_EXPERTS))
    le = jnp.where(sel, logits, NEG_BIG)
    pe = jnp.where(sel, jnp.exp(le - jnp.max(le, axis=-1, keepdims=True)), 0.0)
    e_p = pe / jnp.sum(pe, axis=-1, keepdims=True)
    score = jnp.where(sel, e_p + bias, NEG_BIG)
    i1 = jnp.min(jnp.where(score == jnp.max(score, axis=-1, keepdims=True), lane, 2 * LANES),
                 axis=-1, keepdims=True)
    score2 = jnp.where(lane == i1, NEG_BIG, score)
    i2 = jnp.min(jnp.where(score2 == jnp.max(score2, axis=-1, keepdims=True), lane, 2 * LANES),
                 axis=-1, keepdims=True)
    w1 = jnp.sum(jnp.where(lane == i1, e_p, 0.0), axis=-1, keepdims=True)
    w2 = jnp.sum(jnp.where(lane == i2, e_p, 0.0), axis=-1, keepdims=True)
    wsum = w1 + w2

    oh1 = lane == i1
    oh2 = lane == i2
    ohs = (oh1 | oh2).astype(BF16)
    tri = (lax.broadcasted_iota(I32, (tm, tm), 0) > lax.broadcasted_iota(I32, (tm, tm), 1)).astype(BF16)
    before = _dot(tri, ohs) + run_ref[...]
    r1 = jnp.sum(jnp.where(oh1, before, 0.0), axis=-1, keepdims=True)
    r2 = jnp.sum(jnp.where(oh2, before, 0.0), axis=-1, keepdims=True)
    run_ref[...] = run_ref[...] + jnp.sum(ohs.astype(F32), axis=0, keepdims=True)
    cnt_ref[...] = run_ref[...]

    two = lax.broadcasted_iota(I32, (tm, 2), 1)
    gates_ref[...] = jnp.where(two == 0, g_w * (w1 / wsum), g_w * (w2 / wsum))
    packed = jnp.where(lane == 0, i1.astype(F32), jnp.where(lane == 1, i2.astype(F32),
                       jnp.where(lane == 2, r1, jnp.where(lane == 3, r2, 0.0))))
    eye8 = (lax.broadcasted_iota(I32, (8, LANES), 0) == lax.broadcasted_iota(I32, (8, LANES), 1)).astype(BF16)
    meta_ref[...] = sum(_dot_nt(eye8, part.astype(BF16)) for part in _split3(packed))


def _outproj(x2d, o_swa, o_gdn, wo_bf16, gain, w_router, b_router):
    n = x2d.shape[0]
    tm = ROW_TILE
    row = lambda w: pl.BlockSpec((tm, w), lambda i: (i, 0))
    full = lambda a: pl.BlockSpec(a.shape, lambda i: (0,) * a.ndim)
    return pl.pallas_call(
        _outproj_kernel,
        grid=(n // tm,),
        in_specs=[row(D_MODEL), row(SWA_Q), row(GDN_V), full(wo_bf16), full(gain), full(w_router), full(b_router)],
        out_specs=[row(D_MODEL), row(D_MODEL), row(2), pl.BlockSpec((8, tm), lambda i: (0, i)),
                   pl.BlockSpec((1, LANES), lambda i: (0, 0))],
        out_shape=[jax.ShapeDtypeStruct((n, D_MODEL), F32), jax.ShapeDtypeStruct((n, D_MODEL), F32),
                   jax.ShapeDtypeStruct((n, 2), F32), jax.ShapeDtypeStruct((8, n), F32),
                   jax.ShapeDtypeStruct((1, LANES), F32)],
        scratch_shapes=[pltpu.VMEM((1, LANES), F32)],
        compiler_params=pltpu.CompilerParams(dimension_semantics=("arbitrary",)),
        name="outproj_router",
    )(x2d, o_swa, o_gdn, wo_bf16, gain, w_router, b_router)


def _row_copy(src_ref, src_row, dst_ref, dst_row, sem):
    return pltpu.make_async_copy(src_ref.at[pl.ds(src_row, 1)], dst_ref.at[pl.ds(dst_row, 1)], sem)


def _scatter_kernel(dest_ref, xp_ref, xs_ref, out_ref, aidx_ref, sem, *, tiles_p):
    i = pl.program_id(0)
    tm = xp_ref.shape[0]

    def run(src_ref):
        def issue(g, carry):
            for u in range(DMA_UNROLL):
                r = g * DMA_UNROLL + u
                d0, d1 = dest_ref[0, r], dest_ref[1, r]
                _row_copy(src_ref, r, out_ref, d0, sem).start(priority=0)
                _row_copy(src_ref, r, out_ref, d1, sem).start(priority=1)
                aidx_ref[d0] = 2 * (i * tm + r)
                aidx_ref[d1] = 2 * (i * tm + r) + 1
            return carry

        lax.fori_loop(0, tm // DMA_UNROLL, issue, 0)
        for _ in range(2):
            pltpu.make_async_copy(src_ref, out_ref.at[pl.ds(0, tm)], sem).wait()

    @pl.when(i < tiles_p)
    def _():
        run(xp_ref)

    @pl.when(i >= tiles_p)
    def _():
        run(xs_ref)


def _scatter(dest3, xn_p, xn_s):
    tm = ROW_TILE
    tiles_p, tiles_s = xn_p.shape[0] // tm, xn_s.shape[0] // tm
    rows = 2 * (xn_p.shape[0] + xn_s.shape[0])
    return pl.pallas_call(
        functools.partial(_scatter_kernel, tiles_p=tiles_p),
        grid=(tiles_p + tiles_s,),
        in_specs=[pl.BlockSpec((2, tm), lambda i: (0, i), memory_space=pltpu.SMEM),
                  pl.BlockSpec((tm, D_MODEL), lambda i: (jnp.minimum(i, tiles_p - 1), 0)),
                  pl.BlockSpec((tm, D_MODEL), lambda i: (jnp.maximum(i - tiles_p, 0), 0))],
        out_specs=[pl.BlockSpec(memory_space=pl.ANY), pl.BlockSpec(memory_space=pltpu.SMEM)],
        out_shape=[jax.ShapeDtypeStruct((rows, D_MODEL), F32), jax.ShapeDtypeStruct((rows,), I32)],
        scratch_shapes=[pltpu.SemaphoreType.DMA(())],
        compiler_params=pltpu.CompilerParams(dimension_semantics=("arbitrary",)),
        name="scatter_rows",
    )(dest3, xn_p, xn_s)


def _experts_kernel(blk_ref, exp_ref, lo_ref, hi_ref, first_ref, fresh_ref, last_ref,
                    aidx_ref, x_ref, wg_ref, wu_ref, wd_ref, ya_ref, wg_l, wu_l, wd_l, ybuf, sem, *, n_items):
    j = pl.program_id(0)
    lo, hi = lo_ref[j], hi_ref[j]
    slot = lax.rem(blk_ref[j], 2)

    def wait_rows(s):
        pltpu.make_async_copy(ybuf.at[s], ya_ref.at[pl.ds(0, MOE_BLOCK)], sem.at[s]).wait()

    @pl.when(fresh_ref[j] == 1)
    def _():
        wg_l[...] = wg_ref[0].astype(BF16)
        wu_l[...] = wu_ref[0].astype(BF16)
        wd_l[...] = wd_ref[0].astype(BF16)

    @pl.when((first_ref[j] == 1) & (blk_ref[j] >= 2))
    def _():
        wait_rows(slot)

    def item(is_first, is_last):
        x = x_ref[...].astype(BF16)
        gate = _dot(x, wg_l[...])
        up = _dot(x, wu_l[...])
        hid = (gate * jax.nn.sigmoid(gate)) * up
        y = _dot(hid.astype(BF16), wd_l[...])
        r = lax.broadcasted_iota(I32, (MOE_BLOCK, 1), 0)
        mine = (r >= lo) & (r < hi)
        ybuf[slot] = jnp.where(mine, y, 0.0 if is_first else ybuf[slot])
        if is_last:
            for row in range(MOE_BLOCK):
                _row_copy(ybuf.at[slot], row, ya_ref, aidx_ref[0, 0, row], sem.at[slot]).start(priority=row % 2)

    for is_first in (0, 1):
        for is_last in (0, 1):
            pl.when((hi > lo) & (first_ref[j] == is_first) & (last_ref[j] == is_last))(
                functools.partial(item, is_first, is_last))

    @pl.when(j == n_items - 1)
    def _():
        wait_rows(0)
        wait_rows(1)


def _experts(items, aidx, xs, w_gate, w_up, w_down):
    n_items = items[0].shape[0]
    n_blocks = xs.shape[0] // MOE_BLOCK
    assert n_items > n_blocks + N_EXPERTS - 1 and n_blocks >= 2
    xblk = pl.BlockSpec((MOE_BLOCK, D_MODEL), lambda j, blk, *_: (blk[j], 0))
    wspec = lambda a: pl.BlockSpec((1,) + a.shape[1:], lambda j, blk, ex, *_: (ex[j], 0, 0))
    return pl.pallas_call(
        functools.partial(_experts_kernel, n_items=n_items),
        grid_spec=pltpu.PrefetchScalarGridSpec(
            num_scalar_prefetch=len(items),
            grid=(n_items,),
            in_specs=[pl.BlockSpec((1, 1, MOE_BLOCK), lambda j, blk, *_: (blk[j], 0, 0), memory_space=pltpu.SMEM),
                      xblk, wspec(w_gate), wspec(w_up), wspec(w_down)],
            out_specs=pl.BlockSpec(memory_space=pl.ANY),
            scratch_shapes=[pltpu.VMEM(w_gate.shape[1:], BF16), pltpu.VMEM(w_up.shape[1:], BF16),
                            pltpu.VMEM(w_down.shape[1:], BF16), pltpu.VMEM((2, MOE_BLOCK, D_MODEL), F32),
                            pltpu.SemaphoreType.DMA((2,))]),
        out_shape=jax.ShapeDtypeStruct(xs.shape, F32),
        compiler_params=pltpu.CompilerParams(dimension_semantics=("arbitrary",)),
        name="experts",
    )(*items, aidx.reshape(n_blocks, 1, MOE_BLOCK), xs, w_gate, w_up, w_down)


def _work_items(counts, total_rows):
    n_blocks = total_rows // MOE_BLOCK
    n_items = n_blocks + N_EXPERTS
    end = jnp.cumsum(counts)
    start = end - counts
    first_blk = start // MOE_BLOCK
    nb = jnp.where(counts > 0, (end - 1) // MOE_BLOCK - first_blk + 1, 0)
    item_end = jnp.cumsum(nb)
    used = item_end[-1]
    j = jnp.arange(n_items, dtype=I32)
    jj = jnp.minimum(j, used - 1)
    e = jnp.minimum(jnp.sum((item_end[None, :] <= jj[:, None]).astype(I32), axis=1), N_EXPERTS - 1)
    onehot = (e[:, None] == jnp.arange(N_EXPERTS, dtype=I32)[None, :]).astype(I32)
    pick = lambda a: jnp.sum(onehot * a[None, :], axis=1)
    blk = pick(first_blk) + (jj - (pick(item_end) - pick(nb)))
    lo = jnp.maximum(pick(start), blk * MOE_BLOCK) - blk * MOE_BLOCK
    hi = jnp.minimum(pick(end), (blk + 1) * MOE_BLOCK) - blk * MOE_BLOCK
    live = j < used
    lo = jnp.where(live, lo, 0)
    hi = jnp.where(live, hi, 0)
    prev_blk = jnp.concatenate([jnp.full((1,), -1, I32), blk[:-1]])
    first = (live & (blk != prev_blk)).astype(I32)
    prev_e = jnp.concatenate([jnp.full((1,), -1, I32), e[:-1]])
    fresh = (live & (e != prev_e)).astype(I32)
    next_blk = jnp.concatenate([blk[1:], jnp.full((1,), -1, I32)])
    last = (live & ((blk != next_blk) | (j == used - 1))).astype(I32)
    return (blk.astype(I32), e.astype(I32), lo.astype(I32), hi.astype(I32), first, fresh, last), start


def _combine_kernel(gates_ref, h_ref, gf_ref, ya_ref, out_ref):
    gates = gates_ref[...]
    y = h_ref[...] + (ya_ref[:, 0:D_MODEL] * gates[:, 0:1] + ya_ref[:, D_MODEL:2 * D_MODEL] * gates[:, 1:2])
    out_ref[...] = _rms(y, gf_ref[...])


def _combine(gates, h, gain, ya2, first_token):
    n = h.shape[0]
    tm = COMBINE_TILE
    tile0 = first_token // tm
    row = lambda w: pl.BlockSpec((tm, w), lambda i: (i, 0))
    return pl.pallas_call(
        _combine_kernel,
        grid=(n // tm,),
        in_specs=[row(2), row(D_MODEL), pl.BlockSpec(gain.shape, lambda i: (0, 0)),
                  pl.BlockSpec((tm, 2 * D_MODEL), lambda i: (i + tile0, 0))],
        out_specs=row(D_MODEL),
        out_shape=jax.ShapeDtypeStruct((n, D_MODEL), F32),
        compiler_params=pltpu.CompilerParams(dimension_semantics=("arbitrary",)),
        name="combine_norm",
    )(gates, h, gain, ya2)


def kernel(x_prompt, x_sample, cache_swa_k, cache_swa_v, state_gdn_conv, state_gdn, norm_mix, w_in, swa_sinks,
           gdn_conv_w, gdn_A_log, gdn_dt_bias, gdn_norm_w, w_out, norm_ffn, w_router_group, b_router_group,
           w_router_expert, b_router_expert, w_exp_gate, w_exp_up, w_exp_down, norm_final):
    depth = w_in.shape[0]
    assert depth == 1, "single trunk layer"
    bp, sp, _ = x_prompt.shape
    bs, ts, _ = x_sample.shape
    np_, ns = bp * sp, bs * ts
    l = 0

    w_in_b = w_in[l].astype(BF16)
    w_out_b = w_out[l].astype(BF16)
    g_mix = norm_mix[l].reshape(1, D_MODEL)
    g_ffn = norm_ffn[l].reshape(1, D_MODEL)
    g_fin = norm_final.reshape(1, D_MODEL)
    pad = LANES - N_EXPERTS - N_GROUPS
    w_router = jnp.concatenate([w_router_expert[l], w_router_group[l], jnp.zeros((D_MODEL, pad), F32)],
                               axis=1).astype(BF16)
    b_router = jnp.concatenate([b_router_expert[l].reshape(-1), b_router_group[l], jnp.zeros((pad,), F32)])[None]
    zeros8 = jnp.zeros((GDN_HEADS,), F32)
    alog16 = jnp.concatenate([gdn_A_log[l], zeros8])[None]
    dtb16 = jnp.concatenate([gdn_dt_bias[l], zeros8])[None]
    nw_group = jnp.tile(gdn_norm_w[l].reshape(1, HEAD_DIM), (1, GDN_GROUP))
    sinks = swa_sinks[l]

    q_p, k_p, v_p, c_p, z_p, ab_p = _inproj(x_prompt.reshape(np_, D_MODEL), g_mix, w_in_b)
    nblk = np_ // WINDOW
    k_p3 = k_p.reshape(nblk, WINDOW, SWA_KV)
    v_p3 = v_p.reshape(nblk, WINDOW, SWA_KV)
    (o_swa_p,) = _swa(sinks, q_p.reshape(nblk, WINDOW, SWA_Q), k_p3, v_p3, k_p3, v_p3,
                      bb=1, blocks_per_seq=sp // WINDOW, emit_cache=False)
    c_p3 = c_p.reshape(bp, sp, GDN_CONV_CH)
    o_gdn_p, s_fin_p = _gdn(c_p3, jnp.zeros((bp, CONV_W - 1, GDN_CONV_CH), F32), z_p.reshape(bp, sp, GDN_V),
                            ab_p.reshape(bp, sp, 2 * GDN_HEADS),
                            jnp.zeros((bp, GDN_HEADS, HEAD_DIM, HEAD_DIM), F32),
                            gdn_conv_w[l], alog16, dtb16, nw_group, chunk=GDN_CHUNK, seq_block=bp)
    h_p, xn_p, gates_p, meta_p, cnt_p = _outproj(
        x_prompt.reshape(np_, D_MODEL), o_swa_p.reshape(np_, SWA_Q), o_gdn_p.reshape(np_, GDN_V),
        w_out_b, g_ffn, w_router, b_router)

    q_s, k_s, v_s, c_s, z_s, ab_s = _inproj(x_sample.reshape(ns, D_MODEL), g_mix, w_in_b)
    o_swa_s, kcache_s, vcache_s = _swa(
        sinks, q_s.reshape(bs, ts, SWA_Q), k_s.reshape(bs, ts, SWA_KV), v_s.reshape(bs, ts, SWA_KV),
        cache_swa_k[l].reshape(bs, WINDOW, SWA_KV), cache_swa_v[l].reshape(bs, WINDOW, SWA_KV),
        bb=8, blocks_per_seq=None, emit_cache=True)
    c_s3 = c_s.reshape(bs, ts, GDN_CONV_CH)
    o_gdn_s, s_fin_s = _gdn(c_s3, state_gdn_conv[l], z_s.reshape(bs, ts, GDN_V),
                            ab_s.reshape(bs, ts, 2 * GDN_HEADS), state_gdn[l],
                            gdn_conv_w[l], alog16, dtb16, nw_group, chunk=ts, seq_block=8)
    h_s, xn_s, gates_s, meta_s, cnt_s = _outproj(
        x_sample.reshape(ns, D_MODEL), o_swa_s.reshape(ns, SWA_Q), o_gdn_s.reshape(ns, GDN_V),
        w_out_b, g_ffn, w_router, b_router)

    cnt_p_i = cnt_p[0, :N_EXPERTS].astype(I32)
    cnt_s_i = cnt_s[0, :N_EXPERTS].astype(I32)
    items, start = _work_items(cnt_p_i + cnt_s_i, 2 * (np_ + ns))
    expert_ids = jnp.arange(N_EXPERTS, dtype=I32)
    lookup = lambda table, ids: jnp.sum(jnp.where(ids[..., None] == expert_ids, table, 0), axis=-1)
    dest_p = lookup(start, meta_p[0:2].astype(I32)) + meta_p[2:4].astype(I32)
    dest_s = lookup(start + cnt_p_i, meta_s[0:2].astype(I32)) + meta_s[2:4].astype(I32)
    dest_all = jnp.concatenate([dest_p, dest_s], axis=1)
    xs, aidx = _scatter(dest_all, xn_p, xn_s)
    ya = _experts(items, aidx, xs, w_exp_gate[l], w_exp_up[l], w_exp_down[l])
    ya2 = ya.reshape(np_ + ns, 2 * D_MODEL)
    y_p = _combine(gates_p, h_p, g_fin, ya2, 0)
    y_s = _combine(gates_s, h_s, g_fin, ya2, np_)

    kv5 = lambda a, b: a.reshape(b, -1, SWA_KV_HEADS, HEAD_DIM)[None]
    return (y_p.reshape(bp, sp, D_MODEL), y_s.reshape(bs, ts, D_MODEL),
            kv5(k_p.reshape(bp, sp, SWA_KV)[:, -WINDOW:], bp), kv5(v_p.reshape(bp, sp, SWA_KV)[:, -WINDOW:], bp),
            kv5(kcache_s, bs), kv5(vcache_s, bs),
            c_p3[:, -(CONV_W - 1):][None], c_s3[:, -(CONV_W - 1):][None],
            s_fin_p[None], s_fin_s[None])
```

```python
import functools

import jax
import jax.numpy as jnp
from jax import lax
from jax.experimental import pallas as pl
from jax.experimental.pallas import tpu as pltpu

F32 = jnp.float32
BF16 = jnp.bfloat16
I32 = jnp.int32

D_MODEL = 1024
HEAD_DIM = 64
SWA_HEADS = 8
GDN_HEADS = 8
SWA_KV_HEADS = 2
GQA_GROUP = SWA_HEADS // SWA_KV_HEADS
WINDOW = 128
ATTN_SCALE = HEAD_DIM ** -0.5
CONV_W = 4
N_GROUPS = 8
EXPERTS_PER_GROUP = 8
N_EXPERTS = 64
D_EXPERT = 256
EPS = 1e-6

SWA_Q = SWA_HEADS * HEAD_DIM
SWA_KV = SWA_KV_HEADS * HEAD_DIM
GDN_QK = GDN_HEADS * HEAD_DIM
GDN_V = GDN_HEADS * HEAD_DIM
GDN_CONV_CH = 2 * GDN_QK + GDN_V
D_MIX = SWA_Q + GDN_V
D_IN = SWA_Q + 2 * SWA_KV + GDN_CONV_CH + GDN_V + 2 * GDN_HEADS
COL_K = SWA_Q
COL_V = COL_K + SWA_KV
COL_C = COL_V + SWA_KV
COL_Z = COL_C + GDN_CONV_CH
COL_AB = COL_Z + GDN_V

LANES = 128
NEG_BIG = -1e30
ROW_TILE = 512
MOE_BLOCK = 256
COMBINE_TILE = 256
DMA_UNROLL = 8
GDN_CHUNK = 64
GDN_TILE = 256
GDN_GROUP = 4
GDN_GROUP_W = GDN_GROUP * HEAD_DIM


def _rms(x, g):
    return x * lax.rsqrt(jnp.mean(x * x, axis=-1, keepdims=True) + EPS) * g


def _dot(a, b):
    return jnp.dot(a, b, preferred_element_type=F32)


def _dot_nt(a, b):
    return lax.dot_general(a, b, (((1,), (1,)), ((), ())), preferred_element_type=F32)


def _dot_tn(a, b):
    return lax.dot_general(a, b, (((0,), (0,)), ((), ())), preferred_element_type=F32)


def _split3(x):
    p1 = x.astype(BF16).astype(F32)
    r = x - p1
    p2 = r.astype(BF16).astype(F32)
    p3 = (r - p2).astype(BF16).astype(F32)
    return p1, p2, p3


def _inproj_kernel(x_ref, g_ref, w_ref, q_ref, k_ref, v_ref, c_ref, z_ref, ab_ref):
    x = x_ref[...]
    xb = _rms(x, g_ref[...]).astype(BF16)
    q_ref[...] = _dot(xb, w_ref[:, 0:COL_K])
    k_ref[...] = _dot(xb, w_ref[:, COL_K:COL_V])
    v_ref[...] = _dot(xb, w_ref[:, COL_V:COL_C])
    c_ref[...] = _dot(xb, w_ref[:, COL_C:COL_Z])
    z_ref[...] = _dot(xb, w_ref[:, COL_Z:COL_AB])
    ab_ref[...] = _dot(xb, w_ref[:, COL_AB:D_IN])


def _inproj(x2d, gain, w_bf16):
    n = x2d.shape[0]
    tm = ROW_TILE
    row = lambda w: pl.BlockSpec((tm, w), lambda i: (i, 0))
    full = lambda a: pl.BlockSpec(a.shape, lambda i: (0,) * a.ndim)
    widths = (SWA_Q, SWA_KV, SWA_KV, GDN_CONV_CH, GDN_V, 2 * GDN_HEADS)
    return pl.pallas_call(
        _inproj_kernel,
        grid=(n // tm,),
        in_specs=[row(D_MODEL), full(gain), full(w_bf16)],
        out_specs=[row(w) for w in widths],
        out_shape=[jax.ShapeDtypeStruct((n, w), F32) for w in widths],
        compiler_params=pltpu.CompilerParams(dimension_semantics=("arbitrary",)),
        name="inproj",
    )(x2d, gain, w_bf16)


def _swa_kernel(sink_ref, q_ref, kc_ref, vc_ref, kp_ref, vp_ref, o_ref, *cache_refs,
                bb, t, blocks_per_seq, emit_cache):
    if blocks_per_seq is None:
        has_prev = None
    else:
        has_prev = lax.rem(pl.program_id(0), blocks_per_seq) != 0
    rows = GQA_GROUP * t
    ri = lax.broadcasted_iota(I32, (rows, 1), 0)
    qi = lax.rem(ri, t)
    gi = ri // t
    if emit_cache:
        kj = lax.broadcasted_iota(I32, (rows, WINDOW), 1)
        mask = kj <= qi + (WINDOW - t)
    else:
        kj = lax.broadcasted_iota(I32, (rows, WINDOW + t), 1)
        mask = (kj <= qi + WINDOW) & (kj > qi)
        if has_prev is not None:
            mask = mask & (has_prev | (kj >= WINDOW))
    sinks = []
    for h in range(SWA_KV_HEADS):
        sink = jnp.zeros((rows, 1), F32)
        for g in range(GQA_GROUP):
            sink = jnp.where(gi == g, sink_ref[GQA_GROUP * h + g], sink)
        sinks.append(sink)
    chains = [(b, h) for b in range(bb) for h in range(SWA_KV_HEADS)]
    scores, values = [], []
    for b, h in chains:
        hs = slice(h * HEAD_DIM, (h + 1) * HEAD_DIM)
        kp, vp = kp_ref[b, :, hs], vp_ref[b, :, hs]
        kc, vc = kc_ref[b, :, hs], vc_ref[b, :, hs]
        if emit_cache:
            keys = jnp.concatenate([kp[t:], kc], axis=0)
            vals = jnp.concatenate([vp[t:], vc], axis=0)
            cache_refs[0][b, :, hs] = keys
            cache_refs[1][b, :, hs] = vals
        else:
            keys = jnp.concatenate([kp, kc], axis=0)
            vals = jnp.concatenate([vp, vc], axis=0)
        q4 = jnp.concatenate(
            [q_ref[b, :, (GQA_GROUP * h + g) * HEAD_DIM:(GQA_GROUP * h + g + 1) * HEAD_DIM]
             for g in range(GQA_GROUP)], axis=0)
        scores.append(_dot_nt(q4.astype(BF16), keys.astype(BF16)))
        values.append(vals.astype(BF16))
    probs, dens = [], []
    for (b, h), s in zip(chains, scores):
        s = jnp.where(mask, s * ATTN_SCALE, NEG_BIG)
        m = jnp.maximum(jnp.max(s, axis=-1, keepdims=True), sinks[h])
        p = jnp.exp(s - m)
        dens.append(jnp.sum(p, axis=-1, keepdims=True) + jnp.exp(sinks[h] - m))
        probs.append(p.astype(BF16))
    outs = [_dot(p, v) / den for p, v, den in zip(probs, values, dens)]
    for b in range(bb):
        o_ref[b] = jnp.concatenate([outs[b * SWA_KV_HEADS + h][g * t:(g + 1) * t]
                                    for h in range(SWA_KV_HEADS) for g in range(GQA_GROUP)], axis=-1)


def _swa(sinks, q3, k3, v3, kprev3, vprev3, *, bb, blocks_per_seq, emit_cache):
    nb, t, _ = q3.shape
    cur = lambda w: pl.BlockSpec((bb, t, w), lambda i: (i, 0, 0))
    if blocks_per_seq is None:
        prev = pl.BlockSpec((bb, WINDOW, SWA_KV), lambda i: (i, 0, 0))
    else:
        prev = pl.BlockSpec((bb, WINDOW, SWA_KV), lambda i: (jnp.maximum(i - 1, 0), 0, 0))
    out_specs = [cur(SWA_Q)]
    out_shape = [jax.ShapeDtypeStruct((nb, t, SWA_Q), F32)]
    if emit_cache:
        cache = pl.BlockSpec((bb, WINDOW, SWA_KV), lambda i: (i, 0, 0))
        out_specs += [cache, cache]
        out_shape += [jax.ShapeDtypeStruct((nb, WINDOW, SWA_KV), F32)] * 2
    return pl.pallas_call(
        functools.partial(_swa_kernel, bb=bb, t=t, blocks_per_seq=blocks_per_seq, emit_cache=emit_cache),
        grid=(nb // bb,),
        in_specs=[pl.BlockSpec(memory_space=pltpu.SMEM), cur(SWA_Q), cur(SWA_KV), cur(SWA_KV), prev, prev],
        out_specs=out_specs,
        out_shape=out_shape,
        compiler_params=pltpu.CompilerParams(dimension_semantics=("arbitrary",)),
        name="swa_cache" if emit_cache else "swa_band",
    )(sinks, q3, k3, v3, kprev3, vprev3)


def _gdn_prep_kernel(c_ref, hist_ref, ab_ref, cw_ref, alog_ref, dtb_ref,
                     u_ref, w_ref, qd_ref, kd_ref, qk_ref, gt_ref, cbuf_ref, *, chunk):
    sb, r, _ = c_ref.shape
    tp = sb * r
    cn = chunk
    low = w_ref.dtype
    hist_rows = CONV_W - 1

    @pl.when(pl.program_id(1) == 0)
    def _():
        cbuf_ref[:, 8 - hist_rows:8, :] = hist_ref[...]

    cbuf_ref[:, 8:8 + r, :] = c_ref[...]
    conv = cbuf_ref[:, 8 - hist_rows:8 - hist_rows + r, :] * cw_ref[0:1, :]
    for i in range(1, CONV_W):
        conv = conv + cbuf_ref[:, 8 - hist_rows + i:8 - hist_rows + i + r, :] * cw_ref[i:i + 1, :]
    tail = cbuf_ref[:, 8 + r - hist_rows:8 + r, :]
    cbuf_ref[:, 8 - hist_rows:8, :] = tail
    conv = (conv * jax.nn.sigmoid(conv)).reshape(tp, GDN_CONV_CH)

    ab = ab_ref[...].reshape(tp, 2 * GDN_HEADS)
    is_g = lax.broadcasted_iota(I32, (1, 2 * GDN_HEADS), 1) < GDN_HEADS
    g = jnp.where(is_g, -jnp.exp(alog_ref[...]) * jax.nn.softplus(ab + dtb_ref[...]), 0.0)
    beta = jax.nn.sigmoid(ab)

    ri = lax.broadcasted_iota(I32, (tp, tp), 0)
    ci = lax.broadcasted_iota(I32, (tp, tp), 1)
    same = (ri // cn) == (ci // cn)
    causal = same & (ri >= ci)
    strict = same & (ri > ci)
    eye = (ri == ci).astype(F32)
    stack = jnp.concatenate([causal.astype(BF16), same.astype(BF16)], axis=0)
    both = sum(_dot(stack, p.astype(BF16)) for p in _split3(g))
    gc, gl = both[:tp], both[tp:]
    e16 = lax.broadcasted_iota(I32, (2 * GDN_HEADS, 2 * GDN_HEADS), 0)
    eye16 = (e16 == lax.broadcasted_iota(I32, (2 * GDN_HEADS, 2 * GDN_HEADS), 1)).astype(BF16)
    gc_row = sum(_dot_nt(eye16, p.astype(BF16)) for p in _split3(gc))
    gt_ref[...] = jnp.exp(gl)
    fold = (lax.broadcasted_iota(I32, (tp, cn), 0) % cn == lax.broadcasted_iota(I32, (tp, cn), 1)).astype(low)

    levels = cn.bit_length() - 2
    us, ws, qds, kds, qks = [], [], [], [], []
    for h in range(GDN_HEADS):
        q = conv[:, h * HEAD_DIM:(h + 1) * HEAD_DIM]
        k = conv[:, GDN_QK + h * HEAD_DIM:GDN_QK + (h + 1) * HEAD_DIM]
        v = conv[:, 2 * GDN_QK + h * HEAD_DIM:2 * GDN_QK + (h + 1) * HEAD_DIM]
        q = q * lax.rsqrt(jnp.sum(q * q, axis=-1, keepdims=True) + EPS) * (HEAD_DIM ** -0.5)
        k = k * lax.rsqrt(jnp.sum(k * k, axis=-1, keepdims=True) + EPS)
        b_h = beta[:, GDN_HEADS + h:GDN_HEADS + h + 1]
        g_c = gc[:, h:h + 1]
        decay = jnp.exp(jnp.where(causal, g_c - gc_row[h:h + 1, :], NEG_BIG))
        eg = jnp.exp(g_c)
        kb = k * b_h
        k_l = k.astype(low)
        lmat = jnp.where(strict, _dot_nt(kb.astype(low), k_l) * decay, 0.0)
        qk = _dot_nt(q.astype(low), k_l) * decay
        qks.append(_dot(qk.astype(low), fold))
        l_l = lmat.astype(low)
        x = eye - lmat
        p = _dot(l_l, l_l)
        for lev in range(1, levels + 1):
            p_l = p.astype(low)
            if lev < levels:
                xp = _dot(jnp.concatenate([x.astype(low), p_l], axis=0), p_l)
                x = x + xp[:tp]
                p = xp[tp:]
            else:
                x = x + _dot(x.astype(low), p_l)
        rhs = jnp.concatenate([v * b_h, kb * eg], axis=-1)
        sol = _dot(x.astype(low), rhs.astype(low))
        us.append(sol[:, :HEAD_DIM])
        ws.append(sol[:, HEAD_DIM:])
        qds.append(q * eg)
        kds.append(k * jnp.exp(gl[:, h:h + 1] - g_c))
    u_ref[...] = jnp.concatenate(us, axis=-1)
    w_ref[...] = jnp.concatenate(ws, axis=-1).astype(low)
    qd_ref[...] = jnp.concatenate(qds, axis=-1).astype(low)
    kd_ref[...] = jnp.concatenate(kds, axis=-1).astype(low)
    qk_ref[...] = jnp.concatenate(qks, axis=-1).astype(low)


def _gdn_prep_pair_kernel(c_ref, hist_ref, ab_ref, cw_ref, alog_ref, dtb_ref,
                          u_ref, w_ref, qd_ref, kd_ref, qk_ref, gt_ref, cbuf_ref):
    _, r, _ = c_ref.shape
    tp = r
    cn = HEAD_DIM
    pair_w = 2 * HEAD_DIM
    hist_rows = CONV_W - 1

    @pl.when(pl.program_id(1) == 0)
    def _():
        cbuf_ref[:, 8 - hist_rows:8, :] = hist_ref[...]

    cbuf_ref[:, 8:8 + r, :] = c_ref[...]
    conv = cbuf_ref[:, 8 - hist_rows:8 - hist_rows + r, :] * cw_ref[0:1, :]
    for i in range(1, CONV_W):
        conv = conv + cbuf_ref[:, 8 - hist_rows + i:8 - hist_rows + i + r, :] * cw_ref[i:i + 1, :]
    tail = cbuf_ref[:, 8 + r - hist_rows:8 + r, :]
    cbuf_ref[:, 8 - hist_rows:8, :] = tail
    conv = (conv * jax.nn.sigmoid(conv)).reshape(tp, GDN_CONV_CH)

    ab = ab_ref[...].reshape(tp, 2 * GDN_HEADS)
    is_g = lax.broadcasted_iota(I32, (1, 2 * GDN_HEADS), 1) < GDN_HEADS
    g = jnp.where(is_g, -jnp.exp(alog_ref[...]) * jax.nn.softplus(ab + dtb_ref[...]), 0.0)
    beta = jax.nn.sigmoid(ab)

    ri = lax.broadcasted_iota(I32, (tp, tp), 0)
    ci = lax.broadcasted_iota(I32, (tp, tp), 1)
    same = (ri // cn) == (ci // cn)
    same_l = same.astype(BF16)
    stack = jnp.concatenate([(same & (ri >= ci)).astype(BF16), same_l], axis=0)
    both = sum(_dot(stack, p.astype(BF16)) for p in _split3(g))
    gc, gl = both[:tp], both[tp:]
    gt_ref[...] = jnp.exp(gl)

    lane = lax.broadcasted_iota(I32, (tp, pair_w), 1)
    c_in = lax.broadcasted_iota(I32, (tp, pair_w), 0) % cn
    j_in = lane % cn
    left = lane < cn
    causal = c_in >= j_in
    strict = c_in > j_in
    diag = c_in == j_in
    eye = diag.astype(F32)
    bdmask = ((lax.broadcasted_iota(I32, (pair_w, pair_w), 0) // cn)
              == (lax.broadcasted_iota(I32, (pair_w, pair_w), 1) // cn))
    ones_bd = bdmask.astype(BF16)

    def bd(m):
        return jnp.where(bdmask, jnp.concatenate([m, m], axis=0), 0.0).astype(BF16)

    def head_sum(x):
        hi = x.astype(BF16)
        lo = (x - hi.astype(F32)).astype(BF16)
        return _dot(hi, ones_bd) + _dot(lo, ones_bd)

    levels = cn.bit_length() - 2
    chunks = [slice(n * cn, (n + 1) * cn) for n in range(tp // cn)]
    n_pairs = GDN_HEADS // 2
    pairs = range(n_pairs)
    pick = lambda m, off, p: jnp.where(left, m[:, off + 2 * p:off + 2 * p + 1], m[:, off + 2 * p + 1:off + 2 * p + 2])
    third = lambda i, p: conv[:, i * GDN_QK + p * pair_w:i * GDN_QK + (p + 1) * pair_w]
    qs = [third(0, p) for p in pairs]
    ks = [third(1, p) for p in pairs]
    qs = [q * lax.rsqrt(head_sum(q * q) + EPS) * (HEAD_DIM ** -0.5) for q in qs]
    ks = [k * lax.rsqrt(head_sum(k * k) + EPS) for k in ks]
    gcps = [pick(gc, 0, p) for p in pairs]
    rowms = [sum(_dot(same_l, part.astype(BF16)) for part in _split3(jnp.where(diag, gcp, 0.0))) for gcp in gcps]
    decays = [jnp.exp(jnp.where(causal, gcp - rowm, NEG_BIG)) for gcp, rowm in zip(gcps, rowms)]
    lmats, vbs, kbegs = [], [], []
    for p in pairs:
        ls = slice(p * pair_w, (p + 1) * pair_w)
        q, k, gcp, decay = qs[p], ks[p], gcps[p], decays[p]
        bp = pick(beta, GDN_HEADS, p)
        eg = jnp.exp(gcp)
        kb = k * bp
        vbs.append(third(2, p) * bp)
        kbegs.append(kb * eg)
        qd_ref[:, ls] = (q * eg).astype(BF16)
        kd_ref[:, ls] = (k * jnp.exp(pick(gl, 0, p) - gcp)).astype(BF16)
        q_l, kb_l = q.astype(BF16), kb.astype(BF16)
        kbd = [bd(k[rs]) for rs in chunks]
        kk = jnp.concatenate([_dot_nt(kb_l[rs], kbd[n]) for n, rs in enumerate(chunks)], axis=0)
        qk = jnp.concatenate([_dot_nt(q_l[rs], kbd[n]) for n, rs in enumerate(chunks)], axis=0)
        qk_ref[:, ls] = (qk * decay).astype(BF16)
        lmats.append(jnp.where(strict, kk * decay, 0.0))
    bodies = [(p, rs) for p in range(n_pairs) for rs in chunks]
    xs = [eye[rs] - lmats[p][rs] for p, rs in bodies]
    ps = [_dot(lmats[p][rs].astype(BF16), bd(lmats[p][rs])) for p, rs in bodies]
    for lev in range(1, levels + 1):
        pbd = [bd(pm) for pm in ps]
        if lev < levels:
            xp = [_dot(jnp.concatenate([xm, pm], axis=0).astype(BF16), wm) for xm, pm, wm in zip(xs, ps, pbd)]
            xs = [xm + m[:cn] for xm, m in zip(xs, xp)]
            ps = [m[cn:] for m in xp]
        else:
            xs = [xm + _dot(xm.astype(BF16), wm) for xm, wm in zip(xs, pbd)]
    x_l = [xm.astype(BF16) for xm in xs]
    us = [_dot(xm, bd(vbs[p][rs])) for xm, (p, rs) in zip(x_l, bodies)]
    ws = [_dot(xm, bd(kbegs[p][rs])) for xm, (p, rs) in zip(x_l, bodies)]
    nc = len(chunks)
    for p in range(n_pairs):
        ls = slice(p * pair_w, (p + 1) * pair_w)
        u_ref[:, ls] = jnp.concatenate(us[p * nc:(p + 1) * nc], axis=0)
        w_ref[:, ls] = jnp.concatenate(ws[p * nc:(p + 1) * nc], axis=0).astype(BF16)


def _gdn_scan_kernel(u_ref, w_ref, qd_ref, kd_ref, qk_ref, gt_ref, z_ref, s0_ref, nw_ref,
                     o_ref, sfin_ref, sbd_ref, *, chunk, n_chunks):
    bb = u_ref.shape[0]
    cn = chunk
    low = w_ref.dtype
    gw = GDN_GROUP_W
    ni = pl.program_id(1)

    @pl.when(ni == 0)
    def _():
        sbd_ref[...] = jnp.zeros_like(sbd_ref)
        for b in range(bb):
            for h in range(GDN_HEADS):
                gi, hh = divmod(h, GDN_GROUP)
                ds = slice(hh * HEAD_DIM, (hh + 1) * HEAD_DIM)
                sbd_ref[b, gi, ds, ds] = s0_ref[b, h]

    bdmask = ((lax.broadcasted_iota(I32, (gw, gw), 0) // HEAD_DIM)
              == (lax.broadcasted_iota(I32, (gw, gw), 1) // HEAD_DIM))
    ones_bd = bdmask.astype(BF16)
    vmask = ((lax.broadcasted_iota(I32, (GDN_GROUP * cn, gw), 0) // cn)
             == (lax.broadcasted_iota(I32, (GDN_GROUP * cn, gw), 1) // HEAD_DIM))
    e_row = lax.broadcasted_iota(I32, (2 * GDN_HEADS, gw), 0)
    e_col = lax.broadcasted_iota(I32, (2 * GDN_HEADS, gw), 1) // HEAD_DIM
    chains = [(b, gi) for b in range(bb) for gi in range(GDN_HEADS // GDN_GROUP)]
    lanes = lambda gi: slice(gi * gw, (gi + 1) * gw)
    states = [sbd_ref[b, gi] for b, gi in chains]
    states_l = [s.astype(low) for s in states]
    v_new = [u_ref[b, :, lanes(gi)] - _dot(w_ref[b, :, lanes(gi)], s_l) for (b, gi), s_l in zip(chains, states_l)]
    q_s = [_dot(qd_ref[b, :, lanes(gi)], s_l) for (b, gi), s_l in zip(chains, states_l)]
    v_l = [v.astype(low) for v in v_new]
    outs = []
    for (b, gi), s, v, qs in zip(chains, states, v_l, q_s):
        vbd = jnp.where(vmask, jnp.concatenate([v] * GDN_GROUP, axis=0), jnp.zeros((), low))
        outs.append(qs + _dot(qk_ref[b, :, gi * GDN_GROUP * cn:(gi + 1) * GDN_GROUP * cn], vbd))
        upd = _dot_tn(kd_ref[b, :, lanes(gi)], v)
        expand = (e_row == e_col + gi * GDN_GROUP).astype(BF16)
        gte = sum(_dot(p.astype(BF16), expand) for p in _split3(gt_ref[b, 0:8, :]))[0:1]
        sbd_ref[b, gi] = s * gte + jnp.where(bdmask, upd, 0.0)
    for (b, gi), o in zip(chains, outs):
        o2 = o * o
        hi = o2.astype(BF16)
        lo = (o2 - hi.astype(F32)).astype(BF16)
        ms = (_dot(hi, ones_bd) + _dot(lo, ones_bd)) * (1.0 / HEAD_DIM)
        zg = z_ref[b, :, lanes(gi)]
        o_ref[b, :, lanes(gi)] = o * lax.rsqrt(ms + EPS) * nw_ref[...] * (zg * jax.nn.sigmoid(zg))

    @pl.when(ni == n_chunks - 1)
    def _():
        for b in range(bb):
            for h in range(GDN_HEADS):
                gi, hh = divmod(h, GDN_GROUP)
                ds = slice(hh * HEAD_DIM, (hh + 1) * HEAD_DIM)
                sfin_ref[b, h] = sbd_ref[b, gi, ds, ds]


def _gdn(c3, hist, z3, ab3, s0, conv_w, alog16, dtb16, nw_group, *, chunk, seq_block):
    nseq, t, _ = c3.shape
    n = nseq * t
    sb, r = (1, GDN_TILE) if t >= GDN_TILE else (GDN_TILE // t, t)
    tiles = t // r
    low = BF16 if chunk >= 16 else F32
    blk = lambda w: pl.BlockSpec((sb, r, w), lambda s, i: (s, i, 0))
    full = lambda a: pl.BlockSpec(a.shape, lambda s, i: (0,) * a.ndim)
    flat = lambda w: pl.BlockSpec((GDN_TILE, w), lambda s, i: (s * tiles + i, 0))
    widths = (GDN_V, GDN_V, GDN_QK, GDN_QK, GDN_HEADS * chunk, 2 * GDN_HEADS)
    dtypes = (F32, low, low, low, low, F32)
    prep_out = dict(out_specs=[flat(wd) for wd in widths],
                    out_shape=[jax.ShapeDtypeStruct((n, wd), dt) for wd, dt in zip(widths, dtypes)],
                    compiler_params=pltpu.CompilerParams(dimension_semantics=("arbitrary", "arbitrary")))
    lane_dense = chunk == HEAD_DIM and sb == 1
    u, w, qd, kd, qk, gt = pl.pallas_call(
        _gdn_prep_pair_kernel if lane_dense else functools.partial(_gdn_prep_kernel, chunk=chunk),
        grid=(nseq // sb, tiles),
        in_specs=[blk(GDN_CONV_CH), pl.BlockSpec((sb, CONV_W - 1, GDN_CONV_CH), lambda s, i: (s, 0, 0)),
                  blk(2 * GDN_HEADS), full(conv_w), full(alog16), full(dtb16)],
        scratch_shapes=[pltpu.VMEM((sb, 8 + r, GDN_CONV_CH), F32)],
        name="gdn_prep_pair" if lane_dense else "gdn_prep", **prep_out)(c3, hist, ab3, conv_w, alog16, dtb16)

    n_chunks = t // chunk
    tok = lambda wd: pl.BlockSpec((seq_block, chunk, wd), lambda s, c: (s, c, 0))
    per_seq = pl.BlockSpec((seq_block,) + s0.shape[1:], lambda s, c: (s, 0, 0, 0))
    seq3 = lambda a: a.reshape(nseq, t, a.shape[-1])
    return pl.pallas_call(
        functools.partial(_gdn_scan_kernel, chunk=chunk, n_chunks=n_chunks),
        grid=(nseq // seq_block, n_chunks),
        in_specs=[tok(wd) for wd in widths] + [tok(GDN_V), per_seq,
                                               pl.BlockSpec(nw_group.shape, lambda s, c: (0, 0))],
        out_specs=[tok(GDN_V), per_seq],
        out_shape=[jax.ShapeDtypeStruct((nseq, t, GDN_V), F32), jax.ShapeDtypeStruct(s0.shape, F32)],
        scratch_shapes=[pltpu.VMEM((seq_block, GDN_HEADS // GDN_GROUP, GDN_GROUP_W, GDN_GROUP_W), F32)],
        compiler_params=pltpu.CompilerParams(dimension_semantics=("arbitrary", "arbitrary")),
        name="gdn_scan",
    )(seq3(u), seq3(w), seq3(qd), seq3(kd), seq3(qk), seq3(gt), z3, s0, nw_group)


def _outproj_kernel(x_ref, osw_ref, ogd_ref, wo_ref, gf_ref, wr_ref, br_ref,
                    h_ref, xn_ref, gates_ref, meta_ref, cnt_ref, run_ref):
    i = pl.program_id(0)
    tm = x_ref.shape[0]
    rows = wr_ref.shape[0]

    @pl.when(i == 0)
    def _():
        run_ref[...] = jnp.zeros_like(run_ref)

    h = (x_ref[...] + _dot(osw_ref[...].astype(BF16), wo_ref[0:SWA_Q, :])
         + _dot(ogd_ref[...].astype(BF16), wo_ref[SWA_Q:D_MIX, :]))
    h_ref[...] = h
    xn = _rms(h, gf_ref[...])
    xn_ref[...] = xn
    logits = _dot_nt(wr_ref[...], xn.astype(BF16))

    row = lax.broadcasted_iota(I32, (rows, tm), 0)
    bias = br_ref[...]
    top = lambda v: jnp.max(v, axis=0, keepdims=True)
    tot = lambda v: jnp.sum(v, axis=0, keepdims=True)
    first_at = lambda v: jnp.min(jnp.where(v == top(v), row, 2 * rows), axis=0, keepdims=True)
    is_g = (row >= N_EXPERTS) & (row < N_EXPERTS + N_GROUPS)
    lg = jnp.where(is_g, logits, NEG_BIG)
    pg = jnp.where(is_g, jnp.exp(lg - top(lg)), 0.0)
    group_p = pg / tot(pg)
    g_row = first_at(jnp.where(is_g, group_p + bias, NEG_BIG))
    g_w = tot(jnp.where(row == g_row, group_p, 0.0))
    sel = (row < N_EXPERTS) & ((row // EXPERTS_PER_GROUP) == (g_row - N_EXPERTS))
    le = jnp.where(sel, logits, NEG_BIG)
    pe = jnp.where(sel, jnp.exp(le - top(le)), 0.0)
    e_p = pe / tot(pe)
    score = jnp.where(sel, e_p + bias, NEG_BIG)
    i1 = first_at(score)
    i2 = first_at(jnp.where(row == i1, NEG_BIG, score))
    oh1 = row == i1
    oh2 = row == i2
    w1 = tot(jnp.where(oh1, e_p, 0.0))
    w2 = tot(jnp.where(oh2, e_p, 0.0))
    wsum = w1 + w2

    ohs = (oh1 | oh2).astype(BF16)
    earlier = (lax.broadcasted_iota(I32, (tm, tm), 0) < lax.broadcasted_iota(I32, (tm, tm), 1)).astype(BF16)
    before = _dot(ohs, earlier) + run_ref[...]
    r1 = tot(jnp.where(oh1, before, 0.0))
    r2 = tot(jnp.where(oh2, before, 0.0))
    run_ref[...] = run_ref[...] + jnp.sum(ohs.astype(F32), axis=1, keepdims=True)
    cnt_ref[...] = run_ref[...]

    zero = jnp.zeros_like(w1)
    meta = jnp.concatenate([i1.astype(F32), i2.astype(F32), r1, r2, g_w * (w1 / wsum), g_w * (w2 / wsum),
                            zero, zero], axis=0)
    meta_ref[...] = meta
    eye8 = (lax.broadcasted_iota(I32, (8, LANES), 0) == lax.broadcasted_iota(I32, (8, LANES), 1)).astype(F32)
    gates_ref[...] = sum(_dot_tn(part, eye8) for part in _split3(meta))[:, 4:6]


def _outproj(x2d, o_swa, o_gdn, wo_bf16, gain, w_router, b_router):
    n = x2d.shape[0]
    tm = ROW_TILE
    row = lambda w: pl.BlockSpec((tm, w), lambda i: (i, 0))
    full = lambda a: pl.BlockSpec(a.shape, lambda i: (0,) * a.ndim)
    return pl.pallas_call(
        _outproj_kernel,
        grid=(n // tm,),
        in_specs=[row(D_MODEL), row(SWA_Q), row(GDN_V), full(wo_bf16), full(gain), full(w_router), full(b_router)],
        out_specs=[row(D_MODEL), row(D_MODEL), row(2), pl.BlockSpec((8, tm), lambda i: (0, i)),
                   pl.BlockSpec((LANES, 1), lambda i: (0, 0))],
        out_shape=[jax.ShapeDtypeStruct((n, D_MODEL), F32), jax.ShapeDtypeStruct((n, D_MODEL), F32),
                   jax.ShapeDtypeStruct((n, 2), F32), jax.ShapeDtypeStruct((8, n), F32),
                   jax.ShapeDtypeStruct((LANES, 1), F32)],
        scratch_shapes=[pltpu.VMEM((LANES, 1), F32)],
        compiler_params=pltpu.CompilerParams(dimension_semantics=("arbitrary",)),
        name="outproj_router",
    )(x2d, o_swa, o_gdn, wo_bf16, gain, w_router, b_router)


def _row_copy(src_ref, src_row, dst_ref, dst_row, sem):
    return pltpu.make_async_copy(src_ref.at[pl.ds(src_row, 1)], dst_ref.at[pl.ds(dst_row, 1)], sem)


def _scatter_kernel(dest_ref, xp_ref, xs_ref, out_ref, aidx_ref, sem, *, tiles_p, n_tokens):
    i = pl.program_id(0)
    tm = xp_ref.shape[0]

    def run(src_ref):
        def issue(g, carry):
            for u in range(DMA_UNROLL):
                r = g * DMA_UNROLL + u
                d0, d1 = dest_ref[0, r], dest_ref[1, r]
                _row_copy(src_ref, r, out_ref, d0, sem).start(priority=0)
                _row_copy(src_ref, r, out_ref, d1, sem).start(priority=1)
                aidx_ref[d0] = i * tm + r
                aidx_ref[d1] = n_tokens + i * tm + r
            return carry

        lax.fori_loop(0, tm // DMA_UNROLL, issue, 0)
        for _ in range(2):
            pltpu.make_async_copy(src_ref, out_ref.at[pl.ds(0, tm)], sem).wait()

    @pl.when(i < tiles_p)
    def _():
        run(xp_ref)

    @pl.when(i >= tiles_p)
    def _():
        run(xs_ref)


def _scatter(dest3, xn_p, xn_s):
    tm = ROW_TILE
    tiles_p, tiles_s = xn_p.shape[0] // tm, xn_s.shape[0] // tm
    rows = 2 * (xn_p.shape[0] + xn_s.shape[0])
    return pl.pallas_call(
        functools.partial(_scatter_kernel, tiles_p=tiles_p, n_tokens=rows // 2),
        grid=(tiles_p + tiles_s,),
        in_specs=[pl.BlockSpec((2, tm), lambda i: (0, i), memory_space=pltpu.SMEM),
                  pl.BlockSpec((tm, D_MODEL), lambda i: (jnp.minimum(i, tiles_p - 1), 0)),
                  pl.BlockSpec((tm, D_MODEL), lambda i: (jnp.maximum(i - tiles_p, 0), 0))],
        out_specs=[pl.BlockSpec(memory_space=pl.ANY), pl.BlockSpec(memory_space=pltpu.SMEM)],
        out_shape=[jax.ShapeDtypeStruct((rows, D_MODEL), F32), jax.ShapeDtypeStruct((rows,), I32)],
        scratch_shapes=[pltpu.SemaphoreType.DMA(())],
        compiler_params=pltpu.CompilerParams(dimension_semantics=("arbitrary",)),
        name="scatter_rows",
    )(dest3, xn_p, xn_s)


def _experts_kernel(blk_ref, exp_ref, lo_ref, hi_ref, first_ref, fresh_ref, last_ref,
                    aidx_ref, x_ref, wg_ref, wu_ref, wd_ref, ya_ref, wg_l, wu_l, wd_l, ybuf, sem, *, n_items):
    j = pl.program_id(0)
    lo, hi = lo_ref[j], hi_ref[j]
    slot = lax.rem(blk_ref[j], 2)

    def wait_rows(s):
        pltpu.make_async_copy(ybuf.at[s], ya_ref.at[pl.ds(0, MOE_BLOCK)], sem.at[s]).wait()

    @pl.when(fresh_ref[j] == 1)
    def _():
        wg_l[...] = wg_ref[0].astype(BF16)
        wu_l[...] = wu_ref[0].astype(BF16)
        wd_l[...] = wd_ref[0].astype(BF16)

    @pl.when((first_ref[j] == 1) & (blk_ref[j] >= 2))
    def _():
        wait_rows(slot)

    def item(is_first, is_last):
        x = x_ref[...].astype(BF16)
        gate = _dot(x, wg_l[...])
        up = _dot(x, wu_l[...])
        hid = (gate * jax.nn.sigmoid(gate)) * up
        y = _dot(hid.astype(BF16), wd_l[...])
        r = lax.broadcasted_iota(I32, (MOE_BLOCK, 1), 0)
        mine = (r >= lo) & (r < hi)
        ybuf[slot] = jnp.where(mine, y, 0.0 if is_first else ybuf[slot])
        if is_last:
            for row in range(MOE_BLOCK):
                _row_copy(ybuf.at[slot], row, ya_ref, aidx_ref[0, 0, row], sem.at[slot]).start(priority=row % 2)

    for is_first in (0, 1):
        for is_last in (0, 1):
            pl.when((hi > lo) & (first_ref[j] == is_first) & (last_ref[j] == is_last))(
                functools.partial(item, is_first, is_last))

    @pl.when(j == n_items - 1)
    def _():
        wait_rows(0)
        wait_rows(1)


def _experts(items, aidx, xs, w_gate, w_up, w_down):
    n_items = items[0].shape[0]
    n_blocks = xs.shape[0] // MOE_BLOCK
    assert n_items > n_blocks + N_EXPERTS - 1 and n_blocks >= 2
    xblk = pl.BlockSpec((MOE_BLOCK, D_MODEL), lambda j, blk, *_: (blk[j], 0))
    wspec = lambda a: pl.BlockSpec((1,) + a.shape[1:], lambda j, blk, ex, *_: (ex[j], 0, 0))
    return pl.pallas_call(
        functools.partial(_experts_kernel, n_items=n_items),
        grid_spec=pltpu.PrefetchScalarGridSpec(
            num_scalar_prefetch=len(items),
            grid=(n_items,),
            in_specs=[pl.BlockSpec((1, 1, MOE_BLOCK), lambda j, blk, *_: (blk[j], 0, 0), memory_space=pltpu.SMEM),
                      xblk, wspec(w_gate), wspec(w_up), wspec(w_down)],
            out_specs=pl.BlockSpec(memory_space=pl.ANY),
            scratch_shapes=[pltpu.VMEM(w_gate.shape[1:], BF16), pltpu.VMEM(w_up.shape[1:], BF16),
                            pltpu.VMEM(w_down.shape[1:], BF16), pltpu.VMEM((2, MOE_BLOCK, D_MODEL), F32),
                            pltpu.SemaphoreType.DMA((2,))]),
        out_shape=jax.ShapeDtypeStruct(xs.shape, F32),
        compiler_params=pltpu.CompilerParams(dimension_semantics=("arbitrary",)),
        name="experts",
    )(*items, aidx.reshape(n_blocks, 1, MOE_BLOCK), xs, w_gate, w_up, w_down)


def _work_items(counts, total_rows):
    n_blocks = total_rows // MOE_BLOCK
    n_items = n_blocks + N_EXPERTS
    end = jnp.cumsum(counts)
    start = end - counts
    first_blk = start // MOE_BLOCK
    nb = jnp.where(counts > 0, (end - 1) // MOE_BLOCK - first_blk + 1, 0)
    item_end = jnp.cumsum(nb)
    used = item_end[-1]
    j = jnp.arange(n_items, dtype=I32)
    jj = jnp.minimum(j, used - 1)
    e = jnp.minimum(jnp.sum((item_end[None, :] <= jj[:, None]).astype(I32), axis=1), N_EXPERTS - 1)
    onehot = (e[:, None] == jnp.arange(N_EXPERTS, dtype=I32)[None, :]).astype(I32)
    pick = lambda a: jnp.sum(onehot * a[None, :], axis=1)
    blk = pick(first_blk) + (jj - (pick(item_end) - pick(nb)))
    lo = jnp.maximum(pick(start), blk * MOE_BLOCK) - blk * MOE_BLOCK
    hi = jnp.minimum(pick(end), (blk + 1) * MOE_BLOCK) - blk * MOE_BLOCK
    live = j < used
    lo = jnp.where(live, lo, 0)
    hi = jnp.where(live, hi, 0)
    prev_blk = jnp.concatenate([jnp.full((1,), -1, I32), blk[:-1]])
    first = (live & (blk != prev_blk)).astype(I32)
    prev_e = jnp.concatenate([jnp.full((1,), -1, I32), e[:-1]])
    fresh = (live & (e != prev_e)).astype(I32)
    next_blk = jnp.concatenate([blk[1:], jnp.full((1,), -1, I32)])
    last = (live & ((blk != next_blk) | (j == used - 1))).astype(I32)
    return (blk.astype(I32), e.astype(I32), lo.astype(I32), hi.astype(I32), first, fresh, last), start


def _combine_kernel(gates_ref, h_ref, gf_ref, y0_ref, y1_ref, out_ref):
    gates = gates_ref[...]
    y = h_ref[...] + (y0_ref[...] * gates[:, 0:1] + y1_ref[...] * gates[:, 1:2])
    out_ref[...] = _rms(y, gf_ref[...])


def _combine(gates, h, gain, ya, first_token):
    n = h.shape[0]
    tm = COMBINE_TILE
    tile0 = first_token // tm
    slot_tiles = ya.shape[0] // 2 // tm
    row = lambda w: pl.BlockSpec((tm, w), lambda i: (i, 0))
    slot = lambda k: pl.BlockSpec((tm, D_MODEL), lambda i: (k * slot_tiles + tile0 + i, 0))
    return pl.pallas_call(
        _combine_kernel,
        grid=(n // tm,),
        in_specs=[row(2), row(D_MODEL), pl.BlockSpec(gain.shape, lambda i: (0, 0)), slot(0), slot(1)],
        out_specs=row(D_MODEL),
        out_shape=jax.ShapeDtypeStruct((n, D_MODEL), F32),
        compiler_params=pltpu.CompilerParams(dimension_semantics=("arbitrary",)),
        name="combine_norm",
    )(gates, h, gain, ya, ya)


def kernel(x_prompt, x_sample, cache_swa_k, cache_swa_v, state_gdn_conv, state_gdn, norm_mix, w_in, swa_sinks,
           gdn_conv_w, gdn_A_log, gdn_dt_bias, gdn_norm_w, w_out, norm_ffn, w_router_group, b_router_group,
           w_router_expert, b_router_expert, w_exp_gate, w_exp_up, w_exp_down, norm_final):
    depth = w_in.shape[0]
    assert depth == 1, "single trunk layer"
    bp, sp, _ = x_prompt.shape
    bs, ts, _ = x_sample.shape
    np_, ns = bp * sp, bs * ts
    l = 0

    w_in_b = w_in[l].astype(BF16)
    w_out_b = w_out[l].astype(BF16)
    g_mix = norm_mix[l].reshape(1, D_MODEL)
    g_ffn = norm_ffn[l].reshape(1, D_MODEL)
    g_fin = norm_final.reshape(1, D_MODEL)
    pad = LANES - N_EXPERTS - N_GROUPS
    w_router = jnp.concatenate([w_router_expert[l], w_router_group[l], jnp.zeros((D_MODEL, pad), F32)],
                               axis=1).astype(BF16).T
    b_router = jnp.concatenate([b_router_expert[l].reshape(-1), b_router_group[l],
                                jnp.zeros((pad,), F32)])[:, None]
    zeros8 = jnp.zeros((GDN_HEADS,), F32)
    alog16 = jnp.concatenate([gdn_A_log[l], zeros8])[None]
    dtb16 = jnp.concatenate([gdn_dt_bias[l], zeros8])[None]
    nw_group = jnp.tile(gdn_norm_w[l].reshape(1, HEAD_DIM), (1, GDN_GROUP))
    sinks = swa_sinks[l]

    q_p, k_p, v_p, c_p, z_p, ab_p = _inproj(x_prompt.reshape(np_, D_MODEL), g_mix, w_in_b)
    nblk = np_ // WINDOW
    k_p3 = k_p.reshape(nblk, WINDOW, SWA_KV)
    v_p3 = v_p.reshape(nblk, WINDOW, SWA_KV)
    (o_swa_p,) = _swa(sinks, q_p.reshape(nblk, WINDOW, SWA_Q), k_p3, v_p3, k_p3, v_p3,
                      bb=1, blocks_per_seq=sp // WINDOW, emit_cache=False)
    c_p3 = c_p.reshape(bp, sp, GDN_CONV_CH)
    o_gdn_p, s_fin_p = _gdn(c_p3, jnp.zeros((bp, CONV_W - 1, GDN_CONV_CH), F32), z_p.reshape(bp, sp, GDN_V),
                            ab_p.reshape(bp, sp, 2 * GDN_HEADS),
                            jnp.zeros((bp, GDN_HEADS, HEAD_DIM, HEAD_DIM), F32),
                            gdn_conv_w[l], alog16, dtb16, nw_group, chunk=GDN_CHUNK, seq_block=bp)
    h_p, xn_p, gates_p, meta_p, cnt_p = _outproj(
        x_prompt.reshape(np_, D_MODEL), o_swa_p.reshape(np_, SWA_Q), o_gdn_p.reshape(np_, GDN_V),
        w_out_b, g_ffn, w_router, b_router)

    q_s, k_s, v_s, c_s, z_s, ab_s = _inproj(x_sample.reshape(ns, D_MODEL), g_mix, w_in_b)
    o_swa_s, kcache_s, vcache_s = _swa(
        sinks, q_s.reshape(bs, ts, SWA_Q), k_s.reshape(bs, ts, SWA_KV), v_s.reshape(bs, ts, SWA_KV),
        cache_swa_k[l].reshape(bs, WINDOW, SWA_KV), cache_swa_v[l].reshape(bs, WINDOW, SWA_KV),
        bb=8, blocks_per_seq=None, emit_cache=True)
    c_s3 = c_s.reshape(bs, ts, GDN_CONV_CH)
    o_gdn_s, s_fin_s = _gdn(c_s3, state_gdn_conv[l], z_s.reshape(bs, ts, GDN_V),
                            ab_s.reshape(bs, ts, 2 * GDN_HEADS), state_gdn[l],
                            gdn_conv_w[l], alog16, dtb16, nw_group, chunk=ts, seq_block=8)
    h_s, xn_s, gates_s, meta_s, cnt_s = _outproj(
        x_sample.reshape(ns, D_MODEL), o_swa_s.reshape(ns, SWA_Q), o_gdn_s.reshape(ns, GDN_V),
        w_out_b, g_ffn, w_router, b_router)

    cnt_p_i = cnt_p[:N_EXPERTS, 0].astype(I32)
    cnt_s_i = cnt_s[:N_EXPERTS, 0].astype(I32)
    items, start = _work_items(cnt_p_i + cnt_s_i, 2 * (np_ + ns))
    expert_ids = jnp.arange(N_EXPERTS, dtype=I32)
    lookup = lambda table, ids: jnp.sum(jnp.where(ids[..., None] == expert_ids, table, 0), axis=-1)
    dest_p = lookup(start, meta_p[0:2].astype(I32)) + meta_p[2:4].astype(I32)
    dest_s = lookup(start + cnt_p_i, meta_s[0:2].astype(I32)) + meta_s[2:4].astype(I32)
    dest_all = jnp.concatenate([dest_p, dest_s], axis=1)
    xs, aidx = _scatter(dest_all, xn_p, xn_s)
    ya = _experts(items, aidx, xs, w_exp_gate[l], w_exp_up[l], w_exp_down[l])
    y_p = _combine(gates_p, h_p, g_fin, ya, 0)
    y_s = _combine(gates_s, h_s, g_fin, ya, np_)

    kv5 = lambda a, b: a.reshape(b, -1, SWA_KV_HEADS, HEAD_DIM)[None]
    return (y_p.reshape(bp, sp, D_MODEL), y_s.reshape(bs, ts, D_MODEL),
            kv5(k_p.reshape(bp, sp, SWA_KV)[:, -WINDOW:], bp), kv5(v_p.reshape(bp, sp, SWA_KV)[:, -WINDOW:], bp),
            kv5(kcache_s, bs), kv5(vcache_s, bs),
            c_p3[:, -(CONV_W - 1):][None], c_s3[:, -(CONV_W - 1):][None],
            s_fin_p[None], s_fin_s[None])
```

```python
import functools

import jax
import jax.numpy as jnp
from jax import lax
from jax.experimental import pallas as pl
from jax.experimental.pallas import tpu as pltpu

F32 = jnp.float32
BF16 = jnp.bfloat16
I32 = jnp.int32

D_MODEL = 1024
HEAD_DIM = 64
SWA_HEADS = 8
GDN_HEADS = 8
SWA_KV_HEADS = 2
GQA_GROUP = SWA_HEADS // SWA_KV_HEADS
WINDOW = 128
ATTN_SCALE = HEAD_DIM ** -0.5
CONV_W = 4
N_GROUPS = 8
EXPERTS_PER_GROUP = 8
N_EXPERTS = 64
D_EXPERT = 256
EPS = 1e-6

SWA_Q = SWA_HEADS * HEAD_DIM
SWA_KV = SWA_KV_HEADS * HEAD_DIM
GDN_QK = GDN_HEADS * HEAD_DIM
GDN_V = GDN_HEADS * HEAD_DIM
GDN_CONV_CH = 2 * GDN_QK + GDN_V
D_MIX = SWA_Q + GDN_V
D_IN = SWA_Q + 2 * SWA_KV + GDN_CONV_CH + GDN_V + 2 * GDN_HEADS
COL_K = SWA_Q
COL_V = COL_K + SWA_KV
COL_C = COL_V + SWA_KV
COL_Z = COL_C + GDN_CONV_CH
COL_AB = COL_Z + GDN_V

LANES = 128
NEG_BIG = -1e30
ROW_TILE = 512
MOE_BLOCK = 256
COMBINE_TILE = 256
DMA_UNROLL = 8
GDN_CHUNK = 64
GDN_TILE = 256
GDN_GROUP = 4
GDN_GROUP_W = GDN_GROUP * HEAD_DIM


def _rms(x, g):
    return x * lax.rsqrt(jnp.mean(x * x, axis=-1, keepdims=True) + EPS) * g


def _dot(a, b):
    return jnp.dot(a, b, preferred_element_type=F32)


def _dot_nt(a, b):
    return lax.dot_general(a, b, (((1,), (1,)), ((), ())), preferred_element_type=F32)


def _dot_tn(a, b):
    return lax.dot_general(a, b, (((0,), (0,)), ((), ())), preferred_element_type=F32)


def _split3(x):
    p1 = x.astype(BF16).astype(F32)
    r = x - p1
    p2 = r.astype(BF16).astype(F32)
    p3 = (r - p2).astype(BF16).astype(F32)
    return p1, p2, p3


def _inproj_kernel(x_ref, g_ref, w_ref, q_ref, k_ref, v_ref, c_ref, z_ref, ab_ref):
    x = x_ref[...]
    xb = _rms(x, g_ref[...]).astype(BF16)
    q_ref[...] = _dot(xb, w_ref[:, 0:COL_K])
    k_ref[...] = _dot(xb, w_ref[:, COL_K:COL_V])
    v_ref[...] = _dot(xb, w_ref[:, COL_V:COL_C])
    c_ref[...] = _dot(xb, w_ref[:, COL_C:COL_Z])
    z_ref[...] = _dot(xb, w_ref[:, COL_Z:COL_AB])
    ab_ref[...] = _dot(xb, w_ref[:, COL_AB:D_IN])


def _inproj(x2d, gain, w_bf16):
    n = x2d.shape[0]
    tm = ROW_TILE
    row = lambda w: pl.BlockSpec((tm, w), lambda i: (i, 0))
    full = lambda a: pl.BlockSpec(a.shape, lambda i: (0,) * a.ndim)
    widths = (SWA_Q, SWA_KV, SWA_KV, GDN_CONV_CH, GDN_V, 2 * GDN_HEADS)
    return pl.pallas_call(
        _inproj_kernel,
        grid=(n // tm,),
        in_specs=[row(D_MODEL), full(gain), full(w_bf16)],
        out_specs=[row(w) for w in widths],
        out_shape=[jax.ShapeDtypeStruct((n, w), F32) for w in widths],
        compiler_params=pltpu.CompilerParams(dimension_semantics=("arbitrary",)),
        name="inproj",
    )(x2d, gain, w_bf16)


def _swa_kernel(sink_ref, q_ref, kc_ref, vc_ref, kp_ref, vp_ref, o_ref, *cache_refs,
                bb, t, blocks_per_seq, emit_cache):
    if blocks_per_seq is None:
        has_prev = None
    else:
        has_prev = lax.rem(pl.program_id(0), blocks_per_seq) != 0
    rows = GQA_GROUP * t
    ri = lax.broadcasted_iota(I32, (rows, 1), 0)
    qi = lax.rem(ri, t)
    gi = ri // t
    if emit_cache:
        kj = lax.broadcasted_iota(I32, (rows, WINDOW), 1)
        mask = kj <= qi + (WINDOW - t)
    else:
        kj = lax.broadcasted_iota(I32, (rows, WINDOW + t), 1)
        mask = (kj <= qi + WINDOW) & (kj > qi)
        if has_prev is not None:
            mask = mask & (has_prev | (kj >= WINDOW))
    sinks = []
    for h in range(SWA_KV_HEADS):
        sink = jnp.zeros((rows, 1), F32)
        for g in range(GQA_GROUP):
            sink = jnp.where(gi == g, sink_ref[GQA_GROUP * h + g], sink)
        sinks.append(sink)
    chains = [(b, h) for b in range(bb) for h in range(SWA_KV_HEADS)]
    scores, values = [], []
    for b, h in chains:
        hs = slice(h * HEAD_DIM, (h + 1) * HEAD_DIM)
        kp, vp = kp_ref[b, :, hs], vp_ref[b, :, hs]
        kc, vc = kc_ref[b, :, hs], vc_ref[b, :, hs]
        if emit_cache:
            keys = jnp.concatenate([kp[t:], kc], axis=0)
            vals = jnp.concatenate([vp[t:], vc], axis=0)
            cache_refs[0][b, :, hs] = keys
            cache_refs[1][b, :, hs] = vals
        else:
            keys = jnp.concatenate([kp, kc], axis=0)
            vals = jnp.concatenate([vp, vc], axis=0)
        q4 = jnp.concatenate(
            [q_ref[b, :, (GQA_GROUP * h + g) * HEAD_DIM:(GQA_GROUP * h + g + 1) * HEAD_DIM]
             for g in range(GQA_GROUP)], axis=0)
        scores.append(_dot_nt(q4.astype(BF16), keys.astype(BF16)))
        values.append(vals.astype(BF16))
    probs, dens = [], []
    for (b, h), s in zip(chains, scores):
        s = jnp.where(mask, s * ATTN_SCALE, NEG_BIG)
        m = jnp.maximum(jnp.max(s, axis=-1, keepdims=True), sinks[h])
        p = jnp.exp(s - m)
        dens.append(jnp.sum(p, axis=-1, keepdims=True) + jnp.exp(sinks[h] - m))
        probs.append(p.astype(BF16))
    outs = [_dot(p, v) / den for p, v, den in zip(probs, values, dens)]
    for b in range(bb):
        o_ref[b] = jnp.concatenate([outs[b * SWA_KV_HEADS + h][g * t:(g + 1) * t]
                                    for h in range(SWA_KV_HEADS) for g in range(GQA_GROUP)], axis=-1)


def _swa(sinks, q3, k3, v3, kprev3, vprev3, *, bb, blocks_per_seq, emit_cache):
    nb, t, _ = q3.shape
    cur = lambda w: pl.BlockSpec((bb, t, w), lambda i: (i, 0, 0))
    if blocks_per_seq is None:
        prev = pl.BlockSpec((bb, WINDOW, SWA_KV), lambda i: (i, 0, 0))
    else:
        prev = pl.BlockSpec((bb, WINDOW, SWA_KV), lambda i: (jnp.maximum(i - 1, 0), 0, 0))
    out_specs = [cur(SWA_Q)]
    out_shape = [jax.ShapeDtypeStruct((nb, t, SWA_Q), F32)]
    if emit_cache:
        cache = pl.BlockSpec((bb, WINDOW, SWA_KV), lambda i: (i, 0, 0))
        out_specs += [cache, cache]
        out_shape += [jax.ShapeDtypeStruct((nb, WINDOW, SWA_KV), F32)] * 2
    return pl.pallas_call(
        functools.partial(_swa_kernel, bb=bb, t=t, blocks_per_seq=blocks_per_seq, emit_cache=emit_cache),
        grid=(nb // bb,),
        in_specs=[pl.BlockSpec(memory_space=pltpu.SMEM), cur(SWA_Q), cur(SWA_KV), cur(SWA_KV), prev, prev],
        out_specs=out_specs,
        out_shape=out_shape,
        compiler_params=pltpu.CompilerParams(dimension_semantics=("arbitrary",)),
        name="swa_cache" if emit_cache else "swa_band",
    )(sinks, q3, k3, v3, kprev3, vprev3)


def _gdn_prep_kernel(c_ref, hist_ref, ab_ref, cw_ref, alog_ref, dtb_ref,
                     u_ref, w_ref, qd_ref, kd_ref, qk_ref, gt_ref, cbuf_ref, *, chunk):
    sb, r, _ = c_ref.shape
    tp = sb * r
    cn = chunk
    low = w_ref.dtype
    hist_rows = CONV_W - 1

    @pl.when(pl.program_id(1) == 0)
    def _():
        cbuf_ref[:, 8 - hist_rows:8, :] = hist_ref[...]

    cbuf_ref[:, 8:8 + r, :] = c_ref[...]
    conv = cbuf_ref[:, 8 - hist_rows:8 - hist_rows + r, :] * cw_ref[0:1, :]
    for i in range(1, CONV_W):
        conv = conv + cbuf_ref[:, 8 - hist_rows + i:8 - hist_rows + i + r, :] * cw_ref[i:i + 1, :]
    tail = cbuf_ref[:, 8 + r - hist_rows:8 + r, :]
    cbuf_ref[:, 8 - hist_rows:8, :] = tail
    conv = (conv * jax.nn.sigmoid(conv)).reshape(tp, GDN_CONV_CH)

    ab = ab_ref[...].reshape(tp, 2 * GDN_HEADS)
    is_g = lax.broadcasted_iota(I32, (1, 2 * GDN_HEADS), 1) < GDN_HEADS
    g = jnp.where(is_g, -jnp.exp(alog_ref[...]) * jax.nn.softplus(ab + dtb_ref[...]), 0.0)
    beta = jax.nn.sigmoid(ab)

    ri = lax.broadcasted_iota(I32, (tp, tp), 0)
    ci = lax.broadcasted_iota(I32, (tp, tp), 1)
    same = (ri // cn) == (ci // cn)
    causal = same & (ri >= ci)
    strict = same & (ri > ci)
    eye = (ri == ci).astype(F32)
    stack = jnp.concatenate([causal.astype(BF16), same.astype(BF16)], axis=0)
    both = sum(_dot(stack, p.astype(BF16)) for p in _split3(g))
    gc, gl = both[:tp], both[tp:]
    e16 = lax.broadcasted_iota(I32, (2 * GDN_HEADS, 2 * GDN_HEADS), 0)
    eye16 = (e16 == lax.broadcasted_iota(I32, (2 * GDN_HEADS, 2 * GDN_HEADS), 1)).astype(BF16)
    gc_row = sum(_dot_nt(eye16, p.astype(BF16)) for p in _split3(gc))
    gt_ref[...] = jnp.exp(gl)
    fold = (lax.broadcasted_iota(I32, (tp, cn), 0) % cn == lax.broadcasted_iota(I32, (tp, cn), 1)).astype(low)

    levels = cn.bit_length() - 2
    us, ws, qds, kds, qks = [], [], [], [], []
    for h in range(GDN_HEADS):
        q = conv[:, h * HEAD_DIM:(h + 1) * HEAD_DIM]
        k = conv[:, GDN_QK + h * HEAD_DIM:GDN_QK + (h + 1) * HEAD_DIM]
        v = conv[:, 2 * GDN_QK + h * HEAD_DIM:2 * GDN_QK + (h + 1) * HEAD_DIM]
        q = q * lax.rsqrt(jnp.sum(q * q, axis=-1, keepdims=True) + EPS) * (HEAD_DIM ** -0.5)
        k = k * lax.rsqrt(jnp.sum(k * k, axis=-1, keepdims=True) + EPS)
        b_h = beta[:, GDN_HEADS + h:GDN_HEADS + h + 1]
        g_c = gc[:, h:h + 1]
        decay = jnp.exp(jnp.where(causal, g_c - gc_row[h:h + 1, :], NEG_BIG))
        eg = jnp.exp(g_c)
        kb = k * b_h
        k_l = k.astype(low)
        lmat = jnp.where(strict, _dot_nt(kb.astype(low), k_l) * decay, 0.0)
        qk = _dot_nt(q.astype(low), k_l) * decay
        qks.append(_dot(qk.astype(low), fold))
        l_l = lmat.astype(low)
        x = eye - lmat
        p = _dot(l_l, l_l)
        for lev in range(1, levels + 1):
            p_l = p.astype(low)
            if lev < levels:
                xp = _dot(jnp.concatenate([x.astype(low), p_l], axis=0), p_l)
                x = x + xp[:tp]
                p = xp[tp:]
            else:
                x = x + _dot(x.astype(low), p_l)
        rhs = jnp.concatenate([v * b_h, kb * eg], axis=-1)
        sol = _dot(x.astype(low), rhs.astype(low))
        us.append(sol[:, :HEAD_DIM])
        ws.append(sol[:, HEAD_DIM:])
        qds.append(q * eg)
        kds.append(k * jnp.exp(gl[:, h:h + 1] - g_c))
    u_ref[...] = jnp.concatenate(us, axis=-1)
    w_ref[...] = jnp.concatenate(ws, axis=-1).astype(low)
    qd_ref[...] = jnp.concatenate(qds, axis=-1).astype(low)
    kd_ref[...] = jnp.concatenate(kds, axis=-1).astype(low)
    qk_ref[...] = jnp.concatenate(qks, axis=-1).astype(low)


def _gdn_prep_pair_kernel(c_ref, hist_ref, ab_ref, cw_ref, alog_ref, dtb_ref,
                          u_ref, w_ref, qd_ref, kd_ref, qk_ref, gt_ref, cbuf_ref):
    _, r, _ = c_ref.shape
    tp = r
    cn = HEAD_DIM
    pair_w = 2 * HEAD_DIM
    hist_rows = CONV_W - 1

    @pl.when(pl.program_id(1) == 0)
    def _():
        cbuf_ref[:, 8 - hist_rows:8, :] = hist_ref[...]

    cbuf_ref[:, 8:8 + r, :] = c_ref[...]
    conv = cbuf_ref[:, 8 - hist_rows:8 - hist_rows + r, :] * cw_ref[0:1, :]
    for i in range(1, CONV_W):
        conv = conv + cbuf_ref[:, 8 - hist_rows + i:8 - hist_rows + i + r, :] * cw_ref[i:i + 1, :]
    tail = cbuf_ref[:, 8 + r - hist_rows:8 + r, :]
    cbuf_ref[:, 8 - hist_rows:8, :] = tail
    conv = (conv * jax.nn.sigmoid(conv)).reshape(tp, GDN_CONV_CH)

    ab = ab_ref[...].reshape(tp, 2 * GDN_HEADS)
    is_g = lax.broadcasted_iota(I32, (1, 2 * GDN_HEADS), 1) < GDN_HEADS
    g = jnp.where(is_g, -jnp.exp(alog_ref[...]) * jax.nn.softplus(ab + dtb_ref[...]), 0.0)
    beta = jax.nn.sigmoid(ab)

    ri = lax.broadcasted_iota(I32, (tp, tp), 0)
    ci = lax.broadcasted_iota(I32, (tp, tp), 1)
    same = (ri // cn) == (ci // cn)
    same_l = same.astype(BF16)
    stack = jnp.concatenate([(same & (ri >= ci)).astype(BF16), same_l], axis=0)
    both = sum(_dot(stack, p.astype(BF16)) for p in _split3(g))
    gc, gl = both[:tp], both[tp:]
    gt_ref[...] = jnp.exp(gl)

    lane = lax.broadcasted_iota(I32, (tp, pair_w), 1)
    c_in = lax.broadcasted_iota(I32, (tp, pair_w), 0) % cn
    j_in = lane % cn
    left = lane < cn
    causal = c_in >= j_in
    strict = c_in > j_in
    diag = c_in == j_in
    eye = diag.astype(F32)
    bdmask = ((lax.broadcasted_iota(I32, (pair_w, pair_w), 0) // cn)
              == (lax.broadcasted_iota(I32, (pair_w, pair_w), 1) // cn))
    ones_bd = bdmask.astype(BF16)

    def bd(m):
        return jnp.where(bdmask, jnp.concatenate([m, m], axis=0), 0.0).astype(BF16)

    def head_sum(x):
        hi = x.astype(BF16)
        lo = (x - hi.astype(F32)).astype(BF16)
        return _dot(hi, ones_bd) + _dot(lo, ones_bd)

    levels = cn.bit_length() - 2
    chunks = [slice(n * cn, (n + 1) * cn) for n in range(tp // cn)]
    n_pairs = GDN_HEADS // 2
    pairs = range(n_pairs)
    pick = lambda m, off, p: jnp.where(left, m[:, off + 2 * p:off + 2 * p + 1], m[:, off + 2 * p + 1:off + 2 * p + 2])
    third = lambda i, p: conv[:, i * GDN_QK + p * pair_w:i * GDN_QK + (p + 1) * pair_w]
    qs = [third(0, p) for p in pairs]
    ks = [third(1, p) for p in pairs]
    qs = [q * lax.rsqrt(head_sum(q * q) + EPS) * (HEAD_DIM ** -0.5) for q in qs]
    ks = [k * lax.rsqrt(head_sum(k * k) + EPS) for k in ks]
    gcps = [pick(gc, 0, p) for p in pairs]
    rowms = [sum(_dot(same_l, part.astype(BF16)) for part in _split3(jnp.where(diag, gcp, 0.0))) for gcp in gcps]
    decays = [jnp.exp(jnp.where(causal, gcp - rowm, NEG_BIG)) for gcp, rowm in zip(gcps, rowms)]
    lmats, vbs, kbegs = [], [], []
    for p in pairs:
        ls = slice(p * pair_w, (p + 1) * pair_w)
        q, k, gcp, decay = qs[p], ks[p], gcps[p], decays[p]
        bp = pick(beta, GDN_HEADS, p)
        eg = jnp.exp(gcp)
        kb = k * bp
        vbs.append(third(2, p) * bp)
        kbegs.append(kb * eg)
        qd_ref[:, ls] = (q * eg).astype(BF16)
        kd_ref[:, ls] = (k * jnp.exp(pick(gl, 0, p) - gcp)).astype(BF16)
        q_l, kb_l = q.astype(BF16), kb.astype(BF16)
        kbd = [bd(k[rs]) for rs in chunks]
        kk = jnp.concatenate([_dot_nt(kb_l[rs], kbd[n]) for n, rs in enumerate(chunks)], axis=0)
        qk = jnp.concatenate([_dot_nt(q_l[rs], kbd[n]) for n, rs in enumerate(chunks)], axis=0)
        qk_ref[:, ls] = (qk * decay).astype(BF16)
        lmats.append(jnp.where(strict, kk * decay, 0.0))
    bodies = [(p, rs) for p in range(n_pairs) for rs in chunks]
    xs = [eye[rs] - lmats[p][rs] for p, rs in bodies]
    ps = [_dot(lmats[p][rs].astype(BF16), bd(lmats[p][rs])) for p, rs in bodies]
    for lev in range(1, levels + 1):
        pbd = [bd(pm) for pm in ps]
        if lev < levels:
            xp = [_dot(jnp.concatenate([xm, pm], axis=0).astype(BF16), wm) for xm, pm, wm in zip(xs, ps, pbd)]
            xs = [xm + m[:cn] for xm, m in zip(xs, xp)]
            ps = [m[cn:] for m in xp]
        else:
            xs = [xm + _dot(xm.astype(BF16), wm) for xm, wm in zip(xs, pbd)]
    x_l = [xm.astype(BF16) for xm in xs]
    us = [_dot(xm, bd(vbs[p][rs])) for xm, (p, rs) in zip(x_l, bodies)]
    ws = [_dot(xm, bd(kbegs[p][rs])) for xm, (p, rs) in zip(x_l, bodies)]
    nc = len(chunks)
    for p in range(n_pairs):
        ls = slice(p * pair_w, (p + 1) * pair_w)
        u_ref[:, ls] = jnp.concatenate(us[p * nc:(p + 1) * nc], axis=0)
        w_ref[:, ls] = jnp.concatenate(ws[p * nc:(p + 1) * nc], axis=0).astype(BF16)


def _gdn_scan_kernel(u_ref, w_ref, qd_ref, kd_ref, qk_ref, gt_ref, z_ref, s0_ref, nw_ref,
                     o_ref, sfin_ref, sbd_ref, *, chunk, n_chunks):
    bb = u_ref.shape[0]
    cn = chunk
    low = w_ref.dtype
    gw = GDN_GROUP_W
    ni = pl.program_id(1)

    @pl.when(ni == 0)
    def _():
        sbd_ref[...] = jnp.zeros_like(sbd_ref)
        for b in range(bb):
            for h in range(GDN_HEADS):
                gi, hh = divmod(h, GDN_GROUP)
                ds = slice(hh * HEAD_DIM, (hh + 1) * HEAD_DIM)
                sbd_ref[b, gi, ds, ds] = s0_ref[b, h]

    bdmask = ((lax.broadcasted_iota(I32, (gw, gw), 0) // HEAD_DIM)
              == (lax.broadcasted_iota(I32, (gw, gw), 1) // HEAD_DIM))
    ones_bd = bdmask.astype(BF16)
    vmask = ((lax.broadcasted_iota(I32, (GDN_GROUP * cn, gw), 0) // cn)
             == (lax.broadcasted_iota(I32, (GDN_GROUP * cn, gw), 1) // HEAD_DIM))
    e_row = lax.broadcasted_iota(I32, (2 * GDN_HEADS, gw), 0)
    e_col = lax.broadcasted_iota(I32, (2 * GDN_HEADS, gw), 1) // HEAD_DIM
    chains = [(b, gi) for b in range(bb) for gi in range(GDN_HEADS // GDN_GROUP)]
    lanes = lambda gi: slice(gi * gw, (gi + 1) * gw)
    states = [sbd_ref[b, gi] for b, gi in chains]
    states_l = [s.astype(low) for s in states]
    v_new = [u_ref[b, :, lanes(gi)] - _dot(w_ref[b, :, lanes(gi)], s_l) for (b, gi), s_l in zip(chains, states_l)]
    q_s = [_dot(qd_ref[b, :, lanes(gi)], s_l) for (b, gi), s_l in zip(chains, states_l)]
    v_l = [v.astype(low) for v in v_new]
    outs = []
    for (b, gi), s, v, qs in zip(chains, states, v_l, q_s):
        vbd = jnp.where(vmask, jnp.concatenate([v] * GDN_GROUP, axis=0), jnp.zeros((), low))
        outs.append(qs + _dot(qk_ref[b, :, gi * GDN_GROUP * cn:(gi + 1) * GDN_GROUP * cn], vbd))
        upd = _dot_tn(kd_ref[b, :, lanes(gi)], v)
        expand = (e_row == e_col + gi * GDN_GROUP).astype(BF16)
        gte = sum(_dot(p.astype(BF16), expand) for p in _split3(gt_ref[b, 0:8, :]))[0:1]
        sbd_ref[b, gi] = s * gte + jnp.where(bdmask, upd, 0.0)
    for (b, gi), o in zip(chains, outs):
        o2 = o * o
        hi = o2.astype(BF16)
        lo = (o2 - hi.astype(F32)).astype(BF16)
        ms = (_dot(hi, ones_bd) + _dot(lo, ones_bd)) * (1.0 / HEAD_DIM)
        zg = z_ref[b, :, lanes(gi)]
        o_ref[b, :, lanes(gi)] = o * lax.rsqrt(ms + EPS) * nw_ref[...] * (zg * jax.nn.sigmoid(zg))

    @pl.when(ni == n_chunks - 1)
    def _():
        for b in range(bb):
            for h in range(GDN_HEADS):
                gi, hh = divmod(h, GDN_GROUP)
                ds = slice(hh * HEAD_DIM, (hh + 1) * HEAD_DIM)
                sfin_ref[b, h] = sbd_ref[b, gi, ds, ds]


def _gdn(c3, hist, z3, ab3, s0, conv_w, alog16, dtb16, nw_group, *, chunk, seq_block):
    nseq, t, _ = c3.shape
    n = nseq * t
    sb, r = (1, GDN_TILE) if t >= GDN_TILE else (GDN_TILE // t, t)
    tiles = t // r
    low = BF16 if chunk >= 16 else F32
    blk = lambda w: pl.BlockSpec((sb, r, w), lambda s, i: (s, i, 0))
    full = lambda a: pl.BlockSpec(a.shape, lambda s, i: (0,) * a.ndim)
    flat = lambda w: pl.BlockSpec((GDN_TILE, w), lambda s, i: (s * tiles + i, 0))
    widths = (GDN_V, GDN_V, GDN_QK, GDN_QK, GDN_HEADS * chunk, 2 * GDN_HEADS)
    dtypes = (F32, low, low, low, low, F32)
    prep_out = dict(out_specs=[flat(wd) for wd in widths],
                    out_shape=[jax.ShapeDtypeStruct((n, wd), dt) for wd, dt in zip(widths, dtypes)],
                    compiler_params=pltpu.CompilerParams(dimension_semantics=("arbitrary", "arbitrary")))
    lane_dense = chunk == HEAD_DIM and sb == 1
    u, w, qd, kd, qk, gt = pl.pallas_call(
        _gdn_prep_pair_kernel if lane_dense else functools.partial(_gdn_prep_kernel, chunk=chunk),
        grid=(nseq // sb, tiles),
        in_specs=[blk(GDN_CONV_CH), pl.BlockSpec((sb, CONV_W - 1, GDN_CONV_CH), lambda s, i: (s, 0, 0)),
                  blk(2 * GDN_HEADS), full(conv_w), full(alog16), full(dtb16)],
        scratch_shapes=[pltpu.VMEM((sb, 8 + r, GDN_CONV_CH), F32)],
        name="gdn_prep_pair" if lane_dense else "gdn_prep", **prep_out)(c3, hist, ab3, conv_w, alog16, dtb16)

    n_chunks = t // chunk
    tok = lambda wd: pl.BlockSpec((seq_block, chunk, wd), lambda s, c: (s, c, 0))
    per_seq = pl.BlockSpec((seq_block,) + s0.shape[1:], lambda s, c: (s, 0, 0, 0))
    seq3 = lambda a: a.reshape(nseq, t, a.shape[-1])
    return pl.pallas_call(
        functools.partial(_gdn_scan_kernel, chunk=chunk, n_chunks=n_chunks),
        grid=(nseq // seq_block, n_chunks),
        in_specs=[tok(wd) for wd in widths] + [tok(GDN_V), per_seq,
                                               pl.BlockSpec(nw_group.shape, lambda s, c: (0, 0))],
        out_specs=[tok(GDN_V), per_seq],
        out_shape=[jax.ShapeDtypeStruct((nseq, t, GDN_V), F32), jax.ShapeDtypeStruct(s0.shape, F32)],
        scratch_shapes=[pltpu.VMEM((seq_block, GDN_HEADS // GDN_GROUP, GDN_GROUP_W, GDN_GROUP_W), F32)],
        compiler_params=pltpu.CompilerParams(dimension_semantics=("arbitrary", "arbitrary")),
        name="gdn_scan",
    )(seq3(u), seq3(w), seq3(qd), seq3(kd), seq3(qk), seq3(gt), z3, s0, nw_group)


def _outproj_kernel(x_ref, osw_ref, ogd_ref, wo_ref, gf_ref, wr_ref, br_ref,
                    h_ref, xn_ref, gates_ref, meta_ref, cnt_ref, run_ref):
    i = pl.program_id(0)
    tm = x_ref.shape[0]
    rows = wr_ref.shape[0]

    @pl.when(i == 0)
    def _():
        run_ref[...] = jnp.zeros_like(run_ref)

    h = (x_ref[...] + _dot(osw_ref[...].astype(BF16), wo_ref[0:SWA_Q, :])
         + _dot(ogd_ref[...].astype(BF16), wo_ref[SWA_Q:D_MIX, :]))
    h_ref[...] = h
    xn = _rms(h, gf_ref[...])
    xn_ref[...] = xn
    logits = _dot_nt(wr_ref[...], xn.astype(BF16))

    row = lax.broadcasted_iota(I32, (rows, tm), 0)
    bias = br_ref[...]
    top = lambda v: jnp.max(v, axis=0, keepdims=True)
    tot = lambda v: jnp.sum(v, axis=0, keepdims=True)
    first_at = lambda v: jnp.min(jnp.where(v == top(v), row, 2 * rows), axis=0, keepdims=True)
    is_g = (row >= N_EXPERTS) & (row < N_EXPERTS + N_GROUPS)
    lg = jnp.where(is_g, logits, NEG_BIG)
    pg = jnp.where(is_g, jnp.exp(lg - top(lg)), 0.0)
    group_p = pg / tot(pg)
    g_row = first_at(jnp.where(is_g, group_p + bias, NEG_BIG))
    g_w = tot(jnp.where(row == g_row, group_p, 0.0))
    sel = (row < N_EXPERTS) & ((row // EXPERTS_PER_GROUP) == (g_row - N_EXPERTS))
    le = jnp.where(sel, logits, NEG_BIG)
    pe = jnp.where(sel, jnp.exp(le - top(le)), 0.0)
    e_p = pe / tot(pe)
    score = jnp.where(sel, e_p + bias, NEG_BIG)
    i1 = first_at(score)
    i2 = first_at(jnp.where(row == i1, NEG_BIG, score))
    oh1 = row == i1
    oh2 = row == i2
    w1 = tot(jnp.where(oh1, e_p, 0.0))
    w2 = tot(jnp.where(oh2, e_p, 0.0))
    wsum = w1 + w2

    ohs = (oh1 | oh2).astype(BF16)
    earlier = (lax.broadcasted_iota(I32, (tm, tm), 0) < lax.broadcasted_iota(I32, (tm, tm), 1)).astype(BF16)
    before = _dot(ohs, earlier) + run_ref[...]
    r1 = tot(jnp.where(oh1, before, 0.0))
    r2 = tot(jnp.where(oh2, before, 0.0))
    run_ref[...] = run_ref[...] + jnp.sum(ohs.astype(F32), axis=1, keepdims=True)
    cnt_ref[...] = run_ref[...]

    zero = jnp.zeros_like(w1)
    meta = jnp.concatenate([i1.astype(F32), i2.astype(F32), r1, r2, g_w * (w1 / wsum), g_w * (w2 / wsum),
                            zero, zero], axis=0)
    meta_ref[...] = meta
    eye8 = (lax.broadcasted_iota(I32, (8, LANES), 0) == lax.broadcasted_iota(I32, (8, LANES), 1)).astype(F32)
    gates_ref[...] = sum(_dot_tn(part, eye8) for part in _split3(meta))[:, 4:6]


def _outproj(x2d, o_swa, o_gdn, wo_bf16, gain, w_router, b_router):
    n = x2d.shape[0]
    tm = ROW_TILE
    row = lambda w: pl.BlockSpec((tm, w), lambda i: (i, 0))
    full = lambda a: pl.BlockSpec(a.shape, lambda i: (0,) * a.ndim)
    return pl.pallas_call(
        _outproj_kernel,
        grid=(n // tm,),
        in_specs=[row(D_MODEL), row(SWA_Q), row(GDN_V), full(wo_bf16), full(gain), full(w_router), full(b_router)],
        out_specs=[row(D_MODEL), row(D_MODEL), row(2), pl.BlockSpec((8, tm), lambda i: (0, i)),
                   pl.BlockSpec((LANES, 1), lambda i: (0, 0))],
        out_shape=[jax.ShapeDtypeStruct((n, D_MODEL), F32), jax.ShapeDtypeStruct((n, D_MODEL), F32),
                   jax.ShapeDtypeStruct((n, 2), F32), jax.ShapeDtypeStruct((8, n), F32),
                   jax.ShapeDtypeStruct((LANES, 1), F32)],
        scratch_shapes=[pltpu.VMEM((LANES, 1), F32)],
        compiler_params=pltpu.CompilerParams(dimension_semantics=("arbitrary",)),
        name="outproj_router",
    )(x2d, o_swa, o_gdn, wo_bf16, gain, w_router, b_router)


def _row_copy(src_ref, src_row, dst_ref, dst_row, sem):
    return pltpu.make_async_copy(src_ref.at[pl.ds(src_row, 1)], dst_ref.at[pl.ds(dst_row, 1)], sem)


def _scatter_kernel(dest0_ref, dest1_ref, xp_ref, xs_ref, out_ref, aidx_ref, sem, *, tiles_p, n_tokens):
    i = pl.program_id(0)
    tm = xp_ref.shape[0]

    def run(src_ref):
        def issue(g, carry):
            for u in range(DMA_UNROLL):
                r = g * DMA_UNROLL + u
                d0, d1 = dest0_ref[r], dest1_ref[r]
                _row_copy(src_ref, r, out_ref, d0, sem).start(priority=0)
                _row_copy(src_ref, r, out_ref, d1, sem).start(priority=1)
                aidx_ref[d0] = i * tm + r
                aidx_ref[d1] = n_tokens + i * tm + r
            return carry

        lax.fori_loop(0, tm // DMA_UNROLL, issue, 0)
        for _ in range(2):
            pltpu.make_async_copy(src_ref, out_ref.at[pl.ds(0, tm)], sem).wait()

    @pl.when(i < tiles_p)
    def _():
        run(xp_ref)

    @pl.when(i >= tiles_p)
    def _():
        run(xs_ref)


def _scatter(dest3, xn_p, xn_s):
    tm = ROW_TILE
    tiles_p, tiles_s = xn_p.shape[0] // tm, xn_s.shape[0] // tm
    rows = 2 * (xn_p.shape[0] + xn_s.shape[0])
    return pl.pallas_call(
        functools.partial(_scatter_kernel, tiles_p=tiles_p, n_tokens=rows // 2),
        grid=(tiles_p + tiles_s,),
        in_specs=[pl.BlockSpec((tm,), lambda i: (i,), memory_space=pltpu.SMEM),
                  pl.BlockSpec((tm,), lambda i: (i,), memory_space=pltpu.SMEM),
                  pl.BlockSpec((tm, D_MODEL), lambda i: (jnp.minimum(i, tiles_p - 1), 0)),
                  pl.BlockSpec((tm, D_MODEL), lambda i: (jnp.maximum(i - tiles_p, 0), 0))],
        out_specs=[pl.BlockSpec(memory_space=pl.ANY), pl.BlockSpec(memory_space=pltpu.SMEM)],
        out_shape=[jax.ShapeDtypeStruct((rows, D_MODEL), F32), jax.ShapeDtypeStruct((rows,), I32)],
        scratch_shapes=[pltpu.SemaphoreType.DMA(())],
        compiler_params=pltpu.CompilerParams(dimension_semantics=("arbitrary",)),
        name="scatter_rows",
    )(dest3[0], dest3[1], xn_p, xn_s)


def _experts_kernel(blk_ref, exp_ref, lo_ref, hi_ref, first_ref, fresh_ref, pblk_ref,
                    aprev_ref, x_ref, wg_ref, wu_ref, wd_ref, ya_ref, wg_l, wu_l, wd_l, ybuf, sem,
                    *, n_items, n_blocks):
    j = pl.program_id(0)
    lo, hi = lo_ref[j], hi_ref[j]
    slot = lax.rem(blk_ref[j], 2)
    quarter = MOE_BLOCK // 4

    def wait_rows(s):
        pltpu.make_async_copy(ybuf.at[s], ya_ref.at[pl.ds(0, MOE_BLOCK)], sem.at[s]).wait()

    def send_rows(s, group):
        for row in range(group * quarter, (group + 1) * quarter):
            _row_copy(ybuf.at[s], row, ya_ref, aprev_ref[0, 0, row], sem.at[s]).start(priority=row % 2)

    @pl.when(fresh_ref[j] == 1)
    def _():
        wg_l[...] = wg_ref[0].astype(BF16)
        wu_l[...] = wu_ref[0].astype(BF16)
        wd_l[...] = wd_ref[0].astype(BF16)

    @pl.when((first_ref[j] == 1) & (blk_ref[j] >= 2))
    def _():
        wait_rows(slot)

    def item(is_first, send_prev):
        send = (lambda group: send_rows(1 - slot, group)) if send_prev else (lambda group: None)
        send(0)
        x = x_ref[...].astype(BF16)
        gate = _dot(x, wg_l[...])
        send(1)
        up = _dot(x, wu_l[...])
        hid = (gate * jax.nn.sigmoid(gate)) * up
        send(2)
        y = _dot(hid.astype(BF16), wd_l[...])
        send(3)
        r = lax.broadcasted_iota(I32, (MOE_BLOCK, 1), 0)
        mine = (r >= lo) & (r < hi)
        ybuf[slot] = jnp.where(mine, y, 0.0 if is_first else ybuf[slot])

    live = hi > lo
    pl.when(live & (first_ref[j] == 1) & (blk_ref[j] >= 1))(functools.partial(item, True, True))
    pl.when(live & (first_ref[j] == 1) & (blk_ref[j] == 0))(functools.partial(item, True, False))
    pl.when(live & (first_ref[j] == 0))(functools.partial(item, False, False))

    @pl.when(j == n_items - 1)
    def _():
        last_slot = (n_blocks - 1) % 2
        for group in range(4):
            send_rows(last_slot, group)
        wait_rows(1 - last_slot)
        wait_rows(last_slot)


def _experts(items, aidx, xs, w_gate, w_up, w_down):
    n_items = items[0].shape[0]
    n_blocks = xs.shape[0] // MOE_BLOCK
    assert n_items > n_blocks + N_EXPERTS - 1 and n_blocks >= 2
    xblk = pl.BlockSpec((MOE_BLOCK, D_MODEL), lambda j, blk, *_: (blk[j], 0))
    wspec = lambda a: pl.BlockSpec((1,) + a.shape[1:], lambda j, blk, ex, *_: (ex[j], 0, 0))
    return pl.pallas_call(
        functools.partial(_experts_kernel, n_items=n_items, n_blocks=n_blocks),
        grid_spec=pltpu.PrefetchScalarGridSpec(
            num_scalar_prefetch=len(items),
            grid=(n_items,),
            in_specs=[pl.BlockSpec((1, 1, MOE_BLOCK), lambda j, *pre: (pre[-1][j], 0, 0), memory_space=pltpu.SMEM),
                      xblk, wspec(w_gate), wspec(w_up), wspec(w_down)],
            out_specs=pl.BlockSpec(memory_space=pl.ANY),
            scratch_shapes=[pltpu.VMEM(w_gate.shape[1:], BF16), pltpu.VMEM(w_up.shape[1:], BF16),
                            pltpu.VMEM(w_down.shape[1:], BF16), pltpu.VMEM((2, MOE_BLOCK, D_MODEL), F32),
                            pltpu.SemaphoreType.DMA((2,))]),
        out_shape=jax.ShapeDtypeStruct(xs.shape, F32),
        compiler_params=pltpu.CompilerParams(dimension_semantics=("arbitrary",)),
        name="experts",
    )(*items, aidx.reshape(n_blocks, 1, MOE_BLOCK), xs, w_gate, w_up, w_down)


def _work_items(counts, total_rows):
    n_blocks = total_rows // MOE_BLOCK
    n_items = n_blocks + N_EXPERTS
    end = jnp.cumsum(counts)
    start = end - counts
    first_blk = start // MOE_BLOCK
    nb = jnp.where(counts > 0, (end - 1) // MOE_BLOCK - first_blk + 1, 0)
    item_end = jnp.cumsum(nb)
    used = item_end[-1]
    j = jnp.arange(n_items, dtype=I32)
    jj = jnp.minimum(j, used - 1)
    e = jnp.minimum(jnp.sum((item_end[None, :] <= jj[:, None]).astype(I32), axis=1), N_EXPERTS - 1)
    onehot = (e[:, None] == jnp.arange(N_EXPERTS, dtype=I32)[None, :]).astype(I32)
    pick = lambda a: jnp.sum(onehot * a[None, :], axis=1)
    blk = pick(first_blk) + (jj - (pick(item_end) - pick(nb)))
    lo = jnp.maximum(pick(start), blk * MOE_BLOCK) - blk * MOE_BLOCK
    hi = jnp.minimum(pick(end), (blk + 1) * MOE_BLOCK) - blk * MOE_BLOCK
    live = j < used
    lo = jnp.where(live, lo, 0)
    hi = jnp.where(live, hi, 0)
    prev_blk = jnp.concatenate([jnp.full((1,), -1, I32), blk[:-1]])
    first = (live & (blk != prev_blk)).astype(I32)
    prev_e = jnp.concatenate([jnp.full((1,), -1, I32), e[:-1]])
    fresh = (live & (e != prev_e)).astype(I32)
    pblk = jnp.where(j == n_items - 1, n_blocks - 1, jnp.maximum(blk - 1, 0))
    return (blk.astype(I32), e.astype(I32), lo.astype(I32), hi.astype(I32), first, fresh, pblk.astype(I32)), start


def _combine_kernel(gates_ref, h_ref, gf_ref, y0_ref, y1_ref, out_ref):
    gates = gates_ref[...]
    y = h_ref[...] + (y0_ref[...] * gates[:, 0:1] + y1_ref[...] * gates[:, 1:2])
    out_ref[...] = _rms(y, gf_ref[...])


def _combine(gates, h, gain, ya, first_token):
    n = h.shape[0]
    tm = COMBINE_TILE
    tile0 = first_token // tm
    slot_tiles = ya.shape[0] // 2 // tm
    row = lambda w: pl.BlockSpec((tm, w), lambda i: (i, 0))
    slot = lambda k: pl.BlockSpec((tm, D_MODEL), lambda i: (k * slot_tiles + tile0 + i, 0))
    return pl.pallas_call(
        _combine_kernel,
        grid=(n // tm,),
        in_specs=[row(2), row(D_MODEL), pl.BlockSpec(gain.shape, lambda i: (0, 0)), slot(0), slot(1)],
        out_specs=row(D_MODEL),
        out_shape=jax.ShapeDtypeStruct((n, D_MODEL), F32),
        compiler_params=pltpu.CompilerParams(dimension_semantics=("arbitrary",)),
        name="combine_norm",
    )(gates, h, gain, ya, ya)


def kernel(x_prompt, x_sample, cache_swa_k, cache_swa_v, state_gdn_conv, state_gdn, norm_mix, w_in, swa_sinks,
           gdn_conv_w, gdn_A_log, gdn_dt_bias, gdn_norm_w, w_out, norm_ffn, w_router_group, b_router_group,
           w_router_expert, b_router_expert, w_exp_gate, w_exp_up, w_exp_down, norm_final):
    depth = w_in.shape[0]
    assert depth == 1, "single trunk layer"
    bp, sp, _ = x_prompt.shape
    bs, ts, _ = x_sample.shape
    np_, ns = bp * sp, bs * ts
    l = 0

    w_in_b = w_in[l].astype(BF16)
    w_out_b = w_out[l].astype(BF16)
    g_mix = norm_mix[l].reshape(1, D_MODEL)
    g_ffn = norm_ffn[l].reshape(1, D_MODEL)
    g_fin = norm_final.reshape(1, D_MODEL)
    pad = LANES - N_EXPERTS - N_GROUPS
    w_router = jnp.concatenate([w_router_expert[l], w_router_group[l], jnp.zeros((D_MODEL, pad), F32)],
                               axis=1).astype(BF16).T
    b_router = jnp.concatenate([b_router_expert[l].reshape(-1), b_router_group[l],
                                jnp.zeros((pad,), F32)])[:, None]
    zeros8 = jnp.zeros((GDN_HEADS,), F32)
    alog16 = jnp.concatenate([gdn_A_log[l], zeros8])[None]
    dtb16 = jnp.concatenate([gdn_dt_bias[l], zeros8])[None]
    nw_group = jnp.tile(gdn_norm_w[l].reshape(1, HEAD_DIM), (1, GDN_GROUP))
    sinks = swa_sinks[l]

    q_p, k_p, v_p, c_p, z_p, ab_p = _inproj(x_prompt.reshape(np_, D_MODEL), g_mix, w_in_b)
    nblk = np_ // WINDOW
    k_p3 = k_p.reshape(nblk, WINDOW, SWA_KV)
    v_p3 = v_p.reshape(nblk, WINDOW, SWA_KV)
    (o_swa_p,) = _swa(sinks, q_p.reshape(nblk, WINDOW, SWA_Q), k_p3, v_p3, k_p3, v_p3,
                      bb=1, blocks_per_seq=sp // WINDOW, emit_cache=False)
    c_p3 = c_p.reshape(bp, sp, GDN_CONV_CH)
    o_gdn_p, s_fin_p = _gdn(c_p3, jnp.zeros((bp, CONV_W - 1, GDN_CONV_CH), F32), z_p.reshape(bp, sp, GDN_V),
                            ab_p.reshape(bp, sp, 2 * GDN_HEADS),
                            jnp.zeros((bp, GDN_HEADS, HEAD_DIM, HEAD_DIM), F32),
                            gdn_conv_w[l], alog16, dtb16, nw_group, chunk=GDN_CHUNK, seq_block=bp)
    h_p, xn_p, gates_p, meta_p, cnt_p = _outproj(
        x_prompt.reshape(np_, D_MODEL), o_swa_p.reshape(np_, SWA_Q), o_gdn_p.reshape(np_, GDN_V),
        w_out_b, g_ffn, w_router, b_router)

    q_s, k_s, v_s, c_s, z_s, ab_s = _inproj(x_sample.reshape(ns, D_MODEL), g_mix, w_in_b)
    o_swa_s, kcache_s, vcache_s = _swa(
        sinks, q_s.reshape(bs, ts, SWA_Q), k_s.reshape(bs, ts, SWA_KV), v_s.reshape(bs, ts, SWA_KV),
        cache_swa_k[l].reshape(bs, WINDOW, SWA_KV), cache_swa_v[l].reshape(bs, WINDOW, SWA_KV),
        bb=8, blocks_per_seq=None, emit_cache=True)
    c_s3 = c_s.reshape(bs, ts, GDN_CONV_CH)
    o_gdn_s, s_fin_s = _gdn(c_s3, state_gdn_conv[l], z_s.reshape(bs, ts, GDN_V),
                            ab_s.reshape(bs, ts, 2 * GDN_HEADS), state_gdn[l],
                            gdn_conv_w[l], alog16, dtb16, nw_group, chunk=ts, seq_block=8)
    h_s, xn_s, gates_s, meta_s, cnt_s = _outproj(
        x_sample.reshape(ns, D_MODEL), o_swa_s.reshape(ns, SWA_Q), o_gdn_s.reshape(ns, GDN_V),
        w_out_b, g_ffn, w_router, b_router)

    cnt_p_i = cnt_p[:N_EXPERTS, 0].astype(I32)
    cnt_s_i = cnt_s[:N_EXPERTS, 0].astype(I32)
    items, start = _work_items(cnt_p_i + cnt_s_i, 2 * (np_ + ns))
    expert_ids = jnp.arange(N_EXPERTS, dtype=I32)
    lookup = lambda table, ids: jnp.sum(jnp.where(ids[..., None] == expert_ids, table, 0), axis=-1)
    dest_p = lookup(start, meta_p[0:2].astype(I32)) + meta_p[2:4].astype(I32)
    dest_s = lookup(start + cnt_p_i, meta_s[0:2].astype(I32)) + meta_s[2:4].astype(I32)
    dest_all = jnp.concatenate([dest_p, dest_s], axis=1)
    xs, aidx = _scatter(dest_all, xn_p, xn_s)
    ya = _experts(items, aidx, xs, w_exp_gate[l], w_exp_up[l], w_exp_down[l])
    y_p = _combine(gates_p, h_p, g_fin, ya, 0)
    y_s = _combine(gates_s, h_s, g_fin, ya, np_)

    kv5 = lambda a, b: a.reshape(b, -1, SWA_KV_HEADS, HEAD_DIM)[None]
    return (y_p.reshape(bp, sp, D_MODEL), y_s.reshape(bs, ts, D_MODEL),
            kv5(k_p.reshape(bp, sp, SWA_KV)[:, -WINDOW:], bp), kv5(v_p.reshape(bp, sp, SWA_KV)[:, -WINDOW:], bp),
            kv5(kcache_s, bs), kv5(vcache_s, bs),
            c_p3[:, -(CONV_W - 1):][None], c_s3[:, -(CONV_W - 1):][None],
            s_fin_p[None], s_fin_s[None])
```

```python
import functools

import jax
import jax.numpy as jnp
from jax import lax
from jax.experimental import pallas as pl
from jax.experimental.pallas import tpu as pltpu

F32 = jnp.float32
BF16 = jnp.bfloat16
I32 = jnp.int32

D_MODEL = 1024
HEAD_DIM = 64
SWA_HEADS = 8
GDN_HEADS = 8
SWA_KV_HEADS = 2
GQA_GROUP = SWA_HEADS // SWA_KV_HEADS
WINDOW = 128
ATTN_SCALE = HEAD_DIM ** -0.5
CONV_W = 4
N_GROUPS = 8
EXPERTS_PER_GROUP = 8
N_EXPERTS = 64
D_EXPERT = 256
EPS = 1e-6

SWA_Q = SWA_HEADS * HEAD_DIM
SWA_KV = SWA_KV_HEADS * HEAD_DIM
GDN_QK = GDN_HEADS * HEAD_DIM
GDN_V = GDN_HEADS * HEAD_DIM
GDN_CONV_CH = 2 * GDN_QK + GDN_V
D_MIX = SWA_Q + GDN_V
D_IN = SWA_Q + 2 * SWA_KV + GDN_CONV_CH + GDN_V + 2 * GDN_HEADS
COL_K = SWA_Q
COL_V = COL_K + SWA_KV
COL_C = COL_V + SWA_KV
COL_Z = COL_C + GDN_CONV_CH
COL_AB = COL_Z + GDN_V

LANES = 128
NEG_BIG = -1e30
ROW_TILE = 512
MOE_BLOCK = 256
COMBINE_TILE = 256
DMA_UNROLL = 8
GDN_CHUNK = 64
GDN_TILE = 256
GDN_GROUP = 4
GDN_GROUP_W = GDN_GROUP * HEAD_DIM


def _rms(x, g):
    return x * lax.rsqrt(jnp.mean(x * x, axis=-1, keepdims=True) + EPS) * g


def _dot(a, b):
    return jnp.dot(a, b, preferred_element_type=F32)


def _dot_nt(a, b):
    return lax.dot_general(a, b, (((1,), (1,)), ((), ())), preferred_element_type=F32)


def _dot_tn(a, b):
    return lax.dot_general(a, b, (((0,), (0,)), ((), ())), preferred_element_type=F32)


def _split3(x):
    p1 = x.astype(BF16).astype(F32)
    r = x - p1
    p2 = r.astype(BF16).astype(F32)
    p3 = (r - p2).astype(BF16).astype(F32)
    return p1, p2, p3


def _inproj_kernel(x_ref, g_ref, w_ref, q_ref, k_ref, v_ref, c_ref, z_ref, ab_ref):
    x = x_ref[...]
    xb = _rms(x, g_ref[...]).astype(BF16)
    q_ref[...] = _dot(xb, w_ref[:, 0:COL_K])
    k_ref[...] = _dot(xb, w_ref[:, COL_K:COL_V])
    v_ref[...] = _dot(xb, w_ref[:, COL_V:COL_C])
    c_ref[...] = _dot(xb, w_ref[:, COL_C:COL_Z])
    z_ref[...] = _dot(xb, w_ref[:, COL_Z:COL_AB])
    ab_ref[...] = _dot(xb, w_ref[:, COL_AB:D_IN])


def _inproj(x2d, gain, w_bf16):
    n = x2d.shape[0]
    tm = ROW_TILE
    row = lambda w: pl.BlockSpec((tm, w), lambda i: (i, 0))
    full = lambda a: pl.BlockSpec(a.shape, lambda i: (0,) * a.ndim)
    widths = (SWA_Q, SWA_KV, SWA_KV, GDN_CONV_CH, GDN_V, 2 * GDN_HEADS)
    return pl.pallas_call(
        _inproj_kernel,
        grid=(n // tm,),
        in_specs=[row(D_MODEL), full(gain), full(w_bf16)],
        out_specs=[row(w) for w in widths],
        out_shape=[jax.ShapeDtypeStruct((n, w), F32) for w in widths],
        compiler_params=pltpu.CompilerParams(dimension_semantics=("arbitrary",)),
        name="inproj",
    )(x2d, gain, w_bf16)


def _swa_kernel(sink_ref, q_ref, kc_ref, vc_ref, kp_ref, vp_ref, o_ref, *cache_refs,
                bb, t, blocks_per_seq, emit_cache):
    if blocks_per_seq is None:
        has_prev = None
    else:
        has_prev = lax.rem(pl.program_id(0), blocks_per_seq) != 0
    rows = GQA_GROUP * t
    ri = lax.broadcasted_iota(I32, (rows, 1), 0)
    qi = lax.rem(ri, t)
    gi = ri // t
    if emit_cache:
        kj = lax.broadcasted_iota(I32, (rows, WINDOW), 1)
        mask = kj <= qi + (WINDOW - t)
    else:
        kj = lax.broadcasted_iota(I32, (rows, WINDOW + t), 1)
        mask = (kj <= qi + WINDOW) & (kj > qi)
        if has_prev is not None:
            mask = mask & (has_prev | (kj >= WINDOW))
    sinks = []
    for h in range(SWA_KV_HEADS):
        sink = jnp.zeros((rows, 1), F32)
        for g in range(GQA_GROUP):
            sink = jnp.where(gi == g, sink_ref[GQA_GROUP * h + g], sink)
        sinks.append(sink)
    chains = [(b, h) for b in range(bb) for h in range(SWA_KV_HEADS)]
    scores, values = [], []
    for b, h in chains:
        hs = slice(h * HEAD_DIM, (h + 1) * HEAD_DIM)
        kp, vp = kp_ref[b, :, hs], vp_ref[b, :, hs]
        kc, vc = kc_ref[b, :, hs], vc_ref[b, :, hs]
        if emit_cache:
            keys = jnp.concatenate([kp[t:], kc], axis=0)
            vals = jnp.concatenate([vp[t:], vc], axis=0)
            cache_refs[0][b, :, hs] = keys
            cache_refs[1][b, :, hs] = vals
        else:
            keys = jnp.concatenate([kp, kc], axis=0)
            vals = jnp.concatenate([vp, vc], axis=0)
        q4 = jnp.concatenate(
            [q_ref[b, :, (GQA_GROUP * h + g) * HEAD_DIM:(GQA_GROUP * h + g + 1) * HEAD_DIM]
             for g in range(GQA_GROUP)], axis=0)
        scores.append(_dot_nt(q4.astype(BF16), keys.astype(BF16)))
        values.append(vals.astype(BF16))
    probs, dens = [], []
    for (b, h), s in zip(chains, scores):
        s = jnp.where(mask, s * ATTN_SCALE, NEG_BIG)
        m = jnp.maximum(jnp.max(s, axis=-1, keepdims=True), sinks[h])
        p = jnp.exp(s - m)
        dens.append(jnp.sum(p, axis=-1, keepdims=True) + jnp.exp(sinks[h] - m))
        probs.append(p.astype(BF16))
    outs = [_dot(p, v) / den for p, v, den in zip(probs, values, dens)]
    for b in range(bb):
        o_ref[b] = jnp.concatenate([outs[b * SWA_KV_HEADS + h][g * t:(g + 1) * t]
                                    for h in range(SWA_KV_HEADS) for g in range(GQA_GROUP)], axis=-1)


def _swa(sinks, q3, k3, v3, kprev3, vprev3, *, bb, blocks_per_seq, emit_cache):
    nb, t, _ = q3.shape
    cur = lambda w: pl.BlockSpec((bb, t, w), lambda i: (i, 0, 0))
    if blocks_per_seq is None:
        prev = pl.BlockSpec((bb, WINDOW, SWA_KV), lambda i: (i, 0, 0))
    else:
        prev = pl.BlockSpec((bb, WINDOW, SWA_KV), lambda i: (jnp.maximum(i - 1, 0), 0, 0))
    out_specs = [cur(SWA_Q)]
    out_shape = [jax.ShapeDtypeStruct((nb, t, SWA_Q), F32)]
    if emit_cache:
        cache = pl.BlockSpec((bb, WINDOW, SWA_KV), lambda i: (i, 0, 0))
        out_specs += [cache, cache]
        out_shape += [jax.ShapeDtypeStruct((nb, WINDOW, SWA_KV), F32)] * 2
    return pl.pallas_call(
        functools.partial(_swa_kernel, bb=bb, t=t, blocks_per_seq=blocks_per_seq, emit_cache=emit_cache),
        grid=(nb // bb,),
        in_specs=[pl.BlockSpec(memory_space=pltpu.SMEM), cur(SWA_Q), cur(SWA_KV), cur(SWA_KV), prev, prev],
        out_specs=out_specs,
        out_shape=out_shape,
        compiler_params=pltpu.CompilerParams(dimension_semantics=("arbitrary",)),
        name="swa_cache" if emit_cache else "swa_band",
    )(sinks, q3, k3, v3, kprev3, vprev3)


def _gdn_prep_kernel(c_ref, hist_ref, ab_ref, cw_ref, alog_ref, dtb_ref,
                     u_ref, w_ref, qd_ref, kd_ref, qk_ref, gt_ref, cbuf_ref, *, chunk):
    sb, r, _ = c_ref.shape
    tp = sb * r
    cn = chunk
    low = w_ref.dtype
    hist_rows = CONV_W - 1

    @pl.when(pl.program_id(1) == 0)
    def _():
        cbuf_ref[:, 8 - hist_rows:8, :] = hist_ref[...]

    cbuf_ref[:, 8:8 + r, :] = c_ref[...]
    conv = cbuf_ref[:, 8 - hist_rows:8 - hist_rows + r, :] * cw_ref[0:1, :]
    for i in range(1, CONV_W):
        conv = conv + cbuf_ref[:, 8 - hist_rows + i:8 - hist_rows + i + r, :] * cw_ref[i:i + 1, :]
    tail = cbuf_ref[:, 8 + r - hist_rows:8 + r, :]
    cbuf_ref[:, 8 - hist_rows:8, :] = tail
    conv = (conv * jax.nn.sigmoid(conv)).reshape(tp, GDN_CONV_CH)

    ab = ab_ref[...].reshape(tp, 2 * GDN_HEADS)
    is_g = lax.broadcasted_iota(I32, (1, 2 * GDN_HEADS), 1) < GDN_HEADS
    g = jnp.where(is_g, -jnp.exp(alog_ref[...]) * jax.nn.softplus(ab + dtb_ref[...]), 0.0)
    beta = jax.nn.sigmoid(ab)

    ri = lax.broadcasted_iota(I32, (tp, tp), 0)
    ci = lax.broadcasted_iota(I32, (tp, tp), 1)
    same = (ri // cn) == (ci // cn)
    causal = same & (ri >= ci)
    strict = same & (ri > ci)
    eye = (ri == ci).astype(F32)
    stack = jnp.concatenate([causal.astype(BF16), same.astype(BF16)], axis=0)
    both = sum(_dot(stack, p.astype(BF16)) for p in _split3(g))
    gc, gl = both[:tp], both[tp:]
    e16 = lax.broadcasted_iota(I32, (2 * GDN_HEADS, 2 * GDN_HEADS), 0)
    eye16 = (e16 == lax.broadcasted_iota(I32, (2 * GDN_HEADS, 2 * GDN_HEADS), 1)).astype(BF16)
    gc_row = sum(_dot_nt(eye16, p.astype(BF16)) for p in _split3(gc))
    gt_ref[...] = jnp.exp(gl)
    fold = (lax.broadcasted_iota(I32, (tp, cn), 0) % cn == lax.broadcasted_iota(I32, (tp, cn), 1)).astype(low)

    levels = cn.bit_length() - 2
    us, ws, qds, kds, qks = [], [], [], [], []
    for h in range(GDN_HEADS):
        q = conv[:, h * HEAD_DIM:(h + 1) * HEAD_DIM]
        k = conv[:, GDN_QK + h * HEAD_DIM:GDN_QK + (h + 1) * HEAD_DIM]
        v = conv[:, 2 * GDN_QK + h * HEAD_DIM:2 * GDN_QK + (h + 1) * HEAD_DIM]
        q = q * lax.rsqrt(jnp.sum(q * q, axis=-1, keepdims=True) + EPS) * (HEAD_DIM ** -0.5)
        k = k * lax.rsqrt(jnp.sum(k * k, axis=-1, keepdims=True) + EPS)
        b_h = beta[:, GDN_HEADS + h:GDN_HEADS + h + 1]
        g_c = gc[:, h:h + 1]
        decay = jnp.exp(jnp.where(causal, g_c - gc_row[h:h + 1, :], NEG_BIG))
        eg = jnp.exp(g_c)
        kb = k * b_h
        k_l = k.astype(low)
        lmat = jnp.where(strict, _dot_nt(kb.astype(low), k_l) * decay, 0.0)
        qk = _dot_nt(q.astype(low), k_l) * decay
        qks.append(_dot(qk.astype(low), fold))
        l_l = lmat.astype(low)
        x = eye - lmat
        p = _dot(l_l, l_l)
        for lev in range(1, levels + 1):
            p_l = p.astype(low)
            if lev < levels:
                xp = _dot(jnp.concatenate([x.astype(low), p_l], axis=0), p_l)
                x = x + xp[:tp]
                p = xp[tp:]
            else:
                x = x + _dot(x.astype(low), p_l)
        rhs = jnp.concatenate([v * b_h, kb * eg], axis=-1)
        sol = _dot(x.astype(low), rhs.astype(low))
        us.append(sol[:, :HEAD_DIM])
        ws.append(sol[:, HEAD_DIM:])
        qds.append(q * eg)
        kds.append(k * jnp.exp(gl[:, h:h + 1] - g_c))
    u_ref[...] = jnp.concatenate(us, axis=-1)
    w_ref[...] = jnp.concatenate(ws, axis=-1).astype(low)
    qd_ref[...] = jnp.concatenate(qds, axis=-1).astype(low)
    kd_ref[...] = jnp.concatenate(kds, axis=-1).astype(low)
    qk_ref[...] = jnp.concatenate(qks, axis=-1).astype(low)


def _gdn_prep_pair_kernel(c_ref, hist_ref, ab_ref, cw_ref, alog_ref, dtb_ref,
                          u_ref, w_ref, qd_ref, kd_ref, qk_ref, gt_ref, cbuf_ref):
    _, r, _ = c_ref.shape
    tp = r
    cn = HEAD_DIM
    pair_w = 2 * HEAD_DIM
    hist_rows = CONV_W - 1

    @pl.when(pl.program_id(1) == 0)
    def _():
        cbuf_ref[:, 8 - hist_rows:8, :] = hist_ref[...]

    cbuf_ref[:, 8:8 + r, :] = c_ref[...]
    conv = cbuf_ref[:, 8 - hist_rows:8 - hist_rows + r, :] * cw_ref[0:1, :]
    for i in range(1, CONV_W):
        conv = conv + cbuf_ref[:, 8 - hist_rows + i:8 - hist_rows + i + r, :] * cw_ref[i:i + 1, :]
    tail = cbuf_ref[:, 8 + r - hist_rows:8 + r, :]
    cbuf_ref[:, 8 - hist_rows:8, :] = tail
    conv = (conv * jax.nn.sigmoid(conv)).reshape(tp, GDN_CONV_CH)

    ab = ab_ref[...].reshape(tp, 2 * GDN_HEADS)
    is_g = lax.broadcasted_iota(I32, (1, 2 * GDN_HEADS), 1) < GDN_HEADS
    g = jnp.where(is_g, -jnp.exp(alog_ref[...]) * jax.nn.softplus(ab + dtb_ref[...]), 0.0)
    beta = jax.nn.sigmoid(ab)

    ri = lax.broadcasted_iota(I32, (tp, tp), 0)
    ci = lax.broadcasted_iota(I32, (tp, tp), 1)
    same = (ri // cn) == (ci // cn)
    same_l = same.astype(BF16)
    stack = jnp.concatenate([(same & (ri >= ci)).astype(BF16), same_l], axis=0)
    both = sum(_dot(stack, p.astype(BF16)) for p in _split3(g))
    gc, gl = both[:tp], both[tp:]
    gt_ref[...] = jnp.exp(gl)

    lane = lax.broadcasted_iota(I32, (tp, pair_w), 1)
    c_in = lax.broadcasted_iota(I32, (tp, pair_w), 0) % cn
    j_in = lane % cn
    left = lane < cn
    causal = c_in >= j_in
    strict = c_in > j_in
    diag = c_in == j_in
    eye = diag.astype(F32)
    bdmask = ((lax.broadcasted_iota(I32, (pair_w, pair_w), 0) // cn)
              == (lax.broadcasted_iota(I32, (pair_w, pair_w), 1) // cn))
    ones_bd = bdmask.astype(BF16)

    def bd(m):
        return jnp.where(bdmask, jnp.concatenate([m, m], axis=0), 0.0).astype(BF16)

    def head_sum(x):
        hi = x.astype(BF16)
        lo = (x - hi.astype(F32)).astype(BF16)
        return _dot(hi, ones_bd) + _dot(lo, ones_bd)

    levels = cn.bit_length() - 2
    chunks = [slice(n * cn, (n + 1) * cn) for n in range(tp // cn)]
    n_pairs = GDN_HEADS // 2
    pairs = range(n_pairs)
    pick = lambda m, off, p: jnp.where(left, m[:, off + 2 * p:off + 2 * p + 1], m[:, off + 2 * p + 1:off + 2 * p + 2])
    third = lambda i, p: conv[:, i * GDN_QK + p * pair_w:i * GDN_QK + (p + 1) * pair_w]
    qs = [third(0, p) for p in pairs]
    ks = [third(1, p) for p in pairs]
    qs = [q * lax.rsqrt(head_sum(q * q) + EPS) * (HEAD_DIM ** -0.5) for q in qs]
    ks = [k * lax.rsqrt(head_sum(k * k) + EPS) for k in ks]
    gcps = [pick(gc, 0, p) for p in pairs]
    rowms = [sum(_dot(same_l, part.astype(BF16)) for part in _split3(jnp.where(diag, gcp, 0.0))) for gcp in gcps]
    decays = [jnp.exp(jnp.where(causal, gcp - rowm, NEG_BIG)) for gcp, rowm in zip(gcps, rowms)]
    lmats, vbs, kbegs = [], [], []
    for p in pairs:
        ls = slice(p * pair_w, (p + 1) * pair_w)
        q, k, gcp, decay = qs[p], ks[p], gcps[p], decays[p]
        bp = pick(beta, GDN_HEADS, p)
        eg = jnp.exp(gcp)
        kb = k * bp
        vbs.append(third(2, p) * bp)
        kbegs.append(kb * eg)
        qd_ref[:, ls] = (q * eg).astype(BF16)
        kd_ref[:, ls] = (k * jnp.exp(pick(gl, 0, p) - gcp)).astype(BF16)
        q_l, kb_l = q.astype(BF16), kb.astype(BF16)
        kbd = [bd(k[rs]) for rs in chunks]
        kk = jnp.concatenate([_dot_nt(kb_l[rs], kbd[n]) for n, rs in enumerate(chunks)], axis=0)
        qk = jnp.concatenate([_dot_nt(q_l[rs], kbd[n]) for n, rs in enumerate(chunks)], axis=0)
        qk_ref[:, ls] = (qk * decay).astype(BF16)
        lmats.append(jnp.where(strict, kk * decay, 0.0))
    bodies = [(p, rs) for p in range(n_pairs) for rs in chunks]
    xs = [eye[rs] - lmats[p][rs] for p, rs in bodies]
    ps = [_dot(lmats[p][rs].astype(BF16), bd(lmats[p][rs])) for p, rs in bodies]
    for lev in range(1, levels + 1):
        pbd = [bd(pm) for pm in ps]
        if lev < levels:
            xp = [_dot(jnp.concatenate([xm, pm], axis=0).astype(BF16), wm) for xm, pm, wm in zip(xs, ps, pbd)]
            xs = [xm + m[:cn] for xm, m in zip(xs, xp)]
            ps = [m[cn:] for m in xp]
        else:
            xs = [xm + _dot(xm.astype(BF16), wm) for xm, wm in zip(xs, pbd)]
    x_l = [xm.astype(BF16) for xm in xs]
    us = [_dot(xm, bd(vbs[p][rs])) for xm, (p, rs) in zip(x_l, bodies)]
    ws = [_dot(xm, bd(kbegs[p][rs])) for xm, (p, rs) in zip(x_l, bodies)]
    nc = len(chunks)
    for p in range(n_pairs):
        ls = slice(p * pair_w, (p + 1) * pair_w)
        u_ref[:, ls] = jnp.concatenate(us[p * nc:(p + 1) * nc], axis=0)
        w_ref[:, ls] = jnp.concatenate(ws[p * nc:(p + 1) * nc], axis=0).astype(BF16)


def _gdn_scan_kernel(u_ref, w_ref, qd_ref, kd_ref, qk_ref, gt_ref, z_ref, s0_ref, nw_ref,
                     o_ref, sfin_ref, sbd_ref, *, chunk, n_chunks):
    bb = u_ref.shape[0]
    cn = chunk
    low = w_ref.dtype
    gw = GDN_GROUP_W
    ni = pl.program_id(1)

    @pl.when(ni == 0)
    def _():
        sbd_ref[...] = jnp.zeros_like(sbd_ref)
        for b in range(bb):
            for h in range(GDN_HEADS):
                gi, hh = divmod(h, GDN_GROUP)
                ds = slice(hh * HEAD_DIM, (hh + 1) * HEAD_DIM)
                sbd_ref[b, gi, ds, ds] = s0_ref[b, h]

    bdmask = ((lax.broadcasted_iota(I32, (gw, gw), 0) // HEAD_DIM)
              == (lax.broadcasted_iota(I32, (gw, gw), 1) // HEAD_DIM))
    ones_bd = bdmask.astype(BF16)
    vmask = ((lax.broadcasted_iota(I32, (GDN_GROUP * cn, gw), 0) // cn)
             == (lax.broadcasted_iota(I32, (GDN_GROUP * cn, gw), 1) // HEAD_DIM))
    e_row = lax.broadcasted_iota(I32, (2 * GDN_HEADS, gw), 0)
    e_col = lax.broadcasted_iota(I32, (2 * GDN_HEADS, gw), 1) // HEAD_DIM
    chains = [(b, gi) for b in range(bb) for gi in range(GDN_HEADS // GDN_GROUP)]
    lanes = lambda gi: slice(gi * gw, (gi + 1) * gw)
    states = [sbd_ref[b, gi] for b, gi in chains]
    states_l = [s.astype(low) for s in states]
    v_new = [u_ref[b, :, lanes(gi)] - _dot(w_ref[b, :, lanes(gi)], s_l) for (b, gi), s_l in zip(chains, states_l)]
    q_s = [_dot(qd_ref[b, :, lanes(gi)], s_l) for (b, gi), s_l in zip(chains, states_l)]
    v_l = [v.astype(low) for v in v_new]
    outs = []
    for (b, gi), s, v, qs in zip(chains, states, v_l, q_s):
        vbd = jnp.where(vmask, jnp.concatenate([v] * GDN_GROUP, axis=0), jnp.zeros((), low))
        outs.append(qs + _dot(qk_ref[b, :, gi * GDN_GROUP * cn:(gi + 1) * GDN_GROUP * cn], vbd))
        upd = _dot_tn(kd_ref[b, :, lanes(gi)], v)
        expand = (e_row == e_col + gi * GDN_GROUP).astype(BF16)
        gte = sum(_dot(p.astype(BF16), expand) for p in _split3(gt_ref[b, 0:8, :]))[0:1]
        sbd_ref[b, gi] = s * gte + jnp.where(bdmask, upd, 0.0)
    for (b, gi), o in zip(chains, outs):
        o2 = o * o
        hi = o2.astype(BF16)
        lo = (o2 - hi.astype(F32)).astype(BF16)
        ms = (_dot(hi, ones_bd) + _dot(lo, ones_bd)) * (1.0 / HEAD_DIM)
        zg = z_ref[b, :, lanes(gi)]
        o_ref[b, :, lanes(gi)] = o * lax.rsqrt(ms + EPS) * nw_ref[...] * (zg * jax.nn.sigmoid(zg))

    @pl.when(ni == n_chunks - 1)
    def _():
        for b in range(bb):
            for h in range(GDN_HEADS):
                gi, hh = divmod(h, GDN_GROUP)
                ds = slice(hh * HEAD_DIM, (hh + 1) * HEAD_DIM)
                sfin_ref[b, h] = sbd_ref[b, gi, ds, ds]


def _gdn(c3, hist, z3, ab3, s0, conv_w, alog16, dtb16, nw_group, *, chunk, seq_block):
    nseq, t, _ = c3.shape
    n = nseq * t
    sb, r = (1, GDN_TILE) if t >= GDN_TILE else (GDN_TILE // t, t)
    tiles = t // r
    low = BF16 if chunk >= 16 else F32
    blk = lambda w: pl.BlockSpec((sb, r, w), lambda s, i: (s, i, 0))
    full = lambda a: pl.BlockSpec(a.shape, lambda s, i: (0,) * a.ndim)
    flat = lambda w: pl.BlockSpec((GDN_TILE, w), lambda s, i: (s * tiles + i, 0))
    widths = (GDN_V, GDN_V, GDN_QK, GDN_QK, GDN_HEADS * chunk, 2 * GDN_HEADS)
    dtypes = (F32, low, low, low, low, F32)
    prep_out = dict(out_specs=[flat(wd) for wd in widths],
                    out_shape=[jax.ShapeDtypeStruct((n, wd), dt) for wd, dt in zip(widths, dtypes)],
                    compiler_params=pltpu.CompilerParams(dimension_semantics=("arbitrary", "arbitrary")))
    lane_dense = chunk == HEAD_DIM and sb == 1
    u, w, qd, kd, qk, gt = pl.pallas_call(
        _gdn_prep_pair_kernel if lane_dense else functools.partial(_gdn_prep_kernel, chunk=chunk),
        grid=(nseq // sb, tiles),
        in_specs=[blk(GDN_CONV_CH), pl.BlockSpec((sb, CONV_W - 1, GDN_CONV_CH), lambda s, i: (s, 0, 0)),
                  blk(2 * GDN_HEADS), full(conv_w), full(alog16), full(dtb16)],
        scratch_shapes=[pltpu.VMEM((sb, 8 + r, GDN_CONV_CH), F32)],
        name="gdn_prep_pair" if lane_dense else "gdn_prep", **prep_out)(c3, hist, ab3, conv_w, alog16, dtb16)

    n_chunks = t // chunk
    tok = lambda wd: pl.BlockSpec((seq_block, chunk, wd), lambda s, c: (s, c, 0))
    per_seq = pl.BlockSpec((seq_block,) + s0.shape[1:], lambda s, c: (s, 0, 0, 0))
    seq3 = lambda a: a.reshape(nseq, t, a.shape[-1])
    return pl.pallas_call(
        functools.partial(_gdn_scan_kernel, chunk=chunk, n_chunks=n_chunks),
        grid=(nseq // seq_block, n_chunks),
        in_specs=[tok(wd) for wd in widths] + [tok(GDN_V), per_seq,
                                               pl.BlockSpec(nw_group.shape, lambda s, c: (0, 0))],
        out_specs=[tok(GDN_V), per_seq],
        out_shape=[jax.ShapeDtypeStruct((nseq, t, GDN_V), F32), jax.ShapeDtypeStruct(s0.shape, F32)],
        scratch_shapes=[pltpu.VMEM((seq_block, GDN_HEADS // GDN_GROUP, GDN_GROUP_W, GDN_GROUP_W), F32)],
        compiler_params=pltpu.CompilerParams(dimension_semantics=("arbitrary", "arbitrary")),
        name="gdn_scan",
    )(seq3(u), seq3(w), seq3(qd), seq3(kd), seq3(qk), seq3(gt), z3, s0, nw_group)


def _outproj_kernel(xp_ref, oswp_ref, ogdp_ref, xs_ref, osws_ref, ogds_ref, *rest, tiles_p):
    i = pl.program_id(0)
    run_ref = rest[-1]

    @pl.when(i == 0)
    def _():
        run_ref[...] = jnp.zeros_like(run_ref)

    pl.when(i < tiles_p)(functools.partial(_outproj_tile, xp_ref, oswp_ref, ogdp_ref, *rest))
    pl.when(i >= tiles_p)(functools.partial(_outproj_tile, xs_ref, osws_ref, ogds_ref, *rest))


def _outproj_tile(x_ref, osw_ref, ogd_ref, wo_ref, gf_ref, wr_ref, br_ref,
                  h_ref, xn_ref, gates_ref, meta_ref, cnt_ref, run_ref):
    tm = x_ref.shape[0]
    rows = wr_ref.shape[0]

    h = (x_ref[...] + _dot(osw_ref[...].astype(BF16), wo_ref[0:SWA_Q, :])
         + _dot(ogd_ref[...].astype(BF16), wo_ref[SWA_Q:D_MIX, :]))
    h_ref[...] = h
    xn = _rms(h, gf_ref[...])
    xn_ref[...] = xn
    logits = _dot_nt(wr_ref[...], xn.astype(BF16))

    row = lax.broadcasted_iota(I32, (rows, tm), 0)
    bias = br_ref[...]
    top = lambda v: jnp.max(v, axis=0, keepdims=True)
    tot = lambda v: jnp.sum(v, axis=0, keepdims=True)
    first_at = lambda v: jnp.min(jnp.where(v == top(v), row, 2 * rows), axis=0, keepdims=True)
    is_g = (row >= N_EXPERTS) & (row < N_EXPERTS + N_GROUPS)
    lg = jnp.where(is_g, logits, NEG_BIG)
    pg = jnp.where(is_g, jnp.exp(lg - top(lg)), 0.0)
    group_p = pg / tot(pg)
    g_row = first_at(jnp.where(is_g, group_p + bias, NEG_BIG))
    g_w = tot(jnp.where(row == g_row, group_p, 0.0))
    sel = (row < N_EXPERTS) & ((row // EXPERTS_PER_GROUP) == (g_row - N_EXPERTS))
    le = jnp.where(sel, logits, NEG_BIG)
    pe = jnp.where(sel, jnp.exp(le - top(le)), 0.0)
    e_p = pe / tot(pe)
    score = jnp.where(sel, e_p + bias, NEG_BIG)
    i1 = first_at(score)
    i2 = first_at(jnp.where(row == i1, NEG_BIG, score))
    oh1 = row == i1
    oh2 = row == i2
    w1 = tot(jnp.where(oh1, e_p, 0.0))
    w2 = tot(jnp.where(oh2, e_p, 0.0))
    wsum = w1 + w2

    ohs = (oh1 | oh2).astype(BF16)
    earlier = (lax.broadcasted_iota(I32, (tm, tm), 0) < lax.broadcasted_iota(I32, (tm, tm), 1)).astype(BF16)
    before = _dot(ohs, earlier) + run_ref[...]
    r1 = tot(jnp.where(oh1, before, 0.0))
    r2 = tot(jnp.where(oh2, before, 0.0))
    run_ref[...] = run_ref[...] + jnp.sum(ohs.astype(F32), axis=1, keepdims=True)
    cnt_ref[...] = run_ref[...]

    zero = jnp.zeros_like(w1)
    meta = jnp.concatenate([i1.astype(F32), i2.astype(F32), r1, r2, g_w * (w1 / wsum), g_w * (w2 / wsum),
                            zero, zero], axis=0)
    meta_ref[...] = meta
    eye8 = (lax.broadcasted_iota(I32, (8, LANES), 0) == lax.broadcasted_iota(I32, (8, LANES), 1)).astype(F32)
    gates_ref[...] = sum(_dot_tn(part, eye8) for part in _split3(meta))[:, 4:6]


def _outproj(prompt, sample, wo_bf16, gain, w_router, b_router):
    tm = ROW_TILE
    tiles_p, tiles_s = prompt[0].shape[0] // tm, sample[0].shape[0] // tm
    n = (tiles_p + tiles_s) * tm
    row = lambda w: pl.BlockSpec((tm, w), lambda i: (i, 0))
    rowp = lambda w: pl.BlockSpec((tm, w), lambda i: (jnp.minimum(i, tiles_p - 1), 0))
    rows_ = lambda w: pl.BlockSpec((tm, w), lambda i: (jnp.maximum(i - tiles_p, 0), 0))
    full = lambda a: pl.BlockSpec(a.shape, lambda i: (0,) * a.ndim)
    widths = (D_MODEL, SWA_Q, GDN_V)
    return pl.pallas_call(
        functools.partial(_outproj_kernel, tiles_p=tiles_p),
        grid=(tiles_p + tiles_s,),
        in_specs=[rowp(w) for w in widths] + [rows_(w) for w in widths]
                 + [full(wo_bf16), full(gain), full(w_router), full(b_router)],
        out_specs=[row(D_MODEL), row(D_MODEL), row(2), pl.BlockSpec((8, tm), lambda i: (0, i)),
                   pl.BlockSpec((LANES, 1), lambda i: (0, 0))],
        out_shape=[jax.ShapeDtypeStruct((n, D_MODEL), F32), jax.ShapeDtypeStruct((n, D_MODEL), F32),
                   jax.ShapeDtypeStruct((n, 2), F32), jax.ShapeDtypeStruct((8, n), F32),
                   jax.ShapeDtypeStruct((LANES, 1), F32)],
        scratch_shapes=[pltpu.VMEM((LANES, 1), F32)],
        compiler_params=pltpu.CompilerParams(dimension_semantics=("arbitrary",)),
        name="outproj_router",
    )(*prompt, *sample, wo_bf16, gain, w_router, b_router)


def _row_copy(src_ref, src_row, dst_ref, dst_row, sem):
    return pltpu.make_async_copy(src_ref.at[pl.ds(src_row, 1)], dst_ref.at[pl.ds(dst_row, 1)], sem)


def _invert_kernel(dest0_ref, dest1_ref, aidx_ref, *, n_tokens):
    i = pl.program_id(0)
    tm = dest0_ref.shape[0]

    def body(g, carry):
        for u in range(DMA_UNROLL):
            r = g * DMA_UNROLL + u
            aidx_ref[dest0_ref[r]] = i * tm + r
            aidx_ref[dest1_ref[r]] = n_tokens + i * tm + r
        return carry

    lax.fori_loop(0, tm // DMA_UNROLL, body, 0)


def _invert(dest):
    n = dest.shape[1]
    tm = 2 * ROW_TILE
    idx = pl.BlockSpec((tm,), lambda i: (i,), memory_space=pltpu.SMEM)
    return pl.pallas_call(
        functools.partial(_invert_kernel, n_tokens=n),
        grid=(n // tm,),
        in_specs=[idx, idx],
        out_specs=pl.BlockSpec(memory_space=pltpu.SMEM),
        out_shape=jax.ShapeDtypeStruct((2 * n,), I32),
        compiler_params=pltpu.CompilerParams(dimension_semantics=("arbitrary",)),
        name="invert_order",
    )(dest[0], dest[1])


def _experts_kernel(blk_ref, exp_ref, lo_ref, hi_ref, first_ref, fresh_ref, pblk_ref, nblk_ref,
                    aprev_ref, anext_ref, xn_ref, wg_ref, wu_ref, wd_ref, ya_ref,
                    wg_l, wu_l, wd_l, xbuf, ybuf, gsem, ssem, *, n_items, n_blocks, n_tokens):
    j = pl.program_id(0)
    lo, hi = lo_ref[j], hi_ref[j]
    blk = blk_ref[j]
    slot = lax.rem(blk, 2)
    quarter = MOE_BLOCK // 4

    def wait_sent(s):
        pltpu.make_async_copy(ybuf.at[s], ya_ref.at[pl.ds(0, MOE_BLOCK)], ssem.at[s]).wait()

    def wait_fetched(s):
        pltpu.make_async_copy(xn_ref.at[pl.ds(0, MOE_BLOCK)], xbuf.at[s], gsem.at[s]).wait()

    def send_rows(s, group):
        for row in range(group * quarter, (group + 1) * quarter):
            _row_copy(ybuf.at[s], row, ya_ref, aprev_ref[0, 0, row], ssem.at[s]).start(priority=row % 2)

    def fetch_rows(s, a_ref, group):
        for row in range(group * quarter, (group + 1) * quarter):
            a = a_ref[0, 0, row]
            tok = jnp.where(a >= n_tokens, a - n_tokens, a)
            _row_copy(xn_ref, tok, xbuf.at[s], row, gsem.at[s]).start(priority=row % 2)

    @pl.when(j == 0)
    def _():
        for group in range(4):
            fetch_rows(0, aprev_ref, group)

    @pl.when(fresh_ref[j] == 1)
    def _():
        wg_l[...] = wg_ref[0].astype(BF16)
        wu_l[...] = wu_ref[0].astype(BF16)
        wd_l[...] = wd_ref[0].astype(BF16)

    @pl.when(first_ref[j] == 1)
    def _():
        wait_fetched(slot)

    @pl.when((first_ref[j] == 1) & (blk >= 2))
    def _():
        wait_sent(slot)

    def item(is_first, send_prev, fetch_next):
        def between(group):
            if send_prev:
                send_rows(1 - slot, group)
            if fetch_next:
                fetch_rows(1 - slot, anext_ref, group)

        between(0)
        x = xbuf[slot].astype(BF16)
        gate = _dot(x, wg_l[...])
        between(1)
        up = _dot(x, wu_l[...])
        hid = (gate * jax.nn.sigmoid(gate)) * up
        between(2)
        y = _dot(hid.astype(BF16), wd_l[...])
        between(3)
        r = lax.broadcasted_iota(I32, (MOE_BLOCK, 1), 0)
        mine = (r >= lo) & (r < hi)
        ybuf[slot] = jnp.where(mine, y, 0.0 if is_first else ybuf[slot])

    live = hi > lo
    first = first_ref[j] == 1
    pl.when(live & first & (blk >= 1) & (blk < n_blocks - 1))(functools.partial(item, True, True, True))
    pl.when(live & first & (blk == 0))(functools.partial(item, True, False, True))
    pl.when(live & first & (blk == n_blocks - 1))(functools.partial(item, True, True, False))
    pl.when(live & jnp.logical_not(first))(functools.partial(item, False, False, False))

    @pl.when(j == n_items - 1)
    def _():
        last_slot = (n_blocks - 1) % 2
        for group in range(4):
            send_rows(last_slot, group)
        wait_sent(1 - last_slot)
        wait_sent(last_slot)


def _experts(items, aidx, xn, w_gate, w_up, w_down):
    n_items = items[0].shape[0]
    n_tokens = xn.shape[0]
    n_blocks = 2 * n_tokens // MOE_BLOCK
    assert n_items > n_blocks + N_EXPERTS - 1 and n_blocks >= 2
    wspec = lambda a: pl.BlockSpec((1,) + a.shape[1:], lambda j, blk, ex, *_: (ex[j], 0, 0))
    ablk = lambda which: pl.BlockSpec((1, 1, MOE_BLOCK), lambda j, *pre: (pre[which][j], 0, 0),
                                      memory_space=pltpu.SMEM)
    aidx3 = aidx.reshape(n_blocks, 1, MOE_BLOCK)
    return pl.pallas_call(
        functools.partial(_experts_kernel, n_items=n_items, n_blocks=n_blocks, n_tokens=n_tokens),
        grid_spec=pltpu.PrefetchScalarGridSpec(
            num_scalar_prefetch=len(items),
            grid=(n_items,),
            in_specs=[ablk(6), ablk(7), pl.BlockSpec(memory_space=pl.ANY),
                      wspec(w_gate), wspec(w_up), wspec(w_down)],
            out_specs=pl.BlockSpec(memory_space=pl.ANY),
            scratch_shapes=[pltpu.VMEM(w_gate.shape[1:], BF16), pltpu.VMEM(w_up.shape[1:], BF16),
                            pltpu.VMEM(w_down.shape[1:], BF16),
                            pltpu.VMEM((2, MOE_BLOCK, D_MODEL), F32), pltpu.VMEM((2, MOE_BLOCK, D_MODEL), F32),
                            pltpu.SemaphoreType.DMA((2,)), pltpu.SemaphoreType.DMA((2,))]),
        out_shape=jax.ShapeDtypeStruct((2 * n_tokens, D_MODEL), F32),
        compiler_params=pltpu.CompilerParams(dimension_semantics=("arbitrary",)),
        name="experts",
    )(*items, aidx3, aidx3, xn, w_gate, w_up, w_down)


def _work_items(counts, total_rows):
    n_blocks = total_rows // MOE_BLOCK
    n_items = n_blocks + N_EXPERTS
    end = jnp.cumsum(counts)
    start = end - counts
    first_blk = start // MOE_BLOCK
    nb = jnp.where(counts > 0, (end - 1) // MOE_BLOCK - first_blk + 1, 0)
    item_end = jnp.cumsum(nb)
    used = item_end[-1]
    j = jnp.arange(n_items, dtype=I32)
    jj = jnp.minimum(j, used - 1)
    e = jnp.minimum(jnp.sum((item_end[None, :] <= jj[:, None]).astype(I32), axis=1), N_EXPERTS - 1)
    onehot = (e[:, None] == jnp.arange(N_EXPERTS, dtype=I32)[None, :]).astype(I32)
    pick = lambda a: jnp.sum(onehot * a[None, :], axis=1)
    blk = pick(first_blk) + (jj - (pick(item_end) - pick(nb)))
    lo = jnp.maximum(pick(start), blk * MOE_BLOCK) - blk * MOE_BLOCK
    hi = jnp.minimum(pick(end), (blk + 1) * MOE_BLOCK) - blk * MOE_BLOCK
    live = j < used
    lo = jnp.where(live, lo, 0)
    hi = jnp.where(live, hi, 0)
    prev_blk = jnp.concatenate([jnp.full((1,), -1, I32), blk[:-1]])
    first = (live & (blk != prev_blk)).astype(I32)
    prev_e = jnp.concatenate([jnp.full((1,), -1, I32), e[:-1]])
    fresh = (live & (e != prev_e)).astype(I32)
    pblk = jnp.where(j == n_items - 1, n_blocks - 1, jnp.maximum(blk - 1, 0))
    nblk = jnp.minimum(blk + 1, n_blocks - 1)
    return (blk.astype(I32), e.astype(I32), lo.astype(I32), hi.astype(I32), first, fresh,
            pblk.astype(I32), nblk.astype(I32)), start


def _combine_kernel(gates_ref, h_ref, gf_ref, y0_ref, y1_ref, out_ref):
    gates = gates_ref[...]
    y = h_ref[...] + (y0_ref[...] * gates[:, 0:1] + y1_ref[...] * gates[:, 1:2])
    out_ref[...] = _rms(y, gf_ref[...])


def _combine(gates, h, gain, ya, first_token, n):
    tm = COMBINE_TILE
    tile0 = first_token // tm
    slot_tiles = ya.shape[0] // 2 // tm
    row = lambda w: pl.BlockSpec((tm, w), lambda i: (i, 0))
    tok = lambda w: pl.BlockSpec((tm, w), lambda i: (tile0 + i, 0))
    slot = lambda k: pl.BlockSpec((tm, D_MODEL), lambda i: (k * slot_tiles + tile0 + i, 0))
    return pl.pallas_call(
        _combine_kernel,
        grid=(n // tm,),
        in_specs=[tok(2), tok(D_MODEL), pl.BlockSpec(gain.shape, lambda i: (0, 0)), slot(0), slot(1)],
        out_specs=row(D_MODEL),
        out_shape=jax.ShapeDtypeStruct((n, D_MODEL), F32),
        compiler_params=pltpu.CompilerParams(dimension_semantics=("arbitrary",)),
        name="combine_norm",
    )(gates, h, gain, ya, ya)


def kernel(x_prompt, x_sample, cache_swa_k, cache_swa_v, state_gdn_conv, state_gdn, norm_mix, w_in, swa_sinks,
           gdn_conv_w, gdn_A_log, gdn_dt_bias, gdn_norm_w, w_out, norm_ffn, w_router_group, b_router_group,
           w_router_expert, b_router_expert, w_exp_gate, w_exp_up, w_exp_down, norm_final):
    depth = w_in.shape[0]
    assert depth == 1, "single trunk layer"
    bp, sp, _ = x_prompt.shape
    bs, ts, _ = x_sample.shape
    np_, ns = bp * sp, bs * ts
    l = 0

    w_in_b = w_in[l].astype(BF16)
    w_out_b = w_out[l].astype(BF16)
    g_mix = norm_mix[l].reshape(1, D_MODEL)
    g_ffn = norm_ffn[l].reshape(1, D_MODEL)
    g_fin = norm_final.reshape(1, D_MODEL)
    pad = LANES - N_EXPERTS - N_GROUPS
    w_router = jnp.concatenate([w_router_expert[l], w_router_group[l], jnp.zeros((D_MODEL, pad), F32)],
                               axis=1).astype(BF16).T
    b_router = jnp.concatenate([b_router_expert[l].reshape(-1), b_router_group[l],
                                jnp.zeros((pad,), F32)])[:, None]
    zeros8 = jnp.zeros((GDN_HEADS,), F32)
    alog16 = jnp.concatenate([gdn_A_log[l], zeros8])[None]
    dtb16 = jnp.concatenate([gdn_dt_bias[l], zeros8])[None]
    nw_group = jnp.tile(gdn_norm_w[l].reshape(1, HEAD_DIM), (1, GDN_GROUP))
    sinks = swa_sinks[l]

    q_p, k_p, v_p, c_p, z_p, ab_p = _inproj(x_prompt.reshape(np_, D_MODEL), g_mix, w_in_b)
    nblk = np_ // WINDOW
    k_p3 = k_p.reshape(nblk, WINDOW, SWA_KV)
    v_p3 = v_p.reshape(nblk, WINDOW, SWA_KV)
    (o_swa_p,) = _swa(sinks, q_p.reshape(nblk, WINDOW, SWA_Q), k_p3, v_p3, k_p3, v_p3,
                      bb=1, blocks_per_seq=sp // WINDOW, emit_cache=False)
    c_p3 = c_p.reshape(bp, sp, GDN_CONV_CH)
    o_gdn_p, s_fin_p = _gdn(c_p3, jnp.zeros((bp, CONV_W - 1, GDN_CONV_CH), F32), z_p.reshape(bp, sp, GDN_V),
                            ab_p.reshape(bp, sp, 2 * GDN_HEADS),
                            jnp.zeros((bp, GDN_HEADS, HEAD_DIM, HEAD_DIM), F32),
                            gdn_conv_w[l], alog16, dtb16, nw_group, chunk=GDN_CHUNK, seq_block=bp)
    q_s, k_s, v_s, c_s, z_s, ab_s = _inproj(x_sample.reshape(ns, D_MODEL), g_mix, w_in_b)
    o_swa_s, kcache_s, vcache_s = _swa(
        sinks, q_s.reshape(bs, ts, SWA_Q), k_s.reshape(bs, ts, SWA_KV), v_s.reshape(bs, ts, SWA_KV),
        cache_swa_k[l].reshape(bs, WINDOW, SWA_KV), cache_swa_v[l].reshape(bs, WINDOW, SWA_KV),
        bb=8, blocks_per_seq=None, emit_cache=True)
    c_s3 = c_s.reshape(bs, ts, GDN_CONV_CH)
    o_gdn_s, s_fin_s = _gdn(c_s3, state_gdn_conv[l], z_s.reshape(bs, ts, GDN_V),
                            ab_s.reshape(bs, ts, 2 * GDN_HEADS), state_gdn[l],
                            gdn_conv_w[l], alog16, dtb16, nw_group, chunk=ts, seq_block=8)
    h, xn, gates, meta, cnt = _outproj(
        (x_prompt.reshape(np_, D_MODEL), o_swa_p.reshape(np_, SWA_Q), o_gdn_p.reshape(np_, GDN_V)),
        (x_sample.reshape(ns, D_MODEL), o_swa_s.reshape(ns, SWA_Q), o_gdn_s.reshape(ns, GDN_V)),
        w_out_b, g_ffn, w_router, b_router)
    items, start = _work_items(cnt[:N_EXPERTS, 0].astype(I32), 2 * (np_ + ns))
    expert_ids = jnp.arange(N_EXPERTS, dtype=I32)
    lookup = lambda table, ids: jnp.sum(jnp.where(ids[..., None] == expert_ids, table, 0), axis=-1)
    dest = lookup(start, meta[0:2].astype(I32)) + meta[2:4].astype(I32)
    ya = _experts(items, _invert(dest), xn, w_exp_gate[l], w_exp_up[l], w_exp_down[l])
    y_p = _combine(gates, h, g_fin, ya, 0, np_)
    y_s = _combine(gates, h, g_fin, ya, np_, ns)

    kv5 = lambda a, b: a.reshape(b, -1, SWA_KV_HEADS, HEAD_DIM)[None]
    return (y_p.reshape(bp, sp, D_MODEL), y_s.reshape(bs, ts, D_MODEL),
            kv5(k_p.reshape(bp, sp, SWA_KV)[:, -WINDOW:], bp), kv5(v_p.reshape(bp, sp, SWA_KV)[:, -WINDOW:], bp),
            kv5(kcache_s, bs), kv5(vcache_s, bs),
            c_p3[:, -(CONV_W - 1):][None], c_s3[:, -(CONV_W - 1):][None],
            s_fin_p[None], s_fin_s[None])
```

```python
import functools

import jax
import jax.numpy as jnp
from jax import lax
from jax.experimental import pallas as pl
from jax.experimental.pallas import tpu as pltpu

F32 = jnp.float32
BF16 = jnp.bfloat16
I32 = jnp.int32

D_MODEL = 1024
HEAD_DIM = 64
SWA_HEADS = 8
GDN_HEADS = 8
SWA_KV_HEADS = 2
GQA_GROUP = SWA_HEADS // SWA_KV_HEADS
WINDOW = 128
ATTN_SCALE = HEAD_DIM ** -0.5
CONV_W = 4
N_GROUPS = 8
EXPERTS_PER_GROUP = 8
N_EXPERTS = 64
D_EXPERT = 256
EPS = 1e-6

SWA_Q = SWA_HEADS * HEAD_DIM
SWA_KV = SWA_KV_HEADS * HEAD_DIM
GDN_QK = GDN_HEADS * HEAD_DIM
GDN_V = GDN_HEADS * HEAD_DIM
GDN_CONV_CH = 2 * GDN_QK + GDN_V
D_MIX = SWA_Q + GDN_V
D_IN = SWA_Q + 2 * SWA_KV + GDN_CONV_CH + GDN_V + 2 * GDN_HEADS
COL_K = SWA_Q
COL_V = COL_K + SWA_KV
COL_C = COL_V + SWA_KV
COL_Z = COL_C + GDN_CONV_CH
COL_AB = COL_Z + GDN_V

LANES = 128
NEG_BIG = -1e30
ROW_TILE = 512
MOE_BLOCK = 256
COMBINE_TILE = 512
DMA_UNROLL = 8
SWA_BLOCKS = 2
GDN_CHUNK = 64
GDN_TILE = 256
GDN_GROUP = 4
GDN_GROUP_W = GDN_GROUP * HEAD_DIM


def _rms(x, g):
    return x * lax.rsqrt(jnp.mean(x * x, axis=-1, keepdims=True) + EPS) * g


def _dot(a, b):
    return jnp.dot(a, b, preferred_element_type=F32)


def _dot_nt(a, b):
    return lax.dot_general(a, b, (((1,), (1,)), ((), ())), preferred_element_type=F32)


def _dot_tn(a, b):
    return lax.dot_general(a, b, (((0,), (0,)), ((), ())), preferred_element_type=F32)


def _split3(x):
    p1 = x.astype(BF16).astype(F32)
    r = x - p1
    p2 = r.astype(BF16).astype(F32)
    p3 = (r - p2).astype(BF16).astype(F32)
    return p1, p2, p3


def _inproj_kernel(x_ref, g_ref, w_ref, q_ref, k_ref, v_ref, c_ref, z_ref, ab_ref):
    x = x_ref[...]
    xb = _rms(x, g_ref[...]).astype(BF16)
    q_ref[...] = _dot(xb, w_ref[:, 0:COL_K])
    k_ref[...] = _dot(xb, w_ref[:, COL_K:COL_V])
    v_ref[...] = _dot(xb, w_ref[:, COL_V:COL_C])
    c_ref[...] = _dot(xb, w_ref[:, COL_C:COL_Z])
    z_ref[...] = _dot(xb, w_ref[:, COL_Z:COL_AB])
    ab_ref[...] = _dot(xb, w_ref[:, COL_AB:D_IN])


def _inproj(x2d, gain, w_bf16):
    n = x2d.shape[0]
    tm = ROW_TILE
    row = lambda w: pl.BlockSpec((tm, w), lambda i: (i, 0))
    full = lambda a: pl.BlockSpec(a.shape, lambda i: (0,) * a.ndim)
    widths = (SWA_Q, SWA_KV, SWA_KV, GDN_CONV_CH, GDN_V, 2 * GDN_HEADS)
    return pl.pallas_call(
        _inproj_kernel,
        grid=(n // tm,),
        in_specs=[row(D_MODEL), full(gain), full(w_bf16)],
        out_specs=[row(w) for w in widths],
        out_shape=[jax.ShapeDtypeStruct((n, w), F32) for w in widths],
        compiler_params=pltpu.CompilerParams(dimension_semantics=("arbitrary",)),
        name="inproj",
    )(x2d, gain, w_bf16)


def _swa_kernel(sink_ref, q_ref, kc_ref, vc_ref, kp_ref, vp_ref, o_ref, *cache_refs,
                bb, t, blocks_per_seq, emit_cache):
    rows = GQA_GROUP * t
    ri = lax.broadcasted_iota(I32, (rows, 1), 0)
    qi = lax.rem(ri, t)
    gi = ri // t
    if emit_cache:
        kj = lax.broadcasted_iota(I32, (rows, WINDOW), 1)
        mask = kj <= qi + (WINDOW - t)
        mask_first = mask
    else:
        kj = lax.broadcasted_iota(I32, (rows, WINDOW + t), 1)
        mask = (kj <= qi + WINDOW) & (kj > qi)
        has_prev = lax.rem(pl.program_id(0) * bb, blocks_per_seq) != 0
        mask_first = mask & (has_prev | (kj >= WINDOW))
    sinks = []
    for h in range(SWA_KV_HEADS):
        sink = jnp.zeros((rows, 1), F32)
        for g in range(GQA_GROUP):
            sink = jnp.where(gi == g, sink_ref[GQA_GROUP * h + g], sink)
        sinks.append(sink)
    chains = [(b, h) for b in range(bb) for h in range(SWA_KV_HEADS)]
    scores, values = [], []
    for b, h in chains:
        hs = slice(h * HEAD_DIM, (h + 1) * HEAD_DIM)
        kc, vc = kc_ref[b, :, hs], vc_ref[b, :, hs]
        if emit_cache:
            kp, vp = kp_ref[b, :, hs], vp_ref[b, :, hs]
            keys = jnp.concatenate([kp[t:], kc], axis=0)
            vals = jnp.concatenate([vp[t:], vc], axis=0)
            cache_refs[0][b, :, hs] = keys
            cache_refs[1][b, :, hs] = vals
        else:
            kp, vp = (kp_ref[0, :, hs], vp_ref[0, :, hs]) if b == 0 else (kc_ref[b - 1, :, hs], vc_ref[b - 1, :, hs])
            keys = jnp.concatenate([kp, kc], axis=0)
            vals = jnp.concatenate([vp, vc], axis=0)
        q4 = jnp.concatenate(
            [q_ref[b, :, (GQA_GROUP * h + g) * HEAD_DIM:(GQA_GROUP * h + g + 1) * HEAD_DIM]
             for g in range(GQA_GROUP)], axis=0)
        scores.append(_dot_nt(q4.astype(BF16), keys.astype(BF16)))
        values.append(vals.astype(BF16))
    probs, dens = [], []
    for (b, h), s in zip(chains, scores):
        s = jnp.where(mask_first if b == 0 else mask, s * ATTN_SCALE, NEG_BIG)
        m = jnp.maximum(jnp.max(s, axis=-1, keepdims=True), sinks[h])
        p = jnp.exp(s - m)
        dens.append(jnp.sum(p, axis=-1, keepdims=True) + jnp.exp(sinks[h] - m))
        probs.append(p.astype(BF16))
    outs = [_dot(p, v) / den for p, v, den in zip(probs, values, dens)]
    for b in range(bb):
        o_ref[b] = jnp.concatenate([outs[b * SWA_KV_HEADS + h][g * t:(g + 1) * t]
                                    for h in range(SWA_KV_HEADS) for g in range(GQA_GROUP)], axis=-1)


def _swa(sinks, q3, k3, v3, kprev3, vprev3, *, bb, blocks_per_seq, emit_cache):
    nb, t, _ = q3.shape
    cur = lambda w: pl.BlockSpec((bb, t, w), lambda i: (i, 0, 0))
    if emit_cache:
        prev = pl.BlockSpec((bb, WINDOW, SWA_KV), lambda i: (i, 0, 0))
    else:
        assert blocks_per_seq % bb == 0
        prev = pl.BlockSpec((1, WINDOW, SWA_KV), lambda i: (jnp.maximum(i * bb - 1, 0), 0, 0))
    out_specs = [cur(SWA_Q)]
    out_shape = [jax.ShapeDtypeStruct((nb, t, SWA_Q), F32)]
    if emit_cache:
        cache = pl.BlockSpec((bb, WINDOW, SWA_KV), lambda i: (i, 0, 0))
        out_specs += [cache, cache]
        out_shape += [jax.ShapeDtypeStruct((nb, WINDOW, SWA_KV), F32)] * 2
    return pl.pallas_call(
        functools.partial(_swa_kernel, bb=bb, t=t, blocks_per_seq=blocks_per_seq, emit_cache=emit_cache),
        grid=(nb // bb,),
        in_specs=[pl.BlockSpec(memory_space=pltpu.SMEM), cur(SWA_Q), cur(SWA_KV), cur(SWA_KV), prev, prev],
        out_specs=out_specs,
        out_shape=out_shape,
        compiler_params=pltpu.CompilerParams(dimension_semantics=("arbitrary",)),
        name="swa_cache" if emit_cache else "swa_band",
    )(sinks, q3, k3, v3, kprev3, vprev3)


def _gdn_prep_kernel(c_ref, hist_ref, ab_ref, cw_ref, alog_ref, dtb_ref,
                     u_ref, w_ref, qd_ref, kd_ref, qk_ref, gt_ref, cbuf_ref, *, chunk):
    sb, r, _ = c_ref.shape
    tp = sb * r
    cn = chunk
    low = w_ref.dtype
    hist_rows = CONV_W - 1

    @pl.when(pl.program_id(1) == 0)
    def _():
        cbuf_ref[:, 8 - hist_rows:8, :] = hist_ref[...]

    cbuf_ref[:, 8:8 + r, :] = c_ref[...]
    conv = cbuf_ref[:, 8 - hist_rows:8 - hist_rows + r, :] * cw_ref[0:1, :]
    for i in range(1, CONV_W):
        conv = conv + cbuf_ref[:, 8 - hist_rows + i:8 - hist_rows + i + r, :] * cw_ref[i:i + 1, :]
    tail = cbuf_ref[:, 8 + r - hist_rows:8 + r, :]
    cbuf_ref[:, 8 - hist_rows:8, :] = tail
    conv = (conv * jax.nn.sigmoid(conv)).reshape(tp, GDN_CONV_CH)

    ab = ab_ref[...].reshape(tp, 2 * GDN_HEADS)
    is_g = lax.broadcasted_iota(I32, (1, 2 * GDN_HEADS), 1) < GDN_HEADS
    g = jnp.where(is_g, -jnp.exp(alog_ref[...]) * jax.nn.softplus(ab + dtb_ref[...]), 0.0)
    beta = jax.nn.sigmoid(ab)

    ri = lax.broadcasted_iota(I32, (tp, tp), 0)
    ci = lax.broadcasted_iota(I32, (tp, tp), 1)
    same = (ri // cn) == (ci // cn)
    causal = same & (ri >= ci)
    strict = same & (ri > ci)
    eye = (ri == ci).astype(F32)
    stack = jnp.concatenate([causal.astype(BF16), same.astype(BF16)], axis=0)
    both = sum(_dot(stack, p.astype(BF16)) for p in _split3(g))
    gc, gl = both[:tp], both[tp:]
    e16 = lax.broadcasted_iota(I32, (2 * GDN_HEADS, 2 * GDN_HEADS), 0)
    eye16 = (e16 == lax.broadcasted_iota(I32, (2 * GDN_HEADS, 2 * GDN_HEADS), 1)).astype(BF16)
    gc_row = sum(_dot_nt(eye16, p.astype(BF16)) for p in _split3(gc))
    gt_ref[...] = jnp.exp(gl)
    fold = (lax.broadcasted_iota(I32, (tp, cn), 0) % cn == lax.broadcasted_iota(I32, (tp, cn), 1)).astype(low)

    levels = cn.bit_length() - 2
    heads = range(GDN_HEADS)
    lanes_of = lambda i, h: slice(i * GDN_QK + h * HEAD_DIM, i * GDN_QK + (h + 1) * HEAD_DIM)
    qs = [conv[:, lanes_of(0, h)] for h in heads]
    ks = [conv[:, lanes_of(1, h)] for h in heads]
    qs = [q * lax.rsqrt(jnp.sum(q * q, axis=-1, keepdims=True) + EPS) * (HEAD_DIM ** -0.5) for q in qs]
    ks = [k * lax.rsqrt(jnp.sum(k * k, axis=-1, keepdims=True) + EPS) for k in ks]
    g_cs = [gc[:, h:h + 1] for h in heads]
    b_hs = [beta[:, GDN_HEADS + h:GDN_HEADS + h + 1] for h in heads]
    decays = [jnp.exp(jnp.where(causal, g_cs[h] - gc_row[h:h + 1, :], NEG_BIG)) for h in heads]
    egs = [jnp.exp(g_c) for g_c in g_cs]
    kbs = [k * b_h for k, b_h in zip(ks, b_hs)]
    k_ls = [k.astype(low) for k in ks]
    lmats = [jnp.where(strict, _dot_nt(kb.astype(low), k_l) * decay, 0.0) for kb, k_l, decay in zip(kbs, k_ls, decays)]
    qks = [_dot((_dot_nt(q.astype(low), k_l) * decay).astype(low), fold) for q, k_l, decay in zip(qs, k_ls, decays)]
    xs = [eye - lmat for lmat in lmats]
    ps = [_dot(lmat.astype(low), lmat.astype(low)) for lmat in lmats]
    for lev in range(1, levels + 1):
        p_ls = [p.astype(low) for p in ps]
        if lev < levels:
            xp = [_dot(jnp.concatenate([x.astype(low), p_l], axis=0), p_l) for x, p_l in zip(xs, p_ls)]
            xs = [x + m[:tp] for x, m in zip(xs, xp)]
            ps = [m[tp:] for m in xp]
        else:
            xs = [x + _dot(x.astype(low), p_l) for x, p_l in zip(xs, p_ls)]
    sols = [_dot(xs[h].astype(low),
                 jnp.concatenate([conv[:, lanes_of(2, h)] * b_hs[h], kbs[h] * egs[h]], axis=-1).astype(low))
            for h in heads]
    u_ref[...] = jnp.concatenate([sol[:, :HEAD_DIM] for sol in sols], axis=-1)
    w_ref[...] = jnp.concatenate([sol[:, HEAD_DIM:] for sol in sols], axis=-1).astype(low)
    qd_ref[...] = jnp.concatenate([q * eg for q, eg in zip(qs, egs)], axis=-1).astype(low)
    kd_ref[...] = jnp.concatenate([ks[h] * jnp.exp(gl[:, h:h + 1] - g_cs[h]) for h in heads], axis=-1).astype(low)
    qk_ref[...] = jnp.concatenate(qks, axis=-1).astype(low)


def _gdn_prep_pair_kernel(c_ref, hist_ref, ab_ref, cw_ref, alog_ref, dtb_ref,
                          u_ref, w_ref, qd_ref, kd_ref, qk_ref, gt_ref, cbuf_ref):
    _, r, _ = c_ref.shape
    tp = r
    cn = HEAD_DIM
    pair_w = 2 * HEAD_DIM
    hist_rows = CONV_W - 1

    @pl.when(pl.program_id(1) == 0)
    def _():
        cbuf_ref[:, 8 - hist_rows:8, :] = hist_ref[...]

    cbuf_ref[:, 8:8 + r, :] = c_ref[...]
    conv = cbuf_ref[:, 8 - hist_rows:8 - hist_rows + r, :] * cw_ref[0:1, :]
    for i in range(1, CONV_W):
        conv = conv + cbuf_ref[:, 8 - hist_rows + i:8 - hist_rows + i + r, :] * cw_ref[i:i + 1, :]
    tail = cbuf_ref[:, 8 + r - hist_rows:8 + r, :]
    cbuf_ref[:, 8 - hist_rows:8, :] = tail
    conv = (conv * jax.nn.sigmoid(conv)).reshape(tp, GDN_CONV_CH)

    ab = ab_ref[...].reshape(tp, 2 * GDN_HEADS)
    is_g = lax.broadcasted_iota(I32, (1, 2 * GDN_HEADS), 1) < GDN_HEADS
    g = jnp.where(is_g, -jnp.exp(alog_ref[...]) * jax.nn.softplus(ab + dtb_ref[...]), 0.0)
    beta = jax.nn.sigmoid(ab)

    ri = lax.broadcasted_iota(I32, (tp, tp), 0)
    ci = lax.broadcasted_iota(I32, (tp, tp), 1)
    same = (ri // cn) == (ci // cn)
    same_l = same.astype(BF16)
    stack = jnp.concatenate([(same & (ri >= ci)).astype(BF16), same_l], axis=0)
    both = sum(_dot(stack, p.astype(BF16)) for p in _split3(g))
    gc, gl = both[:tp], both[tp:]
    gt_ref[...] = jnp.exp(gl)

    lane = lax.broadcasted_iota(I32, (tp, pair_w), 1)
    c_in = lax.broadcasted_iota(I32, (tp, pair_w), 0) % cn
    j_in = lane % cn
    left = lane < cn
    causal = c_in >= j_in
    strict = c_in > j_in
    diag = c_in == j_in
    eye = diag.astype(F32)
    bdmask = ((lax.broadcasted_iota(I32, (pair_w, pair_w), 0) // cn)
              == (lax.broadcasted_iota(I32, (pair_w, pair_w), 1) // cn))
    ones_bd = bdmask.astype(BF16)

    def bd(m):
        return jnp.where(bdmask, jnp.concatenate([m, m], axis=0), 0.0).astype(BF16)

    def head_sum(x):
        hi = x.astype(BF16)
        lo = (x - hi.astype(F32)).astype(BF16)
        return _dot(hi, ones_bd) + _dot(lo, ones_bd)

    levels = cn.bit_length() - 2
    chunks = [slice(n * cn, (n + 1) * cn) for n in range(tp // cn)]
    n_pairs = GDN_HEADS // 2
    pairs = range(n_pairs)
    pick = lambda m, off, p: jnp.where(left, m[:, off + 2 * p:off + 2 * p + 1], m[:, off + 2 * p + 1:off + 2 * p + 2])
    third = lambda i, p: conv[:, i * GDN_QK + p * pair_w:i * GDN_QK + (p + 1) * pair_w]
    qs = [third(0, p) for p in pairs]
    ks = [third(1, p) for p in pairs]
    qs = [q * lax.rsqrt(head_sum(q * q) + EPS) * (HEAD_DIM ** -0.5) for q in qs]
    ks = [k * lax.rsqrt(head_sum(k * k) + EPS) for k in ks]
    gcps = [pick(gc, 0, p) for p in pairs]
    rowms = [sum(_dot(same_l, part.astype(BF16)) for part in _split3(jnp.where(diag, gcp, 0.0))) for gcp in gcps]
    decays = [jnp.exp(jnp.where(causal, gcp - rowm, NEG_BIG)) for gcp, rowm in zip(gcps, rowms)]
    lmats, vbs, kbegs = [], [], []
    for p in pairs:
        ls = slice(p * pair_w, (p + 1) * pair_w)
        q, k, gcp, decay = qs[p], ks[p], gcps[p], decays[p]
        bp = pick(beta, GDN_HEADS, p)
        eg = jnp.exp(gcp)
        kb = k * bp
        vbs.append(third(2, p) * bp)
        kbegs.append(kb * eg)
        qd_ref[:, ls] = (q * eg).astype(BF16)
        kd_ref[:, ls] = (k * jnp.exp(pick(gl, 0, p) - gcp)).astype(BF16)
        q_l, kb_l = q.astype(BF16), kb.astype(BF16)
        kbd = [bd(k[rs]) for rs in chunks]
        kk = jnp.concatenate([_dot_nt(kb_l[rs], kbd[n]) for n, rs in enumerate(chunks)], axis=0)
        qk = jnp.concatenate([_dot_nt(q_l[rs], kbd[n]) for n, rs in enumerate(chunks)], axis=0)
        qk_ref[:, ls] = (qk * decay).astype(BF16)
        lmats.append(jnp.where(strict, kk * decay, 0.0))
    bodies = [(p, rs) for p in range(n_pairs) for rs in chunks]
    xs = [eye[rs] - lmats[p][rs] for p, rs in bodies]
    ps = [_dot(lmats[p][rs].astype(BF16), bd(lmats[p][rs])) for p, rs in bodies]
    for lev in range(1, levels + 1):
        pbd = [bd(pm) for pm in ps]
        if lev < levels:
            xp = [_dot(jnp.concatenate([xm, pm], axis=0).astype(BF16), wm) for xm, pm, wm in zip(xs, ps, pbd)]
            xs = [xm + m[:cn] for xm, m in zip(xs, xp)]
            ps = [m[cn:] for m in xp]
        else:
            xs = [xm + _dot(xm.astype(BF16), wm) for xm, wm in zip(xs, pbd)]
    x_l = [xm.astype(BF16) for xm in xs]
    us = [_dot(xm, bd(vbs[p][rs])) for xm, (p, rs) in zip(x_l, bodies)]
    ws = [_dot(xm, bd(kbegs[p][rs])) for xm, (p, rs) in zip(x_l, bodies)]
    nc = len(chunks)
    for p in range(n_pairs):
        ls = slice(p * pair_w, (p + 1) * pair_w)
        u_ref[:, ls] = jnp.concatenate(us[p * nc:(p + 1) * nc], axis=0)
        w_ref[:, ls] = jnp.concatenate(ws[p * nc:(p + 1) * nc], axis=0).astype(BF16)


def _gdn_scan_kernel(u_ref, w_ref, qd_ref, kd_ref, qk_ref, gt_ref, z_ref, s0_ref, nw_ref,
                     o_ref, sfin_ref, sbd_ref, *, chunk, n_chunks):
    bb = u_ref.shape[0]
    cn = chunk
    low = w_ref.dtype
    gw = GDN_GROUP_W
    ni = pl.program_id(1)

    @pl.when(ni == 0)
    def _():
        sbd_ref[...] = jnp.zeros_like(sbd_ref)
        for b in range(bb):
            for h in range(GDN_HEADS):
                gi, hh = divmod(h, GDN_GROUP)
                ds = slice(hh * HEAD_DIM, (hh + 1) * HEAD_DIM)
                sbd_ref[b, gi, ds, ds] = s0_ref[b, h]

    bdmask = ((lax.broadcasted_iota(I32, (gw, gw), 0) // HEAD_DIM)
              == (lax.broadcasted_iota(I32, (gw, gw), 1) // HEAD_DIM))
    ones_bd = bdmask.astype(BF16)
    vmask = ((lax.broadcasted_iota(I32, (GDN_GROUP * cn, gw), 0) // cn)
             == (lax.broadcasted_iota(I32, (GDN_GROUP * cn, gw), 1) // HEAD_DIM))
    e_row = lax.broadcasted_iota(I32, (2 * GDN_HEADS, gw), 0)
    e_col = lax.broadcasted_iota(I32, (2 * GDN_HEADS, gw), 1) // HEAD_DIM
    chains = [(b, gi) for b in range(bb) for gi in range(GDN_HEADS // GDN_GROUP)]
    lanes = lambda gi: slice(gi * gw, (gi + 1) * gw)
    states = [sbd_ref[b, gi] for b, gi in chains]
    states_l = [s.astype(low) for s in states]
    v_new = [u_ref[b, :, lanes(gi)] - _dot(w_ref[b, :, lanes(gi)], s_l) for (b, gi), s_l in zip(chains, states_l)]
    q_s = [_dot(qd_ref[b, :, lanes(gi)], s_l) for (b, gi), s_l in zip(chains, states_l)]
    v_l = [v.astype(low) for v in v_new]
    outs = []
    for (b, gi), s, v, qs in zip(chains, states, v_l, q_s):
        vbd = jnp.where(vmask, jnp.concatenate([v] * GDN_GROUP, axis=0), jnp.zeros((), low))
        outs.append(qs + _dot(qk_ref[b, :, gi * GDN_GROUP * cn:(gi + 1) * GDN_GROUP * cn], vbd))
        upd = _dot_tn(kd_ref[b, :, lanes(gi)], v)
        expand = (e_row == e_col + gi * GDN_GROUP).astype(BF16)
        gte = sum(_dot(p.astype(BF16), expand) for p in _split3(gt_ref[b, 0:8, :]))[0:1]
        sbd_ref[b, gi] = s * gte + jnp.where(bdmask, upd, 0.0)
    for (b, gi), o in zip(chains, outs):
        o2 = o * o
        hi = o2.astype(BF16)
        lo = (o2 - hi.astype(F32)).astype(BF16)
        ms = (_dot(hi, ones_bd) + _dot(lo, ones_bd)) * (1.0 / HEAD_DIM)
        zg = z_ref[b, :, lanes(gi)]
        o_ref[b, :, lanes(gi)] = o * lax.rsqrt(ms + EPS) * nw_ref[...] * (zg * jax.nn.sigmoid(zg))

    @pl.when(ni == n_chunks - 1)
    def _():
        for b in range(bb):
            for h in range(GDN_HEADS):
                gi, hh = divmod(h, GDN_GROUP)
                ds = slice(hh * HEAD_DIM, (hh + 1) * HEAD_DIM)
                sfin_ref[b, h] = sbd_ref[b, gi, ds, ds]


def _gdn(c3, hist, z3, ab3, s0, conv_w, alog16, dtb16, nw_group, *, chunk, seq_block):
    nseq, t, _ = c3.shape
    n = nseq * t
    sb, r = (1, GDN_TILE) if t >= GDN_TILE else (GDN_TILE // t, t)
    tiles = t // r
    low = BF16 if chunk >= 16 else F32
    blk = lambda w: pl.BlockSpec((sb, r, w), lambda s, i: (s, i, 0))
    full = lambda a: pl.BlockSpec(a.shape, lambda s, i: (0,) * a.ndim)
    flat = lambda w: pl.BlockSpec((GDN_TILE, w), lambda s, i: (s * tiles + i, 0))
    widths = (GDN_V, GDN_V, GDN_QK, GDN_QK, GDN_HEADS * chunk, 2 * GDN_HEADS)
    dtypes = (F32, low, low, low, low, F32)
    prep_out = dict(out_specs=[flat(wd) for wd in widths],
                    out_shape=[jax.ShapeDtypeStruct((n, wd), dt) for wd, dt in zip(widths, dtypes)],
                    compiler_params=pltpu.CompilerParams(dimension_semantics=("arbitrary", "arbitrary")))
    lane_dense = chunk == HEAD_DIM and sb == 1
    u, w, qd, kd, qk, gt = pl.pallas_call(
        _gdn_prep_pair_kernel if lane_dense else functools.partial(_gdn_prep_kernel, chunk=chunk),
        grid=(nseq // sb, tiles),
        in_specs=[blk(GDN_CONV_CH), pl.BlockSpec((sb, CONV_W - 1, GDN_CONV_CH), lambda s, i: (s, 0, 0)),
                  blk(2 * GDN_HEADS), full(conv_w), full(alog16), full(dtb16)],
        scratch_shapes=[pltpu.VMEM((sb, 8 + r, GDN_CONV_CH), F32)],
        name="gdn_prep_pair" if lane_dense else "gdn_prep", **prep_out)(c3, hist, ab3, conv_w, alog16, dtb16)

    n_chunks = t // chunk
    tok = lambda wd: pl.BlockSpec((seq_block, chunk, wd), lambda s, c: (s, c, 0))
    per_seq = pl.BlockSpec((seq_block,) + s0.shape[1:], lambda s, c: (s, 0, 0, 0))
    seq3 = lambda a: a.reshape(nseq, t, a.shape[-1])
    return pl.pallas_call(
        functools.partial(_gdn_scan_kernel, chunk=chunk, n_chunks=n_chunks),
        grid=(nseq // seq_block, n_chunks),
        in_specs=[tok(wd) for wd in widths] + [tok(GDN_V), per_seq,
                                               pl.BlockSpec(nw_group.shape, lambda s, c: (0, 0))],
        out_specs=[tok(GDN_V), per_seq],
        out_shape=[jax.ShapeDtypeStruct((nseq, t, GDN_V), F32), jax.ShapeDtypeStruct(s0.shape, F32)],
        scratch_shapes=[pltpu.VMEM((seq_block, GDN_HEADS // GDN_GROUP, GDN_GROUP_W, GDN_GROUP_W), F32)],
        compiler_params=pltpu.CompilerParams(dimension_semantics=("arbitrary", "arbitrary")),
        name="gdn_scan",
    )(seq3(u), seq3(w), seq3(qd), seq3(kd), seq3(qk), seq3(gt), z3, s0, nw_group)


def _outproj_kernel(x_ref, osw_ref, ogd_ref, wo_ref, gf_ref, wr_ref, br_ref,
                    h_ref, xn_ref, gates_ref, meta_ref, cnt_ref, run_ref):
    i = pl.program_id(0)
    tm = x_ref.shape[0]
    rows = wr_ref.shape[0]

    @pl.when(i == 0)
    def _():
        run_ref[...] = jnp.zeros_like(run_ref)

    h = (x_ref[...] + _dot(osw_ref[...].astype(BF16), wo_ref[0:SWA_Q, :])
         + _dot(ogd_ref[...].astype(BF16), wo_ref[SWA_Q:D_MIX, :]))
    h_ref[...] = h
    xn = _rms(h, gf_ref[...])
    xn_ref[...] = xn
    logits = _dot_nt(wr_ref[...], xn.astype(BF16))

    row = lax.broadcasted_iota(I32, (rows, tm), 0)
    bias = br_ref[...]
    top = lambda v: jnp.max(v, axis=0, keepdims=True)
    tot = lambda v: jnp.sum(v, axis=0, keepdims=True)
    first_at = lambda v: jnp.min(jnp.where(v == top(v), row, 2 * rows), axis=0, keepdims=True)
    is_g = (row >= N_EXPERTS) & (row < N_EXPERTS + N_GROUPS)
    lg = jnp.where(is_g, logits, NEG_BIG)
    pg = jnp.where(is_g, jnp.exp(lg - top(lg)), 0.0)
    group_p = pg / tot(pg)
    g_row = first_at(jnp.where(is_g, group_p + bias, NEG_BIG))
    g_w = tot(jnp.where(row == g_row, group_p, 0.0))
    sel = (row < N_EXPERTS) & ((row // EXPERTS_PER_GROUP) == (g_row - N_EXPERTS))
    le = jnp.where(sel, logits, NEG_BIG)
    pe = jnp.where(sel, jnp.exp(le - top(le)), 0.0)
    e_p = pe / tot(pe)
    score = jnp.where(sel, e_p + bias, NEG_BIG)
    i1 = first_at(score)
    i2 = first_at(jnp.where(row == i1, NEG_BIG, score))
    oh1 = row == i1
    oh2 = row == i2
    w1 = tot(jnp.where(oh1, e_p, 0.0))
    w2 = tot(jnp.where(oh2, e_p, 0.0))
    wsum = w1 + w2

    ohs = (oh1 | oh2).astype(BF16)
    earlier = (lax.broadcasted_iota(I32, (tm, tm), 0) < lax.broadcasted_iota(I32, (tm, tm), 1)).astype(BF16)
    before = _dot(ohs, earlier) + run_ref[...]
    r1 = tot(jnp.where(oh1, before, 0.0))
    r2 = tot(jnp.where(oh2, before, 0.0))
    run_ref[...] = run_ref[...] + jnp.sum(ohs.astype(F32), axis=1, keepdims=True)
    cnt_ref[...] = run_ref[...]

    zero = jnp.zeros_like(w1)
    meta = jnp.concatenate([i1.astype(F32), i2.astype(F32), r1, r2, g_w * (w1 / wsum), g_w * (w2 / wsum),
                            zero, zero], axis=0)
    meta_ref[...] = meta
    eye8 = (lax.broadcasted_iota(I32, (8, LANES), 0) == lax.broadcasted_iota(I32, (8, LANES), 1)).astype(F32)
    gates_ref[...] = sum(_dot_tn(part, eye8) for part in _split3(meta))[:, 4:6]


def _outproj(x2d, o_swa, o_gdn, wo_bf16, gain, w_router, b_router):
    n = x2d.shape[0]
    tm = ROW_TILE
    row = lambda w: pl.BlockSpec((tm, w), lambda i: (i, 0))
    full = lambda a: pl.BlockSpec(a.shape, lambda i: (0,) * a.ndim)
    return pl.pallas_call(
        _outproj_kernel,
        grid=(n // tm,),
        in_specs=[row(D_MODEL), row(SWA_Q), row(GDN_V), full(wo_bf16), full(gain), full(w_router), full(b_router)],
        out_specs=[row(D_MODEL), row(D_MODEL), row(2), pl.BlockSpec((8, tm), lambda i: (0, i)),
                   pl.BlockSpec((LANES, 1), lambda i: (0, 0))],
        out_shape=[jax.ShapeDtypeStruct((n, D_MODEL), F32), jax.ShapeDtypeStruct((n, D_MODEL), F32),
                   jax.ShapeDtypeStruct((n, 2), F32), jax.ShapeDtypeStruct((8, n), F32),
                   jax.ShapeDtypeStruct((LANES, 1), F32)],
        scratch_shapes=[pltpu.VMEM((LANES, 1), F32)],
        compiler_params=pltpu.CompilerParams(dimension_semantics=("arbitrary",)),
        name="outproj_router",
    )(x2d, o_swa, o_gdn, wo_bf16, gain, w_router, b_router)


def _row_copy(src_ref, src_row, dst_ref, dst_row, sem):
    return pltpu.make_async_copy(src_ref.at[pl.ds(src_row, 1)], dst_ref.at[pl.ds(dst_row, 1)], sem)


def _scatter_kernel(dest0_ref, dest1_ref, xp_ref, xs_ref, out_ref, aidx_ref, sem, *, tiles_p, n_tokens):
    i = pl.program_id(0)
    tm = xp_ref.shape[0]

    def run(src_ref):
        def issue(g, carry):
            for u in range(DMA_UNROLL):
                r = g * DMA_UNROLL + u
                d0, d1 = dest0_ref[r], dest1_ref[r]
                _row_copy(src_ref, r, out_ref, d0, sem).start(priority=0)
                _row_copy(src_ref, r, out_ref, d1, sem).start(priority=1)
                aidx_ref[d0] = i * tm + r
                aidx_ref[d1] = n_tokens + i * tm + r
            return carry

        lax.fori_loop(0, tm // DMA_UNROLL, issue, 0)
        for _ in range(2):
            pltpu.make_async_copy(src_ref, out_ref.at[pl.ds(0, tm)], sem).wait()

    @pl.when(i < tiles_p)
    def _():
        run(xp_ref)

    @pl.when(i >= tiles_p)
    def _():
        run(xs_ref)


def _scatter(dest, xn_p, xn_s):
    tm = ROW_TILE
    tiles_p, tiles_s = xn_p.shape[0] // tm, xn_s.shape[0] // tm
    rows = 2 * (xn_p.shape[0] + xn_s.shape[0])
    idx = pl.BlockSpec((tm,), lambda i: (i,), memory_space=pltpu.SMEM)
    return pl.pallas_call(
        functools.partial(_scatter_kernel, tiles_p=tiles_p, n_tokens=rows // 2),
        grid=(tiles_p + tiles_s,),
        in_specs=[idx, idx,
                  pl.BlockSpec((tm, D_MODEL), lambda i: (jnp.minimum(i, tiles_p - 1), 0)),
                  pl.BlockSpec((tm, D_MODEL), lambda i: (jnp.maximum(i - tiles_p, 0), 0))],
        out_specs=[pl.BlockSpec(memory_space=pl.ANY), pl.BlockSpec(memory_space=pltpu.SMEM)],
        out_shape=[jax.ShapeDtypeStruct((rows, D_MODEL), F32), jax.ShapeDtypeStruct((rows,), I32)],
        scratch_shapes=[pltpu.SemaphoreType.DMA(())],
        compiler_params=pltpu.CompilerParams(dimension_semantics=("arbitrary",)),
        name="scatter_rows",
    )(dest[0], dest[1], xn_p, xn_s)


def _experts_kernel(blk_ref, exp_ref, lo_ref, hi_ref, first_ref, fresh_ref, pblk_ref,
                    aprev_ref, x_ref, wg_ref, wu_ref, wd_ref, ya_ref, wg_l, wu_l, wd_l, ybuf, sem,
                    *, n_items, n_blocks):
    j = pl.program_id(0)
    lo, hi = lo_ref[j], hi_ref[j]
    slot = lax.rem(blk_ref[j], 2)
    quarter = MOE_BLOCK // 4

    def wait_rows(s):
        pltpu.make_async_copy(ybuf.at[s], ya_ref.at[pl.ds(0, MOE_BLOCK)], sem.at[s]).wait()

    def send_rows(s, group):
        for row in range(group * quarter, (group + 1) * quarter):
            _row_copy(ybuf.at[s], row, ya_ref, aprev_ref[0, 0, row], sem.at[s]).start(priority=row % 2)

    @pl.when(fresh_ref[j] == 1)
    def _():
        wg_l[...] = wg_ref[0].astype(BF16)
        wu_l[...] = wu_ref[0].astype(BF16)
        wd_l[...] = wd_ref[0].astype(BF16)

    @pl.when((first_ref[j] == 1) & (blk_ref[j] >= 2))
    def _():
        wait_rows(slot)

    def item(is_first, send_prev):
        send = (lambda group: send_rows(1 - slot, group)) if send_prev else (lambda group: None)
        send(0)
        x = x_ref[...].astype(BF16)
        gate = _dot(x, wg_l[...])
        send(1)
        up = _dot(x, wu_l[...])
        hid = (gate * jax.nn.sigmoid(gate)) * up
        send(2)
        y = _dot(hid.astype(BF16), wd_l[...])
        send(3)
        r = lax.broadcasted_iota(I32, (MOE_BLOCK, 1), 0)
        mine = (r >= lo) & (r < hi)
        ybuf[slot] = jnp.where(mine, y, 0.0 if is_first else ybuf[slot])

    live = hi > lo
    pl.when(live & (first_ref[j] == 1) & (blk_ref[j] >= 1))(functools.partial(item, True, True))
    pl.when(live & (first_ref[j] == 1) & (blk_ref[j] == 0))(functools.partial(item, True, False))
    pl.when(live & (first_ref[j] == 0))(functools.partial(item, False, False))

    @pl.when(j == n_items - 1)
    def _():
        last_slot = (n_blocks - 1) % 2
        for group in range(4):
            send_rows(last_slot, group)
        wait_rows(1 - last_slot)
        wait_rows(last_slot)


def _experts(items, aidx, xs, w_gate, w_up, w_down):
    n_items = items[0].shape[0]
    n_blocks = xs.shape[0] // MOE_BLOCK
    assert n_items > n_blocks + N_EXPERTS - 1 and n_blocks >= 2
    xblk = pl.BlockSpec((MOE_BLOCK, D_MODEL), lambda j, blk, *_: (blk[j], 0))
    wspec = lambda a: pl.BlockSpec((1,) + a.shape[1:], lambda j, blk, ex, *_: (ex[j], 0, 0))
    return pl.pallas_call(
        functools.partial(_experts_kernel, n_items=n_items, n_blocks=n_blocks),
        grid_spec=pltpu.PrefetchScalarGridSpec(
            num_scalar_prefetch=len(items),
            grid=(n_items,),
            in_specs=[pl.BlockSpec((1, 1, MOE_BLOCK), lambda j, *pre: (pre[-1][j], 0, 0), memory_space=pltpu.SMEM),
                      xblk, wspec(w_gate), wspec(w_up), wspec(w_down)],
            out_specs=pl.BlockSpec(memory_space=pl.ANY),
            scratch_shapes=[pltpu.VMEM(w_gate.shape[1:], BF16), pltpu.VMEM(w_up.shape[1:], BF16),
                            pltpu.VMEM(w_down.shape[1:], BF16), pltpu.VMEM((2, MOE_BLOCK, D_MODEL), F32),
                            pltpu.SemaphoreType.DMA((2,))]),
        out_shape=jax.ShapeDtypeStruct(xs.shape, F32),
        compiler_params=pltpu.CompilerParams(dimension_semantics=("arbitrary",)),
        name="experts",
    )(*items, aidx.reshape(n_blocks, 1, MOE_BLOCK), xs, w_gate, w_up, w_down)


def _work_items(counts, total_rows):
    n_blocks = total_rows // MOE_BLOCK
    n_items = n_blocks + N_EXPERTS
    end = jnp.cumsum(counts)
    start = end - counts
    first_blk = start // MOE_BLOCK
    nb = jnp.where(counts > 0, (end - 1) // MOE_BLOCK - first_blk + 1, 0)
    item_end = jnp.cumsum(nb)
    used = item_end[-1]
    j = jnp.arange(n_items, dtype=I32)
    jj = jnp.minimum(j, used - 1)
    e = jnp.minimum(jnp.sum((item_end[None, :] <= jj[:, None]).astype(I32), axis=1), N_EXPERTS - 1)
    onehot = (e[:, None] == jnp.arange(N_EXPERTS, dtype=I32)[None, :]).astype(I32)
    pick = lambda a: jnp.sum(onehot * a[None, :], axis=1)
    blk = pick(first_blk) + (jj - (pick(item_end) - pick(nb)))
    lo = jnp.maximum(pick(start), blk * MOE_BLOCK) - blk * MOE_BLOCK
    hi = jnp.minimum(pick(end), (blk + 1) * MOE_BLOCK) - blk * MOE_BLOCK
    live = j < used
    lo = jnp.where(live, lo, 0)
    hi = jnp.where(live, hi, 0)
    prev_blk = jnp.concatenate([jnp.full((1,), -1, I32), blk[:-1]])
    first = (live & (blk != prev_blk)).astype(I32)
    prev_e = jnp.concatenate([jnp.full((1,), -1, I32), e[:-1]])
    fresh = (live & (e != prev_e)).astype(I32)
    pblk = jnp.where(j == n_items - 1, n_blocks - 1, jnp.maximum(blk - 1, 0))
    return (blk.astype(I32), e.astype(I32), lo.astype(I32), hi.astype(I32), first, fresh, pblk.astype(I32)), start


def _combine_kernel(gates_ref, h_ref, gf_ref, y0_ref, y1_ref, out_ref):
    gates = gates_ref[...]
    y = h_ref[...] + (y0_ref[...] * gates[:, 0:1] + y1_ref[...] * gates[:, 1:2])
    out_ref[...] = _rms(y, gf_ref[...])


def _combine(gates, h, gain, ya, first_token):
    n = h.shape[0]
    tm = COMBINE_TILE
    tile0 = first_token // tm
    slot_tiles = ya.shape[0] // 2 // tm
    row = lambda w: pl.BlockSpec((tm, w), lambda i: (i, 0))
    slot = lambda k: pl.BlockSpec((tm, D_MODEL), lambda i: (k * slot_tiles + tile0 + i, 0))
    return pl.pallas_call(
        _combine_kernel,
        grid=(n // tm,),
        in_specs=[row(2), row(D_MODEL), pl.BlockSpec(gain.shape, lambda i: (0, 0)), slot(0), slot(1)],
        out_specs=row(D_MODEL),
        out_shape=jax.ShapeDtypeStruct((n, D_MODEL), F32),
        compiler_params=pltpu.CompilerParams(dimension_semantics=("arbitrary",)),
        name="combine_norm",
    )(gates, h, gain, ya, ya)


def kernel(x_prompt, x_sample, cache_swa_k, cache_swa_v, state_gdn_conv, state_gdn, norm_mix, w_in, swa_sinks,
           gdn_conv_w, gdn_A_log, gdn_dt_bias, gdn_norm_w, w_out, norm_ffn, w_router_group, b_router_group,
           w_router_expert, b_router_expert, w_exp_gate, w_exp_up, w_exp_down, norm_final):
    depth = w_in.shape[0]
    assert depth == 1, "single trunk layer"
    bp, sp, _ = x_prompt.shape
    bs, ts, _ = x_sample.shape
    np_, ns = bp * sp, bs * ts
    l = 0

    w_in_b = w_in[l].astype(BF16)
    w_out_b = w_out[l].astype(BF16)
    g_mix = norm_mix[l].reshape(1, D_MODEL)
    g_ffn = norm_ffn[l].reshape(1, D_MODEL)
    g_fin = norm_final.reshape(1, D_MODEL)
    pad = LANES - N_EXPERTS - N_GROUPS
    w_router = jnp.concatenate([w_router_expert[l], w_router_group[l], jnp.zeros((D_MODEL, pad), F32)],
                               axis=1).astype(BF16).T
    b_router = jnp.concatenate([b_router_expert[l].reshape(-1), b_router_group[l],
                                jnp.zeros((pad,), F32)])[:, None]
    zeros8 = jnp.zeros((GDN_HEADS,), F32)
    alog16 = jnp.concatenate([gdn_A_log[l], zeros8])[None]
    dtb16 = jnp.concatenate([gdn_dt_bias[l], zeros8])[None]
    nw_group = jnp.tile(gdn_norm_w[l].reshape(1, HEAD_DIM), (1, GDN_GROUP))
    sinks = swa_sinks[l]

    q_p, k_p, v_p, c_p, z_p, ab_p = _inproj(x_prompt.reshape(np_, D_MODEL), g_mix, w_in_b)
    nblk = np_ // WINDOW
    k_p3 = k_p.reshape(nblk, WINDOW, SWA_KV)
    v_p3 = v_p.reshape(nblk, WINDOW, SWA_KV)
    (o_swa_p,) = _swa(sinks, q_p.reshape(nblk, WINDOW, SWA_Q), k_p3, v_p3, k_p3, v_p3,
                      bb=SWA_BLOCKS, blocks_per_seq=sp // WINDOW, emit_cache=False)
    c_p3 = c_p.reshape(bp, sp, GDN_CONV_CH)
    o_gdn_p, s_fin_p = _gdn(c_p3, jnp.zeros((bp, CONV_W - 1, GDN_CONV_CH), F32), z_p.reshape(bp, sp, GDN_V),
                            ab_p.reshape(bp, sp, 2 * GDN_HEADS),
                            jnp.zeros((bp, GDN_HEADS, HEAD_DIM, HEAD_DIM), F32),
                            gdn_conv_w[l], alog16, dtb16, nw_group, chunk=GDN_CHUNK, seq_block=bp)
    h_p, xn_p, gates_p, meta_p, cnt_p = _outproj(
        x_prompt.reshape(np_, D_MODEL), o_swa_p.reshape(np_, SWA_Q), o_gdn_p.reshape(np_, GDN_V),
        w_out_b, g_ffn, w_router, b_router)

    q_s, k_s, v_s, c_s, z_s, ab_s = _inproj(x_sample.reshape(ns, D_MODEL), g_mix, w_in_b)
    o_swa_s, kcache_s, vcache_s = _swa(
        sinks, q_s.reshape(bs, ts, SWA_Q), k_s.reshape(bs, ts, SWA_KV), v_s.reshape(bs, ts, SWA_KV),
        cache_swa_k[l].reshape(bs, WINDOW, SWA_KV), cache_swa_v[l].reshape(bs, WINDOW, SWA_KV),
        bb=8, blocks_per_seq=None, emit_cache=True)
    c_s3 = c_s.reshape(bs, ts, GDN_CONV_CH)
    o_gdn_s, s_fin_s = _gdn(c_s3, state_gdn_conv[l], z_s.reshape(bs, ts, GDN_V),
                            ab_s.reshape(bs, ts, 2 * GDN_HEADS), state_gdn[l],
                            gdn_conv_w[l], alog16, dtb16, nw_group, chunk=ts, seq_block=8)
    h_s, xn_s, gates_s, meta_s, cnt_s = _outproj(
        x_sample.reshape(ns, D_MODEL), o_swa_s.reshape(ns, SWA_Q), o_gdn_s.reshape(ns, GDN_V),
        w_out_b, g_ffn, w_router, b_router)

    cnt_p_i = cnt_p[:N_EXPERTS, 0].astype(I32)
    cnt_s_i = cnt_s[:N_EXPERTS, 0].astype(I32)
    items, start = _work_items(cnt_p_i + cnt_s_i, 2 * (np_ + ns))
    expert_ids = jnp.arange(N_EXPERTS, dtype=I32)
    lookup = lambda table, ids: jnp.sum(jnp.where(ids[..., None] == expert_ids, table, 0), axis=-1)
    dest_p = lookup(start, meta_p[0:2].astype(I32)) + meta_p[2:4].astype(I32)
    dest_s = lookup(start + cnt_p_i, meta_s[0:2].astype(I32)) + meta_s[2:4].astype(I32)
    xs, aidx = _scatter(jnp.concatenate([dest_p, dest_s], axis=1), xn_p, xn_s)
    ya = _experts(items, aidx, xs, w_exp_gate[l], w_exp_up[l], w_exp_down[l])
    y_p = _combine(gates_p, h_p, g_fin, ya, 0)
    y_s = _combine(gates_s, h_s, g_fin, ya, np_)

    kv5 = lambda a, b: a.reshape(b, -1, SWA_KV_HEADS, HEAD_DIM)[None]
    return (y_p.reshape(bp, sp, D_MODEL), y_s.reshape(bs, ts, D_MODEL),
            kv5(k_p.reshape(bp, sp, SWA_KV)[:, -WINDOW:], bp), kv5(v_p.reshape(bp, sp, SWA_KV)[:, -WINDOW:], bp),
            kv5(kcache_s, bs), kv5(vcache_s, bs),
            c_p3[:, -(CONV_W - 1):][None], c_s3[:, -(CONV_W - 1):][None],
            s_fin_p[None], s_fin_s[None])
```

```python
import functools

import jax
import jax.numpy as jnp
from jax import lax
from jax.experimental import pallas as pl
from jax.experimental.pallas import tpu as pltpu

F32 = jnp.float32
BF16 = jnp.bfloat16
I32 = jnp.int32

D_MODEL = 1024
HEAD_DIM = 64
SWA_HEADS = 8
GDN_HEADS = 8
SWA_KV_HEADS = 2
GQA_GROUP = SWA_HEADS // SWA_KV_HEADS
WINDOW = 128
ATTN_SCALE = HEAD_DIM ** -0.5
CONV_W = 4
N_GROUPS = 8
EXPERTS_PER_GROUP = 8
N_EXPERTS = 64
D_EXPERT = 256
EPS = 1e-6

SWA_Q = SWA_HEADS * HEAD_DIM
SWA_KV = SWA_KV_HEADS * HEAD_DIM
GDN_QK = GDN_HEADS * HEAD_DIM
GDN_V = GDN_HEADS * HEAD_DIM
GDN_CONV_CH = 2 * GDN_QK + GDN_V
D_MIX = SWA_Q + GDN_V
D_IN = SWA_Q + 2 * SWA_KV + GDN_CONV_CH + GDN_V + 2 * GDN_HEADS
COL_K = SWA_Q
COL_V = COL_K + SWA_KV
COL_C = COL_V + SWA_KV
COL_Z = COL_C + GDN_CONV_CH
COL_AB = COL_Z + GDN_V

LANES = 128
NEG_BIG = -1e30
ROW_TILE = 512
MOE_BLOCK = 256
COMBINE_TILE = 512
DMA_UNROLL = 8
SWA_BLOCKS = 4
GDN_CHUNK = 64
GDN_TILE = 256
GDN_GROUP = 4
GDN_GROUP_W = GDN_GROUP * HEAD_DIM


def _rms(x, g):
    return x * lax.rsqrt(jnp.mean(x * x, axis=-1, keepdims=True) + EPS) * g


def _dot(a, b):
    return jnp.dot(a, b, preferred_element_type=F32)


def _dot_nt(a, b):
    return lax.dot_general(a, b, (((1,), (1,)), ((), ())), preferred_element_type=F32)


def _dot_tn(a, b):
    return lax.dot_general(a, b, (((0,), (0,)), ((), ())), preferred_element_type=F32)


def _split3(x):
    p1 = x.astype(BF16).astype(F32)
    r = x - p1
    p2 = r.astype(BF16).astype(F32)
    p3 = (r - p2).astype(BF16).astype(F32)
    return p1, p2, p3


def _inproj_kernel(x_ref, g_ref, w_ref, q_ref, k_ref, v_ref, c_ref, z_ref, ab_ref):
    x = x_ref[...]
    xb = _rms(x, g_ref[...]).astype(BF16)
    q_ref[...] = _dot(xb, w_ref[:, 0:COL_K])
    k_ref[...] = _dot(xb, w_ref[:, COL_K:COL_V])
    v_ref[...] = _dot(xb, w_ref[:, COL_V:COL_C])
    c_ref[...] = _dot(xb, w_ref[:, COL_C:COL_Z])
    z_ref[...] = _dot(xb, w_ref[:, COL_Z:COL_AB])
    ab_ref[...] = _dot(xb, w_ref[:, COL_AB:D_IN])


def _inproj(x2d, gain, w_bf16):
    n = x2d.shape[0]
    tm = ROW_TILE
    row = lambda w: pl.BlockSpec((tm, w), lambda i: (i, 0))
    full = lambda a: pl.BlockSpec(a.shape, lambda i: (0,) * a.ndim)
    widths = (SWA_Q, SWA_KV, SWA_KV, GDN_CONV_CH, GDN_V, 2 * GDN_HEADS)
    return pl.pallas_call(
        _inproj_kernel,
        grid=(n // tm,),
        in_specs=[row(D_MODEL), full(gain), full(w_bf16)],
        out_specs=[row(w) for w in widths],
        out_shape=[jax.ShapeDtypeStruct((n, w), F32) for w in widths],
        compiler_params=pltpu.CompilerParams(dimension_semantics=("arbitrary",)),
        name="inproj",
    )(x2d, gain, w_bf16)


def _swa_kernel(sink_ref, q_ref, kc_ref, vc_ref, kp_ref, vp_ref, o_ref, *cache_refs,
                bb, t, blocks_per_seq, emit_cache):
    rows = GQA_GROUP * t
    ri = lax.broadcasted_iota(I32, (rows, 1), 0)
    qi = lax.rem(ri, t)
    gi = ri // t
    if emit_cache:
        kj = lax.broadcasted_iota(I32, (rows, WINDOW), 1)
        mask = kj <= qi + (WINDOW - t)
        mask_first = mask
    else:
        kj = lax.broadcasted_iota(I32, (rows, WINDOW + t), 1)
        mask = (kj <= qi + WINDOW) & (kj > qi)
        has_prev = lax.rem(pl.program_id(0) * bb, blocks_per_seq) != 0
        mask_first = mask & (has_prev | (kj >= WINDOW))
    sinks = []
    for h in range(SWA_KV_HEADS):
        sink = jnp.zeros((rows, 1), F32)
        for g in range(GQA_GROUP):
            sink = jnp.where(gi == g, sink_ref[GQA_GROUP * h + g], sink)
        sinks.append(sink)
    chains = [(b, h) for b in range(bb) for h in range(SWA_KV_HEADS)]
    scores, values = [], []
    for b, h in chains:
        hs = slice(h * HEAD_DIM, (h + 1) * HEAD_DIM)
        kc, vc = kc_ref[b, :, hs], vc_ref[b, :, hs]
        if emit_cache:
            kp, vp = kp_ref[b, :, hs], vp_ref[b, :, hs]
            keys = jnp.concatenate([kp[t:], kc], axis=0)
            vals = jnp.concatenate([vp[t:], vc], axis=0)
            cache_refs[0][b, :, hs] = keys
            cache_refs[1][b, :, hs] = vals
        else:
            kp, vp = (kp_ref[0, :, hs], vp_ref[0, :, hs]) if b == 0 else (kc_ref[b - 1, :, hs], vc_ref[b - 1, :, hs])
            keys = jnp.concatenate([kp, kc], axis=0)
            vals = jnp.concatenate([vp, vc], axis=0)
        q4 = jnp.concatenate(
            [q_ref[b, :, (GQA_GROUP * h + g) * HEAD_DIM:(GQA_GROUP * h + g + 1) * HEAD_DIM]
             for g in range(GQA_GROUP)], axis=0)
        scores.append(_dot_nt(q4.astype(BF16), keys.astype(BF16)))
        values.append(vals.astype(BF16))
    probs, dens = [], []
    for (b, h), s in zip(chains, scores):
        s = jnp.where(mask_first if b == 0 else mask, s * ATTN_SCALE, NEG_BIG)
        m = jnp.maximum(jnp.max(s, axis=-1, keepdims=True), sinks[h])
        p = jnp.exp(s - m)
        dens.append(jnp.sum(p, axis=-1, keepdims=True) + jnp.exp(sinks[h] - m))
        probs.append(p.astype(BF16))
    outs = [_dot(p, v) / den for p, v, den in zip(probs, values, dens)]
    for b in range(bb):
        o_ref[b] = jnp.concatenate([outs[b * SWA_KV_HEADS + h][g * t:(g + 1) * t]
                                    for h in range(SWA_KV_HEADS) for g in range(GQA_GROUP)], axis=-1)


def _swa(sinks, q3, k3, v3, kprev3, vprev3, *, bb, blocks_per_seq, emit_cache):
    nb, t, _ = q3.shape
    cur = lambda w: pl.BlockSpec((bb, t, w), lambda i: (i, 0, 0))
    if emit_cache:
        prev = pl.BlockSpec((bb, WINDOW, SWA_KV), lambda i: (i, 0, 0))
    else:
        assert blocks_per_seq % bb == 0
        prev = pl.BlockSpec((1, WINDOW, SWA_KV), lambda i: (jnp.maximum(i * bb - 1, 0), 0, 0))
    out_specs = [cur(SWA_Q)]
    out_shape = [jax.ShapeDtypeStruct((nb, t, SWA_Q), F32)]
    if emit_cache:
        cache = pl.BlockSpec((bb, WINDOW, SWA_KV), lambda i: (i, 0, 0))
        out_specs += [cache, cache]
        out_shape += [jax.ShapeDtypeStruct((nb, WINDOW, SWA_KV), F32)] * 2
    return pl.pallas_call(
        functools.partial(_swa_kernel, bb=bb, t=t, blocks_per_seq=blocks_per_seq, emit_cache=emit_cache),
        grid=(nb // bb,),
        in_specs=[pl.BlockSpec(memory_space=pltpu.SMEM), cur(SWA_Q), cur(SWA_KV), cur(SWA_KV), prev, prev],
        out_specs=out_specs,
        out_shape=out_shape,
        compiler_params=pltpu.CompilerParams(dimension_semantics=("arbitrary",)),
        name="swa_cache" if emit_cache else "swa_band",
    )(sinks, q3, k3, v3, kprev3, vprev3)


def _gdn_prep_kernel(c_ref, hist_ref, ab_ref, cw_ref, alog_ref, dtb_ref,
                     u_ref, w_ref, qd_ref, kd_ref, qk_ref, gt_ref, cbuf_ref, *, chunk):
    sb, r, _ = c_ref.shape
    tp = sb * r
    cn = chunk
    low = w_ref.dtype
    hist_rows = CONV_W - 1

    @pl.when(pl.program_id(1) == 0)
    def _():
        cbuf_ref[:, 8 - hist_rows:8, :] = hist_ref[...]

    cbuf_ref[:, 8:8 + r, :] = c_ref[...]
    conv = cbuf_ref[:, 8 - hist_rows:8 - hist_rows + r, :] * cw_ref[0:1, :]
    for i in range(1, CONV_W):
        conv = conv + cbuf_ref[:, 8 - hist_rows + i:8 - hist_rows + i + r, :] * cw_ref[i:i + 1, :]
    tail = cbuf_ref[:, 8 + r - hist_rows:8 + r, :]
    cbuf_ref[:, 8 - hist_rows:8, :] = tail
    conv = (conv * jax.nn.sigmoid(conv)).reshape(tp, GDN_CONV_CH)

    ab = ab_ref[...].reshape(tp, 2 * GDN_HEADS)
    is_g = lax.broadcasted_iota(I32, (1, 2 * GDN_HEADS), 1) < GDN_HEADS
    g = jnp.where(is_g, -jnp.exp(alog_ref[...]) * jax.nn.softplus(ab + dtb_ref[...]), 0.0)
    beta = jax.nn.sigmoid(ab)

    ri = lax.broadcasted_iota(I32, (tp, tp), 0)
    ci = lax.broadcasted_iota(I32, (tp, tp), 1)
    same = (ri // cn) == (ci // cn)
    causal = same & (ri >= ci)
    strict = same & (ri > ci)
    eye = (ri == ci).astype(F32)
    stack = jnp.concatenate([causal.astype(BF16), same.astype(BF16)], axis=0)
    both = sum(_dot(stack, p.astype(BF16)) for p in _split3(g))
    gc, gl = both[:tp], both[tp:]
    e16 = lax.broadcasted_iota(I32, (2 * GDN_HEADS, 2 * GDN_HEADS), 0)
    eye16 = (e16 == lax.broadcasted_iota(I32, (2 * GDN_HEADS, 2 * GDN_HEADS), 1)).astype(BF16)
    gc_row = sum(_dot_nt(eye16, p.astype(BF16)) for p in _split3(gc))
    gt_ref[...] = jnp.exp(gl)
    fold = (lax.broadcasted_iota(I32, (tp, cn), 0) % cn == lax.broadcasted_iota(I32, (tp, cn), 1)).astype(low)

    levels = cn.bit_length() - 2
    heads = range(GDN_HEADS)
    lanes_of = lambda i, h: slice(i * GDN_QK + h * HEAD_DIM, i * GDN_QK + (h + 1) * HEAD_DIM)
    qs = [conv[:, lanes_of(0, h)] for h in heads]
    ks = [conv[:, lanes_of(1, h)] for h in heads]
    qs = [q * lax.rsqrt(jnp.sum(q * q, axis=-1, keepdims=True) + EPS) * (HEAD_DIM ** -0.5) for q in qs]
    ks = [k * lax.rsqrt(jnp.sum(k * k, axis=-1, keepdims=True) + EPS) for k in ks]
    g_cs = [gc[:, h:h + 1] for h in heads]
    b_hs = [beta[:, GDN_HEADS + h:GDN_HEADS + h + 1] for h in heads]
    decays = [jnp.exp(jnp.where(causal, g_cs[h] - gc_row[h:h + 1, :], NEG_BIG)) for h in heads]
    egs = [jnp.exp(g_c) for g_c in g_cs]
    kbs = [k * b_h for k, b_h in zip(ks, b_hs)]
    k_ls = [k.astype(low) for k in ks]
    lmats = [jnp.where(strict, _dot_nt(kb.astype(low), k_l) * decay, 0.0) for kb, k_l, decay in zip(kbs, k_ls, decays)]
    qks = [_dot((_dot_nt(q.astype(low), k_l) * decay).astype(low), fold) for q, k_l, decay in zip(qs, k_ls, decays)]
    xs = [eye - lmat for lmat in lmats]
    ps = [_dot(lmat.astype(low), lmat.astype(low)) for lmat in lmats]
    for lev in range(1, levels + 1):
        p_ls = [p.astype(low) for p in ps]
        if lev < levels:
            xp = [_dot(jnp.concatenate([x.astype(low), p_l], axis=0), p_l) for x, p_l in zip(xs, p_ls)]
            xs = [x + m[:tp] for x, m in zip(xs, xp)]
            ps = [m[tp:] for m in xp]
        else:
            xs = [x + _dot(x.astype(low), p_l) for x, p_l in zip(xs, p_ls)]
    sols = [_dot(xs[h].astype(low),
                 jnp.concatenate([conv[:, lanes_of(2, h)] * b_hs[h], kbs[h] * egs[h]], axis=-1).astype(low))
            for h in heads]
    u_ref[...] = jnp.concatenate([sol[:, :HEAD_DIM] for sol in sols], axis=-1)
    w_ref[...] = jnp.concatenate([sol[:, HEAD_DIM:] for sol in sols], axis=-1).astype(low)
    qd_ref[...] = jnp.concatenate([q * eg for q, eg in zip(qs, egs)], axis=-1).astype(low)
    kd_ref[...] = jnp.concatenate([ks[h] * jnp.exp(gl[:, h:h + 1] - g_cs[h]) for h in heads], axis=-1).astype(low)
    qk_ref[...] = jnp.concatenate(qks, axis=-1).astype(low)


def _gdn_prep_pair_kernel(c_ref, hist_ref, ab_ref, cw_ref, alog_ref, dtb_ref,
                          u_ref, w_ref, qd_ref, kd_ref, qk_ref, gt_ref, cbuf_ref):
    _, r, _ = c_ref.shape
    tp = r
    cn = HEAD_DIM
    pair_w = 2 * HEAD_DIM
    hist_rows = CONV_W - 1

    @pl.when(pl.program_id(1) == 0)
    def _():
        cbuf_ref[:, 8 - hist_rows:8, :] = hist_ref[...]

    cbuf_ref[:, 8:8 + r, :] = c_ref[...]
    conv = cbuf_ref[:, 8 - hist_rows:8 - hist_rows + r, :] * cw_ref[0:1, :]
    for i in range(1, CONV_W):
        conv = conv + cbuf_ref[:, 8 - hist_rows + i:8 - hist_rows + i + r, :] * cw_ref[i:i + 1, :]
    tail = cbuf_ref[:, 8 + r - hist_rows:8 + r, :]
    cbuf_ref[:, 8 - hist_rows:8, :] = tail
    conv = (conv * jax.nn.sigmoid(conv)).reshape(tp, GDN_CONV_CH)

    ab = ab_ref[...].reshape(tp, 2 * GDN_HEADS)
    is_g = lax.broadcasted_iota(I32, (1, 2 * GDN_HEADS), 1) < GDN_HEADS
    g = jnp.where(is_g, -jnp.exp(alog_ref[...]) * jax.nn.softplus(ab + dtb_ref[...]), 0.0)
    beta = jax.nn.sigmoid(ab)

    ri = lax.broadcasted_iota(I32, (tp, tp), 0)
    ci = lax.broadcasted_iota(I32, (tp, tp), 1)
    same = (ri // cn) == (ci // cn)
    same_l = same.astype(BF16)
    stack = jnp.concatenate([(same & (ri >= ci)).astype(BF16), same_l], axis=0)
    both = sum(_dot(stack, p.astype(BF16)) for p in _split3(g))
    gc, gl = both[:tp], both[tp:]
    gt_ref[...] = jnp.exp(gl)

    lane = lax.broadcasted_iota(I32, (tp, pair_w), 1)
    c_in = lax.broadcasted_iota(I32, (tp, pair_w), 0) % cn
    j_in = lane % cn
    left = lane < cn
    causal = c_in >= j_in
    strict = c_in > j_in
    diag = c_in == j_in
    eye = diag.astype(F32)
    bdmask = ((lax.broadcasted_iota(I32, (pair_w, pair_w), 0) // cn)
              == (lax.broadcasted_iota(I32, (pair_w, pair_w), 1) // cn))
    ones_bd = bdmask.astype(BF16)

    def bd(m):
        return jnp.where(bdmask, jnp.concatenate([m, m], axis=0), 0.0).astype(BF16)

    def head_sum(x):
        hi = x.astype(BF16)
        lo = (x - hi.astype(F32)).astype(BF16)
        return _dot(hi, ones_bd) + _dot(lo, ones_bd)

    levels = cn.bit_length() - 2
    chunks = [slice(n * cn, (n + 1) * cn) for n in range(tp // cn)]
    n_pairs = GDN_HEADS // 2
    pairs = range(n_pairs)
    pick = lambda m, off, p: jnp.where(left, m[:, off + 2 * p:off + 2 * p + 1], m[:, off + 2 * p + 1:off + 2 * p + 2])
    third = lambda i, p: conv[:, i * GDN_QK + p * pair_w:i * GDN_QK + (p + 1) * pair_w]
    qs = [third(0, p) for p in pairs]
    ks = [third(1, p) for p in pairs]
    qs = [q * lax.rsqrt(head_sum(q * q) + EPS) * (HEAD_DIM ** -0.5) for q in qs]
    ks = [k * lax.rsqrt(head_sum(k * k) + EPS) for k in ks]
    gcps = [pick(gc, 0, p) for p in pairs]
    rowms = [sum(_dot(same_l, part.astype(BF16)) for part in _split3(jnp.where(diag, gcp, 0.0))) for gcp in gcps]
    decays = [jnp.exp(jnp.where(causal, gcp - rowm, NEG_BIG)) for gcp, rowm in zip(gcps, rowms)]
    lmats, vbs, kbegs = [], [], []
    for p in pairs:
        ls = slice(p * pair_w, (p + 1) * pair_w)
        q, k, gcp, decay = qs[p], ks[p], gcps[p], decays[p]
        bp = pick(beta, GDN_HEADS, p)
        eg = jnp.exp(gcp)
        kb = k * bp
        vbs.append(third(2, p) * bp)
        kbegs.append(kb * eg)
        qd_ref[:, ls] = (q * eg).astype(BF16)
        kd_ref[:, ls] = (k * jnp.exp(pick(gl, 0, p) - gcp)).astype(BF16)
        q_l, kb_l = q.astype(BF16), kb.astype(BF16)
        kbd = [bd(k[rs]) for rs in chunks]
        kk = jnp.concatenate([_dot_nt(kb_l[rs], kbd[n]) for n, rs in enumerate(chunks)], axis=0)
        qk = jnp.concatenate([_dot_nt(q_l[rs], kbd[n]) for n, rs in enumerate(chunks)], axis=0)
        qk_ref[:, ls] = (qk * decay).astype(BF16)
        lmats.append(jnp.where(strict, kk * decay, 0.0))
    bodies = [(p, rs) for p in range(n_pairs) for rs in chunks]
    xs = [eye[rs] - lmats[p][rs] for p, rs in bodies]
    ps = [_dot(lmats[p][rs].astype(BF16), bd(lmats[p][rs])) for p, rs in bodies]
    for lev in range(1, levels + 1):
        pbd = [bd(pm) for pm in ps]
        if lev < levels:
            xp = [_dot(jnp.concatenate([xm, pm], axis=0).astype(BF16), wm) for xm, pm, wm in zip(xs, ps, pbd)]
            xs = [xm + m[:cn] for xm, m in zip(xs, xp)]
            ps = [m[cn:] for m in xp]
        else:
            xs = [xm + _dot(xm.astype(BF16), wm) for xm, wm in zip(xs, pbd)]
    x_l = [xm.astype(BF16) for xm in xs]
    us = [_dot(xm, bd(vbs[p][rs])) for xm, (p, rs) in zip(x_l, bodies)]
    ws = [_dot(xm, bd(kbegs[p][rs])) for xm, (p, rs) in zip(x_l, bodies)]
    nc = len(chunks)
    for p in range(n_pairs):
        ls = slice(p * pair_w, (p + 1) * pair_w)
        u_ref[:, ls] = jnp.concatenate(us[p * nc:(p + 1) * nc], axis=0)
        w_ref[:, ls] = jnp.concatenate(ws[p * nc:(p + 1) * nc], axis=0).astype(BF16)


def _gdn_scan_kernel(u_ref, w_ref, qd_ref, kd_ref, qk_ref, gt_ref, z_ref, s0_ref, nw_ref,
                     o_ref, sfin_ref, sbd_ref, *, chunk, n_chunks):
    bb = u_ref.shape[0]
    cn = chunk
    low = w_ref.dtype
    gw = GDN_GROUP_W
    ni = pl.program_id(1)

    @pl.when(ni == 0)
    def _():
        sbd_ref[...] = jnp.zeros_like(sbd_ref)
        for b in range(bb):
            for h in range(GDN_HEADS):
                gi, hh = divmod(h, GDN_GROUP)
                ds = slice(hh * HEAD_DIM, (hh + 1) * HEAD_DIM)
                sbd_ref[b, gi, ds, ds] = s0_ref[b, h]

    bdmask = ((lax.broadcasted_iota(I32, (gw, gw), 0) // HEAD_DIM)
              == (lax.broadcasted_iota(I32, (gw, gw), 1) // HEAD_DIM))
    ones_bd = bdmask.astype(BF16)
    vmask = ((lax.broadcasted_iota(I32, (GDN_GROUP * cn, gw), 0) // cn)
             == (lax.broadcasted_iota(I32, (GDN_GROUP * cn, gw), 1) // HEAD_DIM))
    e_row = lax.broadcasted_iota(I32, (2 * GDN_HEADS, gw), 0)
    e_col = lax.broadcasted_iota(I32, (2 * GDN_HEADS, gw), 1) // HEAD_DIM
    chains = [(b, gi) for b in range(bb) for gi in range(GDN_HEADS // GDN_GROUP)]
    lanes = lambda gi: slice(gi * gw, (gi + 1) * gw)
    states = [sbd_ref[b, gi] for b, gi in chains]
    states_l = [s.astype(low) for s in states]
    v_new = [u_ref[b, :, lanes(gi)] - _dot(w_ref[b, :, lanes(gi)], s_l) for (b, gi), s_l in zip(chains, states_l)]
    q_s = [_dot(qd_ref[b, :, lanes(gi)], s_l) for (b, gi), s_l in zip(chains, states_l)]
    v_l = [v.astype(low) for v in v_new]
    outs = []
    for (b, gi), s, v, qs in zip(chains, states, v_l, q_s):
        vbd = jnp.where(vmask, jnp.concatenate([v] * GDN_GROUP, axis=0), jnp.zeros((), low))
        outs.append(qs + _dot(qk_ref[b, :, gi * GDN_GROUP * cn:(gi + 1) * GDN_GROUP * cn], vbd))
        upd = _dot_tn(kd_ref[b, :, lanes(gi)], v)
        expand = (e_row == e_col + gi * GDN_GROUP).astype(BF16)
        gte = sum(_dot(p.astype(BF16), expand) for p in _split3(gt_ref[b, 0:8, :]))[0:1]
        sbd_ref[b, gi] = s * gte + jnp.where(bdmask, upd, 0.0)
    for (b, gi), o in zip(chains, outs):
        o2 = o * o
        hi = o2.astype(BF16)
        lo = (o2 - hi.astype(F32)).astype(BF16)
        ms = (_dot(hi, ones_bd) + _dot(lo, ones_bd)) * (1.0 / HEAD_DIM)
        zg = z_ref[b, :, lanes(gi)]
        o_ref[b, :, lanes(gi)] = o * lax.rsqrt(ms + EPS) * nw_ref[...] * (zg * jax.nn.sigmoid(zg))

    @pl.when(ni == n_chunks - 1)
    def _():
        for b in range(bb):
            for h in range(GDN_HEADS):
                gi, hh = divmod(h, GDN_GROUP)
                ds = slice(hh * HEAD_DIM, (hh + 1) * HEAD_DIM)
                sfin_ref[b, h] = sbd_ref[b, gi, ds, ds]


def _gdn(c3, hist, z3, ab3, s0, conv_w, alog16, dtb16, nw_group, *, chunk, seq_block):
    nseq, t, _ = c3.shape
    n = nseq * t
    sb, r = (1, GDN_TILE) if t >= GDN_TILE else (GDN_TILE // t, t)
    tiles = t // r
    low = BF16 if chunk >= 16 else F32
    blk = lambda w: pl.BlockSpec((sb, r, w), lambda s, i: (s, i, 0))
    full = lambda a: pl.BlockSpec(a.shape, lambda s, i: (0,) * a.ndim)
    flat = lambda w: pl.BlockSpec((GDN_TILE, w), lambda s, i: (s * tiles + i, 0))
    widths = (GDN_V, GDN_V, GDN_QK, GDN_QK, GDN_HEADS * chunk, 2 * GDN_HEADS)
    dtypes = (F32, low, low, low, low, F32)
    prep_out = dict(out_specs=[flat(wd) for wd in widths],
                    out_shape=[jax.ShapeDtypeStruct((n, wd), dt) for wd, dt in zip(widths, dtypes)],
                    compiler_params=pltpu.CompilerParams(dimension_semantics=("arbitrary", "arbitrary")))
    lane_dense = chunk == HEAD_DIM and sb == 1
    u, w, qd, kd, qk, gt = pl.pallas_call(
        _gdn_prep_pair_kernel if lane_dense else functools.partial(_gdn_prep_kernel, chunk=chunk),
        grid=(nseq // sb, tiles),
        in_specs=[blk(GDN_CONV_CH), pl.BlockSpec((sb, CONV_W - 1, GDN_CONV_CH), lambda s, i: (s, 0, 0)),
                  blk(2 * GDN_HEADS), full(conv_w), full(alog16), full(dtb16)],
        scratch_shapes=[pltpu.VMEM((sb, 8 + r, GDN_CONV_CH), F32)],
        name="gdn_prep_pair" if lane_dense else "gdn_prep", **prep_out)(c3, hist, ab3, conv_w, alog16, dtb16)

    n_chunks = t // chunk
    tok = lambda wd: pl.BlockSpec((seq_block, chunk, wd), lambda s, c: (s, c, 0))
    per_seq = pl.BlockSpec((seq_block,) + s0.shape[1:], lambda s, c: (s, 0, 0, 0))
    seq3 = lambda a: a.reshape(nseq, t, a.shape[-1])
    return pl.pallas_call(
        functools.partial(_gdn_scan_kernel, chunk=chunk, n_chunks=n_chunks),
        grid=(nseq // seq_block, n_chunks),
        in_specs=[tok(wd) for wd in widths] + [tok(GDN_V), per_seq,
                                               pl.BlockSpec(nw_group.shape, lambda s, c: (0, 0))],
        out_specs=[tok(GDN_V), per_seq],
        out_shape=[jax.ShapeDtypeStruct((nseq, t, GDN_V), F32), jax.ShapeDtypeStruct(s0.shape, F32)],
        scratch_shapes=[pltpu.VMEM((seq_block, GDN_HEADS // GDN_GROUP, GDN_GROUP_W, GDN_GROUP_W), F32)],
        compiler_params=pltpu.CompilerParams(dimension_semantics=("arbitrary", "arbitrary")),
        name="gdn_scan",
    )(seq3(u), seq3(w), seq3(qd), seq3(kd), seq3(qk), seq3(gt), z3, s0, nw_group)


def _outproj_kernel(x_ref, osw_ref, ogd_ref, wo_ref, gf_ref, wr_ref, br_ref,
                    h_ref, xn_ref, gates_ref, meta_ref, cnt_ref, run_ref):
    i = pl.program_id(0)
    tm = x_ref.shape[0]
    rows = wr_ref.shape[0]

    @pl.when(i == 0)
    def _():
        run_ref[...] = jnp.zeros_like(run_ref)

    h = (x_ref[...] + _dot(osw_ref[...].astype(BF16), wo_ref[0:SWA_Q, :])
         + _dot(ogd_ref[...].astype(BF16), wo_ref[SWA_Q:D_MIX, :]))
    h_ref[...] = h
    xn = _rms(h, gf_ref[...])
    xn_ref[...] = xn
    logits = _dot_nt(wr_ref[...], xn.astype(BF16))

    row = lax.broadcasted_iota(I32, (rows, tm), 0)
    bias = br_ref[...]
    top = lambda v: jnp.max(v, axis=0, keepdims=True)
    tot = lambda v: jnp.sum(v, axis=0, keepdims=True)
    first_at = lambda v: jnp.min(jnp.where(v == top(v), row, 2 * rows), axis=0, keepdims=True)
    is_g = (row >= N_EXPERTS) & (row < N_EXPERTS + N_GROUPS)
    lg = jnp.where(is_g, logits, NEG_BIG)
    pg = jnp.where(is_g, jnp.exp(lg - top(lg)), 0.0)
    group_p = pg / tot(pg)
    g_row = first_at(jnp.where(is_g, group_p + bias, NEG_BIG))
    g_w = tot(jnp.where(row == g_row, group_p, 0.0))
    sel = (row < N_EXPERTS) & ((row // EXPERTS_PER_GROUP) == (g_row - N_EXPERTS))
    le = jnp.where(sel, logits, NEG_BIG)
    pe = jnp.where(sel, jnp.exp(le - top(le)), 0.0)
    e_p = pe / tot(pe)
    score = jnp.where(sel, e_p + bias, NEG_BIG)
    i1 = first_at(score)
    i2 = first_at(jnp.where(row == i1, NEG_BIG, score))
    oh1 = row == i1
    oh2 = row == i2
    w1 = tot(jnp.where(oh1, e_p, 0.0))
    w2 = tot(jnp.where(oh2, e_p, 0.0))
    wsum = w1 + w2

    ohs = (oh1 | oh2).astype(BF16)
    earlier = (lax.broadcasted_iota(I32, (tm, tm), 0) < lax.broadcasted_iota(I32, (tm, tm), 1)).astype(BF16)
    before = _dot(ohs, earlier) + run_ref[...]
    r1 = tot(jnp.where(oh1, before, 0.0))
    r2 = tot(jnp.where(oh2, before, 0.0))
    run_ref[...] = run_ref[...] + jnp.sum(ohs.astype(F32), axis=1, keepdims=True)
    cnt_ref[...] = run_ref[...]

    zero = jnp.zeros_like(w1)
    meta = jnp.concatenate([i1.astype(F32), i2.astype(F32), r1, r2, g_w * (w1 / wsum), g_w * (w2 / wsum),
                            zero, zero], axis=0)
    meta_ref[...] = meta
    eye8 = (lax.broadcasted_iota(I32, (8, LANES), 0) == lax.broadcasted_iota(I32, (8, LANES), 1)).astype(F32)
    gates_ref[...] = sum(_dot_tn(part, eye8) for part in _split3(meta))[:, 4:6]


def _outproj(x2d, o_swa, o_gdn, wo_bf16, gain, w_router, b_router):
    n = x2d.shape[0]
    tm = ROW_TILE
    row = lambda w: pl.BlockSpec((tm, w), lambda i: (i, 0))
    full = lambda a: pl.BlockSpec(a.shape, lambda i: (0,) * a.ndim)
    return pl.pallas_call(
        _outproj_kernel,
        grid=(n // tm,),
        in_specs=[row(D_MODEL), row(SWA_Q), row(GDN_V), full(wo_bf16), full(gain), full(w_router), full(b_router)],
        out_specs=[row(D_MODEL), row(D_MODEL), row(2), pl.BlockSpec((8, tm), lambda i: (0, i)),
                   pl.BlockSpec((LANES, 1), lambda i: (0, 0))],
        out_shape=[jax.ShapeDtypeStruct((n, D_MODEL), F32), jax.ShapeDtypeStruct((n, D_MODEL), F32),
                   jax.ShapeDtypeStruct((n, 2), F32), jax.ShapeDtypeStruct((8, n), F32),
                   jax.ShapeDtypeStruct((LANES, 1), F32)],
        scratch_shapes=[pltpu.VMEM((LANES, 1), F32)],
        compiler_params=pltpu.CompilerParams(dimension_semantics=("arbitrary",)),
        name="outproj_router",
    )(x2d, o_swa, o_gdn, wo_bf16, gain, w_router, b_router)


def _row_copy(src_ref, src_row, dst_ref, dst_row, sem):
    return pltpu.make_async_copy(src_ref.at[pl.ds(src_row, 1)], dst_ref.at[pl.ds(dst_row, 1)], sem)


def _scatter_kernel(dest0_ref, dest1_ref, xp_ref, xs_ref, out_ref, aidx_ref, sem, *, tiles_p, n_tokens):
    i = pl.program_id(0)
    tm = xp_ref.shape[0]

    def run(src_ref):
        def issue(g, carry):
            for u in range(DMA_UNROLL):
                r = g * DMA_UNROLL + u
                d0, d1 = dest0_ref[r], dest1_ref[r]
                _row_copy(src_ref, r, out_ref, d0, sem).start(priority=0)
                _row_copy(src_ref, r, out_ref, d1, sem).start(priority=1)
                aidx_ref[d0] = i * tm + r
                aidx_ref[d1] = n_tokens + i * tm + r
            return carry

        lax.fori_loop(0, tm // DMA_UNROLL, issue, 0)
        for _ in range(2):
            pltpu.make_async_copy(src_ref, out_ref.at[pl.ds(0, tm)], sem).wait()

    @pl.when(i < tiles_p)
    def _():
        run(xp_ref)

    @pl.when(i >= tiles_p)
    def _():
        run(xs_ref)


def _scatter(dest, xn_p, xn_s):
    tm = ROW_TILE
    tiles_p, tiles_s = xn_p.shape[0] // tm, xn_s.shape[0] // tm
    rows = 2 * (xn_p.shape[0] + xn_s.shape[0])
    idx = pl.BlockSpec((tm,), lambda i: (i,), memory_space=pltpu.SMEM)
    return pl.pallas_call(
        functools.partial(_scatter_kernel, tiles_p=tiles_p, n_tokens=rows // 2),
        grid=(tiles_p + tiles_s,),
        in_specs=[idx, idx,
                  pl.BlockSpec((tm, D_MODEL), lambda i: (jnp.minimum(i, tiles_p - 1), 0)),
                  pl.BlockSpec((tm, D_MODEL), lambda i: (jnp.maximum(i - tiles_p, 0), 0))],
        out_specs=[pl.BlockSpec(memory_space=pl.ANY), pl.BlockSpec(memory_space=pltpu.SMEM)],
        out_shape=[jax.ShapeDtypeStruct((rows, D_MODEL), F32), jax.ShapeDtypeStruct((rows,), I32)],
        scratch_shapes=[pltpu.SemaphoreType.DMA(())],
        compiler_params=pltpu.CompilerParams(dimension_semantics=("arbitrary",)),
        name="scatter_rows",
    )(dest[0], dest[1], xn_p, xn_s)


def _experts_kernel(blk_ref, exp_ref, lo_ref, hi_ref, first_ref, fresh_ref, pblk_ref,
                    aprev_ref, x_ref, wg_ref, wu_ref, wd_ref, ya_ref, wg_l, wu_l, wd_l, ybuf, sem,
                    *, n_items, n_blocks):
    j = pl.program_id(0)
    lo, hi = lo_ref[j], hi_ref[j]
    slot = lax.rem(blk_ref[j], 2)
    quarter = MOE_BLOCK // 4

    def wait_rows(s):
        pltpu.make_async_copy(ybuf.at[s], ya_ref.at[pl.ds(0, MOE_BLOCK)], sem.at[s]).wait()

    def send_rows(s, group):
        for row in range(group * quarter, (group + 1) * quarter):
            _row_copy(ybuf.at[s], row, ya_ref, aprev_ref[0, 0, row], sem.at[s]).start(priority=row % 2)

    @pl.when(fresh_ref[j] == 1)
    def _():
        wg_l[...] = wg_ref[0].astype(BF16)
        wu_l[...] = wu_ref[0].astype(BF16)
        wd_l[...] = wd_ref[0].astype(BF16)

    @pl.when((first_ref[j] == 1) & (blk_ref[j] >= 2))
    def _():
        wait_rows(slot)

    def item(is_first, send_prev):
        send = (lambda group: send_rows(1 - slot, group)) if send_prev else (lambda group: None)
        send(0)
        x = x_ref[...].astype(BF16)
        gate = _dot(x, wg_l[...])
        send(1)
        up = _dot(x, wu_l[...])
        hid = (gate * jax.nn.sigmoid(gate)) * up
        send(2)
        y = _dot(hid.astype(BF16), wd_l[...])
        send(3)
        r = lax.broadcasted_iota(I32, (MOE_BLOCK, 1), 0)
        mine = (r >= lo) & (r < hi)
        ybuf[slot] = jnp.where(mine, y, 0.0 if is_first else ybuf[slot])

    live = hi > lo
    pl.when(live & (first_ref[j] == 1) & (blk_ref[j] >= 1))(functools.partial(item, True, True))
    pl.when(live & (first_ref[j] == 1) & (blk_ref[j] == 0))(functools.partial(item, True, False))
    pl.when(live & (first_ref[j] == 0))(functools.partial(item, False, False))

    @pl.when(j == n_items - 1)
    def _():
        last_slot = (n_blocks - 1) % 2
        for group in range(4):
            send_rows(last_slot, group)
        wait_rows(1 - last_slot)
        wait_rows(last_slot)


def _experts(items, aidx, xs, w_gate, w_up, w_down):
    n_items = items[0].shape[0]
    n_blocks = xs.shape[0] // MOE_BLOCK
    assert n_items > n_blocks + N_EXPERTS - 1 and n_blocks >= 2
    xblk = pl.BlockSpec((MOE_BLOCK, D_MODEL), lambda j, blk, *_: (blk[j], 0))
    wspec = lambda a: pl.BlockSpec((1,) + a.shape[1:], lambda j, blk, ex, *_: (ex[j], 0, 0))
    return pl.pallas_call(
        functools.partial(_experts_kernel, n_items=n_items, n_blocks=n_blocks),
        grid_spec=pltpu.PrefetchScalarGridSpec(
            num_scalar_prefetch=len(items),
            grid=(n_items,),
            in_specs=[pl.BlockSpec((1, 1, MOE_BLOCK), lambda j, *pre: (pre[-1][j], 0, 0), memory_space=pltpu.SMEM),
                      xblk, wspec(w_gate), wspec(w_up), wspec(w_down)],
            out_specs=pl.BlockSpec(memory_space=pl.ANY),
            scratch_shapes=[pltpu.VMEM(w_gate.shape[1:], BF16), pltpu.VMEM(w_up.shape[1:], BF16),
                            pltpu.VMEM(w_down.shape[1:], BF16), pltpu.VMEM((2, MOE_BLOCK, D_MODEL), F32),
                            pltpu.SemaphoreType.DMA((2,))]),
        out_shape=jax.ShapeDtypeStruct(xs.shape, F32),
        compiler_params=pltpu.CompilerParams(dimension_semantics=("arbitrary",)),
        name="experts",
    )(*items, aidx.reshape(n_blocks, 1, MOE_BLOCK), xs, w_gate, w_up, w_down)


def _work_items(counts, total_rows):
    n_blocks = total_rows // MOE_BLOCK
    n_items = n_blocks + N_EXPERTS
    end = jnp.cumsum(counts)
    start = end - counts
    first_blk = start // MOE_BLOCK
    nb = jnp.where(counts > 0, (end - 1) // MOE_BLOCK - first_blk + 1, 0)
    item_end = jnp.cumsum(nb)
    used = item_end[-1]
    j = jnp.arange(n_items, dtype=I32)
    jj = jnp.minimum(j, used - 1)
    e = jnp.minimum(jnp.sum((item_end[None, :] <= jj[:, None]).astype(I32), axis=1), N_EXPERTS - 1)
    onehot = (e[:, None] == jnp.arange(N_EXPERTS, dtype=I32)[None, :]).astype(I32)
    pick = lambda a: jnp.sum(onehot * a[None, :], axis=1)
    blk = pick(first_blk) + (jj - (pick(item_end) - pick(nb)))
    lo = jnp.maximum(pick(start), blk * MOE_BLOCK) - blk * MOE_BLOCK
    hi = jnp.minimum(pick(end), (blk + 1) * MOE_BLOCK) - blk * MOE_BLOCK
    live = j < used
    lo = jnp.where(live, lo, 0)
    hi = jnp.where(live, hi, 0)
    prev_blk = jnp.concatenate([jnp.full((1,), -1, I32), blk[:-1]])
    first = (live & (blk != prev_blk)).astype(I32)
    prev_e = jnp.concatenate([jnp.full((1,), -1, I32), e[:-1]])
    fresh = (live & (e != prev_e)).astype(I32)
    pblk = jnp.where(j == n_items - 1, n_blocks - 1, jnp.maximum(blk - 1, 0))
    return (blk.astype(I32), e.astype(I32), lo.astype(I32), hi.astype(I32), first, fresh, pblk.astype(I32)), start


def _combine_kernel(gates_ref, h_ref, gf_ref, y0_ref, y1_ref, out_ref):
    gates = gates_ref[...]
    y = h_ref[...] + (y0_ref[...] * gates[:, 0:1] + y1_ref[...] * gates[:, 1:2])
    out_ref[...] = _rms(y, gf_ref[...])


def _combine(gates, h, gain, ya, first_token):
    n = h.shape[0]
    tm = COMBINE_TILE
    tile0 = first_token // tm
    slot_tiles = ya.shape[0] // 2 // tm
    row = lambda w: pl.BlockSpec((tm, w), lambda i: (i, 0))
    slot = lambda k: pl.BlockSpec((tm, D_MODEL), lambda i: (k * slot_tiles + tile0 + i, 0))
    return pl.pallas_call(
        _combine_kernel,
        grid=(n // tm,),
        in_specs=[row(2), row(D_MODEL), pl.BlockSpec(gain.shape, lambda i: (0, 0)), slot(0), slot(1)],
        out_specs=row(D_MODEL),
        out_shape=jax.ShapeDtypeStruct((n, D_MODEL), F32),
        compiler_params=pltpu.CompilerParams(dimension_semantics=("arbitrary",)),
        name="combine_norm",
    )(gates, h, gain, ya, ya)


def kernel(x_prompt, x_sample, cache_swa_k, cache_swa_v, state_gdn_conv, state_gdn, norm_mix, w_in, swa_sinks,
           gdn_conv_w, gdn_A_log, gdn_dt_bias, gdn_norm_w, w_out, norm_ffn, w_router_group, b_router_group,
           w_router_expert, b_router_expert, w_exp_gate, w_exp_up, w_exp_down, norm_final):
    depth = w_in.shape[0]
    assert depth == 1, "single trunk layer"
    bp, sp, _ = x_prompt.shape
    bs, ts, _ = x_sample.shape
    np_, ns = bp * sp, bs * ts
    l = 0

    w_in_b = w_in[l].astype(BF16)
    w_out_b = w_out[l].astype(BF16)
    g_mix = norm_mix[l].reshape(1, D_MODEL)
    g_ffn = norm_ffn[l].reshape(1, D_MODEL)
    g_fin = norm_final.reshape(1, D_MODEL)
    pad = LANES - N_EXPERTS - N_GROUPS
    w_router = jnp.concatenate([w_router_expert[l], w_router_group[l], jnp.zeros((D_MODEL, pad), F32)],
                               axis=1).astype(BF16).T
    b_router = jnp.concatenate([b_router_expert[l].reshape(-1), b_router_group[l],
                                jnp.zeros((pad,), F32)])[:, None]
    zeros8 = jnp.zeros((GDN_HEADS,), F32)
    alog16 = jnp.concatenate([gdn_A_log[l], zeros8])[None]
    dtb16 = jnp.concatenate([gdn_dt_bias[l], zeros8])[None]
    nw_group = jnp.tile(gdn_norm_w[l].reshape(1, HEAD_DIM), (1, GDN_GROUP))
    sinks = swa_sinks[l]

    q_p, k_p, v_p, c_p, z_p, ab_p = _inproj(x_prompt.reshape(np_, D_MODEL), g_mix, w_in_b)
    nblk = np_ // WINDOW
    k_p3 = k_p.reshape(nblk, WINDOW, SWA_KV)
    v_p3 = v_p.reshape(nblk, WINDOW, SWA_KV)
    (o_swa_p,) = _swa(sinks, q_p.reshape(nblk, WINDOW, SWA_Q), k_p3, v_p3, k_p3, v_p3,
                      bb=SWA_BLOCKS, blocks_per_seq=sp // WINDOW, emit_cache=False)
    c_p3 = c_p.reshape(bp, sp, GDN_CONV_CH)
    o_gdn_p, s_fin_p = _gdn(c_p3, jnp.zeros((bp, CONV_W - 1, GDN_CONV_CH), F32), z_p.reshape(bp, sp, GDN_V),
                            ab_p.reshape(bp, sp, 2 * GDN_HEADS),
                            jnp.zeros((bp, GDN_HEADS, HEAD_DIM, HEAD_DIM), F32),
                            gdn_conv_w[l], alog16, dtb16, nw_group, chunk=GDN_CHUNK, seq_block=bp)
    h_p, xn_p, gates_p, meta_p, cnt_p = _outproj(
        x_prompt.reshape(np_, D_MODEL), o_swa_p.reshape(np_, SWA_Q), o_gdn_p.reshape(np_, GDN_V),
        w_out_b, g_ffn, w_router, b_router)

    q_s, k_s, v_s, c_s, z_s, ab_s = _inproj(x_sample.reshape(ns, D_MODEL), g_mix, w_in_b)
    o_swa_s, kcache_s, vcache_s = _swa(
        sinks, q_s.reshape(bs, ts, SWA_Q), k_s.reshape(bs, ts, SWA_KV), v_s.reshape(bs, ts, SWA_KV),
        cache_swa_k[l].reshape(bs, WINDOW, SWA_KV), cache_swa_v[l].reshape(bs, WINDOW, SWA_KV),
        bb=16, blocks_per_seq=None, emit_cache=True)
    c_s3 = c_s.reshape(bs, ts, GDN_CONV_CH)
    o_gdn_s, s_fin_s = _gdn(c_s3, state_gdn_conv[l], z_s.reshape(bs, ts, GDN_V),
                            ab_s.reshape(bs, ts, 2 * GDN_HEADS), state_gdn[l],
                            gdn_conv_w[l], alog16, dtb16, nw_group, chunk=ts, seq_block=16)
    h_s, xn_s, gates_s, meta_s, cnt_s = _outproj(
        x_sample.reshape(ns, D_MODEL), o_swa_s.reshape(ns, SWA_Q), o_gdn_s.reshape(ns, GDN_V),
        w_out_b, g_ffn, w_router, b_router)

    cnt_p_i = cnt_p[:N_EXPERTS, 0].astype(I32)
    cnt_s_i = cnt_s[:N_EXPERTS, 0].astype(I32)
    items, start = _work_items(cnt_p_i + cnt_s_i, 2 * (np_ + ns))
    expert_ids = jnp.arange(N_EXPERTS, dtype=I32)
    lookup = lambda table, ids: jnp.sum(jnp.where(ids[..., None] == expert_ids, table, 0), axis=-1)
    dest_p = lookup(start, meta_p[0:2].astype(I32)) + meta_p[2:4].astype(I32)
    dest_s = lookup(start + cnt_p_i, meta_s[0:2].astype(I32)) + meta_s[2:4].astype(I32)
    xs, aidx = _scatter(jnp.concatenate([dest_p, dest_s], axis=1), xn_p, xn_s)
    ya = _experts(items, aidx, xs, w_exp_gate[l], w_exp_up[l], w_exp_down[l])
    y_p = _combine(gates_p, h_p, g_fin, ya, 0)
    y_s = _combine(gates_s, h_s, g_fin, ya, np_)

    kv5 = lambda a, b: a.reshape(b, -1, SWA_KV_HEADS, HEAD_DIM)[None]
    return (y_p.reshape(bp, sp, D_MODEL), y_s.reshape(bs, ts, D_MODEL),
            kv5(k_p.reshape(bp, sp, SWA_KV)[:, -WINDOW:], bp), kv5(v_p.reshape(bp, sp, SWA_KV)[:, -WINDOW:], bp),
            kv5(kcache_s, bs), kv5(vcache_s, bs),
            c_p3[:, -(CONV_W - 1):][None], c_s3[:, -(CONV_W - 1):][None],
            s_fin_p[None], s_fin_s[None])
```

```python
import functools

import jax
import jax.numpy as jnp
from jax import lax
from jax.experimental import pallas as pl
from jax.experimental.pallas import tpu as pltpu

F32 = jnp.float32
BF16 = jnp.bfloat16
I32 = jnp.int32

D_MODEL = 1024
HEAD_DIM = 64
SWA_HEADS = 8
GDN_HEADS = 8
SWA_KV_HEADS = 2
GQA_GROUP = SWA_HEADS // SWA_KV_HEADS
WINDOW = 128
ATTN_SCALE = HEAD_DIM ** -0.5
CONV_W = 4
N_GROUPS = 8
EXPERTS_PER_GROUP = 8
N_EXPERTS = 64
D_EXPERT = 256
EPS = 1e-6

SWA_Q = SWA_HEADS * HEAD_DIM
SWA_KV = SWA_KV_HEADS * HEAD_DIM
GDN_QK = GDN_HEADS * HEAD_DIM
GDN_V = GDN_HEADS * HEAD_DIM
GDN_CONV_CH = 2 * GDN_QK + GDN_V
D_MIX = SWA_Q + GDN_V
D_IN = SWA_Q + 2 * SWA_KV + GDN_CONV_CH + GDN_V + 2 * GDN_HEADS
COL_K = SWA_Q
COL_V = COL_K + SWA_KV
COL_C = COL_V + SWA_KV
COL_Z = COL_C + GDN_CONV_CH
COL_AB = COL_Z + GDN_V

LANES = 128
NEG_BIG = -1e30
ROW_TILE = 512
MOE_BLOCK = 256
COMBINE_TILE = 512
DMA_UNROLL = 8
SWA_BLOCKS = 4
GDN_CHUNK = 64
GDN_TILE = 256
GDN_GROUP = 4
GDN_GROUP_W = GDN_GROUP * HEAD_DIM


def _rms(x, g):
    return x * lax.rsqrt(jnp.mean(x * x, axis=-1, keepdims=True) + EPS) * g


def _dot(a, b):
    return jnp.dot(a, b, preferred_element_type=F32)


def _dot_nt(a, b):
    return lax.dot_general(a, b, (((1,), (1,)), ((), ())), preferred_element_type=F32)


def _dot_tn(a, b):
    return lax.dot_general(a, b, (((0,), (0,)), ((), ())), preferred_element_type=F32)


def _split3(x):
    p1 = x.astype(BF16).astype(F32)
    r = x - p1
    p2 = r.astype(BF16).astype(F32)
    p3 = (r - p2).astype(BF16).astype(F32)
    return p1, p2, p3


def _inproj_kernel(x_ref, g_ref, w_ref, q_ref, k_ref, v_ref, c_ref, z_ref, ab_ref):
    x = x_ref[...]
    xb = _rms(x, g_ref[...]).astype(BF16)
    q_ref[...] = _dot(xb, w_ref[:, 0:COL_K])
    k_ref[...] = _dot(xb, w_ref[:, COL_K:COL_V])
    v_ref[...] = _dot(xb, w_ref[:, COL_V:COL_C])
    c_ref[...] = _dot(xb, w_ref[:, COL_C:COL_Z])
    z_ref[...] = _dot(xb, w_ref[:, COL_Z:COL_AB])
    ab_ref[...] = _dot(xb, w_ref[:, COL_AB:D_IN])


def _inproj(x2d, gain, w_bf16):
    n = x2d.shape[0]
    tm = ROW_TILE
    row = lambda w: pl.BlockSpec((tm, w), lambda i: (i, 0))
    full = lambda a: pl.BlockSpec(a.shape, lambda i: (0,) * a.ndim)
    widths = (SWA_Q, SWA_KV, SWA_KV, GDN_CONV_CH, GDN_V, 2 * GDN_HEADS)
    return pl.pallas_call(
        _inproj_kernel,
        grid=(n // tm,),
        in_specs=[row(D_MODEL), full(gain), full(w_bf16)],
        out_specs=[row(w) for w in widths],
        out_shape=[jax.ShapeDtypeStruct((n, w), F32) for w in widths],
        compiler_params=pltpu.CompilerParams(dimension_semantics=("arbitrary",)),
        name="inproj",
    )(x2d, gain, w_bf16)


def _swa_kernel(sink_ref, q_ref, kc_ref, vc_ref, kp_ref, vp_ref, o_ref, *cache_refs,
                bb, t, blocks_per_seq, emit_cache):
    rows = GQA_GROUP * t
    ri = lax.broadcasted_iota(I32, (rows, 1), 0)
    qi = lax.rem(ri, t)
    gi = ri // t
    if emit_cache:
        kj = lax.broadcasted_iota(I32, (rows, WINDOW), 1)
        mask = kj <= qi + (WINDOW - t)
        mask_first = mask
    else:
        kj = lax.broadcasted_iota(I32, (rows, WINDOW + t), 1)
        mask = (kj <= qi + WINDOW) & (kj > qi)
        has_prev = lax.rem(pl.program_id(0) * bb, blocks_per_seq) != 0
        mask_first = mask & (has_prev | (kj >= WINDOW))
    sinks = []
    for h in range(SWA_KV_HEADS):
        sink = jnp.zeros((rows, 1), F32)
        for g in range(GQA_GROUP):
            sink = jnp.where(gi == g, sink_ref[GQA_GROUP * h + g], sink)
        sinks.append(sink)
    chains = [(b, h) for b in range(bb) for h in range(SWA_KV_HEADS)]
    scores, values = [], []
    for b, h in chains:
        hs = slice(h * HEAD_DIM, (h + 1) * HEAD_DIM)
        kc, vc = kc_ref[b, :, hs], vc_ref[b, :, hs]
        if emit_cache:
            kp, vp = kp_ref[b, :, hs], vp_ref[b, :, hs]
            keys = jnp.concatenate([kp[t:], kc], axis=0)
            vals = jnp.concatenate([vp[t:], vc], axis=0)
            cache_refs[0][b, :, hs] = keys
            cache_refs[1][b, :, hs] = vals
        else:
            kp, vp = (kp_ref[0, :, hs], vp_ref[0, :, hs]) if b == 0 else (kc_ref[b - 1, :, hs], vc_ref[b - 1, :, hs])
            keys = jnp.concatenate([kp, kc], axis=0)
            vals = jnp.concatenate([vp, vc], axis=0)
        q4 = jnp.concatenate(
            [q_ref[b, :, (GQA_GROUP * h + g) * HEAD_DIM:(GQA_GROUP * h + g + 1) * HEAD_DIM]
             for g in range(GQA_GROUP)], axis=0)
        scores.append(_dot_nt(q4.astype(BF16), keys.astype(BF16)))
        values.append(vals.astype(BF16))
    probs, dens = [], []
    for (b, h), s in zip(chains, scores):
        s = jnp.where(mask_first if b == 0 else mask, s * ATTN_SCALE, NEG_BIG)
        m = jnp.maximum(jnp.max(s, axis=-1, keepdims=True), sinks[h])
        p = jnp.exp(s - m)
        dens.append(jnp.sum(p, axis=-1, keepdims=True) + jnp.exp(sinks[h] - m))
        probs.append(p.astype(BF16))
    outs = [_dot(p, v) / den for p, v, den in zip(probs, values, dens)]
    for b in range(bb):
        o_ref[b] = jnp.concatenate([outs[b * SWA_KV_HEADS + h][g * t:(g + 1) * t]
                                    for h in range(SWA_KV_HEADS) for g in range(GQA_GROUP)], axis=-1)


def _swa(sinks, q3, k3, v3, kprev3, vprev3, *, bb, blocks_per_seq, emit_cache):
    nb, t, _ = q3.shape
    cur = lambda w: pl.BlockSpec((bb, t, w), lambda i: (i, 0, 0))
    if emit_cache:
        prev = pl.BlockSpec((bb, WINDOW, SWA_KV), lambda i: (i, 0, 0))
    else:
        assert blocks_per_seq % bb == 0
        prev = pl.BlockSpec((1, WINDOW, SWA_KV), lambda i: (jnp.maximum(i * bb - 1, 0), 0, 0))
    out_specs = [cur(SWA_Q)]
    out_shape = [jax.ShapeDtypeStruct((nb, t, SWA_Q), F32)]
    if emit_cache:
        cache = pl.BlockSpec((bb, WINDOW, SWA_KV), lambda i: (i, 0, 0))
        out_specs += [cache, cache]
        out_shape += [jax.ShapeDtypeStruct((nb, WINDOW, SWA_KV), F32)] * 2
    return pl.pallas_call(
        functools.partial(_swa_kernel, bb=bb, t=t, blocks_per_seq=blocks_per_seq, emit_cache=emit_cache),
        grid=(nb // bb,),
        in_specs=[pl.BlockSpec(memory_space=pltpu.SMEM), cur(SWA_Q), cur(SWA_KV), cur(SWA_KV), prev, prev],
        out_specs=out_specs,
        out_shape=out_shape,
        compiler_params=pltpu.CompilerParams(dimension_semantics=("arbitrary",)),
        name="swa_cache" if emit_cache else "swa_band",
    )(sinks, q3, k3, v3, kprev3, vprev3)


def _gdn_prep_kernel(c_ref, hist_ref, ab_ref, cw_ref, alog_ref, dtb_ref,
                     u_ref, w_ref, qd_ref, kd_ref, qk_ref, gt_ref, cbuf_ref, *, chunk):
    sb, r, _ = c_ref.shape
    tp = sb * r
    cn = chunk
    low = w_ref.dtype
    hist_rows = CONV_W - 1

    @pl.when(pl.program_id(1) == 0)
    def _():
        cbuf_ref[:, 8 - hist_rows:8, :] = hist_ref[...]

    cbuf_ref[:, 8:8 + r, :] = c_ref[...]
    conv = cbuf_ref[:, 8 - hist_rows:8 - hist_rows + r, :] * cw_ref[0:1, :]
    for i in range(1, CONV_W):
        conv = conv + cbuf_ref[:, 8 - hist_rows + i:8 - hist_rows + i + r, :] * cw_ref[i:i + 1, :]
    tail = cbuf_ref[:, 8 + r - hist_rows:8 + r, :]
    cbuf_ref[:, 8 - hist_rows:8, :] = tail
    conv = (conv * jax.nn.sigmoid(conv)).reshape(tp, GDN_CONV_CH)

    ab = ab_ref[...].reshape(tp, 2 * GDN_HEADS)
    is_g = lax.broadcasted_iota(I32, (1, 2 * GDN_HEADS), 1) < GDN_HEADS
    g = jnp.where(is_g, -jnp.exp(alog_ref[...]) * jax.nn.softplus(ab + dtb_ref[...]), 0.0)
    beta = jax.nn.sigmoid(ab)

    ri = lax.broadcasted_iota(I32, (tp, tp), 0)
    ci = lax.broadcasted_iota(I32, (tp, tp), 1)
    same = (ri // cn) == (ci // cn)
    causal = same & (ri >= ci)
    strict = same & (ri > ci)
    eye = (ri == ci).astype(F32)
    stack = jnp.concatenate([causal.astype(BF16), same.astype(BF16)], axis=0)
    both = sum(_dot(stack, p.astype(BF16)) for p in _split3(g))
    gc, gl = both[:tp], both[tp:]
    e16 = lax.broadcasted_iota(I32, (2 * GDN_HEADS, 2 * GDN_HEADS), 0)
    eye16 = (e16 == lax.broadcasted_iota(I32, (2 * GDN_HEADS, 2 * GDN_HEADS), 1)).astype(BF16)
    gc_row = sum(_dot_nt(eye16, p.astype(BF16)) for p in _split3(gc))
    gt_ref[...] = jnp.exp(gl)
    fold = (lax.broadcasted_iota(I32, (tp, cn), 0) % cn == lax.broadcasted_iota(I32, (tp, cn), 1)).astype(low)

    levels = cn.bit_length() - 2
    heads = range(GDN_HEADS)
    lanes_of = lambda i, h: slice(i * GDN_QK + h * HEAD_DIM, i * GDN_QK + (h + 1) * HEAD_DIM)
    qs = [conv[:, lanes_of(0, h)] for h in heads]
    ks = [conv[:, lanes_of(1, h)] for h in heads]
    qs = [q * lax.rsqrt(jnp.sum(q * q, axis=-1, keepdims=True) + EPS) * (HEAD_DIM ** -0.5) for q in qs]
    ks = [k * lax.rsqrt(jnp.sum(k * k, axis=-1, keepdims=True) + EPS) for k in ks]
    g_cs = [gc[:, h:h + 1] for h in heads]
    b_hs = [beta[:, GDN_HEADS + h:GDN_HEADS + h + 1] for h in heads]
    decays = [jnp.exp(jnp.where(causal, g_cs[h] - gc_row[h:h + 1, :], NEG_BIG)) for h in heads]
    egs = [jnp.exp(g_c) for g_c in g_cs]
    kbs = [k * b_h for k, b_h in zip(ks, b_hs)]
    k_ls = [k.astype(low) for k in ks]
    lmats = [jnp.where(strict, _dot_nt(kb.astype(low), k_l) * decay, 0.0) for kb, k_l, decay in zip(kbs, k_ls, decays)]
    qks = [_dot((_dot_nt(q.astype(low), k_l) * decay).astype(low), fold) for q, k_l, decay in zip(qs, k_ls, decays)]
    xs = [eye - lmat for lmat in lmats]
    ps = [_dot(lmat.astype(low), lmat.astype(low)) for lmat in lmats]
    for lev in range(1, levels + 1):
        p_ls = [p.astype(low) for p in ps]
        if lev < levels:
            xp = [_dot(jnp.concatenate([x.astype(low), p_l], axis=0), p_l) for x, p_l in zip(xs, p_ls)]
            xs = [x + m[:tp] for x, m in zip(xs, xp)]
            ps = [m[tp:] for m in xp]
        else:
            xs = [x + _dot(x.astype(low), p_l) for x, p_l in zip(xs, p_ls)]
    sols = [_dot(xs[h].astype(low),
                 jnp.concatenate([conv[:, lanes_of(2, h)] * b_hs[h], kbs[h] * egs[h]], axis=-1).astype(low))
            for h in heads]
    u_ref[...] = jnp.concatenate([sol[:, :HEAD_DIM] for sol in sols], axis=-1)
    w_ref[...] = jnp.concatenate([sol[:, HEAD_DIM:] for sol in sols], axis=-1).astype(low)
    qd_ref[...] = jnp.concatenate([q * eg for q, eg in zip(qs, egs)], axis=-1).astype(low)
    kd_ref[...] = jnp.concatenate([ks[h] * jnp.exp(gl[:, h:h + 1] - g_cs[h]) for h in heads], axis=-1).astype(low)
    qk_ref[...] = jnp.concatenate(qks, axis=-1).astype(low)


def _gdn_prep_pair_kernel(c_ref, hist_ref, ab_ref, cw_ref, alog_ref, dtb_ref,
                          u_ref, w_ref, qd_ref, kd_ref, qk_ref, gt_ref, cbuf_ref):
    _, r, _ = c_ref.shape
    tp = r
    cn = HEAD_DIM
    pair_w = 2 * HEAD_DIM
    hist_rows = CONV_W - 1
    sublanes = 8

    @pl.when(pl.program_id(1) == 0)
    def _():
        cbuf_ref[...] = jnp.zeros_like(cbuf_ref)
        cbuf_ref[:, sublanes - hist_rows:sublanes, :] = hist_ref[...]

    x3 = c_ref[0].reshape(tp // sublanes, sublanes, GDN_CONV_CH)
    before = cbuf_ref[...]
    sub = lax.broadcasted_iota(I32, (1, sublanes, 1), 1)
    conv3 = x3 * cw_ref[CONV_W - 1:CONV_W, :]
    for s in range(1, CONV_W):
        rx = pltpu.roll(x3, s, axis=1)
        rp = jnp.concatenate([pltpu.roll(before, s, axis=1), rx[:-1]], axis=0)
        conv3 = conv3 + jnp.where(sub < s, rp, rx) * cw_ref[CONV_W - 1 - s:CONV_W - s, :]
    cbuf_ref[...] = x3[tp // sublanes - 1:]
    conv = (conv3 * jax.nn.sigmoid(conv3)).reshape(tp, GDN_CONV_CH)

    ab = ab_ref[...].reshape(tp, 2 * GDN_HEADS)
    is_g = lax.broadcasted_iota(I32, (1, 2 * GDN_HEADS), 1) < GDN_HEADS
    g = jnp.where(is_g, -jnp.exp(alog_ref[...]) * jax.nn.softplus(ab + dtb_ref[...]), 0.0)
    beta = jax.nn.sigmoid(ab)

    ri = lax.broadcasted_iota(I32, (tp, tp), 0)
    ci = lax.broadcasted_iota(I32, (tp, tp), 1)
    same = (ri // cn) == (ci // cn)
    same_l = same.astype(BF16)
    stack = jnp.concatenate([(same & (ri >= ci)).astype(BF16), same_l], axis=0)
    both = sum(_dot(stack, p.astype(BF16)) for p in _split3(g))
    gc, gl = both[:tp], both[tp:]
    gt_ref[...] = jnp.exp(gl)

    lane = lax.broadcasted_iota(I32, (tp, pair_w), 1)
    c_in = lax.broadcasted_iota(I32, (tp, pair_w), 0) % cn
    j_in = lane % cn
    left = lane < cn
    causal = c_in >= j_in
    strict = c_in > j_in
    diag = c_in == j_in
    eye = diag.astype(F32)
    bdmask = ((lax.broadcasted_iota(I32, (pair_w, pair_w), 0) // cn)
              == (lax.broadcasted_iota(I32, (pair_w, pair_w), 1) // cn))
    ones_bd = bdmask.astype(BF16)

    def bd(m):
        return jnp.where(bdmask, jnp.concatenate([m, m], axis=0), 0.0).astype(BF16)

    def head_sum(x):
        hi = x.astype(BF16)
        lo = (x - hi.astype(F32)).astype(BF16)
        return _dot(hi, ones_bd) + _dot(lo, ones_bd)

    levels = cn.bit_length() - 2
    chunks = [slice(n * cn, (n + 1) * cn) for n in range(tp // cn)]
    n_pairs = GDN_HEADS // 2
    pairs = range(n_pairs)
    pick = lambda m, off, p: jnp.where(left, m[:, off + 2 * p:off + 2 * p + 1], m[:, off + 2 * p + 1:off + 2 * p + 2])
    third = lambda i, p: conv[:, i * GDN_QK + p * pair_w:i * GDN_QK + (p + 1) * pair_w]
    qs = [third(0, p) for p in pairs]
    ks = [third(1, p) for p in pairs]
    qs = [q * lax.rsqrt(head_sum(q * q) + EPS) * (HEAD_DIM ** -0.5) for q in qs]
    ks = [k * lax.rsqrt(head_sum(k * k) + EPS) for k in ks]
    gcps = [pick(gc, 0, p) for p in pairs]
    rowms = [sum(_dot(same_l, part.astype(BF16)) for part in _split3(jnp.where(diag, gcp, 0.0))) for gcp in gcps]
    decays = [jnp.exp(jnp.where(causal, gcp - rowm, NEG_BIG)) for gcp, rowm in zip(gcps, rowms)]
    lmats, vbs, kbegs = [], [], []
    for p in pairs:
        ls = slice(p * pair_w, (p + 1) * pair_w)
        q, k, gcp, decay = qs[p], ks[p], gcps[p], decays[p]
        bp = pick(beta, GDN_HEADS, p)
        eg = jnp.exp(gcp)
        kb = k * bp
        vbs.append(third(2, p) * bp)
        kbegs.append(kb * eg)
        qd_ref[:, ls] = (q * eg).astype(BF16)
        kd_ref[:, ls] = (k * jnp.exp(pick(gl, 0, p) - gcp)).astype(BF16)
        q_l, kb_l = q.astype(BF16), kb.astype(BF16)
        kbd = [bd(k[rs]) for rs in chunks]
        kk = jnp.concatenate([_dot_nt(kb_l[rs], kbd[n]) for n, rs in enumerate(chunks)], axis=0)
        qk = jnp.concatenate([_dot_nt(q_l[rs], kbd[n]) for n, rs in enumerate(chunks)], axis=0)
        qk_ref[:, ls] = (qk * decay).astype(BF16)
        lmats.append(jnp.where(strict, kk * decay, 0.0))
    bodies = [(p, rs) for p in range(n_pairs) for rs in chunks]
    xs = [eye[rs] - lmats[p][rs] for p, rs in bodies]
    ps = [_dot(lmats[p][rs].astype(BF16), bd(lmats[p][rs])) for p, rs in bodies]
    for lev in range(1, levels + 1):
        pbd = [bd(pm) for pm in ps]
        if lev < levels:
            xp = [_dot(jnp.concatenate([xm, pm], axis=0).astype(BF16), wm) for xm, pm, wm in zip(xs, ps, pbd)]
            xs = [xm + m[:cn] for xm, m in zip(xs, xp)]
            ps = [m[cn:] for m in xp]
        else:
            xs = [xm + _dot(xm.astype(BF16), wm) for xm, wm in zip(xs, pbd)]
    x_l = [xm.astype(BF16) for xm in xs]
    us = [_dot(xm, bd(vbs[p][rs])) for xm, (p, rs) in zip(x_l, bodies)]
    ws = [_dot(xm, bd(kbegs[p][rs])) for xm, (p, rs) in zip(x_l, bodies)]
    nc = len(chunks)
    for p in range(n_pairs):
        ls = slice(p * pair_w, (p + 1) * pair_w)
        u_ref[:, ls] = jnp.concatenate(us[p * nc:(p + 1) * nc], axis=0)
        w_ref[:, ls] = jnp.concatenate(ws[p * nc:(p + 1) * nc], axis=0).astype(BF16)


def _gdn_scan_kernel(u_ref, w_ref, qd_ref, kd_ref, qk_ref, gt_ref, z_ref, s0_ref, nw_ref,
                     o_ref, sfin_ref, sbd_ref, *, chunk, n_chunks):
    bb = u_ref.shape[0]
    cn = chunk
    low = w_ref.dtype
    gw = GDN_GROUP_W
    ni = pl.program_id(1)

    @pl.when(ni == 0)
    def _():
        sbd_ref[...] = jnp.zeros_like(sbd_ref)
        for b in range(bb):
            for h in range(GDN_HEADS):
                gi, hh = divmod(h, GDN_GROUP)
                ds = slice(hh * HEAD_DIM, (hh + 1) * HEAD_DIM)
                sbd_ref[b, gi, ds, ds] = s0_ref[b, h]

    bdmask = ((lax.broadcasted_iota(I32, (gw, gw), 0) // HEAD_DIM)
              == (lax.broadcasted_iota(I32, (gw, gw), 1) // HEAD_DIM))
    ones_bd = bdmask.astype(BF16)
    vmask = ((lax.broadcasted_iota(I32, (GDN_GROUP * cn, gw), 0) // cn)
             == (lax.broadcasted_iota(I32, (GDN_GROUP * cn, gw), 1) // HEAD_DIM))
    e_row = lax.broadcasted_iota(I32, (2 * GDN_HEADS, gw), 0)
    e_col = lax.broadcasted_iota(I32, (2 * GDN_HEADS, gw), 1) // HEAD_DIM
    chains = [(b, gi) for b in range(bb) for gi in range(GDN_HEADS // GDN_GROUP)]
    lanes = lambda gi: slice(gi * gw, (gi + 1) * gw)
    states = [sbd_ref[b, gi] for b, gi in chains]
    states_l = [s.astype(low) for s in states]
    v_new = [u_ref[b, :, lanes(gi)] - _dot(w_ref[b, :, lanes(gi)], s_l) for (b, gi), s_l in zip(chains, states_l)]
    q_s = [_dot(qd_ref[b, :, lanes(gi)], s_l) for (b, gi), s_l in zip(chains, states_l)]
    v_l = [v.astype(low) for v in v_new]
    outs = []
    for (b, gi), s, v, qs in zip(chains, states, v_l, q_s):
        vbd = jnp.where(vmask, jnp.concatenate([v] * GDN_GROUP, axis=0), jnp.zeros((), low))
        outs.append(qs + _dot(qk_ref[b, :, gi * GDN_GROUP * cn:(gi + 1) * GDN_GROUP * cn], vbd))
        upd = _dot_tn(kd_ref[b, :, lanes(gi)], v)
        expand = (e_row == e_col + gi * GDN_GROUP).astype(BF16)
        gte = sum(_dot(p.astype(BF16), expand) for p in _split3(gt_ref[b, 0:8, :]))[0:1]
        sbd_ref[b, gi] = s * gte + jnp.where(bdmask, upd, 0.0)
    for (b, gi), o in zip(chains, outs):
        o2 = o * o
        hi = o2.astype(BF16)
        lo = (o2 - hi.astype(F32)).astype(BF16)
        ms = (_dot(hi, ones_bd) + _dot(lo, ones_bd)) * (1.0 / HEAD_DIM)
        zg = z_ref[b, :, lanes(gi)]
        o_ref[b, :, lanes(gi)] = o * lax.rsqrt(ms + EPS) * nw_ref[...] * (zg * jax.nn.sigmoid(zg))

    @pl.when(ni == n_chunks - 1)
    def _():
        for b in range(bb):
            for h in range(GDN_HEADS):
                gi, hh = divmod(h, GDN_GROUP)
                ds = slice(hh * HEAD_DIM, (hh + 1) * HEAD_DIM)
                sfin_ref[b, h] = sbd_ref[b, gi, ds, ds]


def _gdn(c3, hist, z3, ab3, s0, conv_w, alog16, dtb16, nw_group, *, chunk, seq_block):
    nseq, t, _ = c3.shape
    n = nseq * t
    sb, r = (1, GDN_TILE) if t >= GDN_TILE else (GDN_TILE // t, t)
    tiles = t // r
    low = BF16 if chunk >= 16 else F32
    blk = lambda w: pl.BlockSpec((sb, r, w), lambda s, i: (s, i, 0))
    full = lambda a: pl.BlockSpec(a.shape, lambda s, i: (0,) * a.ndim)
    flat = lambda w: pl.BlockSpec((GDN_TILE, w), lambda s, i: (s * tiles + i, 0))
    widths = (GDN_V, GDN_V, GDN_QK, GDN_QK, GDN_HEADS * chunk, 2 * GDN_HEADS)
    dtypes = (F32, low, low, low, low, F32)
    prep_out = dict(out_specs=[flat(wd) for wd in widths],
                    out_shape=[jax.ShapeDtypeStruct((n, wd), dt) for wd, dt in zip(widths, dtypes)],
                    compiler_params=pltpu.CompilerParams(dimension_semantics=("arbitrary", "arbitrary")))
    lane_dense = chunk == HEAD_DIM and sb == 1
    u, w, qd, kd, qk, gt = pl.pallas_call(
        _gdn_prep_pair_kernel if lane_dense else functools.partial(_gdn_prep_kernel, chunk=chunk),
        grid=(nseq // sb, tiles),
        in_specs=[blk(GDN_CONV_CH), pl.BlockSpec((sb, CONV_W - 1, GDN_CONV_CH), lambda s, i: (s, 0, 0)),
                  blk(2 * GDN_HEADS), full(conv_w), full(alog16), full(dtb16)],
        scratch_shapes=[pltpu.VMEM((1, 8, GDN_CONV_CH) if lane_dense else (sb, 8 + r, GDN_CONV_CH), F32)],
        name="gdn_prep_pair" if lane_dense else "gdn_prep", **prep_out)(c3, hist, ab3, conv_w, alog16, dtb16)

    n_chunks = t // chunk
    tok = lambda wd: pl.BlockSpec((seq_block, chunk, wd), lambda s, c: (s, c, 0))
    per_seq = pl.BlockSpec((seq_block,) + s0.shape[1:], lambda s, c: (s, 0, 0, 0))
    seq3 = lambda a: a.reshape(nseq, t, a.shape[-1])
    return pl.pallas_call(
        functools.partial(_gdn_scan_kernel, chunk=chunk, n_chunks=n_chunks),
        grid=(nseq // seq_block, n_chunks),
        in_specs=[tok(wd) for wd in widths] + [tok(GDN_V), per_seq,
                                               pl.BlockSpec(nw_group.shape, lambda s, c: (0, 0))],
        out_specs=[tok(GDN_V), per_seq],
        out_shape=[jax.ShapeDtypeStruct((nseq, t, GDN_V), F32), jax.ShapeDtypeStruct(s0.shape, F32)],
        scratch_shapes=[pltpu.VMEM((seq_block, GDN_HEADS // GDN_GROUP, GDN_GROUP_W, GDN_GROUP_W), F32)],
        compiler_params=pltpu.CompilerParams(dimension_semantics=("arbitrary", "arbitrary")),
        name="gdn_scan",
    )(seq3(u), seq3(w), seq3(qd), seq3(kd), seq3(qk), seq3(gt), z3, s0, nw_group)


def _outproj_kernel(x_ref, osw_ref, ogd_ref, wo_ref, gf_ref, wr_ref, br_ref,
                    h_ref, xn_ref, gates_ref, meta_ref, cnt_ref, run_ref):
    i = pl.program_id(0)
    tm = x_ref.shape[0]
    rows = wr_ref.shape[0]

    @pl.when(i == 0)
    def _():
        run_ref[...] = jnp.zeros_like(run_ref)

    h = (x_ref[...] + _dot(osw_ref[...].astype(BF16), wo_ref[0:SWA_Q, :])
         + _dot(ogd_ref[...].astype(BF16), wo_ref[SWA_Q:D_MIX, :]))
    h_ref[...] = h
    xn = _rms(h, gf_ref[...])
    xn_ref[...] = xn
    logits = _dot_nt(wr_ref[...], xn.astype(BF16))

    row = lax.broadcasted_iota(I32, (rows, tm), 0)
    bias = br_ref[...]
    top = lambda v: jnp.max(v, axis=0, keepdims=True)
    tot = lambda v: jnp.sum(v, axis=0, keepdims=True)
    first_at = lambda v: jnp.min(jnp.where(v == top(v), row, 2 * rows), axis=0, keepdims=True)
    is_g = (row >= N_EXPERTS) & (row < N_EXPERTS + N_GROUPS)
    lg = jnp.where(is_g, logits, NEG_BIG)
    pg = jnp.where(is_g, jnp.exp(lg - top(lg)), 0.0)
    group_p = pg / tot(pg)
    g_row = first_at(jnp.where(is_g, group_p + bias, NEG_BIG))
    g_w = tot(jnp.where(row == g_row, group_p, 0.0))
    sel = (row < N_EXPERTS) & ((row // EXPERTS_PER_GROUP) == (g_row - N_EXPERTS))
    le = jnp.where(sel, logits, NEG_BIG)
    pe = jnp.where(sel, jnp.exp(le - top(le)), 0.0)
    e_p = pe / tot(pe)
    score = jnp.where(sel, e_p + bias, NEG_BIG)
    i1 = first_at(score)
    i2 = first_at(jnp.where(row == i1, NEG_BIG, score))
    oh1 = row == i1
    oh2 = row == i2
    w1 = tot(jnp.where(oh1, e_p, 0.0))
    w2 = tot(jnp.where(oh2, e_p, 0.0))
    wsum = w1 + w2

    ohs = (oh1 | oh2).astype(BF16)
    earlier = (lax.broadcasted_iota(I32, (tm, tm), 0) < lax.broadcasted_iota(I32, (tm, tm), 1)).astype(BF16)
    before = _dot(ohs, earlier) + run_ref[...]
    r1 = tot(jnp.where(oh1, before, 0.0))
    r2 = tot(jnp.where(oh2, before, 0.0))
    run_ref[...] = run_ref[...] + jnp.sum(ohs.astype(F32), axis=1, keepdims=True)
    cnt_ref[...] = run_ref[...]

    zero = jnp.zeros_like(w1)
    meta = jnp.concatenate([i1.astype(F32), i2.astype(F32), r1, r2, g_w * (w1 / wsum), g_w * (w2 / wsum),
                            zero, zero], axis=0)
    meta_ref[...] = meta
    eye8 = (lax.broadcasted_iota(I32, (8, LANES), 0) == lax.broadcasted_iota(I32, (8, LANES), 1)).astype(F32)
    gates_ref[...] = sum(_dot_tn(part, eye8) for part in _split3(meta))[:, 4:6]


def _outproj(x2d, o_swa, o_gdn, wo_bf16, gain, w_router, b_router):
    n = x2d.shape[0]
    tm = ROW_TILE
    row = lambda w: pl.BlockSpec((tm, w), lambda i: (i, 0))
    full = lambda a: pl.BlockSpec(a.shape, lambda i: (0,) * a.ndim)
    return pl.pallas_call(
        _outproj_kernel,
        grid=(n // tm,),
        in_specs=[row(D_MODEL), row(SWA_Q), row(GDN_V), full(wo_bf16), full(gain), full(w_router), full(b_router)],
        out_specs=[row(D_MODEL), row(D_MODEL), row(2), pl.BlockSpec((8, tm), lambda i: (0, i)),
                   pl.BlockSpec((LANES, 1), lambda i: (0, 0))],
        out_shape=[jax.ShapeDtypeStruct((n, D_MODEL), F32), jax.ShapeDtypeStruct((n, D_MODEL), F32),
                   jax.ShapeDtypeStruct((n, 2), F32), jax.ShapeDtypeStruct((8, n), F32),
                   jax.ShapeDtypeStruct((LANES, 1), F32)],
        scratch_shapes=[pltpu.VMEM((LANES, 1), F32)],
        compiler_params=pltpu.CompilerParams(dimension_semantics=("arbitrary",)),
        name="outproj_router",
    )(x2d, o_swa, o_gdn, wo_bf16, gain, w_router, b_router)


def _row_copy(src_ref, src_row, dst_ref, dst_row, sem):
    return pltpu.make_async_copy(src_ref.at[pl.ds(src_row, 1)], dst_ref.at[pl.ds(dst_row, 1)], sem)


def _scatter_kernel(dest0_ref, dest1_ref, xp_ref, xs_ref, out_ref, aidx_ref, sem, *, tiles_p, n_tokens):
    i = pl.program_id(0)
    tm = xp_ref.shape[0]

    def run(src_ref):
        def issue(g, carry):
            for u in range(DMA_UNROLL):
                r = g * DMA_UNROLL + u
                d0, d1 = dest0_ref[r], dest1_ref[r]
                _row_copy(src_ref, r, out_ref, d0, sem).start(priority=0)
                _row_copy(src_ref, r, out_ref, d1, sem).start(priority=1)
                aidx_ref[d0] = i * tm + r
                aidx_ref[d1] = n_tokens + i * tm + r
            return carry

        lax.fori_loop(0, tm // DMA_UNROLL, issue, 0)
        for _ in range(2):
            pltpu.make_async_copy(src_ref, out_ref.at[pl.ds(0, tm)], sem).wait()

    @pl.when(i < tiles_p)
    def _():
        run(xp_ref)

    @pl.when(i >= tiles_p)
    def _():
        run(xs_ref)


def _scatter(dest, xn_p, xn_s):
    tm = ROW_TILE
    tiles_p, tiles_s = xn_p.shape[0] // tm, xn_s.shape[0] // tm
    rows = 2 * (xn_p.shape[0] + xn_s.shape[0])
    idx = pl.BlockSpec((tm,), lambda i: (i,), memory_space=pltpu.SMEM)
    return pl.pallas_call(
        functools.partial(_scatter_kernel, tiles_p=tiles_p, n_tokens=rows // 2),
        grid=(tiles_p + tiles_s,),
        in_specs=[idx, idx,
                  pl.BlockSpec((tm, D_MODEL), lambda i: (jnp.minimum(i, tiles_p - 1), 0)),
                  pl.BlockSpec((tm, D_MODEL), lambda i: (jnp.maximum(i - tiles_p, 0), 0))],
        out_specs=[pl.BlockSpec(memory_space=pl.ANY), pl.BlockSpec(memory_space=pltpu.SMEM)],
        out_shape=[jax.ShapeDtypeStruct((rows, D_MODEL), F32), jax.ShapeDtypeStruct((rows,), I32)],
        scratch_shapes=[pltpu.SemaphoreType.DMA(())],
        compiler_params=pltpu.CompilerParams(dimension_semantics=("arbitrary",)),
        name="scatter_rows",
    )(dest[0], dest[1], xn_p, xn_s)


def _experts_kernel(blk_ref, exp_ref, lo_ref, hi_ref, first_ref, fresh_ref, pblk_ref,
                    aprev_ref, x_ref, wg_ref, wu_ref, wd_ref, ya_ref, wg_l, wu_l, wd_l, ybuf, sem,
                    *, n_items, n_blocks):
    j = pl.program_id(0)
    lo, hi = lo_ref[j], hi_ref[j]
    slot = lax.rem(blk_ref[j], 2)
    quarter = MOE_BLOCK // 4

    def wait_rows(s):
        pltpu.make_async_copy(ybuf.at[s], ya_ref.at[pl.ds(0, MOE_BLOCK)], sem.at[s]).wait()

    def send_rows(s, group):
        for row in range(group * quarter, (group + 1) * quarter):
            _row_copy(ybuf.at[s], row, ya_ref, aprev_ref[0, 0, row], sem.at[s]).start(priority=row % 2)

    @pl.when(fresh_ref[j] == 1)
    def _():
        wg_l[...] = wg_ref[0].astype(BF16)
        wu_l[...] = wu_ref[0].astype(BF16)
        wd_l[...] = wd_ref[0].astype(BF16)

    @pl.when((first_ref[j] == 1) & (blk_ref[j] >= 2))
    def _():
        wait_rows(slot)

    def item(is_first, send_prev):
        send = (lambda group: send_rows(1 - slot, group)) if send_prev else (lambda group: None)
        send(0)
        x = x_ref[...].astype(BF16)
        gate = _dot(x, wg_l[...])
        send(1)
        up = _dot(x, wu_l[...])
        hid = (gate * jax.nn.sigmoid(gate)) * up
        send(2)
        y = _dot(hid.astype(BF16), wd_l[...])
        send(3)
        r = lax.broadcasted_iota(I32, (MOE_BLOCK, 1), 0)
        mine = (r >= lo) & (r < hi)
        ybuf[slot] = jnp.where(mine, y, 0.0 if is_first else ybuf[slot])

    live = hi > lo
    pl.when(live & (first_ref[j] == 1) & (blk_ref[j] >= 1))(functools.partial(item, True, True))
    pl.when(live & (first_ref[j] == 1) & (blk_ref[j] == 0))(functools.partial(item, True, False))
    pl.when(live & (first_ref[j] == 0))(functools.partial(item, False, False))

    @pl.when(j == n_items - 1)
    def _():
        last_slot = (n_blocks - 1) % 2
        for group in range(4):
            send_rows(last_slot, group)
        wait_rows(1 - last_slot)
        wait_rows(last_slot)


def _experts(items, aidx, xs, w_gate, w_up, w_down):
    n_items = items[0].shape[0]
    n_blocks = xs.shape[0] // MOE_BLOCK
    assert n_items > n_blocks + N_EXPERTS - 1 and n_blocks >= 2
    xblk = pl.BlockSpec((MOE_BLOCK, D_MODEL), lambda j, blk, *_: (blk[j], 0))
    wspec = lambda a: pl.BlockSpec((1,) + a.shape[1:], lambda j, blk, ex, *_: (ex[j], 0, 0))
    return pl.pallas_call(
        functools.partial(_experts_kernel, n_items=n_items, n_blocks=n_blocks),
        grid_spec=pltpu.PrefetchScalarGridSpec(
            num_scalar_prefetch=len(items),
            grid=(n_items,),
            in_specs=[pl.BlockSpec((1, 1, MOE_BLOCK), lambda j, *pre: (pre[-1][j], 0, 0), memory_space=pltpu.SMEM),
                      xblk, wspec(w_gate), wspec(w_up), wspec(w_down)],
            out_specs=pl.BlockSpec(memory_space=pl.ANY),
            scratch_shapes=[pltpu.VMEM(w_gate.shape[1:], BF16), pltpu.VMEM(w_up.shape[1:], BF16),
                            pltpu.VMEM(w_down.shape[1:], BF16), pltpu.VMEM((2, MOE_BLOCK, D_MODEL), F32),
                            pltpu.SemaphoreType.DMA((2,))]),
        out_shape=jax.ShapeDtypeStruct(xs.shape, F32),
        compiler_params=pltpu.CompilerParams(dimension_semantics=("arbitrary",)),
        name="experts",
    )(*items, aidx.reshape(n_blocks, 1, MOE_BLOCK), xs, w_gate, w_up, w_down)


def _work_items(counts, total_rows):
    n_blocks = total_rows // MOE_BLOCK
    n_items = n_blocks + N_EXPERTS
    end = jnp.cumsum(counts)
    start = end - counts
    first_blk = start // MOE_BLOCK
    nb = jnp.where(counts > 0, (end - 1) // MOE_BLOCK - first_blk + 1, 0)
    item_end = jnp.cumsum(nb)
    used = item_end[-1]
    j = jnp.arange(n_items, dtype=I32)
    jj = jnp.minimum(j, used - 1)
    e = jnp.minimum(jnp.sum((item_end[None, :] <= jj[:, None]).astype(I32), axis=1), N_EXPERTS - 1)
    onehot = (e[:, None] == jnp.arange(N_EXPERTS, dtype=I32)[None, :]).astype(I32)
    pick = lambda a: jnp.sum(onehot * a[None, :], axis=1)
    blk = pick(first_blk) + (jj - (pick(item_end) - pick(nb)))
    lo = jnp.maximum(pick(start), blk * MOE_BLOCK) - blk * MOE_BLOCK
    hi = jnp.minimum(pick(end), (blk + 1) * MOE_BLOCK) - blk * MOE_BLOCK
    live = j < used
    lo = jnp.where(live, lo, 0)
    hi = jnp.where(live, hi, 0)
    prev_blk = jnp.concatenate([jnp.full((1,), -1, I32), blk[:-1]])
    first = (live & (blk != prev_blk)).astype(I32)
    prev_e = jnp.concatenate([jnp.full((1,), -1, I32), e[:-1]])
    fresh = (live & (e != prev_e)).astype(I32)
    pblk = jnp.where(j == n_items - 1, n_blocks - 1, jnp.maximum(blk - 1, 0))
    return (blk.astype(I32), e.astype(I32), lo.astype(I32), hi.astype(I32), first, fresh, pblk.astype(I32)), start


def _combine_kernel(gates_ref, h_ref, gf_ref, y0_ref, y1_ref, out_ref):
    gates = gates_ref[...]
    y = h_ref[...] + (y0_ref[...] * gates[:, 0:1] + y1_ref[...] * gates[:, 1:2])
    out_ref[...] = _rms(y, gf_ref[...])


def _combine(gates, h, gain, ya, first_token):
    n = h.shape[0]
    tm = COMBINE_TILE
    tile0 = first_token // tm
    slot_tiles = ya.shape[0] // 2 // tm
    row = lambda w: pl.BlockSpec((tm, w), lambda i: (i, 0))
    slot = lambda k: pl.BlockSpec((tm, D_MODEL), lambda i: (k * slot_tiles + tile0 + i, 0))
    return pl.pallas_call(
        _combine_kernel,
        grid=(n // tm,),
        in_specs=[row(2), row(D_MODEL), pl.BlockSpec(gain.shape, lambda i: (0, 0)), slot(0), slot(1)],
        out_specs=row(D_MODEL),
        out_shape=jax.ShapeDtypeStruct((n, D_MODEL), F32),
        compiler_params=pltpu.CompilerParams(dimension_semantics=("arbitrary",)),
        name="combine_norm",
    )(gates, h, gain, ya, ya)


def kernel(x_prompt, x_sample, cache_swa_k, cache_swa_v, state_gdn_conv, state_gdn, norm_mix, w_in, swa_sinks,
           gdn_conv_w, gdn_A_log, gdn_dt_bias, gdn_norm_w, w_out, norm_ffn, w_router_group, b_router_group,
           w_router_expert, b_router_expert, w_exp_gate, w_exp_up, w_exp_down, norm_final):
    depth = w_in.shape[0]
    assert depth == 1, "single trunk layer"
    bp, sp, _ = x_prompt.shape
    bs, ts, _ = x_sample.shape
    np_, ns = bp * sp, bs * ts
    l = 0

    w_in_b = w_in[l].astype(BF16)
    w_out_b = w_out[l].astype(BF16)
    g_mix = norm_mix[l].reshape(1, D_MODEL)
    g_ffn = norm_ffn[l].reshape(1, D_MODEL)
    g_fin = norm_final.reshape(1, D_MODEL)
    pad = LANES - N_EXPERTS - N_GROUPS
    w_router = jnp.concatenate([w_router_expert[l], w_router_group[l], jnp.zeros((D_MODEL, pad), F32)],
                               axis=1).astype(BF16).T
    b_router = jnp.concatenate([b_router_expert[l].reshape(-1), b_router_group[l],
                                jnp.zeros((pad,), F32)])[:, None]
    zeros8 = jnp.zeros((GDN_HEADS,), F32)
    alog16 = jnp.concatenate([gdn_A_log[l], zeros8])[None]
    dtb16 = jnp.concatenate([gdn_dt_bias[l], zeros8])[None]
    nw_group = jnp.tile(gdn_norm_w[l].reshape(1, HEAD_DIM), (1, GDN_GROUP))
    sinks = swa_sinks[l]

    q_p, k_p, v_p, c_p, z_p, ab_p = _inproj(x_prompt.reshape(np_, D_MODEL), g_mix, w_in_b)
    nblk = np_ // WINDOW
    k_p3 = k_p.reshape(nblk, WINDOW, SWA_KV)
    v_p3 = v_p.reshape(nblk, WINDOW, SWA_KV)
    (o_swa_p,) = _swa(sinks, q_p.reshape(nblk, WINDOW, SWA_Q), k_p3, v_p3, k_p3, v_p3,
                      bb=SWA_BLOCKS, blocks_per_seq=sp // WINDOW, emit_cache=False)
    c_p3 = c_p.reshape(bp, sp, GDN_CONV_CH)
    o_gdn_p, s_fin_p = _gdn(c_p3, jnp.zeros((bp, CONV_W - 1, GDN_CONV_CH), F32), z_p.reshape(bp, sp, GDN_V),
                            ab_p.reshape(bp, sp, 2 * GDN_HEADS),
                            jnp.zeros((bp, GDN_HEADS, HEAD_DIM, HEAD_DIM), F32),
                            gdn_conv_w[l], alog16, dtb16, nw_group, chunk=GDN_CHUNK, seq_block=bp)
    h_p, xn_p, gates_p, meta_p, cnt_p = _outproj(
        x_prompt.reshape(np_, D_MODEL), o_swa_p.reshape(np_, SWA_Q), o_gdn_p.reshape(np_, GDN_V),
        w_out_b, g_ffn, w_router, b_router)

    q_s, k_s, v_s, c_s, z_s, ab_s = _inproj(x_sample.reshape(ns, D_MODEL), g_mix, w_in_b)
    o_swa_s, kcache_s, vcache_s = _swa(
        sinks, q_s.reshape(bs, ts, SWA_Q), k_s.reshape(bs, ts, SWA_KV), v_s.reshape(bs, ts, SWA_KV),
        cache_swa_k[l].reshape(bs, WINDOW, SWA_KV), cache_swa_v[l].reshape(bs, WINDOW, SWA_KV),
        bb=16, blocks_per_seq=None, emit_cache=True)
    c_s3 = c_s.reshape(bs, ts, GDN_CONV_CH)
    o_gdn_s, s_fin_s = _gdn(c_s3, state_gdn_conv[l], z_s.reshape(bs, ts, GDN_V),
                            ab_s.reshape(bs, ts, 2 * GDN_HEADS), state_gdn[l],
                            gdn_conv_w[l], alog16, dtb16, nw_group, chunk=ts, seq_block=16)
    h_s, xn_s, gates_s, meta_s, cnt_s = _outproj(
        x_sample.reshape(ns, D_MODEL), o_swa_s.reshape(ns, SWA_Q), o_gdn_s.reshape(ns, GDN_V),
        w_out_b, g_ffn, w_router, b_router)

    cnt_p_i = cnt_p[:N_EXPERTS, 0].astype(I32)
    cnt_s_i = cnt_s[:N_EXPERTS, 0].astype(I32)
    items, start = _work_items(cnt_p_i + cnt_s_i, 2 * (np_ + ns))
    expert_ids = jnp.arange(N_EXPERTS, dtype=I32)
    lookup = lambda table, ids: jnp.sum(jnp.where(ids[..., None] == expert_ids, table, 0), axis=-1)
    dest_p = lookup(start, meta_p[0:2].astype(I32)) + meta_p[2:4].astype(I32)
    dest_s = lookup(start + cnt_p_i, meta_s[0:2].astype(I32)) + meta_s[2:4].astype(I32)
    xs, aidx = _scatter(jnp.concatenate([dest_p, dest_s], axis=1), xn_p, xn_s)
    ya = _experts(items, aidx, xs, w_exp_gate[l], w_exp_up[l], w_exp_down[l])
    y_p = _combine(gates_p, h_p, g_fin, ya, 0)
    y_s = _combine(gates_s, h_s, g_fin, ya, np_)

    kv5 = lambda a, b: a.reshape(b, -1, SWA_KV_HEADS, HEAD_DIM)[None]
    return (y_p.reshape(bp, sp, D_MODEL), y_s.reshape(bs, ts, D_MODEL),
            kv5(k_p.reshape(bp, sp, SWA_KV)[:, -WINDOW:], bp), kv5(v_p.reshape(bp, sp, SWA_KV)[:, -WINDOW:], bp),
            kv5(kcache_s, bs), kv5(vcache_s, bs),
            c_p3[:, -(CONV_W - 1):][None], c_s3[:, -(CONV_W - 1):][None],
            s_fin_p[None], s_fin_s[None])
```

```python
import functools

import jax
import jax.numpy as jnp
from jax import lax
from jax.experimental import pallas as pl
from jax.experimental.pallas import tpu as pltpu

F32 = jnp.float32
BF16 = jnp.bfloat16
I32 = jnp.int32

D_MODEL = 1024
HEAD_DIM = 64
SWA_HEADS = 8
GDN_HEADS = 8
SWA_KV_HEADS = 2
GQA_GROUP = SWA_HEADS // SWA_KV_HEADS
WINDOW = 128
ATTN_SCALE = HEAD_DIM ** -0.5
CONV_W = 4
N_GROUPS = 8
EXPERTS_PER_GROUP = 8
N_EXPERTS = 64
D_EXPERT = 256
EPS = 1e-6

SWA_Q = SWA_HEADS * HEAD_DIM
SWA_KV = SWA_KV_HEADS * HEAD_DIM
GDN_QK = GDN_HEADS * HEAD_DIM
GDN_V = GDN_HEADS * HEAD_DIM
GDN_CONV_CH = 2 * GDN_QK + GDN_V
D_MIX = SWA_Q + GDN_V
D_IN = SWA_Q + 2 * SWA_KV + GDN_CONV_CH + GDN_V + 2 * GDN_HEADS
COL_K = SWA_Q
COL_V = COL_K + SWA_KV
COL_C = COL_V + SWA_KV
COL_Z = COL_C + GDN_CONV_CH
COL_AB = COL_Z + GDN_V

LANES = 128
NEG_BIG = -1e30
ROW_TILE = 512
INPROJ_TILE = 1024
MOE_BLOCK = 256
COMBINE_TILE = 512
DMA_UNROLL = 8
SWA_BLOCKS = 4
GDN_CHUNK = 64
GDN_TILE = 256
GDN_GROUP = 4
GDN_GROUP_W = GDN_GROUP * HEAD_DIM


def _rms(x, g):
    return x * lax.rsqrt(jnp.mean(x * x, axis=-1, keepdims=True) + EPS) * g


def _dot(a, b):
    return jnp.dot(a, b, preferred_element_type=F32)


def _dot_nt(a, b):
    return lax.dot_general(a, b, (((1,), (1,)), ((), ())), preferred_element_type=F32)


def _dot_tn(a, b):
    return lax.dot_general(a, b, (((0,), (0,)), ((), ())), preferred_element_type=F32)


def _split3(x):
    p1 = x.astype(BF16).astype(F32)
    r = x - p1
    p2 = r.astype(BF16).astype(F32)
    p3 = (r - p2).astype(BF16).astype(F32)
    return p1, p2, p3


def _inproj_kernel(x_ref, g_ref, w_ref, q_ref, k_ref, v_ref, c_ref, z_ref, ab_ref):
    x = x_ref[...]
    xb = _rms(x, g_ref[...]).astype(BF16)
    q_ref[...] = _dot(xb, w_ref[:, 0:COL_K])
    k_ref[...] = _dot(xb, w_ref[:, COL_K:COL_V])
    v_ref[...] = _dot(xb, w_ref[:, COL_V:COL_C])
    c_ref[...] = _dot(xb, w_ref[:, COL_C:COL_Z])
    z_ref[...] = _dot(xb, w_ref[:, COL_Z:COL_AB])
    ab_ref[...] = _dot(xb, w_ref[:, COL_AB:D_IN])


def _inproj(x2d, gain, w_bf16):
    n = x2d.shape[0]
    tm = INPROJ_TILE
    row = lambda w: pl.BlockSpec((tm, w), lambda i: (i, 0))
    full = lambda a: pl.BlockSpec(a.shape, lambda i: (0,) * a.ndim)
    widths = (SWA_Q, SWA_KV, SWA_KV, GDN_CONV_CH, GDN_V, 2 * GDN_HEADS)
    return pl.pallas_call(
        _inproj_kernel,
        grid=(n // tm,),
        in_specs=[row(D_MODEL), full(gain), full(w_bf16)],
        out_specs=[row(w) for w in widths],
        out_shape=[jax.ShapeDtypeStruct((n, w), F32) for w in widths],
        compiler_params=pltpu.CompilerParams(dimension_semantics=("arbitrary",)),
        name="inproj",
    )(x2d, gain, w_bf16)


def _swa_kernel(sink_ref, q_ref, kc_ref, vc_ref, kp_ref, vp_ref, o_ref, *cache_refs,
                bb, t, blocks_per_seq, emit_cache):
    rows = GQA_GROUP * t
    ri = lax.broadcasted_iota(I32, (rows, 1), 0)
    qi = lax.rem(ri, t)
    gi = ri // t
    if emit_cache:
        kj = lax.broadcasted_iota(I32, (rows, WINDOW), 1)
        mask = kj <= qi + (WINDOW - t)
        mask_first = mask
    else:
        kj = lax.broadcasted_iota(I32, (rows, WINDOW + t), 1)
        mask = (kj <= qi + WINDOW) & (kj > qi)
        has_prev = lax.rem(pl.program_id(0) * bb, blocks_per_seq) != 0
        mask_first = mask & (has_prev | (kj >= WINDOW))
    sinks = []
    for h in range(SWA_KV_HEADS):
        sink = jnp.zeros((rows, 1), F32)
        for g in range(GQA_GROUP):
            sink = jnp.where(gi == g, sink_ref[GQA_GROUP * h + g], sink)
        sinks.append(sink)
    chains = [(b, h) for b in range(bb) for h in range(SWA_KV_HEADS)]
    scores, values = [], []
    for b, h in chains:
        hs = slice(h * HEAD_DIM, (h + 1) * HEAD_DIM)
        kc, vc = kc_ref[b, :, hs], vc_ref[b, :, hs]
        if emit_cache:
            kp, vp = kp_ref[b, :, hs], vp_ref[b, :, hs]
            keys = jnp.concatenate([kp[t:], kc], axis=0)
            vals = jnp.concatenate([vp[t:], vc], axis=0)
            cache_refs[0][b, :, hs] = keys
            cache_refs[1][b, :, hs] = vals
        else:
            kp, vp = (kp_ref[0, :, hs], vp_ref[0, :, hs]) if b == 0 else (kc_ref[b - 1, :, hs], vc_ref[b - 1, :, hs])
            keys = jnp.concatenate([kp, kc], axis=0)
            vals = jnp.concatenate([vp, vc], axis=0)
        q4 = jnp.concatenate(
            [q_ref[b, :, (GQA_GROUP * h + g) * HEAD_DIM:(GQA_GROUP * h + g + 1) * HEAD_DIM]
             for g in range(GQA_GROUP)], axis=0)
        scores.append(_dot_nt(q4.astype(BF16), keys.astype(BF16)))
        values.append(vals.astype(BF16))
    probs, dens = [], []
    for (b, h), s in zip(chains, scores):
        s = jnp.where(mask_first if b == 0 else mask, s * ATTN_SCALE, NEG_BIG)
        m = jnp.maximum(jnp.max(s, axis=-1, keepdims=True), sinks[h])
        p = jnp.exp(s - m)
        dens.append(jnp.sum(p, axis=-1, keepdims=True) + jnp.exp(sinks[h] - m))
        probs.append(p.astype(BF16))
    outs = [_dot(p, v) / den for p, v, den in zip(probs, values, dens)]
    for b in range(bb):
        o_ref[b] = jnp.concatenate([outs[b * SWA_KV_HEADS + h][g * t:(g + 1) * t]
                                    for h in range(SWA_KV_HEADS) for g in range(GQA_GROUP)], axis=-1)


def _swa(sinks, q3, k3, v3, kprev3, vprev3, *, bb, blocks_per_seq, emit_cache):
    nb, t, _ = q3.shape
    cur = lambda w: pl.BlockSpec((bb, t, w), lambda i: (i, 0, 0))
    if emit_cache:
        prev = pl.BlockSpec((bb, WINDOW, SWA_KV), lambda i: (i, 0, 0))
    else:
        assert blocks_per_seq % bb == 0
        prev = pl.BlockSpec((1, WINDOW, SWA_KV), lambda i: (jnp.maximum(i * bb - 1, 0), 0, 0))
    out_specs = [cur(SWA_Q)]
    out_shape = [jax.ShapeDtypeStruct((nb, t, SWA_Q), F32)]
    if emit_cache:
        cache = pl.BlockSpec((bb, WINDOW, SWA_KV), lambda i: (i, 0, 0))
        out_specs += [cache, cache]
        out_shape += [jax.ShapeDtypeStruct((nb, WINDOW, SWA_KV), F32)] * 2
    return pl.pallas_call(
        functools.partial(_swa_kernel, bb=bb, t=t, blocks_per_seq=blocks_per_seq, emit_cache=emit_cache),
        grid=(nb // bb,),
        in_specs=[pl.BlockSpec(memory_space=pltpu.SMEM), cur(SWA_Q), cur(SWA_KV), cur(SWA_KV), prev, prev],
        out_specs=out_specs,
        out_shape=out_shape,
        compiler_params=pltpu.CompilerParams(dimension_semantics=("arbitrary",)),
        name="swa_cache" if emit_cache else "swa_band",
    )(sinks, q3, k3, v3, kprev3, vprev3)


def _gdn_prep_kernel(c_ref, hist_ref, ab_ref, cw_ref, alog_ref, dtb_ref,
                     u_ref, w_ref, qd_ref, kd_ref, qk_ref, gt_ref, cbuf_ref, *, chunk):
    sb, r, _ = c_ref.shape
    tp = sb * r
    cn = chunk
    low = w_ref.dtype
    hist_rows = CONV_W - 1

    @pl.when(pl.program_id(1) == 0)
    def _():
        cbuf_ref[:, 8 - hist_rows:8, :] = hist_ref[...]

    cbuf_ref[:, 8:8 + r, :] = c_ref[...]
    conv = cbuf_ref[:, 8 - hist_rows:8 - hist_rows + r, :] * cw_ref[0:1, :]
    for i in range(1, CONV_W):
        conv = conv + cbuf_ref[:, 8 - hist_rows + i:8 - hist_rows + i + r, :] * cw_ref[i:i + 1, :]
    tail = cbuf_ref[:, 8 + r - hist_rows:8 + r, :]
    cbuf_ref[:, 8 - hist_rows:8, :] = tail
    conv = (conv * jax.nn.sigmoid(conv)).reshape(tp, GDN_CONV_CH)

    ab = ab_ref[...].reshape(tp, 2 * GDN_HEADS)
    is_g = lax.broadcasted_iota(I32, (1, 2 * GDN_HEADS), 1) < GDN_HEADS
    g = jnp.where(is_g, -jnp.exp(alog_ref[...]) * jax.nn.softplus(ab + dtb_ref[...]), 0.0)
    beta = jax.nn.sigmoid(ab)

    ri = lax.broadcasted_iota(I32, (tp, tp), 0)
    ci = lax.broadcasted_iota(I32, (tp, tp), 1)
    same = (ri // cn) == (ci // cn)
    causal = same & (ri >= ci)
    strict = same & (ri > ci)
    eye = (ri == ci).astype(F32)
    stack = jnp.concatenate([causal.astype(BF16), same.astype(BF16)], axis=0)
    both = sum(_dot(stack, p.astype(BF16)) for p in _split3(g))
    gc, gl = both[:tp], both[tp:]
    e16 = lax.broadcasted_iota(I32, (2 * GDN_HEADS, 2 * GDN_HEADS), 0)
    eye16 = (e16 == lax.broadcasted_iota(I32, (2 * GDN_HEADS, 2 * GDN_HEADS), 1)).astype(BF16)
    gc_row = sum(_dot_nt(eye16, p.astype(BF16)) for p in _split3(gc))
    gt_ref[...] = jnp.exp(gl)
    fold = (lax.broadcasted_iota(I32, (tp, cn), 0) % cn == lax.broadcasted_iota(I32, (tp, cn), 1)).astype(low)

    levels = cn.bit_length() - 2
    heads = range(GDN_HEADS)
    lanes_of = lambda i, h: slice(i * GDN_QK + h * HEAD_DIM, i * GDN_QK + (h + 1) * HEAD_DIM)
    qs = [conv[:, lanes_of(0, h)] for h in heads]
    ks = [conv[:, lanes_of(1, h)] for h in heads]
    qs = [q * lax.rsqrt(jnp.sum(q * q, axis=-1, keepdims=True) + EPS) * (HEAD_DIM ** -0.5) for q in qs]
    ks = [k * lax.rsqrt(jnp.sum(k * k, axis=-1, keepdims=True) + EPS) for k in ks]
    g_cs = [gc[:, h:h + 1] for h in heads]
    b_hs = [beta[:, GDN_HEADS + h:GDN_HEADS + h + 1] for h in heads]
    decays = [jnp.exp(jnp.where(causal, g_cs[h] - gc_row[h:h + 1, :], NEG_BIG)) for h in heads]
    egs = [jnp.exp(g_c) for g_c in g_cs]
    kbs = [k * b_h for k, b_h in zip(ks, b_hs)]
    k_ls = [k.astype(low) for k in ks]
    lmats = [jnp.where(strict, _dot_nt(kb.astype(low), k_l) * decay, 0.0) for kb, k_l, decay in zip(kbs, k_ls, decays)]
    qks = [_dot((_dot_nt(q.astype(low), k_l) * decay).astype(low), fold) for q, k_l, decay in zip(qs, k_ls, decays)]
    xs = [eye - lmat for lmat in lmats]
    ps = [_dot(lmat.astype(low), lmat.astype(low)) for lmat in lmats]
    for lev in range(1, levels + 1):
        p_ls = [p.astype(low) for p in ps]
        if lev < levels:
            xp = [_dot(jnp.concatenate([x.astype(low), p_l], axis=0), p_l) for x, p_l in zip(xs, p_ls)]
            xs = [x + m[:tp] for x, m in zip(xs, xp)]
            ps = [m[tp:] for m in xp]
        else:
            xs = [x + _dot(x.astype(low), p_l) for x, p_l in zip(xs, p_ls)]
    sols = [_dot(xs[h].astype(low),
                 jnp.concatenate([conv[:, lanes_of(2, h)] * b_hs[h], kbs[h] * egs[h]], axis=-1).astype(low))
            for h in heads]
    u_ref[...] = jnp.concatenate([sol[:, :HEAD_DIM] for sol in sols], axis=-1)
    w_ref[...] = jnp.concatenate([sol[:, HEAD_DIM:] for sol in sols], axis=-1).astype(low)
    qd_ref[...] = jnp.concatenate([q * eg for q, eg in zip(qs, egs)], axis=-1).astype(low)
    kd_ref[...] = jnp.concatenate([ks[h] * jnp.exp(gl[:, h:h + 1] - g_cs[h]) for h in heads], axis=-1).astype(low)
    qk_ref[...] = jnp.concatenate(qks, axis=-1).astype(low)


def _gdn_prep_pair_kernel(c_ref, hist_ref, ab_ref, cw_ref, alog_ref, dtb_ref,
                          u_ref, w_ref, qd_ref, kd_ref, qk_ref, gt_ref, cbuf_ref):
    _, r, _ = c_ref.shape
    tp = r
    cn = HEAD_DIM
    pair_w = 2 * HEAD_DIM
    hist_rows = CONV_W - 1
    sublanes = 8

    @pl.when(pl.program_id(1) == 0)
    def _():
        cbuf_ref[...] = jnp.zeros_like(cbuf_ref)
        cbuf_ref[:, sublanes - hist_rows:sublanes, :] = hist_ref[...]

    x3 = c_ref[0].reshape(tp // sublanes, sublanes, GDN_CONV_CH)
    before = cbuf_ref[...]
    sub = lax.broadcasted_iota(I32, (1, sublanes, 1), 1)
    conv3 = x3 * cw_ref[CONV_W - 1:CONV_W, :]
    for s in range(1, CONV_W):
        rx = pltpu.roll(x3, s, axis=1)
        rp = jnp.concatenate([pltpu.roll(before, s, axis=1), rx[:-1]], axis=0)
        conv3 = conv3 + jnp.where(sub < s, rp, rx) * cw_ref[CONV_W - 1 - s:CONV_W - s, :]
    cbuf_ref[...] = x3[tp // sublanes - 1:]
    conv = (conv3 * jax.nn.sigmoid(conv3)).reshape(tp, GDN_CONV_CH)

    ab = ab_ref[...].reshape(tp, 2 * GDN_HEADS)
    is_g = lax.broadcasted_iota(I32, (1, 2 * GDN_HEADS), 1) < GDN_HEADS
    g = jnp.where(is_g, -jnp.exp(alog_ref[...]) * jax.nn.softplus(ab + dtb_ref[...]), 0.0)
    beta = jax.nn.sigmoid(ab)

    ri = lax.broadcasted_iota(I32, (tp, tp), 0)
    ci = lax.broadcasted_iota(I32, (tp, tp), 1)
    same = (ri // cn) == (ci // cn)
    same_l = same.astype(BF16)
    stack = jnp.concatenate([(same & (ri >= ci)).astype(BF16), same_l], axis=0)
    both = sum(_dot(stack, p.astype(BF16)) for p in _split3(g))
    gc, gl = both[:tp], both[tp:]
    gt_ref[...] = jnp.exp(gl)

    lane = lax.broadcasted_iota(I32, (tp, pair_w), 1)
    c_in = lax.broadcasted_iota(I32, (tp, pair_w), 0) % cn
    j_in = lane % cn
    left = lane < cn
    causal = c_in >= j_in
    strict = c_in > j_in
    diag = c_in == j_in
    eye = diag.astype(F32)
    bdmask = ((lax.broadcasted_iota(I32, (pair_w, pair_w), 0) // cn)
              == (lax.broadcasted_iota(I32, (pair_w, pair_w), 1) // cn))
    ones_bd = bdmask.astype(BF16)

    def bd(m):
        return jnp.where(bdmask, jnp.concatenate([m, m], axis=0), 0.0).astype(BF16)

    def head_sum(x):
        hi = x.astype(BF16)
        lo = (x - hi.astype(F32)).astype(BF16)
        return _dot(hi, ones_bd) + _dot(lo, ones_bd)

    levels = cn.bit_length() - 2
    chunks = [slice(n * cn, (n + 1) * cn) for n in range(tp // cn)]
    n_pairs = GDN_HEADS // 2
    pairs = range(n_pairs)
    pick = lambda m, off, p: jnp.where(left, m[:, off + 2 * p:off + 2 * p + 1], m[:, off + 2 * p + 1:off + 2 * p + 2])
    third = lambda i, p: conv[:, i * GDN_QK + p * pair_w:i * GDN_QK + (p + 1) * pair_w]
    qs = [third(0, p) for p in pairs]
    ks = [third(1, p) for p in pairs]
    qs = [q * lax.rsqrt(head_sum(q * q) + EPS) * (HEAD_DIM ** -0.5) for q in qs]
    ks = [k * lax.rsqrt(head_sum(k * k) + EPS) for k in ks]
    gcps = [pick(gc, 0, p) for p in pairs]
    rowms = [sum(_dot(same_l, part.astype(BF16)) for part in _split3(jnp.where(diag, gcp, 0.0))) for gcp in gcps]
    decays = [jnp.exp(jnp.where(causal, gcp - rowm, NEG_BIG)) for gcp, rowm in zip(gcps, rowms)]
    lmats, vbs, kbegs = [], [], []
    for p in pairs:
        ls = slice(p * pair_w, (p + 1) * pair_w)
        q, k, gcp, decay = qs[p], ks[p], gcps[p], decays[p]
        bp = pick(beta, GDN_HEADS, p)
        eg = jnp.exp(gcp)
        kb = k * bp
        vbs.append(third(2, p) * bp)
        kbegs.append(kb * eg)
        qd_ref[:, ls] = (q * eg).astype(BF16)
        kd_ref[:, ls] = (k * jnp.exp(pick(gl, 0, p) - gcp)).astype(BF16)
        q_l, kb_l = q.astype(BF16), kb.astype(BF16)
        kbd = [bd(k[rs]) for rs in chunks]
        kk = jnp.concatenate([_dot_nt(kb_l[rs], kbd[n]) for n, rs in enumerate(chunks)], axis=0)
        qk = jnp.concatenate([_dot_nt(q_l[rs], kbd[n]) for n, rs in enumerate(chunks)], axis=0)
        qk_ref[:, ls] = (qk * decay).astype(BF16)
        lmats.append(jnp.where(strict, kk * decay, 0.0))
    bodies = [(p, rs) for p in range(n_pairs) for rs in chunks]
    xs = [eye[rs] - lmats[p][rs] for p, rs in bodies]
    ps = [_dot(lmats[p][rs].astype(BF16), bd(lmats[p][rs])) for p, rs in bodies]
    for lev in range(1, levels + 1):
        pbd = [bd(pm) for pm in ps]
        if lev < levels:
            xp = [_dot(jnp.concatenate([xm, pm], axis=0).astype(BF16), wm) for xm, pm, wm in zip(xs, ps, pbd)]
            xs = [xm + m[:cn] for xm, m in zip(xs, xp)]
            ps = [m[cn:] for m in xp]
        else:
            xs = [xm + _dot(xm.astype(BF16), wm) for xm, wm in zip(xs, pbd)]
    x_l = [xm.astype(BF16) for xm in xs]
    us = [_dot(xm, bd(vbs[p][rs])) for xm, (p, rs) in zip(x_l, bodies)]
    ws = [_dot(xm, bd(kbegs[p][rs])) for xm, (p, rs) in zip(x_l, bodies)]
    nc = len(chunks)
    for p in range(n_pairs):
        ls = slice(p * pair_w, (p + 1) * pair_w)
        u_ref[:, ls] = jnp.concatenate(us[p * nc:(p + 1) * nc], axis=0)
        w_ref[:, ls] = jnp.concatenate(ws[p * nc:(p + 1) * nc], axis=0).astype(BF16)


def _gdn_scan_kernel(u_ref, w_ref, qd_ref, kd_ref, qk_ref, gt_ref, z_ref, s0_ref, nw_ref,
                     o_ref, sfin_ref, sbd_ref, *, chunk, n_chunks):
    bb = u_ref.shape[0]
    cn = chunk
    low = w_ref.dtype
    gw = GDN_GROUP_W
    ni = pl.program_id(1)

    @pl.when(ni == 0)
    def _():
        sbd_ref[...] = jnp.zeros_like(sbd_ref)
        for b in range(bb):
            for h in range(GDN_HEADS):
                gi, hh = divmod(h, GDN_GROUP)
                ds = slice(hh * HEAD_DIM, (hh + 1) * HEAD_DIM)
                sbd_ref[b, gi, ds, ds] = s0_ref[b, h]

    bdmask = ((lax.broadcasted_iota(I32, (gw, gw), 0) // HEAD_DIM)
              == (lax.broadcasted_iota(I32, (gw, gw), 1) // HEAD_DIM))
    ones_bd = bdmask.astype(BF16)
    vmask = ((lax.broadcasted_iota(I32, (GDN_GROUP * cn, gw), 0) // cn)
             == (lax.broadcasted_iota(I32, (GDN_GROUP * cn, gw), 1) // HEAD_DIM))
    e_row = lax.broadcasted_iota(I32, (2 * GDN_HEADS, gw), 0)
    e_col = lax.broadcasted_iota(I32, (2 * GDN_HEADS, gw), 1) // HEAD_DIM
    chains = [(b, gi) for b in range(bb) for gi in range(GDN_HEADS // GDN_GROUP)]
    lanes = lambda gi: slice(gi * gw, (gi + 1) * gw)
    states = [sbd_ref[b, gi] for b, gi in chains]
    states_l = [s.astype(low) for s in states]
    wq_s = [_dot(jnp.concatenate([w_ref[b, :, lanes(gi)], qd_ref[b, :, lanes(gi)]], axis=0), s_l)
            for (b, gi), s_l in zip(chains, states_l)]
    v_new = [u_ref[b, :, lanes(gi)] - m[:cn] for (b, gi), m in zip(chains, wq_s)]
    q_s = [m[cn:] for m in wq_s]
    v_l = [v.astype(low) for v in v_new]
    outs = []
    for (b, gi), s, v, qs in zip(chains, states, v_l, q_s):
        vbd = jnp.where(vmask, jnp.concatenate([v] * GDN_GROUP, axis=0), jnp.zeros((), low))
        outs.append(qs + _dot(qk_ref[b, :, gi * GDN_GROUP * cn:(gi + 1) * GDN_GROUP * cn], vbd))
        upd = _dot_tn(kd_ref[b, :, lanes(gi)], v)
        expand = (e_row == e_col + gi * GDN_GROUP).astype(BF16)
        gte = sum(_dot(p.astype(BF16), expand) for p in _split3(gt_ref[b, 0:8, :]))[0:1]
        sbd_ref[b, gi] = s * gte + jnp.where(bdmask, upd, 0.0)
    for (b, gi), o in zip(chains, outs):
        o2 = o * o
        hi = o2.astype(BF16)
        lo = (o2 - hi.astype(F32)).astype(BF16)
        if cn % 16 == 0:
            sums = _dot(jnp.concatenate([hi, lo], axis=0), ones_bd)
            ms = (sums[:cn] + sums[cn:]) * (1.0 / HEAD_DIM)
        else:
            ms = (_dot(hi, ones_bd) + _dot(lo, ones_bd)) * (1.0 / HEAD_DIM)
        zg = z_ref[b, :, lanes(gi)]
        o_ref[b, :, lanes(gi)] = o * lax.rsqrt(ms + EPS) * nw_ref[...] * (zg * jax.nn.sigmoid(zg))

    @pl.when(ni == n_chunks - 1)
    def _():
        for b in range(bb):
            for h in range(GDN_HEADS):
                gi, hh = divmod(h, GDN_GROUP)
                ds = slice(hh * HEAD_DIM, (hh + 1) * HEAD_DIM)
                sfin_ref[b, h] = sbd_ref[b, gi, ds, ds]


def _gdn(c3, hist, z3, ab3, s0, conv_w, alog16, dtb16, nw_group, *, chunk, seq_block):
    nseq, t, _ = c3.shape
    n = nseq * t
    sb, r = (1, GDN_TILE) if t >= GDN_TILE else (GDN_TILE // t, t)
    tiles = t // r
    low = BF16 if chunk >= 16 else F32
    blk = lambda w: pl.BlockSpec((sb, r, w), lambda s, i: (s, i, 0))
    full = lambda a: pl.BlockSpec(a.shape, lambda s, i: (0,) * a.ndim)
    flat = lambda w: pl.BlockSpec((GDN_TILE, w), lambda s, i: (s * tiles + i, 0))
    widths = (GDN_V, GDN_V, GDN_QK, GDN_QK, GDN_HEADS * chunk, 2 * GDN_HEADS)
    dtypes = (F32, low, low, low, low, F32)
    prep_out = dict(out_specs=[flat(wd) for wd in widths],
                    out_shape=[jax.ShapeDtypeStruct((n, wd), dt) for wd, dt in zip(widths, dtypes)],
                    compiler_params=pltpu.CompilerParams(dimension_semantics=("arbitrary", "arbitrary")))
    lane_dense = chunk == HEAD_DIM and sb == 1
    u, w, qd, kd, qk, gt = pl.pallas_call(
        _gdn_prep_pair_kernel if lane_dense else functools.partial(_gdn_prep_kernel, chunk=chunk),
        grid=(nseq // sb, tiles),
        in_specs=[blk(GDN_CONV_CH), pl.BlockSpec((sb, CONV_W - 1, GDN_CONV_CH), lambda s, i: (s, 0, 0)),
                  blk(2 * GDN_HEADS), full(conv_w), full(alog16), full(dtb16)],
        scratch_shapes=[pltpu.VMEM((1, 8, GDN_CONV_CH) if lane_dense else (sb, 8 + r, GDN_CONV_CH), F32)],
        name="gdn_prep_pair" if lane_dense else "gdn_prep", **prep_out)(c3, hist, ab3, conv_w, alog16, dtb16)

    n_chunks = t // chunk
    tok = lambda wd: pl.BlockSpec((seq_block, chunk, wd), lambda s, c: (s, c, 0))
    per_seq = pl.BlockSpec((seq_block,) + s0.shape[1:], lambda s, c: (s, 0, 0, 0))
    seq3 = lambda a: a.reshape(nseq, t, a.shape[-1])
    return pl.pallas_call(
        functools.partial(_gdn_scan_kernel, chunk=chunk, n_chunks=n_chunks),
        grid=(nseq // seq_block, n_chunks),
        in_specs=[tok(wd) for wd in widths] + [tok(GDN_V), per_seq,
                                               pl.BlockSpec(nw_group.shape, lambda s, c: (0, 0))],
        out_specs=[tok(GDN_V), per_seq],
        out_shape=[jax.ShapeDtypeStruct((nseq, t, GDN_V), F32), jax.ShapeDtypeStruct(s0.shape, F32)],
        scratch_shapes=[pltpu.VMEM((seq_block, GDN_HEADS // GDN_GROUP, GDN_GROUP_W, GDN_GROUP_W), F32)],
        compiler_params=pltpu.CompilerParams(dimension_semantics=("arbitrary", "arbitrary")),
        name="gdn_scan",
    )(seq3(u), seq3(w), seq3(qd), seq3(kd), seq3(qk), seq3(gt), z3, s0, nw_group)


def _outproj_kernel(x_ref, osw_ref, ogd_ref, wo_ref, gf_ref, wr_ref, br_ref,
                    h_ref, xn_ref, gates_ref, meta_ref, cnt_ref, run_ref):
    i = pl.program_id(0)
    tm = x_ref.shape[0]
    rows = wr_ref.shape[0]

    @pl.when(i == 0)
    def _():
        run_ref[...] = jnp.zeros_like(run_ref)

    h = (x_ref[...] + _dot(osw_ref[...].astype(BF16), wo_ref[0:SWA_Q, :])
         + _dot(ogd_ref[...].astype(BF16), wo_ref[SWA_Q:D_MIX, :]))
    h_ref[...] = h
    xn = _rms(h, gf_ref[...])
    xn_ref[...] = xn
    logits = _dot_nt(wr_ref[...], xn.astype(BF16))

    row = lax.broadcasted_iota(I32, (rows, tm), 0)
    bias = br_ref[...]
    top = lambda v: jnp.max(v, axis=0, keepdims=True)
    tot = lambda v: jnp.sum(v, axis=0, keepdims=True)
    first_at = lambda v: jnp.min(jnp.where(v == top(v), row, 2 * rows), axis=0, keepdims=True)
    is_g = (row >= N_EXPERTS) & (row < N_EXPERTS + N_GROUPS)
    lg = jnp.where(is_g, logits, NEG_BIG)
    pg = jnp.where(is_g, jnp.exp(lg - top(lg)), 0.0)
    group_p = pg / tot(pg)
    g_row = first_at(jnp.where(is_g, group_p + bias, NEG_BIG))
    g_w = tot(jnp.where(row == g_row, group_p, 0.0))
    sel = (row < N_EXPERTS) & ((row // EXPERTS_PER_GROUP) == (g_row - N_EXPERTS))
    le = jnp.where(sel, logits, NEG_BIG)
    pe = jnp.where(sel, jnp.exp(le - top(le)), 0.0)
    e_p = pe / tot(pe)
    score = jnp.where(sel, e_p + bias, NEG_BIG)
    i1 = first_at(score)
    i2 = first_at(jnp.where(row == i1, NEG_BIG, score))
    oh1 = row == i1
    oh2 = row == i2
    w1 = tot(jnp.where(oh1, e_p, 0.0))
    w2 = tot(jnp.where(oh2, e_p, 0.0))
    wsum = w1 + w2

    ohs = (oh1 | oh2).astype(BF16)
    earlier = (lax.broadcasted_iota(I32, (tm, tm), 0) < lax.broadcasted_iota(I32, (tm, tm), 1)).astype(BF16)
    before = _dot(ohs, earlier) + run_ref[...]
    r1 = tot(jnp.where(oh1, before, 0.0))
    r2 = tot(jnp.where(oh2, before, 0.0))
    run_ref[...] = run_ref[...] + jnp.sum(ohs.astype(F32), axis=1, keepdims=True)
    cnt_ref[...] = run_ref[...]

    zero = jnp.zeros_like(w1)
    meta = jnp.concatenate([i1.astype(F32), i2.astype(F32), r1, r2, g_w * (w1 / wsum), g_w * (w2 / wsum),
                            zero, zero], axis=0)
    meta_ref[...] = meta
    eye8 = (lax.broadcasted_iota(I32, (8, LANES), 0) == lax.broadcasted_iota(I32, (8, LANES), 1)).astype(F32)
    gates_ref[...] = sum(_dot_tn(part, eye8) for part in _split3(meta))[:, 4:6]


def _outproj(x2d, o_swa, o_gdn, wo_bf16, gain, w_router, b_router):
    n = x2d.shape[0]
    tm = ROW_TILE
    row = lambda w: pl.BlockSpec((tm, w), lambda i: (i, 0))
    full = lambda a: pl.BlockSpec(a.shape, lambda i: (0,) * a.ndim)
    return pl.pallas_call(
        _outproj_kernel,
        grid=(n // tm,),
        in_specs=[row(D_MODEL), row(SWA_Q), row(GDN_V), full(wo_bf16), full(gain), full(w_router), full(b_router)],
        out_specs=[row(D_MODEL), row(D_MODEL), row(2), pl.BlockSpec((8, tm), lambda i: (0, i)),
                   pl.BlockSpec((LANES, 1), lambda i: (0, 0))],
        out_shape=[jax.ShapeDtypeStruct((n, D_MODEL), F32), jax.ShapeDtypeStruct((n, D_MODEL), F32),
                   jax.ShapeDtypeStruct((n, 2), F32), jax.ShapeDtypeStruct((8, n), F32),
                   jax.ShapeDtypeStruct((LANES, 1), F32)],
        scratch_shapes=[pltpu.VMEM((LANES, 1), F32)],
        compiler_params=pltpu.CompilerParams(dimension_semantics=("arbitrary",)),
        name="outproj_router",
    )(x2d, o_swa, o_gdn, wo_bf16, gain, w_router, b_router)


def _row_copy(src_ref, src_row, dst_ref, dst_row, sem):
    return pltpu.make_async_copy(src_ref.at[pl.ds(src_row, 1)], dst_ref.at[pl.ds(dst_row, 1)], sem)


def _scatter_kernel(dest0_ref, dest1_ref, xp_ref, xs_ref, out_ref, aidx_ref, sem, *, tiles_p, n_tokens):
    i = pl.program_id(0)
    tm = xp_ref.shape[0]

    def run(src_ref):
        def issue(g, carry):
            for u in range(DMA_UNROLL):
                r = g * DMA_UNROLL + u
                d0, d1 = dest0_ref[r], dest1_ref[r]
                _row_copy(src_ref, r, out_ref, d0, sem).start(priority=0)
                _row_copy(src_ref, r, out_ref, d1, sem).start(priority=1)
                aidx_ref[d0] = i * tm + r
                aidx_ref[d1] = n_tokens + i * tm + r
            return carry

        lax.fori_loop(0, tm // DMA_UNROLL, issue, 0)
        for _ in range(2):
            pltpu.make_async_copy(src_ref, out_ref.at[pl.ds(0, tm)], sem).wait()

    @pl.when(i < tiles_p)
    def _():
        run(xp_ref)

    @pl.when(i >= tiles_p)
    def _():
        run(xs_ref)


def _scatter(dest, xn_p, xn_s):
    tm = ROW_TILE
    tiles_p, tiles_s = xn_p.shape[0] // tm, xn_s.shape[0] // tm
    rows = 2 * (xn_p.shape[0] + xn_s.shape[0])
    idx = pl.BlockSpec((tm,), lambda i: (i,), memory_space=pltpu.SMEM)
    return pl.pallas_call(
        functools.partial(_scatter_kernel, tiles_p=tiles_p, n_tokens=rows // 2),
        grid=(tiles_p + tiles_s,),
        in_specs=[idx, idx,
                  pl.BlockSpec((tm, D_MODEL), lambda i: (jnp.minimum(i, tiles_p - 1), 0)),
                  pl.BlockSpec((tm, D_MODEL), lambda i: (jnp.maximum(i - tiles_p, 0), 0))],
        out_specs=[pl.BlockSpec(memory_space=pl.ANY), pl.BlockSpec(memory_space=pltpu.SMEM)],
        out_shape=[jax.ShapeDtypeStruct((rows, D_MODEL), F32), jax.ShapeDtypeStruct((rows,), I32)],
        scratch_shapes=[pltpu.SemaphoreType.DMA(())],
        compiler_params=pltpu.CompilerParams(dimension_semantics=("arbitrary",)),
        name="scatter_rows",
    )(dest[0], dest[1], xn_p, xn_s)


def _experts_kernel(blk_ref, exp_ref, lo_ref, hi_ref, first_ref, fresh_ref, pblk_ref,
                    aprev_ref, x_ref, wg_ref, wu_ref, wd_ref, ya_ref, wg_l, wu_l, wd_l, ybuf, sem,
                    *, n_items, n_blocks):
    j = pl.program_id(0)
    lo, hi = lo_ref[j], hi_ref[j]
    slot = lax.rem(blk_ref[j], 2)
    quarter = MOE_BLOCK // 4

    def wait_rows(s):
        pltpu.make_async_copy(ybuf.at[s], ya_ref.at[pl.ds(0, MOE_BLOCK)], sem.at[s]).wait()

    def send_rows(s, group):
        for row in range(group * quarter, (group + 1) * quarter):
            _row_copy(ybuf.at[s], row, ya_ref, aprev_ref[0, 0, row], sem.at[s]).start(priority=row % 2)

    @pl.when(fresh_ref[j] == 1)
    def _():
        wg_l[...] = wg_ref[0].astype(BF16)
        wu_l[...] = wu_ref[0].astype(BF16)
        wd_l[...] = wd_ref[0].astype(BF16)

    @pl.when((first_ref[j] == 1) & (blk_ref[j] >= 2))
    def _():
        wait_rows(slot)

    def item(is_first, send_prev):
        send = (lambda group: send_rows(1 - slot, group)) if send_prev else (lambda group: None)
        send(0)
        x = x_ref[...].astype(BF16)
        gate = _dot(x, wg_l[...])
        send(1)
        up = _dot(x, wu_l[...])
        hid = (gate * jax.nn.sigmoid(gate)) * up
        send(2)
        y = _dot(hid.astype(BF16), wd_l[...])
        send(3)
        r = lax.broadcasted_iota(I32, (MOE_BLOCK, 1), 0)
        mine = (r >= lo) & (r < hi)
        ybuf[slot] = jnp.where(mine, y, 0.0 if is_first else ybuf[slot])

    live = hi > lo
    pl.when(live & (first_ref[j] == 1) & (blk_ref[j] >= 1))(functools.partial(item, True, True))
    pl.when(live & (first_ref[j] == 1) & (blk_ref[j] == 0))(functools.partial(item, True, False))
    pl.when(live & (first_ref[j] == 0))(functools.partial(item, False, False))

    @pl.when(j == n_items - 1)
    def _():
        last_slot = (n_blocks - 1) % 2
        for group in range(4):
            send_rows(last_slot, group)
        wait_rows(1 - last_slot)
        wait_rows(last_slot)


def _experts(items, aidx, xs, w_gate, w_up, w_down):
    n_items = items[0].shape[0]
    n_blocks = xs.shape[0] // MOE_BLOCK
    assert n_items > n_blocks + N_EXPERTS - 1 and n_blocks >= 2
    xblk = pl.BlockSpec((MOE_BLOCK, D_MODEL), lambda j, blk, *_: (blk[j], 0))
    wspec = lambda a: pl.BlockSpec((1,) + a.shape[1:], lambda j, blk, ex, *_: (ex[j], 0, 0))
    return pl.pallas_call(
        functools.partial(_experts_kernel, n_items=n_items, n_blocks=n_blocks),
        grid_spec=pltpu.PrefetchScalarGridSpec(
            num_scalar_prefetch=len(items),
            grid=(n_items,),
            in_specs=[pl.BlockSpec((1, 1, MOE_BLOCK), lambda j, *pre: (pre[-1][j], 0, 0), memory_space=pltpu.SMEM),
                      xblk, wspec(w_gate), wspec(w_up), wspec(w_down)],
            out_specs=pl.BlockSpec(memory_space=pl.ANY),
            scratch_shapes=[pltpu.VMEM(w_gate.shape[1:], BF16), pltpu.VMEM(w_up.shape[1:], BF16),
                            pltpu.VMEM(w_down.shape[1:], BF16), pltpu.VMEM((2, MOE_BLOCK, D_MODEL), F32),
                            pltpu.SemaphoreType.DMA((2,))]),
        out_shape=jax.ShapeDtypeStruct(xs.shape, F32),
        compiler_params=pltpu.CompilerParams(dimension_semantics=("arbitrary",)),
        name="experts",
    )(*items, aidx.reshape(n_blocks, 1, MOE_BLOCK), xs, w_gate, w_up, w_down)


def _work_items(counts, total_rows):
    n_blocks = total_rows // MOE_BLOCK
    n_items = n_blocks + N_EXPERTS
    end = jnp.cumsum(counts)
    start = end - counts
    first_blk = start // MOE_BLOCK
    nb = jnp.where(counts > 0, (end - 1) // MOE_BLOCK - first_blk + 1, 0)
    item_end = jnp.cumsum(nb)
    used = item_end[-1]
    j = jnp.arange(n_items, dtype=I32)
    jj = jnp.minimum(j, used - 1)
    e = jnp.minimum(jnp.sum((item_end[None, :] <= jj[:, None]).astype(I32), axis=1), N_EXPERTS - 1)
    onehot = (e[:, None] == jnp.arange(N_EXPERTS, dtype=I32)[None, :]).astype(I32)
    pick = lambda a: jnp.sum(onehot * a[None, :], axis=1)
    blk = pick(first_blk) + (jj - (pick(item_end) - pick(nb)))
    lo = jnp.maximum(pick(start), blk * MOE_BLOCK) - blk * MOE_BLOCK
    hi = jnp.minimum(pick(end), (blk + 1) * MOE_BLOCK) - blk * MOE_BLOCK
    live = j < used
    lo = jnp.where(live, lo, 0)
    hi = jnp.where(live, hi, 0)
    prev_blk = jnp.concatenate([jnp.full((1,), -1, I32), blk[:-1]])
    first = (live & (blk != prev_blk)).astype(I32)
    prev_e = jnp.concatenate([jnp.full((1,), -1, I32), e[:-1]])
    fresh = (live & (e != prev_e)).astype(I32)
    pblk = jnp.where(j == n_items - 1, n_blocks - 1, jnp.maximum(blk - 1, 0))
    return (blk.astype(I32), e.astype(I32), lo.astype(I32), hi.astype(I32), first, fresh, pblk.astype(I32)), start


def _combine_kernel(gates_ref, h_ref, gf_ref, y0_ref, y1_ref, out_ref):
    gates = gates_ref[...]
    y = h_ref[...] + (y0_ref[...] * gates[:, 0:1] + y1_ref[...] * gates[:, 1:2])
    out_ref[...] = _rms(y, gf_ref[...])


def _combine(gates, h, gain, ya, first_token):
    n = h.shape[0]
    tm = COMBINE_TILE
    tile0 = first_token // tm
    slot_tiles = ya.shape[0] // 2 // tm
    row = lambda w: pl.BlockSpec((tm, w), lambda i: (i, 0))
    slot = lambda k: pl.BlockSpec((tm, D_MODEL), lambda i: (k * slot_tiles + tile0 + i, 0))
    return pl.pallas_call(
        _combine_kernel,
        grid=(n // tm,),
        in_specs=[row(2), row(D_MODEL), pl.BlockSpec(gain.shape, lambda i: (0, 0)), slot(0), slot(1)],
        out_specs=row(D_MODEL),
        out_shape=jax.ShapeDtypeStruct((n, D_MODEL), F32),
        compiler_params=pltpu.CompilerParams(dimension_semantics=("arbitrary",)),
        name="combine_norm",
    )(gates, h, gain, ya, ya)


def kernel(x_prompt, x_sample, cache_swa_k, cache_swa_v, state_gdn_conv, state_gdn, norm_mix, w_in, swa_sinks,
           gdn_conv_w, gdn_A_log, gdn_dt_bias, gdn_norm_w, w_out, norm_ffn, w_router_group, b_router_group,
           w_router_expert, b_router_expert, w_exp_gate, w_exp_up, w_exp_down, norm_final):
    depth = w_in.shape[0]
    assert depth == 1, "single trunk layer"
    bp, sp, _ = x_prompt.shape
    bs, ts, _ = x_sample.shape
    np_, ns = bp * sp, bs * ts
    l = 0

    w_in_b = w_in[l].astype(BF16)
    w_out_b = w_out[l].astype(BF16)
    g_mix = norm_mix[l].reshape(1, D_MODEL)
    g_ffn = norm_ffn[l].reshape(1, D_MODEL)
    g_fin = norm_final.reshape(1, D_MODEL)
    pad = LANES - N_EXPERTS - N_GROUPS
    w_router = jnp.concatenate([w_router_expert[l], w_router_group[l], jnp.zeros((D_MODEL, pad), F32)],
                               axis=1).astype(BF16).T
    b_router = jnp.concatenate([b_router_expert[l].reshape(-1), b_router_group[l],
                                jnp.zeros((pad,), F32)])[:, None]
    zeros8 = jnp.zeros((GDN_HEADS,), F32)
    alog16 = jnp.concatenate([gdn_A_log[l], zeros8])[None]
    dtb16 = jnp.concatenate([gdn_dt_bias[l], zeros8])[None]
    nw_group = jnp.tile(gdn_norm_w[l].reshape(1, HEAD_DIM), (1, GDN_GROUP))
    sinks = swa_sinks[l]

    q_p, k_p, v_p, c_p, z_p, ab_p = _inproj(x_prompt.reshape(np_, D_MODEL), g_mix, w_in_b)
    nblk = np_ // WINDOW
    k_p3 = k_p.reshape(nblk, WINDOW, SWA_KV)
    v_p3 = v_p.reshape(nblk, WINDOW, SWA_KV)
    (o_swa_p,) = _swa(sinks, q_p.reshape(nblk, WINDOW, SWA_Q), k_p3, v_p3, k_p3, v_p3,
                      bb=SWA_BLOCKS, blocks_per_seq=sp // WINDOW, emit_cache=False)
    c_p3 = c_p.reshape(bp, sp, GDN_CONV_CH)
    o_gdn_p, s_fin_p = _gdn(c_p3, jnp.zeros((bp, CONV_W - 1, GDN_CONV_CH), F32), z_p.reshape(bp, sp, GDN_V),
                            ab_p.reshape(bp, sp, 2 * GDN_HEADS),
                            jnp.zeros((bp, GDN_HEADS, HEAD_DIM, HEAD_DIM), F32),
                            gdn_conv_w[l], alog16, dtb16, nw_group, chunk=GDN_CHUNK, seq_block=bp)
    h_p, xn_p, gates_p, meta_p, cnt_p = _outproj(
        x_prompt.reshape(np_, D_MODEL), o_swa_p.reshape(np_, SWA_Q), o_gdn_p.reshape(np_, GDN_V),
        w_out_b, g_ffn, w_router, b_router)

    q_s, k_s, v_s, c_s, z_s, ab_s = _inproj(x_sample.reshape(ns, D_MODEL), g_mix, w_in_b)
    o_swa_s, kcache_s, vcache_s = _swa(
        sinks, q_s.reshape(bs, ts, SWA_Q), k_s.reshape(bs, ts, SWA_KV), v_s.reshape(bs, ts, SWA_KV),
        cache_swa_k[l].reshape(bs, WINDOW, SWA_KV), cache_swa_v[l].reshape(bs, WINDOW, SWA_KV),
        bb=16, blocks_per_seq=None, emit_cache=True)
    c_s3 = c_s.reshape(bs, ts, GDN_CONV_CH)
    o_gdn_s, s_fin_s = _gdn(c_s3, state_gdn_conv[l], z_s.reshape(bs, ts, GDN_V),
                            ab_s.reshape(bs, ts, 2 * GDN_HEADS), state_gdn[l],
                            gdn_conv_w[l], alog16, dtb16, nw_group, chunk=ts, seq_block=16)
    h_s, xn_s, gates_s, meta_s, cnt_s = _outproj(
        x_sample.reshape(ns, D_MODEL), o_swa_s.reshape(ns, SWA_Q), o_gdn_s.reshape(ns, GDN_V),
        w_out_b, g_ffn, w_router, b_router)

    cnt_p_i = cnt_p[:N_EXPERTS, 0].astype(I32)
    cnt_s_i = cnt_s[:N_EXPERTS, 0].astype(I32)
    items, start = _work_items(cnt_p_i + cnt_s_i, 2 * (np_ + ns))
    expert_ids = jnp.arange(N_EXPERTS, dtype=I32)
    lookup = lambda table, ids: jnp.sum(jnp.where(ids[..., None] == expert_ids, table, 0), axis=-1)
    dest_p = lookup(start, meta_p[0:2].astype(I32)) + meta_p[2:4].astype(I32)
    dest_s = lookup(start + cnt_p_i, meta_s[0:2].astype(I32)) + meta_s[2:4].astype(I32)
    xs, aidx = _scatter(jnp.concatenate([dest_p, dest_s], axis=1), xn_p, xn_s)
    ya = _experts(items, aidx, xs, w_exp_gate[l], w_exp_up[l], w_exp_down[l])
    y_p = _combine(gates_p, h_p, g_fin, ya, 0)
    y_s = _combine(gates_s, h_s, g_fin, ya, np_)

    kv5 = lambda a, b: a.reshape(b, -1, SWA_KV_HEADS, HEAD_DIM)[None]
    return (y_p.reshape(bp, sp, D_MODEL), y_s.reshape(bs, ts, D_MODEL),
            kv5(k_p.reshape(bp, sp, SWA_KV)[:, -WINDOW:], bp), kv5(v_p.reshape(bp, sp, SWA_KV)[:, -WINDOW:], bp),
            kv5(kcache_s, bs), kv5(vcache_s, bs),
            c_p3[:, -(CONV_W - 1):][None], c_s3[:, -(CONV_W - 1):][None],
            s_fin_p[None], s_fin_s[None])
```

```python
import functools

import jax
import jax.numpy as jnp
from jax import lax
from jax.experimental import pallas as pl
from jax.experimental.pallas import tpu as pltpu

F32 = jnp.float32
BF16 = jnp.bfloat16
I32 = jnp.int32

D_MODEL = 1024
HEAD_DIM = 64
SWA_HEADS = 8
GDN_HEADS = 8
SWA_KV_HEADS = 2
GQA_GROUP = SWA_HEADS // SWA_KV_HEADS
WINDOW = 128
ATTN_SCALE = HEAD_DIM ** -0.5
CONV_W = 4
N_GROUPS = 8
EXPERTS_PER_GROUP = 8
N_EXPERTS = 64
D_EXPERT = 256
EPS = 1e-6

SWA_Q = SWA_HEADS * HEAD_DIM
SWA_KV = SWA_KV_HEADS * HEAD_DIM
GDN_QK = GDN_HEADS * HEAD_DIM
GDN_V = GDN_HEADS * HEAD_DIM
GDN_CONV_CH = 2 * GDN_QK + GDN_V
D_MIX = SWA_Q + GDN_V
D_IN = SWA_Q + 2 * SWA_KV + GDN_CONV_CH + GDN_V + 2 * GDN_HEADS
COL_K = SWA_Q
COL_V = COL_K + SWA_KV
COL_C = COL_V + SWA_KV
COL_Z = COL_C + GDN_CONV_CH
COL_AB = COL_Z + GDN_V

LANES = 128
NEG_BIG = -1e30
ROW_TILE = 512
INPROJ_TILE = 1024
MOE_BLOCK = 256
COMBINE_TILE = 512
DMA_UNROLL = 8
SWA_BLOCKS = 4
GDN_CHUNK = 64
GDN_TILE = 256
GDN_GROUP = 4
GDN_GROUP_W = GDN_GROUP * HEAD_DIM


def _rms(x, g):
    return x * lax.rsqrt(jnp.mean(x * x, axis=-1, keepdims=True) + EPS) * g


def _dot(a, b):
    return jnp.dot(a, b, preferred_element_type=F32)


def _dot_nt(a, b):
    return lax.dot_general(a, b, (((1,), (1,)), ((), ())), preferred_element_type=F32)


def _dot_tn(a, b):
    return lax.dot_general(a, b, (((0,), (0,)), ((), ())), preferred_element_type=F32)


def _split3(x):
    p1 = x.astype(BF16).astype(F32)
    r = x - p1
    p2 = r.astype(BF16).astype(F32)
    p3 = (r - p2).astype(BF16).astype(F32)
    return p1, p2, p3


def _inproj_kernel(x_ref, g_ref, w_ref, q_ref, k_ref, v_ref, c_ref, z_ref, ab_ref):
    x = x_ref[...]
    xb = _rms(x, g_ref[...]).astype(BF16)
    q_ref[...] = _dot(xb, w_ref[:, 0:COL_K])
    k_ref[...] = _dot(xb, w_ref[:, COL_K:COL_V])
    v_ref[...] = _dot(xb, w_ref[:, COL_V:COL_C])
    c_ref[...] = _dot(xb, w_ref[:, COL_C:COL_Z])
    z_ref[...] = _dot(xb, w_ref[:, COL_Z:COL_AB])
    ab_ref[...] = _dot(xb, w_ref[:, COL_AB:D_IN])


def _inproj(x2d, gain, w_bf16):
    n = x2d.shape[0]
    tm = INPROJ_TILE
    row = lambda w: pl.BlockSpec((tm, w), lambda i: (i, 0))
    full = lambda a: pl.BlockSpec(a.shape, lambda i: (0,) * a.ndim)
    widths = (SWA_Q, SWA_KV, SWA_KV, GDN_CONV_CH, GDN_V, 2 * GDN_HEADS)
    return pl.pallas_call(
        _inproj_kernel,
        grid=(n // tm,),
        in_specs=[row(D_MODEL), full(gain), full(w_bf16)],
        out_specs=[row(w) for w in widths],
        out_shape=[jax.ShapeDtypeStruct((n, w), F32) for w in widths],
        compiler_params=pltpu.CompilerParams(dimension_semantics=("arbitrary",)),
        name="inproj",
    )(x2d, gain, w_bf16)


def _swa_kernel(sink_ref, q_ref, kc_ref, vc_ref, kp_ref, vp_ref, o_ref, *cache_refs,
                bb, t, blocks_per_seq, emit_cache):
    rows = GQA_GROUP * t
    ri = lax.broadcasted_iota(I32, (rows, 1), 0)
    qi = lax.rem(ri, t)
    gi = ri // t
    if emit_cache:
        kj = lax.broadcasted_iota(I32, (rows, WINDOW), 1)
        mask = kj <= qi + (WINDOW - t)
        mask_first = mask
    else:
        kj = lax.broadcasted_iota(I32, (rows, WINDOW + t), 1)
        mask = (kj <= qi + WINDOW) & (kj > qi)
        has_prev = lax.rem(pl.program_id(0) * bb, blocks_per_seq) != 0
        mask_first = mask & (has_prev | (kj >= WINDOW))
    sinks = []
    for h in range(SWA_KV_HEADS):
        sink = jnp.zeros((rows, 1), F32)
        for g in range(GQA_GROUP):
            sink = jnp.where(gi == g, sink_ref[GQA_GROUP * h + g], sink)
        sinks.append(sink)
    chains = [(b, h) for b in range(bb) for h in range(SWA_KV_HEADS)]
    scores, values = [], []
    for b, h in chains:
        hs = slice(h * HEAD_DIM, (h + 1) * HEAD_DIM)
        kc, vc = kc_ref[b, :, hs], vc_ref[b, :, hs]
        if emit_cache:
            kp, vp = kp_ref[b, :, hs], vp_ref[b, :, hs]
            keys = jnp.concatenate([kp[t:], kc], axis=0)
            vals = jnp.concatenate([vp[t:], vc], axis=0)
            cache_refs[0][b, :, hs] = keys
            cache_refs[1][b, :, hs] = vals
        else:
            kp, vp = (kp_ref[0, :, hs], vp_ref[0, :, hs]) if b == 0 else (kc_ref[b - 1, :, hs], vc_ref[b - 1, :, hs])
            keys = jnp.concatenate([kp, kc], axis=0)
            vals = jnp.concatenate([vp, vc], axis=0)
        q4 = jnp.concatenate(
            [q_ref[b, :, (GQA_GROUP * h + g) * HEAD_DIM:(GQA_GROUP * h + g + 1) * HEAD_DIM]
             for g in range(GQA_GROUP)], axis=0)
        scores.append(_dot_nt(q4.astype(BF16), keys.astype(BF16)))
        values.append(vals.astype(BF16))
    probs, dens = [], []
    for (b, h), s in zip(chains, scores):
        s = jnp.where(mask_first if b == 0 else mask, s * ATTN_SCALE, NEG_BIG)
        m = jnp.maximum(jnp.max(s, axis=-1, keepdims=True), sinks[h])
        p = jnp.exp(s - m)
        dens.append(jnp.sum(p, axis=-1, keepdims=True) + jnp.exp(sinks[h] - m))
        probs.append(p.astype(BF16))
    outs = [_dot(p, v) / den for p, v, den in zip(probs, values, dens)]
    for b in range(bb):
        o_ref[b] = jnp.concatenate([outs[b * SWA_KV_HEADS + h][g * t:(g + 1) * t]
                                    for h in range(SWA_KV_HEADS) for g in range(GQA_GROUP)], axis=-1)


def _swa(sinks, q3, k3, v3, kprev3, vprev3, *, bb, blocks_per_seq, emit_cache):
    nb, t, _ = q3.shape
    cur = lambda w: pl.BlockSpec((bb, t, w), lambda i: (i, 0, 0))
    if emit_cache:
        prev = pl.BlockSpec((bb, WINDOW, SWA_KV), lambda i: (i, 0, 0))
    else:
        assert blocks_per_seq % bb == 0
        prev = pl.BlockSpec((1, WINDOW, SWA_KV), lambda i: (jnp.maximum(i * bb - 1, 0), 0, 0))
    out_specs = [cur(SWA_Q)]
    out_shape = [jax.ShapeDtypeStruct((nb, t, SWA_Q), F32)]
    if emit_cache:
        cache = pl.BlockSpec((bb, WINDOW, SWA_KV), lambda i: (i, 0, 0))
        out_specs += [cache, cache]
        out_shape += [jax.ShapeDtypeStruct((nb, WINDOW, SWA_KV), F32)] * 2
    return pl.pallas_call(
        functools.partial(_swa_kernel, bb=bb, t=t, blocks_per_seq=blocks_per_seq, emit_cache=emit_cache),
        grid=(nb // bb,),
        in_specs=[pl.BlockSpec(memory_space=pltpu.SMEM), cur(SWA_Q), cur(SWA_KV), cur(SWA_KV), prev, prev],
        out_specs=out_specs,
        out_shape=out_shape,
        compiler_params=pltpu.CompilerParams(dimension_semantics=("arbitrary",)),
        name="swa_cache" if emit_cache else "swa_band",
    )(sinks, q3, k3, v3, kprev3, vprev3)


def _gdn_prep_kernel(c_ref, hist_ref, ab_ref, cw_ref, alog_ref, dtb_ref,
                     u_ref, w_ref, qd_ref, kd_ref, qk_ref, gt_ref, cbuf_ref, *, chunk):
    sb, r, _ = c_ref.shape
    tp = sb * r
    cn = chunk
    low = w_ref.dtype
    hist_rows = CONV_W - 1

    @pl.when(pl.program_id(1) == 0)
    def _():
        cbuf_ref[:, 8 - hist_rows:8, :] = hist_ref[...]

    cbuf_ref[:, 8:8 + r, :] = c_ref[...]
    conv = cbuf_ref[:, 8 - hist_rows:8 - hist_rows + r, :] * cw_ref[0:1, :]
    for i in range(1, CONV_W):
        conv = conv + cbuf_ref[:, 8 - hist_rows + i:8 - hist_rows + i + r, :] * cw_ref[i:i + 1, :]
    tail = cbuf_ref[:, 8 + r - hist_rows:8 + r, :]
    cbuf_ref[:, 8 - hist_rows:8, :] = tail
    conv = (conv * jax.nn.sigmoid(conv)).reshape(tp, GDN_CONV_CH)

    ab = ab_ref[...].reshape(tp, 2 * GDN_HEADS)
    is_g = lax.broadcasted_iota(I32, (1, 2 * GDN_HEADS), 1) < GDN_HEADS
    g = jnp.where(is_g, -jnp.exp(alog_ref[...]) * jax.nn.softplus(ab + dtb_ref[...]), 0.0)
    beta = jax.nn.sigmoid(ab)

    ri = lax.broadcasted_iota(I32, (tp, tp), 0)
    ci = lax.broadcasted_iota(I32, (tp, tp), 1)
    same = (ri // cn) == (ci // cn)
    causal = same & (ri >= ci)
    strict = same & (ri > ci)
    eye = (ri == ci).astype(F32)
    stack = jnp.concatenate([causal.astype(BF16), same.astype(BF16)], axis=0)
    both = sum(_dot(stack, p.astype(BF16)) for p in _split3(g))
    gc, gl = both[:tp], both[tp:]
    e16 = lax.broadcasted_iota(I32, (2 * GDN_HEADS, 2 * GDN_HEADS), 0)
    eye16 = (e16 == lax.broadcasted_iota(I32, (2 * GDN_HEADS, 2 * GDN_HEADS), 1)).astype(BF16)
    gc_row = sum(_dot_nt(eye16, p.astype(BF16)) for p in _split3(gc))
    gt_ref[...] = jnp.exp(gl)
    fold = (lax.broadcasted_iota(I32, (tp, cn), 0) % cn == lax.broadcasted_iota(I32, (tp, cn), 1)).astype(low)

    levels = cn.bit_length() - 2
    heads = range(GDN_HEADS)
    lanes_of = lambda i, h: slice(i * GDN_QK + h * HEAD_DIM, i * GDN_QK + (h + 1) * HEAD_DIM)
    qs = [conv[:, lanes_of(0, h)] for h in heads]
    ks = [conv[:, lanes_of(1, h)] for h in heads]
    qs = [q * lax.rsqrt(jnp.sum(q * q, axis=-1, keepdims=True) + EPS) * (HEAD_DIM ** -0.5) for q in qs]
    ks = [k * lax.rsqrt(jnp.sum(k * k, axis=-1, keepdims=True) + EPS) for k in ks]
    g_cs = [gc[:, h:h + 1] for h in heads]
    b_hs = [beta[:, GDN_HEADS + h:GDN_HEADS + h + 1] for h in heads]
    decays = [jnp.exp(jnp.where(causal, g_cs[h] - gc_row[h:h + 1, :], NEG_BIG)) for h in heads]
    egs = [jnp.exp(g_c) for g_c in g_cs]
    kbs = [k * b_h for k, b_h in zip(ks, b_hs)]
    k_ls = [k.astype(low) for k in ks]
    lmats = [jnp.where(strict, _dot_nt(kb.astype(low), k_l) * decay, 0.0) for kb, k_l, decay in zip(kbs, k_ls, decays)]
    qks = [_dot((_dot_nt(q.astype(low), k_l) * decay).astype(low), fold) for q, k_l, decay in zip(qs, k_ls, decays)]
    xs = [eye - lmat for lmat in lmats]
    ps = [_dot(lmat.astype(low), lmat.astype(low)) for lmat in lmats]
    for lev in range(1, levels + 1):
        p_ls = [p.astype(low) for p in ps]
        if lev < levels:
            xp = [_dot(jnp.concatenate([x.astype(low), p_l], axis=0), p_l) for x, p_l in zip(xs, p_ls)]
            xs = [x + m[:tp] for x, m in zip(xs, xp)]
            ps = [m[tp:] for m in xp]
        else:
            xs = [x + _dot(x.astype(low), p_l) for x, p_l in zip(xs, p_ls)]
    sols = [_dot(xs[h].astype(low),
                 jnp.concatenate([conv[:, lanes_of(2, h)] * b_hs[h], kbs[h] * egs[h]], axis=-1).astype(low))
            for h in heads]
    u_ref[...] = jnp.concatenate([sol[:, :HEAD_DIM] for sol in sols], axis=-1)
    w_ref[...] = jnp.concatenate([sol[:, HEAD_DIM:] for sol in sols], axis=-1).astype(low)
    qd_ref[...] = jnp.concatenate([q * eg for q, eg in zip(qs, egs)], axis=-1).astype(low)
    kd_ref[...] = jnp.concatenate([ks[h] * jnp.exp(gl[:, h:h + 1] - g_cs[h]) for h in heads], axis=-1).astype(low)
    qk_ref[...] = jnp.concatenate(qks, axis=-1).astype(low)


def _gdn_prep_pair_kernel(c_ref, hist_ref, ab_ref, cw_ref, alog_ref, dtb_ref,
                          u_ref, w_ref, qd_ref, kd_ref, qk_ref, gt_ref, cbuf_ref):
    _, r, _ = c_ref.shape
    tp = r
    cn = HEAD_DIM
    pair_w = 2 * HEAD_DIM
    hist_rows = CONV_W - 1
    sublanes = 8

    @pl.when(pl.program_id(1) == 0)
    def _():
        cbuf_ref[...] = jnp.zeros_like(cbuf_ref)
        cbuf_ref[:, sublanes - hist_rows:sublanes, :] = hist_ref[...]

    x3 = c_ref[0].reshape(tp // sublanes, sublanes, GDN_CONV_CH)
    before = cbuf_ref[...]
    sub = lax.broadcasted_iota(I32, (1, sublanes, 1), 1)
    conv3 = x3 * cw_ref[CONV_W - 1:CONV_W, :]
    for s in range(1, CONV_W):
        rx = pltpu.roll(x3, s, axis=1)
        rp = jnp.concatenate([pltpu.roll(before, s, axis=1), rx[:-1]], axis=0)
        conv3 = conv3 + jnp.where(sub < s, rp, rx) * cw_ref[CONV_W - 1 - s:CONV_W - s, :]
    cbuf_ref[...] = x3[tp // sublanes - 1:]
    conv = (conv3 * jax.nn.sigmoid(conv3)).reshape(tp, GDN_CONV_CH)

    ab = ab_ref[...].reshape(tp, 2 * GDN_HEADS)
    is_g = lax.broadcasted_iota(I32, (1, 2 * GDN_HEADS), 1) < GDN_HEADS
    g = jnp.where(is_g, -jnp.exp(alog_ref[...]) * jax.nn.softplus(ab + dtb_ref[...]), 0.0)
    beta = jax.nn.sigmoid(ab)

    ri = lax.broadcasted_iota(I32, (tp, tp), 0)
    ci = lax.broadcasted_iota(I32, (tp, tp), 1)
    same = (ri // cn) == (ci // cn)
    same_l = same.astype(BF16)
    stack = jnp.concatenate([(same & (ri >= ci)).astype(BF16), same_l], axis=0)
    both = sum(_dot(stack, p.astype(BF16)) for p in _split3(g))
    gc, gl = both[:tp], both[tp:]
    gt_ref[...] = jnp.exp(gl)

    lane = lax.broadcasted_iota(I32, (tp, pair_w), 1)
    c_in = lax.broadcasted_iota(I32, (tp, pair_w), 0) % cn
    j_in = lane % cn
    left = lane < cn
    causal = c_in >= j_in
    strict = c_in > j_in
    diag = c_in == j_in
    eye = diag.astype(F32)
    bdmask = ((lax.broadcasted_iota(I32, (pair_w, pair_w), 0) // cn)
              == (lax.broadcasted_iota(I32, (pair_w, pair_w), 1) // cn))
    ones_bd = bdmask.astype(BF16)

    def bd(m):
        return jnp.where(bdmask, jnp.concatenate([m, m], axis=0), 0.0).astype(BF16)

    def head_sum(x):
        hi = x.astype(BF16)
        lo = (x - hi.astype(F32)).astype(BF16)
        return _dot(hi, ones_bd) + _dot(lo, ones_bd)

    levels = cn.bit_length() - 2
    chunks = [slice(n * cn, (n + 1) * cn) for n in range(tp // cn)]
    n_pairs = GDN_HEADS // 2
    pairs = range(n_pairs)
    pick = lambda m, off, p: jnp.where(left, m[:, off + 2 * p:off + 2 * p + 1], m[:, off + 2 * p + 1:off + 2 * p + 2])
    third = lambda i, p: conv[:, i * GDN_QK + p * pair_w:i * GDN_QK + (p + 1) * pair_w]
    qs = [third(0, p) for p in pairs]
    ks = [third(1, p) for p in pairs]
    qs = [q * lax.rsqrt(head_sum(q * q) + EPS) * (HEAD_DIM ** -0.5) for q in qs]
    ks = [k * lax.rsqrt(head_sum(k * k) + EPS) for k in ks]
    gcps = [pick(gc, 0, p) for p in pairs]
    rowms = [sum(_dot(same_l, part.astype(BF16)) for part in _split3(jnp.where(diag, gcp, 0.0))) for gcp in gcps]
    decays = [jnp.exp(jnp.where(causal, gcp - rowm, NEG_BIG)) for gcp, rowm in zip(gcps, rowms)]
    lmats, vbs, kbegs = [], [], []
    for p in pairs:
        ls = slice(p * pair_w, (p + 1) * pair_w)
        q, k, gcp, decay = qs[p], ks[p], gcps[p], decays[p]
        bp = pick(beta, GDN_HEADS, p)
        eg = jnp.exp(gcp)
        kb = k * bp
        vbs.append(third(2, p) * bp)
        kbegs.append(kb * eg)
        qd_ref[:, ls] = (q * eg).astype(BF16)
        kd_ref[:, ls] = (k * jnp.exp(pick(gl, 0, p) - gcp)).astype(BF16)
        q_l, kb_l = q.astype(BF16), kb.astype(BF16)
        kbd = [bd(k[rs]) for rs in chunks]
        kk = jnp.concatenate([_dot_nt(kb_l[rs], kbd[n]) for n, rs in enumerate(chunks)], axis=0)
        qk = jnp.concatenate([_dot_nt(q_l[rs], kbd[n]) for n, rs in enumerate(chunks)], axis=0)
        qk_ref[:, ls] = (qk * decay).astype(BF16)
        lmats.append(jnp.where(strict, kk * decay, 0.0))
    bodies = [(p, rs) for p in range(n_pairs) for rs in chunks]
    xs = [eye[rs] - lmats[p][rs] for p, rs in bodies]
    ps = [_dot(lmats[p][rs].astype(BF16), bd(lmats[p][rs])) for p, rs in bodies]
    for lev in range(1, levels + 1):
        pbd = [bd(pm) for pm in ps]
        if lev < levels:
            xp = [_dot(jnp.concatenate([xm, pm], axis=0).astype(BF16), wm) for xm, pm, wm in zip(xs, ps, pbd)]
            xs = [xm + m[:cn] for xm, m in zip(xs, xp)]
            ps = [m[cn:] for m in xp]
        else:
            xs = [xm + _dot(xm.astype(BF16), wm) for xm, wm in zip(xs, pbd)]
    x_l = [xm.astype(BF16) for xm in xs]
    us = [_dot(xm, bd(vbs[p][rs])) for xm, (p, rs) in zip(x_l, bodies)]
    ws = [_dot(xm, bd(kbegs[p][rs])) for xm, (p, rs) in zip(x_l, bodies)]
    nc = len(chunks)
    for p in range(n_pairs):
        ls = slice(p * pair_w, (p + 1) * pair_w)
        u_ref[:, ls] = jnp.concatenate(us[p * nc:(p + 1) * nc], axis=0)
        w_ref[:, ls] = jnp.concatenate(ws[p * nc:(p + 1) * nc], axis=0).astype(BF16)


def _gdn_scan_kernel(u_ref, w_ref, qd_ref, kd_ref, qk_ref, gt_ref, z_ref, s0_ref, nw_ref,
                     o_ref, sfin_ref, sbd_ref, *, chunk, n_chunks):
    bb = u_ref.shape[0]
    cn = chunk
    low = w_ref.dtype
    gw = GDN_GROUP_W
    ni = pl.program_id(1)

    @pl.when(ni == 0)
    def _():
        sbd_ref[...] = jnp.zeros_like(sbd_ref)
        for b in range(bb):
            for h in range(GDN_HEADS):
                gi, hh = divmod(h, GDN_GROUP)
                ds = slice(hh * HEAD_DIM, (hh + 1) * HEAD_DIM)
                sbd_ref[b, gi, ds, ds] = s0_ref[b, h]

    bdmask = ((lax.broadcasted_iota(I32, (gw, gw), 0) // HEAD_DIM)
              == (lax.broadcasted_iota(I32, (gw, gw), 1) // HEAD_DIM))
    ones_bd = bdmask.astype(BF16)
    vmask = ((lax.broadcasted_iota(I32, (GDN_GROUP * cn, gw), 0) // cn)
             == (lax.broadcasted_iota(I32, (GDN_GROUP * cn, gw), 1) // HEAD_DIM))
    e_row = lax.broadcasted_iota(I32, (2 * GDN_HEADS, gw), 0)
    e_col = lax.broadcasted_iota(I32, (2 * GDN_HEADS, gw), 1) // HEAD_DIM
    chains = [(b, gi) for b in range(bb) for gi in range(GDN_HEADS // GDN_GROUP)]
    lanes = lambda gi: slice(gi * gw, (gi + 1) * gw)
    states = [sbd_ref[b, gi] for b, gi in chains]
    states_l = [s.astype(low) for s in states]
    wq_s = [_dot(jnp.concatenate([w_ref[b, :, lanes(gi)], qd_ref[b, :, lanes(gi)]], axis=0), s_l)
            for (b, gi), s_l in zip(chains, states_l)]
    v_new = [u_ref[b, :, lanes(gi)] - m[:cn] for (b, gi), m in zip(chains, wq_s)]
    q_s = [m[cn:] for m in wq_s]
    v_l = [v.astype(low) for v in v_new]
    outs = []
    for (b, gi), s, v, qs in zip(chains, states, v_l, q_s):
        vbd = jnp.where(vmask, jnp.concatenate([v] * GDN_GROUP, axis=0), jnp.zeros((), low))
        outs.append(qs + _dot(qk_ref[b, :, gi * GDN_GROUP * cn:(gi + 1) * GDN_GROUP * cn], vbd))
        upd = _dot_tn(kd_ref[b, :, lanes(gi)], v)
        expand = (e_row == e_col + gi * GDN_GROUP).astype(BF16)
        gte = sum(_dot(p.astype(BF16), expand) for p in _split3(gt_ref[b, 0:8, :]))[0:1]
        sbd_ref[b, gi] = s * gte + jnp.where(bdmask, upd, 0.0)
    for (b, gi), o in zip(chains, outs):
        o2 = o * o
        hi = o2.astype(BF16)
        lo = (o2 - hi.astype(F32)).astype(BF16)
        if cn % 16 == 0:
            sums = _dot(jnp.concatenate([hi, lo], axis=0), ones_bd)
            ms = (sums[:cn] + sums[cn:]) * (1.0 / HEAD_DIM)
        else:
            ms = (_dot(hi, ones_bd) + _dot(lo, ones_bd)) * (1.0 / HEAD_DIM)
        zg = z_ref[b, :, lanes(gi)]
        o_ref[b, :, lanes(gi)] = o * lax.rsqrt(ms + EPS) * nw_ref[...] * (zg * jax.nn.sigmoid(zg))

    @pl.when(ni == n_chunks - 1)
    def _():
        for b in range(bb):
            for h in range(GDN_HEADS):
                gi, hh = divmod(h, GDN_GROUP)
                ds = slice(hh * HEAD_DIM, (hh + 1) * HEAD_DIM)
                sfin_ref[b, h] = sbd_ref[b, gi, ds, ds]


def _gdn(c3, hist, z3, ab3, s0, conv_w, alog16, dtb16, nw_group, *, chunk, seq_block):
    nseq, t, _ = c3.shape
    n = nseq * t
    sb, r = (1, GDN_TILE) if t >= GDN_TILE else (GDN_TILE // t, t)
    tiles = t // r
    low = BF16 if chunk >= 16 else F32
    blk = lambda w: pl.BlockSpec((sb, r, w), lambda s, i: (s, i, 0))
    full = lambda a: pl.BlockSpec(a.shape, lambda s, i: (0,) * a.ndim)
    flat = lambda w: pl.BlockSpec((GDN_TILE, w), lambda s, i: (s * tiles + i, 0))
    widths = (GDN_V, GDN_V, GDN_QK, GDN_QK, GDN_HEADS * chunk, 2 * GDN_HEADS)
    dtypes = (F32, low, low, low, low, F32)
    prep_out = dict(out_specs=[flat(wd) for wd in widths],
                    out_shape=[jax.ShapeDtypeStruct((n, wd), dt) for wd, dt in zip(widths, dtypes)],
                    compiler_params=pltpu.CompilerParams(dimension_semantics=("arbitrary", "arbitrary")))
    lane_dense = chunk == HEAD_DIM and sb == 1
    u, w, qd, kd, qk, gt = pl.pallas_call(
        _gdn_prep_pair_kernel if lane_dense else functools.partial(_gdn_prep_kernel, chunk=chunk),
        grid=(nseq // sb, tiles),
        in_specs=[blk(GDN_CONV_CH), pl.BlockSpec((sb, CONV_W - 1, GDN_CONV_CH), lambda s, i: (s, 0, 0)),
                  blk(2 * GDN_HEADS), full(conv_w), full(alog16), full(dtb16)],
        scratch_shapes=[pltpu.VMEM((1, 8, GDN_CONV_CH) if lane_dense else (sb, 8 + r, GDN_CONV_CH), F32)],
        name="gdn_prep_pair" if lane_dense else "gdn_prep", **prep_out)(c3, hist, ab3, conv_w, alog16, dtb16)

    n_chunks = t // chunk
    tok = lambda wd: pl.BlockSpec((seq_block, chunk, wd), lambda s, c: (s, c, 0))
    per_seq = pl.BlockSpec((seq_block,) + s0.shape[1:], lambda s, c: (s, 0, 0, 0))
    seq3 = lambda a: a.reshape(nseq, t, a.shape[-1])
    return pl.pallas_call(
        functools.partial(_gdn_scan_kernel, chunk=chunk, n_chunks=n_chunks),
        grid=(nseq // seq_block, n_chunks),
        in_specs=[tok(wd) for wd in widths] + [tok(GDN_V), per_seq,
                                               pl.BlockSpec(nw_group.shape, lambda s, c: (0, 0))],
        out_specs=[tok(GDN_V), per_seq],
        out_shape=[jax.ShapeDtypeStruct((nseq, t, GDN_V), F32), jax.ShapeDtypeStruct(s0.shape, F32)],
        scratch_shapes=[pltpu.VMEM((seq_block, GDN_HEADS // GDN_GROUP, GDN_GROUP_W, GDN_GROUP_W), F32)],
        compiler_params=pltpu.CompilerParams(dimension_semantics=("arbitrary", "arbitrary")),
        name="gdn_scan",
    )(seq3(u), seq3(w), seq3(qd), seq3(kd), seq3(qk), seq3(gt), z3, s0, nw_group)


def _outproj_kernel(x_ref, osw_ref, ogd_ref, wo_ref, gf_ref, wr_ref, br_ref,
                    h_ref, xn_ref, gates_ref, meta_ref, cnt_ref, run_ref):
    i = pl.program_id(0)
    tm = x_ref.shape[0]
    rows = wr_ref.shape[0]

    @pl.when(i == 0)
    def _():
        run_ref[...] = jnp.zeros_like(run_ref)

    h = (x_ref[...] + _dot(osw_ref[...].astype(BF16), wo_ref[0:SWA_Q, :])
         + _dot(ogd_ref[...].astype(BF16), wo_ref[SWA_Q:D_MIX, :]))
    h_ref[...] = h
    xn = _rms(h, gf_ref[...])
    xn_ref[...] = xn
    logits = _dot_nt(wr_ref[...], xn.astype(BF16))

    row = lax.broadcasted_iota(I32, (rows, tm), 0)
    bias = br_ref[...]
    top = lambda v: jnp.max(v, axis=0, keepdims=True)
    tot = lambda v: jnp.sum(v, axis=0, keepdims=True)
    first_at = lambda v: jnp.min(jnp.where(v == top(v), row, 2 * rows), axis=0, keepdims=True)
    is_g = (row >= N_EXPERTS) & (row < N_EXPERTS + N_GROUPS)
    lg = jnp.where(is_g, logits, NEG_BIG)
    pg = jnp.where(is_g, jnp.exp(lg - top(lg)), 0.0)
    group_p = pg / tot(pg)
    g_row = first_at(jnp.where(is_g, group_p + bias, NEG_BIG))
    g_w = tot(jnp.where(row == g_row, group_p, 0.0))
    sel = (row < N_EXPERTS) & ((row // EXPERTS_PER_GROUP) == (g_row - N_EXPERTS))
    le = jnp.where(sel, logits, NEG_BIG)
    pe = jnp.where(sel, jnp.exp(le - top(le)), 0.0)
    e_p = pe / tot(pe)
    score = jnp.where(sel, e_p + bias, NEG_BIG)
    i1 = first_at(score)
    i2 = first_at(jnp.where(row == i1, NEG_BIG, score))
    oh1 = row == i1
    oh2 = row == i2
    w1 = tot(jnp.where(oh1, e_p, 0.0))
    w2 = tot(jnp.where(oh2, e_p, 0.0))
    wsum = w1 + w2

    ohs = (oh1 | oh2).astype(BF16)
    earlier = (lax.broadcasted_iota(I32, (tm, tm), 0) < lax.broadcasted_iota(I32, (tm, tm), 1)).astype(BF16)
    before = _dot(ohs, earlier) + run_ref[...]
    r1 = tot(jnp.where(oh1, before, 0.0))
    r2 = tot(jnp.where(oh2, before, 0.0))
    run_ref[...] = run_ref[...] + jnp.sum(ohs.astype(F32), axis=1, keepdims=True)
    cnt_ref[...] = run_ref[...]

    zero = jnp.zeros_like(w1)
    meta = jnp.concatenate([i1.astype(F32), i2.astype(F32), r1, r2, g_w * (w1 / wsum), g_w * (w2 / wsum),
                            zero, zero], axis=0)
    meta_ref[...] = meta
    eye8 = (lax.broadcasted_iota(I32, (8, LANES), 0) == lax.broadcasted_iota(I32, (8, LANES), 1)).astype(F32)
    gates_ref[...] = sum(_dot_tn(part, eye8) for part in _split3(meta))[:, 4:6]


def _outproj(x2d, o_swa, o_gdn, wo_bf16, gain, w_router, b_router):
    n = x2d.shape[0]
    tm = ROW_TILE
    row = lambda w: pl.BlockSpec((tm, w), lambda i: (i, 0))
    full = lambda a: pl.BlockSpec(a.shape, lambda i: (0,) * a.ndim)
    return pl.pallas_call(
        _outproj_kernel,
        grid=(n // tm,),
        in_specs=[row(D_MODEL), row(SWA_Q), row(GDN_V), full(wo_bf16), full(gain), full(w_router), full(b_router)],
        out_specs=[row(D_MODEL), row(D_MODEL), row(2), pl.BlockSpec((8, tm), lambda i: (0, i)),
                   pl.BlockSpec((LANES, 1), lambda i: (0, 0))],
        out_shape=[jax.ShapeDtypeStruct((n, D_MODEL), F32), jax.ShapeDtypeStruct((n, D_MODEL), F32),
                   jax.ShapeDtypeStruct((n, 2), F32), jax.ShapeDtypeStruct((8, n), F32),
                   jax.ShapeDtypeStruct((LANES, 1), F32)],
        scratch_shapes=[pltpu.VMEM((LANES, 1), F32)],
        compiler_params=pltpu.CompilerParams(dimension_semantics=("arbitrary",)),
        name="outproj_router",
    )(x2d, o_swa, o_gdn, wo_bf16, gain, w_router, b_router)


def _row_copy(src_ref, src_row, dst_ref, dst_row, sem):
    return pltpu.make_async_copy(src_ref.at[pl.ds(src_row, 1)], dst_ref.at[pl.ds(dst_row, 1)], sem)


def _scatter_kernel(dest0_ref, dest1_ref, xp_ref, xs_ref, out_ref, aidx_ref, sem, *, tiles_p, n_tokens):
    i = pl.program_id(0)
    tm = xp_ref.shape[0]

    def run(src_ref):
        def issue(g, carry):
            for u in range(DMA_UNROLL):
                r = g * DMA_UNROLL + u
                d0, d1 = dest0_ref[r], dest1_ref[r]
                _row_copy(src_ref, r, out_ref, d0, sem).start(priority=0)
                _row_copy(src_ref, r, out_ref, d1, sem).start(priority=1)
                aidx_ref[d0] = i * tm + r
                aidx_ref[d1] = n_tokens + i * tm + r
            return carry

        lax.fori_loop(0, tm // DMA_UNROLL, issue, 0)
        for _ in range(2):
            pltpu.make_async_copy(src_ref, out_ref.at[pl.ds(0, tm)], sem).wait()

    @pl.when(i < tiles_p)
    def _():
        run(xp_ref)

    @pl.when(i >= tiles_p)
    def _():
        run(xs_ref)


def _scatter(dest, xn_p, xn_s):
    tm = ROW_TILE
    tiles_p, tiles_s = xn_p.shape[0] // tm, xn_s.shape[0] // tm
    rows = 2 * (xn_p.shape[0] + xn_s.shape[0])
    idx = pl.BlockSpec((tm,), lambda i: (i,), memory_space=pltpu.SMEM)
    return pl.pallas_call(
        functools.partial(_scatter_kernel, tiles_p=tiles_p, n_tokens=rows // 2),
        grid=(tiles_p + tiles_s,),
        in_specs=[idx, idx,
                  pl.BlockSpec((tm, D_MODEL), lambda i: (jnp.minimum(i, tiles_p - 1), 0)),
                  pl.BlockSpec((tm, D_MODEL), lambda i: (jnp.maximum(i - tiles_p, 0), 0))],
        out_specs=[pl.BlockSpec(memory_space=pl.ANY), pl.BlockSpec(memory_space=pltpu.SMEM)],
        out_shape=[jax.ShapeDtypeStruct((rows, D_MODEL), F32), jax.ShapeDtypeStruct((rows,), I32)],
        scratch_shapes=[pltpu.SemaphoreType.DMA(())],
        compiler_params=pltpu.CompilerParams(dimension_semantics=("arbitrary",)),
        name="scatter_rows",
    )(dest[0], dest[1], xn_p, xn_s)


def _experts_kernel(blk_ref, exp_ref, lo_ref, hi_ref, first_ref, fresh_ref, pblk_ref,
                    aprev_ref, x_ref, wg_ref, wu_ref, wd_ref, ya_ref, wg_l, wu_l, wd_l, ybuf, sem,
                    *, n_items, n_blocks):
    j = pl.program_id(0)
    lo, hi = lo_ref[j], hi_ref[j]
    slot = lax.rem(blk_ref[j], 2)
    quarter = MOE_BLOCK // 4

    def wait_rows(s):
        pltpu.make_async_copy(ybuf.at[s], ya_ref.at[pl.ds(0, MOE_BLOCK)], sem.at[s]).wait()

    def send_rows(s, group):
        for row in range(group * quarter, (group + 1) * quarter):
            _row_copy(ybuf.at[s], row, ya_ref, aprev_ref[0, 0, row], sem.at[s]).start(priority=1)

    @pl.when(fresh_ref[j] == 1)
    def _():
        wg_l[...] = wg_ref[0].astype(BF16)
        wu_l[...] = wu_ref[0].astype(BF16)
        wd_l[...] = wd_ref[0].astype(BF16)

    @pl.when((first_ref[j] == 1) & (blk_ref[j] >= 2))
    def _():
        wait_rows(slot)

    def item(is_first, send_prev):
        send = (lambda group: send_rows(1 - slot, group)) if send_prev else (lambda group: None)
        send(0)
        x = x_ref[...].astype(BF16)
        gate = _dot(x, wg_l[...])
        send(1)
        up = _dot(x, wu_l[...])
        hid = (gate * jax.nn.sigmoid(gate)) * up
        send(2)
        y = _dot(hid.astype(BF16), wd_l[...])
        send(3)
        r = lax.broadcasted_iota(I32, (MOE_BLOCK, 1), 0)
        mine = (r >= lo) & (r < hi)
        ybuf[slot] = jnp.where(mine, y, 0.0 if is_first else ybuf[slot])

    live = hi > lo
    pl.when(live & (first_ref[j] == 1) & (blk_ref[j] >= 1))(functools.partial(item, True, True))
    pl.when(live & (first_ref[j] == 1) & (blk_ref[j] == 0))(functools.partial(item, True, False))
    pl.when(live & (first_ref[j] == 0))(functools.partial(item, False, False))

    @pl.when(j == n_items - 1)
    def _():
        last_slot = (n_blocks - 1) % 2
        for group in range(4):
            send_rows(last_slot, group)
        wait_rows(1 - last_slot)
        wait_rows(last_slot)


def _experts(items, aidx, xs, w_gate, w_up, w_down):
    n_items = items[0].shape[0]
    n_blocks = xs.shape[0] // MOE_BLOCK
    assert n_items > n_blocks + N_EXPERTS - 1 and n_blocks >= 2
    xblk = pl.BlockSpec((MOE_BLOCK, D_MODEL), lambda j, blk, *_: (blk[j], 0))
    wspec = lambda a: pl.BlockSpec((1,) + a.shape[1:], lambda j, blk, ex, *_: (ex[j], 0, 0))
    return pl.pallas_call(
        functools.partial(_experts_kernel, n_items=n_items, n_blocks=n_blocks),
        grid_spec=pltpu.PrefetchScalarGridSpec(
            num_scalar_prefetch=len(items),
            grid=(n_items,),
            in_specs=[pl.BlockSpec((1, 1, MOE_BLOCK), lambda j, *pre: (pre[-1][j], 0, 0), memory_space=pltpu.SMEM),
                      xblk, wspec(w_gate), wspec(w_up), wspec(w_down)],
            out_specs=pl.BlockSpec(memory_space=pl.ANY),
            scratch_shapes=[pltpu.VMEM(w_gate.shape[1:], BF16), pltpu.VMEM(w_up.shape[1:], BF16),
                            pltpu.VMEM(w_down.shape[1:], BF16), pltpu.VMEM((2, MOE_BLOCK, D_MODEL), F32),
                            pltpu.SemaphoreType.DMA((2,))]),
        out_shape=jax.ShapeDtypeStruct(xs.shape, F32),
        compiler_params=pltpu.CompilerParams(dimension_semantics=("arbitrary",)),
        name="experts",
    )(*items, aidx.reshape(n_blocks, 1, MOE_BLOCK), xs, w_gate, w_up, w_down)


def _work_items(counts, total_rows):
    n_blocks = total_rows // MOE_BLOCK
    n_items = n_blocks + N_EXPERTS
    end = jnp.cumsum(counts)
    start = end - counts
    first_blk = start // MOE_BLOCK
    nb = jnp.where(counts > 0, (end - 1) // MOE_BLOCK - first_blk + 1, 0)
    item_end = jnp.cumsum(nb)
    used = item_end[-1]
    j = jnp.arange(n_items, dtype=I32)
    jj = jnp.minimum(j, used - 1)
    e = jnp.minimum(jnp.sum((item_end[None, :] <= jj[:, None]).astype(I32), axis=1), N_EXPERTS - 1)
    onehot = (e[:, None] == jnp.arange(N_EXPERTS, dtype=I32)[None, :]).astype(I32)
    pick = lambda a: jnp.sum(onehot * a[None, :], axis=1)
    blk = pick(first_blk) + (jj - (pick(item_end) - pick(nb)))
    lo = jnp.maximum(pick(start), blk * MOE_BLOCK) - blk * MOE_BLOCK
    hi = jnp.minimum(pick(end), (blk + 1) * MOE_BLOCK) - blk * MOE_BLOCK
    live = j < used
    lo = jnp.where(live, lo, 0)
    hi = jnp.where(live, hi, 0)
    prev_blk = jnp.concatenate([jnp.full((1,), -1, I32), blk[:-1]])
    first = (live & (blk != prev_blk)).astype(I32)
    prev_e = jnp.concatenate([jnp.full((1,), -1, I32), e[:-1]])
    fresh = (live & (e != prev_e)).astype(I32)
    pblk = jnp.where(j == n_items - 1, n_blocks - 1, jnp.maximum(blk - 1, 0))
    return (blk.astype(I32), e.astype(I32), lo.astype(I32), hi.astype(I32), first, fresh, pblk.astype(I32)), start


def _combine_kernel(gates_ref, h_ref, gf_ref, y0_ref, y1_ref, out_ref):
    gates = gates_ref[...]
    y = h_ref[...] + (y0_ref[...] * gates[:, 0:1] + y1_ref[...] * gates[:, 1:2])
    out_ref[...] = _rms(y, gf_ref[...])


def _combine(gates, h, gain, ya, first_token):
    n = h.shape[0]
    tm = COMBINE_TILE
    tile0 = first_token // tm
    slot_tiles = ya.shape[0] // 2 // tm
    row = lambda w: pl.BlockSpec((tm, w), lambda i: (i, 0))
    slot = lambda k: pl.BlockSpec((tm, D_MODEL), lambda i: (k * slot_tiles + tile0 + i, 0))
    return pl.pallas_call(
        _combine_kernel,
        grid=(n // tm,),
        in_specs=[row(2), row(D_MODEL), pl.BlockSpec(gain.shape, lambda i: (0, 0)), slot(0), slot(1)],
        out_specs=row(D_MODEL),
        out_shape=jax.ShapeDtypeStruct((n, D_MODEL), F32),
        compiler_params=pltpu.CompilerParams(dimension_semantics=("arbitrary",)),
        name="combine_norm",
    )(gates, h, gain, ya, ya)


def kernel(x_prompt, x_sample, cache_swa_k, cache_swa_v, state_gdn_conv, state_gdn, norm_mix, w_in, swa_sinks,
           gdn_conv_w, gdn_A_log, gdn_dt_bias, gdn_norm_w, w_out, norm_ffn, w_router_group, b_router_group,
           w_router_expert, b_router_expert, w_exp_gate, w_exp_up, w_exp_down, norm_final):
    depth = w_in.shape[0]
    assert depth == 1, "single trunk layer"
    bp, sp, _ = x_prompt.shape
    bs, ts, _ = x_sample.shape
    np_, ns = bp * sp, bs * ts
    l = 0

    w_in_b = w_in[l].astype(BF16)
    w_out_b = w_out[l].astype(BF16)
    g_mix = norm_mix[l].reshape(1, D_MODEL)
    g_ffn = norm_ffn[l].reshape(1, D_MODEL)
    g_fin = norm_final.reshape(1, D_MODEL)
    pad = LANES - N_EXPERTS - N_GROUPS
    w_router = jnp.concatenate([w_router_expert[l], w_router_group[l], jnp.zeros((D_MODEL, pad), F32)],
                               axis=1).astype(BF16).T
    b_router = jnp.concatenate([b_router_expert[l].reshape(-1), b_router_group[l],
                                jnp.zeros((pad,), F32)])[:, None]
    zeros8 = jnp.zeros((GDN_HEADS,), F32)
    alog16 = jnp.concatenate([gdn_A_log[l], zeros8])[None]
    dtb16 = jnp.concatenate([gdn_dt_bias[l], zeros8])[None]
    nw_group = jnp.tile(gdn_norm_w[l].reshape(1, HEAD_DIM), (1, GDN_GROUP))
    sinks = swa_sinks[l]

    q_p, k_p, v_p, c_p, z_p, ab_p = _inproj(x_prompt.reshape(np_, D_MODEL), g_mix, w_in_b)
    nblk = np_ // WINDOW
    k_p3 = k_p.reshape(nblk, WINDOW, SWA_KV)
    v_p3 = v_p.reshape(nblk, WINDOW, SWA_KV)
    (o_swa_p,) = _swa(sinks, q_p.reshape(nblk, WINDOW, SWA_Q), k_p3, v_p3, k_p3, v_p3,
                      bb=SWA_BLOCKS, blocks_per_seq=sp // WINDOW, emit_cache=False)
    c_p3 = c_p.reshape(bp, sp, GDN_CONV_CH)
    o_gdn_p, s_fin_p = _gdn(c_p3, jnp.zeros((bp, CONV_W - 1, GDN_CONV_CH), F32), z_p.reshape(bp, sp, GDN_V),
                            ab_p.reshape(bp, sp, 2 * GDN_HEADS),
                            jnp.zeros((bp, GDN_HEADS, HEAD_DIM, HEAD_DIM), F32),
                            gdn_conv_w[l], alog16, dtb16, nw_group, chunk=GDN_CHUNK, seq_block=bp)
    h_p, xn_p, gates_p, meta_p, cnt_p = _outproj(
        x_prompt.reshape(np_, D_MODEL), o_swa_p.reshape(np_, SWA_Q), o_gdn_p.reshape(np_, GDN_V),
        w_out_b, g_ffn, w_router, b_router)

    q_s, k_s, v_s, c_s, z_s, ab_s = _inproj(x_sample.reshape(ns, D_MODEL), g_mix, w_in_b)
    o_swa_s, kcache_s, vcache_s = _swa(
        sinks, q_s.reshape(bs, ts, SWA_Q), k_s.reshape(bs, ts, SWA_KV), v_s.reshape(bs, ts, SWA_KV),
        cache_swa_k[l].reshape(bs, WINDOW, SWA_KV), cache_swa_v[l].reshape(bs, WINDOW, SWA_KV),
        bb=16, blocks_per_seq=None, emit_cache=True)
    c_s3 = c_s.reshape(bs, ts, GDN_CONV_CH)
    o_gdn_s, s_fin_s = _gdn(c_s3, state_gdn_conv[l], z_s.reshape(bs, ts, GDN_V),
                            ab_s.reshape(bs, ts, 2 * GDN_HEADS), state_gdn[l],
                            gdn_conv_w[l], alog16, dtb16, nw_group, chunk=ts, seq_block=16)
    h_s, xn_s, gates_s, meta_s, cnt_s = _outproj(
        x_sample.reshape(ns, D_MODEL), o_swa_s.reshape(ns, SWA_Q), o_gdn_s.reshape(ns, GDN_V),
        w_out_b, g_ffn, w_router, b_router)

    cnt_p_i = cnt_p[:N_EXPERTS, 0].astype(I32)
    cnt_s_i = cnt_s[:N_EXPERTS, 0].astype(I32)
    items, start = _work_items(cnt_p_i + cnt_s_i, 2 * (np_ + ns))
    expert_ids = jnp.arange(N_EXPERTS, dtype=I32)
    lookup = lambda table, ids: jnp.sum(jnp.where(ids[..., None] == expert_ids, table, 0), axis=-1)
    dest_p = lookup(start, meta_p[0:2].astype(I32)) + meta_p[2:4].astype(I32)
    dest_s = lookup(start + cnt_p_i, meta_s[0:2].astype(I32)) + meta_s[2:4].astype(I32)
    xs, aidx = _scatter(jnp.concatenate([dest_p, dest_s], axis=1), xn_p, xn_s)
    ya = _experts(items, aidx, xs, w_exp_gate[l], w_exp_up[l], w_exp_down[l])
    y_p = _combine(gates_p, h_p, g_fin, ya, 0)
    y_s = _combine(gates_s, h_s, g_fin, ya, np_)

    kv5 = lambda a, b: a.reshape(b, -1, SWA_KV_HEADS, HEAD_DIM)[None]
    return (y_p.reshape(bp, sp, D_MODEL), y_s.reshape(bs, ts, D_MODEL),
            kv5(k_p.reshape(bp, sp, SWA_KV)[:, -WINDOW:], bp), kv5(v_p.reshape(bp, sp, SWA_KV)[:, -WINDOW:], bp),
            kv5(kcache_s, bs), kv5(vcache_s, bs),
            c_p3[:, -(CONV_W - 1):][None], c_s3[:, -(CONV_W - 1):][None],
            s_fin_p[None], s_fin_s[None])
```

```python
import functools

import jax
import jax.numpy as jnp
from jax import lax
from jax.experimental import pallas as pl
from jax.experimental.pallas import tpu as pltpu

F32 = jnp.float32
BF16 = jnp.bfloat16
I32 = jnp.int32
U32 = jnp.uint32

D_MODEL = 1024
HEAD_DIM = 64
SWA_HEADS = 8
GDN_HEADS = 8
SWA_KV_HEADS = 2
GQA_GROUP = SWA_HEADS // SWA_KV_HEADS
WINDOW = 128
ATTN_SCALE = HEAD_DIM ** -0.5
CONV_W = 4
N_GROUPS = 8
EXPERTS_PER_GROUP = 8
N_EXPERTS = 64
D_EXPERT = 256
EPS = 1e-6

SWA_Q = SWA_HEADS * HEAD_DIM
SWA_KV = SWA_KV_HEADS * HEAD_DIM
GDN_QK = GDN_HEADS * HEAD_DIM
GDN_V = GDN_HEADS * HEAD_DIM
GDN_CONV_CH = 2 * GDN_QK + GDN_V
D_MIX = SWA_Q + GDN_V
D_IN = SWA_Q + 2 * SWA_KV + GDN_CONV_CH + GDN_V + 2 * GDN_HEADS
COL_K = SWA_Q
COL_V = COL_K + SWA_KV
COL_C = COL_V + SWA_KV
COL_Z = COL_C + GDN_CONV_CH
COL_AB = COL_Z + GDN_V
PACKED_W = D_MODEL // 2

LANES = 128
NEG_BIG = -1e30
ROW_TILE = 512
INPROJ_TILE = 1024
MOE_BLOCK = 256
COMBINE_TILE = 512
DMA_UNROLL = 8
SWA_BLOCKS = 4
GDN_CHUNK = 64
GDN_TILE = 256
GDN_GROUP = 4
GDN_GROUP_W = GDN_GROUP * HEAD_DIM


def _rms(x, g):
    return x * lax.rsqrt(jnp.mean(x * x, axis=-1, keepdims=True) + EPS) * g


def _dot(a, b):
    return jnp.dot(a, b, preferred_element_type=F32)


def _dot_nt(a, b):
    return lax.dot_general(a, b, (((1,), (1,)), ((), ())), preferred_element_type=F32)


def _dot_tn(a, b):
    return lax.dot_general(a, b, (((0,), (0,)), ((), ())), preferred_element_type=F32)


def _pack_halves(x):
    w = x.shape[1] // 2
    bits = lambda v: lax.bitcast_convert_type(v.astype(BF16).astype(F32), U32)
    return bits(x[:, :w]) | (bits(x[:, w:]) >> 16)


def _unpack_halves(p):
    hi = lax.bitcast_convert_type(p & jnp.uint32(0xFFFF0000), F32)
    lo = lax.bitcast_convert_type(p << 16, F32)
    return hi.astype(BF16), lo.astype(BF16)


def _split3(x):
    p1 = x.astype(BF16).astype(F32)
    r = x - p1
    p2 = r.astype(BF16).astype(F32)
    p3 = (r - p2).astype(BF16).astype(F32)
    return p1, p2, p3


def _inproj_kernel(x_ref, g_ref, w_ref, q_ref, k_ref, v_ref, c_ref, z_ref, ab_ref):
    x = x_ref[...]
    xb = _rms(x, g_ref[...]).astype(BF16)
    q_ref[...] = _dot(xb, w_ref[:, 0:COL_K])
    k_ref[...] = _dot(xb, w_ref[:, COL_K:COL_V])
    v_ref[...] = _dot(xb, w_ref[:, COL_V:COL_C])
    c_ref[...] = _dot(xb, w_ref[:, COL_C:COL_Z])
    z_ref[...] = _dot(xb, w_ref[:, COL_Z:COL_AB])
    ab_ref[...] = _dot(xb, w_ref[:, COL_AB:D_IN])


def _inproj(x2d, gain, w_bf16):
    n = x2d.shape[0]
    tm = INPROJ_TILE
    row = lambda w: pl.BlockSpec((tm, w), lambda i: (i, 0))
    full = lambda a: pl.BlockSpec(a.shape, lambda i: (0,) * a.ndim)
    widths = (SWA_Q, SWA_KV, SWA_KV, GDN_CONV_CH, GDN_V, 2 * GDN_HEADS)
    return pl.pallas_call(
        _inproj_kernel,
        grid=(n // tm,),
        in_specs=[row(D_MODEL), full(gain), full(w_bf16)],
        out_specs=[row(w) for w in widths],
        out_shape=[jax.ShapeDtypeStruct((n, w), F32) for w in widths],
        compiler_params=pltpu.CompilerParams(dimension_semantics=("arbitrary",)),
        name="inproj",
    )(x2d, gain, w_bf16)


def _swa_kernel(sink_ref, q_ref, kc_ref, vc_ref, kp_ref, vp_ref, o_ref, *cache_refs,
                bb, t, blocks_per_seq, emit_cache):
    rows = GQA_GROUP * t
    ri = lax.broadcasted_iota(I32, (rows, 1), 0)
    qi = lax.rem(ri, t)
    gi = ri // t
    if emit_cache:
        kj = lax.broadcasted_iota(I32, (rows, WINDOW), 1)
        mask = kj <= qi + (WINDOW - t)
        mask_first = mask
    else:
        kj = lax.broadcasted_iota(I32, (rows, WINDOW + t), 1)
        mask = (kj <= qi + WINDOW) & (kj > qi)
        has_prev = lax.rem(pl.program_id(0) * bb, blocks_per_seq) != 0
        mask_first = mask & (has_prev | (kj >= WINDOW))
    sinks = []
    for h in range(SWA_KV_HEADS):
        sink = jnp.zeros((rows, 1), F32)
        for g in range(GQA_GROUP):
            sink = jnp.where(gi == g, sink_ref[GQA_GROUP * h + g], sink)
        sinks.append(sink)
    chains = [(b, h) for b in range(bb) for h in range(SWA_KV_HEADS)]
    scores, values = [], []
    for b, h in chains:
        hs = slice(h * HEAD_DIM, (h + 1) * HEAD_DIM)
        kc, vc = kc_ref[b, :, hs], vc_ref[b, :, hs]
        if emit_cache:
            kp, vp = kp_ref[b, :, hs], vp_ref[b, :, hs]
            keys = jnp.concatenate([kp[t:], kc], axis=0)
            vals = jnp.concatenate([vp[t:], vc], axis=0)
            cache_refs[0][b, :, hs] = keys
            cache_refs[1][b, :, hs] = vals
        else:
            kp, vp = (kp_ref[0, :, hs], vp_ref[0, :, hs]) if b == 0 else (kc_ref[b - 1, :, hs], vc_ref[b - 1, :, hs])
            keys = jnp.concatenate([kp, kc], axis=0)
            vals = jnp.concatenate([vp, vc], axis=0)
        q4 = jnp.concatenate(
            [q_ref[b, :, (GQA_GROUP * h + g) * HEAD_DIM:(GQA_GROUP * h + g + 1) * HEAD_DIM]
             for g in range(GQA_GROUP)], axis=0)
        scores.append(_dot_nt(q4.astype(BF16), keys.astype(BF16)))
        values.append(vals.astype(BF16))
    probs, dens = [], []
    for (b, h), s in zip(chains, scores):
        s = jnp.where(mask_first if b == 0 else mask, s * ATTN_SCALE, NEG_BIG)
        m = jnp.maximum(jnp.max(s, axis=-1, keepdims=True), sinks[h])
        p = jnp.exp(s - m)
        dens.append(jnp.sum(p, axis=-1, keepdims=True) + jnp.exp(sinks[h] - m))
        probs.append(p.astype(BF16))
    outs = [_dot(p, v) / den for p, v, den in zip(probs, values, dens)]
    for b in range(bb):
        o_ref[b] = jnp.concatenate([outs[b * SWA_KV_HEADS + h][g * t:(g + 1) * t]
                                    for h in range(SWA_KV_HEADS) for g in range(GQA_GROUP)], axis=-1)


def _swa(sinks, q3, k3, v3, kprev3, vprev3, *, bb, blocks_per_seq, emit_cache):
    nb, t, _ = q3.shape
    cur = lambda w: pl.BlockSpec((bb, t, w), lambda i: (i, 0, 0))
    if emit_cache:
        prev = pl.BlockSpec((bb, WINDOW, SWA_KV), lambda i: (i, 0, 0))
    else:
        assert blocks_per_seq % bb == 0
        prev = pl.BlockSpec((1, WINDOW, SWA_KV), lambda i: (jnp.maximum(i * bb - 1, 0), 0, 0))
    out_specs = [cur(SWA_Q)]
    out_shape = [jax.ShapeDtypeStruct((nb, t, SWA_Q), F32)]
    if emit_cache:
        cache = pl.BlockSpec((bb, WINDOW, SWA_KV), lambda i: (i, 0, 0))
        out_specs += [cache, cache]
        out_shape += [jax.ShapeDtypeStruct((nb, WINDOW, SWA_KV), F32)] * 2
    return pl.pallas_call(
        functools.partial(_swa_kernel, bb=bb, t=t, blocks_per_seq=blocks_per_seq, emit_cache=emit_cache),
        grid=(nb // bb,),
        in_specs=[pl.BlockSpec(memory_space=pltpu.SMEM), cur(SWA_Q), cur(SWA_KV), cur(SWA_KV), prev, prev],
        out_specs=out_specs,
        out_shape=out_shape,
        compiler_params=pltpu.CompilerParams(dimension_semantics=("arbitrary",)),
        name="swa_cache" if emit_cache else "swa_band",
    )(sinks, q3, k3, v3, kprev3, vprev3)


def _gdn_prep_kernel(c_ref, hist_ref, ab_ref, cw_ref, alog_ref, dtb_ref,
                     u_ref, w_ref, qd_ref, kd_ref, qk_ref, gt_ref, cbuf_ref, *, chunk):
    sb, r, _ = c_ref.shape
    tp = sb * r
    cn = chunk
    low = w_ref.dtype
    hist_rows = CONV_W - 1

    @pl.when(pl.program_id(1) == 0)
    def _():
        cbuf_ref[:, 8 - hist_rows:8, :] = hist_ref[...]

    cbuf_ref[:, 8:8 + r, :] = c_ref[...]
    conv = cbuf_ref[:, 8 - hist_rows:8 - hist_rows + r, :] * cw_ref[0:1, :]
    for i in range(1, CONV_W):
        conv = conv + cbuf_ref[:, 8 - hist_rows + i:8 - hist_rows + i + r, :] * cw_ref[i:i + 1, :]
    tail = cbuf_ref[:, 8 + r - hist_rows:8 + r, :]
    cbuf_ref[:, 8 - hist_rows:8, :] = tail
    conv = (conv * jax.nn.sigmoid(conv)).reshape(tp, GDN_CONV_CH)

    ab = ab_ref[...].reshape(tp, 2 * GDN_HEADS)
    is_g = lax.broadcasted_iota(I32, (1, 2 * GDN_HEADS), 1) < GDN_HEADS
    g = jnp.where(is_g, -jnp.exp(alog_ref[...]) * jax.nn.softplus(ab + dtb_ref[...]), 0.0)
    beta = jax.nn.sigmoid(ab)

    ri = lax.broadcasted_iota(I32, (tp, tp), 0)
    ci = lax.broadcasted_iota(I32, (tp, tp), 1)
    same = (ri // cn) == (ci // cn)
    causal = same & (ri >= ci)
    strict = same & (ri > ci)
    eye = (ri == ci).astype(F32)
    stack = jnp.concatenate([causal.astype(BF16), same.astype(BF16)], axis=0)
    both = sum(_dot(stack, p.astype(BF16)) for p in _split3(g))
    gc, gl = both[:tp], both[tp:]
    e16 = lax.broadcasted_iota(I32, (2 * GDN_HEADS, 2 * GDN_HEADS), 0)
    eye16 = (e16 == lax.broadcasted_iota(I32, (2 * GDN_HEADS, 2 * GDN_HEADS), 1)).astype(BF16)
    gc_row = sum(_dot_nt(eye16, p.astype(BF16)) for p in _split3(gc))
    gt_ref[...] = jnp.exp(gl)
    fold = (lax.broadcasted_iota(I32, (tp, cn), 0) % cn == lax.broadcasted_iota(I32, (tp, cn), 1)).astype(low)

    levels = cn.bit_length() - 2
    heads = range(GDN_HEADS)
    lanes_of = lambda i, h: slice(i * GDN_QK + h * HEAD_DIM, i * GDN_QK + (h + 1) * HEAD_DIM)
    qs = [conv[:, lanes_of(0, h)] for h in heads]
    ks = [conv[:, lanes_of(1, h)] for h in heads]
    qs = [q * lax.rsqrt(jnp.sum(q * q, axis=-1, keepdims=True) + EPS) * (HEAD_DIM ** -0.5) for q in qs]
    ks = [k * lax.rsqrt(jnp.sum(k * k, axis=-1, keepdims=True) + EPS) for k in ks]
    g_cs = [gc[:, h:h + 1] for h in heads]
    b_hs = [beta[:, GDN_HEADS + h:GDN_HEADS + h + 1] for h in heads]
    decays = [jnp.exp(jnp.where(causal, g_cs[h] - gc_row[h:h + 1, :], NEG_BIG)) for h in heads]
    egs = [jnp.exp(g_c) for g_c in g_cs]
    kbs = [k * b_h for k, b_h in zip(ks, b_hs)]
    k_ls = [k.astype(low) for k in ks]
    lmats = [jnp.where(strict, _dot_nt(kb.astype(low), k_l) * decay, 0.0) for kb, k_l, decay in zip(kbs, k_ls, decays)]
    qks = [_dot((_dot_nt(q.astype(low), k_l) * decay).astype(low), fold) for q, k_l, decay in zip(qs, k_ls, decays)]
    xs = [eye - lmat for lmat in lmats]
    ps = [_dot(lmat.astype(low), lmat.astype(low)) for lmat in lmats]
    for lev in range(1, levels + 1):
        p_ls = [p.astype(low) for p in ps]
        if lev < levels:
            xp = [_dot(jnp.concatenate([x.astype(low), p_l], axis=0), p_l) for x, p_l in zip(xs, p_ls)]
            xs = [x + m[:tp] for x, m in zip(xs, xp)]
            ps = [m[tp:] for m in xp]
        else:
            xs = [x + _dot(x.astype(low), p_l) for x, p_l in zip(xs, p_ls)]
    sols = [_dot(xs[h].astype(low),
                 jnp.concatenate([conv[:, lanes_of(2, h)] * b_hs[h], kbs[h] * egs[h]], axis=-1).astype(low))
            for h in heads]
    u_ref[...] = jnp.concatenate([sol[:, :HEAD_DIM] for sol in sols], axis=-1)
    w_ref[...] = jnp.concatenate([sol[:, HEAD_DIM:] for sol in sols], axis=-1).astype(low)
    qd_ref[...] = jnp.concatenate([q * eg for q, eg in zip(qs, egs)], axis=-1).astype(low)
    kd_ref[...] = jnp.concatenate([ks[h] * jnp.exp(gl[:, h:h + 1] - g_cs[h]) for h in heads], axis=-1).astype(low)
    qk_ref[...] = jnp.concatenate(qks, axis=-1).astype(low)


def _gdn_prep_pair_kernel(c_ref, hist_ref, ab_ref, cw_ref, alog_ref, dtb_ref,
                          u_ref, w_ref, qd_ref, kd_ref, qk_ref, gt_ref, cbuf_ref):
    _, r, _ = c_ref.shape
    tp = r
    cn = HEAD_DIM
    pair_w = 2 * HEAD_DIM
    hist_rows = CONV_W - 1
    sublanes = 8

    @pl.when(pl.program_id(1) == 0)
    def _():
        cbuf_ref[...] = jnp.zeros_like(cbuf_ref)
        cbuf_ref[:, sublanes - hist_rows:sublanes, :] = hist_ref[...]

    x3 = c_ref[0].reshape(tp // sublanes, sublanes, GDN_CONV_CH)
    before = cbuf_ref[...]
    sub = lax.broadcasted_iota(I32, (1, sublanes, 1), 1)
    conv3 = x3 * cw_ref[CONV_W - 1:CONV_W, :]
    for s in range(1, CONV_W):
        rx = pltpu.roll(x3, s, axis=1)
        rp = jnp.concatenate([pltpu.roll(before, s, axis=1), rx[:-1]], axis=0)
        conv3 = conv3 + jnp.where(sub < s, rp, rx) * cw_ref[CONV_W - 1 - s:CONV_W - s, :]
    cbuf_ref[...] = x3[tp // sublanes - 1:]
    conv = (conv3 * jax.nn.sigmoid(conv3)).reshape(tp, GDN_CONV_CH)

    ab = ab_ref[...].reshape(tp, 2 * GDN_HEADS)
    is_g = lax.broadcasted_iota(I32, (1, 2 * GDN_HEADS), 1) < GDN_HEADS
    g = jnp.where(is_g, -jnp.exp(alog_ref[...]) * jax.nn.softplus(ab + dtb_ref[...]), 0.0)
    beta = jax.nn.sigmoid(ab)

    ri = lax.broadcasted_iota(I32, (tp, tp), 0)
    ci = lax.broadcasted_iota(I32, (tp, tp), 1)
    same = (ri // cn) == (ci // cn)
    same_l = same.astype(BF16)
    stack = jnp.concatenate([(same & (ri >= ci)).astype(BF16), same_l], axis=0)
    both = sum(_dot(stack, p.astype(BF16)) for p in _split3(g))
    gc, gl = both[:tp], both[tp:]
    gt_ref[...] = jnp.exp(gl)

    lane = lax.broadcasted_iota(I32, (tp, pair_w), 1)
    c_in = lax.broadcasted_iota(I32, (tp, pair_w), 0) % cn
    j_in = lane % cn
    left = lane < cn
    causal = c_in >= j_in
    strict = c_in > j_in
    diag = c_in == j_in
    eye = diag.astype(F32)
    bdmask = ((lax.broadcasted_iota(I32, (pair_w, pair_w), 0) // cn)
              == (lax.broadcasted_iota(I32, (pair_w, pair_w), 1) // cn))
    ones_bd = bdmask.astype(BF16)

    def bd(m):
        return jnp.where(bdmask, jnp.concatenate([m, m], axis=0), 0.0).astype(BF16)

    def head_sum(x):
        hi = x.astype(BF16)
        lo = (x - hi.astype(F32)).astype(BF16)
        return _dot(hi, ones_bd) + _dot(lo, ones_bd)

    levels = cn.bit_length() - 2
    chunks = [slice(n * cn, (n + 1) * cn) for n in range(tp // cn)]
    n_pairs = GDN_HEADS // 2
    pairs = range(n_pairs)
    pick = lambda m, off, p: jnp.where(left, m[:, off + 2 * p:off + 2 * p + 1], m[:, off + 2 * p + 1:off + 2 * p + 2])
    third = lambda i, p: conv[:, i * GDN_QK + p * pair_w:i * GDN_QK + (p + 1) * pair_w]
    qs = [third(0, p) for p in pairs]
    ks = [third(1, p) for p in pairs]
    qs = [q * lax.rsqrt(head_sum(q * q) + EPS) * (HEAD_DIM ** -0.5) for q in qs]
    ks = [k * lax.rsqrt(head_sum(k * k) + EPS) for k in ks]
    gcps = [pick(gc, 0, p) for p in pairs]
    rowms = [sum(_dot(same_l, part.astype(BF16)) for part in _split3(jnp.where(diag, gcp, 0.0))) for gcp in gcps]
    decays = [jnp.exp(jnp.where(causal, gcp - rowm, NEG_BIG)) for gcp, rowm in zip(gcps, rowms)]
    lmats, vbs, kbegs = [], [], []
    for p in pairs:
        ls = slice(p * pair_w, (p + 1) * pair_w)
        q, k, gcp, decay = qs[p], ks[p], gcps[p], decays[p]
        bp = pick(beta, GDN_HEADS, p)
        eg = jnp.exp(gcp)
        kb = k * bp
        vbs.append(third(2, p) * bp)
        kbegs.append(kb * eg)
        qd_ref[:, ls] = (q * eg).astype(BF16)
        kd_ref[:, ls] = (k * jnp.exp(pick(gl, 0, p) - gcp)).astype(BF16)
        q_l, kb_l = q.astype(BF16), kb.astype(BF16)
        kbd = [bd(k[rs]) for rs in chunks]
        kk = jnp.concatenate([_dot_nt(kb_l[rs], kbd[n]) for n, rs in enumerate(chunks)], axis=0)
        qk = jnp.concatenate([_dot_nt(q_l[rs], kbd[n]) for n, rs in enumerate(chunks)], axis=0)
        qk_ref[:, ls] = (qk * decay).astype(BF16)
        lmats.append(jnp.where(strict, kk * decay, 0.0))
    bodies = [(p, rs) for p in range(n_pairs) for rs in chunks]
    xs = [eye[rs] - lmats[p][rs] for p, rs in bodies]
    ps = [_dot(lmats[p][rs].astype(BF16), bd(lmats[p][rs])) for p, rs in bodies]
    for lev in range(1, levels + 1):
        pbd = [bd(pm) for pm in ps]
        if lev < levels:
            xp = [_dot(jnp.concatenate([xm, pm], axis=0).astype(BF16), wm) for xm, pm, wm in zip(xs, ps, pbd)]
            xs = [xm + m[:cn] for xm, m in zip(xs, xp)]
            ps = [m[cn:] for m in xp]
        else:
            xs = [xm + _dot(xm.astype(BF16), wm) for xm, wm in zip(xs, pbd)]
    x_l = [xm.astype(BF16) for xm in xs]
    us = [_dot(xm, bd(vbs[p][rs])) for xm, (p, rs) in zip(x_l, bodies)]
    ws = [_dot(xm, bd(kbegs[p][rs])) for xm, (p, rs) in zip(x_l, bodies)]
    nc = len(chunks)
    for p in range(n_pairs):
        ls = slice(p * pair_w, (p + 1) * pair_w)
        u_ref[:, ls] = jnp.concatenate(us[p * nc:(p + 1) * nc], axis=0)
        w_ref[:, ls] = jnp.concatenate(ws[p * nc:(p + 1) * nc], axis=0).astype(BF16)


def _gdn_scan_kernel(u_ref, w_ref, qd_ref, kd_ref, qk_ref, gt_ref, z_ref, s0_ref, nw_ref,
                     o_ref, sfin_ref, sbd_ref, *, chunk, n_chunks):
    bb = u_ref.shape[0]
    cn = chunk
    low = w_ref.dtype
    gw = GDN_GROUP_W
    ni = pl.program_id(1)

    @pl.when(ni == 0)
    def _():
        sbd_ref[...] = jnp.zeros_like(sbd_ref)
        for b in range(bb):
            for h in range(GDN_HEADS):
                gi, hh = divmod(h, GDN_GROUP)
                ds = slice(hh * HEAD_DIM, (hh + 1) * HEAD_DIM)
                sbd_ref[b, gi, ds, ds] = s0_ref[b, h]

    bdmask = ((lax.broadcasted_iota(I32, (gw, gw), 0) // HEAD_DIM)
              == (lax.broadcasted_iota(I32, (gw, gw), 1) // HEAD_DIM))
    ones_bd = bdmask.astype(BF16)
    vmask = ((lax.broadcasted_iota(I32, (GDN_GROUP * cn, gw), 0) // cn)
             == (lax.broadcasted_iota(I32, (GDN_GROUP * cn, gw), 1) // HEAD_DIM))
    e_row = lax.broadcasted_iota(I32, (2 * GDN_HEADS, gw), 0)
    e_col = lax.broadcasted_iota(I32, (2 * GDN_HEADS, gw), 1) // HEAD_DIM
    chains = [(b, gi) for b in range(bb) for gi in range(GDN_HEADS // GDN_GROUP)]
    lanes = lambda gi: slice(gi * gw, (gi + 1) * gw)
    states = [sbd_ref[b, gi] for b, gi in chains]
    states_l = [s.astype(low) for s in states]
    wq_s = [_dot(jnp.concatenate([w_ref[b, :, lanes(gi)], qd_ref[b, :, lanes(gi)]], axis=0), s_l)
            for (b, gi), s_l in zip(chains, states_l)]
    v_new = [u_ref[b, :, lanes(gi)] - m[:cn] for (b, gi), m in zip(chains, wq_s)]
    q_s = [m[cn:] for m in wq_s]
    v_l = [v.astype(low) for v in v_new]
    outs = []
    for (b, gi), s, v, qs in zip(chains, states, v_l, q_s):
        vbd = jnp.where(vmask, jnp.concatenate([v] * GDN_GROUP, axis=0), jnp.zeros((), low))
        outs.append(qs + _dot(qk_ref[b, :, gi * GDN_GROUP * cn:(gi + 1) * GDN_GROUP * cn], vbd))
        upd = _dot_tn(kd_ref[b, :, lanes(gi)], v)
        expand = (e_row == e_col + gi * GDN_GROUP).astype(BF16)
        gte = sum(_dot(p.astype(BF16), expand) for p in _split3(gt_ref[b, 0:8, :]))[0:1]
        sbd_ref[b, gi] = s * gte + jnp.where(bdmask, upd, 0.0)
    for (b, gi), o in zip(chains, outs):
        o2 = o * o
        hi = o2.astype(BF16)
        lo = (o2 - hi.astype(F32)).astype(BF16)
        if cn % 16 == 0:
            sums = _dot(jnp.concatenate([hi, lo], axis=0), ones_bd)
            ms = (sums[:cn] + sums[cn:]) * (1.0 / HEAD_DIM)
        else:
            ms = (_dot(hi, ones_bd) + _dot(lo, ones_bd)) * (1.0 / HEAD_DIM)
        zg = z_ref[b, :, lanes(gi)]
        o_ref[b, :, lanes(gi)] = o * lax.rsqrt(ms + EPS) * nw_ref[...] * (zg * jax.nn.sigmoid(zg))

    @pl.when(ni == n_chunks - 1)
    def _():
        for b in range(bb):
            for h in range(GDN_HEADS):
                gi, hh = divmod(h, GDN_GROUP)
                ds = slice(hh * HEAD_DIM, (hh + 1) * HEAD_DIM)
                sfin_ref[b, h] = sbd_ref[b, gi, ds, ds]


def _gdn(c3, hist, z3, ab3, s0, conv_w, alog16, dtb16, nw_group, *, chunk, seq_block):
    nseq, t, _ = c3.shape
    n = nseq * t
    sb, r = (1, GDN_TILE) if t >= GDN_TILE else (GDN_TILE // t, t)
    tiles = t // r
    low = BF16 if chunk >= 16 else F32
    blk = lambda w: pl.BlockSpec((sb, r, w), lambda s, i: (s, i, 0))
    full = lambda a: pl.BlockSpec(a.shape, lambda s, i: (0,) * a.ndim)
    flat = lambda w: pl.BlockSpec((GDN_TILE, w), lambda s, i: (s * tiles + i, 0))
    widths = (GDN_V, GDN_V, GDN_QK, GDN_QK, GDN_HEADS * chunk, 2 * GDN_HEADS)
    dtypes = (F32, low, low, low, low, F32)
    prep_out = dict(out_specs=[flat(wd) for wd in widths],
                    out_shape=[jax.ShapeDtypeStruct((n, wd), dt) for wd, dt in zip(widths, dtypes)],
                    compiler_params=pltpu.CompilerParams(dimension_semantics=("arbitrary", "arbitrary")))
    lane_dense = chunk == HEAD_DIM and sb == 1
    u, w, qd, kd, qk, gt = pl.pallas_call(
        _gdn_prep_pair_kernel if lane_dense else functools.partial(_gdn_prep_kernel, chunk=chunk),
        grid=(nseq // sb, tiles),
        in_specs=[blk(GDN_CONV_CH), pl.BlockSpec((sb, CONV_W - 1, GDN_CONV_CH), lambda s, i: (s, 0, 0)),
                  blk(2 * GDN_HEADS), full(conv_w), full(alog16), full(dtb16)],
        scratch_shapes=[pltpu.VMEM((1, 8, GDN_CONV_CH) if lane_dense else (sb, 8 + r, GDN_CONV_CH), F32)],
        name="gdn_prep_pair" if lane_dense else "gdn_prep", **prep_out)(c3, hist, ab3, conv_w, alog16, dtb16)

    n_chunks = t // chunk
    tok = lambda wd: pl.BlockSpec((seq_block, chunk, wd), lambda s, c: (s, c, 0))
    per_seq = pl.BlockSpec((seq_block,) + s0.shape[1:], lambda s, c: (s, 0, 0, 0))
    seq3 = lambda a: a.reshape(nseq, t, a.shape[-1])
    return pl.pallas_call(
        functools.partial(_gdn_scan_kernel, chunk=chunk, n_chunks=n_chunks),
        grid=(nseq // seq_block, n_chunks),
        in_specs=[tok(wd) for wd in widths] + [tok(GDN_V), per_seq,
                                               pl.BlockSpec(nw_group.shape, lambda s, c: (0, 0))],
        out_specs=[tok(GDN_V), per_seq],
        out_shape=[jax.ShapeDtypeStruct((nseq, t, GDN_V), F32), jax.ShapeDtypeStruct(s0.shape, F32)],
        scratch_shapes=[pltpu.VMEM((seq_block, GDN_HEADS // GDN_GROUP, GDN_GROUP_W, GDN_GROUP_W), F32)],
        compiler_params=pltpu.CompilerParams(dimension_semantics=("arbitrary", "arbitrary")),
        name="gdn_scan",
    )(seq3(u), seq3(w), seq3(qd), seq3(kd), seq3(qk), seq3(gt), z3, s0, nw_group)


def _outproj_kernel(x_ref, osw_ref, ogd_ref, wo_ref, gf_ref, wr_ref, br_ref,
                    h_ref, xn_ref, gates_ref, meta_ref, cnt_ref, run_ref):
    i = pl.program_id(0)
    tm = x_ref.shape[0]
    rows = wr_ref.shape[0]

    @pl.when(i == 0)
    def _():
        run_ref[...] = jnp.zeros_like(run_ref)

    h = (x_ref[...] + _dot(osw_ref[...].astype(BF16), wo_ref[0:SWA_Q, :])
         + _dot(ogd_ref[...].astype(BF16), wo_ref[SWA_Q:D_MIX, :]))
    h_ref[...] = h
    xn = _rms(h, gf_ref[...])
    xn_ref[...] = _pack_halves(xn)
    logits = _dot_nt(wr_ref[...], xn.astype(BF16))

    row = lax.broadcasted_iota(I32, (rows, tm), 0)
    bias = br_ref[...]
    top = lambda v: jnp.max(v, axis=0, keepdims=True)
    tot = lambda v: jnp.sum(v, axis=0, keepdims=True)
    first_at = lambda v: jnp.min(jnp.where(v == top(v), row, 2 * rows), axis=0, keepdims=True)
    is_g = (row >= N_EXPERTS) & (row < N_EXPERTS + N_GROUPS)
    lg = jnp.where(is_g, logits, NEG_BIG)
    pg = jnp.where(is_g, jnp.exp(lg - top(lg)), 0.0)
    group_p = pg / tot(pg)
    g_row = first_at(jnp.where(is_g, group_p + bias, NEG_BIG))
    g_w = tot(jnp.where(row == g_row, group_p, 0.0))
    sel = (row < N_EXPERTS) & ((row // EXPERTS_PER_GROUP) == (g_row - N_EXPERTS))
    le = jnp.where(sel, logits, NEG_BIG)
    pe = jnp.where(sel, jnp.exp(le - top(le)), 0.0)
    e_p = pe / tot(pe)
    score = jnp.where(sel, e_p + bias, NEG_BIG)
    i1 = first_at(score)
    i2 = first_at(jnp.where(row == i1, NEG_BIG, score))
    oh1 = row == i1
    oh2 = row == i2
    w1 = tot(jnp.where(oh1, e_p, 0.0))
    w2 = tot(jnp.where(oh2, e_p, 0.0))
    wsum = w1 + w2

    ohs = (oh1 | oh2).astype(BF16)
    earlier = (lax.broadcasted_iota(I32, (tm, tm), 0) < lax.broadcasted_iota(I32, (tm, tm), 1)).astype(BF16)
    before = _dot(ohs, earlier) + run_ref[...]
    r1 = tot(jnp.where(oh1, before, 0.0))
    r2 = tot(jnp.where(oh2, before, 0.0))
    run_ref[...] = run_ref[...] + jnp.sum(ohs.astype(F32), axis=1, keepdims=True)
    cnt_ref[...] = run_ref[...]

    zero = jnp.zeros_like(w1)
    meta = jnp.concatenate([i1.astype(F32), i2.astype(F32), r1, r2, g_w * (w1 / wsum), g_w * (w2 / wsum),
                            zero, zero], axis=0)
    meta_ref[...] = meta
    eye8 = (lax.broadcasted_iota(I32, (8, LANES), 0) == lax.broadcasted_iota(I32, (8, LANES), 1)).astype(F32)
    gates_ref[...] = sum(_dot_tn(part, eye8) for part in _split3(meta))[:, 4:6]


def _outproj(x2d, o_swa, o_gdn, wo_bf16, gain, w_router, b_router):
    n = x2d.shape[0]
    tm = ROW_TILE
    row = lambda w: pl.BlockSpec((tm, w), lambda i: (i, 0))
    full = lambda a: pl.BlockSpec(a.shape, lambda i: (0,) * a.ndim)
    return pl.pallas_call(
        _outproj_kernel,
        grid=(n // tm,),
        in_specs=[row(D_MODEL), row(SWA_Q), row(GDN_V), full(wo_bf16), full(gain), full(w_router), full(b_router)],
        out_specs=[row(D_MODEL), row(PACKED_W), row(2), pl.BlockSpec((8, tm), lambda i: (0, i)),
                   pl.BlockSpec((LANES, 1), lambda i: (0, 0))],
        out_shape=[jax.ShapeDtypeStruct((n, D_MODEL), F32), jax.ShapeDtypeStruct((n, PACKED_W), U32),
                   jax.ShapeDtypeStruct((n, 2), F32), jax.ShapeDtypeStruct((8, n), F32),
                   jax.ShapeDtypeStruct((LANES, 1), F32)],
        scratch_shapes=[pltpu.VMEM((LANES, 1), F32)],
        compiler_params=pltpu.CompilerParams(dimension_semantics=("arbitrary",)),
        name="outproj_router",
    )(x2d, o_swa, o_gdn, wo_bf16, gain, w_router, b_router)


def _row_copy(src_ref, src_row, dst_ref, dst_row, sem):
    return pltpu.make_async_copy(src_ref.at[pl.ds(src_row, 1)], dst_ref.at[pl.ds(dst_row, 1)], sem)


def _scatter_kernel(dest0_ref, dest1_ref, xp_ref, xs_ref, out_ref, aidx_ref, sem, *, tiles_p, n_tokens):
    i = pl.program_id(0)
    tm = xp_ref.shape[0]

    def run(src_ref):
        def issue(g, carry):
            for u in range(DMA_UNROLL):
                r = g * DMA_UNROLL + u
                d0, d1 = dest0_ref[r], dest1_ref[r]
                _row_copy(src_ref, r, out_ref, d0, sem).start(priority=0)
                _row_copy(src_ref, r, out_ref, d1, sem).start(priority=1)
                aidx_ref[d0] = i * tm + r
                aidx_ref[d1] = n_tokens + i * tm + r
            return carry

        lax.fori_loop(0, tm // DMA_UNROLL, issue, 0)
        for _ in range(2):
            pltpu.make_async_copy(src_ref, out_ref.at[pl.ds(0, tm)], sem).wait()

    @pl.when(i < tiles_p)
    def _():
        run(xp_ref)

    @pl.when(i >= tiles_p)
    def _():
        run(xs_ref)


def _scatter(dest, xn_p, xn_s):
    tm = ROW_TILE
    tiles_p, tiles_s = xn_p.shape[0] // tm, xn_s.shape[0] // tm
    rows = 2 * (xn_p.shape[0] + xn_s.shape[0])
    idx = pl.BlockSpec((tm,), lambda i: (i,), memory_space=pltpu.SMEM)
    return pl.pallas_call(
        functools.partial(_scatter_kernel, tiles_p=tiles_p, n_tokens=rows // 2),
        grid=(tiles_p + tiles_s,),
        in_specs=[idx, idx,
                  pl.BlockSpec((tm, PACKED_W), lambda i: (jnp.minimum(i, tiles_p - 1), 0)),
                  pl.BlockSpec((tm, PACKED_W), lambda i: (jnp.maximum(i - tiles_p, 0), 0))],
        out_specs=[pl.BlockSpec(memory_space=pl.ANY), pl.BlockSpec(memory_space=pltpu.SMEM)],
        out_shape=[jax.ShapeDtypeStruct((rows, PACKED_W), U32), jax.ShapeDtypeStruct((rows,), I32)],
        scratch_shapes=[pltpu.SemaphoreType.DMA(())],
        compiler_params=pltpu.CompilerParams(dimension_semantics=("arbitrary",)),
        name="scatter_rows",
    )(dest[0], dest[1], xn_p, xn_s)


def _experts_kernel(blk_ref, exp_ref, lo_ref, hi_ref, first_ref, fresh_ref, pblk_ref,
                    aprev_ref, x_ref, wg_ref, wu_ref, wd_ref, ya_ref, wg_l, wu_l, wd_l, ybuf, sem,
                    *, n_items, n_blocks):
    j = pl.program_id(0)
    lo, hi = lo_ref[j], hi_ref[j]
    slot = lax.rem(blk_ref[j], 2)
    quarter = MOE_BLOCK // 4

    def wait_rows(s):
        pltpu.make_async_copy(ybuf.at[s], ya_ref.at[pl.ds(0, MOE_BLOCK)], sem.at[s]).wait()

    def send_rows(s, group):
        for row in range(group * quarter, (group + 1) * quarter):
            _row_copy(ybuf.at[s], row, ya_ref, aprev_ref[0, 0, row], sem.at[s]).start(priority=row % 2)

    @pl.when(fresh_ref[j] == 1)
    def _():
        wg_l[...] = wg_ref[0].astype(BF16)
        wu_l[...] = wu_ref[0].astype(BF16)
        wd_l[...] = wd_ref[0].astype(BF16)

    @pl.when((first_ref[j] == 1) & (blk_ref[j] >= 2))
    def _():
        wait_rows(slot)

    def item(is_first, send_prev):
        send = (lambda group: send_rows(1 - slot, group)) if send_prev else (lambda group: None)
        send(0)
        x_a, x_b = _unpack_halves(x_ref[...])
        gate = _dot(x_a, wg_l[0:PACKED_W, :]) + _dot(x_b, wg_l[PACKED_W:D_MODEL, :])
        send(1)
        up = _dot(x_a, wu_l[0:PACKED_W, :]) + _dot(x_b, wu_l[PACKED_W:D_MODEL, :])
        hid = (gate * jax.nn.sigmoid(gate)) * up
        send(2)
        y = _dot(hid.astype(BF16), wd_l[...])
        send(3)
        r = lax.broadcasted_iota(I32, (MOE_BLOCK, 1), 0)
        mine = (r >= lo) & (r < hi)
        ybuf[slot] = jnp.where(mine, y, 0.0 if is_first else ybuf[slot])

    live = hi > lo
    pl.when(live & (first_ref[j] == 1) & (blk_ref[j] >= 1))(functools.partial(item, True, True))
    pl.when(live & (first_ref[j] == 1) & (blk_ref[j] == 0))(functools.partial(item, True, False))
    pl.when(live & (first_ref[j] == 0))(functools.partial(item, False, False))

    @pl.when(j == n_items - 1)
    def _():
        last_slot = (n_blocks - 1) % 2
        for group in range(4):
            send_rows(last_slot, group)
        wait_rows(1 - last_slot)
        wait_rows(last_slot)


def _experts(items, aidx, xs, w_gate, w_up, w_down):
    n_items = items[0].shape[0]
    n_blocks = xs.shape[0] // MOE_BLOCK
    assert n_items > n_blocks + N_EXPERTS - 1 and n_blocks >= 2
    xblk = pl.BlockSpec((MOE_BLOCK, PACKED_W), lambda j, blk, *_: (blk[j], 0))
    wspec = lambda a: pl.BlockSpec((1,) + a.shape[1:], lambda j, blk, ex, *_: (ex[j], 0, 0))
    return pl.pallas_call(
        functools.partial(_experts_kernel, n_items=n_items, n_blocks=n_blocks),
        grid_spec=pltpu.PrefetchScalarGridSpec(
            num_scalar_prefetch=len(items),
            grid=(n_items,),
            in_specs=[pl.BlockSpec((1, 1, MOE_BLOCK), lambda j, *pre: (pre[-1][j], 0, 0), memory_space=pltpu.SMEM),
                      xblk, wspec(w_gate), wspec(w_up), wspec(w_down)],
            out_specs=pl.BlockSpec(memory_space=pl.ANY),
            scratch_shapes=[pltpu.VMEM(w_gate.shape[1:], BF16), pltpu.VMEM(w_up.shape[1:], BF16),
                            pltpu.VMEM(w_down.shape[1:], BF16), pltpu.VMEM((2, MOE_BLOCK, D_MODEL), F32),
                            pltpu.SemaphoreType.DMA((2,))]),
        out_shape=jax.ShapeDtypeStruct((xs.shape[0], D_MODEL), F32),
        compiler_params=pltpu.CompilerParams(dimension_semantics=("arbitrary",)),
        name="experts",
    )(*items, aidx.reshape(n_blocks, 1, MOE_BLOCK), xs, w_gate, w_up, w_down)


def _work_items(counts, total_rows):
    n_blocks = total_rows // MOE_BLOCK
    n_items = n_blocks + N_EXPERTS
    end = jnp.cumsum(counts)
    start = end - counts
    first_blk = start // MOE_BLOCK
    nb = jnp.where(counts > 0, (end - 1) // MOE_BLOCK - first_blk + 1, 0)
    item_end = jnp.cumsum(nb)
    used = item_end[-1]
    j = jnp.arange(n_items, dtype=I32)
    jj = jnp.minimum(j, used - 1)
    e = jnp.minimum(jnp.sum((item_end[None, :] <= jj[:, None]).astype(I32), axis=1), N_EXPERTS - 1)
    onehot = (e[:, None] == jnp.arange(N_EXPERTS, dtype=I32)[None, :]).astype(I32)
    pick = lambda a: jnp.sum(onehot * a[None, :], axis=1)
    blk = pick(first_blk) + (jj - (pick(item_end) - pick(nb)))
    lo = jnp.maximum(pick(start), blk * MOE_BLOCK) - blk * MOE_BLOCK
    hi = jnp.minimum(pick(end), (blk + 1) * MOE_BLOCK) - blk * MOE_BLOCK
    live = j < used
    lo = jnp.where(live, lo, 0)
    hi = jnp.where(live, hi, 0)
    prev_blk = jnp.concatenate([jnp.full((1,), -1, I32), blk[:-1]])
    first = (live & (blk != prev_blk)).astype(I32)
    prev_e = jnp.concatenate([jnp.full((1,), -1, I32), e[:-1]])
    fresh = (live & (e != prev_e)).astype(I32)
    pblk = jnp.where(j == n_items - 1, n_blocks - 1, jnp.maximum(blk - 1, 0))
    return (blk.astype(I32), e.astype(I32), lo.astype(I32), hi.astype(I32), first, fresh, pblk.astype(I32)), start


def _combine_kernel(gates_ref, h_ref, gf_ref, y0_ref, y1_ref, out_ref):
    gates = gates_ref[...]
    y = h_ref[...] + (y0_ref[...] * gates[:, 0:1] + y1_ref[...] * gates[:, 1:2])
    out_ref[...] = _rms(y, gf_ref[...])


def _combine(gates, h, gain, ya, first_token):
    n = h.shape[0]
    tm = COMBINE_TILE
    tile0 = first_token // tm
    slot_tiles = ya.shape[0] // 2 // tm
    row = lambda w: pl.BlockSpec((tm, w), lambda i: (i, 0))
    slot = lambda k: pl.BlockSpec((tm, D_MODEL), lambda i: (k * slot_tiles + tile0 + i, 0))
    return pl.pallas_call(
        _combine_kernel,
        grid=(n // tm,),
        in_specs=[row(2), row(D_MODEL), pl.BlockSpec(gain.shape, lambda i: (0, 0)), slot(0), slot(1)],
        out_specs=row(D_MODEL),
        out_shape=jax.ShapeDtypeStruct((n, D_MODEL), F32),
        compiler_params=pltpu.CompilerParams(dimension_semantics=("arbitrary",)),
        name="combine_norm",
    )(gates, h, gain, ya, ya)


def kernel(x_prompt, x_sample, cache_swa_k, cache_swa_v, state_gdn_conv, state_gdn, norm_mix, w_in, swa_sinks,
           gdn_conv_w, gdn_A_log, gdn_dt_bias, gdn_norm_w, w_out, norm_ffn, w_router_group, b_router_group,
           w_router_expert, b_router_expert, w_exp_gate, w_exp_up, w_exp_down, norm_final):
    depth = w_in.shape[0]
    assert depth == 1, "single trunk layer"
    bp, sp, _ = x_prompt.shape
    bs, ts, _ = x_sample.shape
    np_, ns = bp * sp, bs * ts
    l = 0

    w_in_b = w_in[l].astype(BF16)
    w_out_b = w_out[l].astype(BF16)
    g_mix = norm_mix[l].reshape(1, D_MODEL)
    g_ffn = norm_ffn[l].reshape(1, D_MODEL)
    g_fin = norm_final.reshape(1, D_MODEL)
    pad = LANES - N_EXPERTS - N_GROUPS
    w_router = jnp.concatenate([w_router_expert[l], w_router_group[l], jnp.zeros((D_MODEL, pad), F32)],
                               axis=1).astype(BF16).T
    b_router = jnp.concatenate([b_router_expert[l].reshape(-1), b_router_group[l],
                                jnp.zeros((pad,), F32)])[:, None]
    zeros8 = jnp.zeros((GDN_HEADS,), F32)
    alog16 = jnp.concatenate([gdn_A_log[l], zeros8])[None]
    dtb16 = jnp.concatenate([gdn_dt_bias[l], zeros8])[None]
    nw_group = jnp.tile(gdn_norm_w[l].reshape(1, HEAD_DIM), (1, GDN_GROUP))
    sinks = swa_sinks[l]

    q_p, k_p, v_p, c_p, z_p, ab_p = _inproj(x_prompt.reshape(np_, D_MODEL), g_mix, w_in_b)
    nblk = np_ // WINDOW
    k_p3 = k_p.reshape(nblk, WINDOW, SWA_KV)
    v_p3 = v_p.reshape(nblk, WINDOW, SWA_KV)
    (o_swa_p,) = _swa(sinks, q_p.reshape(nblk, WINDOW, SWA_Q), k_p3, v_p3, k_p3, v_p3,
                      bb=SWA_BLOCKS, blocks_per_seq=sp // WINDOW, emit_cache=False)
    c_p3 = c_p.reshape(bp, sp, GDN_CONV_CH)
    o_gdn_p, s_fin_p = _gdn(c_p3, jnp.zeros((bp, CONV_W - 1, GDN_CONV_CH), F32), z_p.reshape(bp, sp, GDN_V),
                            ab_p.reshape(bp, sp, 2 * GDN_HEADS),
                            jnp.zeros((bp, GDN_HEADS, HEAD_DIM, HEAD_DIM), F32),
                            gdn_conv_w[l], alog16, dtb16, nw_group, chunk=GDN_CHUNK, seq_block=bp)
    h_p, xn_p, gates_p, meta_p, cnt_p = _outproj(
        x_prompt.reshape(np_, D_MODEL), o_swa_p.reshape(np_, SWA_Q), o_gdn_p.reshape(np_, GDN_V),
        w_out_b, g_ffn, w_router, b_router)

    q_s, k_s, v_s, c_s, z_s, ab_s = _inproj(x_sample.reshape(ns, D_MODEL), g_mix, w_in_b)
    o_swa_s, kcache_s, vcache_s = _swa(
        sinks, q_s.reshape(bs, ts, SWA_Q), k_s.reshape(bs, ts, SWA_KV), v_s.reshape(bs, ts, SWA_KV),
        cache_swa_k[l].reshape(bs, WINDOW, SWA_KV), cache_swa_v[l].reshape(bs, WINDOW, SWA_KV),
        bb=16, blocks_per_seq=None, emit_cache=True)
    c_s3 = c_s.reshape(bs, ts, GDN_CONV_CH)
    o_gdn_s, s_fin_s = _gdn(c_s3, state_gdn_conv[l], z_s.reshape(bs, ts, GDN_V),
                            ab_s.reshape(bs, ts, 2 * GDN_HEADS), state_gdn[l],
                            gdn_conv_w[l], alog16, dtb16, nw_group, chunk=ts, seq_block=16)
    h_s, xn_s, gates_s, meta_s, cnt_s = _outproj(
        x_sample.reshape(ns, D_MODEL), o_swa_s.reshape(ns, SWA_Q), o_gdn_s.reshape(ns, GDN_V),
        w_out_b, g_ffn, w_router, b_router)

    cnt_p_i = cnt_p[:N_EXPERTS, 0].astype(I32)
    cnt_s_i = cnt_s[:N_EXPERTS, 0].astype(I32)
    items, start = _work_items(cnt_p_i + cnt_s_i, 2 * (np_ + ns))
    expert_ids = jnp.arange(N_EXPERTS, dtype=I32)
    lookup = lambda table, ids: jnp.sum(jnp.where(ids[..., None] == expert_ids, table, 0), axis=-1)
    dest_p = lookup(start, meta_p[0:2].astype(I32)) + meta_p[2:4].astype(I32)
    dest_s = lookup(start + cnt_p_i, meta_s[0:2].astype(I32)) + meta_s[2:4].astype(I32)
    xs, aidx = _scatter(jnp.concatenate([dest_p, dest_s], axis=1), xn_p, xn_s)
    ya = _experts(items, aidx, xs, w_exp_gate[l], w_exp_up[l], w_exp_down[l])
    y_p = _combine(gates_p, h_p, g_fin, ya, 0)
    y_s = _combine(gates_s, h_s, g_fin, ya, np_)

    kv5 = lambda a, b: a.reshape(b, -1, SWA_KV_HEADS, HEAD_DIM)[None]
    return (y_p.reshape(bp, sp, D_MODEL), y_s.reshape(bs, ts, D_MODEL),
            kv5(k_p.reshape(bp, sp, SWA_KV)[:, -WINDOW:], bp), kv5(v_p.reshape(bp, sp, SWA_KV)[:, -WINDOW:], bp),
            kv5(kcache_s, bs), kv5(vcache_s, bs),
            c_p3[:, -(CONV_W - 1):][None], c_s3[:, -(CONV_W - 1):][None],
            s_fin_p[None], s_fin_s[None])
```

```python
import functools

import jax
import jax.numpy as jnp
from jax import lax
from jax.experimental import pallas as pl
from jax.experimental.pallas import tpu as pltpu

F32 = jnp.float32
BF16 = jnp.bfloat16
I32 = jnp.int32
U32 = jnp.uint32

D_MODEL = 1024
HEAD_DIM = 64
SWA_HEADS = 8
GDN_HEADS = 8
SWA_KV_HEADS = 2
GQA_GROUP = SWA_HEADS // SWA_KV_HEADS
WINDOW = 128
ATTN_SCALE = HEAD_DIM ** -0.5
CONV_W = 4
N_GROUPS = 8
EXPERTS_PER_GROUP = 8
N_EXPERTS = 64
D_EXPERT = 256
EPS = 1e-6

SWA_Q = SWA_HEADS * HEAD_DIM
SWA_KV = SWA_KV_HEADS * HEAD_DIM
GDN_QK = GDN_HEADS * HEAD_DIM
GDN_V = GDN_HEADS * HEAD_DIM
GDN_CONV_CH = 2 * GDN_QK + GDN_V
D_MIX = SWA_Q + GDN_V
D_IN = SWA_Q + 2 * SWA_KV + GDN_CONV_CH + GDN_V + 2 * GDN_HEADS
COL_K = SWA_Q
COL_V = COL_K + SWA_KV
COL_C = COL_V + SWA_KV
COL_Z = COL_C + GDN_CONV_CH
COL_AB = COL_Z + GDN_V
PACKED_W = D_MODEL // 2

LANES = 128
NEG_BIG = -1e30
ROW_TILE = 512
INPROJ_TILE = 1024
MOE_BLOCK = 256
COMBINE_TILE = 512
DMA_UNROLL = 8
SWA_BLOCKS = 4
GDN_CHUNK = 64
GDN_TILE = 256
GDN_GROUP = 4
GDN_GROUP_W = GDN_GROUP * HEAD_DIM


def _rms(x, g):
    return x * lax.rsqrt(jnp.mean(x * x, axis=-1, keepdims=True) + EPS) * g


def _dot(a, b):
    return jnp.dot(a, b, preferred_element_type=F32)


def _dot_nt(a, b):
    return lax.dot_general(a, b, (((1,), (1,)), ((), ())), preferred_element_type=F32)


def _dot_tn(a, b):
    return lax.dot_general(a, b, (((0,), (0,)), ((), ())), preferred_element_type=F32)


def _pack_halves(x):
    w = x.shape[1] // 2
    bits = lambda v: lax.bitcast_convert_type(v.astype(BF16).astype(F32), U32)
    return bits(x[:, :w]) | (bits(x[:, w:]) >> 16)


def _unpack_halves(p):
    hi = lax.bitcast_convert_type(p & jnp.uint32(0xFFFF0000), F32)
    lo = lax.bitcast_convert_type(p << 16, F32)
    return hi.astype(BF16), lo.astype(BF16)


def _split3(x):
    p1 = x.astype(BF16).astype(F32)
    r = x - p1
    p2 = r.astype(BF16).astype(F32)
    p3 = (r - p2).astype(BF16).astype(F32)
    return p1, p2, p3


def _inproj_kernel(x_ref, g_ref, w_ref, q_ref, k_ref, v_ref, c_ref, z_ref, ab_ref):
    x = x_ref[...]
    xb = _rms(x, g_ref[...]).astype(BF16)
    q_ref[...] = _dot(xb, w_ref[:, 0:COL_K])
    k_ref[...] = _dot(xb, w_ref[:, COL_K:COL_V])
    v_ref[...] = _dot(xb, w_ref[:, COL_V:COL_C])
    c_ref[...] = _dot(xb, w_ref[:, COL_C:COL_Z])
    z_ref[...] = _dot(xb, w_ref[:, COL_Z:COL_AB])
    ab_ref[...] = _dot(xb, w_ref[:, COL_AB:D_IN])


def _inproj(x2d, gain, w_bf16):
    n = x2d.shape[0]
    tm = INPROJ_TILE
    row = lambda w: pl.BlockSpec((tm, w), lambda i: (i, 0))
    full = lambda a: pl.BlockSpec(a.shape, lambda i: (0,) * a.ndim)
    widths = (SWA_Q, SWA_KV, SWA_KV, GDN_CONV_CH, GDN_V, 2 * GDN_HEADS)
    return pl.pallas_call(
        _inproj_kernel,
        grid=(n // tm,),
        in_specs=[row(D_MODEL), full(gain), full(w_bf16)],
        out_specs=[row(w) for w in widths],
        out_shape=[jax.ShapeDtypeStruct((n, w), F32) for w in widths],
        compiler_params=pltpu.CompilerParams(dimension_semantics=("arbitrary",)),
        name="inproj",
    )(x2d, gain, w_bf16)


def _swa_kernel(sink_ref, q_ref, kc_ref, vc_ref, kp_ref, vp_ref, o_ref, *cache_refs,
                bb, t, blocks_per_seq, emit_cache):
    rows = GQA_GROUP * t
    ri = lax.broadcasted_iota(I32, (rows, 1), 0)
    qi = lax.rem(ri, t)
    gi = ri // t
    if emit_cache:
        kj = lax.broadcasted_iota(I32, (rows, WINDOW), 1)
        mask = kj <= qi + (WINDOW - t)
        mask_first = mask
        place = (lax.broadcasted_iota(I32, (t, WINDOW), 1)
                 == lax.broadcasted_iota(I32, (t, WINDOW), 0) + (WINDOW - t)).astype(F32)
        is_new = lax.broadcasted_iota(I32, (SWA_KV, WINDOW), 1) >= WINDOW - t
    else:
        kj = lax.broadcasted_iota(I32, (rows, WINDOW + t), 1)
        mask = (kj <= qi + WINDOW) & (kj > qi)
        has_prev = lax.rem(pl.program_id(0) * bb, blocks_per_seq) != 0
        mask_first = mask & (has_prev | (kj >= WINDOW))
    sinks = []
    for h in range(SWA_KV_HEADS):
        sink = jnp.zeros((rows, 1), F32)
        for g in range(GQA_GROUP):
            sink = jnp.where(gi == g, sink_ref[GQA_GROUP * h + g], sink)
        sinks.append(sink)
    if emit_cache:
        def updated(old_t, new):
            placed = sum(_dot_tn(part, place) for part in _split3(new))
            return jnp.where(is_new, placed, pltpu.roll(old_t, WINDOW - t, axis=1))

        caches = [(updated(kp_ref[b], kc_ref[b]), updated(vp_ref[b], vc_ref[b])) for b in range(bb)]
        for b in range(bb):
            cache_refs[0][b] = caches[b][0]
            cache_refs[1][b] = caches[b][1]
    chains = [(b, h) for b in range(bb) for h in range(SWA_KV_HEADS)]
    scores, values = [], []
    for b, h in chains:
        hs = slice(h * HEAD_DIM, (h + 1) * HEAD_DIM)
        q4 = jnp.concatenate(
            [q_ref[b, :, (GQA_GROUP * h + g) * HEAD_DIM:(GQA_GROUP * h + g + 1) * HEAD_DIM]
             for g in range(GQA_GROUP)], axis=0)
        if emit_cache:
            keys_t, vals_t = caches[b][0][hs, :], caches[b][1][hs, :]
            scores.append(_dot(q4.astype(BF16), keys_t.astype(BF16)))
            values.append(vals_t.astype(BF16))
        else:
            kp, vp = (kp_ref[0, :, hs], vp_ref[0, :, hs]) if b == 0 else (kc_ref[b - 1, :, hs], vc_ref[b - 1, :, hs])
            keys = jnp.concatenate([kp, kc_ref[b, :, hs]], axis=0)
            vals = jnp.concatenate([vp, vc_ref[b, :, hs]], axis=0)
            scores.append(_dot_nt(q4.astype(BF16), keys.astype(BF16)))
            values.append(vals.astype(BF16))
    probs, dens = [], []
    for (b, h), s in zip(chains, scores):
        s = jnp.where(mask_first if b == 0 else mask, s * ATTN_SCALE, NEG_BIG)
        m = jnp.maximum(jnp.max(s, axis=-1, keepdims=True), sinks[h])
        p = jnp.exp(s - m)
        dens.append(jnp.sum(p, axis=-1, keepdims=True) + jnp.exp(sinks[h] - m))
        probs.append(p.astype(BF16))
    pv = _dot_nt if emit_cache else _dot
    outs = [pv(p, v) / den for p, v, den in zip(probs, values, dens)]
    for b in range(bb):
        o_ref[b] = jnp.concatenate([outs[b * SWA_KV_HEADS + h][g * t:(g + 1) * t]
                                    for h in range(SWA_KV_HEADS) for g in range(GQA_GROUP)], axis=-1)


def _swa(sinks, q3, k3, v3, kprev3, vprev3, *, bb, blocks_per_seq, emit_cache):
    nb, t, _ = q3.shape
    cur = lambda w: pl.BlockSpec((bb, t, w), lambda i: (i, 0, 0))
    if emit_cache:
        prev = pl.BlockSpec((bb, SWA_KV, WINDOW), lambda i: (i, 0, 0))
    else:
        assert blocks_per_seq % bb == 0
        prev = pl.BlockSpec((1, WINDOW, SWA_KV), lambda i: (jnp.maximum(i * bb - 1, 0), 0, 0))
    out_specs = [cur(SWA_Q)]
    out_shape = [jax.ShapeDtypeStruct((nb, t, SWA_Q), F32)]
    if emit_cache:
        out_specs += [prev, prev]
        out_shape += [jax.ShapeDtypeStruct((nb, SWA_KV, WINDOW), F32)] * 2
    return pl.pallas_call(
        functools.partial(_swa_kernel, bb=bb, t=t, blocks_per_seq=blocks_per_seq, emit_cache=emit_cache),
        grid=(nb // bb,),
        in_specs=[pl.BlockSpec(memory_space=pltpu.SMEM), cur(SWA_Q), cur(SWA_KV), cur(SWA_KV), prev, prev],
        out_specs=out_specs,
        out_shape=out_shape,
        compiler_params=pltpu.CompilerParams(dimension_semantics=("arbitrary",)),
        name="swa_cache" if emit_cache else "swa_band",
    )(sinks, q3, k3, v3, kprev3, vprev3)


def _gdn_prep_kernel(c_ref, hist_ref, ab_ref, cw_ref, alog_ref, dtb_ref,
                     u_ref, w_ref, qd_ref, kd_ref, qk_ref, gt_ref, cbuf_ref, *, chunk):
    sb, r, _ = c_ref.shape
    tp = sb * r
    cn = chunk
    low = w_ref.dtype
    hist_rows = CONV_W - 1

    @pl.when(pl.program_id(1) == 0)
    def _():
        cbuf_ref[:, 8 - hist_rows:8, :] = hist_ref[...]

    cbuf_ref[:, 8:8 + r, :] = c_ref[...]
    conv = cbuf_ref[:, 8 - hist_rows:8 - hist_rows + r, :] * cw_ref[0:1, :]
    for i in range(1, CONV_W):
        conv = conv + cbuf_ref[:, 8 - hist_rows + i:8 - hist_rows + i + r, :] * cw_ref[i:i + 1, :]
    tail = cbuf_ref[:, 8 + r - hist_rows:8 + r, :]
    cbuf_ref[:, 8 - hist_rows:8, :] = tail
    conv = (conv * jax.nn.sigmoid(conv)).reshape(tp, GDN_CONV_CH)

    ab = ab_ref[...].reshape(tp, 2 * GDN_HEADS)
    is_g = lax.broadcasted_iota(I32, (1, 2 * GDN_HEADS), 1) < GDN_HEADS
    g = jnp.where(is_g, -jnp.exp(alog_ref[...]) * jax.nn.softplus(ab + dtb_ref[...]), 0.0)
    beta = jax.nn.sigmoid(ab)

    ri = lax.broadcasted_iota(I32, (tp, tp), 0)
    ci = lax.broadcasted_iota(I32, (tp, tp), 1)
    same = (ri // cn) == (ci // cn)
    causal = same & (ri >= ci)
    strict = same & (ri > ci)
    eye = (ri == ci).astype(F32)
    stack = jnp.concatenate([causal.astype(BF16), same.astype(BF16)], axis=0)
    both = sum(_dot(stack, p.astype(BF16)) for p in _split3(g))
    gc, gl = both[:tp], both[tp:]
    e16 = lax.broadcasted_iota(I32, (2 * GDN_HEADS, 2 * GDN_HEADS), 0)
    eye16 = (e16 == lax.broadcasted_iota(I32, (2 * GDN_HEADS, 2 * GDN_HEADS), 1)).astype(BF16)
    gc_row = sum(_dot_nt(eye16, p.astype(BF16)) for p in _split3(gc))
    gt_ref[...] = jnp.exp(gl)
    fold = (lax.broadcasted_iota(I32, (tp, cn), 0) % cn == lax.broadcasted_iota(I32, (tp, cn), 1)).astype(low)

    levels = cn.bit_length() - 2
    heads = range(GDN_HEADS)
    lanes_of = lambda i, h: slice(i * GDN_QK + h * HEAD_DIM, i * GDN_QK + (h + 1) * HEAD_DIM)
    qs = [conv[:, lanes_of(0, h)] for h in heads]
    ks = [conv[:, lanes_of(1, h)] for h in heads]
    qs = [q * lax.rsqrt(jnp.sum(q * q, axis=-1, keepdims=True) + EPS) * (HEAD_DIM ** -0.5) for q in qs]
    ks = [k * lax.rsqrt(jnp.sum(k * k, axis=-1, keepdims=True) + EPS) for k in ks]
    g_cs = [gc[:, h:h + 1] for h in heads]
    b_hs = [beta[:, GDN_HEADS + h:GDN_HEADS + h + 1] for h in heads]
    decays = [jnp.exp(jnp.where(causal, g_cs[h] - gc_row[h:h + 1, :], NEG_BIG)) for h in heads]
    egs = [jnp.exp(g_c) for g_c in g_cs]
    kbs = [k * b_h for k, b_h in zip(ks, b_hs)]
    k_ls = [k.astype(low) for k in ks]
    lmats = [jnp.where(strict, _dot_nt(kb.astype(low), k_l) * decay, 0.0) for kb, k_l, decay in zip(kbs, k_ls, decays)]
    qks = [_dot((_dot_nt(q.astype(low), k_l) * decay).astype(low), fold) for q, k_l, decay in zip(qs, k_ls, decays)]
    xs = [eye - lmat for lmat in lmats]
    ps = [_dot(lmat.astype(low), lmat.astype(low)) for lmat in lmats]
    for lev in range(1, levels + 1):
        p_ls = [p.astype(low) for p in ps]
        if lev < levels:
            xp = [_dot(jnp.concatenate([x.astype(low), p_l], axis=0), p_l) for x, p_l in zip(xs, p_ls)]
            xs = [x + m[:tp] for x, m in zip(xs, xp)]
            ps = [m[tp:] for m in xp]
        else:
            xs = [x + _dot(x.astype(low), p_l) for x, p_l in zip(xs, p_ls)]
    sols = [_dot(xs[h].astype(low),
                 jnp.concatenate([conv[:, lanes_of(2, h)] * b_hs[h], kbs[h] * egs[h]], axis=-1).astype(low))
            for h in heads]
    u_ref[...] = jnp.concatenate([sol[:, :HEAD_DIM] for sol in sols], axis=-1)
    w_ref[...] = jnp.concatenate([sol[:, HEAD_DIM:] for sol in sols], axis=-1).astype(low)
    qd_ref[...] = jnp.concatenate([q * eg for q, eg in zip(qs, egs)], axis=-1).astype(low)
    kd_ref[...] = jnp.concatenate([ks[h] * jnp.exp(gl[:, h:h + 1] - g_cs[h]) for h in heads], axis=-1).astype(low)
    qk_ref[...] = jnp.concatenate(qks, axis=-1).astype(low)


def _gdn_prep_pair_kernel(c_ref, hist_ref, ab_ref, cw_ref, alog_ref, dtb_ref,
                          u_ref, w_ref, qd_ref, kd_ref, qk_ref, gt_ref, cbuf_ref):
    _, r, _ = c_ref.shape
    tp = r
    cn = HEAD_DIM
    pair_w = 2 * HEAD_DIM
    hist_rows = CONV_W - 1
    sublanes = 8

    @pl.when(pl.program_id(1) == 0)
    def _():
        cbuf_ref[...] = jnp.zeros_like(cbuf_ref)
        cbuf_ref[:, sublanes - hist_rows:sublanes, :] = hist_ref[...]

    x3 = c_ref[0].reshape(tp // sublanes, sublanes, GDN_CONV_CH)
    before = cbuf_ref[...]
    sub = lax.broadcasted_iota(I32, (1, sublanes, 1), 1)
    conv3 = x3 * cw_ref[CONV_W - 1:CONV_W, :]
    for s in range(1, CONV_W):
        rx = pltpu.roll(x3, s, axis=1)
        rp = jnp.concatenate([pltpu.roll(before, s, axis=1), rx[:-1]], axis=0)
        conv3 = conv3 + jnp.where(sub < s, rp, rx) * cw_ref[CONV_W - 1 - s:CONV_W - s, :]
    cbuf_ref[...] = x3[tp // sublanes - 1:]
    conv = (conv3 * jax.nn.sigmoid(conv3)).reshape(tp, GDN_CONV_CH)

    ab = ab_ref[...].reshape(tp, 2 * GDN_HEADS)
    is_g = lax.broadcasted_iota(I32, (1, 2 * GDN_HEADS), 1) < GDN_HEADS
    g = jnp.where(is_g, -jnp.exp(alog_ref[...]) * jax.nn.softplus(ab + dtb_ref[...]), 0.0)
    beta = jax.nn.sigmoid(ab)

    ri = lax.broadcasted_iota(I32, (tp, tp), 0)
    ci = lax.broadcasted_iota(I32, (tp, tp), 1)
    same = (ri // cn) == (ci // cn)
    same_l = same.astype(BF16)
    stack = jnp.concatenate([(same & (ri >= ci)).astype(BF16), same_l], axis=0)
    both = sum(_dot(stack, p.astype(BF16)) for p in _split3(g))
    gc, gl = both[:tp], both[tp:]
    gt_ref[...] = jnp.exp(gl)

    lane = lax.broadcasted_iota(I32, (tp, pair_w), 1)
    c_in = lax.broadcasted_iota(I32, (tp, pair_w), 0) % cn
    j_in = lane % cn
    left = lane < cn
    causal = c_in >= j_in
    strict = c_in > j_in
    diag = c_in == j_in
    eye = diag.astype(F32)
    bdmask = ((lax.broadcasted_iota(I32, (pair_w, pair_w), 0) // cn)
              == (lax.broadcasted_iota(I32, (pair_w, pair_w), 1) // cn))
    ones_bd = bdmask.astype(BF16)

    def bd(m):
        return jnp.where(bdmask, jnp.concatenate([m, m], axis=0), 0.0).astype(BF16)

    def head_sum(x):
        hi = x.astype(BF16)
        lo = (x - hi.astype(F32)).astype(BF16)
        return _dot(hi, ones_bd) + _dot(lo, ones_bd)

    levels = cn.bit_length() - 2
    chunks = [slice(n * cn, (n + 1) * cn) for n in range(tp // cn)]
    n_pairs = GDN_HEADS // 2
    pairs = range(n_pairs)
    pick = lambda m, off, p: jnp.where(left, m[:, off + 2 * p:off + 2 * p + 1], m[:, off + 2 * p + 1:off + 2 * p + 2])
    third = lambda i, p: conv[:, i * GDN_QK + p * pair_w:i * GDN_QK + (p + 1) * pair_w]
    qs = [third(0, p) for p in pairs]
    ks = [third(1, p) for p in pairs]
    qs = [q * lax.rsqrt(head_sum(q * q) + EPS) * (HEAD_DIM ** -0.5) for q in qs]
    ks = [k * lax.rsqrt(head_sum(k * k) + EPS) for k in ks]
    gcps = [pick(gc, 0, p) for p in pairs]
    rowms = [sum(_dot(same_l, part.astype(BF16)) for part in _split3(jnp.where(diag, gcp, 0.0))) for gcp in gcps]
    decays = [jnp.exp(jnp.where(causal, gcp - rowm, NEG_BIG)) for gcp, rowm in zip(gcps, rowms)]
    lmats, vbs, kbegs = [], [], []
    for p in pairs:
        ls = slice(p * pair_w, (p + 1) * pair_w)
        q, k, gcp, decay = qs[p], ks[p], gcps[p], decays[p]
        bp = pick(beta, GDN_HEADS, p)
        eg = jnp.exp(gcp)
        kb = k * bp
        vbs.append(third(2, p) * bp)
        kbegs.append(kb * eg)
        qd_ref[:, ls] = (q * eg).astype(BF16)
        kd_ref[:, ls] = (k * jnp.exp(pick(gl, 0, p) - gcp)).astype(BF16)
        q_l, kb_l = q.astype(BF16), kb.astype(BF16)
        kbd = [bd(k[rs]) for rs in chunks]
        kk = jnp.concatenate([_dot_nt(kb_l[rs], kbd[n]) for n, rs in enumerate(chunks)], axis=0)
        qk = jnp.concatenate([_dot_nt(q_l[rs], kbd[n]) for n, rs in enumerate(chunks)], axis=0)
        qk_ref[:, ls] = (qk * decay).astype(BF16)
        lmats.append(jnp.where(strict, kk * decay, 0.0))
    bodies = [(p, rs) for p in range(n_pairs) for rs in chunks]
    xs = [eye[rs] - lmats[p][rs] for p, rs in bodies]
    ps = [_dot(lmats[p][rs].astype(BF16), bd(lmats[p][rs])) for p, rs in bodies]
    for lev in range(1, levels + 1):
        pbd = [bd(pm) for pm in ps]
        if lev < levels:
            xp = [_dot(jnp.concatenate([xm, pm], axis=0).astype(BF16), wm) for xm, pm, wm in zip(xs, ps, pbd)]
            xs = [xm + m[:cn] for xm, m in zip(xs, xp)]
            ps = [m[cn:] for m in xp]
        else:
            xs = [xm + _dot(xm.astype(BF16), wm) for xm, wm in zip(xs, pbd)]
    x_l = [xm.astype(BF16) for xm in xs]
    us = [_dot(xm, bd(vbs[p][rs])) for xm, (p, rs) in zip(x_l, bodies)]
    ws = [_dot(xm, bd(kbegs[p][rs])) for xm, (p, rs) in zip(x_l, bodies)]
    nc = len(chunks)
    for p in range(n_pairs):
        ls = slice(p * pair_w, (p + 1) * pair_w)
        u_ref[:, ls] = jnp.concatenate(us[p * nc:(p + 1) * nc], axis=0)
        w_ref[:, ls] = jnp.concatenate(ws[p * nc:(p + 1) * nc], axis=0).astype(BF16)


def _gdn_scan_kernel(u_ref, w_ref, qd_ref, kd_ref, qk_ref, gt_ref, z_ref, s0_ref, nw_ref,
                     o_ref, sfin_ref, sbd_ref, *, chunk, n_chunks):
    bb = u_ref.shape[0]
    cn = chunk
    low = w_ref.dtype
    gw = GDN_GROUP_W
    ni = pl.program_id(1)

    @pl.when(ni == 0)
    def _():
        sbd_ref[...] = jnp.zeros_like(sbd_ref)
        for b in range(bb):
            for h in range(GDN_HEADS):
                gi, hh = divmod(h, GDN_GROUP)
                ds = slice(hh * HEAD_DIM, (hh + 1) * HEAD_DIM)
                sbd_ref[b, gi, ds, ds] = s0_ref[b, h]

    bdmask = ((lax.broadcasted_iota(I32, (gw, gw), 0) // HEAD_DIM)
              == (lax.broadcasted_iota(I32, (gw, gw), 1) // HEAD_DIM))
    ones_bd = bdmask.astype(BF16)
    vmask = ((lax.broadcasted_iota(I32, (GDN_GROUP * cn, gw), 0) // cn)
             == (lax.broadcasted_iota(I32, (GDN_GROUP * cn, gw), 1) // HEAD_DIM))
    e_row = lax.broadcasted_iota(I32, (2 * GDN_HEADS, gw), 0)
    e_col = lax.broadcasted_iota(I32, (2 * GDN_HEADS, gw), 1) // HEAD_DIM
    chains = [(b, gi) for b in range(bb) for gi in range(GDN_HEADS // GDN_GROUP)]
    lanes = lambda gi: slice(gi * gw, (gi + 1) * gw)
    states = [sbd_ref[b, gi] for b, gi in chains]
    states_l = [s.astype(low) for s in states]
    wq_s = [_dot(jnp.concatenate([w_ref[b, :, lanes(gi)], qd_ref[b, :, lanes(gi)]], axis=0), s_l)
            for (b, gi), s_l in zip(chains, states_l)]
    v_new = [u_ref[b, :, lanes(gi)] - m[:cn] for (b, gi), m in zip(chains, wq_s)]
    q_s = [m[cn:] for m in wq_s]
    v_l = [v.astype(low) for v in v_new]
    outs = []
    for (b, gi), s, v, qs in zip(chains, states, v_l, q_s):
        vbd = jnp.where(vmask, jnp.concatenate([v] * GDN_GROUP, axis=0), jnp.zeros((), low))
        outs.append(qs + _dot(qk_ref[b, :, gi * GDN_GROUP * cn:(gi + 1) * GDN_GROUP * cn], vbd))
        upd = _dot_tn(kd_ref[b, :, lanes(gi)], v)
        expand = (e_row == e_col + gi * GDN_GROUP).astype(BF16)
        gte = sum(_dot(p.astype(BF16), expand) for p in _split3(gt_ref[b, 0:8, :]))[0:1]
        sbd_ref[b, gi] = s * gte + jnp.where(bdmask, upd, 0.0)
    for (b, gi), o in zip(chains, outs):
        o2 = o * o
        hi = o2.astype(BF16)
        lo = (o2 - hi.astype(F32)).astype(BF16)
        if cn % 16 == 0:
            sums = _dot(jnp.concatenate([hi, lo], axis=0), ones_bd)
            ms = (sums[:cn] + sums[cn:]) * (1.0 / HEAD_DIM)
        else:
            ms = (_dot(hi, ones_bd) + _dot(lo, ones_bd)) * (1.0 / HEAD_DIM)
        zg = z_ref[b, :, lanes(gi)]
        o_ref[b, :, lanes(gi)] = o * lax.rsqrt(ms + EPS) * nw_ref[...] * (zg * jax.nn.sigmoid(zg))

    @pl.when(ni == n_chunks - 1)
    def _():
        for b in range(bb):
            for h in range(GDN_HEADS):
                gi, hh = divmod(h, GDN_GROUP)
                ds = slice(hh * HEAD_DIM, (hh + 1) * HEAD_DIM)
                sfin_ref[b, h] = sbd_ref[b, gi, ds, ds]


def _gdn(c3, hist, z3, ab3, s0, conv_w, alog16, dtb16, nw_group, *, chunk, seq_block):
    nseq, t, _ = c3.shape
    n = nseq * t
    sb, r = (1, GDN_TILE) if t >= GDN_TILE else (GDN_TILE // t, t)
    tiles = t // r
    low = BF16 if chunk >= 16 else F32
    blk = lambda w: pl.BlockSpec((sb, r, w), lambda s, i: (s, i, 0))
    full = lambda a: pl.BlockSpec(a.shape, lambda s, i: (0,) * a.ndim)
    flat = lambda w: pl.BlockSpec((GDN_TILE, w), lambda s, i: (s * tiles + i, 0))
    widths = (GDN_V, GDN_V, GDN_QK, GDN_QK, GDN_HEADS * chunk, 2 * GDN_HEADS)
    dtypes = (F32, low, low, low, low, F32)
    prep_out = dict(out_specs=[flat(wd) for wd in widths],
                    out_shape=[jax.ShapeDtypeStruct((n, wd), dt) for wd, dt in zip(widths, dtypes)],
                    compiler_params=pltpu.CompilerParams(dimension_semantics=("arbitrary", "arbitrary")))
    lane_dense = chunk == HEAD_DIM and sb == 1
    u, w, qd, kd, qk, gt = pl.pallas_call(
        _gdn_prep_pair_kernel if lane_dense else functools.partial(_gdn_prep_kernel, chunk=chunk),
        grid=(nseq // sb, tiles),
        in_specs=[blk(GDN_CONV_CH), pl.BlockSpec((sb, CONV_W - 1, GDN_CONV_CH), lambda s, i: (s, 0, 0)),
                  blk(2 * GDN_HEADS), full(conv_w), full(alog16), full(dtb16)],
        scratch_shapes=[pltpu.VMEM((1, 8, GDN_CONV_CH) if lane_dense else (sb, 8 + r, GDN_CONV_CH), F32)],
        name="gdn_prep_pair" if lane_dense else "gdn_prep", **prep_out)(c3, hist, ab3, conv_w, alog16, dtb16)

    n_chunks = t // chunk
    tok = lambda wd: pl.BlockSpec((seq_block, chunk, wd), lambda s, c: (s, c, 0))
    per_seq = pl.BlockSpec((seq_block,) + s0.shape[1:], lambda s, c: (s, 0, 0, 0))
    seq3 = lambda a: a.reshape(nseq, t, a.shape[-1])
    return pl.pallas_call(
        functools.partial(_gdn_scan_kernel, chunk=chunk, n_chunks=n_chunks),
        grid=(nseq // seq_block, n_chunks),
        in_specs=[tok(wd) for wd in widths] + [tok(GDN_V), per_seq,
                                               pl.BlockSpec(nw_group.shape, lambda s, c: (0, 0))],
        out_specs=[tok(GDN_V), per_seq],
        out_shape=[jax.ShapeDtypeStruct((nseq, t, GDN_V), F32), jax.ShapeDtypeStruct(s0.shape, F32)],
        scratch_shapes=[pltpu.VMEM((seq_block, GDN_HEADS // GDN_GROUP, GDN_GROUP_W, GDN_GROUP_W), F32)],
        compiler_params=pltpu.CompilerParams(dimension_semantics=("arbitrary", "arbitrary")),
        name="gdn_scan",
    )(seq3(u), seq3(w), seq3(qd), seq3(kd), seq3(qk), seq3(gt), z3, s0, nw_group)


def _outproj_kernel(x_ref, osw_ref, ogd_ref, wo_ref, gf_ref, wr_ref, br_ref,
                    h_ref, xn_ref, gates_ref, meta_ref, cnt_ref, run_ref):
    i = pl.program_id(0)
    tm = x_ref.shape[0]
    rows = wr_ref.shape[0]

    @pl.when(i == 0)
    def _():
        run_ref[...] = jnp.zeros_like(run_ref)

    h = (x_ref[...] + _dot(osw_ref[...].astype(BF16), wo_ref[0:SWA_Q, :])
         + _dot(ogd_ref[...].astype(BF16), wo_ref[SWA_Q:D_MIX, :]))
    h_ref[...] = h
    xn = _rms(h, gf_ref[...])
    xn_ref[...] = _pack_halves(xn)
    logits = _dot_nt(wr_ref[...], xn.astype(BF16))

    row = lax.broadcasted_iota(I32, (rows, tm), 0)
    bias = br_ref[...]
    top = lambda v: jnp.max(v, axis=0, keepdims=True)
    tot = lambda v: jnp.sum(v, axis=0, keepdims=True)
    first_at = lambda v: jnp.min(jnp.where(v == top(v), row, 2 * rows), axis=0, keepdims=True)
    is_g = (row >= N_EXPERTS) & (row < N_EXPERTS + N_GROUPS)
    lg = jnp.where(is_g, logits, NEG_BIG)
    pg = jnp.where(is_g, jnp.exp(lg - top(lg)), 0.0)
    group_p = pg / tot(pg)
    g_row = first_at(jnp.where(is_g, group_p + bias, NEG_BIG))
    g_w = tot(jnp.where(row == g_row, group_p, 0.0))
    sel = (row < N_EXPERTS) & ((row // EXPERTS_PER_GROUP) == (g_row - N_EXPERTS))
    le = jnp.where(sel, logits, NEG_BIG)
    pe = jnp.where(sel, jnp.exp(le - top(le)), 0.0)
    e_p = pe / tot(pe)
    score = jnp.where(sel, e_p + bias, NEG_BIG)
    i1 = first_at(score)
    i2 = first_at(jnp.where(row == i1, NEG_BIG, score))
    oh1 = row == i1
    oh2 = row == i2
    w1 = tot(jnp.where(oh1, e_p, 0.0))
    w2 = tot(jnp.where(oh2, e_p, 0.0))
    wsum = w1 + w2

    ohs = (oh1 | oh2).astype(BF16)
    earlier = (lax.broadcasted_iota(I32, (tm, tm), 0) < lax.broadcasted_iota(I32, (tm, tm), 1)).astype(BF16)
    before = _dot(ohs, earlier) + run_ref[...]
    r1 = tot(jnp.where(oh1, before, 0.0))
    r2 = tot(jnp.where(oh2, before, 0.0))
    run_ref[...] = run_ref[...] + jnp.sum(ohs.astype(F32), axis=1, keepdims=True)
    cnt_ref[...] = run_ref[...]

    zero = jnp.zeros_like(w1)
    meta = jnp.concatenate([i1.astype(F32), i2.astype(F32), r1, r2, g_w * (w1 / wsum), g_w * (w2 / wsum),
                            zero, zero], axis=0)
    meta_ref[...] = meta
    eye8 = (lax.broadcasted_iota(I32, (8, LANES), 0) == lax.broadcasted_iota(I32, (8, LANES), 1)).astype(F32)
    gates_ref[...] = sum(_dot_tn(part, eye8) for part in _split3(meta))[:, 4:6]


def _outproj(x2d, o_swa, o_gdn, wo_bf16, gain, w_router, b_router):
    n = x2d.shape[0]
    tm = ROW_TILE
    row = lambda w: pl.BlockSpec((tm, w), lambda i: (i, 0))
    full = lambda a: pl.BlockSpec(a.shape, lambda i: (0,) * a.ndim)
    return pl.pallas_call(
        _outproj_kernel,
        grid=(n // tm,),
        in_specs=[row(D_MODEL), row(SWA_Q), row(GDN_V), full(wo_bf16), full(gain), full(w_router), full(b_router)],
        out_specs=[row(D_MODEL), row(PACKED_W), row(2), pl.BlockSpec((8, tm), lambda i: (0, i)),
                   pl.BlockSpec((LANES, 1), lambda i: (0, 0))],
        out_shape=[jax.ShapeDtypeStruct((n, D_MODEL), F32), jax.ShapeDtypeStruct((n, PACKED_W), U32),
                   jax.ShapeDtypeStruct((n, 2), F32), jax.ShapeDtypeStruct((8, n), F32),
                   jax.ShapeDtypeStruct((LANES, 1), F32)],
        scratch_shapes=[pltpu.VMEM((LANES, 1), F32)],
        compiler_params=pltpu.CompilerParams(dimension_semantics=("arbitrary",)),
        name="outproj_router",
    )(x2d, o_swa, o_gdn, wo_bf16, gain, w_router, b_router)


def _row_copy(src_ref, src_row, dst_ref, dst_row, sem):
    return pltpu.make_async_copy(src_ref.at[pl.ds(src_row, 1)], dst_ref.at[pl.ds(dst_row, 1)], sem)


def _scatter_kernel(dest0_ref, dest1_ref, xp_ref, xs_ref, out_ref, aidx_ref, sem, *, tiles_p, n_tokens):
    i = pl.program_id(0)
    tm = xp_ref.shape[0]

    def run(src_ref):
        def issue(g, carry):
            for u in range(DMA_UNROLL):
                r = g * DMA_UNROLL + u
                d0, d1 = dest0_ref[r], dest1_ref[r]
                _row_copy(src_ref, r, out_ref, d0, sem).start(priority=0)
                _row_copy(src_ref, r, out_ref, d1, sem).start(priority=1)
                aidx_ref[d0] = i * tm + r
                aidx_ref[d1] = n_tokens + i * tm + r
            return carry

        lax.fori_loop(0, tm // DMA_UNROLL, issue, 0)
        for _ in range(2):
            pltpu.make_async_copy(src_ref, out_ref.at[pl.ds(0, tm)], sem).wait()

    @pl.when(i < tiles_p)
    def _():
        run(xp_ref)

    @pl.when(i >= tiles_p)
    def _():
        run(xs_ref)


def _scatter(dest, xn_p, xn_s):
    tm = ROW_TILE
    tiles_p, tiles_s = xn_p.shape[0] // tm, xn_s.shape[0] // tm
    rows = 2 * (xn_p.shape[0] + xn_s.shape[0])
    idx = pl.BlockSpec((tm,), lambda i: (i,), memory_space=pltpu.SMEM)
    return pl.pallas_call(
        functools.partial(_scatter_kernel, tiles_p=tiles_p, n_tokens=rows // 2),
        grid=(tiles_p + tiles_s,),
        in_specs=[idx, idx,
                  pl.BlockSpec((tm, PACKED_W), lambda i: (jnp.minimum(i, tiles_p - 1), 0)),
                  pl.BlockSpec((tm, PACKED_W), lambda i: (jnp.maximum(i - tiles_p, 0), 0))],
        out_specs=[pl.BlockSpec(memory_space=pl.ANY), pl.BlockSpec(memory_space=pltpu.SMEM)],
        out_shape=[jax.ShapeDtypeStruct((rows, PACKED_W), U32), jax.ShapeDtypeStruct((rows,), I32)],
        scratch_shapes=[pltpu.SemaphoreType.DMA(())],
        compiler_params=pltpu.CompilerParams(dimension_semantics=("arbitrary",)),
        name="scatter_rows",
    )(dest[0], dest[1], xn_p, xn_s)


def _experts_kernel(blk_ref, exp_ref, lo_ref, hi_ref, first_ref, fresh_ref, pblk_ref,
                    aprev_ref, x_ref, wg_ref, wu_ref, wd_ref, ya_ref, wg_l, wu_l, wd_l, ybuf, sem,
                    *, n_items, n_blocks):
    j = pl.program_id(0)
    lo, hi = lo_ref[j], hi_ref[j]
    slot = lax.rem(blk_ref[j], 2)
    quarter = MOE_BLOCK // 4

    def wait_rows(s):
        pltpu.make_async_copy(ybuf.at[s], ya_ref.at[pl.ds(0, MOE_BLOCK)], sem.at[s]).wait()

    def send_rows(s, group):
        for row in range(group * quarter, (group + 1) * quarter):
            _row_copy(ybuf.at[s], row, ya_ref, aprev_ref[0, 0, row], sem.at[s]).start(priority=row % 2)

    @pl.when(fresh_ref[j] == 1)
    def _():
        wg_l[...] = wg_ref[0].astype(BF16)
        wu_l[...] = wu_ref[0].astype(BF16)
        wd_l[...] = wd_ref[0].astype(BF16)

    @pl.when((first_ref[j] == 1) & (blk_ref[j] >= 2))
    def _():
        wait_rows(slot)

    def item(is_first, send_prev):
        send = (lambda group: send_rows(1 - slot, group)) if send_prev else (lambda group: None)
        send(0)
        x_a, x_b = _unpack_halves(x_ref[...])
        gate = _dot(x_a, wg_l[0:PACKED_W, :]) + _dot(x_b, wg_l[PACKED_W:D_MODEL, :])
        send(1)
        up = _dot(x_a, wu_l[0:PACKED_W, :]) + _dot(x_b, wu_l[PACKED_W:D_MODEL, :])
        hid = (gate * jax.nn.sigmoid(gate)) * up
        send(2)
        y = _dot(hid.astype(BF16), wd_l[...])
        send(3)
        r = lax.broadcasted_iota(I32, (MOE_BLOCK, 1), 0)
        mine = (r >= lo) & (r < hi)
        ybuf[slot] = jnp.where(mine, y, 0.0 if is_first else ybuf[slot])

    live = hi > lo
    pl.when(live & (first_ref[j] == 1) & (blk_ref[j] >= 1))(functools.partial(item, True, True))
    pl.when(live & (first_ref[j] == 1) & (blk_ref[j] == 0))(functools.partial(item, True, False))
    pl.when(live & (first_ref[j] == 0))(functools.partial(item, False, False))

    @pl.when(j == n_items - 1)
    def _():
        last_slot = (n_blocks - 1) % 2
        for group in range(4):
            send_rows(last_slot, group)
        wait_rows(1 - last_slot)
        wait_rows(last_slot)


def _experts(items, aidx, xs, w_gate, w_up, w_down):
    n_items = items[0].shape[0]
    n_blocks = xs.shape[0] // MOE_BLOCK
    assert n_items > n_blocks + N_EXPERTS - 1 and n_blocks >= 2
    xblk = pl.BlockSpec((MOE_BLOCK, PACKED_W), lambda j, blk, *_: (blk[j], 0))
    wspec = lambda a: pl.BlockSpec((1,) + a.shape[1:], lambda j, blk, ex, *_: (ex[j], 0, 0))
    return pl.pallas_call(
        functools.partial(_experts_kernel, n_items=n_items, n_blocks=n_blocks),
        grid_spec=pltpu.PrefetchScalarGridSpec(
            num_scalar_prefetch=len(items),
            grid=(n_items,),
            in_specs=[pl.BlockSpec((1, 1, MOE_BLOCK), lambda j, *pre: (pre[-1][j], 0, 0), memory_space=pltpu.SMEM),
                      xblk, wspec(w_gate), wspec(w_up), wspec(w_down)],
            out_specs=pl.BlockSpec(memory_space=pl.ANY),
            scratch_shapes=[pltpu.VMEM(w_gate.shape[1:], BF16), pltpu.VMEM(w_up.shape[1:], BF16),
                            pltpu.VMEM(w_down.shape[1:], BF16), pltpu.VMEM((2, MOE_BLOCK, D_MODEL), F32),
                            pltpu.SemaphoreType.DMA((2,))]),
        out_shape=jax.ShapeDtypeStruct((xs.shape[0], D_MODEL), F32),
        compiler_params=pltpu.CompilerParams(dimension_semantics=("arbitrary",)),
        name="experts",
    )(*items, aidx.reshape(n_blocks, 1, MOE_BLOCK), xs, w_gate, w_up, w_down)


def _work_items(counts, total_rows):
    n_blocks = total_rows // MOE_BLOCK
    n_items = n_blocks + N_EXPERTS
    end = jnp.cumsum(counts)
    start = end - counts
    first_blk = start // MOE_BLOCK
    nb = jnp.where(counts > 0, (end - 1) // MOE_BLOCK - first_blk + 1, 0)
    item_end = jnp.cumsum(nb)
    used = item_end[-1]
    j = jnp.arange(n_items, dtype=I32)
    jj = jnp.minimum(j, used - 1)
    e = jnp.minimum(jnp.sum((item_end[None, :] <= jj[:, None]).astype(I32), axis=1), N_EXPERTS - 1)
    onehot = (e[:, None] == jnp.arange(N_EXPERTS, dtype=I32)[None, :]).astype(I32)
    pick = lambda a: jnp.sum(onehot * a[None, :], axis=1)
    blk = pick(first_blk) + (jj - (pick(item_end) - pick(nb)))
    lo = jnp.maximum(pick(start), blk * MOE_BLOCK) - blk * MOE_BLOCK
    hi = jnp.minimum(pick(end), (blk + 1) * MOE_BLOCK) - blk * MOE_BLOCK
    live = j < used
    lo = jnp.where(live, lo, 0)
    hi = jnp.where(live, hi, 0)
    prev_blk = jnp.concatenate([jnp.full((1,), -1, I32), blk[:-1]])
    first = (live & (blk != prev_blk)).astype(I32)
    prev_e = jnp.concatenate([jnp.full((1,), -1, I32), e[:-1]])
    fresh = (live & (e != prev_e)).astype(I32)
    pblk = jnp.where(j == n_items - 1, n_blocks - 1, jnp.maximum(blk - 1, 0))
    return (blk.astype(I32), e.astype(I32), lo.astype(I32), hi.astype(I32), first, fresh, pblk.astype(I32)), start


def _combine_kernel(gates_ref, h_ref, gf_ref, y0_ref, y1_ref, out_ref):
    gates = gates_ref[...]
    y = h_ref[...] + (y0_ref[...] * gates[:, 0:1] + y1_ref[...] * gates[:, 1:2])
    out_ref[...] = _rms(y, gf_ref[...])


def _combine(gates, h, gain, ya, first_token):
    n = h.shape[0]
    tm = COMBINE_TILE
    tile0 = first_token // tm
    slot_tiles = ya.shape[0] // 2 // tm
    row = lambda w: pl.BlockSpec((tm, w), lambda i: (i, 0))
    slot = lambda k: pl.BlockSpec((tm, D_MODEL), lambda i: (k * slot_tiles + tile0 + i, 0))
    return pl.pallas_call(
        _combine_kernel,
        grid=(n // tm,),
        in_specs=[row(2), row(D_MODEL), pl.BlockSpec(gain.shape, lambda i: (0, 0)), slot(0), slot(1)],
        out_specs=row(D_MODEL),
        out_shape=jax.ShapeDtypeStruct((n, D_MODEL), F32),
        compiler_params=pltpu.CompilerParams(dimension_semantics=("arbitrary",)),
        name="combine_norm",
    )(gates, h, gain, ya, ya)


def kernel(x_prompt, x_sample, cache_swa_k, cache_swa_v, state_gdn_conv, state_gdn, norm_mix, w_in, swa_sinks,
           gdn_conv_w, gdn_A_log, gdn_dt_bias, gdn_norm_w, w_out, norm_ffn, w_router_group, b_router_group,
           w_router_expert, b_router_expert, w_exp_gate, w_exp_up, w_exp_down, norm_final):
    depth = w_in.shape[0]
    assert depth == 1, "single trunk layer"
    bp, sp, _ = x_prompt.shape
    bs, ts, _ = x_sample.shape
    np_, ns = bp * sp, bs * ts
    l = 0

    w_in_b = w_in[l].astype(BF16)
    w_out_b = w_out[l].astype(BF16)
    g_mix = norm_mix[l].reshape(1, D_MODEL)
    g_ffn = norm_ffn[l].reshape(1, D_MODEL)
    g_fin = norm_final.reshape(1, D_MODEL)
    pad = LANES - N_EXPERTS - N_GROUPS
    w_router = jnp.concatenate([w_router_expert[l], w_router_group[l], jnp.zeros((D_MODEL, pad), F32)],
                               axis=1).astype(BF16).T
    b_router = jnp.concatenate([b_router_expert[l].reshape(-1), b_router_group[l],
                                jnp.zeros((pad,), F32)])[:, None]
    zeros8 = jnp.zeros((GDN_HEADS,), F32)
    alog16 = jnp.concatenate([gdn_A_log[l], zeros8])[None]
    dtb16 = jnp.concatenate([gdn_dt_bias[l], zeros8])[None]
    nw_group = jnp.tile(gdn_norm_w[l].reshape(1, HEAD_DIM), (1, GDN_GROUP))
    sinks = swa_sinks[l]

    q_p, k_p, v_p, c_p, z_p, ab_p = _inproj(x_prompt.reshape(np_, D_MODEL), g_mix, w_in_b)
    nblk = np_ // WINDOW
    k_p3 = k_p.reshape(nblk, WINDOW, SWA_KV)
    v_p3 = v_p.reshape(nblk, WINDOW, SWA_KV)
    (o_swa_p,) = _swa(sinks, q_p.reshape(nblk, WINDOW, SWA_Q), k_p3, v_p3, k_p3, v_p3,
                      bb=SWA_BLOCKS, blocks_per_seq=sp // WINDOW, emit_cache=False)
    c_p3 = c_p.reshape(bp, sp, GDN_CONV_CH)
    o_gdn_p, s_fin_p = _gdn(c_p3, jnp.zeros((bp, CONV_W - 1, GDN_CONV_CH), F32), z_p.reshape(bp, sp, GDN_V),
                            ab_p.reshape(bp, sp, 2 * GDN_HEADS),
                            jnp.zeros((bp, GDN_HEADS, HEAD_DIM, HEAD_DIM), F32),
                            gdn_conv_w[l], alog16, dtb16, nw_group, chunk=GDN_CHUNK, seq_block=bp)
    h_p, xn_p, gates_p, meta_p, cnt_p = _outproj(
        x_prompt.reshape(np_, D_MODEL), o_swa_p.reshape(np_, SWA_Q), o_gdn_p.reshape(np_, GDN_V),
        w_out_b, g_ffn, w_router, b_router)

    q_s, k_s, v_s, c_s, z_s, ab_s = _inproj(x_sample.reshape(ns, D_MODEL), g_mix, w_in_b)
    feature_major = lambda a: jnp.swapaxes(a.reshape(bs, WINDOW, SWA_KV), 1, 2)
    o_swa_s, kcache_t, vcache_t = _swa(
        sinks, q_s.reshape(bs, ts, SWA_Q), k_s.reshape(bs, ts, SWA_KV), v_s.reshape(bs, ts, SWA_KV),
        feature_major(cache_swa_k[l]), feature_major(cache_swa_v[l]),
        bb=16, blocks_per_seq=None, emit_cache=True)
    kcache_s, vcache_s = jnp.swapaxes(kcache_t, 1, 2), jnp.swapaxes(vcache_t, 1, 2)
    c_s3 = c_s.reshape(bs, ts, GDN_CONV_CH)
    o_gdn_s, s_fin_s = _gdn(c_s3, state_gdn_conv[l], z_s.reshape(bs, ts, GDN_V),
                            ab_s.reshape(bs, ts, 2 * GDN_HEADS), state_gdn[l],
                            gdn_conv_w[l], alog16, dtb16, nw_group, chunk=ts, seq_block=16)
    h_s, xn_s, gates_s, meta_s, cnt_s = _outproj(
        x_sample.reshape(ns, D_MODEL), o_swa_s.reshape(ns, SWA_Q), o_gdn_s.reshape(ns, GDN_V),
        w_out_b, g_ffn, w_router, b_router)

    cnt_p_i = cnt_p[:N_EXPERTS, 0].astype(I32)
    cnt_s_i = cnt_s[:N_EXPERTS, 0].astype(I32)
    items, start = _work_items(cnt_p_i + cnt_s_i, 2 * (np_ + ns))
    expert_ids = jnp.arange(N_EXPERTS, dtype=I32)
    lookup = lambda table, ids: jnp.sum(jnp.where(ids[..., None] == expert_ids, table, 0), axis=-1)
    dest_p = lookup(start, meta_p[0:2].astype(I32)) + meta_p[2:4].astype(I32)
    dest_s = lookup(start + cnt_p_i, meta_s[0:2].astype(I32)) + meta_s[2:4].astype(I32)
    xs, aidx = _scatter(jnp.concatenate([dest_p, dest_s], axis=1), xn_p, xn_s)
    ya = _experts(items, aidx, xs, w_exp_gate[l], w_exp_up[l], w_exp_down[l])
    y_p = _combine(gates_p, h_p, g_fin, ya, 0)
    y_s = _combine(gates_s, h_s, g_fin, ya, np_)

    kv5 = lambda a, b: a.reshape(b, -1, SWA_KV_HEADS, HEAD_DIM)[None]
    return (y_p.reshape(bp, sp, D_MODEL), y_s.reshape(bs, ts, D_MODEL),
            kv5(k_p.reshape(bp, sp, SWA_KV)[:, -WINDOW:], bp), kv5(v_p.reshape(bp, sp, SWA_KV)[:, -WINDOW:], bp),
            kv5(kcache_s, bs), kv5(vcache_s, bs),
            c_p3[:, -(CONV_W - 1):][None], c_s3[:, -(CONV_W - 1):][None],
            s_fin_p[None], s_fin_s[None])
```

```python
import functools

import jax
import jax.numpy as jnp
from jax import lax
from jax.experimental import pallas as pl
from jax.experimental.pallas import tpu as pltpu

F32 = jnp.float32
BF16 = jnp.bfloat16
I32 = jnp.int32
U32 = jnp.uint32

D_MODEL = 1024
HEAD_DIM = 64
SWA_HEADS = 8
GDN_HEADS = 8
SWA_KV_HEADS = 2
GQA_GROUP = SWA_HEADS // SWA_KV_HEADS
WINDOW = 128
ATTN_SCALE = HEAD_DIM ** -0.5
CONV_W = 4
N_GROUPS = 8
EXPERTS_PER_GROUP = 8
N_EXPERTS = 64
D_EXPERT = 256
EPS = 1e-6

SWA_Q = SWA_HEADS * HEAD_DIM
SWA_KV = SWA_KV_HEADS * HEAD_DIM
GDN_QK = GDN_HEADS * HEAD_DIM
GDN_V = GDN_HEADS * HEAD_DIM
GDN_CONV_CH = 2 * GDN_QK + GDN_V
D_MIX = SWA_Q + GDN_V
D_IN = SWA_Q + 2 * SWA_KV + GDN_CONV_CH + GDN_V + 2 * GDN_HEADS
COL_K = SWA_Q
COL_V = COL_K + SWA_KV
COL_C = COL_V + SWA_KV
COL_Z = COL_C + GDN_CONV_CH
COL_AB = COL_Z + GDN_V
PACKED_W = D_MODEL // 2

LANES = 128
NEG_BIG = -1e30
ROW_TILE = 512
INPROJ_TILE = 1024
MOE_BLOCK = 256
COMBINE_TILE = 512
DMA_UNROLL = 8
SWA_BLOCKS = 4
GDN_CHUNK = 64
GDN_TILE = 256
GDN_GROUP = 4
GDN_GROUP_W = GDN_GROUP * HEAD_DIM


def _rms(x, g):
    return x * lax.rsqrt(jnp.mean(x * x, axis=-1, keepdims=True) + EPS) * g


def _dot(a, b):
    return jnp.dot(a, b, preferred_element_type=F32)


def _dot_nt(a, b):
    return lax.dot_general(a, b, (((1,), (1,)), ((), ())), preferred_element_type=F32)


def _dot_tn(a, b):
    return lax.dot_general(a, b, (((0,), (0,)), ((), ())), preferred_element_type=F32)


def _pack_halves(x):
    w = x.shape[1] // 2
    bits = lambda v: lax.bitcast_convert_type(v.astype(BF16).astype(F32), U32)
    return bits(x[:, :w]) | (bits(x[:, w:]) >> 16)


def _unpack_halves(p):
    hi = lax.bitcast_convert_type(p & jnp.uint32(0xFFFF0000), F32)
    lo = lax.bitcast_convert_type(p << 16, F32)
    return hi.astype(BF16), lo.astype(BF16)


def _split3(x):
    p1 = x.astype(BF16).astype(F32)
    r = x - p1
    p2 = r.astype(BF16).astype(F32)
    p3 = (r - p2).astype(BF16).astype(F32)
    return p1, p2, p3


def _inproj_kernel(x_ref, g_ref, w_ref, q_ref, k_ref, v_ref, c_ref, z_ref, ab_ref):
    x = x_ref[...]
    xb = _rms(x, g_ref[...]).astype(BF16)
    q_ref[...] = _dot(xb, w_ref[:, 0:COL_K])
    k_ref[...] = _dot(xb, w_ref[:, COL_K:COL_V])
    v_ref[...] = _dot(xb, w_ref[:, COL_V:COL_C])
    c_ref[...] = _dot(xb, w_ref[:, COL_C:COL_Z])
    z_ref[...] = _dot(xb, w_ref[:, COL_Z:COL_AB])
    ab_ref[...] = _dot(xb, w_ref[:, COL_AB:D_IN])


def _inproj(x2d, gain, w_bf16):
    n = x2d.shape[0]
    tm = INPROJ_TILE
    row = lambda w: pl.BlockSpec((tm, w), lambda i: (i, 0))
    full = lambda a: pl.BlockSpec(a.shape, lambda i: (0,) * a.ndim)
    widths = (SWA_Q, SWA_KV, SWA_KV, GDN_CONV_CH, GDN_V, 2 * GDN_HEADS)
    return pl.pallas_call(
        _inproj_kernel,
        grid=(n // tm,),
        in_specs=[row(D_MODEL), full(gain), full(w_bf16)],
        out_specs=[row(w) for w in widths],
        out_shape=[jax.ShapeDtypeStruct((n, w), F32) for w in widths],
        compiler_params=pltpu.CompilerParams(dimension_semantics=("arbitrary",)),
        name="inproj",
    )(x2d, gain, w_bf16)


def _swa_kernel(sink_ref, q_ref, kc_ref, vc_ref, kp_ref, vp_ref, o_ref, *cache_refs,
                bb, t, blocks_per_seq, emit_cache):
    rows = GQA_GROUP * t
    ri = lax.broadcasted_iota(I32, (rows, 1), 0)
    qi = lax.rem(ri, t)
    gi = ri // t
    if emit_cache:
        kj = lax.broadcasted_iota(I32, (rows, WINDOW), 1)
        mask = kj <= qi + (WINDOW - t)
        mask_first = mask
        place = (lax.broadcasted_iota(I32, (t, WINDOW), 1)
                 == lax.broadcasted_iota(I32, (t, WINDOW), 0) + (WINDOW - t)).astype(F32)
        is_new = lax.broadcasted_iota(I32, (SWA_KV, WINDOW), 1) >= WINDOW - t
    else:
        kj = lax.broadcasted_iota(I32, (rows, WINDOW + t), 1)
        mask = (kj <= qi + WINDOW) & (kj > qi)
        has_prev = lax.rem(pl.program_id(0) * bb, blocks_per_seq) != 0
        mask_first = mask & (has_prev | (kj >= WINDOW))
    sinks = []
    for h in range(SWA_KV_HEADS):
        sink = jnp.zeros((rows, 1), F32)
        for g in range(GQA_GROUP):
            sink = jnp.where(gi == g, sink_ref[GQA_GROUP * h + g], sink)
        sinks.append(sink)
    if emit_cache:
        def updated(old_t, new):
            placed = sum(_dot_tn(part, place) for part in _split3(new))
            return jnp.where(is_new, placed, pltpu.roll(old_t, WINDOW - t, axis=1))

        caches = [(updated(kp_ref[b], kc_ref[b]), updated(vp_ref[b], vc_ref[b])) for b in range(bb)]
        for b in range(bb):
            cache_refs[0][b] = caches[b][0]
            cache_refs[1][b] = caches[b][1]
    chains = [(b, h) for b in range(bb) for h in range(SWA_KV_HEADS)]
    scores, values = [], []
    for b, h in chains:
        hs = slice(h * HEAD_DIM, (h + 1) * HEAD_DIM)
        q4 = jnp.concatenate(
            [q_ref[b, :, (GQA_GROUP * h + g) * HEAD_DIM:(GQA_GROUP * h + g + 1) * HEAD_DIM]
             for g in range(GQA_GROUP)], axis=0)
        if emit_cache:
            keys_t, vals_t = caches[b][0][hs, :], caches[b][1][hs, :]
            scores.append(_dot(q4.astype(BF16), keys_t.astype(BF16)))
            values.append(vals_t.astype(BF16))
        else:
            kp, vp = (kp_ref[0, :, hs], vp_ref[0, :, hs]) if b == 0 else (kc_ref[b - 1, :, hs], vc_ref[b - 1, :, hs])
            keys = jnp.concatenate([kp, kc_ref[b, :, hs]], axis=0)
            vals = jnp.concatenate([vp, vc_ref[b, :, hs]], axis=0)
            scores.append(_dot_nt(q4.astype(BF16), keys.astype(BF16)))
            values.append(vals.astype(BF16))
    probs, dens = [], []
    for (b, h), s in zip(chains, scores):
        s = jnp.where(mask_first if b == 0 else mask, s * ATTN_SCALE, NEG_BIG)
        m = jnp.maximum(jnp.max(s, axis=-1, keepdims=True), sinks[h])
        p = jnp.exp(s - m)
        dens.append(jnp.sum(p, axis=-1, keepdims=True) + jnp.exp(sinks[h] - m))
        probs.append(p.astype(BF16))
    pv = _dot_nt if emit_cache else _dot
    outs = [pv(p, v) / den for p, v, den in zip(probs, values, dens)]
    for b in range(bb):
        o_ref[b] = jnp.concatenate([outs[b * SWA_KV_HEADS + h][g * t:(g + 1) * t]
                                    for h in range(SWA_KV_HEADS) for g in range(GQA_GROUP)], axis=-1)


def _swa(sinks, q3, k3, v3, kprev3, vprev3, *, bb, blocks_per_seq, emit_cache):
    nb, t, _ = q3.shape
    cur = lambda w: pl.BlockSpec((bb, t, w), lambda i: (i, 0, 0))
    if emit_cache:
        prev = pl.BlockSpec((bb, SWA_KV, WINDOW), lambda i: (i, 0, 0))
    else:
        assert blocks_per_seq % bb == 0
        prev = pl.BlockSpec((1, WINDOW, SWA_KV), lambda i: (jnp.maximum(i * bb - 1, 0), 0, 0))
    out_specs = [cur(SWA_Q)]
    out_shape = [jax.ShapeDtypeStruct((nb, t, SWA_Q), F32)]
    if emit_cache:
        out_specs += [prev, prev]
        out_shape += [jax.ShapeDtypeStruct((nb, SWA_KV, WINDOW), F32)] * 2
    return pl.pallas_call(
        functools.partial(_swa_kernel, bb=bb, t=t, blocks_per_seq=blocks_per_seq, emit_cache=emit_cache),
        grid=(nb // bb,),
        in_specs=[pl.BlockSpec(memory_space=pltpu.SMEM), cur(SWA_Q), cur(SWA_KV), cur(SWA_KV), prev, prev],
        out_specs=out_specs,
        out_shape=out_shape,
        compiler_params=pltpu.CompilerParams(dimension_semantics=("arbitrary",)),
        name="swa_cache" if emit_cache else "swa_band",
    )(sinks, q3, k3, v3, kprev3, vprev3)


def _gdn_prep_kernel(c_ref, hist_ref, ab_ref, cw_ref, alog_ref, dtb_ref,
                     u_ref, w_ref, qd_ref, kd_ref, qk_ref, gt_ref, cbuf_ref, *, chunk):
    sb, r, _ = c_ref.shape
    tp = sb * r
    cn = chunk
    low = w_ref.dtype
    hist_rows = CONV_W - 1

    @pl.when(pl.program_id(1) == 0)
    def _():
        cbuf_ref[:, 8 - hist_rows:8, :] = hist_ref[...]

    cbuf_ref[:, 8:8 + r, :] = c_ref[...]
    conv = cbuf_ref[:, 8 - hist_rows:8 - hist_rows + r, :] * cw_ref[0:1, :]
    for i in range(1, CONV_W):
        conv = conv + cbuf_ref[:, 8 - hist_rows + i:8 - hist_rows + i + r, :] * cw_ref[i:i + 1, :]
    tail = cbuf_ref[:, 8 + r - hist_rows:8 + r, :]
    cbuf_ref[:, 8 - hist_rows:8, :] = tail
    conv = (conv * jax.nn.sigmoid(conv)).reshape(tp, GDN_CONV_CH)

    ab = ab_ref[...].reshape(tp, 2 * GDN_HEADS)
    is_g = lax.broadcasted_iota(I32, (1, 2 * GDN_HEADS), 1) < GDN_HEADS
    g = jnp.where(is_g, -jnp.exp(alog_ref[...]) * jax.nn.softplus(ab + dtb_ref[...]), 0.0)
    beta = jax.nn.sigmoid(ab)

    ri = lax.broadcasted_iota(I32, (tp, tp), 0)
    ci = lax.broadcasted_iota(I32, (tp, tp), 1)
    same = (ri // cn) == (ci // cn)
    causal = same & (ri >= ci)
    strict = same & (ri > ci)
    eye = (ri == ci).astype(F32)
    stack = jnp.concatenate([causal.astype(BF16), same.astype(BF16)], axis=0)
    both = sum(_dot(stack, p.astype(BF16)) for p in _split3(g))
    gc, gl = both[:tp], both[tp:]
    e16 = lax.broadcasted_iota(I32, (2 * GDN_HEADS, 2 * GDN_HEADS), 0)
    eye16 = (e16 == lax.broadcasted_iota(I32, (2 * GDN_HEADS, 2 * GDN_HEADS), 1)).astype(BF16)
    gc_row = sum(_dot_nt(eye16, p.astype(BF16)) for p in _split3(gc))
    gt_ref[...] = jnp.exp(gl)
    fold = (lax.broadcasted_iota(I32, (tp, cn), 0) % cn == lax.broadcasted_iota(I32, (tp, cn), 1)).astype(low)

    levels = cn.bit_length() - 2
    heads = range(GDN_HEADS)
    lanes_of = lambda i, h: slice(i * GDN_QK + h * HEAD_DIM, i * GDN_QK + (h + 1) * HEAD_DIM)
    qs = [conv[:, lanes_of(0, h)] for h in heads]
    ks = [conv[:, lanes_of(1, h)] for h in heads]
    qs = [q * lax.rsqrt(jnp.sum(q * q, axis=-1, keepdims=True) + EPS) * (HEAD_DIM ** -0.5) for q in qs]
    ks = [k * lax.rsqrt(jnp.sum(k * k, axis=-1, keepdims=True) + EPS) for k in ks]
    g_cs = [gc[:, h:h + 1] for h in heads]
    b_hs = [beta[:, GDN_HEADS + h:GDN_HEADS + h + 1] for h in heads]
    decays = [jnp.exp(jnp.where(causal, g_cs[h] - gc_row[h:h + 1, :], NEG_BIG)) for h in heads]
    egs = [jnp.exp(g_c) for g_c in g_cs]
    kbs = [k * b_h for k, b_h in zip(ks, b_hs)]
    k_ls = [k.astype(low) for k in ks]
    lmats = [jnp.where(strict, _dot_nt(kb.astype(low), k_l) * decay, 0.0) for kb, k_l, decay in zip(kbs, k_ls, decays)]
    qks = [_dot((_dot_nt(q.astype(low), k_l) * decay).astype(low), fold) for q, k_l, decay in zip(qs, k_ls, decays)]
    xs = [eye - lmat for lmat in lmats]
    ps = [_dot(lmat.astype(low), lmat.astype(low)) for lmat in lmats]
    for lev in range(1, levels + 1):
        p_ls = [p.astype(low) for p in ps]
        if lev < levels:
            xp = [_dot(jnp.concatenate([x.astype(low), p_l], axis=0), p_l) for x, p_l in zip(xs, p_ls)]
            xs = [x + m[:tp] for x, m in zip(xs, xp)]
            ps = [m[tp:] for m in xp]
        else:
            xs = [x + _dot(x.astype(low), p_l) for x, p_l in zip(xs, p_ls)]
    sols = [_dot(xs[h].astype(low),
                 jnp.concatenate([conv[:, lanes_of(2, h)] * b_hs[h], kbs[h] * egs[h]], axis=-1).astype(low))
            for h in heads]
    u_ref[...] = jnp.concatenate([sol[:, :HEAD_DIM] for sol in sols], axis=-1)
    w_ref[...] = jnp.concatenate([sol[:, HEAD_DIM:] for sol in sols], axis=-1).astype(low)
    qd_ref[...] = jnp.concatenate([q * eg for q, eg in zip(qs, egs)], axis=-1).astype(low)
    kd_ref[...] = jnp.concatenate([ks[h] * jnp.exp(gl[:, h:h + 1] - g_cs[h]) for h in heads], axis=-1).astype(low)
    qk_ref[...] = jnp.concatenate(qks, axis=-1).astype(low)


def _gdn_prep_pair_kernel(c_ref, hist_ref, ab_ref, cw_ref, alog_ref, dtb_ref,
                          u_ref, w_ref, qd_ref, kd_ref, qk_ref, gt_ref, cbuf_ref):
    _, r, _ = c_ref.shape
    tp = r
    cn = HEAD_DIM
    pair_w = 2 * HEAD_DIM
    hist_rows = CONV_W - 1
    sublanes = 8

    @pl.when(pl.program_id(1) == 0)
    def _():
        cbuf_ref[...] = jnp.zeros_like(cbuf_ref)
        cbuf_ref[:, sublanes - hist_rows:sublanes, :] = hist_ref[...]

    x3 = c_ref[0].reshape(tp // sublanes, sublanes, GDN_CONV_CH)
    before = cbuf_ref[...]
    sub = lax.broadcasted_iota(I32, (1, sublanes, 1), 1)
    conv3 = x3 * cw_ref[CONV_W - 1:CONV_W, :]
    for s in range(1, CONV_W):
        rx = pltpu.roll(x3, s, axis=1)
        rp = jnp.concatenate([pltpu.roll(before, s, axis=1), rx[:-1]], axis=0)
        conv3 = conv3 + jnp.where(sub < s, rp, rx) * cw_ref[CONV_W - 1 - s:CONV_W - s, :]
    cbuf_ref[...] = x3[tp // sublanes - 1:]
    conv = (conv3 * jax.nn.sigmoid(conv3)).reshape(tp, GDN_CONV_CH)

    ab = ab_ref[...].reshape(tp, 2 * GDN_HEADS)
    is_g = lax.broadcasted_iota(I32, (1, 2 * GDN_HEADS), 1) < GDN_HEADS
    g = jnp.where(is_g, -jnp.exp(alog_ref[...]) * jax.nn.softplus(ab + dtb_ref[...]), 0.0)
    beta = jax.nn.sigmoid(ab)

    ri = lax.broadcasted_iota(I32, (tp, tp), 0)
    ci = lax.broadcasted_iota(I32, (tp, tp), 1)
    same = (ri // cn) == (ci // cn)
    same_l = same.astype(BF16)
    stack = jnp.concatenate([(same & (ri >= ci)).astype(BF16), same_l], axis=0)
    both = sum(_dot(stack, p.astype(BF16)) for p in _split3(g))
    gc, gl = both[:tp], both[tp:]
    gt_ref[...] = jnp.exp(gl)

    lane = lax.broadcasted_iota(I32, (tp, pair_w), 1)
    c_in = lax.broadcasted_iota(I32, (tp, pair_w), 0) % cn
    j_in = lane % cn
    left = lane < cn
    causal = c_in >= j_in
    strict = c_in > j_in
    diag = c_in == j_in
    eye = diag.astype(F32)
    bdmask = ((lax.broadcasted_iota(I32, (pair_w, pair_w), 0) // cn)
              == (lax.broadcasted_iota(I32, (pair_w, pair_w), 1) // cn))
    ones_bd = bdmask.astype(BF16)

    def bd(m):
        return jnp.where(bdmask, jnp.concatenate([m, m], axis=0), 0.0).astype(BF16)

    def head_sum(x):
        hi = x.astype(BF16)
        lo = (x - hi.astype(F32)).astype(BF16)
        return _dot(hi, ones_bd) + _dot(lo, ones_bd)

    levels = cn.bit_length() - 2
    chunks = [slice(n * cn, (n + 1) * cn) for n in range(tp // cn)]
    n_pairs = GDN_HEADS // 2
    pairs = range(n_pairs)
    pick = lambda m, off, p: jnp.where(left, m[:, off + 2 * p:off + 2 * p + 1], m[:, off + 2 * p + 1:off + 2 * p + 2])
    third = lambda i, p: conv[:, i * GDN_QK + p * pair_w:i * GDN_QK + (p + 1) * pair_w]
    qs = [third(0, p) for p in pairs]
    ks = [third(1, p) for p in pairs]
    qs = [q * lax.rsqrt(head_sum(q * q) + EPS) * (HEAD_DIM ** -0.5) for q in qs]
    ks = [k * lax.rsqrt(head_sum(k * k) + EPS) for k in ks]
    gcps = [pick(gc, 0, p) for p in pairs]
    rowms = [sum(_dot(same_l, part.astype(BF16)) for part in _split3(jnp.where(diag, gcp, 0.0))) for gcp in gcps]
    decays = [jnp.exp(jnp.where(causal, gcp - rowm, NEG_BIG)) for gcp, rowm in zip(gcps, rowms)]
    lmats, vbs, kbegs = [], [], []
    for p in pairs:
        ls = slice(p * pair_w, (p + 1) * pair_w)
        q, k, gcp, decay = qs[p], ks[p], gcps[p], decays[p]
        bp = pick(beta, GDN_HEADS, p)
        eg = jnp.exp(gcp)
        kb = k * bp
        vbs.append(third(2, p) * bp)
        kbegs.append(kb * eg)
        qd_ref[:, ls] = (q * eg).astype(BF16)
        kd_ref[:, ls] = (k * jnp.exp(pick(gl, 0, p) - gcp)).astype(BF16)
        q_l, kb_l = q.astype(BF16), kb.astype(BF16)
        kbd = [bd(k[rs]) for rs in chunks]
        kk = jnp.concatenate([_dot_nt(kb_l[rs], kbd[n]) for n, rs in enumerate(chunks)], axis=0)
        qk = jnp.concatenate([_dot_nt(q_l[rs], kbd[n]) for n, rs in enumerate(chunks)], axis=0)
        qk_ref[:, ls] = (qk * decay).astype(BF16)
        lmats.append(jnp.where(strict, kk * decay, 0.0))
    bodies = [(p, rs) for p in range(n_pairs) for rs in chunks]
    xs = [eye[rs] - lmats[p][rs] for p, rs in bodies]
    ps = [_dot(lmats[p][rs].astype(BF16), bd(lmats[p][rs])) for p, rs in bodies]
    for lev in range(1, levels + 1):
        pbd = [bd(pm) for pm in ps]
        if lev < levels:
            xp = [_dot(jnp.concatenate([xm, pm], axis=0).astype(BF16), wm) for xm, pm, wm in zip(xs, ps, pbd)]
            xs = [xm + m[:cn] for xm, m in zip(xs, xp)]
            ps = [m[cn:] for m in xp]
        else:
            xs = [xm + _dot(xm.astype(BF16), wm) for xm, wm in zip(xs, pbd)]
    x_l = [xm.astype(BF16) for xm in xs]
    us = [_dot(xm, bd(vbs[p][rs])) for xm, (p, rs) in zip(x_l, bodies)]
    ws = [_dot(xm, bd(kbegs[p][rs])) for xm, (p, rs) in zip(x_l, bodies)]
    nc = len(chunks)
    for p in range(n_pairs):
        ls = slice(p * pair_w, (p + 1) * pair_w)
        u_ref[:, ls] = jnp.concatenate(us[p * nc:(p + 1) * nc], axis=0)
        w_ref[:, ls] = jnp.concatenate(ws[p * nc:(p + 1) * nc], axis=0).astype(BF16)


def _gdn_scan_kernel(u_ref, w_ref, qd_ref, kd_ref, qk_ref, gt_ref, z_ref, s0_ref, nw_ref,
                     o_ref, sfin_ref, sbd_ref, *, chunk, n_chunks):
    bb = u_ref.shape[0]
    cn = chunk
    low = w_ref.dtype
    gw = GDN_GROUP_W
    ni = pl.program_id(1)

    @pl.when(ni == 0)
    def _():
        sbd_ref[...] = jnp.zeros_like(sbd_ref)
        for b in range(bb):
            for h in range(GDN_HEADS):
                gi, hh = divmod(h, GDN_GROUP)
                ds = slice(hh * HEAD_DIM, (hh + 1) * HEAD_DIM)
                sbd_ref[b, gi, ds, ds] = s0_ref[b, h]

    bdmask = ((lax.broadcasted_iota(I32, (gw, gw), 0) // HEAD_DIM)
              == (lax.broadcasted_iota(I32, (gw, gw), 1) // HEAD_DIM))
    ones_bd = bdmask.astype(BF16)
    vmask = ((lax.broadcasted_iota(I32, (GDN_GROUP * cn, gw), 0) // cn)
             == (lax.broadcasted_iota(I32, (GDN_GROUP * cn, gw), 1) // HEAD_DIM))
    e_row = lax.broadcasted_iota(I32, (2 * GDN_HEADS, gw), 0)
    e_col = lax.broadcasted_iota(I32, (2 * GDN_HEADS, gw), 1) // HEAD_DIM
    chains = [(b, gi) for b in range(bb) for gi in range(GDN_HEADS // GDN_GROUP)]
    lanes = lambda gi: slice(gi * gw, (gi + 1) * gw)
    states = [sbd_ref[b, gi] for b, gi in chains]
    states_l = [s.astype(low) for s in states]
    wq_s = [_dot(jnp.concatenate([w_ref[b, :, lanes(gi)], qd_ref[b, :, lanes(gi)]], axis=0), s_l)
            for (b, gi), s_l in zip(chains, states_l)]
    v_new = [u_ref[b, :, lanes(gi)] - m[:cn] for (b, gi), m in zip(chains, wq_s)]
    q_s = [m[cn:] for m in wq_s]
    v_l = [v.astype(low) for v in v_new]
    outs = []
    for (b, gi), s, v, qs in zip(chains, states, v_l, q_s):
        vbd = jnp.where(vmask, jnp.concatenate([v] * GDN_GROUP, axis=0), jnp.zeros((), low))
        outs.append(qs + _dot(qk_ref[b, :, gi * GDN_GROUP * cn:(gi + 1) * GDN_GROUP * cn], vbd))
        upd = _dot_tn(kd_ref[b, :, lanes(gi)], v)
        expand = (e_row == e_col + gi * GDN_GROUP).astype(BF16)
        gte = sum(_dot(p.astype(BF16), expand) for p in _split3(gt_ref[b, 0:8, :]))[0:1]
        sbd_ref[b, gi] = s * gte + jnp.where(bdmask, upd, 0.0)
    for (b, gi), o in zip(chains, outs):
        o2 = o * o
        hi = o2.astype(BF16)
        lo = (o2 - hi.astype(F32)).astype(BF16)
        if cn % 16 == 0:
            sums = _dot(jnp.concatenate([hi, lo], axis=0), ones_bd)
            ms = (sums[:cn] + sums[cn:]) * (1.0 / HEAD_DIM)
        else:
            ms = (_dot(hi, ones_bd) + _dot(lo, ones_bd)) * (1.0 / HEAD_DIM)
        zg = z_ref[b, :, lanes(gi)]
        o_ref[b, :, lanes(gi)] = o * lax.rsqrt(ms + EPS) * nw_ref[...] * (zg * jax.nn.sigmoid(zg))

    @pl.when(ni == n_chunks - 1)
    def _():
        for b in range(bb):
            for h in range(GDN_HEADS):
                gi, hh = divmod(h, GDN_GROUP)
                ds = slice(hh * HEAD_DIM, (hh + 1) * HEAD_DIM)
                sfin_ref[b, h] = sbd_ref[b, gi, ds, ds]


def _gdn(c3, hist, z3, ab3, s0, conv_w, alog16, dtb16, nw_group, *, chunk, seq_block):
    nseq, t, _ = c3.shape
    n = nseq * t
    sb, r = (1, GDN_TILE) if t >= GDN_TILE else (GDN_TILE // t, t)
    tiles = t // r
    low = BF16 if chunk >= 16 else F32
    blk = lambda w: pl.BlockSpec((sb, r, w), lambda s, i: (s, i, 0))
    full = lambda a: pl.BlockSpec(a.shape, lambda s, i: (0,) * a.ndim)
    flat = lambda w: pl.BlockSpec((GDN_TILE, w), lambda s, i: (s * tiles + i, 0))
    widths = (GDN_V, GDN_V, GDN_QK, GDN_QK, GDN_HEADS * chunk, 2 * GDN_HEADS)
    dtypes = (F32, low, low, low, low, F32)
    prep_out = dict(out_specs=[flat(wd) for wd in widths],
                    out_shape=[jax.ShapeDtypeStruct((n, wd), dt) for wd, dt in zip(widths, dtypes)],
                    compiler_params=pltpu.CompilerParams(dimension_semantics=("arbitrary", "arbitrary")))
    lane_dense = chunk == HEAD_DIM and sb == 1
    u, w, qd, kd, qk, gt = pl.pallas_call(
        _gdn_prep_pair_kernel if lane_dense else functools.partial(_gdn_prep_kernel, chunk=chunk),
        grid=(nseq // sb, tiles),
        in_specs=[blk(GDN_CONV_CH), pl.BlockSpec((sb, CONV_W - 1, GDN_CONV_CH), lambda s, i: (s, 0, 0)),
                  blk(2 * GDN_HEADS), full(conv_w), full(alog16), full(dtb16)],
        scratch_shapes=[pltpu.VMEM((1, 8, GDN_CONV_CH) if lane_dense else (sb, 8 + r, GDN_CONV_CH), F32)],
        name="gdn_prep_pair" if lane_dense else "gdn_prep", **prep_out)(c3, hist, ab3, conv_w, alog16, dtb16)

    n_chunks = t // chunk
    tok = lambda wd: pl.BlockSpec((seq_block, chunk, wd), lambda s, c: (s, c, 0))
    per_seq = pl.BlockSpec((seq_block,) + s0.shape[1:], lambda s, c: (s, 0, 0, 0))
    seq3 = lambda a: a.reshape(nseq, t, a.shape[-1])
    return pl.pallas_call(
        functools.partial(_gdn_scan_kernel, chunk=chunk, n_chunks=n_chunks),
        grid=(nseq // seq_block, n_chunks),
        in_specs=[tok(wd) for wd in widths] + [tok(GDN_V), per_seq,
                                               pl.BlockSpec(nw_group.shape, lambda s, c: (0, 0))],
        out_specs=[tok(GDN_V), per_seq],
        out_shape=[jax.ShapeDtypeStruct((nseq, t, GDN_V), F32), jax.ShapeDtypeStruct(s0.shape, F32)],
        scratch_shapes=[pltpu.VMEM((seq_block, GDN_HEADS // GDN_GROUP, GDN_GROUP_W, GDN_GROUP_W), F32)],
        compiler_params=pltpu.CompilerParams(dimension_semantics=("arbitrary", "arbitrary")),
        name="gdn_scan",
    )(seq3(u), seq3(w), seq3(qd), seq3(kd), seq3(qk), seq3(gt), z3, s0, nw_group)


def _outproj_kernel(x_ref, osw_ref, ogd_ref, wo_ref, gf_ref, wr_ref, br_ref,
                    h_ref, xn_ref, gates_ref, meta_ref, cnt_ref, run_ref):
    i = pl.program_id(0)
    tm = x_ref.shape[0]
    rows = wr_ref.shape[0]

    @pl.when(i == 0)
    def _():
        run_ref[...] = jnp.zeros_like(run_ref)

    h = (x_ref[...] + _dot(osw_ref[...].astype(BF16), wo_ref[0:SWA_Q, :])
         + _dot(ogd_ref[...].astype(BF16), wo_ref[SWA_Q:D_MIX, :]))
    h_ref[...] = h
    xn = _rms(h, gf_ref[...])
    xn_ref[...] = _pack_halves(xn)
    logits = _dot_nt(wr_ref[...], xn.astype(BF16))

    row = lax.broadcasted_iota(I32, (rows, tm), 0)
    bias = br_ref[...]
    top = lambda v: jnp.max(v, axis=0, keepdims=True)
    tot = lambda v: jnp.sum(v, axis=0, keepdims=True)
    first_at = lambda v: jnp.min(jnp.where(v == top(v), row, 2 * rows), axis=0, keepdims=True)
    is_g = (row >= N_EXPERTS) & (row < N_EXPERTS + N_GROUPS)
    lg = jnp.where(is_g, logits, NEG_BIG)
    pg = jnp.where(is_g, jnp.exp(lg - top(lg)), 0.0)
    group_p = pg / tot(pg)
    g_row = first_at(jnp.where(is_g, group_p + bias, NEG_BIG))
    g_w = tot(jnp.where(row == g_row, group_p, 0.0))
    sel = (row < N_EXPERTS) & ((row // EXPERTS_PER_GROUP) == (g_row - N_EXPERTS))
    le = jnp.where(sel, logits, NEG_BIG)
    pe = jnp.where(sel, jnp.exp(le - top(le)), 0.0)
    e_p = pe / tot(pe)
    score = jnp.where(sel, e_p + bias, NEG_BIG)
    i1 = first_at(score)
    i2 = first_at(jnp.where(row == i1, NEG_BIG, score))
    oh1 = row == i1
    oh2 = row == i2
    w1 = tot(jnp.where(oh1, e_p, 0.0))
    w2 = tot(jnp.where(oh2, e_p, 0.0))
    wsum = w1 + w2

    ohs = (oh1 | oh2).astype(BF16)
    earlier = (lax.broadcasted_iota(I32, (tm, tm), 0) < lax.broadcasted_iota(I32, (tm, tm), 1)).astype(BF16)
    before = _dot(ohs, earlier) + run_ref[...]
    r1 = tot(jnp.where(oh1, before, 0.0))
    r2 = tot(jnp.where(oh2, before, 0.0))
    run_ref[...] = run_ref[...] + jnp.sum(ohs.astype(F32), axis=1, keepdims=True)
    cnt_ref[...] = run_ref[...]

    zero = jnp.zeros_like(w1)
    meta = jnp.concatenate([i1.astype(F32), i2.astype(F32), r1, r2, g_w * (w1 / wsum), g_w * (w2 / wsum),
                            zero, zero], axis=0)
    meta_ref[...] = meta
    eye8 = (lax.broadcasted_iota(I32, (8, LANES), 0) == lax.broadcasted_iota(I32, (8, LANES), 1)).astype(F32)
    gates_ref[...] = sum(_dot_tn(part, eye8) for part in _split3(meta))[:, 4:6]


def _outproj(x2d, o_swa, o_gdn, wo_bf16, gain, w_router, b_router):
    n = x2d.shape[0]
    tm = ROW_TILE
    row = lambda w: pl.BlockSpec((tm, w), lambda i: (i, 0))
    full = lambda a: pl.BlockSpec(a.shape, lambda i: (0,) * a.ndim)
    return pl.pallas_call(
        _outproj_kernel,
        grid=(n // tm,),
        in_specs=[row(D_MODEL), row(SWA_Q), row(GDN_V), full(wo_bf16), full(gain), full(w_router), full(b_router)],
        out_specs=[row(D_MODEL), row(PACKED_W), row(2), pl.BlockSpec((8, tm), lambda i: (0, i)),
                   pl.BlockSpec((LANES, 1), lambda i: (0, 0))],
        out_shape=[jax.ShapeDtypeStruct((n, D_MODEL), F32), jax.ShapeDtypeStruct((n, PACKED_W), U32),
                   jax.ShapeDtypeStruct((n, 2), F32), jax.ShapeDtypeStruct((8, n), F32),
                   jax.ShapeDtypeStruct((LANES, 1), F32)],
        scratch_shapes=[pltpu.VMEM((LANES, 1), F32)],
        compiler_params=pltpu.CompilerParams(dimension_semantics=("arbitrary",)),
        name="outproj_router",
    )(x2d, o_swa, o_gdn, wo_bf16, gain, w_router, b_router)


def _row_copy(src_ref, src_row, dst_ref, dst_row, sem):
    return pltpu.make_async_copy(src_ref.at[pl.ds(src_row, 1)], dst_ref.at[pl.ds(dst_row, 1)], sem)


def _scatter_kernel(dest0_ref, dest1_ref, xp_ref, xs_ref, wg_ref, wu_ref, wd_ref,
                    out_ref, aidx_ref, wg_l, wu_l, wd_l, sem, *, tiles_p, n_tokens, weight_steps):
    i = pl.program_id(0)
    tm = xp_ref.shape[0]

    def start_rows(src_ref):
        def issue(g, carry):
            for u in range(DMA_UNROLL):
                r = g * DMA_UNROLL + u
                d0, d1 = dest0_ref[r], dest1_ref[r]
                _row_copy(src_ref, r, out_ref, d0, sem).start(priority=0)
                _row_copy(src_ref, r, out_ref, d1, sem).start(priority=1)
                aidx_ref[d0] = i * tm + r
                aidx_ref[d1] = n_tokens + i * tm + r
            return carry

        lax.fori_loop(0, tm // DMA_UNROLL, issue, 0)

    pl.when(i < tiles_p)(functools.partial(start_rows, xp_ref))
    pl.when(i >= tiles_p)(functools.partial(start_rows, xs_ref))

    @pl.when(i < weight_steps)
    def _():
        wg_l[...] = wg_ref[...].astype(BF16)
        wu_l[...] = wu_ref[...].astype(BF16)
        wd_l[...] = wd_ref[...].astype(BF16)

    for _ in range(2):
        pltpu.make_async_copy(xp_ref, out_ref.at[pl.ds(0, tm)], sem).wait()


def _scatter(dest, xn_p, xn_s, w_gate, w_up, w_down):
    tm = ROW_TILE
    tiles_p, tiles_s = xn_p.shape[0] // tm, xn_s.shape[0] // tm
    steps = tiles_p + tiles_s
    rows = 2 * (xn_p.shape[0] + xn_s.shape[0])
    per_step = -(-N_EXPERTS // steps)
    weight_steps = N_EXPERTS // per_step
    assert weight_steps * per_step == N_EXPERTS and weight_steps <= steps
    idx = pl.BlockSpec((tm,), lambda i: (i,), memory_space=pltpu.SMEM)
    wblk = lambda a: pl.BlockSpec((per_step,) + a.shape[1:], lambda i: (jnp.minimum(i, weight_steps - 1), 0, 0))
    weights = (w_gate, w_up, w_down)
    return pl.pallas_call(
        functools.partial(_scatter_kernel, tiles_p=tiles_p, n_tokens=rows // 2, weight_steps=weight_steps),
        grid=(steps,),
        in_specs=[idx, idx,
                  pl.BlockSpec((tm, PACKED_W), lambda i: (jnp.minimum(i, tiles_p - 1), 0)),
                  pl.BlockSpec((tm, PACKED_W), lambda i: (jnp.maximum(i - tiles_p, 0), 0))]
                 + [wblk(a) for a in weights],
        out_specs=[pl.BlockSpec(memory_space=pl.ANY), pl.BlockSpec(memory_space=pltpu.SMEM)]
                  + [wblk(a) for a in weights],
        out_shape=[jax.ShapeDtypeStruct((rows, PACKED_W), U32), jax.ShapeDtypeStruct((rows,), I32)]
                  + [jax.ShapeDtypeStruct(a.shape, BF16) for a in weights],
        scratch_shapes=[pltpu.SemaphoreType.DMA(())],
        compiler_params=pltpu.CompilerParams(dimension_semantics=("arbitrary",)),
        name="scatter_rows",
    )(dest[0], dest[1], xn_p, xn_s, *weights)


def _experts_kernel(blk_ref, exp_ref, lo_ref, hi_ref, first_ref, pblk_ref,
                    aprev_ref, x_ref, wg_ref, wu_ref, wd_ref, ya_ref, ybuf, sem, *, n_items, n_blocks):
    wg_l, wu_l, wd_l = wg_ref.at[0], wu_ref.at[0], wd_ref.at[0]
    j = pl.program_id(0)
    lo, hi = lo_ref[j], hi_ref[j]
    slot = lax.rem(blk_ref[j], 2)
    quarter = MOE_BLOCK // 4

    def wait_rows(s):
        pltpu.make_async_copy(ybuf.at[s], ya_ref.at[pl.ds(0, MOE_BLOCK)], sem.at[s]).wait()

    def send_rows(s, group):
        for row in range(group * quarter, (group + 1) * quarter):
            _row_copy(ybuf.at[s], row, ya_ref, aprev_ref[0, 0, row], sem.at[s]).start(priority=row % 2)

    @pl.when((first_ref[j] == 1) & (blk_ref[j] >= 2))
    def _():
        wait_rows(slot)

    def item(is_first, send_prev):
        send = (lambda group: send_rows(1 - slot, group)) if send_prev else (lambda group: None)
        send(0)
        x_a, x_b = _unpack_halves(x_ref[...])
        gate = _dot(x_a, wg_l[0:PACKED_W, :]) + _dot(x_b, wg_l[PACKED_W:D_MODEL, :])
        send(1)
        up = _dot(x_a, wu_l[0:PACKED_W, :]) + _dot(x_b, wu_l[PACKED_W:D_MODEL, :])
        hid = (gate * jax.nn.sigmoid(gate)) * up
        send(2)
        y = _dot(hid.astype(BF16), wd_l[...])
        send(3)
        r = lax.broadcasted_iota(I32, (MOE_BLOCK, 1), 0)
        mine = (r >= lo) & (r < hi)
        ybuf[slot] = jnp.where(mine, y, 0.0 if is_first else ybuf[slot])

    live = hi > lo
    pl.when(live & (first_ref[j] == 1) & (blk_ref[j] >= 1))(functools.partial(item, True, True))
    pl.when(live & (first_ref[j] == 1) & (blk_ref[j] == 0))(functools.partial(item, True, False))
    pl.when(live & (first_ref[j] == 0))(functools.partial(item, False, False))

    @pl.when(j == n_items - 1)
    def _():
        last_slot = (n_blocks - 1) % 2
        for group in range(4):
            send_rows(last_slot, group)
        wait_rows(1 - last_slot)
        wait_rows(last_slot)


def _experts(items, aidx, xs, w_gate, w_up, w_down):
    n_items = items[0].shape[0]
    n_blocks = xs.shape[0] // MOE_BLOCK
    assert n_items > n_blocks + N_EXPERTS - 1 and n_blocks >= 2
    xblk = pl.BlockSpec((MOE_BLOCK, PACKED_W), lambda j, blk, *_: (blk[j], 0))
    wspec = lambda a: pl.BlockSpec((1,) + a.shape[1:], lambda j, blk, ex, *_: (ex[j], 0, 0))
    return pl.pallas_call(
        functools.partial(_experts_kernel, n_items=n_items, n_blocks=n_blocks),
        grid_spec=pltpu.PrefetchScalarGridSpec(
            num_scalar_prefetch=len(items),
            grid=(n_items,),
            in_specs=[pl.BlockSpec((1, 1, MOE_BLOCK), lambda j, *pre: (pre[-1][j], 0, 0), memory_space=pltpu.SMEM),
                      xblk, wspec(w_gate), wspec(w_up), wspec(w_down)],
            out_specs=pl.BlockSpec(memory_space=pl.ANY),
            scratch_shapes=[pltpu.VMEM((2, MOE_BLOCK, D_MODEL), F32), pltpu.SemaphoreType.DMA((2,))]),
        out_shape=jax.ShapeDtypeStruct((xs.shape[0], D_MODEL), F32),
        compiler_params=pltpu.CompilerParams(dimension_semantics=("arbitrary",)),
        name="experts",
    )(*items, aidx.reshape(n_blocks, 1, MOE_BLOCK), xs, w_gate, w_up, w_down)


def _work_items(counts, total_rows):
    n_blocks = total_rows // MOE_BLOCK
    n_items = n_blocks + N_EXPERTS
    end = jnp.cumsum(counts)
    start = end - counts
    first_blk = start // MOE_BLOCK
    nb = jnp.where(counts > 0, (end - 1) // MOE_BLOCK - first_blk + 1, 0)
    item_end = jnp.cumsum(nb)
    used = item_end[-1]
    j = jnp.arange(n_items, dtype=I32)
    jj = jnp.minimum(j, used - 1)
    e = jnp.minimum(jnp.sum((item_end[None, :] <= jj[:, None]).astype(I32), axis=1), N_EXPERTS - 1)
    onehot = (e[:, None] == jnp.arange(N_EXPERTS, dtype=I32)[None, :]).astype(I32)
    pick = lambda a: jnp.sum(onehot * a[None, :], axis=1)
    blk = pick(first_blk) + (jj - (pick(item_end) - pick(nb)))
    lo = jnp.maximum(pick(start), blk * MOE_BLOCK) - blk * MOE_BLOCK
    hi = jnp.minimum(pick(end), (blk + 1) * MOE_BLOCK) - blk * MOE_BLOCK
    live = j < used
    lo = jnp.where(live, lo, 0)
    hi = jnp.where(live, hi, 0)
    prev_blk = jnp.concatenate([jnp.full((1,), -1, I32), blk[:-1]])
    first = (live & (blk != prev_blk)).astype(I32)
    pblk = jnp.where(j == n_items - 1, n_blocks - 1, jnp.maximum(blk - 1, 0))
    return (blk.astype(I32), e.astype(I32), lo.astype(I32), hi.astype(I32), first, pblk.astype(I32)), start


def _combine_kernel(gates_ref, h_ref, gf_ref, y0_ref, y1_ref, out_ref):
    gates = gates_ref[...]
    y = h_ref[...] + (y0_ref[...] * gates[:, 0:1] + y1_ref[...] * gates[:, 1:2])
    out_ref[...] = _rms(y, gf_ref[...])


def _combine(gates, h, gain, ya, first_token):
    n = h.shape[0]
    tm = COMBINE_TILE
    tile0 = first_token // tm
    slot_tiles = ya.shape[0] // 2 // tm
    row = lambda w: pl.BlockSpec((tm, w), lambda i: (i, 0))
    slot = lambda k: pl.BlockSpec((tm, D_MODEL), lambda i: (k * slot_tiles + tile0 + i, 0))
    return pl.pallas_call(
        _combine_kernel,
        grid=(n // tm,),
        in_specs=[row(2), row(D_MODEL), pl.BlockSpec(gain.shape, lambda i: (0, 0)), slot(0), slot(1)],
        out_specs=row(D_MODEL),
        out_shape=jax.ShapeDtypeStruct((n, D_MODEL), F32),
        compiler_params=pltpu.CompilerParams(dimension_semantics=("arbitrary",)),
        name="combine_norm",
    )(gates, h, gain, ya, ya)


def kernel(x_prompt, x_sample, cache_swa_k, cache_swa_v, state_gdn_conv, state_gdn, norm_mix, w_in, swa_sinks,
           gdn_conv_w, gdn_A_log, gdn_dt_bias, gdn_norm_w, w_out, norm_ffn, w_router_group, b_router_group,
           w_router_expert, b_router_expert, w_exp_gate, w_exp_up, w_exp_down, norm_final):
    depth = w_in.shape[0]
    assert depth == 1, "single trunk layer"
    bp, sp, _ = x_prompt.shape
    bs, ts, _ = x_sample.shape
    np_, ns = bp * sp, bs * ts
    l = 0

    w_in_b = w_in[l].astype(BF16)
    w_out_b = w_out[l].astype(BF16)
    g_mix = norm_mix[l].reshape(1, D_MODEL)
    g_ffn = norm_ffn[l].reshape(1, D_MODEL)
    g_fin = norm_final.reshape(1, D_MODEL)
    pad = LANES - N_EXPERTS - N_GROUPS
    w_router = jnp.concatenate([w_router_expert[l], w_router_group[l], jnp.zeros((D_MODEL, pad), F32)],
                               axis=1).astype(BF16).T
    b_router = jnp.concatenate([b_router_expert[l].reshape(-1), b_router_group[l],
                                jnp.zeros((pad,), F32)])[:, None]
    zeros8 = jnp.zeros((GDN_HEADS,), F32)
    alog16 = jnp.concatenate([gdn_A_log[l], zeros8])[None]
    dtb16 = jnp.concatenate([gdn_dt_bias[l], zeros8])[None]
    nw_group = jnp.tile(gdn_norm_w[l].reshape(1, HEAD_DIM), (1, GDN_GROUP))
    sinks = swa_sinks[l]

    q_p, k_p, v_p, c_p, z_p, ab_p = _inproj(x_prompt.reshape(np_, D_MODEL), g_mix, w_in_b)
    nblk = np_ // WINDOW
    k_p3 = k_p.reshape(nblk, WINDOW, SWA_KV)
    v_p3 = v_p.reshape(nblk, WINDOW, SWA_KV)
    (o_swa_p,) = _swa(sinks, q_p.reshape(nblk, WINDOW, SWA_Q), k_p3, v_p3, k_p3, v_p3,
                      bb=SWA_BLOCKS, blocks_per_seq=sp // WINDOW, emit_cache=False)
    c_p3 = c_p.reshape(bp, sp, GDN_CONV_CH)
    o_gdn_p, s_fin_p = _gdn(c_p3, jnp.zeros((bp, CONV_W - 1, GDN_CONV_CH), F32), z_p.reshape(bp, sp, GDN_V),
                            ab_p.reshape(bp, sp, 2 * GDN_HEADS),
                            jnp.zeros((bp, GDN_HEADS, HEAD_DIM, HEAD_DIM), F32),
                            gdn_conv_w[l], alog16, dtb16, nw_group, chunk=GDN_CHUNK, seq_block=bp)
    h_p, xn_p, gates_p, meta_p, cnt_p = _outproj(
        x_prompt.reshape(np_, D_MODEL), o_swa_p.reshape(np_, SWA_Q), o_gdn_p.reshape(np_, GDN_V),
        w_out_b, g_ffn, w_router, b_router)

    q_s, k_s, v_s, c_s, z_s, ab_s = _inproj(x_sample.reshape(ns, D_MODEL), g_mix, w_in_b)
    feature_major = lambda a: jnp.swapaxes(a.reshape(bs, WINDOW, SWA_KV), 1, 2)
    o_swa_s, kcache_t, vcache_t = _swa(
        sinks, q_s.reshape(bs, ts, SWA_Q), k_s.reshape(bs, ts, SWA_KV), v_s.reshape(bs, ts, SWA_KV),
        feature_major(cache_swa_k[l]), feature_major(cache_swa_v[l]),
        bb=16, blocks_per_seq=None, emit_cache=True)
    kcache_s, vcache_s = jnp.swapaxes(kcache_t, 1, 2), jnp.swapaxes(vcache_t, 1, 2)
    c_s3 = c_s.reshape(bs, ts, GDN_CONV_CH)
    o_gdn_s, s_fin_s = _gdn(c_s3, state_gdn_conv[l], z_s.reshape(bs, ts, GDN_V),
                            ab_s.reshape(bs, ts, 2 * GDN_HEADS), state_gdn[l],
                            gdn_conv_w[l], alog16, dtb16, nw_group, chunk=ts, seq_block=16)
    h_s, xn_s, gates_s, meta_s, cnt_s = _outproj(
        x_sample.reshape(ns, D_MODEL), o_swa_s.reshape(ns, SWA_Q), o_gdn_s.reshape(ns, GDN_V),
        w_out_b, g_ffn, w_router, b_router)

    cnt_p_i = cnt_p[:N_EXPERTS, 0].astype(I32)
    cnt_s_i = cnt_s[:N_EXPERTS, 0].astype(I32)
    items, start = _work_items(cnt_p_i + cnt_s_i, 2 * (np_ + ns))
    expert_ids = jnp.arange(N_EXPERTS, dtype=I32)
    lookup = lambda table, ids: jnp.sum(jnp.where(ids[..., None] == expert_ids, table, 0), axis=-1)
    dest_p = lookup(start, meta_p[0:2].astype(I32)) + meta_p[2:4].astype(I32)
    dest_s = lookup(start + cnt_p_i, meta_s[0:2].astype(I32)) + meta_s[2:4].astype(I32)
    xs, aidx, wg_l, wu_l, wd_l = _scatter(jnp.concatenate([dest_p, dest_s], axis=1), xn_p, xn_s,
                                          w_exp_gate[l], w_exp_up[l], w_exp_down[l])
    ya = _experts(items, aidx, xs, wg_l, wu_l, wd_l)
    y_p = _combine(gates_p, h_p, g_fin, ya, 0)
    y_s = _combine(gates_s, h_s, g_fin, ya, np_)

    kv5 = lambda a, b: a.reshape(b, -1, SWA_KV_HEADS, HEAD_DIM)[None]
    return (y_p.reshape(bp, sp, D_MODEL), y_s.reshape(bs, ts, D_MODEL),
            kv5(k_p.reshape(bp, sp, SWA_KV)[:, -WINDOW:], bp), kv5(v_p.reshape(bp, sp, SWA_KV)[:, -WINDOW:], bp),
            kv5(kcache_s, bs), kv5(vcache_s, bs),
            c_p3[:, -(CONV_W - 1):][None], c_s3[:, -(CONV_W - 1):][None],
            s_fin_p[None], s_fin_s[None])
```

```python
import functools

import jax
import jax.numpy as jnp
from jax import lax
from jax.experimental import pallas as pl
from jax.experimental.pallas import tpu as pltpu

F32 = jnp.float32
BF16 = jnp.bfloat16
I32 = jnp.int32
U32 = jnp.uint32

D_MODEL = 1024
HEAD_DIM = 64
SWA_HEADS = 8
GDN_HEADS = 8
SWA_KV_HEADS = 2
GQA_GROUP = SWA_HEADS // SWA_KV_HEADS
WINDOW = 128
ATTN_SCALE = HEAD_DIM ** -0.5
CONV_W = 4
N_GROUPS = 8
EXPERTS_PER_GROUP = 8
N_EXPERTS = 64
D_EXPERT = 256
EPS = 1e-6

SWA_Q = SWA_HEADS * HEAD_DIM
SWA_KV = SWA_KV_HEADS * HEAD_DIM
GDN_QK = GDN_HEADS * HEAD_DIM
GDN_V = GDN_HEADS * HEAD_DIM
GDN_CONV_CH = 2 * GDN_QK + GDN_V
D_MIX = SWA_Q + GDN_V
D_IN = SWA_Q + 2 * SWA_KV + GDN_CONV_CH + GDN_V + 2 * GDN_HEADS
COL_K = SWA_Q
COL_V = COL_K + SWA_KV
COL_C = COL_V + SWA_KV
COL_Z = COL_C + GDN_CONV_CH
COL_AB = COL_Z + GDN_V
PACKED_W = D_MODEL // 2

LANES = 128
NEG_BIG = -1e30
ROW_TILE = 512
INPROJ_TILE = 1024
MOE_BLOCK = 256
COMBINE_TILE = 512
DMA_UNROLL = 8
SWA_BLOCKS = 4
GDN_CHUNK = 64
GDN_TILE = 256
GDN_SCAN_CHUNKS = 4
GDN_GROUP = 4
GDN_GROUP_W = GDN_GROUP * HEAD_DIM


def _rms(x, g):
    return x * lax.rsqrt(jnp.mean(x * x, axis=-1, keepdims=True) + EPS) * g


def _dot(a, b):
    return jnp.dot(a, b, preferred_element_type=F32)


def _dot_nt(a, b):
    return lax.dot_general(a, b, (((1,), (1,)), ((), ())), preferred_element_type=F32)


def _dot_tn(a, b):
    return lax.dot_general(a, b, (((0,), (0,)), ((), ())), preferred_element_type=F32)


def _pack_halves(x):
    w = x.shape[1] // 2
    bits = lambda v: lax.bitcast_convert_type(v.astype(BF16).astype(F32), U32)
    return bits(x[:, :w]) | (bits(x[:, w:]) >> 16)


def _unpack_halves(p):
    hi = lax.bitcast_convert_type(p & jnp.uint32(0xFFFF0000), F32)
    lo = lax.bitcast_convert_type(p << 16, F32)
    return hi.astype(BF16), lo.astype(BF16)


def _split3(x):
    p1 = x.astype(BF16).astype(F32)
    r = x - p1
    p2 = r.astype(BF16).astype(F32)
    p3 = (r - p2).astype(BF16).astype(F32)
    return p1, p2, p3


def _inproj_kernel(x_ref, g_ref, w_ref, q_ref, k_ref, v_ref, c_ref, z_ref, ab_ref):
    x = x_ref[...]
    xb = _rms(x, g_ref[...]).astype(BF16)
    q_ref[...] = _dot(xb, w_ref[:, 0:COL_K])
    k_ref[...] = _dot(xb, w_ref[:, COL_K:COL_V])
    v_ref[...] = _dot(xb, w_ref[:, COL_V:COL_C])
    c_ref[...] = _dot(xb, w_ref[:, COL_C:COL_Z])
    z_ref[...] = _dot(xb, w_ref[:, COL_Z:COL_AB])
    ab_ref[...] = _dot(xb, w_ref[:, COL_AB:D_IN])


def _inproj(x2d, gain, w_bf16):
    n = x2d.shape[0]
    tm = INPROJ_TILE
    row = lambda w: pl.BlockSpec((tm, w), lambda i: (i, 0))
    full = lambda a: pl.BlockSpec(a.shape, lambda i: (0,) * a.ndim)
    widths = (SWA_Q, SWA_KV, SWA_KV, GDN_CONV_CH, GDN_V, 2 * GDN_HEADS)
    return pl.pallas_call(
        _inproj_kernel,
        grid=(n // tm,),
        in_specs=[row(D_MODEL), full(gain), full(w_bf16)],
        out_specs=[row(w) for w in widths],
        out_shape=[jax.ShapeDtypeStruct((n, w), F32) for w in widths],
        compiler_params=pltpu.CompilerParams(dimension_semantics=("arbitrary",)),
        name="inproj",
    )(x2d, gain, w_bf16)


def _swa_kernel(sink_ref, q_ref, kc_ref, vc_ref, kp_ref, vp_ref, o_ref, *cache_refs,
                bb, t, blocks_per_seq, emit_cache):
    rows = GQA_GROUP * t
    ri = lax.broadcasted_iota(I32, (rows, 1), 0)
    qi = lax.rem(ri, t)
    gi = ri // t
    if emit_cache:
        kj = lax.broadcasted_iota(I32, (rows, WINDOW), 1)
        mask = kj <= qi + (WINDOW - t)
        mask_first = mask
        place = (lax.broadcasted_iota(I32, (t, WINDOW), 1)
                 == lax.broadcasted_iota(I32, (t, WINDOW), 0) + (WINDOW - t)).astype(F32)
        is_new = lax.broadcasted_iota(I32, (SWA_KV, WINDOW), 1) >= WINDOW - t
    else:
        kj = lax.broadcasted_iota(I32, (rows, WINDOW + t), 1)
        mask = (kj <= qi + WINDOW) & (kj > qi)
        has_prev = lax.rem(pl.program_id(0) * bb, blocks_per_seq) != 0
        mask_first = mask & (has_prev | (kj >= WINDOW))
    sinks = []
    for h in range(SWA_KV_HEADS):
        sink = jnp.zeros((rows, 1), F32)
        for g in range(GQA_GROUP):
            sink = jnp.where(gi == g, sink_ref[GQA_GROUP * h + g], sink)
        sinks.append(sink)
    if emit_cache:
        def updated(old_t, new):
            placed = sum(_dot_tn(part, place) for part in _split3(new))
            return jnp.where(is_new, placed, pltpu.roll(old_t, WINDOW - t, axis=1))

        caches = [(updated(kp_ref[b], kc_ref[b]), updated(vp_ref[b], vc_ref[b])) for b in range(bb)]
        for b in range(bb):
            cache_refs[0][b] = caches[b][0]
            cache_refs[1][b] = caches[b][1]
    chains = [(b, h) for b in range(bb) for h in range(SWA_KV_HEADS)]
    scores, values = [], []
    for b, h in chains:
        hs = slice(h * HEAD_DIM, (h + 1) * HEAD_DIM)
        q4 = jnp.concatenate(
            [q_ref[b, :, (GQA_GROUP * h + g) * HEAD_DIM:(GQA_GROUP * h + g + 1) * HEAD_DIM]
             for g in range(GQA_GROUP)], axis=0)
        if emit_cache:
            keys_t, vals_t = caches[b][0][hs, :], caches[b][1][hs, :]
            scores.append(_dot(q4.astype(BF16), keys_t.astype(BF16)))
            values.append(vals_t.astype(BF16))
        else:
            kp, vp = (kp_ref[0, :, hs], vp_ref[0, :, hs]) if b == 0 else (kc_ref[b - 1, :, hs], vc_ref[b - 1, :, hs])
            keys = jnp.concatenate([kp, kc_ref[b, :, hs]], axis=0)
            vals = jnp.concatenate([vp, vc_ref[b, :, hs]], axis=0)
            scores.append(_dot_nt(q4.astype(BF16), keys.astype(BF16)))
            values.append(vals.astype(BF16))
    probs, dens = [], []
    for (b, h), s in zip(chains, scores):
        s = jnp.where(mask_first if b == 0 else mask, s * ATTN_SCALE, NEG_BIG)
        m = jnp.maximum(jnp.max(s, axis=-1, keepdims=True), sinks[h])
        p = jnp.exp(s - m)
        dens.append(jnp.sum(p, axis=-1, keepdims=True) + jnp.exp(sinks[h] - m))
        probs.append(p.astype(BF16))
    pv = _dot_nt if emit_cache else _dot
    outs = [pv(p, v) / den for p, v, den in zip(probs, values, dens)]
    for b in range(bb):
        o_ref[b] = jnp.concatenate([outs[b * SWA_KV_HEADS + h][g * t:(g + 1) * t]
                                    for h in range(SWA_KV_HEADS) for g in range(GQA_GROUP)], axis=-1)


def _swa(sinks, q3, k3, v3, kprev3, vprev3, *, bb, blocks_per_seq, emit_cache):
    nb, t, _ = q3.shape
    cur = lambda w: pl.BlockSpec((bb, t, w), lambda i: (i, 0, 0))
    if emit_cache:
        prev = pl.BlockSpec((bb, SWA_KV, WINDOW), lambda i: (i, 0, 0))
    else:
        assert blocks_per_seq % bb == 0
        prev = pl.BlockSpec((1, WINDOW, SWA_KV), lambda i: (jnp.maximum(i * bb - 1, 0), 0, 0))
    out_specs = [cur(SWA_Q)]
    out_shape = [jax.ShapeDtypeStruct((nb, t, SWA_Q), F32)]
    if emit_cache:
        out_specs += [prev, prev]
        out_shape += [jax.ShapeDtypeStruct((nb, SWA_KV, WINDOW), F32)] * 2
    return pl.pallas_call(
        functools.partial(_swa_kernel, bb=bb, t=t, blocks_per_seq=blocks_per_seq, emit_cache=emit_cache),
        grid=(nb // bb,),
        in_specs=[pl.BlockSpec(memory_space=pltpu.SMEM), cur(SWA_Q), cur(SWA_KV), cur(SWA_KV), prev, prev],
        out_specs=out_specs,
        out_shape=out_shape,
        compiler_params=pltpu.CompilerParams(dimension_semantics=("arbitrary",)),
        name="swa_cache" if emit_cache else "swa_band",
    )(sinks, q3, k3, v3, kprev3, vprev3)


def _gdn_prep_kernel(c_ref, hist_ref, ab_ref, cw_ref, alog_ref, dtb_ref,
                     u_ref, w_ref, qd_ref, kd_ref, qk_ref, gt_ref, cbuf_ref, *, chunk):
    sb, r, _ = c_ref.shape
    tp = sb * r
    cn = chunk
    low = w_ref.dtype
    hist_rows = CONV_W - 1

    @pl.when(pl.program_id(1) == 0)
    def _():
        cbuf_ref[:, 8 - hist_rows:8, :] = hist_ref[...]

    cbuf_ref[:, 8:8 + r, :] = c_ref[...]
    conv = cbuf_ref[:, 8 - hist_rows:8 - hist_rows + r, :] * cw_ref[0:1, :]
    for i in range(1, CONV_W):
        conv = conv + cbuf_ref[:, 8 - hist_rows + i:8 - hist_rows + i + r, :] * cw_ref[i:i + 1, :]
    tail = cbuf_ref[:, 8 + r - hist_rows:8 + r, :]
    cbuf_ref[:, 8 - hist_rows:8, :] = tail
    conv = (conv * jax.nn.sigmoid(conv)).reshape(tp, GDN_CONV_CH)

    ab = ab_ref[...].reshape(tp, 2 * GDN_HEADS)
    is_g = lax.broadcasted_iota(I32, (1, 2 * GDN_HEADS), 1) < GDN_HEADS
    g = jnp.where(is_g, -jnp.exp(alog_ref[...]) * jax.nn.softplus(ab + dtb_ref[...]), 0.0)
    beta = jax.nn.sigmoid(ab)

    ri = lax.broadcasted_iota(I32, (tp, tp), 0)
    ci = lax.broadcasted_iota(I32, (tp, tp), 1)
    same = (ri // cn) == (ci // cn)
    causal = same & (ri >= ci)
    strict = same & (ri > ci)
    eye = (ri == ci).astype(F32)
    stack = jnp.concatenate([causal.astype(BF16), same.astype(BF16)], axis=0)
    both = sum(_dot(stack, p.astype(BF16)) for p in _split3(g))
    gc, gl = both[:tp], both[tp:]
    e16 = lax.broadcasted_iota(I32, (2 * GDN_HEADS, 2 * GDN_HEADS), 0)
    eye16 = (e16 == lax.broadcasted_iota(I32, (2 * GDN_HEADS, 2 * GDN_HEADS), 1)).astype(BF16)
    gc_row = sum(_dot_nt(eye16, p.astype(BF16)) for p in _split3(gc))
    gt_ref[...] = jnp.exp(gl)
    fold = (lax.broadcasted_iota(I32, (tp, cn), 0) % cn == lax.broadcasted_iota(I32, (tp, cn), 1)).astype(low)

    levels = cn.bit_length() - 2
    heads = range(GDN_HEADS)
    lanes_of = lambda i, h: slice(i * GDN_QK + h * HEAD_DIM, i * GDN_QK + (h + 1) * HEAD_DIM)
    qs = [conv[:, lanes_of(0, h)] for h in heads]
    ks = [conv[:, lanes_of(1, h)] for h in heads]
    qs = [q * lax.rsqrt(jnp.sum(q * q, axis=-1, keepdims=True) + EPS) * (HEAD_DIM ** -0.5) for q in qs]
    ks = [k * lax.rsqrt(jnp.sum(k * k, axis=-1, keepdims=True) + EPS) for k in ks]
    g_cs = [gc[:, h:h + 1] for h in heads]
    b_hs = [beta[:, GDN_HEADS + h:GDN_HEADS + h + 1] for h in heads]
    decays = [jnp.exp(jnp.where(causal, g_cs[h] - gc_row[h:h + 1, :], NEG_BIG)) for h in heads]
    egs = [jnp.exp(g_c) for g_c in g_cs]
    kbs = [k * b_h for k, b_h in zip(ks, b_hs)]
    k_ls = [k.astype(low) for k in ks]
    lmats = [jnp.where(strict, _dot_nt(kb.astype(low), k_l) * decay, 0.0) for kb, k_l, decay in zip(kbs, k_ls, decays)]
    qks = [_dot((_dot_nt(q.astype(low), k_l) * decay).astype(low), fold) for q, k_l, decay in zip(qs, k_ls, decays)]
    xs = [eye - lmat for lmat in lmats]
    ps = [_dot(lmat.astype(low), lmat.astype(low)) for lmat in lmats]
    for lev in range(1, levels + 1):
        p_ls = [p.astype(low) for p in ps]
        if lev < levels:
            xp = [_dot(jnp.concatenate([x.astype(low), p_l], axis=0), p_l) for x, p_l in zip(xs, p_ls)]
            xs = [x + m[:tp] for x, m in zip(xs, xp)]
            ps = [m[tp:] for m in xp]
        else:
            xs = [x + _dot(x.astype(low), p_l) for x, p_l in zip(xs, p_ls)]
    sols = [_dot(xs[h].astype(low),
                 jnp.concatenate([conv[:, lanes_of(2, h)] * b_hs[h], kbs[h] * egs[h]], axis=-1).astype(low))
            for h in heads]
    u_ref[...] = jnp.concatenate([sol[:, :HEAD_DIM] for sol in sols], axis=-1)
    w_ref[...] = jnp.concatenate([sol[:, HEAD_DIM:] for sol in sols], axis=-1).astype(low)
    qd_ref[...] = jnp.concatenate([q * eg for q, eg in zip(qs, egs)], axis=-1).astype(low)
    kd_ref[...] = jnp.concatenate([ks[h] * jnp.exp(gl[:, h:h + 1] - g_cs[h]) for h in heads], axis=-1).astype(low)
    qk_ref[...] = jnp.concatenate(qks, axis=-1).astype(low)


def _gdn_prep_pair_kernel(c_ref, hist_ref, ab_ref, cw_ref, alog_ref, dtb_ref,
                          u_ref, w_ref, qd_ref, kd_ref, qk_ref, gt_ref, cbuf_ref):
    _, r, _ = c_ref.shape
    tp = r
    cn = HEAD_DIM
    pair_w = 2 * HEAD_DIM
    hist_rows = CONV_W - 1
    sublanes = 8

    @pl.when(pl.program_id(1) == 0)
    def _():
        cbuf_ref[...] = jnp.zeros_like(cbuf_ref)
        cbuf_ref[:, sublanes - hist_rows:sublanes, :] = hist_ref[...]

    x3 = c_ref[0].reshape(tp // sublanes, sublanes, GDN_CONV_CH)
    before = cbuf_ref[...]
    sub = lax.broadcasted_iota(I32, (1, sublanes, 1), 1)
    conv3 = x3 * cw_ref[CONV_W - 1:CONV_W, :]
    for s in range(1, CONV_W):
        rx = pltpu.roll(x3, s, axis=1)
        rp = jnp.concatenate([pltpu.roll(before, s, axis=1), rx[:-1]], axis=0)
        conv3 = conv3 + jnp.where(sub < s, rp, rx) * cw_ref[CONV_W - 1 - s:CONV_W - s, :]
    cbuf_ref[...] = x3[tp // sublanes - 1:]
    conv = (conv3 * jax.nn.sigmoid(conv3)).reshape(tp, GDN_CONV_CH)

    ab = ab_ref[...].reshape(tp, 2 * GDN_HEADS)
    is_g = lax.broadcasted_iota(I32, (1, 2 * GDN_HEADS), 1) < GDN_HEADS
    g = jnp.where(is_g, -jnp.exp(alog_ref[...]) * jax.nn.softplus(ab + dtb_ref[...]), 0.0)
    beta = jax.nn.sigmoid(ab)

    ri = lax.broadcasted_iota(I32, (tp, tp), 0)
    ci = lax.broadcasted_iota(I32, (tp, tp), 1)
    same = (ri // cn) == (ci // cn)
    same_l = same.astype(BF16)
    stack = jnp.concatenate([(same & (ri >= ci)).astype(BF16), same_l], axis=0)
    both = sum(_dot(stack, p.astype(BF16)) for p in _split3(g))
    gc, gl = both[:tp], both[tp:]
    gt_ref[...] = jnp.exp(gl)

    lane = lax.broadcasted_iota(I32, (tp, pair_w), 1)
    c_in = lax.broadcasted_iota(I32, (tp, pair_w), 0) % cn
    j_in = lane % cn
    left = lane < cn
    causal = c_in >= j_in
    strict = c_in > j_in
    diag = c_in == j_in
    eye = diag.astype(F32)
    bdmask = ((lax.broadcasted_iota(I32, (pair_w, pair_w), 0) // cn)
              == (lax.broadcasted_iota(I32, (pair_w, pair_w), 1) // cn))
    ones_bd = bdmask.astype(BF16)

    def bd(m):
        return jnp.where(bdmask, jnp.concatenate([m, m], axis=0), 0.0).astype(BF16)

    def head_sum(x):
        hi = x.astype(BF16)
        lo = (x - hi.astype(F32)).astype(BF16)
        return _dot(hi, ones_bd) + _dot(lo, ones_bd)

    levels = cn.bit_length() - 2
    chunks = [slice(n * cn, (n + 1) * cn) for n in range(tp // cn)]
    n_pairs = GDN_HEADS // 2
    pairs = range(n_pairs)
    pick = lambda m, off, p: jnp.where(left, m[:, off + 2 * p:off + 2 * p + 1], m[:, off + 2 * p + 1:off + 2 * p + 2])
    third = lambda i, p: conv[:, i * GDN_QK + p * pair_w:i * GDN_QK + (p + 1) * pair_w]
    qs = [third(0, p) for p in pairs]
    ks = [third(1, p) for p in pairs]
    qs = [q * lax.rsqrt(head_sum(q * q) + EPS) * (HEAD_DIM ** -0.5) for q in qs]
    ks = [k * lax.rsqrt(head_sum(k * k) + EPS) for k in ks]
    gcps = [pick(gc, 0, p) for p in pairs]
    rowms = [sum(_dot(same_l, part.astype(BF16)) for part in _split3(jnp.where(diag, gcp, 0.0))) for gcp in gcps]
    decays = [jnp.exp(jnp.where(causal, gcp - rowm, NEG_BIG)) for gcp, rowm in zip(gcps, rowms)]
    lmats, vbs, kbegs = [], [], []
    for p in pairs:
        ls = slice(p * pair_w, (p + 1) * pair_w)
        q, k, gcp, decay = qs[p], ks[p], gcps[p], decays[p]
        bp = pick(beta, GDN_HEADS, p)
        eg = jnp.exp(gcp)
        kb = k * bp
        vbs.append(third(2, p) * bp)
        kbegs.append(kb * eg)
        qd_ref[:, ls] = (q * eg).astype(BF16)
        kd_ref[:, ls] = (k * jnp.exp(pick(gl, 0, p) - gcp)).astype(BF16)
        q_l, kb_l = q.astype(BF16), kb.astype(BF16)
        kbd = [bd(k[rs]) for rs in chunks]
        kk = jnp.concatenate([_dot_nt(kb_l[rs], kbd[n]) for n, rs in enumerate(chunks)], axis=0)
        qk = jnp.concatenate([_dot_nt(q_l[rs], kbd[n]) for n, rs in enumerate(chunks)], axis=0)
        qk_ref[:, ls] = (qk * decay).astype(BF16)
        lmats.append(jnp.where(strict, kk * decay, 0.0))
    bodies = [(p, rs) for p in range(n_pairs) for rs in chunks]
    xs = [eye[rs] - lmats[p][rs] for p, rs in bodies]
    ps = [_dot(lmats[p][rs].astype(BF16), bd(lmats[p][rs])) for p, rs in bodies]
    for lev in range(1, levels + 1):
        pbd = [bd(pm) for pm in ps]
        if lev < levels:
            xp = [_dot(jnp.concatenate([xm, pm], axis=0).astype(BF16), wm) for xm, pm, wm in zip(xs, ps, pbd)]
            xs = [xm + m[:cn] for xm, m in zip(xs, xp)]
            ps = [m[cn:] for m in xp]
        else:
            xs = [xm + _dot(xm.astype(BF16), wm) for xm, wm in zip(xs, pbd)]
    x_l = [xm.astype(BF16) for xm in xs]
    us = [_dot(xm, bd(vbs[p][rs])) for xm, (p, rs) in zip(x_l, bodies)]
    ws = [_dot(xm, bd(kbegs[p][rs])) for xm, (p, rs) in zip(x_l, bodies)]
    nc = len(chunks)
    for p in range(n_pairs):
        ls = slice(p * pair_w, (p + 1) * pair_w)
        u_ref[:, ls] = jnp.concatenate(us[p * nc:(p + 1) * nc], axis=0)
        w_ref[:, ls] = jnp.concatenate(ws[p * nc:(p + 1) * nc], axis=0).astype(BF16)


def _gdn_scan_kernel(u_ref, w_ref, qd_ref, kd_ref, qk_ref, gt_ref, z_ref, s0_ref, nw_ref,
                     o_ref, sfin_ref, sbd_ref, *, chunk, n_chunks):
    bb = u_ref.shape[0]
    cn = chunk
    low = w_ref.dtype
    gw = GDN_GROUP_W
    ni = pl.program_id(1)

    @pl.when(ni == 0)
    def _():
        sbd_ref[...] = jnp.zeros_like(sbd_ref)
        for b in range(bb):
            for h in range(GDN_HEADS):
                gi, hh = divmod(h, GDN_GROUP)
                ds = slice(hh * HEAD_DIM, (hh + 1) * HEAD_DIM)
                sbd_ref[b, gi, ds, ds] = s0_ref[b, h]

    bdmask = ((lax.broadcasted_iota(I32, (gw, gw), 0) // HEAD_DIM)
              == (lax.broadcasted_iota(I32, (gw, gw), 1) // HEAD_DIM))
    ones_bd = bdmask.astype(BF16)
    vmask = ((lax.broadcasted_iota(I32, (GDN_GROUP * cn, gw), 0) // cn)
             == (lax.broadcasted_iota(I32, (GDN_GROUP * cn, gw), 1) // HEAD_DIM))
    e_row = lax.broadcasted_iota(I32, (2 * GDN_HEADS, gw), 0)
    e_col = lax.broadcasted_iota(I32, (2 * GDN_HEADS, gw), 1) // HEAD_DIM
    chains = [(b, gi) for b in range(bb) for gi in range(GDN_HEADS // GDN_GROUP)]
    lanes = lambda gi: slice(gi * gw, (gi + 1) * gw)
    for c in range(u_ref.shape[1] // cn):
        rs = slice(c * cn, (c + 1) * cn)
        states = [sbd_ref[b, gi] for b, gi in chains]
        states_l = [s.astype(low) for s in states]
        wq_s = [_dot(jnp.concatenate([w_ref[b, rs, lanes(gi)], qd_ref[b, rs, lanes(gi)]], axis=0), s_l)
                for (b, gi), s_l in zip(chains, states_l)]
        v_new = [u_ref[b, rs, lanes(gi)] - m[:cn] for (b, gi), m in zip(chains, wq_s)]
        q_s = [m[cn:] for m in wq_s]
        v_l = [v.astype(low) for v in v_new]
        outs = []
        for (b, gi), s, v, qs in zip(chains, states, v_l, q_s):
            vbd = jnp.where(vmask, jnp.concatenate([v] * GDN_GROUP, axis=0), jnp.zeros((), low))
            outs.append(qs + _dot(qk_ref[b, rs, gi * GDN_GROUP * cn:(gi + 1) * GDN_GROUP * cn], vbd))
            upd = _dot_tn(kd_ref[b, rs, lanes(gi)], v)
            expand = (e_row == e_col + gi * GDN_GROUP).astype(BF16)
            decay8 = gt_ref[b, c * cn:c * cn + 8, :]
            gte = sum(_dot(p.astype(BF16), expand) for p in _split3(decay8))[0:1]
            sbd_ref[b, gi] = s * gte + jnp.where(bdmask, upd, 0.0)
        for (b, gi), o in zip(chains, outs):
            o2 = o * o
            hi = o2.astype(BF16)
            lo = (o2 - hi.astype(F32)).astype(BF16)
            if cn % 16 == 0:
                sums = _dot(jnp.concatenate([hi, lo], axis=0), ones_bd)
                ms = (sums[:cn] + sums[cn:]) * (1.0 / HEAD_DIM)
            else:
                ms = (_dot(hi, ones_bd) + _dot(lo, ones_bd)) * (1.0 / HEAD_DIM)
            zg = z_ref[b, rs, lanes(gi)]
            o_ref[b, rs, lanes(gi)] = o * lax.rsqrt(ms + EPS) * nw_ref[...] * (zg * jax.nn.sigmoid(zg))

    @pl.when(ni == n_chunks - 1)
    def _():
        for b in range(bb):
            for h in range(GDN_HEADS):
                gi, hh = divmod(h, GDN_GROUP)
                ds = slice(hh * HEAD_DIM, (hh + 1) * HEAD_DIM)
                sfin_ref[b, h] = sbd_ref[b, gi, ds, ds]


def _gdn(c3, hist, z3, ab3, s0, conv_w, alog16, dtb16, nw_group, *, chunk, seq_block):
    nseq, t, _ = c3.shape
    n = nseq * t
    sb, r = (1, GDN_TILE) if t >= GDN_TILE else (GDN_TILE // t, t)
    tiles = t // r
    low = BF16 if chunk >= 16 else F32
    blk = lambda w: pl.BlockSpec((sb, r, w), lambda s, i: (s, i, 0))
    full = lambda a: pl.BlockSpec(a.shape, lambda s, i: (0,) * a.ndim)
    flat = lambda w: pl.BlockSpec((GDN_TILE, w), lambda s, i: (s * tiles + i, 0))
    widths = (GDN_V, GDN_V, GDN_QK, GDN_QK, GDN_HEADS * chunk, 2 * GDN_HEADS)
    dtypes = (F32, low, low, low, low, F32)
    prep_out = dict(out_specs=[flat(wd) for wd in widths],
                    out_shape=[jax.ShapeDtypeStruct((n, wd), dt) for wd, dt in zip(widths, dtypes)],
                    compiler_params=pltpu.CompilerParams(dimension_semantics=("arbitrary", "arbitrary")))
    lane_dense = chunk == HEAD_DIM and sb == 1
    u, w, qd, kd, qk, gt = pl.pallas_call(
        _gdn_prep_pair_kernel if lane_dense else functools.partial(_gdn_prep_kernel, chunk=chunk),
        grid=(nseq // sb, tiles),
        in_specs=[blk(GDN_CONV_CH), pl.BlockSpec((sb, CONV_W - 1, GDN_CONV_CH), lambda s, i: (s, 0, 0)),
                  blk(2 * GDN_HEADS), full(conv_w), full(alog16), full(dtb16)],
        scratch_shapes=[pltpu.VMEM((1, 8, GDN_CONV_CH) if lane_dense else (sb, 8 + r, GDN_CONV_CH), F32)],
        name="gdn_prep_pair" if lane_dense else "gdn_prep", **prep_out)(c3, hist, ab3, conv_w, alog16, dtb16)

    rows_per_step = min(t, GDN_SCAN_CHUNKS * chunk)
    n_chunks = t // rows_per_step
    tok = lambda wd: pl.BlockSpec((seq_block, rows_per_step, wd), lambda s, c: (s, c, 0))
    per_seq = pl.BlockSpec((seq_block,) + s0.shape[1:], lambda s, c: (s, 0, 0, 0))
    seq3 = lambda a: a.reshape(nseq, t, a.shape[-1])
    return pl.pallas_call(
        functools.partial(_gdn_scan_kernel, chunk=chunk, n_chunks=n_chunks),
        grid=(nseq // seq_block, n_chunks),
        in_specs=[tok(wd) for wd in widths] + [tok(GDN_V), per_seq,
                                               pl.BlockSpec(nw_group.shape, lambda s, c: (0, 0))],
        out_specs=[tok(GDN_V), per_seq],
        out_shape=[jax.ShapeDtypeStruct((nseq, t, GDN_V), F32), jax.ShapeDtypeStruct(s0.shape, F32)],
        scratch_shapes=[pltpu.VMEM((seq_block, GDN_HEADS // GDN_GROUP, GDN_GROUP_W, GDN_GROUP_W), F32)],
        compiler_params=pltpu.CompilerParams(dimension_semantics=("arbitrary", "arbitrary")),
        name="gdn_scan",
    )(seq3(u), seq3(w), seq3(qd), seq3(kd), seq3(qk), seq3(gt), z3, s0, nw_group)


def _outproj_kernel(x_ref, osw_ref, ogd_ref, wo_ref, gf_ref, wr_ref, br_ref,
                    h_ref, xn_ref, gates_ref, meta_ref, cnt_ref, run_ref):
    i = pl.program_id(0)
    tm = x_ref.shape[0]
    rows = wr_ref.shape[0]

    @pl.when(i == 0)
    def _():
        run_ref[...] = jnp.zeros_like(run_ref)

    h = (x_ref[...] + _dot(osw_ref[...].astype(BF16), wo_ref[0:SWA_Q, :])
         + _dot(ogd_ref[...].astype(BF16), wo_ref[SWA_Q:D_MIX, :]))
    h_ref[...] = h
    xn = _rms(h, gf_ref[...])
    xn_ref[...] = _pack_halves(xn)
    logits = _dot_nt(wr_ref[...], xn.astype(BF16))

    row = lax.broadcasted_iota(I32, (rows, tm), 0)
    bias = br_ref[...]
    top = lambda v: jnp.max(v, axis=0, keepdims=True)
    tot = lambda v: jnp.sum(v, axis=0, keepdims=True)
    first_at = lambda v: jnp.min(jnp.where(v == top(v), row, 2 * rows), axis=0, keepdims=True)
    is_g = (row >= N_EXPERTS) & (row < N_EXPERTS + N_GROUPS)
    lg = jnp.where(is_g, logits, NEG_BIG)
    pg = jnp.where(is_g, jnp.exp(lg - top(lg)), 0.0)
    group_p = pg / tot(pg)
    g_row = first_at(jnp.where(is_g, group_p + bias, NEG_BIG))
    g_w = tot(jnp.where(row == g_row, group_p, 0.0))
    sel = (row < N_EXPERTS) & ((row // EXPERTS_PER_GROUP) == (g_row - N_EXPERTS))
    le = jnp.where(sel, logits, NEG_BIG)
    pe = jnp.where(sel, jnp.exp(le - top(le)), 0.0)
    e_p = pe / tot(pe)
    score = jnp.where(sel, e_p + bias, NEG_BIG)
    i1 = first_at(score)
    i2 = first_at(jnp.where(row == i1, NEG_BIG, score))
    oh1 = row == i1
    oh2 = row == i2
    w1 = tot(jnp.where(oh1, e_p, 0.0))
    w2 = tot(jnp.where(oh2, e_p, 0.0))
    wsum = w1 + w2

    ohs = (oh1 | oh2).astype(BF16)
    earlier = (lax.broadcasted_iota(I32, (tm, tm), 0) < lax.broadcasted_iota(I32, (tm, tm), 1)).astype(BF16)
    before = _dot(ohs, earlier) + run_ref[...]
    r1 = tot(jnp.where(oh1, before, 0.0))
    r2 = tot(jnp.where(oh2, before, 0.0))
    run_ref[...] = run_ref[...] + jnp.sum(ohs.astype(F32), axis=1, keepdims=True)
    cnt_ref[...] = run_ref[...]

    zero = jnp.zeros_like(w1)
    meta = jnp.concatenate([i1.astype(F32), i2.astype(F32), r1, r2, g_w * (w1 / wsum), g_w * (w2 / wsum),
                            zero, zero], axis=0)
    meta_ref[...] = meta
    eye8 = (lax.broadcasted_iota(I32, (8, LANES), 0) == lax.broadcasted_iota(I32, (8, LANES), 1)).astype(F32)
    gates_ref[...] = sum(_dot_tn(part, eye8) for part in _split3(meta))[:, 4:6]


def _outproj(x2d, o_swa, o_gdn, wo_bf16, gain, w_router, b_router):
    n = x2d.shape[0]
    tm = ROW_TILE
    row = lambda w: pl.BlockSpec((tm, w), lambda i: (i, 0))
    full = lambda a: pl.BlockSpec(a.shape, lambda i: (0,) * a.ndim)
    return pl.pallas_call(
        _outproj_kernel,
        grid=(n // tm,),
        in_specs=[row(D_MODEL), row(SWA_Q), row(GDN_V), full(wo_bf16), full(gain), full(w_router), full(b_router)],
        out_specs=[row(D_MODEL), row(PACKED_W), row(2), pl.BlockSpec((8, tm), lambda i: (0, i)),
                   pl.BlockSpec((LANES, 1), lambda i: (0, 0))],
        out_shape=[jax.ShapeDtypeStruct((n, D_MODEL), F32), jax.ShapeDtypeStruct((n, PACKED_W), U32),
                   jax.ShapeDtypeStruct((n, 2), F32), jax.ShapeDtypeStruct((8, n), F32),
                   jax.ShapeDtypeStruct((LANES, 1), F32)],
        scratch_shapes=[pltpu.VMEM((LANES, 1), F32)],
        compiler_params=pltpu.CompilerParams(dimension_semantics=("arbitrary",)),
        name="outproj_router",
    )(x2d, o_swa, o_gdn, wo_bf16, gain, w_router, b_router)


def _row_copy(src_ref, src_row, dst_ref, dst_row, sem):
    return pltpu.make_async_copy(src_ref.at[pl.ds(src_row, 1)], dst_ref.at[pl.ds(dst_row, 1)], sem)


def _scatter_kernel(dest0_ref, dest1_ref, xp_ref, xs_ref, out_ref, aidx_ref, sem, *, tiles_p, n_tokens):
    i = pl.program_id(0)
    tm = xp_ref.shape[0]

    def run(src_ref):
        def issue(g, carry):
            for u in range(DMA_UNROLL):
                r = g * DMA_UNROLL + u
                d0, d1 = dest0_ref[r], dest1_ref[r]
                _row_copy(src_ref, r, out_ref, d0, sem).start(priority=0)
                _row_copy(src_ref, r, out_ref, d1, sem).start(priority=1)
                aidx_ref[d0] = i * tm + r
                aidx_ref[d1] = n_tokens + i * tm + r
            return carry

        lax.fori_loop(0, tm // DMA_UNROLL, issue, 0)
        for _ in range(2):
            pltpu.make_async_copy(src_ref, out_ref.at[pl.ds(0, tm)], sem).wait()

    @pl.when(i < tiles_p)
    def _():
        run(xp_ref)

    @pl.when(i >= tiles_p)
    def _():
        run(xs_ref)


def _scatter(dest, xn_p, xn_s):
    tm = ROW_TILE
    tiles_p, tiles_s = xn_p.shape[0] // tm, xn_s.shape[0] // tm
    rows = 2 * (xn_p.shape[0] + xn_s.shape[0])
    idx = pl.BlockSpec((tm,), lambda i: (i,), memory_space=pltpu.SMEM)
    return pl.pallas_call(
        functools.partial(_scatter_kernel, tiles_p=tiles_p, n_tokens=rows // 2),
        grid=(tiles_p + tiles_s,),
        in_specs=[idx, idx,
                  pl.BlockSpec((tm, PACKED_W), lambda i: (jnp.minimum(i, tiles_p - 1), 0)),
                  pl.BlockSpec((tm, PACKED_W), lambda i: (jnp.maximum(i - tiles_p, 0), 0))],
        out_specs=[pl.BlockSpec(memory_space=pl.ANY), pl.BlockSpec(memory_space=pltpu.SMEM)],
        out_shape=[jax.ShapeDtypeStruct((rows, PACKED_W), U32), jax.ShapeDtypeStruct((rows,), I32)],
        scratch_shapes=[pltpu.SemaphoreType.DMA(())],
        compiler_params=pltpu.CompilerParams(dimension_semantics=("arbitrary",)),
        name="scatter_rows",
    )(dest[0], dest[1], xn_p, xn_s)


def _experts_kernel(blk_ref, exp_ref, lo_ref, hi_ref, first_ref, fresh_ref, pblk_ref,
                    aprev_ref, x_ref, wg_ref, wu_ref, wd_ref, ya_ref, wg_l, wu_l, wd_l, ybuf, sem,
                    *, n_items, n_blocks):
    j = pl.program_id(0)
    lo, hi = lo_ref[j], hi_ref[j]
    slot = lax.rem(blk_ref[j], 2)
    quarter = MOE_BLOCK // 4

    def wait_rows(s):
        pltpu.make_async_copy(ybuf.at[s], ya_ref.at[pl.ds(0, MOE_BLOCK)], sem.at[s]).wait()

    def send_rows(s, group):
        for row in range(group * quarter, (group + 1) * quarter):
            _row_copy(ybuf.at[s], row, ya_ref, aprev_ref[0, 0, row], sem.at[s]).start(priority=row % 2)

    @pl.when(fresh_ref[j] == 1)
    def _():
        wg_l[...] = wg_ref[0].astype(BF16)
        wu_l[...] = wu_ref[0].astype(BF16)
        wd_l[...] = wd_ref[0].astype(BF16)

    @pl.when((first_ref[j] == 1) & (blk_ref[j] >= 2))
    def _():
        wait_rows(slot)

    def item(is_first, send_prev):
        send = (lambda group: send_rows(1 - slot, group)) if send_prev else (lambda group: None)
        send(0)
        x_a, x_b = _unpack_halves(x_ref[...])
        gate = _dot(x_a, wg_l[0:PACKED_W, :]) + _dot(x_b, wg_l[PACKED_W:D_MODEL, :])
        send(1)
        up = _dot(x_a, wu_l[0:PACKED_W, :]) + _dot(x_b, wu_l[PACKED_W:D_MODEL, :])
        hid = (gate * jax.nn.sigmoid(gate)) * up
        send(2)
        y = _dot(hid.astype(BF16), wd_l[...])
        send(3)
        r = lax.broadcasted_iota(I32, (MOE_BLOCK, 1), 0)
        mine = (r >= lo) & (r < hi)
        ybuf[slot] = jnp.where(mine, y, 0.0 if is_first else ybuf[slot])

    live = hi > lo
    pl.when(live & (first_ref[j] == 1) & (blk_ref[j] >= 1))(functools.partial(item, True, True))
    pl.when(live & (first_ref[j] == 1) & (blk_ref[j] == 0))(functools.partial(item, True, False))
    pl.when(live & (first_ref[j] == 0))(functools.partial(item, False, False))

    @pl.when(j == n_items - 1)
    def _():
        last_slot = (n_blocks - 1) % 2
        for group in range(4):
            send_rows(last_slot, group)
        wait_rows(1 - last_slot)
        wait_rows(last_slot)


def _experts(items, aidx, xs, w_gate, w_up, w_down):
    n_items = items[0].shape[0]
    n_blocks = xs.shape[0] // MOE_BLOCK
    assert n_items > n_blocks + N_EXPERTS - 1 and n_blocks >= 2
    xblk = pl.BlockSpec((MOE_BLOCK, PACKED_W), lambda j, blk, *_: (blk[j], 0))
    wspec = lambda a: pl.BlockSpec((1,) + a.shape[1:], lambda j, blk, ex, *_: (ex[j], 0, 0))
    return pl.pallas_call(
        functools.partial(_experts_kernel, n_items=n_items, n_blocks=n_blocks),
        grid_spec=pltpu.PrefetchScalarGridSpec(
            num_scalar_prefetch=len(items),
            grid=(n_items,),
            in_specs=[pl.BlockSpec((1, 1, MOE_BLOCK), lambda j, *pre: (pre[-1][j], 0, 0), memory_space=pltpu.SMEM),
                      xblk, wspec(w_gate), wspec(w_up), wspec(w_down)],
            out_specs=pl.BlockSpec(memory_space=pl.ANY),
            scratch_shapes=[pltpu.VMEM(w_gate.shape[1:], BF16), pltpu.VMEM(w_up.shape[1:], BF16),
                            pltpu.VMEM(w_down.shape[1:], BF16), pltpu.VMEM((2, MOE_BLOCK, D_MODEL), F32),
                            pltpu.SemaphoreType.DMA((2,))]),
        out_shape=jax.ShapeDtypeStruct((xs.shape[0], D_MODEL), F32),
        compiler_params=pltpu.CompilerParams(dimension_semantics=("arbitrary",)),
        name="experts",
    )(*items, aidx.reshape(n_blocks, 1, MOE_BLOCK), xs, w_gate, w_up, w_down)


def _work_items(counts, total_rows):
    n_blocks = total_rows // MOE_BLOCK
    n_items = n_blocks + N_EXPERTS
    end = jnp.cumsum(counts)
    start = end - counts
    first_blk = start // MOE_BLOCK
    nb = jnp.where(counts > 0, (end - 1) // MOE_BLOCK - first_blk + 1, 0)
    item_end = jnp.cumsum(nb)
    used = item_end[-1]
    j = jnp.arange(n_items, dtype=I32)
    jj = jnp.minimum(j, used - 1)
    e = jnp.minimum(jnp.sum((item_end[None, :] <= jj[:, None]).astype(I32), axis=1), N_EXPERTS - 1)
    onehot = (e[:, None] == jnp.arange(N_EXPERTS, dtype=I32)[None, :]).astype(I32)
    pick = lambda a: jnp.sum(onehot * a[None, :], axis=1)
    blk = pick(first_blk) + (jj - (pick(item_end) - pick(nb)))
    lo = jnp.maximum(pick(start), blk * MOE_BLOCK) - blk * MOE_BLOCK
    hi = jnp.minimum(pick(end), (blk + 1) * MOE_BLOCK) - blk * MOE_BLOCK
    live = j < used
    lo = jnp.where(live, lo, 0)
    hi = jnp.where(live, hi, 0)
    prev_blk = jnp.concatenate([jnp.full((1,), -1, I32), blk[:-1]])
    first = (live & (blk != prev_blk)).astype(I32)
    prev_e = jnp.concatenate([jnp.full((1,), -1, I32), e[:-1]])
    fresh = (live & (e != prev_e)).astype(I32)
    pblk = jnp.where(j == n_items - 1, n_blocks - 1, jnp.maximum(blk - 1, 0))
    return (blk.astype(I32), e.astype(I32), lo.astype(I32), hi.astype(I32), first, fresh, pblk.astype(I32)), start


def _combine_kernel(gates_ref, h_ref, gf_ref, y0_ref, y1_ref, out_ref):
    gates = gates_ref[...]
    y = h_ref[...] + (y0_ref[...] * gates[:, 0:1] + y1_ref[...] * gates[:, 1:2])
    out_ref[...] = _rms(y, gf_ref[...])


def _combine(gates, h, gain, ya, first_token):
    n = h.shape[0]
    tm = COMBINE_TILE
    tile0 = first_token // tm
    slot_tiles = ya.shape[0] // 2 // tm
    row = lambda w: pl.BlockSpec((tm, w), lambda i: (i, 0))
    slot = lambda k: pl.BlockSpec((tm, D_MODEL), lambda i: (k * slot_tiles + tile0 + i, 0))
    return pl.pallas_call(
        _combine_kernel,
        grid=(n // tm,),
        in_specs=[row(2), row(D_MODEL), pl.BlockSpec(gain.shape, lambda i: (0, 0)), slot(0), slot(1)],
        out_specs=row(D_MODEL),
        out_shape=jax.ShapeDtypeStruct((n, D_MODEL), F32),
        compiler_params=pltpu.CompilerParams(dimension_semantics=("arbitrary",)),
        name="combine_norm",
    )(gates, h, gain, ya, ya)


def kernel(x_prompt, x_sample, cache_swa_k, cache_swa_v, state_gdn_conv, state_gdn, norm_mix, w_in, swa_sinks,
           gdn_conv_w, gdn_A_log, gdn_dt_bias, gdn_norm_w, w_out, norm_ffn, w_router_group, b_router_group,
           w_router_expert, b_router_expert, w_exp_gate, w_exp_up, w_exp_down, norm_final):
    depth = w_in.shape[0]
    assert depth == 1, "single trunk layer"
    bp, sp, _ = x_prompt.shape
    bs, ts, _ = x_sample.shape
    np_, ns = bp * sp, bs * ts
    l = 0

    w_in_b = w_in[l].astype(BF16)
    w_out_b = w_out[l].astype(BF16)
    g_mix = norm_mix[l].reshape(1, D_MODEL)
    g_ffn = norm_ffn[l].reshape(1, D_MODEL)
    g_fin = norm_final.reshape(1, D_MODEL)
    pad = LANES - N_EXPERTS - N_GROUPS
    w_router = jnp.concatenate([w_router_expert[l], w_router_group[l], jnp.zeros((D_MODEL, pad), F32)],
                               axis=1).astype(BF16).T
    b_router = jnp.concatenate([b_router_expert[l].reshape(-1), b_router_group[l],
                                jnp.zeros((pad,), F32)])[:, None]
    zeros8 = jnp.zeros((GDN_HEADS,), F32)
    alog16 = jnp.concatenate([gdn_A_log[l], zeros8])[None]
    dtb16 = jnp.concatenate([gdn_dt_bias[l], zeros8])[None]
    nw_group = jnp.tile(gdn_norm_w[l].reshape(1, HEAD_DIM), (1, GDN_GROUP))
    sinks = swa_sinks[l]

    q_p, k_p, v_p, c_p, z_p, ab_p = _inproj(x_prompt.reshape(np_, D_MODEL), g_mix, w_in_b)
    nblk = np_ // WINDOW
    k_p3 = k_p.reshape(nblk, WINDOW, SWA_KV)
    v_p3 = v_p.reshape(nblk, WINDOW, SWA_KV)
    (o_swa_p,) = _swa(sinks, q_p.reshape(nblk, WINDOW, SWA_Q), k_p3, v_p3, k_p3, v_p3,
                      bb=SWA_BLOCKS, blocks_per_seq=sp // WINDOW, emit_cache=False)
    c_p3 = c_p.reshape(bp, sp, GDN_CONV_CH)
    o_gdn_p, s_fin_p = _gdn(c_p3, jnp.zeros((bp, CONV_W - 1, GDN_CONV_CH), F32), z_p.reshape(bp, sp, GDN_V),
                            ab_p.reshape(bp, sp, 2 * GDN_HEADS),
                            jnp.zeros((bp, GDN_HEADS, HEAD_DIM, HEAD_DIM), F32),
                            gdn_conv_w[l], alog16, dtb16, nw_group, chunk=GDN_CHUNK, seq_block=bp)
    h_p, xn_p, gates_p, meta_p, cnt_p = _outproj(
        x_prompt.reshape(np_, D_MODEL), o_swa_p.reshape(np_, SWA_Q), o_gdn_p.reshape(np_, GDN_V),
        w_out_b, g_ffn, w_router, b_router)

    q_s, k_s, v_s, c_s, z_s, ab_s = _inproj(x_sample.reshape(ns, D_MODEL), g_mix, w_in_b)
    feature_major = lambda a: jnp.swapaxes(a.reshape(bs, WINDOW, SWA_KV), 1, 2)
    o_swa_s, kcache_t, vcache_t = _swa(
        sinks, q_s.reshape(bs, ts, SWA_Q), k_s.reshape(bs, ts, SWA_KV), v_s.reshape(bs, ts, SWA_KV),
        feature_major(cache_swa_k[l]), feature_major(cache_swa_v[l]),
        bb=16, blocks_per_seq=None, emit_cache=True)
    kcache_s, vcache_s = jnp.swapaxes(kcache_t, 1, 2), jnp.swapaxes(vcache_t, 1, 2)
    c_s3 = c_s.reshape(bs, ts, GDN_CONV_CH)
    o_gdn_s, s_fin_s = _gdn(c_s3, state_gdn_conv[l], z_s.reshape(bs, ts, GDN_V),
                            ab_s.reshape(bs, ts, 2 * GDN_HEADS), state_gdn[l],
                            gdn_conv_w[l], alog16, dtb16, nw_group, chunk=ts, seq_block=16)
    h_s, xn_s, gates_s, meta_s, cnt_s = _outproj(
        x_sample.reshape(ns, D_MODEL), o_swa_s.reshape(ns, SWA_Q), o_gdn_s.reshape(ns, GDN_V),
        w_out_b, g_ffn, w_router, b_router)

    cnt_p_i = cnt_p[:N_EXPERTS, 0].astype(I32)
    cnt_s_i = cnt_s[:N_EXPERTS, 0].astype(I32)
    items, start = _work_items(cnt_p_i + cnt_s_i, 2 * (np_ + ns))
    expert_ids = jnp.arange(N_EXPERTS, dtype=I32)
    lookup = lambda table, ids: jnp.sum(jnp.where(ids[..., None] == expert_ids, table, 0), axis=-1)
    dest_p = lookup(start, meta_p[0:2].astype(I32)) + meta_p[2:4].astype(I32)
    dest_s = lookup(start + cnt_p_i, meta_s[0:2].astype(I32)) + meta_s[2:4].astype(I32)
    xs, aidx = _scatter(jnp.concatenate([dest_p, dest_s], axis=1), xn_p, xn_s)
    ya = _experts(items, aidx, xs, w_exp_gate[l], w_exp_up[l], w_exp_down[l])
    y_p = _combine(gates_p, h_p, g_fin, ya, 0)
    y_s = _combine(gates_s, h_s, g_fin, ya, np_)

    kv5 = lambda a, b: a.reshape(b, -1, SWA_KV_HEADS, HEAD_DIM)[None]
    return (y_p.reshape(bp, sp, D_MODEL), y_s.reshape(bs, ts, D_MODEL),
            kv5(k_p.reshape(bp, sp, SWA_KV)[:, -WINDOW:], bp), kv5(v_p.reshape(bp, sp, SWA_KV)[:, -WINDOW:], bp),
            kv5(kcache_s, bs), kv5(vcache_s, bs),
            c_p3[:, -(CONV_W - 1):][None], c_s3[:, -(CONV_W - 1):][None],
            s_fin_p[None], s_fin_s[None])
```

```python
import functools

import jax
import jax.numpy as jnp
from jax import lax
from jax.experimental import pallas as pl
from jax.experimental.pallas import tpu as pltpu

F32 = jnp.float32
BF16 = jnp.bfloat16
I32 = jnp.int32
U32 = jnp.uint32

D_MODEL = 1024
HEAD_DIM = 64
SWA_HEADS = 8
GDN_HEADS = 8
SWA_KV_HEADS = 2
GQA_GROUP = SWA_HEADS // SWA_KV_HEADS
WINDOW = 128
ATTN_SCALE = HEAD_DIM ** -0.5
CONV_W = 4
N_GROUPS = 8
EXPERTS_PER_GROUP = 8
N_EXPERTS = 64
D_EXPERT = 256
EPS = 1e-6

SWA_Q = SWA_HEADS * HEAD_DIM
SWA_KV = SWA_KV_HEADS * HEAD_DIM
GDN_QK = GDN_HEADS * HEAD_DIM
GDN_V = GDN_HEADS * HEAD_DIM
GDN_CONV_CH = 2 * GDN_QK + GDN_V
D_MIX = SWA_Q + GDN_V
D_IN = SWA_Q + 2 * SWA_KV + GDN_CONV_CH + GDN_V + 2 * GDN_HEADS
COL_K = SWA_Q
COL_V = COL_K + SWA_KV
COL_C = COL_V + SWA_KV
COL_Z = COL_C + GDN_CONV_CH
COL_AB = COL_Z + GDN_V
PACKED_W = D_MODEL // 2

LANES = 128
NEG_BIG = -1e30
ROW_TILE = 512
INPROJ_TILE = 1024
MOE_BLOCK = 256
COMBINE_TILE = 512
DMA_UNROLL = 8
SWA_BLOCKS = 8
GDN_CHUNK = 64
GDN_TILE = 256
GDN_SCAN_CHUNKS = 8
GDN_GROUP = 4
GDN_GROUP_W = GDN_GROUP * HEAD_DIM


def _rms(x, g):
    return x * lax.rsqrt(jnp.mean(x * x, axis=-1, keepdims=True) + EPS) * g


def _dot(a, b):
    return jnp.dot(a, b, preferred_element_type=F32)


def _dot_nt(a, b):
    return lax.dot_general(a, b, (((1,), (1,)), ((), ())), preferred_element_type=F32)


def _dot_tn(a, b):
    return lax.dot_general(a, b, (((0,), (0,)), ((), ())), preferred_element_type=F32)


def _pack_halves(x):
    w = x.shape[1] // 2
    bits = lambda v: lax.bitcast_convert_type(v.astype(BF16).astype(F32), U32)
    return bits(x[:, :w]) | (bits(x[:, w:]) >> 16)


def _unpack_halves(p):
    hi = lax.bitcast_convert_type(p & jnp.uint32(0xFFFF0000), F32)
    lo = lax.bitcast_convert_type(p << 16, F32)
    return hi.astype(BF16), lo.astype(BF16)


def _split3(x):
    p1 = x.astype(BF16).astype(F32)
    r = x - p1
    p2 = r.astype(BF16).astype(F32)
    p3 = (r - p2).astype(BF16).astype(F32)
    return p1, p2, p3


def _inproj_kernel(x_ref, g_ref, w_ref, q_ref, k_ref, v_ref, c_ref, z_ref, ab_ref):
    x = x_ref[...]
    xb = _rms(x, g_ref[...]).astype(BF16)
    q_ref[...] = _dot(xb, w_ref[:, 0:COL_K])
    k_ref[...] = _dot(xb, w_ref[:, COL_K:COL_V])
    v_ref[...] = _dot(xb, w_ref[:, COL_V:COL_C])
    c_ref[...] = _dot(xb, w_ref[:, COL_C:COL_Z])
    z_ref[...] = _dot(xb, w_ref[:, COL_Z:COL_AB])
    ab_ref[...] = _dot(xb, w_ref[:, COL_AB:D_IN])


def _inproj(x2d, gain, w_bf16):
    n = x2d.shape[0]
    tm = INPROJ_TILE
    row = lambda w: pl.BlockSpec((tm, w), lambda i: (i, 0))
    full = lambda a: pl.BlockSpec(a.shape, lambda i: (0,) * a.ndim)
    widths = (SWA_Q, SWA_KV, SWA_KV, GDN_CONV_CH, GDN_V, 2 * GDN_HEADS)
    return pl.pallas_call(
        _inproj_kernel,
        grid=(n // tm,),
        in_specs=[row(D_MODEL), full(gain), full(w_bf16)],
        out_specs=[row(w) for w in widths],
        out_shape=[jax.ShapeDtypeStruct((n, w), F32) for w in widths],
        compiler_params=pltpu.CompilerParams(dimension_semantics=("arbitrary",)),
        name="inproj",
    )(x2d, gain, w_bf16)


def _swa_kernel(sink_ref, q_ref, kc_ref, vc_ref, kp_ref, vp_ref, o_ref, *cache_refs,
                bb, t, blocks_per_seq, emit_cache):
    rows = GQA_GROUP * t
    ri = lax.broadcasted_iota(I32, (rows, 1), 0)
    qi = lax.rem(ri, t)
    gi = ri // t
    if emit_cache:
        kj = lax.broadcasted_iota(I32, (rows, WINDOW), 1)
        mask = kj <= qi + (WINDOW - t)
        mask_first = mask
        place = (lax.broadcasted_iota(I32, (t, WINDOW), 1)
                 == lax.broadcasted_iota(I32, (t, WINDOW), 0) + (WINDOW - t)).astype(F32)
        is_new = lax.broadcasted_iota(I32, (SWA_KV, WINDOW), 1) >= WINDOW - t
    else:
        kj = lax.broadcasted_iota(I32, (rows, WINDOW + t), 1)
        mask = (kj <= qi + WINDOW) & (kj > qi)
        has_prev = lax.rem(pl.program_id(0) * bb, blocks_per_seq) != 0
        mask_first = mask & (has_prev | (kj >= WINDOW))
    sinks = []
    for h in range(SWA_KV_HEADS):
        sink = jnp.zeros((rows, 1), F32)
        for g in range(GQA_GROUP):
            sink = jnp.where(gi == g, sink_ref[GQA_GROUP * h + g], sink)
        sinks.append(sink)
    if emit_cache:
        def updated(old_t, new):
            placed = sum(_dot_tn(part, place) for part in _split3(new))
            return jnp.where(is_new, placed, pltpu.roll(old_t, WINDOW - t, axis=1))

        caches = [(updated(kp_ref[b], kc_ref[b]), updated(vp_ref[b], vc_ref[b])) for b in range(bb)]
        for b in range(bb):
            cache_refs[0][b] = caches[b][0]
            cache_refs[1][b] = caches[b][1]
    chains = [(b, h) for b in range(bb) for h in range(SWA_KV_HEADS)]
    scores, values = [], []
    for b, h in chains:
        hs = slice(h * HEAD_DIM, (h + 1) * HEAD_DIM)
        q4 = jnp.concatenate(
            [q_ref[b, :, (GQA_GROUP * h + g) * HEAD_DIM:(GQA_GROUP * h + g + 1) * HEAD_DIM]
             for g in range(GQA_GROUP)], axis=0)
        if emit_cache:
            keys_t, vals_t = caches[b][0][hs, :], caches[b][1][hs, :]
            scores.append(_dot(q4.astype(BF16), keys_t.astype(BF16)))
            values.append(vals_t.astype(BF16))
        else:
            kp, vp = (kp_ref[0, :, hs], vp_ref[0, :, hs]) if b == 0 else (kc_ref[b - 1, :, hs], vc_ref[b - 1, :, hs])
            keys = jnp.concatenate([kp, kc_ref[b, :, hs]], axis=0)
            vals = jnp.concatenate([vp, vc_ref[b, :, hs]], axis=0)
            scores.append(_dot_nt(q4.astype(BF16), keys.astype(BF16)))
            values.append(vals.astype(BF16))
    probs, dens = [], []
    for (b, h), s in zip(chains, scores):
        s = jnp.where(mask_first if b == 0 else mask, s * ATTN_SCALE, NEG_BIG)
        m = jnp.maximum(jnp.max(s, axis=-1, keepdims=True), sinks[h])
        p = jnp.exp(s - m)
        dens.append(jnp.sum(p, axis=-1, keepdims=True) + jnp.exp(sinks[h] - m))
        probs.append(p.astype(BF16))
    pv = _dot_nt if emit_cache else _dot
    outs = [pv(p, v) / den for p, v, den in zip(probs, values, dens)]
    for b in range(bb):
        o_ref[b] = jnp.concatenate([outs[b * SWA_KV_HEADS + h][g * t:(g + 1) * t]
                                    for h in range(SWA_KV_HEADS) for g in range(GQA_GROUP)], axis=-1)


def _swa(sinks, q3, k3, v3, kprev3, vprev3, *, bb, blocks_per_seq, emit_cache):
    nb, t, _ = q3.shape
    cur = lambda w: pl.BlockSpec((bb, t, w), lambda i: (i, 0, 0))
    if emit_cache:
        prev = pl.BlockSpec((bb, SWA_KV, WINDOW), lambda i: (i, 0, 0))
    else:
        assert blocks_per_seq % bb == 0
        prev = pl.BlockSpec((1, WINDOW, SWA_KV), lambda i: (jnp.maximum(i * bb - 1, 0), 0, 0))
    out_specs = [cur(SWA_Q)]
    out_shape = [jax.ShapeDtypeStruct((nb, t, SWA_Q), F32)]
    if emit_cache:
        out_specs += [prev, prev]
        out_shape += [jax.ShapeDtypeStruct((nb, SWA_KV, WINDOW), F32)] * 2
    return pl.pallas_call(
        functools.partial(_swa_kernel, bb=bb, t=t, blocks_per_seq=blocks_per_seq, emit_cache=emit_cache),
        grid=(nb // bb,),
        in_specs=[pl.BlockSpec(memory_space=pltpu.SMEM), cur(SWA_Q), cur(SWA_KV), cur(SWA_KV), prev, prev],
        out_specs=out_specs,
        out_shape=out_shape,
        compiler_params=pltpu.CompilerParams(dimension_semantics=("arbitrary",)),
        name="swa_cache" if emit_cache else "swa_band",
    )(sinks, q3, k3, v3, kprev3, vprev3)


def _gdn_prep_kernel(c_ref, hist_ref, ab_ref, cw_ref, alog_ref, dtb_ref,
                     u_ref, w_ref, qd_ref, kd_ref, qk_ref, gt_ref, cbuf_ref, *, chunk):
    sb, r, _ = c_ref.shape
    tp = sb * r
    cn = chunk
    low = w_ref.dtype
    hist_rows = CONV_W - 1

    @pl.when(pl.program_id(1) == 0)
    def _():
        cbuf_ref[:, 8 - hist_rows:8, :] = hist_ref[...]

    cbuf_ref[:, 8:8 + r, :] = c_ref[...]
    conv = cbuf_ref[:, 8 - hist_rows:8 - hist_rows + r, :] * cw_ref[0:1, :]
    for i in range(1, CONV_W):
        conv = conv + cbuf_ref[:, 8 - hist_rows + i:8 - hist_rows + i + r, :] * cw_ref[i:i + 1, :]
    tail = cbuf_ref[:, 8 + r - hist_rows:8 + r, :]
    cbuf_ref[:, 8 - hist_rows:8, :] = tail
    conv = (conv * jax.nn.sigmoid(conv)).reshape(tp, GDN_CONV_CH)

    ab = ab_ref[...].reshape(tp, 2 * GDN_HEADS)
    is_g = lax.broadcasted_iota(I32, (1, 2 * GDN_HEADS), 1) < GDN_HEADS
    g = jnp.where(is_g, -jnp.exp(alog_ref[...]) * jax.nn.softplus(ab + dtb_ref[...]), 0.0)
    beta = jax.nn.sigmoid(ab)

    ri = lax.broadcasted_iota(I32, (tp, tp), 0)
    ci = lax.broadcasted_iota(I32, (tp, tp), 1)
    same = (ri // cn) == (ci // cn)
    causal = same & (ri >= ci)
    strict = same & (ri > ci)
    eye = (ri == ci).astype(F32)
    stack = jnp.concatenate([causal.astype(BF16), same.astype(BF16)], axis=0)
    both = sum(_dot(stack, p.astype(BF16)) for p in _split3(g))
    gc, gl = both[:tp], both[tp:]
    e16 = lax.broadcasted_iota(I32, (2 * GDN_HEADS, 2 * GDN_HEADS), 0)
    eye16 = (e16 == lax.broadcasted_iota(I32, (2 * GDN_HEADS, 2 * GDN_HEADS), 1)).astype(BF16)
    gc_row = sum(_dot_nt(eye16, p.astype(BF16)) for p in _split3(gc))
    gt_ref[...] = jnp.exp(gl)
    fold = (lax.broadcasted_iota(I32, (tp, cn), 0) % cn == lax.broadcasted_iota(I32, (tp, cn), 1)).astype(low)

    levels = cn.bit_length() - 2
    heads = range(GDN_HEADS)
    lanes_of = lambda i, h: slice(i * GDN_QK + h * HEAD_DIM, i * GDN_QK + (h + 1) * HEAD_DIM)
    qs = [conv[:, lanes_of(0, h)] for h in heads]
    ks = [conv[:, lanes_of(1, h)] for h in heads]
    qs = [q * lax.rsqrt(jnp.sum(q * q, axis=-1, keepdims=True) + EPS) * (HEAD_DIM ** -0.5) for q in qs]
    ks = [k * lax.rsqrt(jnp.sum(k * k, axis=-1, keepdims=True) + EPS) for k in ks]
    g_cs = [gc[:, h:h + 1] for h in heads]
    b_hs = [beta[:, GDN_HEADS + h:GDN_HEADS + h + 1] for h in heads]
    decays = [jnp.exp(jnp.where(causal, g_cs[h] - gc_row[h:h + 1, :], NEG_BIG)) for h in heads]
    egs = [jnp.exp(g_c) for g_c in g_cs]
    kbs = [k * b_h for k, b_h in zip(ks, b_hs)]
    k_ls = [k.astype(low) for k in ks]
    lmats = [jnp.where(strict, _dot_nt(kb.astype(low), k_l) * decay, 0.0) for kb, k_l, decay in zip(kbs, k_ls, decays)]
    qks = [_dot((_dot_nt(q.astype(low), k_l) * decay).astype(low), fold) for q, k_l, decay in zip(qs, k_ls, decays)]
    xs = [eye - lmat for lmat in lmats]
    ps = [_dot(lmat.astype(low), lmat.astype(low)) for lmat in lmats]
    for lev in range(1, levels + 1):
        p_ls = [p.astype(low) for p in ps]
        if lev < levels:
            xp = [_dot(jnp.concatenate([x.astype(low), p_l], axis=0), p_l) for x, p_l in zip(xs, p_ls)]
            xs = [x + m[:tp] for x, m in zip(xs, xp)]
            ps = [m[tp:] for m in xp]
        else:
            xs = [x + _dot(x.astype(low), p_l) for x, p_l in zip(xs, p_ls)]
    sols = [_dot(xs[h].astype(low),
                 jnp.concatenate([conv[:, lanes_of(2, h)] * b_hs[h], kbs[h] * egs[h]], axis=-1).astype(low))
            for h in heads]
    u_ref[...] = jnp.concatenate([sol[:, :HEAD_DIM] for sol in sols], axis=-1)
    w_ref[...] = jnp.concatenate([sol[:, HEAD_DIM:] for sol in sols], axis=-1).astype(low)
    qd_ref[...] = jnp.concatenate([q * eg for q, eg in zip(qs, egs)], axis=-1).astype(low)
    kd_ref[...] = jnp.concatenate([ks[h] * jnp.exp(gl[:, h:h + 1] - g_cs[h]) for h in heads], axis=-1).astype(low)
    qk_ref[...] = jnp.concatenate(qks, axis=-1).astype(low)


def _gdn_prep_pair_kernel(c_ref, hist_ref, ab_ref, cw_ref, alog_ref, dtb_ref,
                          u_ref, w_ref, qd_ref, kd_ref, qk_ref, gt_ref, cbuf_ref):
    _, r, _ = c_ref.shape
    tp = r
    cn = HEAD_DIM
    pair_w = 2 * HEAD_DIM
    hist_rows = CONV_W - 1
    sublanes = 8

    @pl.when(pl.program_id(1) == 0)
    def _():
        cbuf_ref[...] = jnp.zeros_like(cbuf_ref)
        cbuf_ref[:, sublanes - hist_rows:sublanes, :] = hist_ref[...]

    x3 = c_ref[0].reshape(tp // sublanes, sublanes, GDN_CONV_CH)
    before = cbuf_ref[...]
    sub = lax.broadcasted_iota(I32, (1, sublanes, 1), 1)
    conv3 = x3 * cw_ref[CONV_W - 1:CONV_W, :]
    for s in range(1, CONV_W):
        rx = pltpu.roll(x3, s, axis=1)
        rp = jnp.concatenate([pltpu.roll(before, s, axis=1), rx[:-1]], axis=0)
        conv3 = conv3 + jnp.where(sub < s, rp, rx) * cw_ref[CONV_W - 1 - s:CONV_W - s, :]
    cbuf_ref[...] = x3[tp // sublanes - 1:]
    conv = (conv3 * jax.nn.sigmoid(conv3)).reshape(tp, GDN_CONV_CH)

    ab = ab_ref[...].reshape(tp, 2 * GDN_HEADS)
    is_g = lax.broadcasted_iota(I32, (1, 2 * GDN_HEADS), 1) < GDN_HEADS
    g = jnp.where(is_g, -jnp.exp(alog_ref[...]) * jax.nn.softplus(ab + dtb_ref[...]), 0.0)
    beta = jax.nn.sigmoid(ab)

    ri = lax.broadcasted_iota(I32, (tp, tp), 0)
    ci = lax.broadcasted_iota(I32, (tp, tp), 1)
    same = (ri // cn) == (ci // cn)
    same_l = same.astype(BF16)
    stack = jnp.concatenate([(same & (ri >= ci)).astype(BF16), same_l], axis=0)
    both = sum(_dot(stack, p.astype(BF16)) for p in _split3(g))
    gc, gl = both[:tp], both[tp:]
    gt_ref[...] = jnp.exp(gl)

    lane = lax.broadcasted_iota(I32, (tp, pair_w), 1)
    c_in = lax.broadcasted_iota(I32, (tp, pair_w), 0) % cn
    j_in = lane % cn
    left = lane < cn
    causal = c_in >= j_in
    strict = c_in > j_in
    diag = c_in == j_in
    eye = diag.astype(F32)
    bdmask = ((lax.broadcasted_iota(I32, (pair_w, pair_w), 0) // cn)
              == (lax.broadcasted_iota(I32, (pair_w, pair_w), 1) // cn))
    ones_bd = bdmask.astype(BF16)

    def bd(m):
        return jnp.where(bdmask, jnp.concatenate([m, m], axis=0), 0.0).astype(BF16)

    def head_sum(x):
        hi = x.astype(BF16)
        lo = (x - hi.astype(F32)).astype(BF16)
        return _dot(hi, ones_bd) + _dot(lo, ones_bd)

    levels = cn.bit_length() - 2
    chunks = [slice(n * cn, (n + 1) * cn) for n in range(tp // cn)]
    n_pairs = GDN_HEADS // 2
    pairs = range(n_pairs)
    pick = lambda m, off, p: jnp.where(left, m[:, off + 2 * p:off + 2 * p + 1], m[:, off + 2 * p + 1:off + 2 * p + 2])
    third = lambda i, p: conv[:, i * GDN_QK + p * pair_w:i * GDN_QK + (p + 1) * pair_w]
    qs = [third(0, p) for p in pairs]
    ks = [third(1, p) for p in pairs]
    qs = [q * lax.rsqrt(head_sum(q * q) + EPS) * (HEAD_DIM ** -0.5) for q in qs]
    ks = [k * lax.rsqrt(head_sum(k * k) + EPS) for k in ks]
    gcps = [pick(gc, 0, p) for p in pairs]
    rowms = [sum(_dot(same_l, part.astype(BF16)) for part in _split3(jnp.where(diag, gcp, 0.0))) for gcp in gcps]
    decays = [jnp.exp(jnp.where(causal, gcp - rowm, NEG_BIG)) for gcp, rowm in zip(gcps, rowms)]
    lmats, vbs, kbegs = [], [], []
    for p in pairs:
        ls = slice(p * pair_w, (p + 1) * pair_w)
        q, k, gcp, decay = qs[p], ks[p], gcps[p], decays[p]
        bp = pick(beta, GDN_HEADS, p)
        eg = jnp.exp(gcp)
        kb = k * bp
        vbs.append(third(2, p) * bp)
        kbegs.append(kb * eg)
        qd_ref[:, ls] = (q * eg).astype(BF16)
        kd_ref[:, ls] = (k * jnp.exp(pick(gl, 0, p) - gcp)).astype(BF16)
        q_l, kb_l = q.astype(BF16), kb.astype(BF16)
        kbd = [bd(k[rs]) for rs in chunks]
        kk = jnp.concatenate([_dot_nt(kb_l[rs], kbd[n]) for n, rs in enumerate(chunks)], axis=0)
        qk = jnp.concatenate([_dot_nt(q_l[rs], kbd[n]) for n, rs in enumerate(chunks)], axis=0)
        qk_ref[:, ls] = (qk * decay).astype(BF16)
        lmats.append(jnp.where(strict, kk * decay, 0.0))
    bodies = [(p, rs) for p in range(n_pairs) for rs in chunks]
    xs = [eye[rs] - lmats[p][rs] for p, rs in bodies]
    ps = [_dot(lmats[p][rs].astype(BF16), bd(lmats[p][rs])) for p, rs in bodies]
    for lev in range(1, levels + 1):
        pbd = [bd(pm) for pm in ps]
        if lev < levels:
            xp = [_dot(jnp.concatenate([xm, pm], axis=0).astype(BF16), wm) for xm, pm, wm in zip(xs, ps, pbd)]
            xs = [xm + m[:cn] for xm, m in zip(xs, xp)]
            ps = [m[cn:] for m in xp]
        else:
            xs = [xm + _dot(xm.astype(BF16), wm) for xm, wm in zip(xs, pbd)]
    x_l = [xm.astype(BF16) for xm in xs]
    us = [_dot(xm, bd(vbs[p][rs])) for xm, (p, rs) in zip(x_l, bodies)]
    ws = [_dot(xm, bd(kbegs[p][rs])) for xm, (p, rs) in zip(x_l, bodies)]
    nc = len(chunks)
    for p in range(n_pairs):
        ls = slice(p * pair_w, (p + 1) * pair_w)
        u_ref[:, ls] = jnp.concatenate(us[p * nc:(p + 1) * nc], axis=0)
        w_ref[:, ls] = jnp.concatenate(ws[p * nc:(p + 1) * nc], axis=0).astype(BF16)


def _gdn_scan_kernel(u_ref, w_ref, qd_ref, kd_ref, qk_ref, gt_ref, z_ref, s0_ref, nw_ref,
                     o_ref, sfin_ref, sbd_ref, *, chunk, n_chunks):
    bb = u_ref.shape[0]
    cn = chunk
    low = w_ref.dtype
    gw = GDN_GROUP_W
    ni = pl.program_id(1)

    @pl.when(ni == 0)
    def _():
        sbd_ref[...] = jnp.zeros_like(sbd_ref)
        for b in range(bb):
            for h in range(GDN_HEADS):
                gi, hh = divmod(h, GDN_GROUP)
                ds = slice(hh * HEAD_DIM, (hh + 1) * HEAD_DIM)
                sbd_ref[b, gi, ds, ds] = s0_ref[b, h]

    bdmask = ((lax.broadcasted_iota(I32, (gw, gw), 0) // HEAD_DIM)
              == (lax.broadcasted_iota(I32, (gw, gw), 1) // HEAD_DIM))
    ones_bd = bdmask.astype(BF16)
    vmask = ((lax.broadcasted_iota(I32, (GDN_GROUP * cn, gw), 0) // cn)
             == (lax.broadcasted_iota(I32, (GDN_GROUP * cn, gw), 1) // HEAD_DIM))
    e_row = lax.broadcasted_iota(I32, (2 * GDN_HEADS, gw), 0)
    e_col = lax.broadcasted_iota(I32, (2 * GDN_HEADS, gw), 1) // HEAD_DIM
    chains = [(b, gi) for b in range(bb) for gi in range(GDN_HEADS // GDN_GROUP)]
    lanes = lambda gi: slice(gi * gw, (gi + 1) * gw)
    for c in range(u_ref.shape[1] // cn):
        rs = slice(c * cn, (c + 1) * cn)
        states = [sbd_ref[b, gi] for b, gi in chains]
        states_l = [s.astype(low) for s in states]
        wq_s = [_dot(jnp.concatenate([w_ref[b, rs, lanes(gi)], qd_ref[b, rs, lanes(gi)]], axis=0), s_l)
                for (b, gi), s_l in zip(chains, states_l)]
        v_new = [u_ref[b, rs, lanes(gi)] - m[:cn] for (b, gi), m in zip(chains, wq_s)]
        q_s = [m[cn:] for m in wq_s]
        v_l = [v.astype(low) for v in v_new]
        outs = []
        for (b, gi), s, v, qs in zip(chains, states, v_l, q_s):
            vbd = jnp.where(vmask, jnp.concatenate([v] * GDN_GROUP, axis=0), jnp.zeros((), low))
            outs.append(qs + _dot(qk_ref[b, rs, gi * GDN_GROUP * cn:(gi + 1) * GDN_GROUP * cn], vbd))
            upd = _dot_tn(kd_ref[b, rs, lanes(gi)], v)
            expand = (e_row == e_col + gi * GDN_GROUP).astype(BF16)
            decay8 = gt_ref[b, c * cn:c * cn + 8, :]
            gte = sum(_dot(p.astype(BF16), expand) for p in _split3(decay8))[0:1]
            sbd_ref[b, gi] = s * gte + jnp.where(bdmask, upd, 0.0)
        for (b, gi), o in zip(chains, outs):
            o2 = o * o
            hi = o2.astype(BF16)
            lo = (o2 - hi.astype(F32)).astype(BF16)
            if cn % 16 == 0:
                sums = _dot(jnp.concatenate([hi, lo], axis=0), ones_bd)
                ms = (sums[:cn] + sums[cn:]) * (1.0 / HEAD_DIM)
            else:
                ms = (_dot(hi, ones_bd) + _dot(lo, ones_bd)) * (1.0 / HEAD_DIM)
            zg = z_ref[b, rs, lanes(gi)]
            o_ref[b, rs, lanes(gi)] = o * lax.rsqrt(ms + EPS) * nw_ref[...] * (zg * jax.nn.sigmoid(zg))

    @pl.when(ni == n_chunks - 1)
    def _():
        for b in range(bb):
            for h in range(GDN_HEADS):
                gi, hh = divmod(h, GDN_GROUP)
                ds = slice(hh * HEAD_DIM, (hh + 1) * HEAD_DIM)
                sfin_ref[b, h] = sbd_ref[b, gi, ds, ds]


def _gdn(c3, hist, z3, ab3, s0, conv_w, alog16, dtb16, nw_group, *, chunk, seq_block):
    nseq, t, _ = c3.shape
    n = nseq * t
    sb, r = (1, GDN_TILE) if t >= GDN_TILE else (GDN_TILE // t, t)
    tiles = t // r
    low = BF16 if chunk >= 16 else F32
    blk = lambda w: pl.BlockSpec((sb, r, w), lambda s, i: (s, i, 0))
    full = lambda a: pl.BlockSpec(a.shape, lambda s, i: (0,) * a.ndim)
    flat = lambda w: pl.BlockSpec((GDN_TILE, w), lambda s, i: (s * tiles + i, 0))
    widths = (GDN_V, GDN_V, GDN_QK, GDN_QK, GDN_HEADS * chunk, 2 * GDN_HEADS)
    dtypes = (F32, low, low, low, low, F32)
    prep_out = dict(out_specs=[flat(wd) for wd in widths],
                    out_shape=[jax.ShapeDtypeStruct((n, wd), dt) for wd, dt in zip(widths, dtypes)],
                    compiler_params=pltpu.CompilerParams(dimension_semantics=("arbitrary", "arbitrary")))
    lane_dense = chunk == HEAD_DIM and sb == 1
    u, w, qd, kd, qk, gt = pl.pallas_call(
        _gdn_prep_pair_kernel if lane_dense else functools.partial(_gdn_prep_kernel, chunk=chunk),
        grid=(nseq // sb, tiles),
        in_specs=[blk(GDN_CONV_CH), pl.BlockSpec((sb, CONV_W - 1, GDN_CONV_CH), lambda s, i: (s, 0, 0)),
                  blk(2 * GDN_HEADS), full(conv_w), full(alog16), full(dtb16)],
        scratch_shapes=[pltpu.VMEM((1, 8, GDN_CONV_CH) if lane_dense else (sb, 8 + r, GDN_CONV_CH), F32)],
        name="gdn_prep_pair" if lane_dense else "gdn_prep", **prep_out)(c3, hist, ab3, conv_w, alog16, dtb16)

    rows_per_step = min(t, GDN_SCAN_CHUNKS * chunk)
    n_chunks = t // rows_per_step
    tok = lambda wd: pl.BlockSpec((seq_block, rows_per_step, wd), lambda s, c: (s, c, 0))
    per_seq = pl.BlockSpec((seq_block,) + s0.shape[1:], lambda s, c: (s, 0, 0, 0))
    seq3 = lambda a: a.reshape(nseq, t, a.shape[-1])
    return pl.pallas_call(
        functools.partial(_gdn_scan_kernel, chunk=chunk, n_chunks=n_chunks),
        grid=(nseq // seq_block, n_chunks),
        in_specs=[tok(wd) for wd in widths] + [tok(GDN_V), per_seq,
                                               pl.BlockSpec(nw_group.shape, lambda s, c: (0, 0))],
        out_specs=[tok(GDN_V), per_seq],
        out_shape=[jax.ShapeDtypeStruct((nseq, t, GDN_V), F32), jax.ShapeDtypeStruct(s0.shape, F32)],
        scratch_shapes=[pltpu.VMEM((seq_block, GDN_HEADS // GDN_GROUP, GDN_GROUP_W, GDN_GROUP_W), F32)],
        compiler_params=pltpu.CompilerParams(dimension_semantics=("arbitrary", "arbitrary")),
        name="gdn_scan",
    )(seq3(u), seq3(w), seq3(qd), seq3(kd), seq3(qk), seq3(gt), z3, s0, nw_group)


def _outproj_kernel(x_ref, osw_ref, ogd_ref, wo_ref, gf_ref, wr_ref, br_ref,
                    h_ref, xn_ref, gates_ref, meta_ref, cnt_ref, run_ref):
    i = pl.program_id(0)
    tm = x_ref.shape[0]
    rows = wr_ref.shape[0]

    @pl.when(i == 0)
    def _():
        run_ref[...] = jnp.zeros_like(run_ref)

    h = (x_ref[...] + _dot(osw_ref[...].astype(BF16), wo_ref[0:SWA_Q, :])
         + _dot(ogd_ref[...].astype(BF16), wo_ref[SWA_Q:D_MIX, :]))
    h_ref[...] = h
    xn = _rms(h, gf_ref[...])
    xn_ref[...] = _pack_halves(xn)
    logits = _dot_nt(wr_ref[...], xn.astype(BF16))

    row = lax.broadcasted_iota(I32, (rows, tm), 0)
    bias = br_ref[...]
    top = lambda v: jnp.max(v, axis=0, keepdims=True)
    tot = lambda v: jnp.sum(v, axis=0, keepdims=True)
    first_at = lambda v: jnp.min(jnp.where(v == top(v), row, 2 * rows), axis=0, keepdims=True)
    is_g = (row >= N_EXPERTS) & (row < N_EXPERTS + N_GROUPS)
    lg = jnp.where(is_g, logits, NEG_BIG)
    pg = jnp.where(is_g, jnp.exp(lg - top(lg)), 0.0)
    group_p = pg / tot(pg)
    g_row = first_at(jnp.where(is_g, group_p + bias, NEG_BIG))
    g_w = tot(jnp.where(row == g_row, group_p, 0.0))
    sel = (row < N_EXPERTS) & ((row // EXPERTS_PER_GROUP) == (g_row - N_EXPERTS))
    le = jnp.where(sel, logits, NEG_BIG)
    pe = jnp.where(sel, jnp.exp(le - top(le)), 0.0)
    e_p = pe / tot(pe)
    score = jnp.where(sel, e_p + bias, NEG_BIG)
    i1 = first_at(score)
    i2 = first_at(jnp.where(row == i1, NEG_BIG, score))
    oh1 = row == i1
    oh2 = row == i2
    w1 = tot(jnp.where(oh1, e_p, 0.0))
    w2 = tot(jnp.where(oh2, e_p, 0.0))
    wsum = w1 + w2

    ohs = (oh1 | oh2).astype(BF16)
    earlier = (lax.broadcasted_iota(I32, (tm, tm), 0) < lax.broadcasted_iota(I32, (tm, tm), 1)).astype(BF16)
    before = _dot(ohs, earlier) + run_ref[...]
    r1 = tot(jnp.where(oh1, before, 0.0))
    r2 = tot(jnp.where(oh2, before, 0.0))
    run_ref[...] = run_ref[...] + jnp.sum(ohs.astype(F32), axis=1, keepdims=True)
    cnt_ref[...] = run_ref[...]

    zero = jnp.zeros_like(w1)
    meta = jnp.concatenate([i1.astype(F32), i2.astype(F32), r1, r2, g_w * (w1 / wsum), g_w * (w2 / wsum),
                            zero, zero], axis=0)
    meta_ref[...] = meta
    eye8 = (lax.broadcasted_iota(I32, (8, LANES), 0) == lax.broadcasted_iota(I32, (8, LANES), 1)).astype(F32)
    gates_ref[...] = sum(_dot_tn(part, eye8) for part in _split3(meta))[:, 4:6]


def _outproj(x2d, o_swa, o_gdn, wo_bf16, gain, w_router, b_router):
    n = x2d.shape[0]
    tm = ROW_TILE
    row = lambda w: pl.BlockSpec((tm, w), lambda i: (i, 0))
    full = lambda a: pl.BlockSpec(a.shape, lambda i: (0,) * a.ndim)
    return pl.pallas_call(
        _outproj_kernel,
        grid=(n // tm,),
        in_specs=[row(D_MODEL), row(SWA_Q), row(GDN_V), full(wo_bf16), full(gain), full(w_router), full(b_router)],
        out_specs=[row(D_MODEL), row(PACKED_W), row(2), pl.BlockSpec((8, tm), lambda i: (0, i)),
                   pl.BlockSpec((LANES, 1), lambda i: (0, 0))],
        out_shape=[jax.ShapeDtypeStruct((n, D_MODEL), F32), jax.ShapeDtypeStruct((n, PACKED_W), U32),
                   jax.ShapeDtypeStruct((n, 2), F32), jax.ShapeDtypeStruct((8, n), F32),
                   jax.ShapeDtypeStruct((LANES, 1), F32)],
        scratch_shapes=[pltpu.VMEM((LANES, 1), F32)],
        compiler_params=pltpu.CompilerParams(dimension_semantics=("arbitrary",)),
        name="outproj_router",
    )(x2d, o_swa, o_gdn, wo_bf16, gain, w_router, b_router)


def _row_copy(src_ref, src_row, dst_ref, dst_row, sem):
    return pltpu.make_async_copy(src_ref.at[pl.ds(src_row, 1)], dst_ref.at[pl.ds(dst_row, 1)], sem)


def _scatter_kernel(dest0_ref, dest1_ref, xp_ref, xs_ref, out_ref, aidx_ref, sem, *, tiles_p, n_tokens):
    i = pl.program_id(0)
    tm = xp_ref.shape[0]

    def run(src_ref):
        def issue(g, carry):
            for u in range(DMA_UNROLL):
                r = g * DMA_UNROLL + u
                d0, d1 = dest0_ref[r], dest1_ref[r]
                _row_copy(src_ref, r, out_ref, d0, sem).start(priority=0)
                _row_copy(src_ref, r, out_ref, d1, sem).start(priority=1)
                aidx_ref[d0] = i * tm + r
                aidx_ref[d1] = n_tokens + i * tm + r
            return carry

        lax.fori_loop(0, tm // DMA_UNROLL, issue, 0)
        for _ in range(2):
            pltpu.make_async_copy(src_ref, out_ref.at[pl.ds(0, tm)], sem).wait()

    @pl.when(i < tiles_p)
    def _():
        run(xp_ref)

    @pl.when(i >= tiles_p)
    def _():
        run(xs_ref)


def _scatter(dest, xn_p, xn_s):
    tm = ROW_TILE
    tiles_p, tiles_s = xn_p.shape[0] // tm, xn_s.shape[0] // tm
    rows = 2 * (xn_p.shape[0] + xn_s.shape[0])
    idx = pl.BlockSpec((tm,), lambda i: (i,), memory_space=pltpu.SMEM)
    return pl.pallas_call(
        functools.partial(_scatter_kernel, tiles_p=tiles_p, n_tokens=rows // 2),
        grid=(tiles_p + tiles_s,),
        in_specs=[idx, idx,
                  pl.BlockSpec((tm, PACKED_W), lambda i: (jnp.minimum(i, tiles_p - 1), 0)),
                  pl.BlockSpec((tm, PACKED_W), lambda i: (jnp.maximum(i - tiles_p, 0), 0))],
        out_specs=[pl.BlockSpec(memory_space=pl.ANY), pl.BlockSpec(memory_space=pltpu.SMEM)],
        out_shape=[jax.ShapeDtypeStruct((rows, PACKED_W), U32), jax.ShapeDtypeStruct((rows,), I32)],
        scratch_shapes=[pltpu.SemaphoreType.DMA(())],
        compiler_params=pltpu.CompilerParams(dimension_semantics=("arbitrary",)),
        name="scatter_rows",
    )(dest[0], dest[1], xn_p, xn_s)


def _experts_kernel(blk_ref, exp_ref, lo_ref, hi_ref, first_ref, fresh_ref, pblk_ref,
                    aprev_ref, x_ref, wg_ref, wu_ref, wd_ref, ya_ref, wg_l, wu_l, wd_l, ybuf, sem,
                    *, n_items, n_blocks):
    j = pl.program_id(0)
    lo, hi = lo_ref[j], hi_ref[j]
    slot = lax.rem(blk_ref[j], 2)
    quarter = MOE_BLOCK // 4

    def wait_rows(s):
        pltpu.make_async_copy(ybuf.at[s], ya_ref.at[pl.ds(0, MOE_BLOCK)], sem.at[s]).wait()

    def send_rows(s, group):
        for row in range(group * quarter, (group + 1) * quarter):
            _row_copy(ybuf.at[s], row, ya_ref, aprev_ref[0, 0, row], sem.at[s]).start(priority=row % 2)

    @pl.when(fresh_ref[j] == 1)
    def _():
        wg_l[...] = wg_ref[0].astype(BF16)
        wu_l[...] = wu_ref[0].astype(BF16)
        wd_l[...] = wd_ref[0].astype(BF16)

    @pl.when((first_ref[j] == 1) & (blk_ref[j] >= 2))
    def _():
        wait_rows(slot)

    def item(is_first, send_prev):
        send = (lambda group: send_rows(1 - slot, group)) if send_prev else (lambda group: None)
        send(0)
        x_a, x_b = _unpack_halves(x_ref[...])
        gate = _dot(x_a, wg_l[0:PACKED_W, :]) + _dot(x_b, wg_l[PACKED_W:D_MODEL, :])
        send(1)
        up = _dot(x_a, wu_l[0:PACKED_W, :]) + _dot(x_b, wu_l[PACKED_W:D_MODEL, :])
        hid = (gate * jax.nn.sigmoid(gate)) * up
        send(2)
        y = _dot(hid.astype(BF16), wd_l[...])
        send(3)
        r = lax.broadcasted_iota(I32, (MOE_BLOCK, 1), 0)
        mine = (r >= lo) & (r < hi)
        ybuf[slot] = jnp.where(mine, y, 0.0 if is_first else ybuf[slot])

    live = hi > lo
    pl.when(live & (first_ref[j] == 1) & (blk_ref[j] >= 1))(functools.partial(item, True, True))
    pl.when(live & (first_ref[j] == 1) & (blk_ref[j] == 0))(functools.partial(item, True, False))
    pl.when(live & (first_ref[j] == 0))(functools.partial(item, False, False))

    @pl.when(j == n_items - 1)
    def _():
        last_slot = (n_blocks - 1) % 2
        for group in range(4):
            send_rows(last_slot, group)
        wait_rows(1 - last_slot)
        wait_rows(last_slot)


def _experts(items, aidx, xs, w_gate, w_up, w_down):
    n_items = items[0].shape[0]
    n_blocks = xs.shape[0] // MOE_BLOCK
    assert n_items > n_blocks + N_EXPERTS - 1 and n_blocks >= 2
    xblk = pl.BlockSpec((MOE_BLOCK, PACKED_W), lambda j, blk, *_: (blk[j], 0))
    wspec = lambda a: pl.BlockSpec((1,) + a.shape[1:], lambda j, blk, ex, *_: (ex[j], 0, 0))
    return pl.pallas_call(
        functools.partial(_experts_kernel, n_items=n_items, n_blocks=n_blocks),
        grid_spec=pltpu.PrefetchScalarGridSpec(
            num_scalar_prefetch=len(items),
            grid=(n_items,),
            in_specs=[pl.BlockSpec((1, 1, MOE_BLOCK), lambda j, *pre: (pre[-1][j], 0, 0), memory_space=pltpu.SMEM),
                      xblk, wspec(w_gate), wspec(w_up), wspec(w_down)],
            out_specs=pl.BlockSpec(memory_space=pl.ANY),
            scratch_shapes=[pltpu.VMEM(w_gate.shape[1:], BF16), pltpu.VMEM(w_up.shape[1:], BF16),
                            pltpu.VMEM(w_down.shape[1:], BF16), pltpu.VMEM((2, MOE_BLOCK, D_MODEL), F32),
                            pltpu.SemaphoreType.DMA((2,))]),
        out_shape=jax.ShapeDtypeStruct((xs.shape[0], D_MODEL), F32),
        compiler_params=pltpu.CompilerParams(dimension_semantics=("arbitrary",)),
        name="experts",
    )(*items, aidx.reshape(n_blocks, 1, MOE_BLOCK), xs, w_gate, w_up, w_down)


def _work_items(counts, total_rows):
    n_blocks = total_rows // MOE_BLOCK
    n_items = n_blocks + N_EXPERTS
    end = jnp.cumsum(counts)
    start = end - counts
    first_blk = start // MOE_BLOCK
    nb = jnp.where(counts > 0, (end - 1) // MOE_BLOCK - first_blk + 1, 0)
    item_end = jnp.cumsum(nb)
    used = item_end[-1]
    j = jnp.arange(n_items, dtype=I32)
    jj = jnp.minimum(j, used - 1)
    e = jnp.minimum(jnp.sum((item_end[None, :] <= jj[:, None]).astype(I32), axis=1), N_EXPERTS - 1)
    onehot = (e[:, None] == jnp.arange(N_EXPERTS, dtype=I32)[None, :]).astype(I32)
    pick = lambda a: jnp.sum(onehot * a[None, :], axis=1)
    blk = pick(first_blk) + (jj - (pick(item_end) - pick(nb)))
    lo = jnp.maximum(pick(start), blk * MOE_BLOCK) - blk * MOE_BLOCK
    hi = jnp.minimum(pick(end), (blk + 1) * MOE_BLOCK) - blk * MOE_BLOCK
    live = j < used
    lo = jnp.where(live, lo, 0)
    hi = jnp.where(live, hi, 0)
    prev_blk = jnp.concatenate([jnp.full((1,), -1, I32), blk[:-1]])
    first = (live & (blk != prev_blk)).astype(I32)
    prev_e = jnp.concatenate([jnp.full((1,), -1, I32), e[:-1]])
    fresh = (live & (e != prev_e)).astype(I32)
    pblk = jnp.where(j == n_items - 1, n_blocks - 1, jnp.maximum(blk - 1, 0))
    return (blk.astype(I32), e.astype(I32), lo.astype(I32), hi.astype(I32), first, fresh, pblk.astype(I32)), start


def _combine_kernel(gates_ref, h_ref, gf_ref, y0_ref, y1_ref, out_ref):
    gates = gates_ref[...]
    y = h_ref[...] + (y0_ref[...] * gates[:, 0:1] + y1_ref[...] * gates[:, 1:2])
    out_ref[...] = _rms(y, gf_ref[...])


def _combine(gates, h, gain, ya, first_token):
    n = h.shape[0]
    tm = COMBINE_TILE
    tile0 = first_token // tm
    slot_tiles = ya.shape[0] // 2 // tm
    row = lambda w: pl.BlockSpec((tm, w), lambda i: (i, 0))
    slot = lambda k: pl.BlockSpec((tm, D_MODEL), lambda i: (k * slot_tiles + tile0 + i, 0))
    return pl.pallas_call(
        _combine_kernel,
        grid=(n // tm,),
        in_specs=[row(2), row(D_MODEL), pl.BlockSpec(gain.shape, lambda i: (0, 0)), slot(0), slot(1)],
        out_specs=row(D_MODEL),
        out_shape=jax.ShapeDtypeStruct((n, D_MODEL), F32),
        compiler_params=pltpu.CompilerParams(dimension_semantics=("arbitrary",)),
        name="combine_norm",
    )(gates, h, gain, ya, ya)


def kernel(x_prompt, x_sample, cache_swa_k, cache_swa_v, state_gdn_conv, state_gdn, norm_mix, w_in, swa_sinks,
           gdn_conv_w, gdn_A_log, gdn_dt_bias, gdn_norm_w, w_out, norm_ffn, w_router_group, b_router_group,
           w_router_expert, b_router_expert, w_exp_gate, w_exp_up, w_exp_down, norm_final):
    depth = w_in.shape[0]
    assert depth == 1, "single trunk layer"
    bp, sp, _ = x_prompt.shape
    bs, ts, _ = x_sample.shape
    np_, ns = bp * sp, bs * ts
    l = 0

    w_in_b = w_in[l].astype(BF16)
    w_out_b = w_out[l].astype(BF16)
    g_mix = norm_mix[l].reshape(1, D_MODEL)
    g_ffn = norm_ffn[l].reshape(1, D_MODEL)
    g_fin = norm_final.reshape(1, D_MODEL)
    pad = LANES - N_EXPERTS - N_GROUPS
    w_router = jnp.concatenate([w_router_expert[l], w_router_group[l], jnp.zeros((D_MODEL, pad), F32)],
                               axis=1).astype(BF16).T
    b_router = jnp.concatenate([b_router_expert[l].reshape(-1), b_router_group[l],
                                jnp.zeros((pad,), F32)])[:, None]
    zeros8 = jnp.zeros((GDN_HEADS,), F32)
    alog16 = jnp.concatenate([gdn_A_log[l], zeros8])[None]
    dtb16 = jnp.concatenate([gdn_dt_bias[l], zeros8])[None]
    nw_group = jnp.tile(gdn_norm_w[l].reshape(1, HEAD_DIM), (1, GDN_GROUP))
    sinks = swa_sinks[l]

    q_p, k_p, v_p, c_p, z_p, ab_p = _inproj(x_prompt.reshape(np_, D_MODEL), g_mix, w_in_b)
    nblk = np_ // WINDOW
    k_p3 = k_p.reshape(nblk, WINDOW, SWA_KV)
    v_p3 = v_p.reshape(nblk, WINDOW, SWA_KV)
    (o_swa_p,) = _swa(sinks, q_p.reshape(nblk, WINDOW, SWA_Q), k_p3, v_p3, k_p3, v_p3,
                      bb=SWA_BLOCKS, blocks_per_seq=sp // WINDOW, emit_cache=False)
    c_p3 = c_p.reshape(bp, sp, GDN_CONV_CH)
    o_gdn_p, s_fin_p = _gdn(c_p3, jnp.zeros((bp, CONV_W - 1, GDN_CONV_CH), F32), z_p.reshape(bp, sp, GDN_V),
                            ab_p.reshape(bp, sp, 2 * GDN_HEADS),
                            jnp.zeros((bp, GDN_HEADS, HEAD_DIM, HEAD_DIM), F32),
                            gdn_conv_w[l], alog16, dtb16, nw_group, chunk=GDN_CHUNK, seq_block=bp)
    h_p, xn_p, gates_p, meta_p, cnt_p = _outproj(
        x_prompt.reshape(np_, D_MODEL), o_swa_p.reshape(np_, SWA_Q), o_gdn_p.reshape(np_, GDN_V),
        w_out_b, g_ffn, w_router, b_router)

    q_s, k_s, v_s, c_s, z_s, ab_s = _inproj(x_sample.reshape(ns, D_MODEL), g_mix, w_in_b)
    feature_major = lambda a: jnp.swapaxes(a.reshape(bs, WINDOW, SWA_KV), 1, 2)
    o_swa_s, kcache_t, vcache_t = _swa(
        sinks, q_s.reshape(bs, ts, SWA_Q), k_s.reshape(bs, ts, SWA_KV), v_s.reshape(bs, ts, SWA_KV),
        feature_major(cache_swa_k[l]), feature_major(cache_swa_v[l]),
        bb=16, blocks_per_seq=None, emit_cache=True)
    kcache_s, vcache_s = jnp.swapaxes(kcache_t, 1, 2), jnp.swapaxes(vcache_t, 1, 2)
    c_s3 = c_s.reshape(bs, ts, GDN_CONV_CH)
    o_gdn_s, s_fin_s = _gdn(c_s3, state_gdn_conv[l], z_s.reshape(bs, ts, GDN_V),
                            ab_s.reshape(bs, ts, 2 * GDN_HEADS), state_gdn[l],
                            gdn_conv_w[l], alog16, dtb16, nw_group, chunk=ts, seq_block=16)
    h_s, xn_s, gates_s, meta_s, cnt_s = _outproj(
        x_sample.reshape(ns, D_MODEL), o_swa_s.reshape(ns, SWA_Q), o_gdn_s.reshape(ns, GDN_V),
        w_out_b, g_ffn, w_router, b_router)

    cnt_p_i = cnt_p[:N_EXPERTS, 0].astype(I32)
    cnt_s_i = cnt_s[:N_EXPERTS, 0].astype(I32)
    items, start = _work_items(cnt_p_i + cnt_s_i, 2 * (np_ + ns))
    expert_ids = jnp.arange(N_EXPERTS, dtype=I32)
    lookup = lambda table, ids: jnp.sum(jnp.where(ids[..., None] == expert_ids, table, 0), axis=-1)
    dest_p = lookup(start, meta_p[0:2].astype(I32)) + meta_p[2:4].astype(I32)
    dest_s = lookup(start + cnt_p_i, meta_s[0:2].astype(I32)) + meta_s[2:4].astype(I32)
    xs, aidx = _scatter(jnp.concatenate([dest_p, dest_s], axis=1), xn_p, xn_s)
    ya = _experts(items, aidx, xs, w_exp_gate[l], w_exp_up[l], w_exp_down[l])
    y_p = _combine(gates_p, h_p, g_fin, ya, 0)
    y_s = _combine(gates_s, h_s, g_fin, ya, np_)

    kv5 = lambda a, b: a.reshape(b, -1, SWA_KV_HEADS, HEAD_DIM)[None]
    return (y_p.reshape(bp, sp, D_MODEL), y_s.reshape(bs, ts, D_MODEL),
            kv5(k_p.reshape(bp, sp, SWA_KV)[:, -WINDOW:], bp), kv5(v_p.reshape(bp, sp, SWA_KV)[:, -WINDOW:], bp),
            kv5(kcache_s, bs), kv5(vcache_s, bs),
            c_p3[:, -(CONV_W - 1):][None], c_s3[:, -(CONV_W - 1):][None],
            s_fin_p[None], s_fin_s[None])
```
